```python
import jax, jax.numpy as jnp
from jax import lax
import numpy as np

D_MODEL = 2048
BATCH = 8
SEQ = 4096
DEPTH = 1

CHUNK = 64
Q_BLOCK = 128
D_MIX = D_MODEL
FOX_WIDTH = D_MIX // 2
GLA_WIDTH = D_MIX - FOX_WIDTH
FOX_HEAD_DIM = 128
FOX_HEADS = FOX_WIDTH // FOX_HEAD_DIM
GLA_HEADS = 4
GLA_DV = GLA_WIDTH // GLA_HEADS
GLA_DK = GLA_DV // 2
GLA_KEY_WIDTH = GLA_HEADS * GLA_DK
GLA_GATE_RANK = 16
GLA_GATE_TEMP = 16.0
D_FF = 4 * D_MODEL
N_MOD = 6
EPS = 1e-6

IN_SIZES = (FOX_WIDTH, FOX_WIDTH, FOX_WIDTH, FOX_HEADS,
            GLA_KEY_WIDTH, GLA_KEY_WIDTH, GLA_WIDTH, GLA_GATE_RANK, GLA_WIDTH)
IN_OFFSETS = tuple(int(o) for o in np.cumsum(IN_SIZES)[:-1])
D_IN_PROJ = int(sum(IN_SIZES))

kernel_name = "hymba_fox_gla_sandwich_adaln_block"


def rmsnorm(x, g):
    xf = x.astype(jnp.float32)
    y = xf * lax.rsqrt(jnp.mean(xf * xf, axis=-1, keepdims=True) + EPS)
    return (y * g.astype(jnp.float32)).astype(x.dtype)


def head_rmsnorm(x, g):
    xf = x.astype(jnp.float32)
    y = xf * lax.rsqrt(jnp.mean(xf * xf, axis=-1, keepdims=True) + EPS)
    return (y * g.astype(jnp.float32)).astype(x.dtype)


def forgetting_attention(q, k, v, log_f):
    s = q.shape[1]
    scale = FOX_HEAD_DIM ** -0.5
    cum = jnp.transpose(jnp.cumsum(log_f, axis=1), (0, 2, 1))
    neg = jnp.finfo(jnp.float32).min
    outs = []
    for i in range(s // Q_BLOCK):
        q0, q1 = i * Q_BLOCK, (i + 1) * Q_BLOCK
        qb, kb, vb = q[:, q0:q1], k[:, :q1], v[:, :q1]
        logits = jnp.einsum('bqhd,bkhd->bhqk', qb, kb,
                            preferred_element_type=jnp.float32) * scale
        bias = cum[:, :, q0:q1, None] - cum[:, :, None, :q1]
        q_pos = jnp.arange(q0, q1)[:, None]
        k_pos = jnp.arange(q1)[None, :]
        logits = jnp.where(q_pos >= k_pos, logits + bias, neg)
        p = jax.nn.softmax(logits, axis=-1)
        outs.append(jnp.einsum('bhqk,bkhd->bqhd', p.astype(vb.dtype), vb))
    return jnp.concatenate(outs, axis=1)


def gla_chunk_causal(q, k, v, log_a):
    b, s, h, dk = q.shape
    dv = v.shape[-1]
    nc = s // CHUNK
    qc = q.reshape(b, nc, CHUNK, h, dk).astype(jnp.float32) * (dk ** -0.5)
    kc = k.reshape(b, nc, CHUNK, h, dk).astype(jnp.float32)
    vc = v.reshape(b, nc, CHUNK, h, dv).astype(jnp.float32)
    la = log_a.reshape(b, nc, CHUNK, h, dk)
    cum = jnp.cumsum(la, axis=2)
    total = cum[:, :, -1]
    k_dec = kc * jnp.exp(total[:, :, None] - cum)
    u = jnp.einsum('bnchk,bnchv->nbhkv', k_dec, vc)
    decay = jnp.transpose(jnp.exp(total), (1, 0, 2, 3))

    def step(state, inp):
        d_n, u_n = inp
        state = d_n[..., None] * state + u_n
        return state, state

    init = jnp.zeros((b, h, dk, dv), jnp.float32)
    _, states = lax.scan(step, init, (decay, u))
    o = jnp.einsum('bnchk,nbhkv->bnchv', qc, states)
    return o.reshape(b, s, h, dv).astype(v.dtype)


def token_mixer(h, w_in, b_fgate, w_gla_a2, b_gla_a2, g_fox_out, g_gla_out, w_out):
    b, s, _ = h.shape
    proj = h @ w_in
    fq, fk, fv, ff, gq, gk, gv, ga, gr = jnp.split(proj, IN_OFFSETS, axis=-1)

    log_f = jax.nn.log_sigmoid((ff + b_fgate).astype(jnp.float32))
    fox = forgetting_attention(fq.reshape(b, s, FOX_HEADS, FOX_HEAD_DIM),
                               fk.reshape(b, s, FOX_HEADS, FOX_HEAD_DIM),
                               fv.reshape(b, s, FOX_HEADS, FOX_HEAD_DIM), log_f)
    fox = head_rmsnorm(fox, g_fox_out).reshape(b, s, FOX_WIDTH)

    log_a = jax.nn.log_sigmoid((ga @ w_gla_a2 + b_gla_a2).astype(jnp.float32)) / GLA_GATE_TEMP
    gla = gla_chunk_causal(gq.reshape(b, s, GLA_HEADS, GLA_DK),
                           gk.reshape(b, s, GLA_HEADS, GLA_DK),
                           gv.reshape(b, s, GLA_HEADS, GLA_DV),
                           log_a.reshape(b, s, GLA_HEADS, GLA_DK))
    gla = head_rmsnorm(gla, g_gla_out).reshape(b, s, GLA_WIDTH) * jax.nn.silu(gr)

    return jnp.concatenate([fox, gla], axis=-1) @ w_out


def squared_relu_mlp(h, w_mlp_in, w_mlp_out):
    return jnp.square(jax.nn.relu(h @ w_mlp_in)) @ w_mlp_out


def _fwd_setup_inputs(seed: int = 0) -> dict:
    key = jax.random.key(seed)
    ks = jax.random.split(key, 20)
    nrm = lambda k, shape, s: jax.random.normal(k, shape, jnp.float32) * s
    gain = lambda k, shape: 1.0 + nrm(k, shape, 0.02)
    L = DEPTH
    return {
        "x": nrm(ks[0], (BATCH, SEQ, D_MODEL), 1.0),
        "c": nrm(ks[1], (BATCH, D_MODEL), 1.0),
        "w_ada": nrm(ks[2], (L, D_MODEL, N_MOD * D_MODEL), 0.5 * D_MODEL ** -0.5),
        "b_ada": nrm(ks[3], (L, N_MOD * D_MODEL), 0.02),
        "g_pre_mix": gain(ks[4], (L, D_MODEL)),
        "g_post_mix": gain(ks[5], (L, D_MODEL)),
        "w_in": nrm(ks[6], (L, D_MODEL, D_IN_PROJ), D_MODEL ** -0.5),
        "b_fgate": 2.0 + nrm(ks[7], (L, FOX_HEADS), 0.1),
        "w_gla_a2": nrm(ks[8], (L, GLA_GATE_RANK, GLA_KEY_WIDTH), GLA_GATE_RANK ** -0.5),
        "b_gla_a2": nrm(ks[9], (L, GLA_KEY_WIDTH), 0.02),
        "g_fox_out": gain(ks[10], (L, FOX_HEADS, FOX_HEAD_DIM)),
        "g_gla_out": gain(ks[11], (L, GLA_HEADS, GLA_DV)),
        "w_out": nrm(ks[12], (L, D_MIX, D_MODEL), D_MIX ** -0.5),
        "g_pre_mlp": gain(ks[13], (L, D_MODEL)),
        "g_post_mlp": gain(ks[14], (L, D_MODEL)),
        "w_mlp_in": nrm(ks[15], (L, D_MODEL, D_FF), D_MODEL ** -0.5),
        "w_mlp_out": nrm(ks[16], (L, D_FF, D_MODEL), D_FF ** -0.5),
    }


def _fwd_reference(x, c, w_ada, b_ada, g_pre_mix, g_post_mix, w_in, b_fgate, w_gla_a2,
              b_gla_a2, g_fox_out, g_gla_out, w_out, g_pre_mlp, g_post_mlp,
              w_mlp_in, w_mlp_out):
    c_act = jax.nn.silu(c)
    for i in range(DEPTH):
        mod = (c_act @ w_ada[i] + b_ada[i])[:, None, :]
        shift_m, scale_m, gate_m, shift_f, scale_f, gate_f = jnp.split(mod, N_MOD, axis=-1)

        h = rmsnorm(x, g_pre_mix[i]) * (1.0 + scale_m) + shift_m
        y = token_mixer(h, w_in[i], b_fgate[i], w_gla_a2[i], b_gla_a2[i],
                        g_fox_out[i], g_gla_out[i], w_out[i])
        x = x + gate_m * rmsnorm(y, g_post_mix[i])

        h = rmsnorm(x, g_pre_mlp[i]) * (1.0 + scale_f) + shift_f
        y = squared_relu_mlp(h, w_mlp_in[i], w_mlp_out[i])
        x = x + gate_f * rmsnorm(y, g_post_mlp[i])
    return x


import jax as _jax
import jax.numpy as _jnp

TWIN_FORMAT = 'train_step'
FWD_PARAMS = ['x', 'c', 'w_ada', 'b_ada', 'g_pre_mix', 'g_post_mix', 'w_in', 'b_fgate', 'w_gla_a2', 'b_gla_a2', 'g_fox_out', 'g_gla_out', 'w_out', 'g_pre_mlp', 'g_post_mlp', 'w_mlp_in', 'w_mlp_out']
TWIN_WEIGHTS = ['w_ada', 'b_ada', 'g_pre_mix', 'g_post_mix', 'w_in', 'b_fgate', 'w_gla_a2', 'b_gla_a2', 'g_fox_out', 'g_gla_out', 'w_out', 'g_pre_mlp', 'g_post_mlp', 'w_mlp_in', 'w_mlp_out']
TWIN_DIFF_INPUT = 'x'
TWIN_INPUTS = ['x', 'c', 'w_ada', 'b_ada', 'g_pre_mix', 'g_post_mix', 'w_in', 'b_fgate', 'w_gla_a2', 'b_gla_a2', 'g_fox_out', 'g_gla_out', 'w_out', 'g_pre_mlp', 'g_post_mlp', 'w_mlp_in', 'w_mlp_out', 'loss_target', 'm_w_ada', 'm_b_ada', 'm_g_pre_mix', 'm_g_post_mix', 'm_w_in', 'm_b_fgate', 'm_w_gla_a2', 'm_b_gla_a2', 'm_g_fox_out', 'm_g_gla_out', 'm_w_out', 'm_g_pre_mlp', 'm_g_post_mlp', 'm_w_mlp_in', 'm_w_mlp_out', 'v_w_ada', 'v_b_ada', 'v_g_pre_mix', 'v_g_post_mix', 'v_w_in', 'v_b_fgate', 'v_w_gla_a2', 'v_b_gla_a2', 'v_g_fox_out', 'v_g_gla_out', 'v_w_out', 'v_g_pre_mlp', 'v_g_post_mlp', 'v_w_mlp_in', 'v_w_mlp_out']
TWIN_OUTPUTS = ['loss', 'grad_x', 'grad_w_ada', 'grad_b_ada', 'grad_g_pre_mix', 'grad_g_post_mix', 'grad_w_in', 'grad_b_fgate', 'grad_w_gla_a2', 'grad_b_gla_a2', 'grad_g_fox_out', 'grad_g_gla_out', 'grad_w_out', 'grad_g_pre_mlp', 'grad_g_post_mlp', 'grad_w_mlp_in', 'grad_w_mlp_out', 'delta_w_ada', 'delta_b_ada', 'delta_g_pre_mix', 'delta_g_post_mix', 'delta_w_in', 'delta_b_fgate', 'delta_w_gla_a2', 'delta_b_gla_a2', 'delta_g_fox_out', 'delta_g_gla_out', 'delta_w_out', 'delta_g_pre_mlp', 'delta_g_post_mlp', 'delta_w_mlp_in', 'delta_w_mlp_out', 'new_m_w_ada', 'new_m_b_ada', 'new_m_g_pre_mix', 'new_m_g_post_mix', 'new_m_w_in', 'new_m_b_fgate', 'new_m_w_gla_a2', 'new_m_b_gla_a2', 'new_m_g_fox_out', 'new_m_g_gla_out', 'new_m_w_out', 'new_m_g_pre_mlp', 'new_m_g_post_mlp', 'new_m_w_mlp_in', 'new_m_w_mlp_out', 'new_v_w_ada', 'new_v_b_ada', 'new_v_g_pre_mix', 'new_v_g_post_mix', 'new_v_w_in', 'new_v_b_fgate', 'new_v_w_gla_a2', 'new_v_b_gla_a2', 'new_v_g_fox_out', 'new_v_g_gla_out', 'new_v_w_out', 'new_v_g_pre_mlp', 'new_v_g_post_mlp', 'new_v_w_mlp_in', 'new_v_w_mlp_out']
TWIN_LEAF_KINDS = {'loss': 'loss', 'grad_x': 'grad_x', 'grad_w_ada': 'grad_w', 'grad_b_ada': 'grad_w', 'grad_g_pre_mix': 'grad_w', 'grad_g_post_mix': 'grad_w', 'grad_w_in': 'grad_w', 'grad_b_fgate': 'grad_w', 'grad_w_gla_a2': 'grad_w', 'grad_b_gla_a2': 'grad_w', 'grad_g_fox_out': 'grad_w', 'grad_g_gla_out': 'grad_w', 'grad_w_out': 'grad_w', 'grad_g_pre_mlp': 'grad_w', 'grad_g_post_mlp': 'grad_w', 'grad_w_mlp_in': 'grad_w', 'grad_w_mlp_out': 'grad_w', 'delta_w_ada': 'delta_w', 'delta_b_ada': 'delta_w', 'delta_g_pre_mix': 'delta_w', 'delta_g_post_mix': 'delta_w', 'delta_w_in': 'delta_w', 'delta_b_fgate': 'delta_w', 'delta_w_gla_a2': 'delta_w', 'delta_b_gla_a2': 'delta_w', 'delta_g_fox_out': 'delta_w', 'delta_g_gla_out': 'delta_w', 'delta_w_out': 'delta_w', 'delta_g_pre_mlp': 'delta_w', 'delta_g_post_mlp': 'delta_w', 'delta_w_mlp_in': 'delta_w', 'delta_w_mlp_out': 'delta_w', 'new_m_w_ada': 'new_m', 'new_m_b_ada': 'new_m', 'new_m_g_pre_mix': 'new_m', 'new_m_g_post_mix': 'new_m', 'new_m_w_in': 'new_m', 'new_m_b_fgate': 'new_m', 'new_m_w_gla_a2': 'new_m', 'new_m_b_gla_a2': 'new_m', 'new_m_g_fox_out': 'new_m', 'new_m_g_gla_out': 'new_m', 'new_m_w_out': 'new_m', 'new_m_g_pre_mlp': 'new_m', 'new_m_g_post_mlp': 'new_m', 'new_m_w_mlp_in': 'new_m', 'new_m_w_mlp_out': 'new_m', 'new_v_w_ada': 'new_v', 'new_v_b_ada': 'new_v', 'new_v_g_pre_mix': 'new_v', 'new_v_g_post_mix': 'new_v', 'new_v_w_in': 'new_v', 'new_v_b_fgate': 'new_v', 'new_v_w_gla_a2': 'new_v', 'new_v_b_gla_a2': 'new_v', 'new_v_g_fox_out': 'new_v', 'new_v_g_gla_out': 'new_v', 'new_v_w_out': 'new_v', 'new_v_g_pre_mlp': 'new_v', 'new_v_g_post_mlp': 'new_v', 'new_v_w_mlp_in': 'new_v', 'new_v_w_mlp_out': 'new_v'}


def _forward(args):
    return _fwd_reference(*[args[k] for k in FWD_PARAMS])


def _output_shape():
    def fwd():
        inp = _fwd_setup_inputs(0)
        return _fwd_reference(*[inp[k] for k in FWD_PARAMS])
    out = _jax.eval_shape(fwd)
    return out.shape, out.dtype

N_MICROBATCH = 1
ADAM_LR = 0.001
ADAM_B1 = 0.9
ADAM_B2 = 0.999
ADAM_EPS = 1e-08
ADAM_WD = 0.01
ADAM_STEP = 10
PER_EXAMPLE_BATCH_AXIS = {'x': 0, 'c': 0, 'loss_target': 0}
SHARED_INPUTS = []
_WEIGHT_DTYPES = {'w_ada': _jnp.float32, 'b_ada': _jnp.float32, 'g_pre_mix': _jnp.float32, 'g_post_mix': _jnp.float32, 'w_in': _jnp.float32, 'b_fgate': _jnp.float32, 'w_gla_a2': _jnp.float32, 'b_gla_a2': _jnp.float32, 'g_fox_out': _jnp.float32, 'g_gla_out': _jnp.float32, 'w_out': _jnp.float32, 'g_pre_mlp': _jnp.float32, 'g_post_mlp': _jnp.float32, 'w_mlp_in': _jnp.float32, 'w_mlp_out': _jnp.float32}
MOMENT_SCALE = {'w_ada': 6.672256e-01, 'b_ada': 1.401821e+00, 'g_pre_mix': 6.456156e-02, 'g_post_mix': 1.659762e+00, 'w_in': 1.057247e-01, 'b_fgate': 4.339997e-01, 'w_gla_a2': 8.806372e-03, 'b_gla_a2': 2.435132e-02, 'g_fox_out': 2.527232e-01, 'g_gla_out': 3.572290e-02, 'w_out': 1.907890e-01, 'g_pre_mlp': 6.141826e-02, 'g_post_mlp': 1.670116e+00, 'w_mlp_in': 4.261585e-02, 'w_mlp_out': 1.710698e-01}


def _to_microbatches(a, axis):
    t = _jnp.moveaxis(a, axis, 0)
    t = t.reshape((N_MICROBATCH, t.shape[0] // N_MICROBATCH) + t.shape[1:])
    return _jnp.moveaxis(t, 1, axis + 1)


def setup_inputs(seed: int = 0) -> dict:
    inp = _fwd_setup_inputs(seed)
    key = _jax.random.fold_in(_jax.random.key(seed), 7919)
    shape, _ = _output_shape()
    out = dict(inp)
    out["loss_target"] = _jax.random.normal(_jax.random.fold_in(key, 0), shape, _jnp.float32)
    for i, name in enumerate(TWIN_WEIGHTS):
        w = inp[name].astype(_jnp.float32)
        if MOMENT_SCALE is None:
            s = _jnp.sqrt(_jnp.mean(_jnp.square(w)) + 1e-30)
        else:
            s = MOMENT_SCALE[name]
        km, kv = _jax.random.split(_jax.random.fold_in(key, i + 1))
        out[name] = w
        out["m_" + name] = s * _jax.random.normal(km, w.shape, _jnp.float32)
        out["v_" + name] = (s * s) * _jax.random.uniform(kv, w.shape, _jnp.float32, 0.5, 1.5)
    if N_MICROBATCH > 1:
        for name, axis in PER_EXAMPLE_BATCH_AXIS.items():
            out[name] = _to_microbatches(out[name], axis)
    return {'x': out['x'], 'c': out['c'], 'w_ada': out['w_ada'], 'b_ada': out['b_ada'], 'g_pre_mix': out['g_pre_mix'], 'g_post_mix': out['g_post_mix'], 'w_in': out['w_in'], 'b_fgate': out['b_fgate'], 'w_gla_a2': out['w_gla_a2'], 'b_gla_a2': out['b_gla_a2'], 'g_fox_out': out['g_fox_out'], 'g_gla_out': out['g_gla_out'], 'w_out': out['w_out'], 'g_pre_mlp': out['g_pre_mlp'], 'g_post_mlp': out['g_post_mlp'], 'w_mlp_in': out['w_mlp_in'], 'w_mlp_out': out['w_mlp_out'], 'loss_target': out['loss_target'], 'm_w_ada': out['m_w_ada'], 'm_b_ada': out['m_b_ada'], 'm_g_pre_mix': out['m_g_pre_mix'], 'm_g_post_mix': out['m_g_post_mix'], 'm_w_in': out['m_w_in'], 'm_b_fgate': out['m_b_fgate'], 'm_w_gla_a2': out['m_w_gla_a2'], 'm_b_gla_a2': out['m_b_gla_a2'], 'm_g_fox_out': out['m_g_fox_out'], 'm_g_gla_out': out['m_g_gla_out'], 'm_w_out': out['m_w_out'], 'm_g_pre_mlp': out['m_g_pre_mlp'], 'm_g_post_mlp': out['m_g_post_mlp'], 'm_w_mlp_in': out['m_w_mlp_in'], 'm_w_mlp_out': out['m_w_mlp_out'], 'v_w_ada': out['v_w_ada'], 'v_b_ada': out['v_b_ada'], 'v_g_pre_mix': out['v_g_pre_mix'], 'v_g_post_mix': out['v_g_post_mix'], 'v_w_in': out['v_w_in'], 'v_b_fgate': out['v_b_fgate'], 'v_w_gla_a2': out['v_w_gla_a2'], 'v_b_gla_a2': out['v_b_gla_a2'], 'v_g_fox_out': out['v_g_fox_out'], 'v_g_gla_out': out['v_g_gla_out'], 'v_w_out': out['v_w_out'], 'v_g_pre_mlp': out['v_g_pre_mlp'], 'v_g_post_mlp': out['v_g_post_mlp'], 'v_w_mlp_in': out['v_w_mlp_in'], 'v_w_mlp_out': out['v_w_mlp_out']}


def _loss(weights, diff, rest, loss_target):
    with _jax.named_scope("forward"):
        args = {**rest, TWIN_DIFF_INPUT: diff, **{k: w.astype(_WEIGHT_DTYPES[k]) for k, w in weights.items()}}
        y = _forward(args)
    with _jax.named_scope("loss_head"):
        err = _jnp.square(y.astype(_jnp.float32) - loss_target)
        return 0.5 * _jnp.sum(_jnp.mean(err, axis=-1)) if err.ndim else 0.5 * err


def _adamw(w, g, m, v):
    m = ADAM_B1 * m + (1.0 - ADAM_B1) * g
    v = ADAM_B2 * v + (1.0 - ADAM_B2) * _jnp.square(g)
    m_hat = m / (1.0 - ADAM_B1 ** ADAM_STEP)
    v_hat = v / (1.0 - ADAM_B2 ** ADAM_STEP)
    delta = -ADAM_LR * (m_hat / (_jnp.sqrt(v_hat) + ADAM_EPS) + ADAM_WD * w)
    return delta, m, v


def reference(x, c, w_ada, b_ada, g_pre_mix, g_post_mix, w_in, b_fgate, w_gla_a2, b_gla_a2, g_fox_out, g_gla_out, w_out, g_pre_mlp, g_post_mlp, w_mlp_in, w_mlp_out, loss_target, m_w_ada, m_b_ada, m_g_pre_mix, m_g_post_mix, m_w_in, m_b_fgate, m_w_gla_a2, m_b_gla_a2, m_g_fox_out, m_g_gla_out, m_w_out, m_g_pre_mlp, m_g_post_mlp, m_w_mlp_in, m_w_mlp_out, v_w_ada, v_b_ada, v_g_pre_mix, v_g_post_mix, v_w_in, v_b_fgate, v_w_gla_a2, v_b_gla_a2, v_g_fox_out, v_g_gla_out, v_w_out, v_g_pre_mlp, v_g_post_mlp, v_w_mlp_in, v_w_mlp_out):
    given = dict(x=x, c=c, w_ada=w_ada, b_ada=b_ada, g_pre_mix=g_pre_mix, g_post_mix=g_post_mix, w_in=w_in, b_fgate=b_fgate, w_gla_a2=w_gla_a2, b_gla_a2=b_gla_a2, g_fox_out=g_fox_out, g_gla_out=g_gla_out, w_out=w_out, g_pre_mlp=g_pre_mlp, g_post_mlp=g_post_mlp, w_mlp_in=w_mlp_in, w_mlp_out=w_mlp_out, loss_target=loss_target, m_w_ada=m_w_ada, m_b_ada=m_b_ada, m_g_pre_mix=m_g_pre_mix, m_g_post_mix=m_g_post_mix, m_w_in=m_w_in, m_b_fgate=m_b_fgate, m_w_gla_a2=m_w_gla_a2, m_b_gla_a2=m_b_gla_a2, m_g_fox_out=m_g_fox_out, m_g_gla_out=m_g_gla_out, m_w_out=m_w_out, m_g_pre_mlp=m_g_pre_mlp, m_g_post_mlp=m_g_post_mlp, m_w_mlp_in=m_w_mlp_in, m_w_mlp_out=m_w_mlp_out, v_w_ada=v_w_ada, v_b_ada=v_b_ada, v_g_pre_mix=v_g_pre_mix, v_g_post_mix=v_g_post_mix, v_w_in=v_w_in, v_b_fgate=v_b_fgate, v_w_gla_a2=v_w_gla_a2, v_b_gla_a2=v_b_gla_a2, v_g_fox_out=v_g_fox_out, v_g_gla_out=v_g_gla_out, v_w_out=v_w_out, v_g_pre_mlp=v_g_pre_mlp, v_g_post_mlp=v_g_post_mlp, v_w_mlp_in=v_w_mlp_in, v_w_mlp_out=v_w_mlp_out)
    weights = {n: given[n] for n in TWIN_WEIGHTS}
    shared = {n: given[n] for n in SHARED_INPUTS}
    per_example = {n: given[n] for n in ['x', 'c']}
    grad_fn = _jax.value_and_grad(_loss, argnums=(0, 1))

    def one_microbatch(ex, loss_target):
        ex = dict(ex)
        diff = ex.pop(TWIN_DIFF_INPUT)
        return grad_fn(weights, diff, {**shared, **ex}, loss_target)

    if N_MICROBATCH == 1:
        loss, (grad_w, grad_x) = one_microbatch(per_example, given["loss_target"])
    else:
        def body(carry, xs):
            loss_sum, grad_sum = carry
            l_k, (gw_k, gx_k) = one_microbatch(xs[0], xs[1])
            with _jax.named_scope("update"):
                return (loss_sum + l_k, _jax.tree.map(_jnp.add, grad_sum, gw_k)), gx_k

        init = (_jnp.zeros((), _jnp.float32), _jax.tree.map(_jnp.zeros_like, weights))
        (loss, grad_w), grad_x = _jax.lax.scan(body, init, (per_example, given["loss_target"]))
    with _jax.named_scope("update"):
        delta_w, new_m, new_v = {}, {}, {}
        for n in TWIN_WEIGHTS:
            delta_w[n], new_m[n], new_v[n] = _adamw(weights[n], grad_w[n], given["m_" + n], given["v_" + n])
    return (loss, grad_x, *[grad_w[n] for n in TWIN_WEIGHTS], *[delta_w[n] for n in TWIN_WEIGHTS],
            *[new_m[n] for n in TWIN_WEIGHTS], *[new_v[n] for n in TWIN_WEIGHTS])
```

```python
import functools

import numpy as np
import jax
import jax.numpy as jnp
from jax import lax
from jax.experimental import pallas as pl
from jax.experimental.pallas import tpu as pltpu

F32 = jnp.float32
BF16 = jnp.bfloat16
MESH = pl.DeviceIdType.MESH
N_DEV = 8

D_MODEL = 2048
FOX_HEADS = 8
FOX_HEAD_DIM = 128
GLA_HEADS = 4
GLA_DK = 128
GLA_DV = 256
GLA_RANK = 16
GLA_TEMP = 16.0
CHUNK = 64
D_FF = 8192
W_MAIN = 6144
W_SMALL = 128
EPS = 1e-6
NEG = float(np.finfo(np.float32).min)

ADAM_LR = 0.001
ADAM_B1 = 0.9
ADAM_B2 = 0.999
ADAM_EPS = 1e-08
ADAM_WD = 0.01
ADAM_STEP = 10

ROW_T = 256
FOX_T = 512
GLA_R = 512
CUM_T = 256
VMEM_LIMIT = 56 * 1024 * 1024


def _call(body, **kw):
    return pl.pallas_call(body, **kw)


def _params(sem=None):
    return pltpu.CompilerParams(dimension_semantics=sem, vmem_limit_bytes=VMEM_LIMIT)


def _my_pos():
    return lax.axis_index("x"), lax.axis_index("y"), lax.axis_index("c")


def _my_rank():
    x, y, c = _my_pos()
    return 4 * x + 2 * y + c


def _all_gather(name, arrays):
    n = len(arrays)

    def body(*refs):
        ins = refs[:n]
        outs = refs[n:2 * n]
        send_sems, recv_sems, local_sems = refs[2 * n:]
        x, y, c = _my_pos()
        me, sibling = (x, y, c), (x, y, 1 - c)
        chips = [(1 - x, y), (x, 1 - y), (1 - x, 1 - y)]

        def slot(a, px, py, pc):
            return outs[a].at[4 * px + 2 * py + pc]

        def copy(a, k, block, to, src=None):
            return pltpu.make_async_remote_copy(
                src_ref=slot(a, *block) if src is None else src, dst_ref=slot(a, *block),
                send_sem=send_sems.at[a, k], recv_sem=recv_sems.at[a, k],
                device_id=to, device_id_type=MESH)

        started = []
        for a in range(n):
            mine = pltpu.make_async_copy(ins[a], slot(a, *me), local_sems.at[a])
            mine.start()
            started.append(mine)
        first = []
        for a in range(n):
            first.append(copy(a, 0, me, sibling, src=ins[a]))
            first += [copy(a, 1 + j, me, (*chip, c), src=ins[a]) for j, chip in enumerate(chips)]
        for cp in first:
            cp.start()
        passed = []
        for j, chip in enumerate(chips):
            for a in range(n):
                copy(a, 1 + j, (*chip, c), me).wait_recv()
                fwd = copy(a, 4 + j, (*chip, c), sibling)
                fwd.start()
                passed.append(fwd)
        for a in range(n):
            copy(a, 0, sibling, me).wait_recv()
            for j, chip in enumerate(chips):
                copy(a, 4 + j, (*chip, 1 - c), me).wait_recv()
        for cp in first + passed:
            cp.wait_send()
        for mine in started:
            mine.wait()

    hbm = pl.BlockSpec(memory_space=pltpu.HBM)
    return _call(
        body, name=name,
        out_shape=[jax.ShapeDtypeStruct((N_DEV,) + a.shape, a.dtype) for a in arrays],
        in_specs=[hbm] * n, out_specs=[hbm] * n,
        scratch_shapes=[pltpu.SemaphoreType.DMA((n, 7)), pltpu.SemaphoreType.DMA((n, 7)),
                        pltpu.SemaphoreType.DMA((n,))],
    )(*arrays)


def _all_to_all(name, arrays):
    n = len(arrays)

    def body(*refs):
        ins = refs[:n]
        outs = refs[n:2 * n]
        send_sems, recv_sems, local_sems = refs[2 * n:]
        x, y, c = _my_pos()
        me = 4 * x + 2 * y + c
        flips = [(kx, ky, kc) for kx in (0, 1) for ky in (0, 1) for kc in (0, 1)][1:]

        def peer(k):
            kx, ky, kc = flips[k]
            return (1 - x if kx else x), (1 - y if ky else y), (1 - c if kc else c)

        local = []
        for a in range(n):
            cp = pltpu.make_async_copy(ins[a].at[me], outs[a].at[me], local_sems.at[a])
            cp.start()
            local.append(cp)
        sends = []
        for k in range(7):
            px, py, pc = peer(k)
            pr = 4 * px + 2 * py + pc
            for a in range(n):
                cp = pltpu.make_async_remote_copy(
                    src_ref=ins[a].at[pr], dst_ref=outs[a].at[me],
                    send_sem=send_sems.at[a, k], recv_sem=recv_sems.at[a, k],
                    device_id=(px, py, pc), device_id_type=MESH)
                cp.start()
                sends.append(cp)
        for k in range(7):
            px, py, pc = peer(k)
            pr = 4 * px + 2 * py + pc
            for a in range(n):
                pltpu.make_async_remote_copy(
                    src_ref=ins[a].at[pr], dst_ref=outs[a].at[pr],
                    send_sem=send_sems.at[a, k], recv_sem=recv_sems.at[a, k],
                    device_id=(px, py, pc), device_id_type=MESH).wait_recv()
        for cp in sends:
            cp.wait_send()
        for cp in local:
            cp.wait()

    hbm = pl.BlockSpec(memory_space=pltpu.HBM)
    return _call(
        body, name=name,
        out_shape=[jax.ShapeDtypeStruct(a.shape, a.dtype) for a in arrays],
        in_specs=[hbm] * n, out_specs=[hbm] * n,
        scratch_shapes=[pltpu.SemaphoreType.DMA((n, 7)), pltpu.SemaphoreType.DMA((n, 7)),
                        pltpu.SemaphoreType.DMA((n,))],
    )(*arrays)


NN = ((1,), (0,))
NT = ((1,), (1,))
TN = ((0,), (0,))


def _matmul(name, a, b, *, contract, grid, a_spec, b_spec, out_specs, out_shapes, acc_shape,
            extra=(), extra_specs=(), epilogue=None):
    nk = grid[2]
    n_extra = len(extra)
    n_out = len(out_shapes)

    def body(*refs):
        a_ref, b_ref = refs[0], refs[1]
        extra_refs = refs[2:2 + n_extra]
        out_refs = refs[2 + n_extra:2 + n_extra + n_out]
        acc_ref = refs[-1]
        k = pl.program_id(2)
        prod = lax.dot_general(a_ref[...].astype(BF16), b_ref[...].astype(BF16), (contract, ((), ())),
                               preferred_element_type=F32)

        @pl.when(k == 0)
        def _():
            acc_ref[...] = prod

        @pl.when(k > 0)
        def _():
            acc_ref[...] += prod

        @pl.when(k == nk - 1)
        def _():
            acc = acc_ref[...]
            if epilogue is None:
                res = (acc,)
            else:
                res = epilogue(acc, *[r[...] for r in extra_refs])
            for o_ref, val in zip(out_refs, res):
                o_ref[...] = val.astype(o_ref.dtype)

    outs = _call(
        body, name=name, grid=grid,
        in_specs=[a_spec, b_spec, *extra_specs], out_specs=list(out_specs), out_shape=list(out_shapes),
        scratch_shapes=[pltpu.VMEM(acc_shape, F32)],
        compiler_params=_params(("parallel", "parallel", "arbitrary")),
    )(a, b, *extra)
    return outs


def _tile(n, t):
    t = min(n, t)
    assert n % t == 0, (n, t)
    return t


def _mm_plain(name, a, b, contract, out_dtype, tm=1024, tn=1024, tk=512, extra=(), epilogue=None,
              n_out=1, out_dtypes=None):
    if contract == NN:
        (m, kd), (_, n) = a.shape, b.shape
    elif contract == NT:
        (m, kd), (n, _) = a.shape, b.shape
    else:
        (kd, m), (_, n) = a.shape, b.shape
    tm, tn, tk = _tile(m, tm), _tile(n, tn), _tile(kd, tk)
    if contract == NN:
        a_spec = pl.BlockSpec((tm, tk), lambda i, j, k: (i, k))
        b_spec = pl.BlockSpec((tk, tn), lambda i, j, k: (k, j))
    elif contract == NT:
        a_spec = pl.BlockSpec((tm, tk), lambda i, j, k: (i, k))
        b_spec = pl.BlockSpec((tn, tk), lambda i, j, k: (j, k))
    else:
        a_spec = pl.BlockSpec((tk, tm), lambda i, j, k: (k, i))
        b_spec = pl.BlockSpec((tk, tn), lambda i, j, k: (k, j))
    o_spec = pl.BlockSpec((tm, tn), lambda i, j, k: (i, j))
    out_dtypes = out_dtypes or [out_dtype] * n_out
    outs = _matmul(
        name, a, b, contract=contract, grid=(m // tm, n // tn, kd // tk), a_spec=a_spec, b_spec=b_spec,
        out_specs=[o_spec] * len(out_dtypes), out_shapes=[jax.ShapeDtypeStruct((m, n), dt) for dt in out_dtypes],
        acc_shape=(tm, tn), extra=extra, extra_specs=[o_spec] * len(extra), epilogue=epilogue)
    return outs[0] if len(out_dtypes) == 1 else outs


def _rows_call(name, body, row_in, vec_in, row_out, vec_out, s):
    t = _tile(s, ROW_T)
    in_specs = []
    args = []
    for arr, width, cb in row_in:
        in_specs.append(pl.BlockSpec((t, width), functools.partial(lambda i, cb: (i, cb), cb=cb)))
        args.append(arr)
    for v in vec_in:
        in_specs.append(pl.BlockSpec(v.shape, lambda i: (0, 0)))
        args.append(v)
    out_specs = []
    out_shapes = []
    for width, dt in row_out:
        out_specs.append(pl.BlockSpec((t, width), lambda i: (i, 0)))
        out_shapes.append(jax.ShapeDtypeStruct((s, width), dt))
    for width in vec_out:
        out_specs.append(pl.BlockSpec((1, width), lambda i: (0, 0)))
        out_shapes.append(jax.ShapeDtypeStruct((1, width), F32))
    return _call(body, name=name, grid=(s // t,), in_specs=in_specs, out_specs=out_specs, out_shape=out_shapes,
                 compiler_params=_params(("arbitrary",)))(*args)


def _acc_vec(ref, val):
    _acc_row(ref, jnp.sum(val, axis=0, keepdims=True))


def _acc_row(ref, part):
    @pl.when(pl.program_id(0) == 0)
    def _():
        ref[...] = part

    @pl.when(pl.program_id(0) > 0)
    def _():
        ref[...] += part


def _rms(v):
    return lax.rsqrt(jnp.mean(v * v, axis=-1, keepdims=True) + EPS)


def _norm_bwd(dxn, xn, r):
    return r * (dxn - xn * jnp.mean(dxn * xn, axis=-1, keepdims=True))


def _premix(x, g, scale, shift):
    s = x.shape[0]

    def body(x_ref, g_ref, sc_ref, sh_ref, h_ref):
        xv = x_ref[...]
        h_ref[...] = ((xv * _rms(xv) * g_ref[...]) * (1.0 + sc_ref[...]) + sh_ref[...]).astype(BF16)

    return _rows_call("premix", body, [(x, D_MODEL, 0)], [g, scale, shift], [(D_MODEL, BF16)], [], s)[0]


def _sigmoid(z):
    return 1.0 / (1.0 + jnp.exp(-z))


def _mix_fwd(o_fox, o_gla, pm, g_fox, g_gla):
    s = o_fox.shape[0]

    def body(of_ref, og_ref, gr_ref, gf_ref, gg_ref, mix_ref):
        for h in range(FOX_HEADS):
            sl = slice(h * FOX_HEAD_DIM, (h + 1) * FOX_HEAD_DIM)
            seg = of_ref[:, sl]
            mix_ref[:, sl] = (seg * _rms(seg) * gf_ref[:, sl]).astype(BF16)
        for h in range(GLA_HEADS):
            sl = slice(h * GLA_DV, (h + 1) * GLA_DV)
            seg = og_ref[:, sl]
            gr = gr_ref[:, sl].astype(F32)
            val = (seg * _rms(seg) * gg_ref[:, sl]) * (gr * _sigmoid(gr))
            mix_ref[:, pl.ds(FOX_HEADS * FOX_HEAD_DIM + h * GLA_DV, GLA_DV)] = val.astype(BF16)

    return _rows_call("mix_fwd", body, [(o_fox, 1024, 0), (o_gla, 1024, 0), (pm, 1024, 5)], [g_fox, g_gla],
                      [(D_MODEL, BF16)], [], s)[0]


def _mix_bwd(dmix, o_fox, o_gla, pm, g_fox, g_gla):
    s = o_fox.shape[0]

    def body(dm_ref, of_ref, og_ref, gr_ref, gf_ref, gg_ref, dof_ref, dog_ref, dgr_ref, dgf_ref, dgg_ref):
        dgf = []
        for h in range(FOX_HEADS):
            sl = slice(h * FOX_HEAD_DIM, (h + 1) * FOX_HEAD_DIM)
            seg = of_ref[:, sl]
            r = _rms(seg)
            segn = seg * r
            dout = dm_ref[:, sl]
            dgf.append(jnp.sum(dout * segn, axis=0, keepdims=True))
            dof_ref[:, sl] = _norm_bwd(dout * gf_ref[:, sl], segn, r).astype(BF16)
        dgg = []
        for h in range(GLA_HEADS):
            sl = slice(h * GLA_DV, (h + 1) * GLA_DV)
            seg = og_ref[:, sl]
            r = _rms(seg)
            segn = seg * r
            gl = segn * gg_ref[:, sl]
            gr = gr_ref[:, sl].astype(F32)
            sig = _sigmoid(gr)
            dout = dm_ref[:, pl.ds(FOX_HEADS * FOX_HEAD_DIM + h * GLA_DV, GLA_DV)]
            dgr_ref[:, sl] = (dout * gl * (sig * (1.0 + gr * (1.0 - sig)))).astype(BF16)
            dgl = dout * (gr * sig)
            dgg.append(jnp.sum(dgl * segn, axis=0, keepdims=True))
            dog_ref[:, sl] = _norm_bwd(dgl * gg_ref[:, sl], segn, r).astype(BF16)
        _acc_row(dgf_ref, jnp.concatenate(dgf, axis=1))
        _acc_row(dgg_ref, jnp.concatenate(dgg, axis=1))

    return _rows_call("mix_bwd", body, [(dmix, D_MODEL, 0), (o_fox, 1024, 0), (o_gla, 1024, 0), (pm, 1024, 5)],
                      [g_fox, g_gla], [(1024, BF16), (1024, BF16), (1024, BF16)], [1024, 1024], s)


def _postmix_premlp(x, y, gate_m, g_post_mix, g_pre_mlp, scale_f, shift_f):
    s = x.shape[0]

    def body(x_ref, y_ref, gm_ref, gpm_ref, gpl_ref, sc_ref, sh_ref, x1_ref, h2_ref):
        yv = y_ref[...]
        x1 = x_ref[...] + gm_ref[...] * (yv * _rms(yv) * gpm_ref[...])
        x1_ref[...] = x1
        h2_ref[...] = ((x1 * _rms(x1) * gpl_ref[...]) * (1.0 + sc_ref[...]) + sh_ref[...]).astype(BF16)

    return _rows_call("postmix_premlp", body, [(x, D_MODEL, 0), (y, D_MODEL, 0)],
                      [gate_m, g_post_mix, g_pre_mlp, scale_f, shift_f], [(D_MODEL, F32), (D_MODEL, BF16)], [], s)


def _loss_postmlp_bwd(x1, y2, target, gate_f, g_post_mlp):
    s = x1.shape[0]

    def body(x1_ref, y2_ref, t_ref, gf_ref, g_ref, dx2_ref, dy2_ref, loss_ref, dgate_ref, dg_ref):
        yv = y2_ref[...]
        r = _rms(yv)
        yn = yv * r
        o = yn * g_ref[...]
        e = (x1_ref[...] + gf_ref[...] * o) - t_ref[...]
        part = 0.5 * jnp.sum(jnp.mean(e * e, axis=-1, keepdims=True), axis=0, keepdims=True)
        _acc_vec(loss_ref, jnp.broadcast_to(part, (1, 128)))
        dx2 = e * (1.0 / D_MODEL)
        dx2_ref[...] = dx2
        _acc_vec(dgate_ref, dx2 * o)
        do = dx2 * gf_ref[...]
        _acc_vec(dg_ref, do * yn)
        dy2_ref[...] = _norm_bwd(do * g_ref[...], yn, r).astype(BF16)

    return _rows_call("loss_postmlp_bwd", body, [(x1, D_MODEL, 0), (y2, D_MODEL, 0), (target, D_MODEL, 0)],
                      [gate_f, g_post_mlp], [(D_MODEL, F32), (D_MODEL, BF16)], [128, D_MODEL, D_MODEL], s)


def _premlp_postmix_bwd(dh2, dx2, x1, y, scale_f, g_pre_mlp, gate_m, g_post_mix):
    s = x1.shape[0]

    def body(dh2_ref, dx2_ref, x1_ref, y_ref, sc_ref, gpl_ref, gm_ref, gpm_ref,
             dx1_ref, dy_ref, dsc_ref, dsh_ref, dgpl_ref, dgm_ref, dgpm_ref):
        x1 = x1_ref[...]
        r1 = _rms(x1)
        x1n = x1 * r1
        dh2 = dh2_ref[...]
        _acc_vec(dsc_ref, dh2 * (x1n * gpl_ref[...]))
        _acc_vec(dsh_ref, dh2)
        dn2 = dh2 * (1.0 + sc_ref[...])
        _acc_vec(dgpl_ref, dn2 * x1n)
        dx1 = dx2_ref[...] + _norm_bwd(dn2 * gpl_ref[...], x1n, r1)
        dx1_ref[...] = dx1
        yv = y_ref[...]
        ry = _rms(yv)
        yn = yv * ry
        _acc_vec(dgm_ref, dx1 * (yn * gpm_ref[...]))
        do = dx1 * gm_ref[...]
        _acc_vec(dgpm_ref, do * yn)
        dy_ref[...] = _norm_bwd(do * gpm_ref[...], yn, ry).astype(BF16)

    return _rows_call("premlp_postmix_bwd", body,
                      [(dh2, D_MODEL, 0), (dx2, D_MODEL, 0), (x1, D_MODEL, 0), (y, D_MODEL, 0)],
                      [scale_f, g_pre_mlp, gate_m, g_post_mix], [(D_MODEL, F32), (D_MODEL, BF16)],
                      [D_MODEL] * 5, s)


def _premix_bwd(dh, dx1, x, g_pre_mix, scale_m):
    s = x.shape[0]

    def body(dh_ref, dx1_ref, x_ref, g_ref, sc_ref, gx_ref, dsc_ref, dsh_ref, dg_ref):
        xv = x_ref[...]
        r = _rms(xv)
        xn = xv * r
        dh = dh_ref[...]
        _acc_vec(dsc_ref, dh * (xn * g_ref[...]))
        _acc_vec(dsh_ref, dh)
        dn1 = dh * (1.0 + sc_ref[...])
        _acc_vec(dg_ref, dn1 * xn)
        gx_ref[...] = dx1_ref[...] + _norm_bwd(dn1 * g_ref[...], xn, r)

    return _rows_call("premix_bwd", body, [(dh, D_MODEL, 0), (dx1, D_MODEL, 0), (x, D_MODEL, 0)],
                      [g_pre_mix, scale_m], [(D_MODEL, F32)], [D_MODEL] * 3, s)


def _split3(v):
    hi = v.astype(BF16)
    r1 = v - hi.astype(F32)
    mid = r1.astype(BF16)
    lo = (r1 - mid.astype(F32)).astype(BF16)
    return hi, mid, lo


def _dot_exact01(v, tri, contract=NN, tri_first=False):
    acc = None
    for part in _split3(v):
        lhs, rhs = (tri, part) if tri_first else (part, tri)
        p = lax.dot_general(lhs, rhs, (contract, ((), ())), preferred_element_type=F32)
        acc = p if acc is None else acc + p
    return acc


def _log_sigmoid(z):
    return jnp.minimum(z, 0.0) - jnp.log(1.0 + jnp.exp(-jnp.abs(z)))


def _fox_cum(small, bvec):
    s = small.shape[0]
    t = _tile(s, CUM_T)

    def body(sm_ref, b_ref, out_ref, carry):
        @pl.when(pl.program_id(0) == 0)
        def _():
            carry[...] = jnp.zeros_like(carry)

        lf = _log_sigmoid(sm_ref[...] + b_ref[...])
        lft = lf.T[0:FOX_HEADS, :]
        row = lax.broadcasted_iota(jnp.int32, (t, t), 0)
        col = lax.broadcasted_iota(jnp.int32, (t, t), 1)
        upper = (row <= col).astype(BF16)
        cum = _dot_exact01(lft, upper) + carry[:, 0:1]
        out_ref[...] = cum
        carry[...] = carry[...] + jnp.sum(lft, axis=1, keepdims=True)

    return _call(body, name="fox_cum", grid=(s // t,),
                 in_specs=[pl.BlockSpec((t, W_SMALL), lambda i: (i, 0)), pl.BlockSpec((1, W_SMALL), lambda i: (0, 0))],
                 out_specs=pl.BlockSpec((FOX_HEADS, t), lambda i: (0, i)),
                 out_shape=jax.ShapeDtypeStruct((FOX_HEADS, s), F32),
                 scratch_shapes=[pltpu.VMEM((FOX_HEADS, 128), F32)],
                 compiler_params=_params(("arbitrary",)))(small, bvec)


def _fox_cum_bwd(dc, dcq, small, bvec):
    s = small.shape[0]
    t = _tile(s, CUM_T)
    nb = s // t

    def body(dc_ref, dcq_ref, sm_ref, b_ref, out_ref, db_ref, carry):
        @pl.when(pl.program_id(0) == 0)
        def _():
            carry[...] = jnp.zeros_like(carry)
            db_ref[...] = jnp.zeros_like(db_ref)

        lane = lax.broadcasted_iota(jnp.int32, (t, W_SMALL), 1)
        dcq = jnp.zeros((t, W_SMALL), F32)
        for hh in range(FOX_HEADS):
            dcq = jnp.where(lane == hh, dcq_ref[hh], dcq)
        dcv = dc_ref[...] + dcq.T[0:FOX_HEADS, :]
        row = lax.broadcasted_iota(jnp.int32, (t, t), 0)
        col = lax.broadcasted_iota(jnp.int32, (t, t), 1)
        lower = (row >= col).astype(BF16)
        dlf = _dot_exact01(dcv, lower) + carry[:, 0:1]
        carry[...] = carry[...] + jnp.sum(dcv, axis=1, keepdims=True)
        z = sm_ref[...] + b_ref[...]
        zt = z.T[0:FOX_HEADS, :]
        dff = dlf * _sigmoid(-zt)
        db_ref[...] = db_ref[...] + jnp.sum(dff, axis=1, keepdims=True)
        full = jnp.concatenate([dff, jnp.zeros((W_SMALL - FOX_HEADS, t), F32)], axis=0)
        out_ref[...] = full.T

    return _call(body, name="fox_cum_bwd", grid=(nb,),
                 in_specs=[pl.BlockSpec((FOX_HEADS, t), lambda i: (0, nb - 1 - i)),
                           pl.BlockSpec((FOX_HEADS, t, 1), lambda i: (0, nb - 1 - i, 0)),
                           pl.BlockSpec((t, W_SMALL), lambda i: (nb - 1 - i, 0)),
                           pl.BlockSpec((1, W_SMALL), lambda i: (0, 0))],
                 out_specs=[pl.BlockSpec((t, W_SMALL), lambda i: (nb - 1 - i, 0)),
                            pl.BlockSpec((FOX_HEADS, 128), lambda i: (0, 0))],
                 out_shape=[jax.ShapeDtypeStruct((s, W_SMALL), F32), jax.ShapeDtypeStruct((FOX_HEADS, 128), F32)],
                 scratch_shapes=[pltpu.VMEM((FOX_HEADS, 128), F32)],
                 compiler_params=_params(("arbitrary",)))(dc, dcq, small, bvec)


FOX_SCALE = FOX_HEAD_DIM ** -0.5


def _fox_fwd(pm, crow):
    s = pm.shape[0]
    t = _tile(s, FOX_T)
    nb = s // t

    def body(q_ref, k_ref, v_ref, c_ref, o_ref, lse_ref, m_s, l_s, acc_s):
        i = pl.program_id(1)
        j = pl.program_id(2)

        @pl.when(j == 0)
        def _():
            m_s[...] = jnp.full_like(m_s, NEG)
            l_s[...] = jnp.zeros_like(l_s)
            acc_s[...] = jnp.zeros_like(acc_s)

        @pl.when(j <= i)
        def _():
            sc = lax.dot_general(q_ref[...], k_ref[...], (NT, ((), ())), preferred_element_type=F32)
            sc = sc * FOX_SCALE - c_ref[...]
            row = lax.broadcasted_iota(jnp.int32, (t, t), 0)
            col = lax.broadcasted_iota(jnp.int32, (t, t), 1)
            sc = jnp.where((j < i) | (row >= col), sc, NEG)
            m_prev = m_s[...]
            m_new = jnp.maximum(m_prev, jnp.max(sc, axis=1, keepdims=True))
            alpha = jnp.exp(m_prev - m_new)
            p = jnp.exp(sc - m_new)
            l_s[...] = alpha * l_s[...] + jnp.sum(p, axis=1, keepdims=True)
            p_hi = p.astype(BF16)
            p_lo = (p - p_hi.astype(F32)).astype(BF16)
            pv = jnp.dot(p_hi, v_ref[...], preferred_element_type=F32)
            pv = pv + jnp.dot(p_lo, v_ref[...], preferred_element_type=F32)
            acc_s[...] = alpha * acc_s[...] + pv
            m_s[...] = m_new

        @pl.when(j == i)
        def _():
            o_ref[...] = acc_s[...] / l_s[...]
            lse_ref[...] = m_s[...] + jnp.log(l_s[...])

    return _call(
        body, name="fox_fwd", grid=(FOX_HEADS, nb, nb),
        in_specs=[pl.BlockSpec((t, 128), lambda h, i, j: (i, h)),
                  pl.BlockSpec((t, 128), lambda h, i, j: (jnp.minimum(i, j), FOX_HEADS + h)),
                  pl.BlockSpec((t, 128), lambda h, i, j: (jnp.minimum(i, j), 2 * FOX_HEADS + h)),
                  pl.BlockSpec((None, 1, t), lambda h, i, j: (h, 0, jnp.minimum(i, j)))],
        out_specs=[pl.BlockSpec((t, 128), lambda h, i, j: (i, h)),
                   pl.BlockSpec((None, t, 1), lambda h, i, j: (h, i, 0))],
        out_shape=[jax.ShapeDtypeStruct((s, FOX_HEADS * 128), F32), jax.ShapeDtypeStruct((FOX_HEADS, s, 1), F32)],
        scratch_shapes=[pltpu.VMEM((t, 1), F32), pltpu.VMEM((t, 1), F32), pltpu.VMEM((t, 128), F32)],
        compiler_params=_params(("parallel", "parallel", "arbitrary")),
    )(pm, pm, pm, crow)


def _fox_bwd(pm, crow, o, lse, do):
    s = pm.shape[0]
    t = _tile(s, FOX_T)
    nb = s // t

    def body(q_ref, do_ref, o_ref, lse_ref, k_ref, v_ref, c_ref, dq_ref, dk_ref, dv_ref, dc_ref, dcq_ref,
             dk_acc, dv_acc, dc_acc, delta_s):
        j = pl.program_id(1)
        i = pl.program_id(2)

        @pl.when((j == 0) & (i == 0))
        def _():
            dq_ref[...] = jnp.zeros_like(dq_ref)
            dcq_ref[...] = jnp.zeros_like(dcq_ref)

        @pl.when(i == j)
        def _():
            dk_acc[...] = jnp.zeros_like(dk_acc)
            dv_acc[...] = jnp.zeros_like(dv_acc)
            dc_acc[...] = jnp.zeros_like(dc_acc)

        rows = pl.ds(pl.multiple_of(i * t, t), t)

        @pl.when(j == 0)
        def _():
            delta_s[rows, :] = jnp.sum(do_ref[...].astype(F32) * o_ref[...], axis=1, keepdims=True)

        @pl.when(i >= j)
        def _():
            q = q_ref[...]
            dov = do_ref[...]
            sc = lax.dot_general(q, k_ref[...], (NT, ((), ())), preferred_element_type=F32)
            sc = sc * FOX_SCALE - c_ref[...]
            row = lax.broadcasted_iota(jnp.int32, (t, t), 0)
            col = lax.broadcasted_iota(jnp.int32, (t, t), 1)
            p = jnp.where((i > j) | (row >= col), jnp.exp(sc - lse_ref[...]), 0.0)
            dp = lax.dot_general(dov, v_ref[...], (NT, ((), ())), preferred_element_type=F32)
            ds = p * (dp - delta_s[rows, :])
            dsb = ds.astype(BF16)
            dv_acc[...] += lax.dot_general(p.astype(BF16), dov, (TN, ((), ())), preferred_element_type=F32)
            dk_acc[...] += lax.dot_general(dsb, q, (TN, ((), ())), preferred_element_type=F32)
            dq_ref[rows, :] += jnp.dot(dsb, k_ref[...], preferred_element_type=F32) * FOX_SCALE
            dc_acc[...] -= jnp.sum(ds, axis=0, keepdims=True)
            dcq_ref[rows, :] += jnp.sum(ds, axis=1, keepdims=True)

        @pl.when(i == nb - 1)
        def _():
            dk_ref[...] = dk_acc[...] * FOX_SCALE
            dv_ref[...] = dv_acc[...]
            dc_ref[...] = dc_acc[...]

    qi = lambda h, j, i: (jnp.maximum(i, j), h)
    return _call(
        body, name="fox_bwd", grid=(FOX_HEADS, nb, nb),
        in_specs=[pl.BlockSpec((t, 128), qi), pl.BlockSpec((t, 128), qi), pl.BlockSpec((t, 128), qi),
                  pl.BlockSpec((None, t, 1), lambda h, j, i: (h, jnp.maximum(i, j), 0)),
                  pl.BlockSpec((t, 128), lambda h, j, i: (j, FOX_HEADS + h)),
                  pl.BlockSpec((t, 128), lambda h, j, i: (j, 2 * FOX_HEADS + h)),
                  pl.BlockSpec((None, 1, t), lambda h, j, i: (h, 0, j))],
        out_specs=[pl.BlockSpec((s, 128), lambda h, j, i: (0, h)),
                   pl.BlockSpec((t, 128), lambda h, j, i: (j, h)),
                   pl.BlockSpec((t, 128), lambda h, j, i: (j, h)),
                   pl.BlockSpec((None, 1, t), lambda h, j, i: (h, 0, j)),
                   pl.BlockSpec((None, s, 1), lambda h, j, i: (h, 0, 0))],
        out_shape=[jax.ShapeDtypeStruct((s, 1024), F32), jax.ShapeDtypeStruct((s, 1024), F32),
                   jax.ShapeDtypeStruct((s, 1024), F32), jax.ShapeDtypeStruct((FOX_HEADS, 1, s), F32),
                   jax.ShapeDtypeStruct((FOX_HEADS, s, 1), F32)],
        scratch_shapes=[pltpu.VMEM((t, 128), F32), pltpu.VMEM((t, 128), F32), pltpu.VMEM((1, t), F32),
                        pltpu.VMEM((s, 1), F32)],
        compiler_params=_params(("parallel", "arbitrary", "arbitrary")),
    )(pm, do, o, lse, pm, pm, crow)


GLA_SCALE = GLA_DK ** -0.5
GLA_Q_BLK = 3072 // 128
GLA_K_BLK = 3584 // 128
GLA_V_BLK = 4096 // 256


def _gla_gate(sm, wa_ref, b_ref):
    return jnp.dot(sm.astype(BF16), wa_ref[...], preferred_element_type=F32) + b_ref[...]


def _tri(n, strict):
    row = lax.broadcasted_iota(jnp.int32, (n, n), 0)
    col = lax.broadcasted_iota(jnp.int32, (n, n), 1)
    return ((row > col) if strict else (row >= col)).astype(BF16)


def _gla_fwd(pm, small, wa_pad, b_a2):
    s = pm.shape[0]
    r = _tile(s, GLA_R)
    nc = r // CHUNK

    def body(q_ref, k_ref, v_ref, sm_ref, wa_ref, b_ref, o_ref, st_ref, state):
        @pl.when(pl.program_id(1) == 0)
        def _():
            state[...] = jnp.zeros_like(state)

        tri = _tri(CHUNK, False)
        for c in range(nc):
            rows = slice(c * CHUNK, (c + 1) * CHUNK)
            la = _log_sigmoid(_gla_gate(sm_ref[rows, :], wa_ref, b_ref)) * (1.0 / GLA_TEMP)
            cum = _dot_exact01(la, tri, tri_first=True)
            total = jnp.sum(la, axis=0, keepdims=True)
            kdec = k_ref[rows, :].astype(F32) * jnp.exp(total - cum)
            ut = lax.dot_general(v_ref[rows, :], kdec.astype(BF16), (TN, ((), ())), preferred_element_type=F32)
            new = state[...] * jnp.exp(total) + ut
            state[...] = new
            newb = new.astype(BF16)
            st_ref[c] = newb
            qs = (q_ref[rows, :].astype(F32) * GLA_SCALE).astype(BF16)
            o_ref[rows, :] = lax.dot_general(qs, newb, (NT, ((), ())), preferred_element_type=F32)

    return _call(
        body, name="gla_fwd", grid=(GLA_HEADS, s // r),
        in_specs=[pl.BlockSpec((r, 128), lambda h, i: (i, GLA_Q_BLK + h)),
                  pl.BlockSpec((r, 128), lambda h, i: (i, GLA_K_BLK + h)),
                  pl.BlockSpec((r, 256), lambda h, i: (i, GLA_V_BLK + h)),
                  pl.BlockSpec((r, W_SMALL), lambda h, i: (i, 0)),
                  pl.BlockSpec((W_SMALL, 128), lambda h, i: (0, h)),
                  pl.BlockSpec((1, 128), lambda h, i: (0, h))],
        out_specs=[pl.BlockSpec((r, 256), lambda h, i: (i, h)),
                   pl.BlockSpec((nc, None, GLA_DV, GLA_DK), lambda h, i: (i, h, 0, 0))],
        out_shape=[jax.ShapeDtypeStruct((s, 1024), F32),
                   jax.ShapeDtypeStruct((s // CHUNK, GLA_HEADS, GLA_DV, GLA_DK), BF16)],
        scratch_shapes=[pltpu.VMEM((GLA_DV, GLA_DK), F32)],
        compiler_params=_params(("parallel", "arbitrary")),
    )(pm, pm, pm, small, wa_pad, b_a2)


def _gla_bwd(pm, small, wa_pad, b_a2, states, do):
    s = pm.shape[0]
    r = _tile(s, GLA_R)
    nc = r // CHUNK
    nb = s // r

    def body(q_ref, k_ref, v_ref, sm_ref, wa_ref, b_ref, do_ref, st_ref, prev_ref,
             dq_ref, dk_ref, dv_ref, dza_ref, db_ref, carry):
        step = pl.program_id(1)

        @pl.when(step == 0)
        def _():
            carry[...] = jnp.zeros_like(carry)
            db_ref[...] = jnp.zeros_like(db_ref)

        tri = _tri(CHUNK, False)
        tri_strict = _tri(CHUNK, True)
        db = jnp.zeros((1, 128), F32)
        for c in reversed(range(nc)):
            rows = slice(c * CHUNK, (c + 1) * CHUNK)
            z = _gla_gate(sm_ref[rows, :], wa_ref, b_ref)
            la = _log_sigmoid(z) * (1.0 / GLA_TEMP)
            cum = _dot_exact01(la, tri, tri_first=True)
            total = jnp.sum(la, axis=0, keepdims=True)
            w = jnp.exp(total - cum)
            decay = jnp.exp(total)
            kdec = k_ref[rows, :].astype(F32) * w
            dov = do_ref[rows, :]
            qs = (q_ref[rows, :].astype(F32) * GLA_SCALE).astype(BF16)
            dq_ref[rows, :] = jnp.dot(dov, st_ref[c], preferred_element_type=F32) * GLA_SCALE
            gt = lax.dot_general(dov, qs, (TN, ((), ())), preferred_element_type=F32) + carry[...]
            gtb = gt.astype(BF16)
            dv_ref[rows, :] = lax.dot_general(kdec.astype(BF16), gtb, (NT, ((), ())), preferred_element_type=F32)
            dkdec = jnp.dot(v_ref[rows, :], gtb, preferred_element_type=F32)
            dk_ref[rows, :] = dkdec * w
            e = dkdec * kdec
            if c > 0:
                prev = st_ref[c - 1].astype(F32)
            else:
                prev = jnp.where(step == nb - 1, 0.0, prev_ref[0].astype(F32))
            dtot = jnp.sum(gt * prev, axis=0, keepdims=True) * decay
            dla = dtot + _dot_exact01(e, tri_strict, tri_first=True)
            dza = dla * (1.0 / GLA_TEMP) * _sigmoid(-z)
            dza_ref[rows, :] = dza.astype(BF16)
            db = db + jnp.sum(dza, axis=0, keepdims=True)
            carry[...] = gt * decay
        db_ref[...] += db

    blk = lambda h, i: nb - 1 - i
    return _call(
        body, name="gla_bwd", grid=(GLA_HEADS, nb),
        in_specs=[pl.BlockSpec((r, 128), lambda h, i: (blk(h, i), GLA_Q_BLK + h)),
                  pl.BlockSpec((r, 128), lambda h, i: (blk(h, i), GLA_K_BLK + h)),
                  pl.BlockSpec((r, 256), lambda h, i: (blk(h, i), GLA_V_BLK + h)),
                  pl.BlockSpec((r, W_SMALL), lambda h, i: (blk(h, i), 0)),
                  pl.BlockSpec((W_SMALL, 128), lambda h, i: (0, h)),
                  pl.BlockSpec((1, 128), lambda h, i: (0, h)),
                  pl.BlockSpec((r, 256), lambda h, i: (blk(h, i), h)),
                  pl.BlockSpec((nc, None, GLA_DV, GLA_DK), lambda h, i: (blk(h, i), h, 0, 0)),
                  pl.BlockSpec((1, None, GLA_DV, GLA_DK),
                               lambda h, i: (jnp.maximum(blk(h, i) * nc - 1, 0), h, 0, 0))],
        out_specs=[pl.BlockSpec((r, 128), lambda h, i: (blk(h, i), h)),
                   pl.BlockSpec((r, 128), lambda h, i: (blk(h, i), h)),
                   pl.BlockSpec((r, 256), lambda h, i: (blk(h, i), h)),
                   pl.BlockSpec((r, 128), lambda h, i: (blk(h, i), h)),
                   pl.BlockSpec((1, 128), lambda h, i: (0, h))],
        out_shape=[jax.ShapeDtypeStruct((s, 512), F32), jax.ShapeDtypeStruct((s, 512), F32),
                   jax.ShapeDtypeStruct((s, 1024), F32), jax.ShapeDtypeStruct((s, 512), BF16),
                   jax.ShapeDtypeStruct((1, 512), F32)],
        scratch_shapes=[pltpu.VMEM((GLA_DV, GLA_DK), F32)],
        compiler_params=_params(("parallel", "arbitrary")),
    )(pm, pm, pm, small, wa_pad, b_a2, do, states, states)


def _modulation(c_all, w_ada):
    n = w_ada.shape[1]
    tn = _tile(n, 512)

    def body(c_ref, w_ref, out_ref, ca_ref):
        cv = c_ref[...]
        ca = cv * _sigmoid(cv)
        ca_ref[...] = ca
        out_ref[...] = jnp.dot(ca.astype(BF16), w_ref[...].astype(BF16), preferred_element_type=F32)

    return _call(body, name="modulation", grid=(n // tn,),
                 in_specs=[pl.BlockSpec((N_DEV, D_MODEL), lambda j: (0, 0)),
                           pl.BlockSpec((D_MODEL, tn), lambda j: (0, j))],
                 out_specs=[pl.BlockSpec((N_DEV, tn), lambda j: (0, j)),
                            pl.BlockSpec((N_DEV, D_MODEL), lambda j: (0, 0))],
                 out_shape=[jax.ShapeDtypeStruct((N_DEV, n), F32), jax.ShapeDtypeStruct((N_DEV, D_MODEL), F32)],
                 compiler_params=_params(("arbitrary",)))(c_all, w_ada)


def _adamw_math(w, g, m, v):
    m = ADAM_B1 * m + (1.0 - ADAM_B1) * g
    v = ADAM_B2 * v + (1.0 - ADAM_B2) * (g * g)
    m_hat = m / (1.0 - ADAM_B1 ** ADAM_STEP)
    v_hat = v / (1.0 - ADAM_B2 ** ADAM_STEP)
    delta = -ADAM_LR * (m_hat / (jnp.sqrt(v_hat) + ADAM_EPS) + ADAM_WD * w)
    return delta, m, v


def _adamw_slabs(name, w, slabs, m, v, tr=256):
    rr, cc = w.shape
    tr = _tile(rr, tr)

    def body(w_ref, s_ref, m_ref, v_ref, g_ref, d_ref, nm_ref, nv_ref):
        g = s_ref[0].astype(F32)
        for r in range(1, N_DEV):
            g = g + s_ref[r].astype(F32)
        g_ref[...] = g
        d, nm, nv = _adamw_math(w_ref[...], g, m_ref[...], v_ref[...])
        d_ref[...] = d
        nm_ref[...] = nm
        nv_ref[...] = nv

    spec = pl.BlockSpec((tr, cc), lambda i: (i, 0))
    return _call(body, name=name, grid=(rr // tr,),
                 in_specs=[spec, pl.BlockSpec((N_DEV, tr, cc), lambda i: (0, i, 0)), spec, spec],
                 out_specs=[spec] * 4, out_shape=[jax.ShapeDtypeStruct((rr, cc), F32)] * 4,
                 compiler_params=_params(("parallel",)))(w, slabs, m, v)


def _adamw_ada(w, cat, dm, m, v, tr=256):
    rr, cc = w.shape
    tr = _tile(rr, tr)

    def body(w_ref, ca_ref, dm_ref, m_ref, v_ref, g_ref, d_ref, nm_ref, nv_ref):
        g = ca_ref[:, 0:1] * dm_ref[0:1, :]
        for b in range(1, N_DEV):
            g = g + ca_ref[:, b:b + 1] * dm_ref[b:b + 1, :]
        g_ref[...] = g
        d, nm, nv = _adamw_math(w_ref[...], g, m_ref[...], v_ref[...])
        d_ref[...] = d
        nm_ref[...] = nm
        nv_ref[...] = nv

    spec = pl.BlockSpec((tr, cc), lambda i: (i, 0))
    return _call(body, name="adamw_ada", grid=(rr // tr,),
                 in_specs=[spec, pl.BlockSpec((tr, N_DEV), lambda i: (i, 0)),
                           pl.BlockSpec((N_DEV, cc), lambda i: (0, 0)), spec, spec],
                 out_specs=[spec] * 4, out_shape=[jax.ShapeDtypeStruct((rr, cc), F32)] * 4,
                 compiler_params=_params(("parallel",)))(w, cat, dm, m, v)


def _sum_devices(gathered):
    ln = gathered.shape[-1]

    def body(g_ref, out_ref):
        acc = g_ref[0]
        for r in range(1, N_DEV):
            acc = acc + g_ref[r]
        out_ref[...] = acc

    return _call(body, name="sum_devices",
                 in_specs=[pl.BlockSpec(memory_space=pltpu.VMEM)], out_specs=pl.BlockSpec(memory_space=pltpu.VMEM),
                 out_shape=jax.ShapeDtypeStruct((1, ln), F32))(gathered)


def _adamw_flat(w, g, m, v):
    def body(w_ref, g_ref, m_ref, v_ref, d_ref, nm_ref, nv_ref):
        d, nm, nv = _adamw_math(w_ref[...], g_ref[...], m_ref[...], v_ref[...])
        d_ref[...] = d
        nm_ref[...] = nm
        nv_ref[...] = nv

    vm = pl.BlockSpec(memory_space=pltpu.VMEM)
    return _call(body, name="adamw_small", in_specs=[vm] * 4, out_specs=[vm] * 3,
                 out_shape=[jax.ShapeDtypeStruct(w.shape, F32)] * 3)(w, g, m, v)


def _from_col_shards(g):
    return jnp.transpose(g, (1, 0, 2)).reshape(g.shape[1], N_DEV * g.shape[2])


def _pad_lanes(v, n):
    return jnp.concatenate([v, jnp.zeros(v.shape[:-1] + (n - v.shape[-1],), v.dtype)], axis=-1)


def kernel(x, c, w_ada, b_ada, g_pre_mix, g_post_mix, w_in, b_fgate, w_gla_a2, b_gla_a2, g_fox_out, g_gla_out, w_out, g_pre_mlp, g_post_mlp, w_mlp_in, w_mlp_out, loss_target, m_w_ada, m_b_ada, m_g_pre_mix, m_g_post_mix, m_w_in, m_b_fgate, m_w_gla_a2, m_b_gla_a2, m_g_fox_out, m_g_gla_out, m_w_out, m_g_pre_mlp, m_g_post_mlp, m_w_mlp_in, m_w_mlp_out, v_w_ada, v_b_ada, v_g_pre_mix, v_g_post_mix, v_w_in, v_b_fgate, v_w_gla_a2, v_b_gla_a2, v_g_fox_out, v_g_gla_out, v_w_out, v_g_pre_mlp, v_g_post_mlp, v_w_mlp_in, v_w_mlp_out):
    rank = _my_rank()
    xs = x[0]
    s = xs.shape[0]
    target = loss_target[0]

    c_all, wa2_g, ggla_g = _all_gather("gather_small", [c, w_gla_a2[0], g_gla_out[0]])
    win_g, wout_g, wmi_g, wmo_g = _all_gather(
        "gather_weights",
        [w_in[0].astype(BF16), w_out[0].astype(BF16), w_mlp_in[0].astype(BF16), w_mlp_out[0].astype(BF16)])
    w_a2 = _from_col_shards(wa2_g)
    g_gla = _from_col_shards(ggla_g).reshape(1, 1024)
    g_fox = g_fox_out.reshape(1, 1024)
    win_full = _from_col_shards(win_g)
    w_main = jnp.concatenate([win_full[:, :3072], win_full[:, 3080:5128], win_full[:, 5144:6168]], axis=1)
    w_small = _pad_lanes(jnp.concatenate([win_full[:, 3072:3080], win_full[:, 5128:5144]], axis=1), W_SMALL)
    w_out_full = wout_g.reshape(D_MODEL, D_MODEL)
    w_mo_full = wmo_g.reshape(D_FF, D_MODEL)
    wa_pad = jnp.concatenate([jnp.zeros((8, 512), BF16), w_a2.astype(BF16), jnp.zeros((104, 512), BF16)], axis=0)
    bf_vec = _pad_lanes(b_fgate, W_SMALL)

    mod_part, c_act = _modulation(c_all.reshape(N_DEV, D_MODEL), w_ada[0])
    (mod_g,) = _all_gather("gather_mod", [mod_part])
    mod = lax.dynamic_slice_in_dim(mod_g, rank, 1, axis=1).reshape(1, 6 * D_MODEL) + b_ada
    shift_m, scale_m, gate_m, shift_f, scale_f, gate_f = [mod[:, i * D_MODEL:(i + 1) * D_MODEL] for i in range(6)]

    h = _premix(xs, g_pre_mix, scale_m, shift_m)
    pm = _mm_plain("proj_main", h, w_main, NN, BF16)
    small = _mm_plain("proj_small", h, w_small, NN, F32)
    crow = _fox_cum(small, bf_vec).reshape(FOX_HEADS, 1, s)
    o_fox, lse = _fox_fwd(pm, crow)
    o_gla, states = _gla_fwd(pm, small, wa_pad, b_gla_a2)
    mix = _mix_fwd(o_fox, o_gla, pm, g_fox, g_gla)
    y = _mm_plain("out_proj", mix, w_out_full, NN, F32)
    x1, h2 = _postmix_premlp(xs, y, gate_m, g_post_mix, g_pre_mlp, scale_f, shift_f)

    tm, tn, tk = _tile(s, 1024), 1024, 512
    nsh = 1024 // tn

    def relu2(acc):
        rl = jnp.maximum(acc, 0.0)
        return rl * rl, rl

    z, a_relu = _matmul(
        "mlp_in", h2, wmi_g, contract=NN, grid=(s // tm, D_FF // tn, D_MODEL // tk),
        a_spec=pl.BlockSpec((tm, tk), lambda i, j, k: (i, k)),
        b_spec=pl.BlockSpec((None, tk, tn), lambda i, j, k: (j // nsh, k, j % nsh)),
        out_specs=[pl.BlockSpec((tm, tn), lambda i, j, k: (i, j))] * 2,
        out_shapes=[jax.ShapeDtypeStruct((s, D_FF), BF16)] * 2, acc_shape=(tm, tn), epilogue=relu2)
    y2 = _mm_plain("mlp_out", z, w_mo_full, NN, F32)

    dx2, dy2, loss_vec, dgate_f, dg_post_mlp = _loss_postmlp_bwd(x1, y2, target, gate_f, g_post_mlp)
    loss = lax.psum(loss_vec[0, 0], ("x", "y", "c"))

    da = _mm_plain("mlp_out_dx", dy2, w_mo_full, NT, BF16, extra=(a_relu,),
                   epilogue=lambda acc, rl: (acc * (2.0 * rl.astype(F32)),))
    dw_mo = _mm_plain("mlp_out_dw", z, dy2, TN, BF16)
    kb = 1024 // tk
    (dh2,) = _matmul(
        "mlp_in_dx", da, wmi_g, contract=NT, grid=(s // tm, D_MODEL // tn, D_FF // tk),
        a_spec=pl.BlockSpec((tm, tk), lambda i, j, k: (i, k)),
        b_spec=pl.BlockSpec((None, tn, tk), lambda i, j, k: (k // kb, j, k % kb)),
        out_specs=[pl.BlockSpec((tm, tn), lambda i, j, k: (i, j))],
        out_shapes=[jax.ShapeDtypeStruct((s, D_MODEL), F32)], acc_shape=(tm, tn))
    ts = _tile(s, 512)
    (dw_mi,) = _matmul(
        "mlp_in_dw", h2, da, contract=TN, grid=(D_MODEL // 1024, D_FF // tn, s // ts),
        a_spec=pl.BlockSpec((ts, 1024), lambda i, j, k: (k, i)),
        b_spec=pl.BlockSpec((ts, tn), lambda i, j, k: (k, j)),
        out_specs=[pl.BlockSpec((None, 1024, tn), lambda i, j, k: (j // nsh, i, j % nsh))],
        out_shapes=[jax.ShapeDtypeStruct((N_DEV, D_MODEL, 1024), BF16)], acc_shape=(1024, tn))

    dx1, dy, dscale_f, dshift_f, dg_pre_mlp, dgate_m, dg_post_mix = _premlp_postmix_bwd(
        dh2, dx2, x1, y, scale_f, g_pre_mlp, gate_m, g_post_mix)

    dmix = _mm_plain("out_proj_dx", dy, w_out_full, NT, F32)
    dw_out = _mm_plain("out_proj_dw", mix, dy, TN, BF16)
    do_fox, do_gla, dgr, dg_fox, dg_gla = _mix_bwd(dmix, o_fox, o_gla, pm, g_fox, g_gla)

    dq, dk, dv, dc, dcq = _fox_bwd(pm, crow, o_fox, lse, do_fox)
    dsmall_f, db_f = _fox_cum_bwd(dc.reshape(FOX_HEADS, s), dcq, small, bf_vec)
    dgq, dgk, dgv, dza, db_a2 = _gla_bwd(pm, small, wa_pad, b_gla_a2, states, do_gla)
    dsmall = _mm_plain("gate_dx", dza, wa_pad, NT, F32, tn=128, extra=(dsmall_f,),
                       epilogue=lambda acc, other: (acc + other,))
    dwa_pad = _mm_plain("gate_dw", small, dza, TN, F32, tm=128, tn=512)

    dpm = jnp.concatenate([dq.astype(BF16), dk.astype(BF16), dv.astype(BF16), dgq.astype(BF16), dgk.astype(BF16),
                           dgv.astype(BF16), dgr], axis=1)
    dh_small = _mm_plain("proj_small_dx", dsmall, w_small, NT, F32, tk=128)
    dh = _mm_plain("proj_main_dx", dpm, w_main, NT, F32, extra=(dh_small,),
                   epilogue=lambda acc, other: (acc + other,))
    dw_main = _mm_plain("proj_main_dw", h, dpm, TN, BF16)
    dw_small = _mm_plain("proj_small_dw", h, dsmall, TN, BF16, tn=128)
    grad_x, dscale_m, dshift_m, dg_pre_mix = _premix_bwd(dh, dx1, xs, g_pre_mix, scale_m)

    dwin_full = jnp.concatenate([dw_main[:, :3072], dw_small[:, 0:8], dw_main[:, 3072:5120], dw_small[:, 8:24],
                                 dw_main[:, 5120:6144]], axis=1)
    dwin_slabs = jnp.transpose(dwin_full.reshape(D_MODEL, N_DEV, 771), (1, 0, 2))
    r_in, r_out, r_mi, r_mo = _all_to_all(
        "exchange_grads",
        [dwin_slabs, dw_out.reshape(N_DEV, 256, D_MODEL), dw_mi, dw_mo.reshape(N_DEV, 1024, D_MODEL)])

    dmod = jnp.concatenate([dshift_m, dscale_m, dgate_m, dshift_f, dscale_f, dgate_f], axis=1)
    flat = jnp.concatenate(
        [dmod, dg_pre_mix, dg_post_mix, dg_fox, dg_pre_mlp, dg_post_mlp, db_a2,
         dwa_pad[8:24, :].reshape(1, GLA_RANK * 512), dg_gla, _pad_lanes(db_f[:, 0].reshape(1, FOX_HEADS), 128)],
        axis=1)
    (flat_g,) = _all_gather("gather_small_grads", [flat])
    tot = _sum_devices(flat_g)

    g_in, d_in, nm_in, nv_in = _adamw_slabs("adamw_w_in", w_in[0], r_in, m_w_in[0], v_w_in[0])
    g_out, d_out, nm_out, nv_out = _adamw_slabs("adamw_w_out", w_out[0], r_out, m_w_out[0], v_w_out[0])
    g_mi, d_mi, nm_mi, nv_mi = _adamw_slabs("adamw_w_mlp_in", w_mlp_in[0], r_mi, m_w_mlp_in[0], v_w_mlp_in[0])
    g_mo, d_mo, nm_mo, nv_mo = _adamw_slabs("adamw_w_mlp_out", w_mlp_out[0], r_mo, m_w_mlp_out[0], v_w_mlp_out[0])

    dm_cols = lax.dynamic_slice_in_dim(flat_g[:, 0, :6 * D_MODEL], rank * 1536, 1536, axis=1)
    g_ada, d_ada, nm_ada, nv_ada = _adamw_ada(w_ada[0], c_act.T, dm_cols, m_w_ada[0], v_w_ada[0])

    o = 0
    seg = {}
    for name, n in (("b_ada", 12288), ("g_pre_mix", 2048), ("g_post_mix", 2048), ("g_fox_out", 1024),
                    ("g_pre_mlp", 2048), ("g_post_mlp", 2048), ("b_gla_a2", 512), ("w_gla_a2", 8192),
                    ("g_gla_out", 1024), ("b_fgate", 128)):
        seg[name] = tot[:, o:o + n]
        o += n
    g_wa2 = lax.dynamic_slice_in_dim(seg["w_gla_a2"].reshape(GLA_RANK, 512), rank * 64, 64, axis=1)
    g_ggla = lax.dynamic_slice_in_dim(seg["g_gla_out"].reshape(GLA_HEADS, GLA_DV), rank * 32, 32, axis=1)
    small_names = ["b_ada", "g_pre_mix", "g_post_mix", "g_fox_out", "g_pre_mlp", "g_post_mlp", "b_gla_a2",
                   "w_gla_a2", "g_gla_out", "b_fgate"]
    small_grads = {**seg, "w_gla_a2": g_wa2.reshape(1, 1024), "g_gla_out": g_ggla.reshape(1, 128)}
    weights = dict(b_ada=b_ada, g_pre_mix=g_pre_mix, g_post_mix=g_post_mix, g_fox_out=g_fox_out,
                   g_pre_mlp=g_pre_mlp, g_post_mlp=g_post_mlp, b_gla_a2=b_gla_a2, w_gla_a2=w_gla_a2,
                   g_gla_out=g_gla_out, b_fgate=b_fgate)
    moms = dict(b_ada=m_b_ada, g_pre_mix=m_g_pre_mix, g_post_mix=m_g_post_mix, g_fox_out=m_g_fox_out,
                g_pre_mlp=m_g_pre_mlp, g_post_mlp=m_g_post_mlp, b_gla_a2=m_b_gla_a2, w_gla_a2=m_w_gla_a2,
                g_gla_out=m_g_gla_out, b_fgate=m_b_fgate)
    vels = dict(b_ada=v_b_ada, g_pre_mix=v_g_pre_mix, g_post_mix=v_g_post_mix, g_fox_out=v_g_fox_out,
                g_pre_mlp=v_g_pre_mlp, g_post_mlp=v_g_post_mlp, b_gla_a2=v_b_gla_a2, w_gla_a2=v_w_gla_a2,
                g_gla_out=v_g_gla_out, b_fgate=v_b_fgate)

    def flatten(d, fill):
        parts = []
        for nm in small_names:
            p = d[nm].reshape(1, -1)
            if nm == "b_fgate":
                p = jnp.concatenate([p[:, :FOX_HEADS], jnp.full((1, 128 - FOX_HEADS), fill, F32)], axis=1)
            parts.append(p)
        return jnp.concatenate(parts, axis=1).reshape(-1, 128)

    fw, fg, fm, fv = flatten(weights, 0.0), flatten(small_grads, 0.0), flatten(moms, 0.0), flatten(vels, 1.0)
    fd, fnm, fnv = _adamw_flat(fw, fg, fm, fv)

    def unflatten(fl):
        fl = fl.reshape(1, -1)
        out = {}
        o = 0
        for nm in small_names:
            n = 128 if nm == "b_fgate" else weights[nm].size
            piece = fl[:, o:o + n]
            if nm == "b_fgate":
                piece = piece[:, :FOX_HEADS]
            out[nm] = piece.reshape(weights[nm].shape)
            o += n
        return out

    sg, sd, snm, snv = unflatten(fg), unflatten(fd), unflatten(fnm), unflatten(fnv)

    big = dict(w_ada=(g_ada, d_ada, nm_ada, nv_ada), w_in=(g_in, d_in, nm_in, nv_in),
               w_out=(g_out, d_out, nm_out, nv_out), w_mlp_in=(g_mi, d_mi, nm_mi, nv_mi),
               w_mlp_out=(g_mo, d_mo, nm_mo, nv_mo))
    order = ["w_ada", "b_ada", "g_pre_mix", "g_post_mix", "w_in", "b_fgate", "w_gla_a2", "b_gla_a2", "g_fox_out",
             "g_gla_out", "w_out", "g_pre_mlp", "g_post_mlp", "w_mlp_in", "w_mlp_out"]

    def pick(nm, idx):
        if nm in big:
            return big[nm][idx][None]
        return (sg, sd, snm, snv)[idx][nm]

    grads = [pick(nm, 0) for nm in order]
    deltas = [pick(nm, 1) for nm in order]
    new_m = [pick(nm, 2) for nm in order]
    new_v = [pick(nm, 3) for nm in order]
    return (loss, grad_x[None], *grads, *deltas, *new_m, *new_v)
```

```python
import functools

import numpy as np
import jax
import jax.numpy as jnp
from jax import lax
from jax.experimental import pallas as pl
from jax.experimental.pallas import tpu as pltpu

F32 = jnp.float32
BF16 = jnp.bfloat16
MESH = pl.DeviceIdType.MESH
N_DEV = 8

D_MODEL = 2048
FOX_HEADS = 8
FOX_HEAD_DIM = 128
GLA_HEADS = 4
GLA_DK = 128
GLA_DV = 256
GLA_RANK = 16
GLA_TEMP = 16.0
CHUNK = 64
D_FF = 8192
W_MAIN = 6144
W_SMALL = 128
EPS = 1e-6
NEG = float(np.finfo(np.float32).min)

ADAM_LR = 0.001
ADAM_B1 = 0.9
ADAM_B2 = 0.999
ADAM_EPS = 1e-08
ADAM_WD = 0.01
ADAM_STEP = 10

ROW_T = 256
FOX_T = 512
GLA_R = 512
CUM_T = 256
VMEM_LIMIT = 56 * 1024 * 1024


def _call(body, **kw):
    return pl.pallas_call(body, **kw)


def _params(sem=None):
    return pltpu.CompilerParams(dimension_semantics=sem, vmem_limit_bytes=VMEM_LIMIT)


def _my_pos():
    return lax.axis_index("x"), lax.axis_index("y"), lax.axis_index("c")


def _my_rank():
    x, y, c = _my_pos()
    return 4 * x + 2 * y + c


def _all_gather(name, arrays):
    n = len(arrays)

    def body(*refs):
        ins = refs[:n]
        outs = refs[n:2 * n]
        send_sems, recv_sems, local_sems = refs[2 * n:]
        x, y, c = _my_pos()
        me, sibling = (x, y, c), (x, y, 1 - c)
        chips = [(1 - x, y), (x, 1 - y), (1 - x, 1 - y)]

        def slot(a, px, py, pc):
            return outs[a].at[4 * px + 2 * py + pc]

        def copy(a, k, block, to, src=None):
            return pltpu.make_async_remote_copy(
                src_ref=slot(a, *block) if src is None else src, dst_ref=slot(a, *block),
                send_sem=send_sems.at[a, k], recv_sem=recv_sems.at[a, k],
                device_id=to, device_id_type=MESH)

        started = []
        for a in range(n):
            mine = pltpu.make_async_copy(ins[a], slot(a, *me), local_sems.at[a])
            mine.start()
            started.append(mine)
        first = []
        for a in range(n):
            first.append(copy(a, 0, me, sibling, src=ins[a]))
            first += [copy(a, 1 + j, me, (*chip, c), src=ins[a]) for j, chip in enumerate(chips)]
        for cp in first:
            cp.start()
        passed = []
        for j, chip in enumerate(chips):
            for a in range(n):
                copy(a, 1 + j, (*chip, c), me).wait_recv()
                fwd = copy(a, 4 + j, (*chip, c), sibling)
                fwd.start()
                passed.append(fwd)
        for a in range(n):
            copy(a, 0, sibling, me).wait_recv()
            for j, chip in enumerate(chips):
                copy(a, 4 + j, (*chip, 1 - c), me).wait_recv()
        for cp in first + passed:
            cp.wait_send()
        for mine in started:
            mine.wait()

    hbm = pl.BlockSpec(memory_space=pltpu.HBM)
    return _call(
        body, name=name,
        out_shape=[jax.ShapeDtypeStruct((N_DEV,) + a.shape, a.dtype) for a in arrays],
        in_specs=[hbm] * n, out_specs=[hbm] * n,
        scratch_shapes=[pltpu.SemaphoreType.DMA((n, 7)), pltpu.SemaphoreType.DMA((n, 7)),
                        pltpu.SemaphoreType.DMA((n,))],
    )(*arrays)


def _all_to_all(name, arrays):
    n = len(arrays)

    def body(*refs):
        ins = refs[:n]
        outs = refs[n:2 * n]
        send_sems, recv_sems, local_sems = refs[2 * n:]
        x, y, c = _my_pos()
        me = 4 * x + 2 * y + c
        flips = [(kx, ky, kc) for kx in (0, 1) for ky in (0, 1) for kc in (0, 1)][1:]

        def peer(k):
            kx, ky, kc = flips[k]
            return (1 - x if kx else x), (1 - y if ky else y), (1 - c if kc else c)

        local = []
        for a in range(n):
            cp = pltpu.make_async_copy(ins[a].at[me], outs[a].at[me], local_sems.at[a])
            cp.start()
            local.append(cp)
        sends = []
        for k in range(7):
            px, py, pc = peer(k)
            pr = 4 * px + 2 * py + pc
            for a in range(n):
                cp = pltpu.make_async_remote_copy(
                    src_ref=ins[a].at[pr], dst_ref=outs[a].at[me],
                    send_sem=send_sems.at[a, k], recv_sem=recv_sems.at[a, k],
                    device_id=(px, py, pc), device_id_type=MESH)
                cp.start()
                sends.append(cp)
        for k in range(7):
            px, py, pc = peer(k)
            pr = 4 * px + 2 * py + pc
            for a in range(n):
                pltpu.make_async_remote_copy(
                    src_ref=ins[a].at[pr], dst_ref=outs[a].at[pr],
                    send_sem=send_sems.at[a, k], recv_sem=recv_sems.at[a, k],
                    device_id=(px, py, pc), device_id_type=MESH).wait_recv()
        for cp in sends:
            cp.wait_send()
        for cp in local:
            cp.wait()

    hbm = pl.BlockSpec(memory_space=pltpu.HBM)
    return _call(
        body, name=name,
        out_shape=[jax.ShapeDtypeStruct(a.shape, a.dtype) for a in arrays],
        in_specs=[hbm] * n, out_specs=[hbm] * n,
        scratch_shapes=[pltpu.SemaphoreType.DMA((n, 7)), pltpu.SemaphoreType.DMA((n, 7)),
                        pltpu.SemaphoreType.DMA((n,))],
    )(*arrays)


_HBM = pl.BlockSpec(memory_space=pltpu.HBM)
_SEM = pl.BlockSpec(memory_space=pltpu.SEMAPHORE)
_FLIPS = [(kx, ky, kc) for kx in (0, 1) for ky in (0, 1) for kc in (0, 1)][1:]


def _peers():
    x, y, c = _my_pos()
    out = []
    for kx, ky, kc in _FLIPS:
        px, py, pc = (1 - x if kx else x), (1 - y if ky else y), (1 - c if kc else c)
        out.append(((px, py, pc), 4 * px + 2 * py + pc))
    return out


def _exchange_copy(gather, ins, lands, send_sems, recv_sems, a, k, peer, peer_rank, me):
    return pltpu.make_async_remote_copy(
        src_ref=ins[a] if gather else ins[a].at[peer_rank],
        dst_ref=lands[a].at[me],
        send_sem=send_sems[a].at[k], recv_sem=recv_sems[a].at[k],
        device_id=peer, device_id_type=MESH)


def _exchange_start(name, arrays, gather):
    n = len(arrays)
    land_shapes = [((N_DEV,) + a.shape) if gather else a.shape for a in arrays]

    def body(*refs):
        ins, lands = refs[:n], refs[n:2 * n]
        send_sems, recv_sems = refs[2 * n:3 * n], refs[3 * n:4 * n]
        token = refs[-1]
        me = _my_rank()
        for a in range(n):
            for k, (peer, peer_rank) in enumerate(_peers()):
                _exchange_copy(gather, ins, lands, send_sems, recv_sems, a, k, peer, peer_rank, me).start()
        token[...] = jnp.zeros_like(token)

    sems = [pltpu.SemaphoreType.DMA((7,))] * (2 * n)
    outs = pl.pallas_call(
        body, name=name,
        out_shape=(*sems, *[pltpu.HBM(a.shape, a.dtype) for a in arrays],
                   *[pltpu.HBM(ls, a.dtype) for ls, a in zip(land_shapes, arrays)],
                   jax.ShapeDtypeStruct((8, 128), F32)),
        in_specs=[_HBM] * (2 * n),
        out_specs=(*[_SEM] * (2 * n), *[_HBM] * (2 * n), pl.BlockSpec(memory_space=pltpu.VMEM)),
        input_output_aliases={i: 2 * n + i for i in range(2 * n)},
        compiler_params=pltpu.CompilerParams(has_side_effects=pltpu.SideEffectType.DATAFLOW_SIDE_EFFECTING),
    )(*[pltpu.with_memory_space_constraint(a, pltpu.HBM) for a in arrays],
      *[pltpu.with_memory_space_constraint(lax.empty(ls, a.dtype), pltpu.HBM) for ls, a in zip(land_shapes, arrays)])
    return outs[:n], outs[n:2 * n], outs[2 * n:3 * n], outs[3 * n:4 * n], outs[-1]


def _exchange_wait(name, send_sems, recv_sems, thru, lands, after, gather):
    n = len(thru)

    def body(*refs):
        ins, lnd = refs[:n], refs[n:2 * n]
        ssem, rsem = refs[2 * n:3 * n], refs[3 * n:4 * n]
        local_sems = refs[-1]
        me = _my_rank()
        local = []
        for a in range(n):
            cp = pltpu.make_async_copy(ins[a] if gather else ins[a].at[me], lnd[a].at[me], local_sems.at[a])
            cp.start()
            local.append(cp)
        for a in range(n):
            for k, (peer, peer_rank) in enumerate(_peers()):
                cp = _exchange_copy(gather, ins, lnd, ssem, rsem, a, k, peer, peer_rank, peer_rank)
                cp.wait_send()
                cp.wait_recv()
        for cp in local:
            cp.wait()

    outs = pl.pallas_call(
        body, name=name,
        out_shape=(*[pltpu.HBM(t.shape, t.dtype) for t in thru], *[pltpu.HBM(ld.shape, ld.dtype) for ld in lands]),
        in_specs=[*[_HBM] * (2 * n), *[_SEM] * (2 * n), pl.BlockSpec(memory_space=pl.ANY)],
        out_specs=tuple([_HBM] * (2 * n)),
        input_output_aliases={i: i for i in range(2 * n)},
        scratch_shapes=[pltpu.SemaphoreType.DMA((n,))],
        compiler_params=pltpu.CompilerParams(has_side_effects=pltpu.SideEffectType.DATAFLOW_SIDE_EFFECTING),
    )(*thru, *lands, *send_sems, *recv_sems, after)
    return outs[n:]


def _tied(value, dep):
    return lax.optimization_barrier((value, dep))[0]


NN = ((1,), (0,))
NT = ((1,), (1,))
TN = ((0,), (0,))


def _matmul(name, a, b, *, contract, grid, a_spec, b_spec, out_specs, out_shapes, acc_shape,
            extra=(), extra_specs=(), epilogue=None):
    nk = grid[2]
    n_extra = len(extra)
    n_out = len(out_shapes)

    def body(*refs):
        a_ref, b_ref = refs[0], refs[1]
        extra_refs = refs[2:2 + n_extra]
        out_refs = refs[2 + n_extra:2 + n_extra + n_out]
        acc_ref = refs[-1]
        k = pl.program_id(2)
        prod = lax.dot_general(a_ref[...].astype(BF16), b_ref[...].astype(BF16), (contract, ((), ())),
                               preferred_element_type=F32)

        @pl.when(k == 0)
        def _():
            acc_ref[...] = prod

        @pl.when(k > 0)
        def _():
            acc_ref[...] += prod

        @pl.when(k == nk - 1)
        def _():
            acc = acc_ref[...]
            if epilogue is None:
                res = (acc,)
            else:
                res = epilogue(acc, *[r[...] for r in extra_refs])
            for o_ref, val in zip(out_refs, res):
                o_ref[...] = val.astype(o_ref.dtype)

    outs = _call(
        body, name=name, grid=grid,
        in_specs=[a_spec, b_spec, *extra_specs], out_specs=list(out_specs), out_shape=list(out_shapes),
        scratch_shapes=[pltpu.VMEM(acc_shape, F32)],
        compiler_params=_params(("parallel", "parallel", "arbitrary")),
    )(a, b, *extra)
    return outs


def _tile(n, t):
    t = min(n, t)
    assert n % t == 0, (n, t)
    return t


def _mm_plain(name, a, b, contract, out_dtype, tm=1024, tn=1024, tk=512, extra=(), epilogue=None,
              n_out=1, out_dtypes=None):
    if contract == NN:
        (m, kd), (_, n) = a.shape, b.shape
    elif contract == NT:
        (m, kd), (n, _) = a.shape, b.shape
    else:
        (kd, m), (_, n) = a.shape, b.shape
    tm, tn, tk = _tile(m, tm), _tile(n, tn), _tile(kd, tk)
    if contract == NN:
        a_spec = pl.BlockSpec((tm, tk), lambda i, j, k: (i, k))
        b_spec = pl.BlockSpec((tk, tn), lambda i, j, k: (k, j))
    elif contract == NT:
        a_spec = pl.BlockSpec((tm, tk), lambda i, j, k: (i, k))
        b_spec = pl.BlockSpec((tn, tk), lambda i, j, k: (j, k))
    else:
        a_spec = pl.BlockSpec((tk, tm), lambda i, j, k: (k, i))
        b_spec = pl.BlockSpec((tk, tn), lambda i, j, k: (k, j))
    o_spec = pl.BlockSpec((tm, tn), lambda i, j, k: (i, j))
    out_dtypes = out_dtypes or [out_dtype] * n_out
    outs = _matmul(
        name, a, b, contract=contract, grid=(m // tm, n // tn, kd // tk), a_spec=a_spec, b_spec=b_spec,
        out_specs=[o_spec] * len(out_dtypes), out_shapes=[jax.ShapeDtypeStruct((m, n), dt) for dt in out_dtypes],
        acc_shape=(tm, tn), extra=extra, extra_specs=[o_spec] * len(extra), epilogue=epilogue)
    return outs[0] if len(out_dtypes) == 1 else outs


def _rows_call(name, body, row_in, vec_in, row_out, vec_out, s):
    t = _tile(s, ROW_T)
    in_specs = []
    args = []
    for arr, width, cb in row_in:
        in_specs.append(pl.BlockSpec((t, width), functools.partial(lambda i, cb: (i, cb), cb=cb)))
        args.append(arr)
    for v in vec_in:
        in_specs.append(pl.BlockSpec(v.shape, lambda i: (0, 0)))
        args.append(v)
    out_specs = []
    out_shapes = []
    for width, dt in row_out:
        out_specs.append(pl.BlockSpec((t, width), lambda i: (i, 0)))
        out_shapes.append(jax.ShapeDtypeStruct((s, width), dt))
    for width in vec_out:
        out_specs.append(pl.BlockSpec((1, width), lambda i: (0, 0)))
        out_shapes.append(jax.ShapeDtypeStruct((1, width), F32))
    return _call(body, name=name, grid=(s // t,), in_specs=in_specs, out_specs=out_specs, out_shape=out_shapes,
                 compiler_params=_params(("arbitrary",)))(*args)


def _acc_vec(ref, val):
    _acc_row(ref, jnp.sum(val, axis=0, keepdims=True))


def _acc_row(ref, part):
    @pl.when(pl.program_id(0) == 0)
    def _():
        ref[...] = part

    @pl.when(pl.program_id(0) > 0)
    def _():
        ref[...] += part


def _rms(v):
    return lax.rsqrt(jnp.mean(v * v, axis=-1, keepdims=True) + EPS)


def _norm_bwd(dxn, xn, r):
    return r * (dxn - xn * jnp.mean(dxn * xn, axis=-1, keepdims=True))


def _premix(x, g, scale, shift):
    s = x.shape[0]

    def body(x_ref, g_ref, sc_ref, sh_ref, h_ref):
        xv = x_ref[...]
        h_ref[...] = ((xv * _rms(xv) * g_ref[...]) * (1.0 + sc_ref[...]) + sh_ref[...]).astype(BF16)

    return _rows_call("premix", body, [(x, D_MODEL, 0)], [g, scale, shift], [(D_MODEL, BF16)], [], s)[0]


def _sigmoid(z):
    return 1.0 / (1.0 + jnp.exp(-z))


def _mix_fwd(o_fox, o_gla, pm, g_fox, g_gla):
    s = o_fox.shape[0]

    def body(of_ref, og_ref, gr_ref, gf_ref, gg_ref, mix_ref):
        for h in range(FOX_HEADS):
            sl = slice(h * FOX_HEAD_DIM, (h + 1) * FOX_HEAD_DIM)
            seg = of_ref[:, sl]
            mix_ref[:, sl] = (seg * _rms(seg) * gf_ref[:, sl]).astype(BF16)
        for h in range(GLA_HEADS):
            sl = slice(h * GLA_DV, (h + 1) * GLA_DV)
            seg = og_ref[:, sl]
            gr = gr_ref[:, sl].astype(F32)
            val = (seg * _rms(seg) * gg_ref[:, sl]) * (gr * _sigmoid(gr))
            mix_ref[:, pl.ds(FOX_HEADS * FOX_HEAD_DIM + h * GLA_DV, GLA_DV)] = val.astype(BF16)

    return _rows_call("mix_fwd", body, [(o_fox, 1024, 0), (o_gla, 1024, 0), (pm, 1024, 5)], [g_fox, g_gla],
                      [(D_MODEL, BF16)], [], s)[0]


def _mix_bwd(dmix, o_fox, o_gla, pm, g_fox, g_gla):
    s = o_fox.shape[0]

    def body(dm_ref, of_ref, og_ref, gr_ref, gf_ref, gg_ref, dof_ref, dog_ref, dgr_ref, dgf_ref, dgg_ref):
        dgf = []
        for h in range(FOX_HEADS):
            sl = slice(h * FOX_HEAD_DIM, (h + 1) * FOX_HEAD_DIM)
            seg = of_ref[:, sl]
            r = _rms(seg)
            segn = seg * r
            dout = dm_ref[:, sl]
            dgf.append(jnp.sum(dout * segn, axis=0, keepdims=True))
            dof_ref[:, sl] = _norm_bwd(dout * gf_ref[:, sl], segn, r).astype(BF16)
        dgg = []
        for h in range(GLA_HEADS):
            sl = slice(h * GLA_DV, (h + 1) * GLA_DV)
            seg = og_ref[:, sl]
            r = _rms(seg)
            segn = seg * r
            gl = segn * gg_ref[:, sl]
            gr = gr_ref[:, sl].astype(F32)
            sig = _sigmoid(gr)
            dout = dm_ref[:, pl.ds(FOX_HEADS * FOX_HEAD_DIM + h * GLA_DV, GLA_DV)]
            dgr_ref[:, sl] = (dout * gl * (sig * (1.0 + gr * (1.0 - sig)))).astype(BF16)
            dgl = dout * (gr * sig)
            dgg.append(jnp.sum(dgl * segn, axis=0, keepdims=True))
            dog_ref[:, sl] = _norm_bwd(dgl * gg_ref[:, sl], segn, r).astype(BF16)
        _acc_row(dgf_ref, jnp.concatenate(dgf, axis=1))
        _acc_row(dgg_ref, jnp.concatenate(dgg, axis=1))

    return _rows_call("mix_bwd", body, [(dmix, D_MODEL, 0), (o_fox, 1024, 0), (o_gla, 1024, 0), (pm, 1024, 5)],
                      [g_fox, g_gla], [(1024, BF16), (1024, BF16), (1024, BF16)], [1024, 1024], s)


def _postmix_premlp(x, y, gate_m, g_post_mix, g_pre_mlp, scale_f, shift_f):
    s = x.shape[0]

    def body(x_ref, y_ref, gm_ref, gpm_ref, gpl_ref, sc_ref, sh_ref, x1_ref, h2_ref):
        yv = y_ref[...]
        x1 = x_ref[...] + gm_ref[...] * (yv * _rms(yv) * gpm_ref[...])
        x1_ref[...] = x1
        h2_ref[...] = ((x1 * _rms(x1) * gpl_ref[...]) * (1.0 + sc_ref[...]) + sh_ref[...]).astype(BF16)

    return _rows_call("postmix_premlp", body, [(x, D_MODEL, 0), (y, D_MODEL, 0)],
                      [gate_m, g_post_mix, g_pre_mlp, scale_f, shift_f], [(D_MODEL, F32), (D_MODEL, BF16)], [], s)


def _loss_postmlp_bwd(x1, y2, target, gate_f, g_post_mlp):
    s = x1.shape[0]

    def body(x1_ref, y2_ref, t_ref, gf_ref, g_ref, dx2_ref, dy2_ref, loss_ref, dgate_ref, dg_ref):
        yv = y2_ref[...]
        r = _rms(yv)
        yn = yv * r
        o = yn * g_ref[...]
        e = (x1_ref[...] + gf_ref[...] * o) - t_ref[...]
        part = 0.5 * jnp.sum(jnp.mean(e * e, axis=-1, keepdims=True), axis=0, keepdims=True)
        _acc_vec(loss_ref, jnp.broadcast_to(part, (1, 128)))
        dx2 = e * (1.0 / D_MODEL)
        dx2_ref[...] = dx2
        _acc_vec(dgate_ref, dx2 * o)
        do = dx2 * gf_ref[...]
        _acc_vec(dg_ref, do * yn)
        dy2_ref[...] = _norm_bwd(do * g_ref[...], yn, r).astype(BF16)

    return _rows_call("loss_postmlp_bwd", body, [(x1, D_MODEL, 0), (y2, D_MODEL, 0), (target, D_MODEL, 0)],
                      [gate_f, g_post_mlp], [(D_MODEL, F32), (D_MODEL, BF16)], [128, D_MODEL, D_MODEL], s)


def _premlp_postmix_bwd(dh2, dx2, x1, y, scale_f, g_pre_mlp, gate_m, g_post_mix):
    s = x1.shape[0]

    def body(dh2_ref, dx2_ref, x1_ref, y_ref, sc_ref, gpl_ref, gm_ref, gpm_ref,
             dx1_ref, dy_ref, dsc_ref, dsh_ref, dgpl_ref, dgm_ref, dgpm_ref):
        x1 = x1_ref[...]
        r1 = _rms(x1)
        x1n = x1 * r1
        dh2 = dh2_ref[...]
        _acc_vec(dsc_ref, dh2 * (x1n * gpl_ref[...]))
        _acc_vec(dsh_ref, dh2)
        dn2 = dh2 * (1.0 + sc_ref[...])
        _acc_vec(dgpl_ref, dn2 * x1n)
        dx1 = dx2_ref[...] + _norm_bwd(dn2 * gpl_ref[...], x1n, r1)
        dx1_ref[...] = dx1
        yv = y_ref[...]
        ry = _rms(yv)
        yn = yv * ry
        _acc_vec(dgm_ref, dx1 * (yn * gpm_ref[...]))
        do = dx1 * gm_ref[...]
        _acc_vec(dgpm_ref, do * yn)
        dy_ref[...] = _norm_bwd(do * gpm_ref[...], yn, ry).astype(BF16)

    return _rows_call("premlp_postmix_bwd", body,
                      [(dh2, D_MODEL, 0), (dx2, D_MODEL, 0), (x1, D_MODEL, 0), (y, D_MODEL, 0)],
                      [scale_f, g_pre_mlp, gate_m, g_post_mix], [(D_MODEL, F32), (D_MODEL, BF16)],
                      [D_MODEL] * 5, s)


def _premix_bwd(dh, dx1, x, g_pre_mix, scale_m):
    s = x.shape[0]

    def body(dh_ref, dx1_ref, x_ref, g_ref, sc_ref, gx_ref, dsc_ref, dsh_ref, dg_ref):
        xv = x_ref[...]
        r = _rms(xv)
        xn = xv * r
        dh = dh_ref[...]
        _acc_vec(dsc_ref, dh * (xn * g_ref[...]))
        _acc_vec(dsh_ref, dh)
        dn1 = dh * (1.0 + sc_ref[...])
        _acc_vec(dg_ref, dn1 * xn)
        gx_ref[...] = dx1_ref[...] + _norm_bwd(dn1 * g_ref[...], xn, r)

    return _rows_call("premix_bwd", body, [(dh, D_MODEL, 0), (dx1, D_MODEL, 0), (x, D_MODEL, 0)],
                      [g_pre_mix, scale_m], [(D_MODEL, F32)], [D_MODEL] * 3, s)


def _split3(v):
    hi = v.astype(BF16)
    r1 = v - hi.astype(F32)
    mid = r1.astype(BF16)
    lo = (r1 - mid.astype(F32)).astype(BF16)
    return hi, mid, lo


def _dot_exact01(v, tri, contract=NN, tri_first=False):
    acc = None
    for part in _split3(v):
        lhs, rhs = (tri, part) if tri_first else (part, tri)
        p = lax.dot_general(lhs, rhs, (contract, ((), ())), preferred_element_type=F32)
        acc = p if acc is None else acc + p
    return acc


def _log_sigmoid(z):
    return jnp.minimum(z, 0.0) - jnp.log(1.0 + jnp.exp(-jnp.abs(z)))


def _fox_cum(small, bvec):
    s = small.shape[0]
    t = _tile(s, CUM_T)

    def body(sm_ref, b_ref, out_ref, carry):
        @pl.when(pl.program_id(0) == 0)
        def _():
            carry[...] = jnp.zeros_like(carry)

        lf = _log_sigmoid(sm_ref[...] + b_ref[...])
        lft = lf.T[0:FOX_HEADS, :]
        row = lax.broadcasted_iota(jnp.int32, (t, t), 0)
        col = lax.broadcasted_iota(jnp.int32, (t, t), 1)
        upper = (row <= col).astype(BF16)
        cum = _dot_exact01(lft, upper) + carry[:, 0:1]
        out_ref[...] = cum
        carry[...] = carry[...] + jnp.sum(lft, axis=1, keepdims=True)

    return _call(body, name="fox_cum", grid=(s // t,),
                 in_specs=[pl.BlockSpec((t, W_SMALL), lambda i: (i, 0)), pl.BlockSpec((1, W_SMALL), lambda i: (0, 0))],
                 out_specs=pl.BlockSpec((FOX_HEADS, t), lambda i: (0, i)),
                 out_shape=jax.ShapeDtypeStruct((FOX_HEADS, s), F32),
                 scratch_shapes=[pltpu.VMEM((FOX_HEADS, 128), F32)],
                 compiler_params=_params(("arbitrary",)))(small, bvec)


def _fox_cum_bwd(dc, dcq, small, bvec):
    s = small.shape[0]
    t = _tile(s, CUM_T)
    nb = s // t

    def body(dc_ref, dcq_ref, sm_ref, b_ref, out_ref, db_ref, carry):
        @pl.when(pl.program_id(0) == 0)
        def _():
            carry[...] = jnp.zeros_like(carry)
            db_ref[...] = jnp.zeros_like(db_ref)

        lane = lax.broadcasted_iota(jnp.int32, (t, W_SMALL), 1)
        dcq = jnp.zeros((t, W_SMALL), F32)
        for hh in range(FOX_HEADS):
            dcq = jnp.where(lane == hh, dcq_ref[hh], dcq)
        dcv = dc_ref[...] + dcq.T[0:FOX_HEADS, :]
        row = lax.broadcasted_iota(jnp.int32, (t, t), 0)
        col = lax.broadcasted_iota(jnp.int32, (t, t), 1)
        lower = (row >= col).astype(BF16)
        dlf = _dot_exact01(dcv, lower) + carry[:, 0:1]
        carry[...] = carry[...] + jnp.sum(dcv, axis=1, keepdims=True)
        z = sm_ref[...] + b_ref[...]
        zt = z.T[0:FOX_HEADS, :]
        dff = dlf * _sigmoid(-zt)
        db_ref[...] = db_ref[...] + jnp.sum(dff, axis=1, keepdims=True)
        full = jnp.concatenate([dff, jnp.zeros((W_SMALL - FOX_HEADS, t), F32)], axis=0)
        out_ref[...] = full.T

    return _call(body, name="fox_cum_bwd", grid=(nb,),
                 in_specs=[pl.BlockSpec((FOX_HEADS, t), lambda i: (0, nb - 1 - i)),
                           pl.BlockSpec((FOX_HEADS, t, 1), lambda i: (0, nb - 1 - i, 0)),
                           pl.BlockSpec((t, W_SMALL), lambda i: (nb - 1 - i, 0)),
                           pl.BlockSpec((1, W_SMALL), lambda i: (0, 0))],
                 out_specs=[pl.BlockSpec((t, W_SMALL), lambda i: (nb - 1 - i, 0)),
                            pl.BlockSpec((FOX_HEADS, 128), lambda i: (0, 0))],
                 out_shape=[jax.ShapeDtypeStruct((s, W_SMALL), F32), jax.ShapeDtypeStruct((FOX_HEADS, 128), F32)],
                 scratch_shapes=[pltpu.VMEM((FOX_HEADS, 128), F32)],
                 compiler_params=_params(("arbitrary",)))(dc, dcq, small, bvec)


FOX_SCALE = FOX_HEAD_DIM ** -0.5


def _fox_fwd(pm, crow):
    s = pm.shape[0]
    t = _tile(s, FOX_T)
    nb = s // t

    def body(q_ref, k_ref, v_ref, c_ref, o_ref, lse_ref, m_s, l_s, acc_s):
        i = pl.program_id(1)
        j = pl.program_id(2)

        @pl.when(j == 0)
        def _():
            m_s[...] = jnp.full_like(m_s, NEG)
            l_s[...] = jnp.zeros_like(l_s)
            acc_s[...] = jnp.zeros_like(acc_s)

        @pl.when(j <= i)
        def _():
            sc = lax.dot_general(q_ref[...], k_ref[...], (NT, ((), ())), preferred_element_type=F32)
            sc = sc * FOX_SCALE - c_ref[...]
            row = lax.broadcasted_iota(jnp.int32, (t, t), 0)
            col = lax.broadcasted_iota(jnp.int32, (t, t), 1)
            sc = jnp.where((j < i) | (row >= col), sc, NEG)
            m_prev = m_s[...]
            m_new = jnp.maximum(m_prev, jnp.max(sc, axis=1, keepdims=True))
            alpha = jnp.exp(m_prev - m_new)
            p = jnp.exp(sc - m_new)
            l_s[...] = alpha * l_s[...] + jnp.sum(p, axis=1, keepdims=True)
            p_hi = p.astype(BF16)
            p_lo = (p - p_hi.astype(F32)).astype(BF16)
            pv = jnp.dot(p_hi, v_ref[...], preferred_element_type=F32)
            pv = pv + jnp.dot(p_lo, v_ref[...], preferred_element_type=F32)
            acc_s[...] = alpha * acc_s[...] + pv
            m_s[...] = m_new

        @pl.when(j == i)
        def _():
            o_ref[...] = acc_s[...] / l_s[...]
            lse_ref[...] = m_s[...] + jnp.log(l_s[...])

    return _call(
        body, name="fox_fwd", grid=(FOX_HEADS, nb, nb),
        in_specs=[pl.BlockSpec((t, 128), lambda h, i, j: (i, h)),
                  pl.BlockSpec((t, 128), lambda h, i, j: (jnp.minimum(i, j), FOX_HEADS + h)),
                  pl.BlockSpec((t, 128), lambda h, i, j: (jnp.minimum(i, j), 2 * FOX_HEADS + h)),
                  pl.BlockSpec((None, 1, t), lambda h, i, j: (h, 0, jnp.minimum(i, j)))],
        out_specs=[pl.BlockSpec((t, 128), lambda h, i, j: (i, h)),
                   pl.BlockSpec((None, t, 1), lambda h, i, j: (h, i, 0))],
        out_shape=[jax.ShapeDtypeStruct((s, FOX_HEADS * 128), F32), jax.ShapeDtypeStruct((FOX_HEADS, s, 1), F32)],
        scratch_shapes=[pltpu.VMEM((t, 1), F32), pltpu.VMEM((t, 1), F32), pltpu.VMEM((t, 128), F32)],
        compiler_params=_params(("parallel", "parallel", "arbitrary")),
    )(pm, pm, pm, crow)


def _fox_bwd(pm, crow, o, lse, do):
    s = pm.shape[0]
    t = _tile(s, FOX_T)
    nb = s // t

    def body(q_ref, do_ref, o_ref, lse_ref, k_ref, v_ref, c_ref, dq_ref, dk_ref, dv_ref, dc_ref, dcq_ref,
             dk_acc, dv_acc, dc_acc, delta_s):
        j = pl.program_id(1)
        i = pl.program_id(2)

        @pl.when((j == 0) & (i == 0))
        def _():
            dq_ref[...] = jnp.zeros_like(dq_ref)
            dcq_ref[...] = jnp.zeros_like(dcq_ref)

        @pl.when(i == j)
        def _():
            dk_acc[...] = jnp.zeros_like(dk_acc)
            dv_acc[...] = jnp.zeros_like(dv_acc)
            dc_acc[...] = jnp.zeros_like(dc_acc)

        rows = pl.ds(pl.multiple_of(i * t, t), t)

        @pl.when(j == 0)
        def _():
            delta_s[rows, :] = jnp.sum(do_ref[...].astype(F32) * o_ref[...], axis=1, keepdims=True)

        @pl.when(i >= j)
        def _():
            q = q_ref[...]
            dov = do_ref[...]
            sc = lax.dot_general(q, k_ref[...], (NT, ((), ())), preferred_element_type=F32)
            sc = sc * FOX_SCALE - c_ref[...]
            row = lax.broadcasted_iota(jnp.int32, (t, t), 0)
            col = lax.broadcasted_iota(jnp.int32, (t, t), 1)
            p = jnp.where((i > j) | (row >= col), jnp.exp(sc - lse_ref[...]), 0.0)
            dp = lax.dot_general(dov, v_ref[...], (NT, ((), ())), preferred_element_type=F32)
            ds = p * (dp - delta_s[rows, :])
            dsb = ds.astype(BF16)
            dv_acc[...] += lax.dot_general(p.astype(BF16), dov, (TN, ((), ())), preferred_element_type=F32)
            dk_acc[...] += lax.dot_general(dsb, q, (TN, ((), ())), preferred_element_type=F32)
            dq_ref[rows, :] += jnp.dot(dsb, k_ref[...], preferred_element_type=F32) * FOX_SCALE
            dc_acc[...] -= jnp.sum(ds, axis=0, keepdims=True)
            dcq_ref[rows, :] += jnp.sum(ds, axis=1, keepdims=True)

        @pl.when(i == nb - 1)
        def _():
            dk_ref[...] = dk_acc[...] * FOX_SCALE
            dv_ref[...] = dv_acc[...]
            dc_ref[...] = dc_acc[...]

    qi = lambda h, j, i: (jnp.maximum(i, j), h)
    return _call(
        body, name="fox_bwd", grid=(FOX_HEADS, nb, nb),
        in_specs=[pl.BlockSpec((t, 128), qi), pl.BlockSpec((t, 128), qi), pl.BlockSpec((t, 128), qi),
                  pl.BlockSpec((None, t, 1), lambda h, j, i: (h, jnp.maximum(i, j), 0)),
                  pl.BlockSpec((t, 128), lambda h, j, i: (j, FOX_HEADS + h)),
                  pl.BlockSpec((t, 128), lambda h, j, i: (j, 2 * FOX_HEADS + h)),
                  pl.BlockSpec((None, 1, t), lambda h, j, i: (h, 0, j))],
        out_specs=[pl.BlockSpec((s, 128), lambda h, j, i: (0, h)),
                   pl.BlockSpec((t, 128), lambda h, j, i: (j, h)),
                   pl.BlockSpec((t, 128), lambda h, j, i: (j, h)),
                   pl.BlockSpec((None, 1, t), lambda h, j, i: (h, 0, j)),
                   pl.BlockSpec((None, s, 1), lambda h, j, i: (h, 0, 0))],
        out_shape=[jax.ShapeDtypeStruct((s, 1024), F32), jax.ShapeDtypeStruct((s, 1024), F32),
                   jax.ShapeDtypeStruct((s, 1024), F32), jax.ShapeDtypeStruct((FOX_HEADS, 1, s), F32),
                   jax.ShapeDtypeStruct((FOX_HEADS, s, 1), F32)],
        scratch_shapes=[pltpu.VMEM((t, 128), F32), pltpu.VMEM((t, 128), F32), pltpu.VMEM((1, t), F32),
                        pltpu.VMEM((s, 1), F32)],
        compiler_params=_params(("parallel", "arbitrary", "arbitrary")),
    )(pm, do, o, lse, pm, pm, crow)


GLA_SCALE = GLA_DK ** -0.5
GLA_Q_BLK = 3072 // 128
GLA_K_BLK = 3584 // 128
GLA_V_BLK = 4096 // 256


def _gla_gate(sm, wa_ref, b_ref):
    return jnp.dot(sm.astype(BF16), wa_ref[...], preferred_element_type=F32) + b_ref[...]


def _tri(n, strict):
    row = lax.broadcasted_iota(jnp.int32, (n, n), 0)
    col = lax.broadcasted_iota(jnp.int32, (n, n), 1)
    return ((row > col) if strict else (row >= col)).astype(BF16)


def _gla_fwd(pm, small, wa_pad, b_a2):
    s = pm.shape[0]
    r = _tile(s, GLA_R)
    nc = r // CHUNK

    def body(q_ref, k_ref, v_ref, sm_ref, wa_ref, b_ref, o_ref, st_ref, state):
        @pl.when(pl.program_id(1) == 0)
        def _():
            state[...] = jnp.zeros_like(state)

        tri = _tri(CHUNK, False)
        for c in range(nc):
            rows = slice(c * CHUNK, (c + 1) * CHUNK)
            la = _log_sigmoid(_gla_gate(sm_ref[rows, :], wa_ref, b_ref)) * (1.0 / GLA_TEMP)
            cum = _dot_exact01(la, tri, tri_first=True)
            total = jnp.sum(la, axis=0, keepdims=True)
            kdec = k_ref[rows, :].astype(F32) * jnp.exp(total - cum)
            ut = lax.dot_general(v_ref[rows, :], kdec.astype(BF16), (TN, ((), ())), preferred_element_type=F32)
            new = state[...] * jnp.exp(total) + ut
            state[...] = new
            newb = new.astype(BF16)
            st_ref[c] = newb
            qs = (q_ref[rows, :].astype(F32) * GLA_SCALE).astype(BF16)
            o_ref[rows, :] = lax.dot_general(qs, newb, (NT, ((), ())), preferred_element_type=F32)

    return _call(
        body, name="gla_fwd", grid=(GLA_HEADS, s // r),
        in_specs=[pl.BlockSpec((r, 128), lambda h, i: (i, GLA_Q_BLK + h)),
                  pl.BlockSpec((r, 128), lambda h, i: (i, GLA_K_BLK + h)),
                  pl.BlockSpec((r, 256), lambda h, i: (i, GLA_V_BLK + h)),
                  pl.BlockSpec((r, W_SMALL), lambda h, i: (i, 0)),
                  pl.BlockSpec((W_SMALL, 128), lambda h, i: (0, h)),
                  pl.BlockSpec((1, 128), lambda h, i: (0, h))],
        out_specs=[pl.BlockSpec((r, 256), lambda h, i: (i, h)),
                   pl.BlockSpec((nc, None, GLA_DV, GLA_DK), lambda h, i: (i, h, 0, 0))],
        out_shape=[jax.ShapeDtypeStruct((s, 1024), F32),
                   jax.ShapeDtypeStruct((s // CHUNK, GLA_HEADS, GLA_DV, GLA_DK), BF16)],
        scratch_shapes=[pltpu.VMEM((GLA_DV, GLA_DK), F32)],
        compiler_params=_params(("parallel", "arbitrary")),
    )(pm, pm, pm, small, wa_pad, b_a2)


def _gla_bwd(pm, small, wa_pad, b_a2, states, do):
    s = pm.shape[0]
    r = _tile(s, GLA_R)
    nc = r // CHUNK
    nb = s // r

    def body(q_ref, k_ref, v_ref, sm_ref, wa_ref, b_ref, do_ref, st_ref, prev_ref,
             dq_ref, dk_ref, dv_ref, dza_ref, db_ref, carry):
        step = pl.program_id(1)

        @pl.when(step == 0)
        def _():
            carry[...] = jnp.zeros_like(carry)
            db_ref[...] = jnp.zeros_like(db_ref)

        tri = _tri(CHUNK, False)
        tri_strict = _tri(CHUNK, True)
        db = jnp.zeros((1, 128), F32)
        for c in reversed(range(nc)):
            rows = slice(c * CHUNK, (c + 1) * CHUNK)
            z = _gla_gate(sm_ref[rows, :], wa_ref, b_ref)
            la = _log_sigmoid(z) * (1.0 / GLA_TEMP)
            cum = _dot_exact01(la, tri, tri_first=True)
            total = jnp.sum(la, axis=0, keepdims=True)
            w = jnp.exp(total - cum)
            decay = jnp.exp(total)
            kdec = k_ref[rows, :].astype(F32) * w
            dov = do_ref[rows, :]
            qs = (q_ref[rows, :].astype(F32) * GLA_SCALE).astype(BF16)
            dq_ref[rows, :] = jnp.dot(dov, st_ref[c], preferred_element_type=F32) * GLA_SCALE
            gt = lax.dot_general(dov, qs, (TN, ((), ())), preferred_element_type=F32) + carry[...]
            gtb = gt.astype(BF16)
            dv_ref[rows, :] = lax.dot_general(kdec.astype(BF16), gtb, (NT, ((), ())), preferred_element_type=F32)
            dkdec = jnp.dot(v_ref[rows, :], gtb, preferred_element_type=F32)
            dk_ref[rows, :] = dkdec * w
            e = dkdec * kdec
            if c > 0:
                prev = st_ref[c - 1].astype(F32)
            else:
                prev = jnp.where(step == nb - 1, 0.0, prev_ref[0].astype(F32))
            dtot = jnp.sum(gt * prev, axis=0, keepdims=True) * decay
            dla = dtot + _dot_exact01(e, tri_strict, tri_first=True)
            dza = dla * (1.0 / GLA_TEMP) * _sigmoid(-z)
            dza_ref[rows, :] = dza.astype(BF16)
            db = db + jnp.sum(dza, axis=0, keepdims=True)
            carry[...] = gt * decay
        db_ref[...] += db

    blk = lambda h, i: nb - 1 - i
    return _call(
        body, name="gla_bwd", grid=(GLA_HEADS, nb),
        in_specs=[pl.BlockSpec((r, 128), lambda h, i: (blk(h, i), GLA_Q_BLK + h)),
                  pl.BlockSpec((r, 128), lambda h, i: (blk(h, i), GLA_K_BLK + h)),
                  pl.BlockSpec((r, 256), lambda h, i: (blk(h, i), GLA_V_BLK + h)),
                  pl.BlockSpec((r, W_SMALL), lambda h, i: (blk(h, i), 0)),
                  pl.BlockSpec((W_SMALL, 128), lambda h, i: (0, h)),
                  pl.BlockSpec((1, 128), lambda h, i: (0, h)),
                  pl.BlockSpec((r, 256), lambda h, i: (blk(h, i), h)),
                  pl.BlockSpec((nc, None, GLA_DV, GLA_DK), lambda h, i: (blk(h, i), h, 0, 0)),
                  pl.BlockSpec((1, None, GLA_DV, GLA_DK),
                               lambda h, i: (jnp.maximum(blk(h, i) * nc - 1, 0), h, 0, 0))],
        out_specs=[pl.BlockSpec((r, 128), lambda h, i: (blk(h, i), h)),
                   pl.BlockSpec((r, 128), lambda h, i: (blk(h, i), h)),
                   pl.BlockSpec((r, 256), lambda h, i: (blk(h, i), h)),
                   pl.BlockSpec((r, 128), lambda h, i: (blk(h, i), h)),
                   pl.BlockSpec((1, 128), lambda h, i: (0, h))],
        out_shape=[jax.ShapeDtypeStruct((s, 512), F32), jax.ShapeDtypeStruct((s, 512), F32),
                   jax.ShapeDtypeStruct((s, 1024), F32), jax.ShapeDtypeStruct((s, 512), BF16),
                   jax.ShapeDtypeStruct((1, 512), F32)],
        scratch_shapes=[pltpu.VMEM((GLA_DV, GLA_DK), F32)],
        compiler_params=_params(("parallel", "arbitrary")),
    )(pm, pm, pm, small, wa_pad, b_a2, do, states, states)


def _modulation(c_all, w_ada):
    n = w_ada.shape[1]
    tn = _tile(n, 512)

    def body(c_ref, w_ref, out_ref, ca_ref):
        cv = c_ref[...]
        ca = cv * _sigmoid(cv)
        ca_ref[...] = ca
        out_ref[...] = jnp.dot(ca.astype(BF16), w_ref[...].astype(BF16), preferred_element_type=F32)

    return _call(body, name="modulation", grid=(n // tn,),
                 in_specs=[pl.BlockSpec((N_DEV, D_MODEL), lambda j: (0, 0)),
                           pl.BlockSpec((D_MODEL, tn), lambda j: (0, j))],
                 out_specs=[pl.BlockSpec((N_DEV, tn), lambda j: (0, j)),
                            pl.BlockSpec((N_DEV, D_MODEL), lambda j: (0, 0))],
                 out_shape=[jax.ShapeDtypeStruct((N_DEV, n), F32), jax.ShapeDtypeStruct((N_DEV, D_MODEL), F32)],
                 compiler_params=_params(("arbitrary",)))(c_all, w_ada)


def _adamw_math(w, g, m, v):
    m = ADAM_B1 * m + (1.0 - ADAM_B1) * g
    v = ADAM_B2 * v + (1.0 - ADAM_B2) * (g * g)
    m_hat = m / (1.0 - ADAM_B1 ** ADAM_STEP)
    v_hat = v / (1.0 - ADAM_B2 ** ADAM_STEP)
    delta = -ADAM_LR * (m_hat / (jnp.sqrt(v_hat) + ADAM_EPS) + ADAM_WD * w)
    return delta, m, v


def _adamw_slabs(name, w, slabs, m, v, tr=256):
    rr, cc = w.shape
    tr = _tile(rr, tr)

    def body(w_ref, s_ref, m_ref, v_ref, g_ref, d_ref, nm_ref, nv_ref):
        g = s_ref[0].astype(F32)
        for r in range(1, N_DEV):
            g = g + s_ref[r].astype(F32)
        g_ref[...] = g
        d, nm, nv = _adamw_math(w_ref[...], g, m_ref[...], v_ref[...])
        d_ref[...] = d
        nm_ref[...] = nm
        nv_ref[...] = nv

    spec = pl.BlockSpec((tr, cc), lambda i: (i, 0))
    return _call(body, name=name, grid=(rr // tr,),
                 in_specs=[spec, pl.BlockSpec((N_DEV, tr, cc), lambda i: (0, i, 0)), spec, spec],
                 out_specs=[spec] * 4, out_shape=[jax.ShapeDtypeStruct((rr, cc), F32)] * 4,
                 compiler_params=_params(("parallel",)))(w, slabs, m, v)


def _adamw_ada(w, cat, dm, m, v, tr=256):
    rr, cc = w.shape
    tr = _tile(rr, tr)

    def body(w_ref, ca_ref, dm_ref, m_ref, v_ref, g_ref, d_ref, nm_ref, nv_ref):
        g = ca_ref[:, 0:1] * dm_ref[0:1, :]
        for b in range(1, N_DEV):
            g = g + ca_ref[:, b:b + 1] * dm_ref[b:b + 1, :]
        g_ref[...] = g
        d, nm, nv = _adamw_math(w_ref[...], g, m_ref[...], v_ref[...])
        d_ref[...] = d
        nm_ref[...] = nm
        nv_ref[...] = nv

    spec = pl.BlockSpec((tr, cc), lambda i: (i, 0))
    return _call(body, name="adamw_ada", grid=(rr // tr,),
                 in_specs=[spec, pl.BlockSpec((tr, N_DEV), lambda i: (i, 0)),
                           pl.BlockSpec((N_DEV, cc), lambda i: (0, 0)), spec, spec],
                 out_specs=[spec] * 4, out_shape=[jax.ShapeDtypeStruct((rr, cc), F32)] * 4,
                 compiler_params=_params(("parallel",)))(w, cat, dm, m, v)


def _sum_devices(gathered):
    ln = gathered.shape[-1]

    def body(g_ref, out_ref):
        acc = g_ref[0]
        for r in range(1, N_DEV):
            acc = acc + g_ref[r]
        out_ref[...] = acc

    return _call(body, name="sum_devices",
                 in_specs=[pl.BlockSpec(memory_space=pltpu.VMEM)], out_specs=pl.BlockSpec(memory_space=pltpu.VMEM),
                 out_shape=jax.ShapeDtypeStruct((1, ln), F32))(gathered)


def _adamw_flat(w, g, m, v):
    def body(w_ref, g_ref, m_ref, v_ref, d_ref, nm_ref, nv_ref):
        d, nm, nv = _adamw_math(w_ref[...], g_ref[...], m_ref[...], v_ref[...])
        d_ref[...] = d
        nm_ref[...] = nm
        nv_ref[...] = nv

    vm = pl.BlockSpec(memory_space=pltpu.VMEM)
    return _call(body, name="adamw_small", in_specs=[vm] * 4, out_specs=[vm] * 3,
                 out_shape=[jax.ShapeDtypeStruct(w.shape, F32)] * 3)(w, g, m, v)


def _from_col_shards(g):
    return jnp.transpose(g, (1, 0, 2)).reshape(g.shape[1], N_DEV * g.shape[2])


def _pad_lanes(v, n):
    return jnp.concatenate([v, jnp.zeros(v.shape[:-1] + (n - v.shape[-1],), v.dtype)], axis=-1)


def kernel(x, c, w_ada, b_ada, g_pre_mix, g_post_mix, w_in, b_fgate, w_gla_a2, b_gla_a2, g_fox_out, g_gla_out, w_out, g_pre_mlp, g_post_mlp, w_mlp_in, w_mlp_out, loss_target, m_w_ada, m_b_ada, m_g_pre_mix, m_g_post_mix, m_w_in, m_b_fgate, m_w_gla_a2, m_b_gla_a2, m_g_fox_out, m_g_gla_out, m_w_out, m_g_pre_mlp, m_g_post_mlp, m_w_mlp_in, m_w_mlp_out, v_w_ada, v_b_ada, v_g_pre_mix, v_g_post_mix, v_w_in, v_b_fgate, v_w_gla_a2, v_b_gla_a2, v_g_fox_out, v_g_gla_out, v_w_out, v_g_pre_mlp, v_g_post_mlp, v_w_mlp_in, v_w_mlp_out):
    rank = _my_rank()
    xs = x[0]
    s = xs.shape[0]
    target = loss_target[0]

    c_all, wa2_g, ggla_g, win_g = _all_gather("gather_first", [c, w_gla_a2[0], g_gla_out[0], w_in[0].astype(BF16)])
    rest = [_tied(w_out[0].astype(BF16), c_all), w_mlp_in[0].astype(BF16), w_mlp_out[0].astype(BF16)]
    gs_send, gs_recv, gs_thru, gs_land, gs_token = _exchange_start("gather_rest_start", rest, gather=True)
    w_a2 = _from_col_shards(wa2_g)
    g_gla = _from_col_shards(ggla_g).reshape(1, 1024)
    g_fox = g_fox_out.reshape(1, 1024)
    win_full = _from_col_shards(win_g)
    w_main = jnp.concatenate([win_full[:, :3072], win_full[:, 3080:5128], win_full[:, 5144:6168]], axis=1)
    w_small = _pad_lanes(jnp.concatenate([win_full[:, 3072:3080], win_full[:, 5128:5144]], axis=1), W_SMALL)
    wa_pad =jnp.concatenate([jnp.zeros((8, 512), BF16), w_a2.astype(BF16), jnp.zeros((104, 512), BF16)], axis=0)
    bf_vec = _pad_lanes(b_fgate, W_SMALL)

    mod_part, c_act = _modulation(c_all.reshape(N_DEV, D_MODEL), w_ada[0])
    (mod_g,) = _all_gather("gather_mod", [mod_part])
    mod = lax.dynamic_slice_in_dim(mod_g, rank, 1, axis=1).reshape(1, 6 * D_MODEL) + b_ada
    mod = _tied(mod, gs_token)
    shift_m, scale_m, gate_m, shift_f, scale_f, gate_f = [mod[:, i * D_MODEL:(i + 1) * D_MODEL] for i in range(6)]

    h = _premix(xs, g_pre_mix, scale_m, shift_m)
    pm = _mm_plain("proj_main", h, w_main, NN, BF16)
    small = _mm_plain("proj_small", h, w_small, NN, F32)
    crow = _fox_cum(small, bf_vec).reshape(FOX_HEADS, 1, s)
    o_fox, lse = _fox_fwd(pm, crow)
    o_gla, states = _gla_fwd(pm, small, wa_pad, b_gla_a2)
    mix = _mix_fwd(o_fox, o_gla, pm, g_fox, g_gla)
    wout_g, wmi_g, wmo_g = _exchange_wait("gather_rest_wait", gs_send, gs_recv, gs_thru, gs_land, mix, gather=True)
    w_out_full = wout_g.reshape(D_MODEL, D_MODEL)
    w_mo_full = wmo_g.reshape(D_FF, D_MODEL)
    y = _mm_plain("out_proj", mix, w_out_full, NN, F32)
    x1, h2 = _postmix_premlp(xs, y, gate_m, g_post_mix, g_pre_mlp, scale_f, shift_f)

    tm, tn, tk = _tile(s, 1024), 1024, 512
    nsh = 1024 // tn

    def relu2(acc):
        rl = jnp.maximum(acc, 0.0)
        return rl * rl, rl

    z, a_relu = _matmul(
        "mlp_in", h2, wmi_g, contract=NN, grid=(s // tm, D_FF // tn, D_MODEL // tk),
        a_spec=pl.BlockSpec((tm, tk), lambda i, j, k: (i, k)),
        b_spec=pl.BlockSpec((None, tk, tn), lambda i, j, k: (j // nsh, k, j % nsh)),
        out_specs=[pl.BlockSpec((tm, tn), lambda i, j, k: (i, j))] * 2,
        out_shapes=[jax.ShapeDtypeStruct((s, D_FF), BF16)] * 2, acc_shape=(tm, tn), epilogue=relu2)
    y2 = _mm_plain("mlp_out", z, w_mo_full, NN, F32)

    dx2, dy2, loss_vec, dgate_f, dg_post_mlp = _loss_postmlp_bwd(x1, y2, target, gate_f, g_post_mlp)
    loss = lax.psum(loss_vec[0, 0], ("x", "y", "c"))

    da = _mm_plain("mlp_out_dx", dy2, w_mo_full, NT, BF16, extra=(a_relu,),
                   epilogue=lambda acc, rl: (acc * (2.0 * rl.astype(F32)),))
    dw_mo = _mm_plain("mlp_out_dw", z, dy2, TN, BF16)
    x_mo = _exchange_start("grad_mlp_out_start", [dw_mo.reshape(N_DEV, 1024, D_MODEL)], gather=False)
    da = _tied(da, x_mo[4])
    kb = 1024 // tk
    (dh2,) = _matmul(
        "mlp_in_dx", da, wmi_g, contract=NT, grid=(s // tm, D_MODEL // tn, D_FF // tk),
        a_spec=pl.BlockSpec((tm, tk), lambda i, j, k: (i, k)),
        b_spec=pl.BlockSpec((None, tn, tk), lambda i, j, k: (k // kb, j, k % kb)),
        out_specs=[pl.BlockSpec((tm, tn), lambda i, j, k: (i, j))],
        out_shapes=[jax.ShapeDtypeStruct((s, D_MODEL), F32)], acc_shape=(tm, tn))
    ts = _tile(s, 512)
    (dw_mi,) = _matmul(
        "mlp_in_dw", h2, da, contract=TN, grid=(D_MODEL // 1024, D_FF // tn, s // ts),
        a_spec=pl.BlockSpec((ts, 1024), lambda i, j, k: (k, i)),
        b_spec=pl.BlockSpec((ts, tn), lambda i, j, k: (k, j)),
        out_specs=[pl.BlockSpec((None, 1024, tn), lambda i, j, k: (j // nsh, i, j % nsh))],
        out_shapes=[jax.ShapeDtypeStruct((N_DEV, D_MODEL, 1024), BF16)], acc_shape=(1024, tn))
    x_mi = _exchange_start("grad_mlp_in_start", [dw_mi], gather=False)
    dh2 = _tied(dh2, x_mi[4])

    dx1, dy, dscale_f, dshift_f, dg_pre_mlp, dgate_m, dg_post_mix = _premlp_postmix_bwd(
        dh2, dx2, x1, y, scale_f, g_pre_mlp, gate_m, g_post_mix)

    dmix = _mm_plain("out_proj_dx", dy, w_out_full, NT, F32)
    dw_out = _mm_plain("out_proj_dw", mix, dy, TN, BF16)
    x_out = _exchange_start("grad_out_start", [dw_out.reshape(N_DEV, 256, D_MODEL)], gather=False)
    dmix = _tied(dmix, x_out[4])
    do_fox, do_gla, dgr, dg_fox, dg_gla = _mix_bwd(dmix, o_fox, o_gla, pm, g_fox, g_gla)

    dq, dk, dv, dc, dcq = _fox_bwd(pm, crow, o_fox, lse, do_fox)
    dsmall_f, db_f = _fox_cum_bwd(dc.reshape(FOX_HEADS, s), dcq, small, bf_vec)
    dgq, dgk, dgv, dza, db_a2 = _gla_bwd(pm, small, wa_pad, b_gla_a2, states, do_gla)
    dsmall = _mm_plain("gate_dx", dza, wa_pad, NT, F32, tn=128, extra=(dsmall_f,),
                       epilogue=lambda acc, other: (acc + other,))
    dwa_pad = _mm_plain("gate_dw", small, dza, TN, F32, tm=128, tn=512)

    dpm = jnp.concatenate([dq.astype(BF16), dk.astype(BF16), dv.astype(BF16), dgq.astype(BF16), dgk.astype(BF16),
                           dgv.astype(BF16), dgr], axis=1)
    dw_main = _mm_plain("proj_main_dw", h, dpm, TN, BF16)
    dw_small = _mm_plain("proj_small_dw", h, dsmall, TN, BF16, tn=128)
    dwin_full = jnp.concatenate([dw_main[:, :3072], dw_small[:, 0:8], dw_main[:, 3072:5120], dw_small[:, 8:24],
                                 dw_main[:, 5120:6144]], axis=1)
    dwin_slabs = jnp.transpose(dwin_full.reshape(D_MODEL, N_DEV, 771), (1, 0, 2))
    x_in = _exchange_start("grad_in_start", [dwin_slabs], gather=False)
    dpm = _tied(dpm, x_in[4])
    dh_small = _mm_plain("proj_small_dx", dsmall, w_small, NT, F32, tk=128)
    dh = _mm_plain("proj_main_dx", dpm, w_main, NT, F32, extra=(dh_small,),
                   epilogue=lambda acc, other: (acc + other,))
    grad_x, dscale_m, dshift_m, dg_pre_mix = _premix_bwd(dh, dx1, xs, g_pre_mix, scale_m)

    dmod = jnp.concatenate([dshift_m, dscale_m, dgate_m, dshift_f, dscale_f, dgate_f], axis=1)
    flat = jnp.concatenate(
        [dmod, dg_pre_mix, dg_post_mix, dg_fox, dg_pre_mlp, dg_post_mlp, db_a2,
         dwa_pad[8:24, :].reshape(1, GLA_RANK * 512), dg_gla, _pad_lanes(db_f[:, 0].reshape(1, FOX_HEADS), 128)],
        axis=1)
    (flat_g,) = _all_gather("gather_small_grads", [flat])
    tot = _sum_devices(flat_g)

    (r_mo,) = _exchange_wait("grad_mlp_out_wait", *x_mo[:4], grad_x, gather=False)
    g_mo, d_mo, nm_mo, nv_mo = _adamw_slabs("adamw_w_mlp_out", w_mlp_out[0], r_mo, m_w_mlp_out[0], v_w_mlp_out[0])
    (r_mi,) = _exchange_wait("grad_mlp_in_wait", *x_mi[:4], g_mo, gather=False)
    g_mi, d_mi, nm_mi, nv_mi = _adamw_slabs("adamw_w_mlp_in", w_mlp_in[0], r_mi, m_w_mlp_in[0], v_w_mlp_in[0])
    (r_out,) = _exchange_wait("grad_out_wait", *x_out[:4], g_mi, gather=False)
    g_out, d_out, nm_out, nv_out = _adamw_slabs("adamw_w_out", w_out[0], r_out, m_w_out[0], v_w_out[0])

    dm_cols = lax.dynamic_slice_in_dim(flat_g[:, 0, :6 * D_MODEL], rank * 1536, 1536, axis=1)
    g_ada, d_ada, nm_ada, nv_ada = _adamw_ada(w_ada[0], c_act.T, dm_cols, m_w_ada[0], v_w_ada[0])
    (r_in,) = _exchange_wait("grad_in_wait", *x_in[:4], g_ada, gather=False)
    g_in, d_in, nm_in, nv_in = _adamw_slabs("adamw_w_in", w_in[0], r_in, m_w_in[0], v_w_in[0])

    o = 0
    seg = {}
    for name, n in (("b_ada", 12288), ("g_pre_mix", 2048), ("g_post_mix", 2048), ("g_fox_out", 1024),
                    ("g_pre_mlp", 2048), ("g_post_mlp", 2048), ("b_gla_a2", 512), ("w_gla_a2", 8192),
                    ("g_gla_out", 1024), ("b_fgate", 128)):
        seg[name] = tot[:, o:o + n]
        o += n
    g_wa2 = lax.dynamic_slice_in_dim(seg["w_gla_a2"].reshape(GLA_RANK, 512), rank * 64, 64, axis=1)
    g_ggla = lax.dynamic_slice_in_dim(seg["g_gla_out"].reshape(GLA_HEADS, GLA_DV), rank * 32, 32, axis=1)
    small_names = ["b_ada", "g_pre_mix", "g_post_mix", "g_fox_out", "g_pre_mlp", "g_post_mlp", "b_gla_a2",
                   "w_gla_a2", "g_gla_out", "b_fgate"]
    small_grads = {**seg, "w_gla_a2": g_wa2.reshape(1, 1024), "g_gla_out": g_ggla.reshape(1, 128)}
    weights = dict(b_ada=b_ada, g_pre_mix=g_pre_mix, g_post_mix=g_post_mix, g_fox_out=g_fox_out,
                   g_pre_mlp=g_pre_mlp, g_post_mlp=g_post_mlp, b_gla_a2=b_gla_a2, w_gla_a2=w_gla_a2,
                   g_gla_out=g_gla_out, b_fgate=b_fgate)
    moms = dict(b_ada=m_b_ada, g_pre_mix=m_g_pre_mix, g_post_mix=m_g_post_mix, g_fox_out=m_g_fox_out,
                g_pre_mlp=m_g_pre_mlp, g_post_mlp=m_g_post_mlp, b_gla_a2=m_b_gla_a2, w_gla_a2=m_w_gla_a2,
                g_gla_out=m_g_gla_out, b_fgate=m_b_fgate)
    vels = dict(b_ada=v_b_ada, g_pre_mix=v_g_pre_mix, g_post_mix=v_g_post_mix, g_fox_out=v_g_fox_out,
                g_pre_mlp=v_g_pre_mlp, g_post_mlp=v_g_post_mlp, b_gla_a2=v_b_gla_a2, w_gla_a2=v_w_gla_a2,
                g_gla_out=v_g_gla_out, b_fgate=v_b_fgate)

    def flatten(d, fill):
        parts = []
        for nm in small_names:
            p = d[nm].reshape(1, -1)
            if nm == "b_fgate":
                p = jnp.concatenate([p[:, :FOX_HEADS], jnp.full((1, 128 - FOX_HEADS), fill, F32)], axis=1)
            parts.append(p)
        return jnp.concatenate(parts, axis=1).reshape(-1, 128)

    fw, fg, fm, fv = flatten(weights, 0.0), flatten(small_grads, 0.0), flatten(moms, 0.0), flatten(vels, 1.0)
    fd, fnm, fnv = _adamw_flat(fw, fg, fm, fv)

    def unflatten(fl):
        fl = fl.reshape(1, -1)
        out = {}
        o = 0
        for nm in small_names:
            n = 128 if nm == "b_fgate" else weights[nm].size
            piece = fl[:, o:o + n]
            if nm == "b_fgate":
                piece = piece[:, :FOX_HEADS]
            out[nm] = piece.reshape(weights[nm].shape)
            o += n
        return out

    sg, sd, snm, snv = unflatten(fg), unflatten(fd), unflatten(fnm), unflatten(fnv)

    big = dict(w_ada=(g_ada, d_ada, nm_ada, nv_ada), w_in=(g_in, d_in, nm_in, nv_in),
               w_out=(g_out, d_out, nm_out, nv_out), w_mlp_in=(g_mi, d_mi, nm_mi, nv_mi),
               w_mlp_out=(g_mo, d_mo, nm_mo, nv_mo))
    order = ["w_ada", "b_ada", "g_pre_mix", "g_post_mix", "w_in", "b_fgate", "w_gla_a2", "b_gla_a2", "g_fox_out",
             "g_gla_out", "w_out", "g_pre_mlp", "g_post_mlp", "w_mlp_in", "w_mlp_out"]

    def pick(nm, idx):
        if nm in big:
            return big[nm][idx][None]
        return (sg, sd, snm, snv)[idx][nm]

    grads = [pick(nm, 0) for nm in order]
    deltas = [pick(nm, 1) for nm in order]
    new_m = [pick(nm, 2) for nm in order]
    new_v = [pick(nm, 3) for nm in order]
    return (loss, grad_x[None], *grads, *deltas, *new_m, *new_v)
```

```python
import functools

import numpy as np
import jax
import jax.numpy as jnp
from jax import lax
from jax.experimental import pallas as pl
from jax.experimental.pallas import tpu as pltpu

F32 = jnp.float32
BF16 = jnp.bfloat16
MESH = pl.DeviceIdType.MESH
N_DEV = 8

D_MODEL = 2048
FOX_HEADS = 8
FOX_HEAD_DIM = 128
GLA_HEADS = 4
GLA_DK = 128
GLA_DV = 256
GLA_RANK = 16
GLA_TEMP = 16.0
CHUNK = 64
D_FF = 8192
W_MAIN = 6144
W_SMALL = 128
EPS = 1e-6
NEG = float(np.finfo(np.float32).min)

ADAM_LR = 0.001
ADAM_B1 = 0.9
ADAM_B2 = 0.999
ADAM_EPS = 1e-08
ADAM_WD = 0.01
ADAM_STEP = 10

ROW_T = 256
FOX_T = 512
GLA_R = 512
CUM_T = 256
VMEM_LIMIT = 56 * 1024 * 1024


def _call(body, deps=(), **kw):
    if not deps:
        return pl.pallas_call(body, **kw)
    n_in, n_dep = len(kw["in_specs"]), len(deps)

    def with_deps(*refs):
        return body(*refs[:n_in], *refs[n_in + n_dep:])

    kw["in_specs"] = [*kw["in_specs"], *[pl.BlockSpec(memory_space=pl.ANY)] * n_dep]
    call = pl.pallas_call(with_deps, **kw)
    return lambda *args: call(*args, *deps)


def _params(sem=None):
    return pltpu.CompilerParams(dimension_semantics=sem, vmem_limit_bytes=VMEM_LIMIT)


def _my_pos():
    return lax.axis_index("x"), lax.axis_index("y"), lax.axis_index("c")


def _my_rank():
    x, y, c = _my_pos()
    return 4 * x + 2 * y + c


def _all_gather(name, arrays):
    n = len(arrays)

    def body(*refs):
        ins = refs[:n]
        outs = refs[n:2 * n]
        send_sems, recv_sems, local_sems = refs[2 * n:]
        x, y, c = _my_pos()
        me, sibling = (x, y, c), (x, y, 1 - c)
        chips = [(1 - x, y), (x, 1 - y), (1 - x, 1 - y)]

        def slot(a, px, py, pc):
            return outs[a].at[4 * px + 2 * py + pc]

        def copy(a, k, block, to, src=None):
            return pltpu.make_async_remote_copy(
                src_ref=slot(a, *block) if src is None else src, dst_ref=slot(a, *block),
                send_sem=send_sems.at[a, k], recv_sem=recv_sems.at[a, k],
                device_id=to, device_id_type=MESH)

        started = []
        for a in range(n):
            mine = pltpu.make_async_copy(ins[a], slot(a, *me), local_sems.at[a])
            mine.start()
            started.append(mine)
        first = []
        for a in range(n):
            first.append(copy(a, 0, me, sibling, src=ins[a]))
            first += [copy(a, 1 + j, me, (*chip, c), src=ins[a]) for j, chip in enumerate(chips)]
        for cp in first:
            cp.start()
        passed = []
        for j, chip in enumerate(chips):
            for a in range(n):
                copy(a, 1 + j, (*chip, c), me).wait_recv()
                fwd = copy(a, 4 + j, (*chip, c), sibling)
                fwd.start()
                passed.append(fwd)
        for a in range(n):
            copy(a, 0, sibling, me).wait_recv()
            for j, chip in enumerate(chips):
                copy(a, 4 + j, (*chip, 1 - c), me).wait_recv()
        for cp in first + passed:
            cp.wait_send()
        for mine in started:
            mine.wait()

    hbm = pl.BlockSpec(memory_space=pltpu.HBM)
    return _call(
        body, name=name,
        out_shape=[jax.ShapeDtypeStruct((N_DEV,) + a.shape, a.dtype) for a in arrays],
        in_specs=[hbm] * n, out_specs=[hbm] * n,
        scratch_shapes=[pltpu.SemaphoreType.DMA((n, 7)), pltpu.SemaphoreType.DMA((n, 7)),
                        pltpu.SemaphoreType.DMA((n,))],
    )(*arrays)


def _all_to_all(name, arrays):
    n = len(arrays)

    def body(*refs):
        ins = refs[:n]
        outs = refs[n:2 * n]
        send_sems, recv_sems, local_sems = refs[2 * n:]
        x, y, c = _my_pos()
        me = 4 * x + 2 * y + c
        flips = [(kx, ky, kc) for kx in (0, 1) for ky in (0, 1) for kc in (0, 1)][1:]

        def peer(k):
            kx, ky, kc = flips[k]
            return (1 - x if kx else x), (1 - y if ky else y), (1 - c if kc else c)

        local = []
        for a in range(n):
            cp = pltpu.make_async_copy(ins[a].at[me], outs[a].at[me], local_sems.at[a])
            cp.start()
            local.append(cp)
        sends = []
        for k in range(7):
            px, py, pc = peer(k)
            pr = 4 * px + 2 * py + pc
            for a in range(n):
                cp = pltpu.make_async_remote_copy(
                    src_ref=ins[a].at[pr], dst_ref=outs[a].at[me],
                    send_sem=send_sems.at[a, k], recv_sem=recv_sems.at[a, k],
                    device_id=(px, py, pc), device_id_type=MESH)
                cp.start()
                sends.append(cp)
        for k in range(7):
            px, py, pc = peer(k)
            pr = 4 * px + 2 * py + pc
            for a in range(n):
                pltpu.make_async_remote_copy(
                    src_ref=ins[a].at[pr], dst_ref=outs[a].at[pr],
                    send_sem=send_sems.at[a, k], recv_sem=recv_sems.at[a, k],
                    device_id=(px, py, pc), device_id_type=MESH).wait_recv()
        for cp in sends:
            cp.wait_send()
        for cp in local:
            cp.wait()

    hbm = pl.BlockSpec(memory_space=pltpu.HBM)
    return _call(
        body, name=name,
        out_shape=[jax.ShapeDtypeStruct(a.shape, a.dtype) for a in arrays],
        in_specs=[hbm] * n, out_specs=[hbm] * n,
        scratch_shapes=[pltpu.SemaphoreType.DMA((n, 7)), pltpu.SemaphoreType.DMA((n, 7)),
                        pltpu.SemaphoreType.DMA((n,))],
    )(*arrays)


_HBM = pl.BlockSpec(memory_space=pltpu.HBM)
_SEM = pl.BlockSpec(memory_space=pltpu.SEMAPHORE)
_FLIPS = [(kx, ky, kc) for kx in (0, 1) for ky in (0, 1) for kc in (0, 1)][1:]


def _peers():
    x, y, c = _my_pos()
    out = []
    for kx, ky, kc in _FLIPS:
        px, py, pc = (1 - x if kx else x), (1 - y if ky else y), (1 - c if kc else c)
        out.append(((px, py, pc), 4 * px + 2 * py + pc))
    return out


def _exchange_copy(gather, ins, lands, send_sems, recv_sems, a, k, peer, peer_rank, me):
    return pltpu.make_async_remote_copy(
        src_ref=ins[a] if gather else ins[a].at[peer_rank],
        dst_ref=lands[a].at[me],
        send_sem=send_sems[a].at[k], recv_sem=recv_sems[a].at[k],
        device_id=peer, device_id_type=MESH)


def _exchange_start(name, arrays, gather, after=()):
    n = len(arrays)
    n_after = len(after)
    land_shapes = [((N_DEV,) + a.shape) if gather else a.shape for a in arrays]

    def body(*refs):
        ins, lands = refs[:n], refs[n:2 * n]
        refs = refs[n_after:]
        send_sems, recv_sems = refs[2 * n:3 * n], refs[3 * n:4 * n]
        token = refs[-1]
        me = _my_rank()
        for a in range(n):
            for k, (peer, peer_rank) in enumerate(_peers()):
                _exchange_copy(gather, ins, lands, send_sems, recv_sems, a, k, peer, peer_rank, me).start()
        token[...] = jnp.zeros_like(token)

    sems = [pltpu.SemaphoreType.DMA((7,))] * (2 * n)
    outs = pl.pallas_call(
        body, name=name,
        out_shape=(*sems, *[pltpu.HBM(a.shape, a.dtype) for a in arrays],
                   *[pltpu.HBM(ls, a.dtype) for ls, a in zip(land_shapes, arrays)],
                   jax.ShapeDtypeStruct((8, 128), F32)),
        in_specs=[*[_HBM] * (2 * n), *[pl.BlockSpec(memory_space=pl.ANY)] * n_after],
        out_specs=(*[_SEM] * (2 * n), *[_HBM] * (2 * n), pl.BlockSpec(memory_space=pltpu.VMEM)),
        input_output_aliases={i: 2 * n + i for i in range(2 * n)},
        compiler_params=pltpu.CompilerParams(has_side_effects=pltpu.SideEffectType.DATAFLOW_SIDE_EFFECTING),
    )(*[pltpu.with_memory_space_constraint(a, pltpu.HBM) for a in arrays],
      *[pltpu.with_memory_space_constraint(lax.empty(ls, a.dtype), pltpu.HBM) for ls, a in zip(land_shapes, arrays)],
      *after)
    return outs[:n], outs[n:2 * n], outs[2 * n:3 * n], outs[3 * n:4 * n], outs[-1]


def _exchange_wait(name, send_sems, recv_sems, thru, lands, after, gather):
    n = len(thru)

    def body(*refs):
        ins, lnd = refs[:n], refs[n:2 * n]
        ssem, rsem = refs[2 * n:3 * n], refs[3 * n:4 * n]
        local_sems = refs[-1]
        me = _my_rank()
        local = []
        for a in range(n):
            cp = pltpu.make_async_copy(ins[a] if gather else ins[a].at[me], lnd[a].at[me], local_sems.at[a])
            cp.start()
            local.append(cp)
        for a in range(n):
            for k, (peer, peer_rank) in enumerate(_peers()):
                cp = _exchange_copy(gather, ins, lnd, ssem, rsem, a, k, peer, peer_rank, peer_rank)
                cp.wait_send()
                cp.wait_recv()
        for cp in local:
            cp.wait()

    outs = pl.pallas_call(
        body, name=name,
        out_shape=(*[pltpu.HBM(t.shape, t.dtype) for t in thru], *[pltpu.HBM(ld.shape, ld.dtype) for ld in lands]),
        in_specs=[*[_HBM] * (2 * n), *[_SEM] * (2 * n), pl.BlockSpec(memory_space=pl.ANY)],
        out_specs=tuple([_HBM] * (2 * n)),
        input_output_aliases={i: i for i in range(2 * n)},
        scratch_shapes=[pltpu.SemaphoreType.DMA((n,))],
        compiler_params=pltpu.CompilerParams(has_side_effects=pltpu.SideEffectType.DATAFLOW_SIDE_EFFECTING),
    )(*thru, *lands, *send_sems, *recv_sems, after)
    return outs[n:]


NN = ((1,), (0,))
NT = ((1,), (1,))
TN = ((0,), (0,))


def _matmul(name, a, b, *, contract, grid, a_spec, b_spec, out_specs, out_shapes, acc_shape,
            extra=(), extra_specs=(), epilogue=None, deps=()):
    nk = grid[2]
    n_extra = len(extra)
    n_out = len(out_shapes)

    def body(*refs):
        a_ref, b_ref = refs[0], refs[1]
        extra_refs = refs[2:2 + n_extra]
        out_refs = refs[2 + n_extra:2 + n_extra + n_out]
        acc_ref = refs[-1]
        k = pl.program_id(2)
        prod = lax.dot_general(a_ref[...].astype(BF16), b_ref[...].astype(BF16), (contract, ((), ())),
                               preferred_element_type=F32)

        @pl.when(k == 0)
        def _():
            acc_ref[...] = prod

        @pl.when(k > 0)
        def _():
            acc_ref[...] += prod

        @pl.when(k == nk - 1)
        def _():
            acc = acc_ref[...]
            if epilogue is None:
                res = (acc,)
            else:
                res = epilogue(acc, *[r[...] for r in extra_refs])
            for o_ref, val in zip(out_refs, res):
                o_ref[...] = val.astype(o_ref.dtype)

    outs = _call(
        body, deps=deps, name=name, grid=grid,
        in_specs=[a_spec, b_spec, *extra_specs], out_specs=list(out_specs), out_shape=list(out_shapes),
        scratch_shapes=[pltpu.VMEM(acc_shape, F32)],
        compiler_params=_params(("parallel", "parallel", "arbitrary")),
    )(a, b, *extra)
    return outs


def _tile(n, t):
    t = min(n, t)
    assert n % t == 0, (n, t)
    return t


def _mm_plain(name, a, b, contract, out_dtype, tm=1024, tn=1024, tk=512, extra=(), epilogue=None,
              n_out=1, out_dtypes=None, deps=()):
    if contract == NN:
        (m, kd), (_, n) = a.shape, b.shape
    elif contract == NT:
        (m, kd), (n, _) = a.shape, b.shape
    else:
        (kd, m), (_, n) = a.shape, b.shape
    tm, tn, tk = _tile(m, tm), _tile(n, tn), _tile(kd, tk)
    if contract == NN:
        a_spec = pl.BlockSpec((tm, tk), lambda i, j, k: (i, k))
        b_spec = pl.BlockSpec((tk, tn), lambda i, j, k: (k, j))
    elif contract == NT:
        a_spec = pl.BlockSpec((tm, tk), lambda i, j, k: (i, k))
        b_spec = pl.BlockSpec((tn, tk), lambda i, j, k: (j, k))
    else:
        a_spec = pl.BlockSpec((tk, tm), lambda i, j, k: (k, i))
        b_spec = pl.BlockSpec((tk, tn), lambda i, j, k: (k, j))
    o_spec = pl.BlockSpec((tm, tn), lambda i, j, k: (i, j))
    out_dtypes = out_dtypes or [out_dtype] * n_out
    outs = _matmul(
        name, a, b, contract=contract, grid=(m // tm, n // tn, kd // tk), a_spec=a_spec, b_spec=b_spec,
        out_specs=[o_spec] * len(out_dtypes), out_shapes=[jax.ShapeDtypeStruct((m, n), dt) for dt in out_dtypes],
        acc_shape=(tm, tn), extra=extra, extra_specs=[o_spec] * len(extra), epilogue=epilogue, deps=deps)
    return outs[0] if len(out_dtypes) == 1 else outs


def _rows_call(name, body, row_in, vec_in, row_out, vec_out, s, deps=()):
    t = _tile(s, ROW_T)
    in_specs = []
    args = []
    for arr, width, cb in row_in:
        in_specs.append(pl.BlockSpec((t, width), functools.partial(lambda i, cb: (i, cb), cb=cb)))
        args.append(arr)
    for v in vec_in:
        in_specs.append(pl.BlockSpec(v.shape, lambda i: (0, 0)))
        args.append(v)
    out_specs = []
    out_shapes = []
    for width, dt in row_out:
        out_specs.append(pl.BlockSpec((t, width), lambda i: (i, 0)))
        out_shapes.append(jax.ShapeDtypeStruct((s, width), dt))
    for width in vec_out:
        out_specs.append(pl.BlockSpec((1, width), lambda i: (0, 0)))
        out_shapes.append(jax.ShapeDtypeStruct((1, width), F32))
    return _call(body, deps=deps, name=name, grid=(s // t,), in_specs=in_specs, out_specs=out_specs,
                 out_shape=out_shapes, compiler_params=_params(("arbitrary",)))(*args)


def _acc_vec(ref, val):
    _acc_row(ref, jnp.sum(val, axis=0, keepdims=True))


def _acc_row(ref, part):
    @pl.when(pl.program_id(0) == 0)
    def _():
        ref[...] = part

    @pl.when(pl.program_id(0) > 0)
    def _():
        ref[...] += part


def _rms(v):
    return lax.rsqrt(jnp.mean(v * v, axis=-1, keepdims=True) + EPS)


def _norm_bwd(dxn, xn, r):
    return r * (dxn - xn * jnp.mean(dxn * xn, axis=-1, keepdims=True))


def _premix(x, g, scale, shift, deps=()):
    s = x.shape[0]

    def body(x_ref, g_ref, sc_ref, sh_ref, h_ref):
        xv = x_ref[...]
        h_ref[...] = ((xv * _rms(xv) * g_ref[...]) * (1.0 + sc_ref[...]) + sh_ref[...]).astype(BF16)

    return _rows_call("premix", body, [(x, D_MODEL, 0)], [g, scale, shift], [(D_MODEL, BF16)], [], s, deps)[0]


def _sigmoid(z):
    return 1.0 / (1.0 + jnp.exp(-z))


def _mix_fwd(o_fox, o_gla, pm, g_fox, g_gla):
    s = o_fox.shape[0]

    def body(of_ref, og_ref, gr_ref, gf_ref, gg_ref, mix_ref):
        for h in range(FOX_HEADS):
            sl = slice(h * FOX_HEAD_DIM, (h + 1) * FOX_HEAD_DIM)
            seg = of_ref[:, sl]
            mix_ref[:, sl] = (seg * _rms(seg) * gf_ref[:, sl]).astype(BF16)
        for h in range(GLA_HEADS):
            sl = slice(h * GLA_DV, (h + 1) * GLA_DV)
            seg = og_ref[:, sl]
            gr = gr_ref[:, sl].astype(F32)
            val = (seg * _rms(seg) * gg_ref[:, sl]) * (gr * _sigmoid(gr))
            mix_ref[:, pl.ds(FOX_HEADS * FOX_HEAD_DIM + h * GLA_DV, GLA_DV)] = val.astype(BF16)

    return _rows_call("mix_fwd", body, [(o_fox, 1024, 0), (o_gla, 1024, 0), (pm, 1024, 5)], [g_fox, g_gla],
                      [(D_MODEL, BF16)], [], s)[0]


def _mix_bwd(dmix, o_fox, o_gla, pm, g_fox, g_gla, deps=()):
    s = o_fox.shape[0]

    def body(dm_ref, of_ref, og_ref, gr_ref, gf_ref, gg_ref, dof_ref, dog_ref, dgr_ref, dgf_ref, dgg_ref):
        dgf = []
        for h in range(FOX_HEADS):
            sl = slice(h * FOX_HEAD_DIM, (h + 1) * FOX_HEAD_DIM)
            seg = of_ref[:, sl]
            r = _rms(seg)
            segn = seg * r
            dout = dm_ref[:, sl]
            dgf.append(jnp.sum(dout * segn, axis=0, keepdims=True))
            dof_ref[:, sl] = _norm_bwd(dout * gf_ref[:, sl], segn, r).astype(BF16)
        dgg = []
        for h in range(GLA_HEADS):
            sl = slice(h * GLA_DV, (h + 1) * GLA_DV)
            seg = og_ref[:, sl]
            r = _rms(seg)
            segn = seg * r
            gl = segn * gg_ref[:, sl]
            gr = gr_ref[:, sl].astype(F32)
            sig = _sigmoid(gr)
            dout = dm_ref[:, pl.ds(FOX_HEADS * FOX_HEAD_DIM + h * GLA_DV, GLA_DV)]
            dgr_ref[:, sl] = (dout * gl * (sig * (1.0 + gr * (1.0 - sig)))).astype(BF16)
            dgl = dout * (gr * sig)
            dgg.append(jnp.sum(dgl * segn, axis=0, keepdims=True))
            dog_ref[:, sl] = _norm_bwd(dgl * gg_ref[:, sl], segn, r).astype(BF16)
        _acc_row(dgf_ref, jnp.concatenate(dgf, axis=1))
        _acc_row(dgg_ref, jnp.concatenate(dgg, axis=1))

    return _rows_call("mix_bwd", body, [(dmix, D_MODEL, 0), (o_fox, 1024, 0), (o_gla, 1024, 0), (pm, 1024, 5)],
                      [g_fox, g_gla], [(1024, BF16), (1024, BF16), (1024, BF16)], [1024, 1024], s, deps)


def _postmix_premlp(x, y, gate_m, g_post_mix, g_pre_mlp, scale_f, shift_f):
    s = x.shape[0]

    def body(x_ref, y_ref, gm_ref, gpm_ref, gpl_ref, sc_ref, sh_ref, x1_ref, h2_ref):
        yv = y_ref[...]
        x1 = x_ref[...] + gm_ref[...] * (yv * _rms(yv) * gpm_ref[...])
        x1_ref[...] = x1
        h2_ref[...] = ((x1 * _rms(x1) * gpl_ref[...]) * (1.0 + sc_ref[...]) + sh_ref[...]).astype(BF16)

    return _rows_call("postmix_premlp", body, [(x, D_MODEL, 0), (y, D_MODEL, 0)],
                      [gate_m, g_post_mix, g_pre_mlp, scale_f, shift_f], [(D_MODEL, F32), (D_MODEL, BF16)], [], s)


def _loss_postmlp_bwd(x1, y2, target, gate_f, g_post_mlp):
    s = x1.shape[0]

    def body(x1_ref, y2_ref, t_ref, gf_ref, g_ref, dx2_ref, dy2_ref, loss_ref, dgate_ref, dg_ref):
        yv = y2_ref[...]
        r = _rms(yv)
        yn = yv * r
        o = yn * g_ref[...]
        e = (x1_ref[...] + gf_ref[...] * o) - t_ref[...]
        part = 0.5 * jnp.sum(jnp.mean(e * e, axis=-1, keepdims=True), axis=0, keepdims=True)
        _acc_vec(loss_ref, jnp.broadcast_to(part, (1, 128)))
        dx2 = e * (1.0 / D_MODEL)
        dx2_ref[...] = dx2
        _acc_vec(dgate_ref, dx2 * o)
        do = dx2 * gf_ref[...]
        _acc_vec(dg_ref, do * yn)
        dy2_ref[...] = _norm_bwd(do * g_ref[...], yn, r).astype(BF16)

    return _rows_call("loss_postmlp_bwd", body, [(x1, D_MODEL, 0), (y2, D_MODEL, 0), (target, D_MODEL, 0)],
                      [gate_f, g_post_mlp], [(D_MODEL, F32), (D_MODEL, BF16)], [128, D_MODEL, D_MODEL], s)


def _premlp_postmix_bwd(dh2, dx2, x1, y, scale_f, g_pre_mlp, gate_m, g_post_mix, deps=()):
    s = x1.shape[0]

    def body(dh2_ref, dx2_ref, x1_ref, y_ref, sc_ref, gpl_ref, gm_ref, gpm_ref,
             dx1_ref, dy_ref, dsc_ref, dsh_ref, dgpl_ref, dgm_ref, dgpm_ref):
        x1 = x1_ref[...]
        r1 = _rms(x1)
        x1n = x1 * r1
        dh2 = dh2_ref[...]
        _acc_vec(dsc_ref, dh2 * (x1n * gpl_ref[...]))
        _acc_vec(dsh_ref, dh2)
        dn2 = dh2 * (1.0 + sc_ref[...])
        _acc_vec(dgpl_ref, dn2 * x1n)
        dx1 = dx2_ref[...] + _norm_bwd(dn2 * gpl_ref[...], x1n, r1)
        dx1_ref[...] = dx1
        yv = y_ref[...]
        ry = _rms(yv)
        yn = yv * ry
        _acc_vec(dgm_ref, dx1 * (yn * gpm_ref[...]))
        do = dx1 * gm_ref[...]
        _acc_vec(dgpm_ref, do * yn)
        dy_ref[...] = _norm_bwd(do * gpm_ref[...], yn, ry).astype(BF16)

    return _rows_call("premlp_postmix_bwd", body,
                      [(dh2, D_MODEL, 0), (dx2, D_MODEL, 0), (x1, D_MODEL, 0), (y, D_MODEL, 0)],
                      [scale_f, g_pre_mlp, gate_m, g_post_mix], [(D_MODEL, F32), (D_MODEL, BF16)],
                      [D_MODEL] * 5, s, deps)


def _premix_bwd(dh, dx1, x, g_pre_mix, scale_m):
    s = x.shape[0]

    def body(dh_ref, dx1_ref, x_ref, g_ref, sc_ref, gx_ref, dsc_ref, dsh_ref, dg_ref):
        xv = x_ref[...]
        r = _rms(xv)
        xn = xv * r
        dh = dh_ref[...]
        _acc_vec(dsc_ref, dh * (xn * g_ref[...]))
        _acc_vec(dsh_ref, dh)
        dn1 = dh * (1.0 + sc_ref[...])
        _acc_vec(dg_ref, dn1 * xn)
        gx_ref[...] = dx1_ref[...] + _norm_bwd(dn1 * g_ref[...], xn, r)

    return _rows_call("premix_bwd", body, [(dh, D_MODEL, 0), (dx1, D_MODEL, 0), (x, D_MODEL, 0)],
                      [g_pre_mix, scale_m], [(D_MODEL, F32)], [D_MODEL] * 3, s)


def _split3(v):
    hi = v.astype(BF16)
    r1 = v - hi.astype(F32)
    mid = r1.astype(BF16)
    lo = (r1 - mid.astype(F32)).astype(BF16)
    return hi, mid, lo


def _dot_exact01(v, tri, contract=NN, tri_first=False):
    acc = None
    for part in _split3(v):
        lhs, rhs = (tri, part) if tri_first else (part, tri)
        p = lax.dot_general(lhs, rhs, (contract, ((), ())), preferred_element_type=F32)
        acc = p if acc is None else acc + p
    return acc


def _log_sigmoid(z):
    return jnp.minimum(z, 0.0) - jnp.log(1.0 + jnp.exp(-jnp.abs(z)))


def _fox_cum(small, bvec):
    s = small.shape[0]
    t = _tile(s, CUM_T)

    def body(sm_ref, b_ref, out_ref, carry):
        @pl.when(pl.program_id(0) == 0)
        def _():
            carry[...] = jnp.zeros_like(carry)

        lf = _log_sigmoid(sm_ref[...] + b_ref[...])
        lft = lf.T[0:FOX_HEADS, :]
        row = lax.broadcasted_iota(jnp.int32, (t, t), 0)
        col = lax.broadcasted_iota(jnp.int32, (t, t), 1)
        upper = (row <= col).astype(BF16)
        cum = _dot_exact01(lft, upper) + carry[:, 0:1]
        out_ref[...] = cum
        carry[...] = carry[...] + jnp.sum(lft, axis=1, keepdims=True)

    return _call(body, name="fox_cum", grid=(s // t,),
                 in_specs=[pl.BlockSpec((t, W_SMALL), lambda i: (i, 0)), pl.BlockSpec((1, W_SMALL), lambda i: (0, 0))],
                 out_specs=pl.BlockSpec((FOX_HEADS, t), lambda i: (0, i)),
                 out_shape=jax.ShapeDtypeStruct((FOX_HEADS, s), F32),
                 scratch_shapes=[pltpu.VMEM((FOX_HEADS, 128), F32)],
                 compiler_params=_params(("arbitrary",)))(small, bvec)


def _fox_cum_bwd(dc, dcq, small, bvec):
    s = small.shape[0]
    t = _tile(s, CUM_T)
    nb = s // t

    def body(dc_ref, dcq_ref, sm_ref, b_ref, out_ref, db_ref, carry):
        @pl.when(pl.program_id(0) == 0)
        def _():
            carry[...] = jnp.zeros_like(carry)
            db_ref[...] = jnp.zeros_like(db_ref)

        lane = lax.broadcasted_iota(jnp.int32, (t, W_SMALL), 1)
        dcq = jnp.zeros((t, W_SMALL), F32)
        for hh in range(FOX_HEADS):
            dcq = jnp.where(lane == hh, dcq_ref[hh], dcq)
        dcv = dc_ref[...] + dcq.T[0:FOX_HEADS, :]
        row = lax.broadcasted_iota(jnp.int32, (t, t), 0)
        col = lax.broadcasted_iota(jnp.int32, (t, t), 1)
        lower = (row >= col).astype(BF16)
        dlf = _dot_exact01(dcv, lower) + carry[:, 0:1]
        carry[...] = carry[...] + jnp.sum(dcv, axis=1, keepdims=True)
        z = sm_ref[...] + b_ref[...]
        zt = z.T[0:FOX_HEADS, :]
        dff = dlf * _sigmoid(-zt)
        db_ref[...] = db_ref[...] + jnp.sum(dff, axis=1, keepdims=True)
        full = jnp.concatenate([dff, jnp.zeros((W_SMALL - FOX_HEADS, t), F32)], axis=0)
        out_ref[...] = full.T

    return _call(body, name="fox_cum_bwd", grid=(nb,),
                 in_specs=[pl.BlockSpec((FOX_HEADS, t), lambda i: (0, nb - 1 - i)),
                           pl.BlockSpec((FOX_HEADS, t, 1), lambda i: (0, nb - 1 - i, 0)),
                           pl.BlockSpec((t, W_SMALL), lambda i: (nb - 1 - i, 0)),
                           pl.BlockSpec((1, W_SMALL), lambda i: (0, 0))],
                 out_specs=[pl.BlockSpec((t, W_SMALL), lambda i: (nb - 1 - i, 0)),
                            pl.BlockSpec((FOX_HEADS, 128), lambda i: (0, 0))],
                 out_shape=[jax.ShapeDtypeStruct((s, W_SMALL), F32), jax.ShapeDtypeStruct((FOX_HEADS, 128), F32)],
                 scratch_shapes=[pltpu.VMEM((FOX_HEADS, 128), F32)],
                 compiler_params=_params(("arbitrary",)))(dc, dcq, small, bvec)


FOX_SCALE = FOX_HEAD_DIM ** -0.5


def _fox_fwd(pm, crow):
    s = pm.shape[0]
    t = _tile(s, FOX_T)
    nb = s // t

    def body(q_ref, k_ref, v_ref, c_ref, o_ref, lse_ref, m_s, l_s, acc_s):
        i = pl.program_id(1)
        j = pl.program_id(2)

        @pl.when(j == 0)
        def _():
            m_s[...] = jnp.full_like(m_s, NEG)
            l_s[...] = jnp.zeros_like(l_s)
            acc_s[...] = jnp.zeros_like(acc_s)

        @pl.when(j <= i)
        def _():
            sc = lax.dot_general(q_ref[...], k_ref[...], (NT, ((), ())), preferred_element_type=F32)
            sc = sc * FOX_SCALE - c_ref[...]
            row = lax.broadcasted_iota(jnp.int32, (t, t), 0)
            col = lax.broadcasted_iota(jnp.int32, (t, t), 1)
            sc = jnp.where((j < i) | (row >= col), sc, NEG)
            m_prev = m_s[...]
            m_new = jnp.maximum(m_prev, jnp.max(sc, axis=1, keepdims=True))
            alpha = jnp.exp(m_prev - m_new)
            p = jnp.exp(sc - m_new)
            l_s[...] = alpha * l_s[...] + jnp.sum(p, axis=1, keepdims=True)
            p_hi = p.astype(BF16)
            p_lo = (p - p_hi.astype(F32)).astype(BF16)
            pv = jnp.dot(p_hi, v_ref[...], preferred_element_type=F32)
            pv = pv + jnp.dot(p_lo, v_ref[...], preferred_element_type=F32)
            acc_s[...] = alpha * acc_s[...] + pv
            m_s[...] = m_new

        @pl.when(j == i)
        def _():
            o_ref[...] = acc_s[...] / l_s[...]
            lse_ref[...] = m_s[...] + jnp.log(l_s[...])

    return _call(
        body, name="fox_fwd", grid=(FOX_HEADS, nb, nb),
        in_specs=[pl.BlockSpec((t, 128), lambda h, i, j: (i, h)),
                  pl.BlockSpec((t, 128), lambda h, i, j: (jnp.minimum(i, j), FOX_HEADS + h)),
                  pl.BlockSpec((t, 128), lambda h, i, j: (jnp.minimum(i, j), 2 * FOX_HEADS + h)),
                  pl.BlockSpec((None, 1, t), lambda h, i, j: (h, 0, jnp.minimum(i, j)))],
        out_specs=[pl.BlockSpec((t, 128), lambda h, i, j: (i, h)),
                   pl.BlockSpec((None, t, 1), lambda h, i, j: (h, i, 0))],
        out_shape=[jax.ShapeDtypeStruct((s, FOX_HEADS * 128), F32), jax.ShapeDtypeStruct((FOX_HEADS, s, 1), F32)],
        scratch_shapes=[pltpu.VMEM((t, 1), F32), pltpu.VMEM((t, 1), F32), pltpu.VMEM((t, 128), F32)],
        compiler_params=_params(("parallel", "parallel", "arbitrary")),
    )(pm, pm, pm, crow)


def _fox_bwd(pm, crow, o, lse, do):
    s = pm.shape[0]
    t = _tile(s, FOX_T)
    nb = s // t

    def body(q_ref, do_ref, o_ref, lse_ref, k_ref, v_ref, c_ref, dq_ref, dk_ref, dv_ref, dc_ref, dcq_ref,
             dk_acc, dv_acc, dc_acc, delta_s):
        j = pl.program_id(1)
        i = pl.program_id(2)

        @pl.when((j == 0) & (i == 0))
        def _():
            dq_ref[...] = jnp.zeros_like(dq_ref)
            dcq_ref[...] = jnp.zeros_like(dcq_ref)

        @pl.when(i == j)
        def _():
            dk_acc[...] = jnp.zeros_like(dk_acc)
            dv_acc[...] = jnp.zeros_like(dv_acc)
            dc_acc[...] = jnp.zeros_like(dc_acc)

        rows = pl.ds(pl.multiple_of(i * t, t), t)

        @pl.when(j == 0)
        def _():
            delta_s[rows, :] = jnp.sum(do_ref[...].astype(F32) * o_ref[...], axis=1, keepdims=True)

        @pl.when(i >= j)
        def _():
            q = q_ref[...]
            dov = do_ref[...]
            sc = lax.dot_general(q, k_ref[...], (NT, ((), ())), preferred_element_type=F32)
            sc = sc * FOX_SCALE - c_ref[...]
            row = lax.broadcasted_iota(jnp.int32, (t, t), 0)
            col = lax.broadcasted_iota(jnp.int32, (t, t), 1)
            p = jnp.where((i > j) | (row >= col), jnp.exp(sc - lse_ref[...]), 0.0)
            dp = lax.dot_general(dov, v_ref[...], (NT, ((), ())), preferred_element_type=F32)
            ds = p * (dp - delta_s[rows, :])
            dsb = ds.astype(BF16)
            dv_acc[...] += lax.dot_general(p.astype(BF16), dov, (TN, ((), ())), preferred_element_type=F32)
            dk_acc[...] += lax.dot_general(dsb, q, (TN, ((), ())), preferred_element_type=F32)
            dq_ref[rows, :] += jnp.dot(dsb, k_ref[...], preferred_element_type=F32) * FOX_SCALE
            dc_acc[...] -= jnp.sum(ds, axis=0, keepdims=True)
            dcq_ref[rows, :] += jnp.sum(ds, axis=1, keepdims=True)

        @pl.when(i == nb - 1)
        def _():
            dk_ref[...] = dk_acc[...] * FOX_SCALE
            dv_ref[...] = dv_acc[...]
            dc_ref[...] = dc_acc[...]

    qi = lambda h, j, i: (jnp.maximum(i, j), h)
    return _call(
        body, name="fox_bwd", grid=(FOX_HEADS, nb, nb),
        in_specs=[pl.BlockSpec((t, 128), qi), pl.BlockSpec((t, 128), qi), pl.BlockSpec((t, 128), qi),
                  pl.BlockSpec((None, t, 1), lambda h, j, i: (h, jnp.maximum(i, j), 0)),
                  pl.BlockSpec((t, 128), lambda h, j, i: (j, FOX_HEADS + h)),
                  pl.BlockSpec((t, 128), lambda h, j, i: (j, 2 * FOX_HEADS + h)),
                  pl.BlockSpec((None, 1, t), lambda h, j, i: (h, 0, j))],
        out_specs=[pl.BlockSpec((s, 128), lambda h, j, i: (0, h)),
                   pl.BlockSpec((t, 128), lambda h, j, i: (j, h)),
                   pl.BlockSpec((t, 128), lambda h, j, i: (j, h)),
                   pl.BlockSpec((None, 1, t), lambda h, j, i: (h, 0, j)),
                   pl.BlockSpec((None, s, 1), lambda h, j, i: (h, 0, 0))],
        out_shape=[jax.ShapeDtypeStruct((s, 1024), F32), jax.ShapeDtypeStruct((s, 1024), F32),
                   jax.ShapeDtypeStruct((s, 1024), F32), jax.ShapeDtypeStruct((FOX_HEADS, 1, s), F32),
                   jax.ShapeDtypeStruct((FOX_HEADS, s, 1), F32)],
        scratch_shapes=[pltpu.VMEM((t, 128), F32), pltpu.VMEM((t, 128), F32), pltpu.VMEM((1, t), F32),
                        pltpu.VMEM((s, 1), F32)],
        compiler_params=_params(("parallel", "arbitrary", "arbitrary")),
    )(pm, do, o, lse, pm, pm, crow)


GLA_SCALE = GLA_DK ** -0.5
GLA_Q_BLK = 3072 // 128
GLA_K_BLK = 3584 // 128
GLA_V_BLK = 4096 // 256


def _gla_gate(sm, wa_ref, b_ref):
    return jnp.dot(sm.astype(BF16), wa_ref[...], preferred_element_type=F32) + b_ref[...]


def _tri(n, strict):
    row = lax.broadcasted_iota(jnp.int32, (n, n), 0)
    col = lax.broadcasted_iota(jnp.int32, (n, n), 1)
    return ((row > col) if strict else (row >= col)).astype(BF16)


def _gla_fwd(pm, small, wa_pad, b_a2):
    s = pm.shape[0]
    r = _tile(s, GLA_R)
    nc = r // CHUNK

    def body(q_ref, k_ref, v_ref, sm_ref, wa_ref, b_ref, o_ref, st_ref, state):
        @pl.when(pl.program_id(1) == 0)
        def _():
            state[...] = jnp.zeros_like(state)

        tri = _tri(CHUNK, False)
        for c in range(nc):
            rows = slice(c * CHUNK, (c + 1) * CHUNK)
            la = _log_sigmoid(_gla_gate(sm_ref[rows, :], wa_ref, b_ref)) * (1.0 / GLA_TEMP)
            cum = _dot_exact01(la, tri, tri_first=True)
            total = jnp.sum(la, axis=0, keepdims=True)
            kdec = k_ref[rows, :].astype(F32) * jnp.exp(total - cum)
            ut = lax.dot_general(v_ref[rows, :], kdec.astype(BF16), (TN, ((), ())), preferred_element_type=F32)
            new = state[...] * jnp.exp(total) + ut
            state[...] = new
            newb = new.astype(BF16)
            st_ref[c] = newb
            qs = (q_ref[rows, :].astype(F32) * GLA_SCALE).astype(BF16)
            o_ref[rows, :] = lax.dot_general(qs, newb, (NT, ((), ())), preferred_element_type=F32)

    return _call(
        body, name="gla_fwd", grid=(GLA_HEADS, s // r),
        in_specs=[pl.BlockSpec((r, 128), lambda h, i: (i, GLA_Q_BLK + h)),
                  pl.BlockSpec((r, 128), lambda h, i: (i, GLA_K_BLK + h)),
                  pl.BlockSpec((r, 256), lambda h, i: (i, GLA_V_BLK + h)),
                  pl.BlockSpec((r, W_SMALL), lambda h, i: (i, 0)),
                  pl.BlockSpec((W_SMALL, 128), lambda h, i: (0, h)),
                  pl.BlockSpec((1, 128), lambda h, i: (0, h))],
        out_specs=[pl.BlockSpec((r, 256), lambda h, i: (i, h)),
                   pl.BlockSpec((nc, None, GLA_DV, GLA_DK), lambda h, i: (i, h, 0, 0))],
        out_shape=[jax.ShapeDtypeStruct((s, 1024), F32),
                   jax.ShapeDtypeStruct((s // CHUNK, GLA_HEADS, GLA_DV, GLA_DK), BF16)],
        scratch_shapes=[pltpu.VMEM((GLA_DV, GLA_DK), F32)],
        compiler_params=_params(("parallel", "arbitrary")),
    )(pm, pm, pm, small, wa_pad, b_a2)


def _gla_bwd(pm, small, wa_pad, b_a2, states, do):
    s = pm.shape[0]
    r = _tile(s, GLA_R)
    nc = r // CHUNK
    nb = s // r

    def body(q_ref, k_ref, v_ref, sm_ref, wa_ref, b_ref, do_ref, st_ref, prev_ref,
             dq_ref, dk_ref, dv_ref, dza_ref, db_ref, carry):
        step = pl.program_id(1)

        @pl.when(step == 0)
        def _():
            carry[...] = jnp.zeros_like(carry)
            db_ref[...] = jnp.zeros_like(db_ref)

        tri = _tri(CHUNK, False)
        tri_strict = _tri(CHUNK, True)
        db = jnp.zeros((1, 128), F32)
        for c in reversed(range(nc)):
            rows = slice(c * CHUNK, (c + 1) * CHUNK)
            z = _gla_gate(sm_ref[rows, :], wa_ref, b_ref)
            la = _log_sigmoid(z) * (1.0 / GLA_TEMP)
            cum = _dot_exact01(la, tri, tri_first=True)
            total = jnp.sum(la, axis=0, keepdims=True)
            w = jnp.exp(total - cum)
            decay = jnp.exp(total)
            kdec = k_ref[rows, :].astype(F32) * w
            dov = do_ref[rows, :]
            qs = (q_ref[rows, :].astype(F32) * GLA_SCALE).astype(BF16)
            dq_ref[rows, :] = jnp.dot(dov, st_ref[c], preferred_element_type=F32) * GLA_SCALE
            gt = lax.dot_general(dov, qs, (TN, ((), ())), preferred_element_type=F32) + carry[...]
            gtb = gt.astype(BF16)
            dv_ref[rows, :] = lax.dot_general(kdec.astype(BF16), gtb, (NT, ((), ())), preferred_element_type=F32)
            dkdec = jnp.dot(v_ref[rows, :], gtb, preferred_element_type=F32)
            dk_ref[rows, :] = dkdec * w
            e = dkdec * kdec
            if c > 0:
                prev = st_ref[c - 1].astype(F32)
            else:
                prev = jnp.where(step == nb - 1, 0.0, prev_ref[0].astype(F32))
            dtot = jnp.sum(gt * prev, axis=0, keepdims=True) * decay
            dla = dtot + _dot_exact01(e, tri_strict, tri_first=True)
            dza = dla * (1.0 / GLA_TEMP) * _sigmoid(-z)
            dza_ref[rows, :] = dza.astype(BF16)
            db = db + jnp.sum(dza, axis=0, keepdims=True)
            carry[...] = gt * decay
        db_ref[...] += db

    blk = lambda h, i: nb - 1 - i
    return _call(
        body, name="gla_bwd", grid=(GLA_HEADS, nb),
        in_specs=[pl.BlockSpec((r, 128), lambda h, i: (blk(h, i), GLA_Q_BLK + h)),
                  pl.BlockSpec((r, 128), lambda h, i: (blk(h, i), GLA_K_BLK + h)),
                  pl.BlockSpec((r, 256), lambda h, i: (blk(h, i), GLA_V_BLK + h)),
                  pl.BlockSpec((r, W_SMALL), lambda h, i: (blk(h, i), 0)),
                  pl.BlockSpec((W_SMALL, 128), lambda h, i: (0, h)),
                  pl.BlockSpec((1, 128), lambda h, i: (0, h)),
                  pl.BlockSpec((r, 256), lambda h, i: (blk(h, i), h)),
                  pl.BlockSpec((nc, None, GLA_DV, GLA_DK), lambda h, i: (blk(h, i), h, 0, 0)),
                  pl.BlockSpec((1, None, GLA_DV, GLA_DK),
                               lambda h, i: (jnp.maximum(blk(h, i) * nc - 1, 0), h, 0, 0))],
        out_specs=[pl.BlockSpec((r, 128), lambda h, i: (blk(h, i), h)),
                   pl.BlockSpec((r, 128), lambda h, i: (blk(h, i), h)),
                   pl.BlockSpec((r, 256), lambda h, i: (blk(h, i), h)),
                   pl.BlockSpec((r, 128), lambda h, i: (blk(h, i), h)),
                   pl.BlockSpec((1, 128), lambda h, i: (0, h))],
        out_shape=[jax.ShapeDtypeStruct((s, 512), F32), jax.ShapeDtypeStruct((s, 512), F32),
                   jax.ShapeDtypeStruct((s, 1024), F32), jax.ShapeDtypeStruct((s, 512), BF16),
                   jax.ShapeDtypeStruct((1, 512), F32)],
        scratch_shapes=[pltpu.VMEM((GLA_DV, GLA_DK), F32)],
        compiler_params=_params(("parallel", "arbitrary")),
    )(pm, pm, pm, small, wa_pad, b_a2, do, states, states)


def _modulation(c_all, w_ada):
    n = w_ada.shape[1]
    tn = _tile(n, 512)

    def body(c_ref, w_ref, out_ref, ca_ref):
        cv = c_ref[...]
        ca = cv * _sigmoid(cv)
        ca_ref[...] = ca
        out_ref[...] = jnp.dot(ca.astype(BF16), w_ref[...].astype(BF16), preferred_element_type=F32)

    return _call(body, name="modulation", grid=(n // tn,),
                 in_specs=[pl.BlockSpec((N_DEV, D_MODEL), lambda j: (0, 0)),
                           pl.BlockSpec((D_MODEL, tn), lambda j: (0, j))],
                 out_specs=[pl.BlockSpec((N_DEV, tn), lambda j: (0, j)),
                            pl.BlockSpec((N_DEV, D_MODEL), lambda j: (0, 0))],
                 out_shape=[jax.ShapeDtypeStruct((N_DEV, n), F32), jax.ShapeDtypeStruct((N_DEV, D_MODEL), F32)],
                 compiler_params=_params(("arbitrary",)))(c_all, w_ada)


def _adamw_math(w, g, m, v):
    m = ADAM_B1 * m + (1.0 - ADAM_B1) * g
    v = ADAM_B2 * v + (1.0 - ADAM_B2) * (g * g)
    m_hat = m / (1.0 - ADAM_B1 ** ADAM_STEP)
    v_hat = v / (1.0 - ADAM_B2 ** ADAM_STEP)
    delta = -ADAM_LR * (m_hat / (jnp.sqrt(v_hat) + ADAM_EPS) + ADAM_WD * w)
    return delta, m, v


def _adamw_slabs(name, w, slabs, m, v, tr=256):
    rr, cc = w.shape
    tr = _tile(rr, tr)

    def body(w_ref, s_ref, m_ref, v_ref, g_ref, d_ref, nm_ref, nv_ref):
        g = s_ref[0].astype(F32)
        for r in range(1, N_DEV):
            g = g + s_ref[r].astype(F32)
        g_ref[...] = g
        d, nm, nv = _adamw_math(w_ref[...], g, m_ref[...], v_ref[...])
        d_ref[...] = d
        nm_ref[...] = nm
        nv_ref[...] = nv

    spec = pl.BlockSpec((tr, cc), lambda i: (i, 0))
    return _call(body, name=name, grid=(rr // tr,),
                 in_specs=[spec, pl.BlockSpec((N_DEV, tr, cc), lambda i: (0, i, 0)), spec, spec],
                 out_specs=[spec] * 4, out_shape=[jax.ShapeDtypeStruct((rr, cc), F32)] * 4,
                 compiler_params=_params(("parallel",)))(w, slabs, m, v)


def _adamw_ada(w, cat, dm, m, v, tr=256):
    rr, cc = w.shape
    tr = _tile(rr, tr)

    def body(w_ref, ca_ref, dm_ref, m_ref, v_ref, g_ref, d_ref, nm_ref, nv_ref):
        g = ca_ref[:, 0:1] * dm_ref[0:1, :]
        for b in range(1, N_DEV):
            g = g + ca_ref[:, b:b + 1] * dm_ref[b:b + 1, :]
        g_ref[...] = g
        d, nm, nv = _adamw_math(w_ref[...], g, m_ref[...], v_ref[...])
        d_ref[...] = d
        nm_ref[...] = nm
        nv_ref[...] = nv

    spec = pl.BlockSpec((tr, cc), lambda i: (i, 0))
    return _call(body, name="adamw_ada", grid=(rr // tr,),
                 in_specs=[spec, pl.BlockSpec((tr, N_DEV), lambda i: (i, 0)),
                           pl.BlockSpec((N_DEV, cc), lambda i: (0, 0)), spec, spec],
                 out_specs=[spec] * 4, out_shape=[jax.ShapeDtypeStruct((rr, cc), F32)] * 4,
                 compiler_params=_params(("parallel",)))(w, cat, dm, m, v)


def _sum_devices(gathered):
    ln = gathered.shape[-1]

    def body(g_ref, out_ref):
        acc = g_ref[0]
        for r in range(1, N_DEV):
            acc = acc + g_ref[r]
        out_ref[...] = acc

    return _call(body, name="sum_devices",
                 in_specs=[pl.BlockSpec(memory_space=pltpu.VMEM)], out_specs=pl.BlockSpec(memory_space=pltpu.VMEM),
                 out_shape=jax.ShapeDtypeStruct((1, ln), F32))(gathered)


def _adamw_flat(w, g, m, v):
    def body(w_ref, g_ref, m_ref, v_ref, d_ref, nm_ref, nv_ref):
        d, nm, nv = _adamw_math(w_ref[...], g_ref[...], m_ref[...], v_ref[...])
        d_ref[...] = d
        nm_ref[...] = nm
        nv_ref[...] = nv

    vm = pl.BlockSpec(memory_space=pltpu.VMEM)
    return _call(body, name="adamw_small", in_specs=[vm] * 4, out_specs=[vm] * 3,
                 out_shape=[jax.ShapeDtypeStruct(w.shape, F32)] * 3)(w, g, m, v)


def _from_col_shards(g):
    return jnp.transpose(g, (1, 0, 2)).reshape(g.shape[1], N_DEV * g.shape[2])


def _pad_lanes(v, n):
    return jnp.concatenate([v, jnp.zeros(v.shape[:-1] + (n - v.shape[-1],), v.dtype)], axis=-1)


def kernel(x, c, w_ada, b_ada, g_pre_mix, g_post_mix, w_in, b_fgate, w_gla_a2, b_gla_a2, g_fox_out, g_gla_out, w_out, g_pre_mlp, g_post_mlp, w_mlp_in, w_mlp_out, loss_target, m_w_ada, m_b_ada, m_g_pre_mix, m_g_post_mix, m_w_in, m_b_fgate, m_w_gla_a2, m_b_gla_a2, m_g_fox_out, m_g_gla_out, m_w_out, m_g_pre_mlp, m_g_post_mlp, m_w_mlp_in, m_w_mlp_out, v_w_ada, v_b_ada, v_g_pre_mix, v_g_post_mix, v_w_in, v_b_fgate, v_w_gla_a2, v_b_gla_a2, v_g_fox_out, v_g_gla_out, v_w_out, v_g_pre_mlp, v_g_post_mlp, v_w_mlp_in, v_w_mlp_out):
    rank = _my_rank()
    xs = x[0]
    s = xs.shape[0]
    target = loss_target[0]

    c_all, wa2_g, ggla_g, win_g = _all_gather("gather_first", [c, w_gla_a2[0], g_gla_out[0], w_in[0].astype(BF16)])
    rest = [w_out[0].astype(BF16), w_mlp_in[0].astype(BF16), w_mlp_out[0].astype(BF16)]
    gs_send, gs_recv, gs_thru, gs_land, gs_token = _exchange_start("gather_rest_start", rest, gather=True,
                                                                   after=(c_all,))
    w_a2 = _from_col_shards(wa2_g)
    g_gla = _from_col_shards(ggla_g).reshape(1, 1024)
    g_fox = g_fox_out.reshape(1, 1024)
    win_full = _from_col_shards(win_g)
    w_main = jnp.concatenate([win_full[:, :3072], win_full[:, 3080:5128], win_full[:, 5144:6168]], axis=1)
    w_small = _pad_lanes(jnp.concatenate([win_full[:, 3072:3080], win_full[:, 5128:5144]], axis=1), W_SMALL)
    wa_pad =jnp.concatenate([jnp.zeros((8, 512), BF16), w_a2.astype(BF16), jnp.zeros((104, 512), BF16)], axis=0)
    bf_vec = _pad_lanes(b_fgate, W_SMALL)

    mod_part, c_act = _modulation(c_all.reshape(N_DEV, D_MODEL), w_ada[0])
    (mod_g,) = _all_gather("gather_mod", [mod_part])
    mod = lax.dynamic_slice_in_dim(mod_g, rank, 1, axis=1).reshape(1, 6 * D_MODEL) + b_ada
    shift_m, scale_m, gate_m, shift_f, scale_f, gate_f = [mod[:, i * D_MODEL:(i + 1) * D_MODEL] for i in range(6)]

    h = _premix(xs, g_pre_mix, scale_m, shift_m, deps=(gs_token,))
    pm = _mm_plain("proj_main", h, w_main, NN, BF16)
    small = _mm_plain("proj_small", h, w_small, NN, F32)
    crow = _fox_cum(small, bf_vec).reshape(FOX_HEADS, 1, s)
    o_fox, lse = _fox_fwd(pm, crow)
    o_gla, states = _gla_fwd(pm, small, wa_pad, b_gla_a2)
    mix = _mix_fwd(o_fox, o_gla, pm, g_fox, g_gla)
    wout_g, wmi_g, wmo_g = _exchange_wait("gather_rest_wait", gs_send, gs_recv, gs_thru, gs_land, mix, gather=True)
    w_out_full = wout_g.reshape(D_MODEL, D_MODEL)
    w_mo_full = wmo_g.reshape(D_FF, D_MODEL)
    y = _mm_plain("out_proj", mix, w_out_full, NN, F32)
    x1, h2 = _postmix_premlp(xs, y, gate_m, g_post_mix, g_pre_mlp, scale_f, shift_f)

    tm, tn, tk = _tile(s, 1024), 1024, 512
    nsh = 1024 // tn

    def relu2(acc):
        rl = jnp.maximum(acc, 0.0)
        return rl * rl, rl

    z, a_relu = _matmul(
        "mlp_in", h2, wmi_g, contract=NN, grid=(s // tm, D_FF // tn, D_MODEL // tk),
        a_spec=pl.BlockSpec((tm, tk), lambda i, j, k: (i, k)),
        b_spec=pl.BlockSpec((None, tk, tn), lambda i, j, k: (j // nsh, k, j % nsh)),
        out_specs=[pl.BlockSpec((tm, tn), lambda i, j, k: (i, j))] * 2,
        out_shapes=[jax.ShapeDtypeStruct((s, D_FF), BF16)] * 2, acc_shape=(tm, tn), epilogue=relu2)
    y2 = _mm_plain("mlp_out", z, w_mo_full, NN, F32)

    dx2, dy2, loss_vec, dgate_f, dg_post_mlp = _loss_postmlp_bwd(x1, y2, target, gate_f, g_post_mlp)
    loss = lax.psum(loss_vec[0, 0], ("x", "y", "c"))

    da = _mm_plain("mlp_out_dx", dy2, w_mo_full, NT, BF16, extra=(a_relu,),
                   epilogue=lambda acc, rl: (acc * (2.0 * rl.astype(F32)),))
    dw_mo = _mm_plain("mlp_out_dw", z, dy2, TN, BF16)
    x_mo = _exchange_start("grad_mlp_out_start", [dw_mo.reshape(N_DEV, 1024, D_MODEL)], gather=False)
    kb = 1024 // tk
    (dh2,) = _matmul(
        "mlp_in_dx", da, wmi_g, contract=NT, grid=(s // tm, D_MODEL // tn, D_FF // tk),
        a_spec=pl.BlockSpec((tm, tk), lambda i, j, k: (i, k)),
        b_spec=pl.BlockSpec((None, tn, tk), lambda i, j, k: (k // kb, j, k % kb)),
        out_specs=[pl.BlockSpec((tm, tn), lambda i, j, k: (i, j))],
        out_shapes=[jax.ShapeDtypeStruct((s, D_MODEL), F32)], acc_shape=(tm, tn), deps=(x_mo[4],))
    ts = _tile(s, 512)
    (dw_mi,) = _matmul(
        "mlp_in_dw", h2, da, contract=TN, grid=(D_MODEL // 1024, D_FF // tn, s // ts),
        a_spec=pl.BlockSpec((ts, 1024), lambda i, j, k: (k, i)),
        b_spec=pl.BlockSpec((ts, tn), lambda i, j, k: (k, j)),
        out_specs=[pl.BlockSpec((None, 1024, tn), lambda i, j, k: (j // nsh, i, j % nsh))],
        out_shapes=[jax.ShapeDtypeStruct((N_DEV, D_MODEL, 1024), BF16)], acc_shape=(1024, tn))
    x_mi = _exchange_start("grad_mlp_in_start", [dw_mi], gather=False)

    dx1, dy, dscale_f, dshift_f, dg_pre_mlp, dgate_m, dg_post_mix = _premlp_postmix_bwd(
        dh2, dx2, x1, y, scale_f, g_pre_mlp, gate_m, g_post_mix, deps=(x_mi[4],))

    dmix = _mm_plain("out_proj_dx", dy, w_out_full, NT, F32)
    dw_out = _mm_plain("out_proj_dw", mix, dy, TN, BF16)
    x_out = _exchange_start("grad_out_start", [dw_out.reshape(N_DEV, 256, D_MODEL)], gather=False)
    do_fox, do_gla, dgr, dg_fox, dg_gla = _mix_bwd(dmix, o_fox, o_gla, pm, g_fox, g_gla, deps=(x_out[4],))

    dq, dk, dv, dc, dcq = _fox_bwd(pm, crow, o_fox, lse, do_fox)
    dsmall_f, db_f = _fox_cum_bwd(dc.reshape(FOX_HEADS, s), dcq, small, bf_vec)
    dgq, dgk, dgv, dza, db_a2 = _gla_bwd(pm, small, wa_pad, b_gla_a2, states, do_gla)
    dsmall = _mm_plain("gate_dx", dza, wa_pad, NT, F32, tn=128, extra=(dsmall_f,),
                       epilogue=lambda acc, other: (acc + other,))
    dwa_pad = _mm_plain("gate_dw", small, dza, TN, F32, tm=128, tn=512)

    dpm = jnp.concatenate([dq.astype(BF16), dk.astype(BF16), dv.astype(BF16), dgq.astype(BF16), dgk.astype(BF16),
                           dgv.astype(BF16), dgr], axis=1)
    dw_main = _mm_plain("proj_main_dw", h, dpm, TN, BF16)
    dw_small = _mm_plain("proj_small_dw", h, dsmall, TN, BF16, tn=128)
    dwin_full = jnp.concatenate([dw_main[:, :3072], dw_small[:, 0:8], dw_main[:, 3072:5120], dw_small[:, 8:24],
                                 dw_main[:, 5120:6144]], axis=1)
    dwin_slabs = jnp.transpose(dwin_full.reshape(D_MODEL, N_DEV, 771), (1, 0, 2))
    x_in = _exchange_start("grad_in_start", [dwin_slabs], gather=False)
    dh_small = _mm_plain("proj_small_dx", dsmall, w_small, NT, F32, tk=128)
    dh = _mm_plain("proj_main_dx", dpm, w_main, NT, F32, extra=(dh_small,),
                   epilogue=lambda acc, other: (acc + other,), deps=(x_in[4],))
    grad_x, dscale_m, dshift_m, dg_pre_mix = _premix_bwd(dh, dx1, xs, g_pre_mix, scale_m)

    dmod = jnp.concatenate([dshift_m, dscale_m, dgate_m, dshift_f, dscale_f, dgate_f], axis=1)
    flat = jnp.concatenate(
        [dmod, dg_pre_mix, dg_post_mix, dg_fox, dg_pre_mlp, dg_post_mlp, db_a2,
         dwa_pad[8:24, :].reshape(1, GLA_RANK * 512), dg_gla, _pad_lanes(db_f[:, 0].reshape(1, FOX_HEADS), 128)],
        axis=1)
    (flat_g,) = _all_gather("gather_small_grads", [flat])
    tot = _sum_devices(flat_g)

    (r_mo,) = _exchange_wait("grad_mlp_out_wait", *x_mo[:4], grad_x, gather=False)
    g_mo, d_mo, nm_mo, nv_mo = _adamw_slabs("adamw_w_mlp_out", w_mlp_out[0], r_mo, m_w_mlp_out[0], v_w_mlp_out[0])
    (r_mi,) = _exchange_wait("grad_mlp_in_wait", *x_mi[:4], g_mo, gather=False)
    g_mi, d_mi, nm_mi, nv_mi = _adamw_slabs("adamw_w_mlp_in", w_mlp_in[0], r_mi, m_w_mlp_in[0], v_w_mlp_in[0])
    (r_out,) = _exchange_wait("grad_out_wait", *x_out[:4], g_mi, gather=False)
    g_out, d_out, nm_out, nv_out = _adamw_slabs("adamw_w_out", w_out[0], r_out, m_w_out[0], v_w_out[0])

    dm_cols = lax.dynamic_slice_in_dim(flat_g[:, 0, :6 * D_MODEL], rank * 1536, 1536, axis=1)
    g_ada, d_ada, nm_ada, nv_ada = _adamw_ada(w_ada[0], c_act.T, dm_cols, m_w_ada[0], v_w_ada[0])
    (r_in,) = _exchange_wait("grad_in_wait", *x_in[:4], g_ada, gather=False)
    g_in, d_in, nm_in, nv_in = _adamw_slabs("adamw_w_in", w_in[0], r_in, m_w_in[0], v_w_in[0])

    o = 0
    seg = {}
    for name, n in (("b_ada", 12288), ("g_pre_mix", 2048), ("g_post_mix", 2048), ("g_fox_out", 1024),
                    ("g_pre_mlp", 2048), ("g_post_mlp", 2048), ("b_gla_a2", 512), ("w_gla_a2", 8192),
                    ("g_gla_out", 1024), ("b_fgate", 128)):
        seg[name] = tot[:, o:o + n]
        o += n
    g_wa2 = lax.dynamic_slice_in_dim(seg["w_gla_a2"].reshape(GLA_RANK, 512), rank * 64, 64, axis=1)
    g_ggla = lax.dynamic_slice_in_dim(seg["g_gla_out"].reshape(GLA_HEADS, GLA_DV), rank * 32, 32, axis=1)
    small_names = ["b_ada", "g_pre_mix", "g_post_mix", "g_fox_out", "g_pre_mlp", "g_post_mlp", "b_gla_a2",
                   "w_gla_a2", "g_gla_out", "b_fgate"]
    small_grads = {**seg, "w_gla_a2": g_wa2.reshape(1, 1024), "g_gla_out": g_ggla.reshape(1, 128)}
    weights = dict(b_ada=b_ada, g_pre_mix=g_pre_mix, g_post_mix=g_post_mix, g_fox_out=g_fox_out,
                   g_pre_mlp=g_pre_mlp, g_post_mlp=g_post_mlp, b_gla_a2=b_gla_a2, w_gla_a2=w_gla_a2,
                   g_gla_out=g_gla_out, b_fgate=b_fgate)
    moms = dict(b_ada=m_b_ada, g_pre_mix=m_g_pre_mix, g_post_mix=m_g_post_mix, g_fox_out=m_g_fox_out,
                g_pre_mlp=m_g_pre_mlp, g_post_mlp=m_g_post_mlp, b_gla_a2=m_b_gla_a2, w_gla_a2=m_w_gla_a2,
                g_gla_out=m_g_gla_out, b_fgate=m_b_fgate)
    vels = dict(b_ada=v_b_ada, g_pre_mix=v_g_pre_mix, g_post_mix=v_g_post_mix, g_fox_out=v_g_fox_out,
                g_pre_mlp=v_g_pre_mlp, g_post_mlp=v_g_post_mlp, b_gla_a2=v_b_gla_a2, w_gla_a2=v_w_gla_a2,
                g_gla_out=v_g_gla_out, b_fgate=v_b_fgate)

    def flatten(d, fill):
        parts = []
        for nm in small_names:
            p = d[nm].reshape(1, -1)
            if nm == "b_fgate":
                p = jnp.concatenate([p[:, :FOX_HEADS], jnp.full((1, 128 - FOX_HEADS), fill, F32)], axis=1)
            parts.append(p)
        return jnp.concatenate(parts, axis=1).reshape(-1, 128)

    fw, fg, fm, fv = flatten(weights, 0.0), flatten(small_grads, 0.0), flatten(moms, 0.0), flatten(vels, 1.0)
    fd, fnm, fnv = _adamw_flat(fw, fg, fm, fv)

    def unflatten(fl):
        fl = fl.reshape(1, -1)
        out = {}
        o = 0
        for nm in small_names:
            n = 128 if nm == "b_fgate" else weights[nm].size
            piece = fl[:, o:o + n]
            if nm == "b_fgate":
                piece = piece[:, :FOX_HEADS]
            out[nm] = piece.reshape(weights[nm].shape)
            o += n
        return out

    sg, sd, snm, snv = unflatten(fg), unflatten(fd), unflatten(fnm), unflatten(fnv)

    big = dict(w_ada=(g_ada, d_ada, nm_ada, nv_ada), w_in=(g_in, d_in, nm_in, nv_in),
               w_out=(g_out, d_out, nm_out, nv_out), w_mlp_in=(g_mi, d_mi, nm_mi, nv_mi),
               w_mlp_out=(g_mo, d_mo, nm_mo, nv_mo))
    order = ["w_ada", "b_ada", "g_pre_mix", "g_post_mix", "w_in", "b_fgate", "w_gla_a2", "b_gla_a2", "g_fox_out",
             "g_gla_out", "w_out", "g_pre_mlp", "g_post_mlp", "w_mlp_in", "w_mlp_out"]

    def pick(nm, idx):
        if nm in big:
            return big[nm][idx][None]
        return (sg, sd, snm, snv)[idx][nm]

    grads = [pick(nm, 0) for nm in order]
    deltas = [pick(nm, 1) for nm in order]
    new_m = [pick(nm, 2) for nm in order]
    new_v = [pick(nm, 3) for nm in order]
    return (loss, grad_x[None], *grads, *deltas, *new_m, *new_v)
```

```python
import functools

import numpy as np
import jax
import jax.numpy as jnp
from jax import lax
from jax.experimental import pallas as pl
from jax.experimental.pallas import tpu as pltpu

F32 = jnp.float32
BF16 = jnp.bfloat16
MESH = pl.DeviceIdType.MESH
N_DEV = 8

D_MODEL = 2048
FOX_HEADS = 8
FOX_HEAD_DIM = 128
GLA_HEADS = 4
GLA_DK = 128
GLA_DV = 256
GLA_RANK = 16
GLA_TEMP = 16.0
CHUNK = 64
D_FF = 8192
W_MAIN = 6144
W_SMALL = 128
EPS = 1e-6
NEG = float(np.finfo(np.float32).min)

ADAM_LR = 0.001
ADAM_B1 = 0.9
ADAM_B2 = 0.999
ADAM_EPS = 1e-08
ADAM_WD = 0.01
ADAM_STEP = 10

ROW_T = 256
FOX_T = 512
GLA_R = 512
CUM_T = 256
VMEM_LIMIT = 56 * 1024 * 1024


def _call(body, deps=(), **kw):
    if not deps:
        return pl.pallas_call(body, **kw)
    n_in, n_dep = len(kw["in_specs"]), len(deps)

    def with_deps(*refs):
        return body(*refs[:n_in], *refs[n_in + n_dep:])

    kw["in_specs"] = [*kw["in_specs"], *[pl.BlockSpec(memory_space=pl.ANY)] * n_dep]
    call = pl.pallas_call(with_deps, **kw)
    return lambda *args: call(*args, *deps)


def _params(sem=None):
    return pltpu.CompilerParams(dimension_semantics=sem, vmem_limit_bytes=VMEM_LIMIT)


def _my_pos():
    return lax.axis_index("x"), lax.axis_index("y"), lax.axis_index("c")


def _my_rank():
    x, y, c = _my_pos()
    return 4 * x + 2 * y + c


def _all_gather(name, arrays):
    n = len(arrays)

    def body(*refs):
        ins = refs[:n]
        outs = refs[n:2 * n]
        send_sems, recv_sems, local_sems = refs[2 * n:]
        x, y, c = _my_pos()
        me, sibling = (x, y, c), (x, y, 1 - c)
        chips = [(1 - x, y), (x, 1 - y), (1 - x, 1 - y)]

        def slot(a, px, py, pc):
            return outs[a].at[4 * px + 2 * py + pc]

        def copy(a, k, block, to, src=None):
            return pltpu.make_async_remote_copy(
                src_ref=slot(a, *block) if src is None else src, dst_ref=slot(a, *block),
                send_sem=send_sems.at[a, k], recv_sem=recv_sems.at[a, k],
                device_id=to, device_id_type=MESH)

        started = []
        for a in range(n):
            mine = pltpu.make_async_copy(ins[a], slot(a, *me), local_sems.at[a])
            mine.start()
            started.append(mine)
        first = []
        for a in range(n):
            first.append(copy(a, 0, me, sibling, src=ins[a]))
            first += [copy(a, 1 + j, me, (*chip, c), src=ins[a]) for j, chip in enumerate(chips)]
        for cp in first:
            cp.start()
        passed = []
        for j, chip in enumerate(chips):
            for a in range(n):
                copy(a, 1 + j, (*chip, c), me).wait_recv()
                fwd = copy(a, 4 + j, (*chip, c), sibling)
                fwd.start()
                passed.append(fwd)
        for a in range(n):
            copy(a, 0, sibling, me).wait_recv()
            for j, chip in enumerate(chips):
                copy(a, 4 + j, (*chip, 1 - c), me).wait_recv()
        for cp in first + passed:
            cp.wait_send()
        for mine in started:
            mine.wait()

    hbm = pl.BlockSpec(memory_space=pltpu.HBM)
    return _call(
        body, name=name,
        out_shape=[jax.ShapeDtypeStruct((N_DEV,) + a.shape, a.dtype) for a in arrays],
        in_specs=[hbm] * n, out_specs=[hbm] * n,
        scratch_shapes=[pltpu.SemaphoreType.DMA((n, 7)), pltpu.SemaphoreType.DMA((n, 7)),
                        pltpu.SemaphoreType.DMA((n,))],
    )(*arrays)


def _all_to_all(name, arrays):
    n = len(arrays)

    def body(*refs):
        ins = refs[:n]
        outs = refs[n:2 * n]
        send_sems, recv_sems, local_sems = refs[2 * n:]
        x, y, c = _my_pos()
        me = 4 * x + 2 * y + c
        flips = [(kx, ky, kc) for kx in (0, 1) for ky in (0, 1) for kc in (0, 1)][1:]

        def peer(k):
            kx, ky, kc = flips[k]
            return (1 - x if kx else x), (1 - y if ky else y), (1 - c if kc else c)

        local = []
        for a in range(n):
            cp = pltpu.make_async_copy(ins[a].at[me], outs[a].at[me], local_sems.at[a])
            cp.start()
            local.append(cp)
        sends = []
        for k in range(7):
            px, py, pc = peer(k)
            pr = 4 * px + 2 * py + pc
            for a in range(n):
                cp = pltpu.make_async_remote_copy(
                    src_ref=ins[a].at[pr], dst_ref=outs[a].at[me],
                    send_sem=send_sems.at[a, k], recv_sem=recv_sems.at[a, k],
                    device_id=(px, py, pc), device_id_type=MESH)
                cp.start()
                sends.append(cp)
        for k in range(7):
            px, py, pc = peer(k)
            pr = 4 * px + 2 * py + pc
            for a in range(n):
                pltpu.make_async_remote_copy(
                    src_ref=ins[a].at[pr], dst_ref=outs[a].at[pr],
                    send_sem=send_sems.at[a, k], recv_sem=recv_sems.at[a, k],
                    device_id=(px, py, pc), device_id_type=MESH).wait_recv()
        for cp in sends:
            cp.wait_send()
        for cp in local:
            cp.wait()

    hbm = pl.BlockSpec(memory_space=pltpu.HBM)
    return _call(
        body, name=name,
        out_shape=[jax.ShapeDtypeStruct(a.shape, a.dtype) for a in arrays],
        in_specs=[hbm] * n, out_specs=[hbm] * n,
        scratch_shapes=[pltpu.SemaphoreType.DMA((n, 7)), pltpu.SemaphoreType.DMA((n, 7)),
                        pltpu.SemaphoreType.DMA((n,))],
    )(*arrays)


_HBM = pl.BlockSpec(memory_space=pltpu.HBM)
_SEM = pl.BlockSpec(memory_space=pltpu.SEMAPHORE)
_FLIPS = [(kx, ky, kc) for kx in (0, 1) for ky in (0, 1) for kc in (0, 1)][1:]


def _peers():
    x, y, c = _my_pos()
    out = []
    for kx, ky, kc in _FLIPS:
        px, py, pc = (1 - x if kx else x), (1 - y if ky else y), (1 - c if kc else c)
        out.append(((px, py, pc), 4 * px + 2 * py + pc))
    return out


def _exchange_copy(gather, ins, lands, send_sems, recv_sems, a, k, peer, peer_rank, me):
    return pltpu.make_async_remote_copy(
        src_ref=ins[a] if gather else ins[a].at[peer_rank],
        dst_ref=lands[a].at[me],
        send_sem=send_sems[a].at[k], recv_sem=recv_sems[a].at[k],
        device_id=peer, device_id_type=MESH)


def _exchange_start(name, arrays, gather, after=()):
    n = len(arrays)
    n_after = len(after)
    land_shapes = [((N_DEV,) + a.shape) if gather else a.shape for a in arrays]

    def body(*refs):
        ins, lands = refs[:n], refs[n:2 * n]
        refs = refs[n_after:]
        send_sems, recv_sems = refs[2 * n:3 * n], refs[3 * n:4 * n]
        token = refs[-1]
        me = _my_rank()
        for a in range(n):
            for k, (peer, peer_rank) in enumerate(_peers()):
                _exchange_copy(gather, ins, lands, send_sems, recv_sems, a, k, peer, peer_rank, me).start()
        token[...] = jnp.zeros_like(token)

    sems = [pltpu.SemaphoreType.DMA((7,))] * (2 * n)
    outs = pl.pallas_call(
        body, name=name,
        out_shape=(*sems, *[pltpu.HBM(a.shape, a.dtype) for a in arrays],
                   *[pltpu.HBM(ls, a.dtype) for ls, a in zip(land_shapes, arrays)],
                   jax.ShapeDtypeStruct((8, 128), F32)),
        in_specs=[*[_HBM] * (2 * n), *[pl.BlockSpec(memory_space=pl.ANY)] * n_after],
        out_specs=(*[_SEM] * (2 * n), *[_HBM] * (2 * n), pl.BlockSpec(memory_space=pltpu.VMEM)),
        input_output_aliases={i: 2 * n + i for i in range(2 * n)},
        compiler_params=pltpu.CompilerParams(has_side_effects=pltpu.SideEffectType.DATAFLOW_SIDE_EFFECTING),
    )(*[pltpu.with_memory_space_constraint(a, pltpu.HBM) for a in arrays],
      *[pltpu.with_memory_space_constraint(lax.empty(ls, a.dtype), pltpu.HBM) for ls, a in zip(land_shapes, arrays)],
      *after)
    return outs[:n], outs[n:2 * n], outs[2 * n:3 * n], outs[3 * n:4 * n], outs[-1]


def _exchange_wait(name, send_sems, recv_sems, thru, lands, after, gather):
    n = len(thru)

    def body(*refs):
        ins, lnd = refs[:n], refs[n:2 * n]
        ssem, rsem = refs[2 * n:3 * n], refs[3 * n:4 * n]
        local_sems = refs[-1]
        me = _my_rank()
        local = []
        for a in range(n):
            cp = pltpu.make_async_copy(ins[a] if gather else ins[a].at[me], lnd[a].at[me], local_sems.at[a])
            cp.start()
            local.append(cp)
        for a in range(n):
            for k, (peer, peer_rank) in enumerate(_peers()):
                cp = _exchange_copy(gather, ins, lnd, ssem, rsem, a, k, peer, peer_rank, peer_rank)
                cp.wait_send()
                cp.wait_recv()
        for cp in local:
            cp.wait()

    outs = pl.pallas_call(
        body, name=name,
        out_shape=(*[pltpu.HBM(t.shape, t.dtype) for t in thru], *[pltpu.HBM(ld.shape, ld.dtype) for ld in lands]),
        in_specs=[*[_HBM] * (2 * n), *[_SEM] * (2 * n), pl.BlockSpec(memory_space=pl.ANY)],
        out_specs=tuple([_HBM] * (2 * n)),
        input_output_aliases={i: i for i in range(2 * n)},
        scratch_shapes=[pltpu.SemaphoreType.DMA((n,))],
        compiler_params=pltpu.CompilerParams(has_side_effects=pltpu.SideEffectType.DATAFLOW_SIDE_EFFECTING),
    )(*thru, *lands, *send_sems, *recv_sems, after)
    return outs[n:]


NN = ((1,), (0,))
NT = ((1,), (1,))
TN = ((0,), (0,))


def _matmul(name, a, b, *, contract, grid, a_spec, b_spec, out_specs, out_shapes, acc_shape,
            extra=(), extra_specs=(), epilogue=None, deps=()):
    nk = grid[2]
    n_extra = len(extra)
    n_out = len(out_shapes)

    def body(*refs):
        a_ref, b_ref = refs[0], refs[1]
        extra_refs = refs[2:2 + n_extra]
        out_refs = refs[2 + n_extra:2 + n_extra + n_out]
        acc_ref = refs[-1]
        k = pl.program_id(2)

        def prod():
            return lax.dot_general(a_ref[...].astype(BF16), b_ref[...].astype(BF16), (contract, ((), ())),
                                   preferred_element_type=F32)

        def finish(acc):
            res = (acc,) if epilogue is None else epilogue(acc, *[r[...] for r in extra_refs])
            for o_ref, val in zip(out_refs, res):
                o_ref[...] = val.astype(o_ref.dtype)

        if nk == 1:
            finish(prod())
            return

        @pl.when(k == 0)
        def _():
            acc_ref[...] = prod()

        @pl.when((k > 0) & (k < nk - 1))
        def _():
            acc_ref[...] += prod()

        @pl.when(k == nk - 1)
        def _():
            finish(acc_ref[...] + prod())

    outs = _call(
        body, deps=deps, name=name, grid=grid,
        in_specs=[a_spec, b_spec, *extra_specs], out_specs=list(out_specs), out_shape=list(out_shapes),
        scratch_shapes=[pltpu.VMEM(acc_shape if nk > 1 else (8, 128), F32)],
        compiler_params=_params(("parallel", "parallel", "arbitrary")),
    )(a, b, *extra)
    return outs


def _tile(n, t):
    t = min(n, t)
    assert n % t == 0, (n, t)
    return t


def _mm_plain(name, a, b, contract, out_dtype, tm=1024, tn=1024, tk=2048, extra=(), epilogue=None,
              n_out=1, out_dtypes=None, deps=()):
    if contract == NN:
        (m, kd), (_, n) = a.shape, b.shape
    elif contract == NT:
        (m, kd), (n, _) = a.shape, b.shape
    else:
        (kd, m), (_, n) = a.shape, b.shape
    tm, tn, tk = _tile(m, tm), _tile(n, tn), _tile(kd, tk)
    if contract == NN:
        a_spec = pl.BlockSpec((tm, tk), lambda i, j, k: (i, k))
        b_spec = pl.BlockSpec((tk, tn), lambda i, j, k: (k, j))
    elif contract == NT:
        a_spec = pl.BlockSpec((tm, tk), lambda i, j, k: (i, k))
        b_spec = pl.BlockSpec((tn, tk), lambda i, j, k: (j, k))
    else:
        a_spec = pl.BlockSpec((tk, tm), lambda i, j, k: (k, i))
        b_spec = pl.BlockSpec((tk, tn), lambda i, j, k: (k, j))
    o_spec = pl.BlockSpec((tm, tn), lambda i, j, k: (i, j))
    out_dtypes = out_dtypes or [out_dtype] * n_out
    outs = _matmul(
        name, a, b, contract=contract, grid=(m // tm, n // tn, kd // tk), a_spec=a_spec, b_spec=b_spec,
        out_specs=[o_spec] * len(out_dtypes), out_shapes=[jax.ShapeDtypeStruct((m, n), dt) for dt in out_dtypes],
        acc_shape=(tm, tn), extra=extra, extra_specs=[o_spec] * len(extra), epilogue=epilogue, deps=deps)
    return outs[0] if len(out_dtypes) == 1 else outs


def _rows_call(name, body, row_in, vec_in, row_out, vec_out, s, deps=()):
    t = _tile(s, ROW_T)
    in_specs = []
    args = []
    for arr, width, cb in row_in:
        in_specs.append(pl.BlockSpec((t, width), functools.partial(lambda i, cb: (i, cb), cb=cb)))
        args.append(arr)
    for v in vec_in:
        in_specs.append(pl.BlockSpec(v.shape, lambda i: (0, 0)))
        args.append(v)
    out_specs = []
    out_shapes = []
    for width, dt in row_out:
        out_specs.append(pl.BlockSpec((t, width), lambda i: (i, 0)))
        out_shapes.append(jax.ShapeDtypeStruct((s, width), dt))
    for width in vec_out:
        out_specs.append(pl.BlockSpec((1, width), lambda i: (0, 0)))
        out_shapes.append(jax.ShapeDtypeStruct((1, width), F32))
    return _call(body, deps=deps, name=name, grid=(s // t,), in_specs=in_specs, out_specs=out_specs,
                 out_shape=out_shapes, compiler_params=_params(("arbitrary",)))(*args)


def _acc_vec(ref, val):
    _acc_row(ref, jnp.sum(val, axis=0, keepdims=True))


def _acc_row(ref, part):
    @pl.when(pl.program_id(0) == 0)
    def _():
        ref[...] = part

    @pl.when(pl.program_id(0) > 0)
    def _():
        ref[...] += part


def _rms(v):
    return lax.rsqrt(jnp.mean(v * v, axis=-1, keepdims=True) + EPS)


def _norm_bwd(dxn, xn, r):
    return r * (dxn - xn * jnp.mean(dxn * xn, axis=-1, keepdims=True))


def _premix(x, g, scale, shift, deps=()):
    s = x.shape[0]

    def body(x_ref, g_ref, sc_ref, sh_ref, h_ref):
        xv = x_ref[...]
        h_ref[...] = ((xv * _rms(xv) * g_ref[...]) * (1.0 + sc_ref[...]) + sh_ref[...]).astype(BF16)

    return _rows_call("premix", body, [(x, D_MODEL, 0)], [g, scale, shift], [(D_MODEL, BF16)], [], s, deps)[0]


def _sigmoid(z):
    return 1.0 / (1.0 + jnp.exp(-z))


def _mix_fwd(o_fox, o_gla, pm, g_fox, g_gla):
    s = o_fox.shape[0]

    def body(of_ref, og_ref, gr_ref, gf_ref, gg_ref, mix_ref):
        for h in range(FOX_HEADS):
            sl = slice(h * FOX_HEAD_DIM, (h + 1) * FOX_HEAD_DIM)
            seg = of_ref[:, sl]
            mix_ref[:, sl] = (seg * _rms(seg) * gf_ref[:, sl]).astype(BF16)
        for h in range(GLA_HEADS):
            sl = slice(h * GLA_DV, (h + 1) * GLA_DV)
            seg = og_ref[:, sl]
            gr = gr_ref[:, sl].astype(F32)
            val = (seg * _rms(seg) * gg_ref[:, sl]) * (gr * _sigmoid(gr))
            mix_ref[:, pl.ds(FOX_HEADS * FOX_HEAD_DIM + h * GLA_DV, GLA_DV)] = val.astype(BF16)

    return _rows_call("mix_fwd", body, [(o_fox, 1024, 0), (o_gla, 1024, 0), (pm, 1024, 5)], [g_fox, g_gla],
                      [(D_MODEL, BF16)], [], s)[0]


def _mix_bwd(dmix, o_fox, o_gla, pm, g_fox, g_gla, deps=()):
    s = o_fox.shape[0]

    def body(dm_ref, of_ref, og_ref, gr_ref, gf_ref, gg_ref, dof_ref, dog_ref, dgr_ref, dgf_ref, dgg_ref):
        dgf = []
        for h in range(FOX_HEADS):
            sl = slice(h * FOX_HEAD_DIM, (h + 1) * FOX_HEAD_DIM)
            seg = of_ref[:, sl]
            r = _rms(seg)
            segn = seg * r
            dout = dm_ref[:, sl]
            dgf.append(jnp.sum(dout * segn, axis=0, keepdims=True))
            dof_ref[:, sl] = _norm_bwd(dout * gf_ref[:, sl], segn, r).astype(BF16)
        dgg = []
        for h in range(GLA_HEADS):
            sl = slice(h * GLA_DV, (h + 1) * GLA_DV)
            seg = og_ref[:, sl]
            r = _rms(seg)
            segn = seg * r
            gl = segn * gg_ref[:, sl]
            gr = gr_ref[:, sl].astype(F32)
            sig = _sigmoid(gr)
            dout = dm_ref[:, pl.ds(FOX_HEADS * FOX_HEAD_DIM + h * GLA_DV, GLA_DV)]
            dgr_ref[:, sl] = (dout * gl * (sig * (1.0 + gr * (1.0 - sig)))).astype(BF16)
            dgl = dout * (gr * sig)
            dgg.append(jnp.sum(dgl * segn, axis=0, keepdims=True))
            dog_ref[:, sl] = _norm_bwd(dgl * gg_ref[:, sl], segn, r).astype(BF16)
        _acc_row(dgf_ref, jnp.concatenate(dgf, axis=1))
        _acc_row(dgg_ref, jnp.concatenate(dgg, axis=1))

    return _rows_call("mix_bwd", body, [(dmix, D_MODEL, 0), (o_fox, 1024, 0), (o_gla, 1024, 0), (pm, 1024, 5)],
                      [g_fox, g_gla], [(1024, BF16), (1024, BF16), (1024, BF16)], [1024, 1024], s, deps)


def _postmix_premlp(x, y, gate_m, g_post_mix, g_pre_mlp, scale_f, shift_f):
    s = x.shape[0]

    def body(x_ref, y_ref, gm_ref, gpm_ref, gpl_ref, sc_ref, sh_ref, x1_ref, h2_ref):
        yv = y_ref[...]
        x1 = x_ref[...] + gm_ref[...] * (yv * _rms(yv) * gpm_ref[...])
        x1_ref[...] = x1
        h2_ref[...] = ((x1 * _rms(x1) * gpl_ref[...]) * (1.0 + sc_ref[...]) + sh_ref[...]).astype(BF16)

    return _rows_call("postmix_premlp", body, [(x, D_MODEL, 0), (y, D_MODEL, 0)],
                      [gate_m, g_post_mix, g_pre_mlp, scale_f, shift_f], [(D_MODEL, F32), (D_MODEL, BF16)], [], s)


def _loss_postmlp_bwd(x1, y2, target, gate_f, g_post_mlp):
    s = x1.shape[0]

    def body(x1_ref, y2_ref, t_ref, gf_ref, g_ref, dx2_ref, dy2_ref, loss_ref, dgate_ref, dg_ref):
        yv = y2_ref[...]
        r = _rms(yv)
        yn = yv * r
        o = yn * g_ref[...]
        e = (x1_ref[...] + gf_ref[...] * o) - t_ref[...]
        part = 0.5 * jnp.sum(jnp.mean(e * e, axis=-1, keepdims=True), axis=0, keepdims=True)
        _acc_vec(loss_ref, jnp.broadcast_to(part, (1, 128)))
        dx2 = e * (1.0 / D_MODEL)
        dx2_ref[...] = dx2
        _acc_vec(dgate_ref, dx2 * o)
        do = dx2 * gf_ref[...]
        _acc_vec(dg_ref, do * yn)
        dy2_ref[...] = _norm_bwd(do * g_ref[...], yn, r).astype(BF16)

    return _rows_call("loss_postmlp_bwd", body, [(x1, D_MODEL, 0), (y2, D_MODEL, 0), (target, D_MODEL, 0)],
                      [gate_f, g_post_mlp], [(D_MODEL, F32), (D_MODEL, BF16)], [128, D_MODEL, D_MODEL], s)


def _premlp_postmix_bwd(dh2, dx2, x1, y, scale_f, g_pre_mlp, gate_m, g_post_mix, deps=()):
    s = x1.shape[0]

    def body(dh2_ref, dx2_ref, x1_ref, y_ref, sc_ref, gpl_ref, gm_ref, gpm_ref,
             dx1_ref, dy_ref, dsc_ref, dsh_ref, dgpl_ref, dgm_ref, dgpm_ref):
        x1 = x1_ref[...]
        r1 = _rms(x1)
        x1n = x1 * r1
        dh2 = dh2_ref[...]
        _acc_vec(dsc_ref, dh2 * (x1n * gpl_ref[...]))
        _acc_vec(dsh_ref, dh2)
        dn2 = dh2 * (1.0 + sc_ref[...])
        _acc_vec(dgpl_ref, dn2 * x1n)
        dx1 = dx2_ref[...] + _norm_bwd(dn2 * gpl_ref[...], x1n, r1)
        dx1_ref[...] = dx1
        yv = y_ref[...]
        ry = _rms(yv)
        yn = yv * ry
        _acc_vec(dgm_ref, dx1 * (yn * gpm_ref[...]))
        do = dx1 * gm_ref[...]
        _acc_vec(dgpm_ref, do * yn)
        dy_ref[...] = _norm_bwd(do * gpm_ref[...], yn, ry).astype(BF16)

    return _rows_call("premlp_postmix_bwd", body,
                      [(dh2, D_MODEL, 0), (dx2, D_MODEL, 0), (x1, D_MODEL, 0), (y, D_MODEL, 0)],
                      [scale_f, g_pre_mlp, gate_m, g_post_mix], [(D_MODEL, F32), (D_MODEL, BF16)],
                      [D_MODEL] * 5, s, deps)


def _premix_bwd(dh, dx1, x, g_pre_mix, scale_m):
    s = x.shape[0]

    def body(dh_ref, dx1_ref, x_ref, g_ref, sc_ref, gx_ref, dsc_ref, dsh_ref, dg_ref):
        xv = x_ref[...]
        r = _rms(xv)
        xn = xv * r
        dh = dh_ref[...]
        _acc_vec(dsc_ref, dh * (xn * g_ref[...]))
        _acc_vec(dsh_ref, dh)
        dn1 = dh * (1.0 + sc_ref[...])
        _acc_vec(dg_ref, dn1 * xn)
        gx_ref[...] = dx1_ref[...] + _norm_bwd(dn1 * g_ref[...], xn, r)

    return _rows_call("premix_bwd", body, [(dh, D_MODEL, 0), (dx1, D_MODEL, 0), (x, D_MODEL, 0)],
                      [g_pre_mix, scale_m], [(D_MODEL, F32)], [D_MODEL] * 3, s)


def _split3(v):
    hi = v.astype(BF16)
    r1 = v - hi.astype(F32)
    mid = r1.astype(BF16)
    lo = (r1 - mid.astype(F32)).astype(BF16)
    return hi, mid, lo


def _dot_exact01(v, tri, contract=NN, tri_first=False):
    acc = None
    for part in _split3(v):
        lhs, rhs = (tri, part) if tri_first else (part, tri)
        p = lax.dot_general(lhs, rhs, (contract, ((), ())), preferred_element_type=F32)
        acc = p if acc is None else acc + p
    return acc


def _log_sigmoid(z):
    return jnp.minimum(z, 0.0) - jnp.log(1.0 + jnp.exp(-jnp.abs(z)))


def _fox_cum(small, bvec):
    s = small.shape[0]
    t = _tile(s, CUM_T)

    def body(sm_ref, b_ref, out_ref, carry):
        @pl.when(pl.program_id(0) == 0)
        def _():
            carry[...] = jnp.zeros_like(carry)

        lf = _log_sigmoid(sm_ref[...] + b_ref[...])
        lft = lf.T[0:FOX_HEADS, :]
        row = lax.broadcasted_iota(jnp.int32, (t, t), 0)
        col = lax.broadcasted_iota(jnp.int32, (t, t), 1)
        upper = (row <= col).astype(BF16)
        cum = _dot_exact01(lft, upper) + carry[:, 0:1]
        out_ref[...] = cum
        carry[...] = carry[...] + jnp.sum(lft, axis=1, keepdims=True)

    return _call(body, name="fox_cum", grid=(s // t,),
                 in_specs=[pl.BlockSpec((t, W_SMALL), lambda i: (i, 0)), pl.BlockSpec((1, W_SMALL), lambda i: (0, 0))],
                 out_specs=pl.BlockSpec((FOX_HEADS, t), lambda i: (0, i)),
                 out_shape=jax.ShapeDtypeStruct((FOX_HEADS, s), F32),
                 scratch_shapes=[pltpu.VMEM((FOX_HEADS, 128), F32)],
                 compiler_params=_params(("arbitrary",)))(small, bvec)


def _fox_cum_bwd(dc, dcq, small, bvec):
    s = small.shape[0]
    t = _tile(s, CUM_T)
    nb = s // t

    def body(dc_ref, dcq_ref, sm_ref, b_ref, out_ref, db_ref, carry):
        @pl.when(pl.program_id(0) == 0)
        def _():
            carry[...] = jnp.zeros_like(carry)
            db_ref[...] = jnp.zeros_like(db_ref)

        lane = lax.broadcasted_iota(jnp.int32, (t, W_SMALL), 1)
        dcq = jnp.zeros((t, W_SMALL), F32)
        for hh in range(FOX_HEADS):
            dcq = jnp.where(lane == hh, dcq_ref[hh], dcq)
        dcv = dc_ref[...] + dcq.T[0:FOX_HEADS, :]
        row = lax.broadcasted_iota(jnp.int32, (t, t), 0)
        col = lax.broadcasted_iota(jnp.int32, (t, t), 1)
        lower = (row >= col).astype(BF16)
        dlf = _dot_exact01(dcv, lower) + carry[:, 0:1]
        carry[...] = carry[...] + jnp.sum(dcv, axis=1, keepdims=True)
        z = sm_ref[...] + b_ref[...]
        zt = z.T[0:FOX_HEADS, :]
        dff = dlf * _sigmoid(-zt)
        db_ref[...] = db_ref[...] + jnp.sum(dff, axis=1, keepdims=True)
        full = jnp.concatenate([dff, jnp.zeros((W_SMALL - FOX_HEADS, t), F32)], axis=0)
        out_ref[...] = full.T

    return _call(body, name="fox_cum_bwd", grid=(nb,),
                 in_specs=[pl.BlockSpec((FOX_HEADS, t), lambda i: (0, nb - 1 - i)),
                           pl.BlockSpec((FOX_HEADS, t, 1), lambda i: (0, nb - 1 - i, 0)),
                           pl.BlockSpec((t, W_SMALL), lambda i: (nb - 1 - i, 0)),
                           pl.BlockSpec((1, W_SMALL), lambda i: (0, 0))],
                 out_specs=[pl.BlockSpec((t, W_SMALL), lambda i: (nb - 1 - i, 0)),
                            pl.BlockSpec((FOX_HEADS, 128), lambda i: (0, 0))],
                 out_shape=[jax.ShapeDtypeStruct((s, W_SMALL), F32), jax.ShapeDtypeStruct((FOX_HEADS, 128), F32)],
                 scratch_shapes=[pltpu.VMEM((FOX_HEADS, 128), F32)],
                 compiler_params=_params(("arbitrary",)))(dc, dcq, small, bvec)


FOX_SCALE = FOX_HEAD_DIM ** -0.5


def _fox_fwd(pm, crow):
    s = pm.shape[0]
    t = _tile(s, FOX_T)
    nb = s // t

    def body(q_ref, k_ref, v_ref, c_ref, o_ref, lse_ref, m_s, l_s, acc_s):
        i = pl.program_id(1)
        j = pl.program_id(2)

        @pl.when(j == 0)
        def _():
            m_s[...] = jnp.full_like(m_s, NEG)
            l_s[...] = jnp.zeros_like(l_s)
            acc_s[...] = jnp.zeros_like(acc_s)

        @pl.when(j <= i)
        def _():
            sc = lax.dot_general(q_ref[...], k_ref[...], (NT, ((), ())), preferred_element_type=F32)
            sc = sc * FOX_SCALE - c_ref[...]
            row = lax.broadcasted_iota(jnp.int32, (t, t), 0)
            col = lax.broadcasted_iota(jnp.int32, (t, t), 1)
            sc = jnp.where((j < i) | (row >= col), sc, NEG)
            m_prev = m_s[...]
            m_new = jnp.maximum(m_prev, jnp.max(sc, axis=1, keepdims=True))
            alpha = jnp.exp(m_prev - m_new)
            p = jnp.exp(sc - m_new)
            l_s[...] = alpha * l_s[...] + jnp.sum(p, axis=1, keepdims=True)
            p_hi = p.astype(BF16)
            p_lo = (p - p_hi.astype(F32)).astype(BF16)
            pv = jnp.dot(p_hi, v_ref[...], preferred_element_type=F32)
            pv = pv + jnp.dot(p_lo, v_ref[...], preferred_element_type=F32)
            acc_s[...] = alpha * acc_s[...] + pv
            m_s[...] = m_new

        @pl.when(j == i)
        def _():
            o_ref[...] = acc_s[...] / l_s[...]
            lse_ref[...] = m_s[...] + jnp.log(l_s[...])

    return _call(
        body, name="fox_fwd", grid=(FOX_HEADS, nb, nb),
        in_specs=[pl.BlockSpec((t, 128), lambda h, i, j: (i, h)),
                  pl.BlockSpec((t, 128), lambda h, i, j: (jnp.minimum(i, j), FOX_HEADS + h)),
                  pl.BlockSpec((t, 128), lambda h, i, j: (jnp.minimum(i, j), 2 * FOX_HEADS + h)),
                  pl.BlockSpec((None, 1, t), lambda h, i, j: (h, 0, jnp.minimum(i, j)))],
        out_specs=[pl.BlockSpec((t, 128), lambda h, i, j: (i, h)),
                   pl.BlockSpec((None, t, 1), lambda h, i, j: (h, i, 0))],
        out_shape=[jax.ShapeDtypeStruct((s, FOX_HEADS * 128), F32), jax.ShapeDtypeStruct((FOX_HEADS, s, 1), F32)],
        scratch_shapes=[pltpu.VMEM((t, 1), F32), pltpu.VMEM((t, 1), F32), pltpu.VMEM((t, 128), F32)],
        compiler_params=_params(("parallel", "parallel", "arbitrary")),
    )(pm, pm, pm, crow)


def _fox_bwd(pm, crow, o, lse, do):
    s = pm.shape[0]
    t = _tile(s, FOX_T)
    nb = s // t

    def body(q_ref, do_ref, o_ref, lse_ref, k_ref, v_ref, c_ref, dq_ref, dk_ref, dv_ref, dc_ref, dcq_ref,
             dk_acc, dv_acc, dc_acc, delta_s):
        j = pl.program_id(1)
        i = pl.program_id(2)

        @pl.when((j == 0) & (i == 0))
        def _():
            dq_ref[...] = jnp.zeros_like(dq_ref)
            dcq_ref[...] = jnp.zeros_like(dcq_ref)

        @pl.when(i == j)
        def _():
            dk_acc[...] = jnp.zeros_like(dk_acc)
            dv_acc[...] = jnp.zeros_like(dv_acc)
            dc_acc[...] = jnp.zeros_like(dc_acc)

        rows = pl.ds(pl.multiple_of(i * t, t), t)

        @pl.when(j == 0)
        def _():
            delta_s[rows, :] = jnp.sum(do_ref[...].astype(F32) * o_ref[...], axis=1, keepdims=True)

        @pl.when(i >= j)
        def _():
            q = q_ref[...]
            dov = do_ref[...]
            sc = lax.dot_general(q, k_ref[...], (NT, ((), ())), preferred_element_type=F32)
            sc = sc * FOX_SCALE - c_ref[...]
            row = lax.broadcasted_iota(jnp.int32, (t, t), 0)
            col = lax.broadcasted_iota(jnp.int32, (t, t), 1)
            p = jnp.where((i > j) | (row >= col), jnp.exp(sc - lse_ref[...]), 0.0)
            dp = lax.dot_general(dov, v_ref[...], (NT, ((), ())), preferred_element_type=F32)
            ds = p * (dp - delta_s[rows, :])
            dsb = ds.astype(BF16)
            dv_acc[...] += lax.dot_general(p.astype(BF16), dov, (TN, ((), ())), preferred_element_type=F32)
            dk_acc[...] += lax.dot_general(dsb, q, (TN, ((), ())), preferred_element_type=F32)
            dq_ref[rows, :] += jnp.dot(dsb, k_ref[...], preferred_element_type=F32) * FOX_SCALE
            dc_acc[...] -= jnp.sum(ds, axis=0, keepdims=True)
            dcq_ref[rows, :] += jnp.sum(ds, axis=1, keepdims=True)

        @pl.when(i == nb - 1)
        def _():
            dk_ref[...] = dk_acc[...] * FOX_SCALE
            dv_ref[...] = dv_acc[...]
            dc_ref[...] = dc_acc[...]

    qi = lambda h, j, i: (jnp.maximum(i, j), h)
    return _call(
        body, name="fox_bwd", grid=(FOX_HEADS, nb, nb),
        in_specs=[pl.BlockSpec((t, 128), qi), pl.BlockSpec((t, 128), qi), pl.BlockSpec((t, 128), qi),
                  pl.BlockSpec((None, t, 1), lambda h, j, i: (h, jnp.maximum(i, j), 0)),
                  pl.BlockSpec((t, 128), lambda h, j, i: (j, FOX_HEADS + h)),
                  pl.BlockSpec((t, 128), lambda h, j, i: (j, 2 * FOX_HEADS + h)),
                  pl.BlockSpec((None, 1, t), lambda h, j, i: (h, 0, j))],
        out_specs=[pl.BlockSpec((s, 128), lambda h, j, i: (0, h)),
                   pl.BlockSpec((t, 128), lambda h, j, i: (j, h)),
                   pl.BlockSpec((t, 128), lambda h, j, i: (j, h)),
                   pl.BlockSpec((None, 1, t), lambda h, j, i: (h, 0, j)),
                   pl.BlockSpec((None, s, 1), lambda h, j, i: (h, 0, 0))],
        out_shape=[jax.ShapeDtypeStruct((s, 1024), F32), jax.ShapeDtypeStruct((s, 1024), F32),
                   jax.ShapeDtypeStruct((s, 1024), F32), jax.ShapeDtypeStruct((FOX_HEADS, 1, s), F32),
                   jax.ShapeDtypeStruct((FOX_HEADS, s, 1), F32)],
        scratch_shapes=[pltpu.VMEM((t, 128), F32), pltpu.VMEM((t, 128), F32), pltpu.VMEM((1, t), F32),
                        pltpu.VMEM((s, 1), F32)],
        compiler_params=_params(("parallel", "arbitrary", "arbitrary")),
    )(pm, do, o, lse, pm, pm, crow)


GLA_SCALE = GLA_DK ** -0.5
GLA_Q_BLK = 3072 // 128
GLA_K_BLK = 3584 // 128
GLA_V_BLK = 4096 // 256


def _gla_gate(sm, wa_ref, b_ref):
    return jnp.dot(sm.astype(BF16), wa_ref[...], preferred_element_type=F32) + b_ref[...]


def _tri(n, strict):
    row = lax.broadcasted_iota(jnp.int32, (n, n), 0)
    col = lax.broadcasted_iota(jnp.int32, (n, n), 1)
    return ((row > col) if strict else (row >= col)).astype(BF16)


def _gla_fwd(pm, small, wa_pad, b_a2):
    s = pm.shape[0]
    r = _tile(s, GLA_R)
    nc = r // CHUNK

    def body(q_ref, k_ref, v_ref, sm_ref, wa_ref, b_ref, o_ref, st_ref, state):
        @pl.when(pl.program_id(1) == 0)
        def _():
            state[...] = jnp.zeros_like(state)

        tri = _tri(CHUNK, False)
        for c in range(nc):
            rows = slice(c * CHUNK, (c + 1) * CHUNK)
            la = _log_sigmoid(_gla_gate(sm_ref[rows, :], wa_ref, b_ref)) * (1.0 / GLA_TEMP)
            cum = _dot_exact01(la, tri, tri_first=True)
            total = jnp.sum(la, axis=0, keepdims=True)
            kdec = k_ref[rows, :].astype(F32) * jnp.exp(total - cum)
            ut = lax.dot_general(v_ref[rows, :], kdec.astype(BF16), (TN, ((), ())), preferred_element_type=F32)
            new = state[...] * jnp.exp(total) + ut
            state[...] = new
            newb = new.astype(BF16)
            st_ref[c] = newb
            qs = (q_ref[rows, :].astype(F32) * GLA_SCALE).astype(BF16)
            o_ref[rows, :] = lax.dot_general(qs, newb, (NT, ((), ())), preferred_element_type=F32)

    return _call(
        body, name="gla_fwd", grid=(GLA_HEADS, s // r),
        in_specs=[pl.BlockSpec((r, 128), lambda h, i: (i, GLA_Q_BLK + h)),
                  pl.BlockSpec((r, 128), lambda h, i: (i, GLA_K_BLK + h)),
                  pl.BlockSpec((r, 256), lambda h, i: (i, GLA_V_BLK + h)),
                  pl.BlockSpec((r, W_SMALL), lambda h, i: (i, 0)),
                  pl.BlockSpec((W_SMALL, 128), lambda h, i: (0, h)),
                  pl.BlockSpec((1, 128), lambda h, i: (0, h))],
        out_specs=[pl.BlockSpec((r, 256), lambda h, i: (i, h)),
                   pl.BlockSpec((nc, None, GLA_DV, GLA_DK), lambda h, i: (i, h, 0, 0))],
        out_shape=[jax.ShapeDtypeStruct((s, 1024), F32),
                   jax.ShapeDtypeStruct((s // CHUNK, GLA_HEADS, GLA_DV, GLA_DK), BF16)],
        scratch_shapes=[pltpu.VMEM((GLA_DV, GLA_DK), F32)],
        compiler_params=_params(("parallel", "arbitrary")),
    )(pm, pm, pm, small, wa_pad, b_a2)


def _gla_bwd(pm, small, wa_pad, b_a2, states, do):
    s = pm.shape[0]
    r = _tile(s, GLA_R)
    nc = r // CHUNK
    nb = s // r

    def body(q_ref, k_ref, v_ref, sm_ref, wa_ref, b_ref, do_ref, st_ref, prev_ref,
             dq_ref, dk_ref, dv_ref, dza_ref, db_ref, carry):
        step = pl.program_id(1)

        @pl.when(step == 0)
        def _():
            carry[...] = jnp.zeros_like(carry)
            db_ref[...] = jnp.zeros_like(db_ref)

        tri = _tri(CHUNK, False)
        tri_strict = _tri(CHUNK, True)
        db = jnp.zeros((1, 128), F32)
        for c in reversed(range(nc)):
            rows = slice(c * CHUNK, (c + 1) * CHUNK)
            z = _gla_gate(sm_ref[rows, :], wa_ref, b_ref)
            la = _log_sigmoid(z) * (1.0 / GLA_TEMP)
            cum = _dot_exact01(la, tri, tri_first=True)
            total = jnp.sum(la, axis=0, keepdims=True)
            w = jnp.exp(total - cum)
            decay = jnp.exp(total)
            kdec = k_ref[rows, :].astype(F32) * w
            dov = do_ref[rows, :]
            qs = (q_ref[rows, :].astype(F32) * GLA_SCALE).astype(BF16)
            dq_ref[rows, :] = jnp.dot(dov, st_ref[c], preferred_element_type=F32) * GLA_SCALE
            gt = lax.dot_general(dov, qs, (TN, ((), ())), preferred_element_type=F32) + carry[...]
            gtb = gt.astype(BF16)
            dv_ref[rows, :] = lax.dot_general(kdec.astype(BF16), gtb, (NT, ((), ())), preferred_element_type=F32)
            dkdec = jnp.dot(v_ref[rows, :], gtb, preferred_element_type=F32)
            dk_ref[rows, :] = dkdec * w
            e = dkdec * kdec
            if c > 0:
                prev = st_ref[c - 1].astype(F32)
            else:
                prev = jnp.where(step == nb - 1, 0.0, prev_ref[0].astype(F32))
            dtot = jnp.sum(gt * prev, axis=0, keepdims=True) * decay
            dla = dtot + _dot_exact01(e, tri_strict, tri_first=True)
            dza = dla * (1.0 / GLA_TEMP) * _sigmoid(-z)
            dza_ref[rows, :] = dza.astype(BF16)
            db = db + jnp.sum(dza, axis=0, keepdims=True)
            carry[...] = gt * decay
        db_ref[...] += db

    blk = lambda h, i: nb - 1 - i
    return _call(
        body, name="gla_bwd", grid=(GLA_HEADS, nb),
        in_specs=[pl.BlockSpec((r, 128), lambda h, i: (blk(h, i), GLA_Q_BLK + h)),
                  pl.BlockSpec((r, 128), lambda h, i: (blk(h, i), GLA_K_BLK + h)),
                  pl.BlockSpec((r, 256), lambda h, i: (blk(h, i), GLA_V_BLK + h)),
                  pl.BlockSpec((r, W_SMALL), lambda h, i: (blk(h, i), 0)),
                  pl.BlockSpec((W_SMALL, 128), lambda h, i: (0, h)),
                  pl.BlockSpec((1, 128), lambda h, i: (0, h)),
                  pl.BlockSpec((r, 256), lambda h, i: (blk(h, i), h)),
                  pl.BlockSpec((nc, None, GLA_DV, GLA_DK), lambda h, i: (blk(h, i), h, 0, 0)),
                  pl.BlockSpec((1, None, GLA_DV, GLA_DK),
                               lambda h, i: (jnp.maximum(blk(h, i) * nc - 1, 0), h, 0, 0))],
        out_specs=[pl.BlockSpec((r, 128), lambda h, i: (blk(h, i), h)),
                   pl.BlockSpec((r, 128), lambda h, i: (blk(h, i), h)),
                   pl.BlockSpec((r, 256), lambda h, i: (blk(h, i), h)),
                   pl.BlockSpec((r, 128), lambda h, i: (blk(h, i), h)),
                   pl.BlockSpec((1, 128), lambda h, i: (0, h))],
        out_shape=[jax.ShapeDtypeStruct((s, 512), F32), jax.ShapeDtypeStruct((s, 512), F32),
                   jax.ShapeDtypeStruct((s, 1024), F32), jax.ShapeDtypeStruct((s, 512), BF16),
                   jax.ShapeDtypeStruct((1, 512), F32)],
        scratch_shapes=[pltpu.VMEM((GLA_DV, GLA_DK), F32)],
        compiler_params=_params(("parallel", "arbitrary")),
    )(pm, pm, pm, small, wa_pad, b_a2, do, states, states)


def _modulation(c_all, w_ada):
    n = w_ada.shape[1]
    tn = _tile(n, 512)

    def body(c_ref, w_ref, out_ref, ca_ref):
        cv = c_ref[...]
        ca = cv * _sigmoid(cv)
        ca_ref[...] = ca
        out_ref[...] = jnp.dot(ca.astype(BF16), w_ref[...].astype(BF16), preferred_element_type=F32)

    return _call(body, name="modulation", grid=(n // tn,),
                 in_specs=[pl.BlockSpec((N_DEV, D_MODEL), lambda j: (0, 0)),
                           pl.BlockSpec((D_MODEL, tn), lambda j: (0, j))],
                 out_specs=[pl.BlockSpec((N_DEV, tn), lambda j: (0, j)),
                            pl.BlockSpec((N_DEV, D_MODEL), lambda j: (0, 0))],
                 out_shape=[jax.ShapeDtypeStruct((N_DEV, n), F32), jax.ShapeDtypeStruct((N_DEV, D_MODEL), F32)],
                 compiler_params=_params(("arbitrary",)))(c_all, w_ada)


def _adamw_math(w, g, m, v):
    m = ADAM_B1 * m + (1.0 - ADAM_B1) * g
    v = ADAM_B2 * v + (1.0 - ADAM_B2) * (g * g)
    m_hat = m / (1.0 - ADAM_B1 ** ADAM_STEP)
    v_hat = v / (1.0 - ADAM_B2 ** ADAM_STEP)
    delta = -ADAM_LR * (m_hat / (jnp.sqrt(v_hat) + ADAM_EPS) + ADAM_WD * w)
    return delta, m, v


def _adamw_slabs(name, w, slabs, m, v, tr=256):
    rr, cc = w.shape
    tr = _tile(rr, tr)

    def body(w_ref, s_ref, m_ref, v_ref, g_ref, d_ref, nm_ref, nv_ref):
        g = s_ref[0].astype(F32)
        for r in range(1, N_DEV):
            g = g + s_ref[r].astype(F32)
        g_ref[...] = g
        d, nm, nv = _adamw_math(w_ref[...], g, m_ref[...], v_ref[...])
        d_ref[...] = d
        nm_ref[...] = nm
        nv_ref[...] = nv

    spec = pl.BlockSpec((tr, cc), lambda i: (i, 0))
    return _call(body, name=name, grid=(rr // tr,),
                 in_specs=[spec, pl.BlockSpec((N_DEV, tr, cc), lambda i: (0, i, 0)), spec, spec],
                 out_specs=[spec] * 4, out_shape=[jax.ShapeDtypeStruct((rr, cc), F32)] * 4,
                 compiler_params=_params(("parallel",)))(w, slabs, m, v)


def _adamw_ada(w, cat, dm, m, v, tr=256):
    rr, cc = w.shape
    tr = _tile(rr, tr)

    def body(w_ref, ca_ref, dm_ref, m_ref, v_ref, g_ref, d_ref, nm_ref, nv_ref):
        g = ca_ref[:, 0:1] * dm_ref[0:1, :]
        for b in range(1, N_DEV):
            g = g + ca_ref[:, b:b + 1] * dm_ref[b:b + 1, :]
        g_ref[...] = g
        d, nm, nv = _adamw_math(w_ref[...], g, m_ref[...], v_ref[...])
        d_ref[...] = d
        nm_ref[...] = nm
        nv_ref[...] = nv

    spec = pl.BlockSpec((tr, cc), lambda i: (i, 0))
    return _call(body, name="adamw_ada", grid=(rr // tr,),
                 in_specs=[spec, pl.BlockSpec((tr, N_DEV), lambda i: (i, 0)),
                           pl.BlockSpec((N_DEV, cc), lambda i: (0, 0)), spec, spec],
                 out_specs=[spec] * 4, out_shape=[jax.ShapeDtypeStruct((rr, cc), F32)] * 4,
                 compiler_params=_params(("parallel",)))(w, cat, dm, m, v)


def _sum_devices(gathered):
    ln = gathered.shape[-1]

    def body(g_ref, out_ref):
        acc = g_ref[0]
        for r in range(1, N_DEV):
            acc = acc + g_ref[r]
        out_ref[...] = acc

    return _call(body, name="sum_devices",
                 in_specs=[pl.BlockSpec(memory_space=pltpu.VMEM)], out_specs=pl.BlockSpec(memory_space=pltpu.VMEM),
                 out_shape=jax.ShapeDtypeStruct((1, ln), F32))(gathered)


def _adamw_flat(w, g, m, v):
    def body(w_ref, g_ref, m_ref, v_ref, d_ref, nm_ref, nv_ref):
        d, nm, nv = _adamw_math(w_ref[...], g_ref[...], m_ref[...], v_ref[...])
        d_ref[...] = d
        nm_ref[...] = nm
        nv_ref[...] = nv

    vm = pl.BlockSpec(memory_space=pltpu.VMEM)
    return _call(body, name="adamw_small", in_specs=[vm] * 4, out_specs=[vm] * 3,
                 out_shape=[jax.ShapeDtypeStruct(w.shape, F32)] * 3)(w, g, m, v)


def _from_col_shards(g):
    return jnp.transpose(g, (1, 0, 2)).reshape(g.shape[1], N_DEV * g.shape[2])


def _pad_lanes(v, n):
    return jnp.concatenate([v, jnp.zeros(v.shape[:-1] + (n - v.shape[-1],), v.dtype)], axis=-1)


def kernel(x, c, w_ada, b_ada, g_pre_mix, g_post_mix, w_in, b_fgate, w_gla_a2, b_gla_a2, g_fox_out, g_gla_out, w_out, g_pre_mlp, g_post_mlp, w_mlp_in, w_mlp_out, loss_target, m_w_ada, m_b_ada, m_g_pre_mix, m_g_post_mix, m_w_in, m_b_fgate, m_w_gla_a2, m_b_gla_a2, m_g_fox_out, m_g_gla_out, m_w_out, m_g_pre_mlp, m_g_post_mlp, m_w_mlp_in, m_w_mlp_out, v_w_ada, v_b_ada, v_g_pre_mix, v_g_post_mix, v_w_in, v_b_fgate, v_w_gla_a2, v_b_gla_a2, v_g_fox_out, v_g_gla_out, v_w_out, v_g_pre_mlp, v_g_post_mlp, v_w_mlp_in, v_w_mlp_out):
    rank = _my_rank()
    xs = x[0]
    s = xs.shape[0]
    target = loss_target[0]

    c_all, wa2_g, ggla_g, win_g = _all_gather("gather_first", [c, w_gla_a2[0], g_gla_out[0], w_in[0].astype(BF16)])
    rest = [w_out[0].astype(BF16), w_mlp_in[0].astype(BF16), w_mlp_out[0].astype(BF16)]
    gs_send, gs_recv, gs_thru, gs_land, gs_token = _exchange_start("gather_rest_start", rest, gather=True,
                                                                   after=(c_all,))
    w_a2 = _from_col_shards(wa2_g)
    g_gla = _from_col_shards(ggla_g).reshape(1, 1024)
    g_fox = g_fox_out.reshape(1, 1024)
    win_full = _from_col_shards(win_g)
    w_main = jnp.concatenate([win_full[:, :3072], win_full[:, 3080:5128], win_full[:, 5144:6168]], axis=1)
    w_small = _pad_lanes(jnp.concatenate([win_full[:, 3072:3080], win_full[:, 5128:5144]], axis=1), W_SMALL)
    wa_pad =jnp.concatenate([jnp.zeros((8, 512), BF16), w_a2.astype(BF16), jnp.zeros((104, 512), BF16)], axis=0)
    bf_vec = _pad_lanes(b_fgate, W_SMALL)

    mod_part, c_act = _modulation(c_all.reshape(N_DEV, D_MODEL), w_ada[0])
    (mod_g,) = _all_gather("gather_mod", [mod_part])
    mod = lax.dynamic_slice_in_dim(mod_g, rank, 1, axis=1).reshape(1, 6 * D_MODEL) + b_ada
    shift_m, scale_m, gate_m, shift_f, scale_f, gate_f = [mod[:, i * D_MODEL:(i + 1) * D_MODEL] for i in range(6)]

    h = _premix(xs, g_pre_mix, scale_m, shift_m, deps=(gs_token,))
    pm = _mm_plain("proj_main", h, w_main, NN, BF16)
    small = _mm_plain("proj_small", h, w_small, NN, F32)
    crow = _fox_cum(small, bf_vec).reshape(FOX_HEADS, 1, s)
    o_fox, lse = _fox_fwd(pm, crow)
    o_gla, states = _gla_fwd(pm, small, wa_pad, b_gla_a2)
    mix = _mix_fwd(o_fox, o_gla, pm, g_fox, g_gla)
    wout_g, wmi_g, wmo_g = _exchange_wait("gather_rest_wait", gs_send, gs_recv, gs_thru, gs_land, mix, gather=True)
    w_out_full = wout_g.reshape(D_MODEL, D_MODEL)
    w_mo_full = wmo_g.reshape(D_FF, D_MODEL)
    y = _mm_plain("out_proj", mix, w_out_full, NN, F32)
    x1, h2 = _postmix_premlp(xs, y, gate_m, g_post_mix, g_pre_mlp, scale_f, shift_f)

    tm, tn, tk = _tile(s, 1024), 1024, 2048
    nsh = 1024 // tn

    def relu2(acc):
        rl = jnp.maximum(acc, 0.0)
        return rl * rl, rl

    z, a_relu = _matmul(
        "mlp_in", h2, wmi_g, contract=NN, grid=(s // tm, D_FF // tn, D_MODEL // tk),
        a_spec=pl.BlockSpec((tm, tk), lambda i, j, k: (i, k)),
        b_spec=pl.BlockSpec((None, tk, tn), lambda i, j, k: (j // nsh, k, j % nsh)),
        out_specs=[pl.BlockSpec((tm, tn), lambda i, j, k: (i, j))] * 2,
        out_shapes=[jax.ShapeDtypeStruct((s, D_FF), BF16)] * 2, acc_shape=(tm, tn), epilogue=relu2)
    y2 = _mm_plain("mlp_out", z, w_mo_full, NN, F32)

    dx2, dy2, loss_vec, dgate_f, dg_post_mlp = _loss_postmlp_bwd(x1, y2, target, gate_f, g_post_mlp)
    loss = lax.psum(loss_vec[0, 0], ("x", "y", "c"))

    da = _mm_plain("mlp_out_dx", dy2, w_mo_full, NT, BF16, extra=(a_relu,),
                   epilogue=lambda acc, rl: (acc * (2.0 * rl.astype(F32)),))
    dw_mo = _mm_plain("mlp_out_dw", z, dy2, TN, BF16)
    x_mo = _exchange_start("grad_mlp_out_start", [dw_mo.reshape(N_DEV, 1024, D_MODEL)], gather=False)
    tkx = 1024
    (dh2,) = _matmul(
        "mlp_in_dx", da, wmi_g, contract=NT, grid=(s // tm, D_MODEL // tn, D_FF // tkx),
        a_spec=pl.BlockSpec((tm, tkx), lambda i, j, k: (i, k)),
        b_spec=pl.BlockSpec((None, tn, tkx), lambda i, j, k: (k, j, 0)),
        out_specs=[pl.BlockSpec((tm, tn), lambda i, j, k: (i, j))],
        out_shapes=[jax.ShapeDtypeStruct((s, D_MODEL), F32)], acc_shape=(tm, tn), deps=(x_mo[4],))
    ts = _tile(s, 2048)
    (dw_mi,) = _matmul(
        "mlp_in_dw", h2, da, contract=TN, grid=(D_MODEL // 1024, D_FF // tn, s // ts),
        a_spec=pl.BlockSpec((ts, 1024), lambda i, j, k: (k, i)),
        b_spec=pl.BlockSpec((ts, tn), lambda i, j, k: (k, j)),
        out_specs=[pl.BlockSpec((None, 1024, tn), lambda i, j, k: (j // nsh, i, j % nsh))],
        out_shapes=[jax.ShapeDtypeStruct((N_DEV, D_MODEL, 1024), BF16)], acc_shape=(1024, tn))
    x_mi = _exchange_start("grad_mlp_in_start", [dw_mi], gather=False)

    dx1, dy, dscale_f, dshift_f, dg_pre_mlp, dgate_m, dg_post_mix = _premlp_postmix_bwd(
        dh2, dx2, x1, y, scale_f, g_pre_mlp, gate_m, g_post_mix, deps=(x_mi[4],))

    dmix = _mm_plain("out_proj_dx", dy, w_out_full, NT, F32)
    dw_out = _mm_plain("out_proj_dw", mix, dy, TN, BF16)
    x_out = _exchange_start("grad_out_start", [dw_out.reshape(N_DEV, 256, D_MODEL)], gather=False)
    do_fox, do_gla, dgr, dg_fox, dg_gla = _mix_bwd(dmix, o_fox, o_gla, pm, g_fox, g_gla, deps=(x_out[4],))

    dq, dk, dv, dc, dcq = _fox_bwd(pm, crow, o_fox, lse, do_fox)
    dsmall_f, db_f = _fox_cum_bwd(dc.reshape(FOX_HEADS, s), dcq, small, bf_vec)
    dgq, dgk, dgv, dza, db_a2 = _gla_bwd(pm, small, wa_pad, b_gla_a2, states, do_gla)
    dsmall = _mm_plain("gate_dx", dza, wa_pad, NT, F32, tn=128, extra=(dsmall_f,),
                       epilogue=lambda acc, other: (acc + other,))
    dwa_pad = _mm_plain("gate_dw", small, dza, TN, F32, tm=128, tn=512)

    dpm = jnp.concatenate([dq.astype(BF16), dk.astype(BF16), dv.astype(BF16), dgq.astype(BF16), dgk.astype(BF16),
                           dgv.astype(BF16), dgr], axis=1)
    dw_main = _mm_plain("proj_main_dw", h, dpm, TN, BF16)
    dw_small = _mm_plain("proj_small_dw", h, dsmall, TN, BF16, tn=128)
    dwin_full = jnp.concatenate([dw_main[:, :3072], dw_small[:, 0:8], dw_main[:, 3072:5120], dw_small[:, 8:24],
                                 dw_main[:, 5120:6144]], axis=1)
    dwin_slabs = jnp.transpose(dwin_full.reshape(D_MODEL, N_DEV, 771), (1, 0, 2))
    x_in = _exchange_start("grad_in_start", [dwin_slabs], gather=False)
    dh_small = _mm_plain("proj_small_dx", dsmall, w_small, NT, F32, tk=128)
    dh = _mm_plain("proj_main_dx", dpm, w_main, NT, F32, extra=(dh_small,),
                   epilogue=lambda acc, other: (acc + other,), deps=(x_in[4],))
    grad_x, dscale_m, dshift_m, dg_pre_mix = _premix_bwd(dh, dx1, xs, g_pre_mix, scale_m)

    dmod = jnp.concatenate([dshift_m, dscale_m, dgate_m, dshift_f, dscale_f, dgate_f], axis=1)
    flat = jnp.concatenate(
        [dmod, dg_pre_mix, dg_post_mix, dg_fox, dg_pre_mlp, dg_post_mlp, db_a2,
         dwa_pad[8:24, :].reshape(1, GLA_RANK * 512), dg_gla, _pad_lanes(db_f[:, 0].reshape(1, FOX_HEADS), 128)],
        axis=1)
    (flat_g,) = _all_gather("gather_small_grads", [flat])
    tot = _sum_devices(flat_g)

    (r_mo,) = _exchange_wait("grad_mlp_out_wait", *x_mo[:4], grad_x, gather=False)
    g_mo, d_mo, nm_mo, nv_mo = _adamw_slabs("adamw_w_mlp_out", w_mlp_out[0], r_mo, m_w_mlp_out[0], v_w_mlp_out[0])
    (r_mi,) = _exchange_wait("grad_mlp_in_wait", *x_mi[:4], g_mo, gather=False)
    g_mi, d_mi, nm_mi, nv_mi = _adamw_slabs("adamw_w_mlp_in", w_mlp_in[0], r_mi, m_w_mlp_in[0], v_w_mlp_in[0])
    (r_out,) = _exchange_wait("grad_out_wait", *x_out[:4], g_mi, gather=False)
    g_out, d_out, nm_out, nv_out = _adamw_slabs("adamw_w_out", w_out[0], r_out, m_w_out[0], v_w_out[0])

    dm_cols = lax.dynamic_slice_in_dim(flat_g[:, 0, :6 * D_MODEL], rank * 1536, 1536, axis=1)
    g_ada, d_ada, nm_ada, nv_ada = _adamw_ada(w_ada[0], c_act.T, dm_cols, m_w_ada[0], v_w_ada[0])
    (r_in,) = _exchange_wait("grad_in_wait", *x_in[:4], g_ada, gather=False)
    g_in, d_in, nm_in, nv_in = _adamw_slabs("adamw_w_in", w_in[0], r_in, m_w_in[0], v_w_in[0])

    o = 0
    seg = {}
    for name, n in (("b_ada", 12288), ("g_pre_mix", 2048), ("g_post_mix", 2048), ("g_fox_out", 1024),
                    ("g_pre_mlp", 2048), ("g_post_mlp", 2048), ("b_gla_a2", 512), ("w_gla_a2", 8192),
                    ("g_gla_out", 1024), ("b_fgate", 128)):
        seg[name] = tot[:, o:o + n]
        o += n
    g_wa2 = lax.dynamic_slice_in_dim(seg["w_gla_a2"].reshape(GLA_RANK, 512), rank * 64, 64, axis=1)
    g_ggla = lax.dynamic_slice_in_dim(seg["g_gla_out"].reshape(GLA_HEADS, GLA_DV), rank * 32, 32, axis=1)
    small_names = ["b_ada", "g_pre_mix", "g_post_mix", "g_fox_out", "g_pre_mlp", "g_post_mlp", "b_gla_a2",
                   "w_gla_a2", "g_gla_out", "b_fgate"]
    small_grads = {**seg, "w_gla_a2": g_wa2.reshape(1, 1024), "g_gla_out": g_ggla.reshape(1, 128)}
    weights = dict(b_ada=b_ada, g_pre_mix=g_pre_mix, g_post_mix=g_post_mix, g_fox_out=g_fox_out,
                   g_pre_mlp=g_pre_mlp, g_post_mlp=g_post_mlp, b_gla_a2=b_gla_a2, w_gla_a2=w_gla_a2,
                   g_gla_out=g_gla_out, b_fgate=b_fgate)
    moms = dict(b_ada=m_b_ada, g_pre_mix=m_g_pre_mix, g_post_mix=m_g_post_mix, g_fox_out=m_g_fox_out,
                g_pre_mlp=m_g_pre_mlp, g_post_mlp=m_g_post_mlp, b_gla_a2=m_b_gla_a2, w_gla_a2=m_w_gla_a2,
                g_gla_out=m_g_gla_out, b_fgate=m_b_fgate)
    vels = dict(b_ada=v_b_ada, g_pre_mix=v_g_pre_mix, g_post_mix=v_g_post_mix, g_fox_out=v_g_fox_out,
                g_pre_mlp=v_g_pre_mlp, g_post_mlp=v_g_post_mlp, b_gla_a2=v_b_gla_a2, w_gla_a2=v_w_gla_a2,
                g_gla_out=v_g_gla_out, b_fgate=v_b_fgate)

    def flatten(d, fill):
        parts = []
        for nm in small_names:
            p = d[nm].reshape(1, -1)
            if nm == "b_fgate":
                p = jnp.concatenate([p[:, :FOX_HEADS], jnp.full((1, 128 - FOX_HEADS), fill, F32)], axis=1)
            parts.append(p)
        return jnp.concatenate(parts, axis=1).reshape(-1, 128)

    fw, fg, fm, fv = flatten(weights, 0.0), flatten(small_grads, 0.0), flatten(moms, 0.0), flatten(vels, 1.0)
    fd, fnm, fnv = _adamw_flat(fw, fg, fm, fv)

    def unflatten(fl):
        fl = fl.reshape(1, -1)
        out = {}
        o = 0
        for nm in small_names:
            n = 128 if nm == "b_fgate" else weights[nm].size
            piece = fl[:, o:o + n]
            if nm == "b_fgate":
                piece = piece[:, :FOX_HEADS]
            out[nm] = piece.reshape(weights[nm].shape)
            o += n
        return out

    sg, sd, snm, snv = unflatten(fg), unflatten(fd), unflatten(fnm), unflatten(fnv)

    big = dict(w_ada=(g_ada, d_ada, nm_ada, nv_ada), w_in=(g_in, d_in, nm_in, nv_in),
               w_out=(g_out, d_out, nm_out, nv_out), w_mlp_in=(g_mi, d_mi, nm_mi, nv_mi),
               w_mlp_out=(g_mo, d_mo, nm_mo, nv_mo))
    order = ["w_ada", "b_ada", "g_pre_mix", "g_post_mix", "w_in", "b_fgate", "w_gla_a2", "b_gla_a2", "g_fox_out",
             "g_gla_out", "w_out", "g_pre_mlp", "g_post_mlp", "w_mlp_in", "w_mlp_out"]

    def pick(nm, idx):
        if nm in big:
            return big[nm][idx][None]
        return (sg, sd, snm, snv)[idx][nm]

    grads = [pick(nm, 0) for nm in order]
    deltas = [pick(nm, 1) for nm in order]
    new_m = [pick(nm, 2) for nm in order]
    new_v = [pick(nm, 3) for nm in order]
    return (loss, grad_x[None], *grads, *deltas, *new_m, *new_v)
```

```python
import functools

import numpy as np
import jax
import jax.numpy as jnp
from jax import lax
from jax.experimental import pallas as pl
from jax.experimental.pallas import tpu as pltpu

F32 = jnp.float32
BF16 = jnp.bfloat16
MESH = pl.DeviceIdType.MESH
N_DEV = 8

D_MODEL = 2048
FOX_HEADS = 8
FOX_HEAD_DIM = 128
GLA_HEADS = 4
GLA_DK = 128
GLA_DV = 256
GLA_RANK = 16
GLA_TEMP = 16.0
CHUNK = 64
D_FF = 8192
W_MAIN = 6144
W_SMALL = 128
EPS = 1e-6
NEG = float(np.finfo(np.float32).min)

ADAM_LR = 0.001
ADAM_B1 = 0.9
ADAM_B2 = 0.999
ADAM_EPS = 1e-08
ADAM_WD = 0.01
ADAM_STEP = 10

ROW_T = 256
FOX_T = 512
GLA_R = 512
CUM_T = 256
VMEM_LIMIT = 56 * 1024 * 1024


def _call(body, deps=(), **kw):
    if not deps:
        return pl.pallas_call(body, **kw)
    n_in, n_dep = len(kw["in_specs"]), len(deps)

    def with_deps(*refs):
        return body(*refs[:n_in], *refs[n_in + n_dep:])

    kw["in_specs"] = [*kw["in_specs"], *[pl.BlockSpec(memory_space=pl.ANY)] * n_dep]
    call = pl.pallas_call(with_deps, **kw)
    return lambda *args: call(*args, *deps)


def _params(sem=None):
    return pltpu.CompilerParams(dimension_semantics=sem, vmem_limit_bytes=VMEM_LIMIT)


def _my_pos():
    return lax.axis_index("x"), lax.axis_index("y"), lax.axis_index("c")


def _my_rank():
    x, y, c = _my_pos()
    return 4 * x + 2 * y + c


def _all_gather(name, arrays):
    n = len(arrays)

    def body(*refs):
        ins = refs[:n]
        outs = refs[n:2 * n]
        send_sems, recv_sems, local_sems = refs[2 * n:]
        x, y, c = _my_pos()
        me, sibling = (x, y, c), (x, y, 1 - c)
        chips = [(1 - x, y), (x, 1 - y), (1 - x, 1 - y)]

        def slot(a, px, py, pc):
            return outs[a].at[4 * px + 2 * py + pc]

        def copy(a, k, block, to, src=None):
            return pltpu.make_async_remote_copy(
                src_ref=slot(a, *block) if src is None else src, dst_ref=slot(a, *block),
                send_sem=send_sems.at[a, k], recv_sem=recv_sems.at[a, k],
                device_id=to, device_id_type=MESH)

        started = []
        for a in range(n):
            mine = pltpu.make_async_copy(ins[a], slot(a, *me), local_sems.at[a])
            mine.start()
            started.append(mine)
        first = []
        for a in range(n):
            first.append(copy(a, 0, me, sibling, src=ins[a]))
            first += [copy(a, 1 + j, me, (*chip, c), src=ins[a]) for j, chip in enumerate(chips)]
        for cp in first:
            cp.start()
        passed = []
        for j, chip in enumerate(chips):
            for a in range(n):
                copy(a, 1 + j, (*chip, c), me).wait_recv()
                fwd = copy(a, 4 + j, (*chip, c), sibling)
                fwd.start()
                passed.append(fwd)
        for a in range(n):
            copy(a, 0, sibling, me).wait_recv()
            for j, chip in enumerate(chips):
                copy(a, 4 + j, (*chip, 1 - c), me).wait_recv()
        for cp in first + passed:
            cp.wait_send()
        for mine in started:
            mine.wait()

    hbm = pl.BlockSpec(memory_space=pltpu.HBM)
    return _call(
        body, name=name,
        out_shape=[jax.ShapeDtypeStruct((N_DEV,) + a.shape, a.dtype) for a in arrays],
        in_specs=[hbm] * n, out_specs=[hbm] * n,
        scratch_shapes=[pltpu.SemaphoreType.DMA((n, 7)), pltpu.SemaphoreType.DMA((n, 7)),
                        pltpu.SemaphoreType.DMA((n,))],
    )(*arrays)


def _own_slot(name, src, gather, rank):
    shape = ((N_DEV,) + src.shape) if gather else src.shape
    rr, cc = shape[1], shape[2]
    tr = _tile(rr, 512)

    def body(rank_ref, s_ref, o_ref):
        o_ref[...] = s_ref[...].astype(o_ref.dtype)

    if gather:
        in_spec = pl.BlockSpec((tr, cc), lambda i, rk: (i, 0))
    else:
        in_spec = pl.BlockSpec((None, tr, cc), lambda i, rk: (rk[0], i, 0))
    grid_spec = pltpu.PrefetchScalarGridSpec(
        num_scalar_prefetch=1, grid=(rr // tr,), in_specs=[in_spec],
        out_specs=pl.BlockSpec((None, tr, cc), lambda i, rk: (rk[0], i, 0)))
    return _call(body, name=name, grid_spec=grid_spec, out_shape=jax.ShapeDtypeStruct(shape, BF16),
                 compiler_params=_params(("arbitrary",)))(jnp.reshape(rank, (1,)).astype(jnp.int32), src)


_HBM = pl.BlockSpec(memory_space=pltpu.HBM)
_SEM = pl.BlockSpec(memory_space=pltpu.SEMAPHORE)
_FLIPS = [(kx, ky, kc) for kx in (0, 1) for ky in (0, 1) for kc in (0, 1)][1:]


def _peers():
    x, y, c = _my_pos()
    out = []
    for kx, ky, kc in _FLIPS:
        px, py, pc = (1 - x if kx else x), (1 - y if ky else y), (1 - c if kc else c)
        out.append(((px, py, pc), 4 * px + 2 * py + pc))
    return out


def _exchange_copy(srcs, lands, send_sems, recv_sems, a, k, peer, peer_rank, slot):
    return pltpu.make_async_remote_copy(
        src_ref=lands[a].at[slot] if srcs is None else srcs[a].at[peer_rank],
        dst_ref=lands[a].at[slot],
        send_sem=send_sems[a].at[k], recv_sem=recv_sems[a].at[k],
        device_id=peer, device_id_type=MESH)


def _exchange_start(name, lands, srcs=None, after=()):
    n = len(lands)
    n_src = 0 if srcs is None else n
    n_in = n + n_src + len(after)

    def body(*refs):
        lnd = refs[:n]
        src = None if srcs is None else refs[n:2 * n]
        send_sems, recv_sems = refs[n_in:n_in + n], refs[n_in + n:n_in + 2 * n]
        token = refs[-1]
        me = _my_rank()
        for a in range(n):
            for k, (peer, peer_rank) in enumerate(_peers()):
                _exchange_copy(src, lnd, send_sems, recv_sems, a, k, peer, peer_rank, me).start()
        token[...] = jnp.zeros_like(token)

    sems = [pltpu.SemaphoreType.DMA((7,))] * (2 * n)
    thru = list(lands) + ([] if srcs is None else list(srcs))
    outs = pl.pallas_call(
        body, name=name,
        out_shape=(*sems, *[pltpu.HBM(t.shape, t.dtype) for t in thru], jax.ShapeDtypeStruct((8, 128), F32)),
        in_specs=[*[_HBM] * len(thru), *[pl.BlockSpec(memory_space=pl.ANY)] * len(after)],
        out_specs=(*[_SEM] * (2 * n), *[_HBM] * len(thru), pl.BlockSpec(memory_space=pltpu.VMEM)),
        input_output_aliases={i: 2 * n + i for i in range(len(thru))},
        compiler_params=pltpu.CompilerParams(has_side_effects=pltpu.SideEffectType.DATAFLOW_SIDE_EFFECTING),
    )(*[pltpu.with_memory_space_constraint(t, pltpu.HBM) for t in thru], *after)
    lands_thru = outs[2 * n:3 * n]
    srcs_thru = None if srcs is None else outs[3 * n:4 * n]
    return outs[:n], outs[n:2 * n], srcs_thru, lands_thru, outs[-1]


def _exchange_wait(name, send_sems, recv_sems, srcs, lands, after):
    n = len(lands)
    thru = list(lands) + ([] if srcs is None else list(srcs))

    def body(*refs):
        lnd = refs[:n]
        src = None if srcs is None else refs[n:2 * n]
        ssem, rsem = refs[len(thru):len(thru) + n], refs[len(thru) + n:len(thru) + 2 * n]
        for a in range(n):
            for k, (peer, peer_rank) in enumerate(_peers()):
                cp = _exchange_copy(src, lnd, ssem, rsem, a, k, peer, peer_rank, peer_rank)
                cp.wait_send()
                cp.wait_recv()

    outs = pl.pallas_call(
        body, name=name,
        out_shape=tuple(pltpu.HBM(t.shape, t.dtype) for t in thru),
        in_specs=[*[_HBM] * len(thru), *[_SEM] * (2 * n), pl.BlockSpec(memory_space=pl.ANY)],
        out_specs=tuple([_HBM] * len(thru)),
        input_output_aliases={i: i for i in range(len(thru))},
        compiler_params=pltpu.CompilerParams(has_side_effects=pltpu.SideEffectType.DATAFLOW_SIDE_EFFECTING),
    )(*thru, *send_sems, *recv_sems, after)
    return outs[:n]


NN = ((1,), (0,))
NT = ((1,), (1,))
TN = ((0,), (0,))


def _matmul(name, a, b, *, contract, grid, a_spec, b_spec, out_specs, out_shapes, acc_shape,
            extra=(), extra_specs=(), epilogue=None, deps=()):
    nk = grid[2]
    n_extra = len(extra)
    n_out = len(out_shapes)

    def body(*refs):
        a_ref, b_ref = refs[0], refs[1]
        extra_refs = refs[2:2 + n_extra]
        out_refs = refs[2 + n_extra:2 + n_extra + n_out]
        acc_ref = refs[-1]
        k = pl.program_id(2)

        def prod():
            return lax.dot_general(a_ref[...].astype(BF16), b_ref[...].astype(BF16), (contract, ((), ())),
                                   preferred_element_type=F32)

        def finish(acc):
            res = (acc,) if epilogue is None else epilogue(acc, *[r[...] for r in extra_refs])
            for o_ref, val in zip(out_refs, res):
                o_ref[...] = val.astype(o_ref.dtype)

        if nk == 1:
            finish(prod())
            return

        @pl.when(k == 0)
        def _():
            acc_ref[...] = prod()

        @pl.when((k > 0) & (k < nk - 1))
        def _():
            acc_ref[...] += prod()

        @pl.when(k == nk - 1)
        def _():
            finish(acc_ref[...] + prod())

    outs = _call(
        body, deps=deps, name=name, grid=grid,
        in_specs=[a_spec, b_spec, *extra_specs], out_specs=list(out_specs), out_shape=list(out_shapes),
        scratch_shapes=[pltpu.VMEM(acc_shape if nk > 1 else (8, 128), F32)],
        compiler_params=_params(("parallel", "parallel", "arbitrary")),
    )(a, b, *extra)
    return outs


def _tile(n, t):
    t = min(n, t)
    assert n % t == 0, (n, t)
    return t


def _mm_plain(name, a, b, contract, out_dtype, tm=1024, tn=1024, tk=2048, extra=(), epilogue=None,
              n_out=1, out_dtypes=None, deps=()):
    if contract == NN:
        (m, kd), (_, n) = a.shape, b.shape
    elif contract == NT:
        (m, kd), (n, _) = a.shape, b.shape
    else:
        (kd, m), (_, n) = a.shape, b.shape
    tm, tn, tk = _tile(m, tm), _tile(n, tn), _tile(kd, tk)
    if contract == NN:
        a_spec = pl.BlockSpec((tm, tk), lambda i, j, k: (i, k))
        b_spec = pl.BlockSpec((tk, tn), lambda i, j, k: (k, j))
    elif contract == NT:
        a_spec = pl.BlockSpec((tm, tk), lambda i, j, k: (i, k))
        b_spec = pl.BlockSpec((tn, tk), lambda i, j, k: (j, k))
    else:
        a_spec = pl.BlockSpec((tk, tm), lambda i, j, k: (k, i))
        b_spec = pl.BlockSpec((tk, tn), lambda i, j, k: (k, j))
    o_spec = pl.BlockSpec((tm, tn), lambda i, j, k: (i, j))
    out_dtypes = out_dtypes or [out_dtype] * n_out
    outs = _matmul(
        name, a, b, contract=contract, grid=(m // tm, n // tn, kd // tk), a_spec=a_spec, b_spec=b_spec,
        out_specs=[o_spec] * len(out_dtypes), out_shapes=[jax.ShapeDtypeStruct((m, n), dt) for dt in out_dtypes],
        acc_shape=(tm, tn), extra=extra, extra_specs=[o_spec] * len(extra), epilogue=epilogue, deps=deps)
    return outs[0] if len(out_dtypes) == 1 else outs


def _rows_call(name, body, row_in, vec_in, row_out, vec_out, s, deps=()):
    t = _tile(s, ROW_T)
    in_specs = []
    args = []
    for arr, width, cb in row_in:
        in_specs.append(pl.BlockSpec((t, width), functools.partial(lambda i, cb: (i, cb), cb=cb)))
        args.append(arr)
    for v in vec_in:
        in_specs.append(pl.BlockSpec(v.shape, lambda i: (0, 0)))
        args.append(v)
    out_specs = []
    out_shapes = []
    for width, dt in row_out:
        out_specs.append(pl.BlockSpec((t, width), lambda i: (i, 0)))
        out_shapes.append(jax.ShapeDtypeStruct((s, width), dt))
    for width in vec_out:
        out_specs.append(pl.BlockSpec((1, width), lambda i: (0, 0)))
        out_shapes.append(jax.ShapeDtypeStruct((1, width), F32))
    return _call(body, deps=deps, name=name, grid=(s // t,), in_specs=in_specs, out_specs=out_specs,
                 out_shape=out_shapes, compiler_params=_params(("arbitrary",)))(*args)


def _acc_vec(ref, val):
    _acc_row(ref, jnp.sum(val, axis=0, keepdims=True))


def _acc_row(ref, part):
    @pl.when(pl.program_id(0) == 0)
    def _():
        ref[...] = part

    @pl.when(pl.program_id(0) > 0)
    def _():
        ref[...] += part


def _rms(v):
    return lax.rsqrt(jnp.mean(v * v, axis=-1, keepdims=True) + EPS)


def _norm_bwd(dxn, xn, r):
    return r * (dxn - xn * jnp.mean(dxn * xn, axis=-1, keepdims=True))


def _premix(x, g, scale, shift, deps=()):
    s = x.shape[0]

    def body(x_ref, g_ref, sc_ref, sh_ref, h_ref):
        xv = x_ref[...]
        h_ref[...] = ((xv * _rms(xv) * g_ref[...]) * (1.0 + sc_ref[...]) + sh_ref[...]).astype(BF16)

    return _rows_call("premix", body, [(x, D_MODEL, 0)], [g, scale, shift], [(D_MODEL, BF16)], [], s, deps)[0]


def _sigmoid(z):
    return 1.0 / (1.0 + jnp.exp(-z))


def _mix_fwd(o_fox, o_gla, pm, g_fox, g_gla):
    s = o_fox.shape[0]

    def body(of_ref, og_ref, gr_ref, gf_ref, gg_ref, mix_ref):
        for h in range(FOX_HEADS):
            sl = slice(h * FOX_HEAD_DIM, (h + 1) * FOX_HEAD_DIM)
            seg = of_ref[:, sl]
            mix_ref[:, sl] = (seg * _rms(seg) * gf_ref[:, sl]).astype(BF16)
        for h in range(GLA_HEADS):
            sl = slice(h * GLA_DV, (h + 1) * GLA_DV)
            seg = og_ref[:, sl]
            gr = gr_ref[:, sl].astype(F32)
            val = (seg * _rms(seg) * gg_ref[:, sl]) * (gr * _sigmoid(gr))
            mix_ref[:, pl.ds(FOX_HEADS * FOX_HEAD_DIM + h * GLA_DV, GLA_DV)] = val.astype(BF16)

    return _rows_call("mix_fwd", body, [(o_fox, 1024, 0), (o_gla, 1024, 0), (pm, 1024, 5)], [g_fox, g_gla],
                      [(D_MODEL, BF16)], [], s)[0]


def _mix_bwd(dmix, o_fox, o_gla, pm, g_fox, g_gla, deps=()):
    s = o_fox.shape[0]

    def body(dm_ref, of_ref, og_ref, gr_ref, gf_ref, gg_ref, dof_ref, dog_ref, dgr_ref, dgf_ref, dgg_ref):
        dgf = []
        for h in range(FOX_HEADS):
            sl = slice(h * FOX_HEAD_DIM, (h + 1) * FOX_HEAD_DIM)
            seg = of_ref[:, sl]
            r = _rms(seg)
            segn = seg * r
            dout = dm_ref[:, sl]
            dgf.append(jnp.sum(dout * segn, axis=0, keepdims=True))
            dof_ref[:, sl] = _norm_bwd(dout * gf_ref[:, sl], segn, r).astype(BF16)
        dgg = []
        for h in range(GLA_HEADS):
            sl = slice(h * GLA_DV, (h + 1) * GLA_DV)
            seg = og_ref[:, sl]
            r = _rms(seg)
            segn = seg * r
            gl = segn * gg_ref[:, sl]
            gr = gr_ref[:, sl].astype(F32)
            sig = _sigmoid(gr)
            dout = dm_ref[:, pl.ds(FOX_HEADS * FOX_HEAD_DIM + h * GLA_DV, GLA_DV)]
            dgr_ref[:, sl] = (dout * gl * (sig * (1.0 + gr * (1.0 - sig)))).astype(BF16)
            dgl = dout * (gr * sig)
            dgg.append(jnp.sum(dgl * segn, axis=0, keepdims=True))
            dog_ref[:, sl] = _norm_bwd(dgl * gg_ref[:, sl], segn, r).astype(BF16)
        _acc_row(dgf_ref, jnp.concatenate(dgf, axis=1))
        _acc_row(dgg_ref, jnp.concatenate(dgg, axis=1))

    return _rows_call("mix_bwd", body, [(dmix, D_MODEL, 0), (o_fox, 1024, 0), (o_gla, 1024, 0), (pm, 1024, 5)],
                      [g_fox, g_gla], [(1024, BF16), (1024, BF16), (1024, BF16)], [1024, 1024], s, deps)


def _postmix_premlp(x, y, gate_m, g_post_mix, g_pre_mlp, scale_f, shift_f):
    s = x.shape[0]

    def body(x_ref, y_ref, gm_ref, gpm_ref, gpl_ref, sc_ref, sh_ref, x1_ref, h2_ref):
        yv = y_ref[...]
        x1 = x_ref[...] + gm_ref[...] * (yv * _rms(yv) * gpm_ref[...])
        x1_ref[...] = x1
        h2_ref[...] = ((x1 * _rms(x1) * gpl_ref[...]) * (1.0 + sc_ref[...]) + sh_ref[...]).astype(BF16)

    return _rows_call("postmix_premlp", body, [(x, D_MODEL, 0), (y, D_MODEL, 0)],
                      [gate_m, g_post_mix, g_pre_mlp, scale_f, shift_f], [(D_MODEL, F32), (D_MODEL, BF16)], [], s)


def _loss_postmlp_bwd(x1, y2, target, gate_f, g_post_mlp):
    s = x1.shape[0]

    def body(x1_ref, y2_ref, t_ref, gf_ref, g_ref, dx2_ref, dy2_ref, loss_ref, dgate_ref, dg_ref):
        yv = y2_ref[...]
        r = _rms(yv)
        yn = yv * r
        o = yn * g_ref[...]
        e = (x1_ref[...] + gf_ref[...] * o) - t_ref[...]
        part = 0.5 * jnp.sum(jnp.mean(e * e, axis=-1, keepdims=True), axis=0, keepdims=True)
        _acc_vec(loss_ref, jnp.broadcast_to(part, (1, 128)))
        dx2 = e * (1.0 / D_MODEL)
        dx2_ref[...] = dx2
        _acc_vec(dgate_ref, dx2 * o)
        do = dx2 * gf_ref[...]
        _acc_vec(dg_ref, do * yn)
        dy2_ref[...] = _norm_bwd(do * g_ref[...], yn, r).astype(BF16)

    return _rows_call("loss_postmlp_bwd", body, [(x1, D_MODEL, 0), (y2, D_MODEL, 0), (target, D_MODEL, 0)],
                      [gate_f, g_post_mlp], [(D_MODEL, F32), (D_MODEL, BF16)], [128, D_MODEL, D_MODEL], s)


def _premlp_postmix_bwd(dh2, dx2, x1, y, scale_f, g_pre_mlp, gate_m, g_post_mix, deps=()):
    s = x1.shape[0]

    def body(dh2_ref, dx2_ref, x1_ref, y_ref, sc_ref, gpl_ref, gm_ref, gpm_ref,
             dx1_ref, dy_ref, dsc_ref, dsh_ref, dgpl_ref, dgm_ref, dgpm_ref):
        x1 = x1_ref[...]
        r1 = _rms(x1)
        x1n = x1 * r1
        dh2 = dh2_ref[...]
        _acc_vec(dsc_ref, dh2 * (x1n * gpl_ref[...]))
        _acc_vec(dsh_ref, dh2)
        dn2 = dh2 * (1.0 + sc_ref[...])
        _acc_vec(dgpl_ref, dn2 * x1n)
        dx1 = dx2_ref[...] + _norm_bwd(dn2 * gpl_ref[...], x1n, r1)
        dx1_ref[...] = dx1
        yv = y_ref[...]
        ry = _rms(yv)
        yn = yv * ry
        _acc_vec(dgm_ref, dx1 * (yn * gpm_ref[...]))
        do = dx1 * gm_ref[...]
        _acc_vec(dgpm_ref, do * yn)
        dy_ref[...] = _norm_bwd(do * gpm_ref[...], yn, ry).astype(BF16)

    return _rows_call("premlp_postmix_bwd", body,
                      [(dh2, D_MODEL, 0), (dx2, D_MODEL, 0), (x1, D_MODEL, 0), (y, D_MODEL, 0)],
                      [scale_f, g_pre_mlp, gate_m, g_post_mix], [(D_MODEL, F32), (D_MODEL, BF16)],
                      [D_MODEL] * 5, s, deps)


def _premix_bwd(dh, dx1, x, g_pre_mix, scale_m):
    s = x.shape[0]

    def body(dh_ref, dx1_ref, x_ref, g_ref, sc_ref, gx_ref, dsc_ref, dsh_ref, dg_ref):
        xv = x_ref[...]
        r = _rms(xv)
        xn = xv * r
        dh = dh_ref[...]
        _acc_vec(dsc_ref, dh * (xn * g_ref[...]))
        _acc_vec(dsh_ref, dh)
        dn1 = dh * (1.0 + sc_ref[...])
        _acc_vec(dg_ref, dn1 * xn)
        gx_ref[...] = dx1_ref[...] + _norm_bwd(dn1 * g_ref[...], xn, r)

    return _rows_call("premix_bwd", body, [(dh, D_MODEL, 0), (dx1, D_MODEL, 0), (x, D_MODEL, 0)],
                      [g_pre_mix, scale_m], [(D_MODEL, F32)], [D_MODEL] * 3, s)


def _split3(v):
    hi = v.astype(BF16)
    r1 = v - hi.astype(F32)
    mid = r1.astype(BF16)
    lo = (r1 - mid.astype(F32)).astype(BF16)
    return hi, mid, lo


def _dot_exact01(v, tri, contract=NN, tri_first=False):
    acc = None
    for part in _split3(v):
        lhs, rhs = (tri, part) if tri_first else (part, tri)
        p = lax.dot_general(lhs, rhs, (contract, ((), ())), preferred_element_type=F32)
        acc = p if acc is None else acc + p
    return acc


def _log_sigmoid(z):
    return jnp.minimum(z, 0.0) - jnp.log(1.0 + jnp.exp(-jnp.abs(z)))


def _fox_cum(small, bvec):
    s = small.shape[0]
    t = _tile(s, CUM_T)

    def body(sm_ref, b_ref, out_ref, carry):
        @pl.when(pl.program_id(0) == 0)
        def _():
            carry[...] = jnp.zeros_like(carry)

        lf = _log_sigmoid(sm_ref[...] + b_ref[...])
        lft = lf.T[0:FOX_HEADS, :]
        row = lax.broadcasted_iota(jnp.int32, (t, t), 0)
        col = lax.broadcasted_iota(jnp.int32, (t, t), 1)
        upper = (row <= col).astype(BF16)
        cum = _dot_exact01(lft, upper) + carry[:, 0:1]
        out_ref[...] = cum
        carry[...] = carry[...] + jnp.sum(lft, axis=1, keepdims=True)

    return _call(body, name="fox_cum", grid=(s // t,),
                 in_specs=[pl.BlockSpec((t, W_SMALL), lambda i: (i, 0)), pl.BlockSpec((1, W_SMALL), lambda i: (0, 0))],
                 out_specs=pl.BlockSpec((FOX_HEADS, t), lambda i: (0, i)),
                 out_shape=jax.ShapeDtypeStruct((FOX_HEADS, s), F32),
                 scratch_shapes=[pltpu.VMEM((FOX_HEADS, 128), F32)],
                 compiler_params=_params(("arbitrary",)))(small, bvec)


def _fox_cum_bwd(dc, dcq, small, bvec):
    s = small.shape[0]
    t = _tile(s, CUM_T)
    nb = s // t

    def body(dc_ref, dcq_ref, sm_ref, b_ref, out_ref, db_ref, carry):
        @pl.when(pl.program_id(0) == 0)
        def _():
            carry[...] = jnp.zeros_like(carry)
            db_ref[...] = jnp.zeros_like(db_ref)

        lane = lax.broadcasted_iota(jnp.int32, (t, W_SMALL), 1)
        dcq = jnp.zeros((t, W_SMALL), F32)
        for hh in range(FOX_HEADS):
            dcq = jnp.where(lane == hh, dcq_ref[hh], dcq)
        dcv = dc_ref[...] + dcq.T[0:FOX_HEADS, :]
        row = lax.broadcasted_iota(jnp.int32, (t, t), 0)
        col = lax.broadcasted_iota(jnp.int32, (t, t), 1)
        lower = (row >= col).astype(BF16)
        dlf = _dot_exact01(dcv, lower) + carry[:, 0:1]
        carry[...] = carry[...] + jnp.sum(dcv, axis=1, keepdims=True)
        z = sm_ref[...] + b_ref[...]
        zt = z.T[0:FOX_HEADS, :]
        dff = dlf * _sigmoid(-zt)
        db_ref[...] = db_ref[...] + jnp.sum(dff, axis=1, keepdims=True)
        full = jnp.concatenate([dff, jnp.zeros((W_SMALL - FOX_HEADS, t), F32)], axis=0)
        out_ref[...] = full.T

    return _call(body, name="fox_cum_bwd", grid=(nb,),
                 in_specs=[pl.BlockSpec((FOX_HEADS, t), lambda i: (0, nb - 1 - i)),
                           pl.BlockSpec((FOX_HEADS, t, 1), lambda i: (0, nb - 1 - i, 0)),
                           pl.BlockSpec((t, W_SMALL), lambda i: (nb - 1 - i, 0)),
                           pl.BlockSpec((1, W_SMALL), lambda i: (0, 0))],
                 out_specs=[pl.BlockSpec((t, W_SMALL), lambda i: (nb - 1 - i, 0)),
                            pl.BlockSpec((FOX_HEADS, 128), lambda i: (0, 0))],
                 out_shape=[jax.ShapeDtypeStruct((s, W_SMALL), F32), jax.ShapeDtypeStruct((FOX_HEADS, 128), F32)],
                 scratch_shapes=[pltpu.VMEM((FOX_HEADS, 128), F32)],
                 compiler_params=_params(("arbitrary",)))(dc, dcq, small, bvec)


FOX_SCALE = FOX_HEAD_DIM ** -0.5


def _fox_fwd(pm, crow):
    s = pm.shape[0]
    t = _tile(s, FOX_T)
    nb = s // t

    def body(q_ref, k_ref, v_ref, c_ref, o_ref, lse_ref, m_s, l_s, acc_s):
        i = pl.program_id(1)
        j = pl.program_id(2)

        @pl.when(j == 0)
        def _():
            m_s[...] = jnp.full_like(m_s, NEG)
            l_s[...] = jnp.zeros_like(l_s)
            acc_s[...] = jnp.zeros_like(acc_s)

        @pl.when(j <= i)
        def _():
            sc = lax.dot_general(q_ref[...], k_ref[...], (NT, ((), ())), preferred_element_type=F32)
            sc = sc * FOX_SCALE - c_ref[...]
            row = lax.broadcasted_iota(jnp.int32, (t, t), 0)
            col = lax.broadcasted_iota(jnp.int32, (t, t), 1)
            sc = jnp.where((j < i) | (row >= col), sc, NEG)
            m_prev = m_s[...]
            m_new = jnp.maximum(m_prev, jnp.max(sc, axis=1, keepdims=True))
            alpha = jnp.exp(m_prev - m_new)
            p = jnp.exp(sc - m_new)
            l_s[...] = alpha * l_s[...] + jnp.sum(p, axis=1, keepdims=True)
            p_hi = p.astype(BF16)
            p_lo = (p - p_hi.astype(F32)).astype(BF16)
            pv = jnp.dot(p_hi, v_ref[...], preferred_element_type=F32)
            pv = pv + jnp.dot(p_lo, v_ref[...], preferred_element_type=F32)
            acc_s[...] = alpha * acc_s[...] + pv
            m_s[...] = m_new

        @pl.when(j == i)
        def _():
            o_ref[...] = acc_s[...] / l_s[...]
            lse_ref[...] = m_s[...] + jnp.log(l_s[...])

    return _call(
        body, name="fox_fwd", grid=(FOX_HEADS, nb, nb),
        in_specs=[pl.BlockSpec((t, 128), lambda h, i, j: (i, h)),
                  pl.BlockSpec((t, 128), lambda h, i, j: (jnp.minimum(i, j), FOX_HEADS + h)),
                  pl.BlockSpec((t, 128), lambda h, i, j: (jnp.minimum(i, j), 2 * FOX_HEADS + h)),
                  pl.BlockSpec((None, 1, t), lambda h, i, j: (h, 0, jnp.minimum(i, j)))],
        out_specs=[pl.BlockSpec((t, 128), lambda h, i, j: (i, h)),
                   pl.BlockSpec((None, t, 1), lambda h, i, j: (h, i, 0))],
        out_shape=[jax.ShapeDtypeStruct((s, FOX_HEADS * 128), F32), jax.ShapeDtypeStruct((FOX_HEADS, s, 1), F32)],
        scratch_shapes=[pltpu.VMEM((t, 1), F32), pltpu.VMEM((t, 1), F32), pltpu.VMEM((t, 128), F32)],
        compiler_params=_params(("parallel", "parallel", "arbitrary")),
    )(pm, pm, pm, crow)


def _fox_bwd(pm, crow, o, lse, do):
    s = pm.shape[0]
    t = _tile(s, FOX_T)
    nb = s // t

    def body(q_ref, do_ref, o_ref, lse_ref, k_ref, v_ref, c_ref, dq_ref, dk_ref, dv_ref, dc_ref, dcq_ref,
             dk_acc, dv_acc, dc_acc, delta_s):
        j = pl.program_id(1)
        i = pl.program_id(2)

        @pl.when((j == 0) & (i == 0))
        def _():
            dq_ref[...] = jnp.zeros_like(dq_ref)
            dcq_ref[...] = jnp.zeros_like(dcq_ref)

        @pl.when(i == j)
        def _():
            dk_acc[...] = jnp.zeros_like(dk_acc)
            dv_acc[...] = jnp.zeros_like(dv_acc)
            dc_acc[...] = jnp.zeros_like(dc_acc)

        rows = pl.ds(pl.multiple_of(i * t, t), t)

        @pl.when(j == 0)
        def _():
            delta_s[rows, :] = jnp.sum(do_ref[...].astype(F32) * o_ref[...], axis=1, keepdims=True)

        @pl.when(i >= j)
        def _():
            q = q_ref[...]
            dov = do_ref[...]
            sc = lax.dot_general(q, k_ref[...], (NT, ((), ())), preferred_element_type=F32)
            sc = sc * FOX_SCALE - c_ref[...]
            row = lax.broadcasted_iota(jnp.int32, (t, t), 0)
            col = lax.broadcasted_iota(jnp.int32, (t, t), 1)
            p = jnp.where((i > j) | (row >= col), jnp.exp(sc - lse_ref[...]), 0.0)
            dp = lax.dot_general(dov, v_ref[...], (NT, ((), ())), preferred_element_type=F32)
            ds = p * (dp - delta_s[rows, :])
            dsb = ds.astype(BF16)
            dv_acc[...] += lax.dot_general(p.astype(BF16), dov, (TN, ((), ())), preferred_element_type=F32)
            dk_acc[...] += lax.dot_general(dsb, q, (TN, ((), ())), preferred_element_type=F32)
            dq_ref[rows, :] += jnp.dot(dsb, k_ref[...], preferred_element_type=F32) * FOX_SCALE
            dc_acc[...] -= jnp.sum(ds, axis=0, keepdims=True)
            dcq_ref[rows, :] += jnp.sum(ds, axis=1, keepdims=True)

        @pl.when(i == nb - 1)
        def _():
            dk_ref[...] = dk_acc[...] * FOX_SCALE
            dv_ref[...] = dv_acc[...]
            dc_ref[...] = dc_acc[...]

    qi = lambda h, j, i: (jnp.maximum(i, j), h)
    return _call(
        body, name="fox_bwd", grid=(FOX_HEADS, nb, nb),
        in_specs=[pl.BlockSpec((t, 128), qi), pl.BlockSpec((t, 128), qi), pl.BlockSpec((t, 128), qi),
                  pl.BlockSpec((None, t, 1), lambda h, j, i: (h, jnp.maximum(i, j), 0)),
                  pl.BlockSpec((t, 128), lambda h, j, i: (j, FOX_HEADS + h)),
                  pl.BlockSpec((t, 128), lambda h, j, i: (j, 2 * FOX_HEADS + h)),
                  pl.BlockSpec((None, 1, t), lambda h, j, i: (h, 0, j))],
        out_specs=[pl.BlockSpec((s, 128), lambda h, j, i: (0, h)),
                   pl.BlockSpec((t, 128), lambda h, j, i: (j, h)),
                   pl.BlockSpec((t, 128), lambda h, j, i: (j, h)),
                   pl.BlockSpec((None, 1, t), lambda h, j, i: (h, 0, j)),
                   pl.BlockSpec((None, s, 1), lambda h, j, i: (h, 0, 0))],
        out_shape=[jax.ShapeDtypeStruct((s, 1024), F32), jax.ShapeDtypeStruct((s, 1024), F32),
                   jax.ShapeDtypeStruct((s, 1024), F32), jax.ShapeDtypeStruct((FOX_HEADS, 1, s), F32),
                   jax.ShapeDtypeStruct((FOX_HEADS, s, 1), F32)],
        scratch_shapes=[pltpu.VMEM((t, 128), F32), pltpu.VMEM((t, 128), F32), pltpu.VMEM((1, t), F32),
                        pltpu.VMEM((s, 1), F32)],
        compiler_params=_params(("parallel", "arbitrary", "arbitrary")),
    )(pm, do, o, lse, pm, pm, crow)


GLA_SCALE = GLA_DK ** -0.5
GLA_Q_BLK = 3072 // 128
GLA_K_BLK = 3584 // 128
GLA_V_BLK = 4096 // 256


def _gla_gate(sm, wa_ref, b_ref):
    return jnp.dot(sm.astype(BF16), wa_ref[...], preferred_element_type=F32) + b_ref[...]


def _tri(n, strict):
    row = lax.broadcasted_iota(jnp.int32, (n, n), 0)
    col = lax.broadcasted_iota(jnp.int32, (n, n), 1)
    return ((row > col) if strict else (row >= col)).astype(BF16)


def _gla_fwd(pm, small, wa_pad, b_a2):
    s = pm.shape[0]
    r = _tile(s, GLA_R)
    nc = r // CHUNK

    def body(q_ref, k_ref, v_ref, sm_ref, wa_ref, b_ref, o_ref, st_ref, state):
        @pl.when(pl.program_id(1) == 0)
        def _():
            state[...] = jnp.zeros_like(state)

        tri = _tri(CHUNK, False)
        for c in range(nc):
            rows = slice(c * CHUNK, (c + 1) * CHUNK)
            la = _log_sigmoid(_gla_gate(sm_ref[rows, :], wa_ref, b_ref)) * (1.0 / GLA_TEMP)
            cum = _dot_exact01(la, tri, tri_first=True)
            total = jnp.sum(la, axis=0, keepdims=True)
            kdec = k_ref[rows, :].astype(F32) * jnp.exp(total - cum)
            ut = lax.dot_general(v_ref[rows, :], kdec.astype(BF16), (TN, ((), ())), preferred_element_type=F32)
            new = state[...] * jnp.exp(total) + ut
            state[...] = new
            newb = new.astype(BF16)
            st_ref[c] = newb
            qs = (q_ref[rows, :].astype(F32) * GLA_SCALE).astype(BF16)
            o_ref[rows, :] = lax.dot_general(qs, newb, (NT, ((), ())), preferred_element_type=F32)

    return _call(
        body, name="gla_fwd", grid=(GLA_HEADS, s // r),
        in_specs=[pl.BlockSpec((r, 128), lambda h, i: (i, GLA_Q_BLK + h)),
                  pl.BlockSpec((r, 128), lambda h, i: (i, GLA_K_BLK + h)),
                  pl.BlockSpec((r, 256), lambda h, i: (i, GLA_V_BLK + h)),
                  pl.BlockSpec((r, W_SMALL), lambda h, i: (i, 0)),
                  pl.BlockSpec((W_SMALL, 128), lambda h, i: (0, h)),
                  pl.BlockSpec((1, 128), lambda h, i: (0, h))],
        out_specs=[pl.BlockSpec((r, 256), lambda h, i: (i, h)),
                   pl.BlockSpec((nc, None, GLA_DV, GLA_DK), lambda h, i: (i, h, 0, 0))],
        out_shape=[jax.ShapeDtypeStruct((s, 1024), F32),
                   jax.ShapeDtypeStruct((s // CHUNK, GLA_HEADS, GLA_DV, GLA_DK), BF16)],
        scratch_shapes=[pltpu.VMEM((GLA_DV, GLA_DK), F32)],
        compiler_params=_params(("parallel", "arbitrary")),
    )(pm, pm, pm, small, wa_pad, b_a2)


def _gla_bwd(pm, small, wa_pad, b_a2, states, do):
    s = pm.shape[0]
    r = _tile(s, GLA_R)
    nc = r // CHUNK
    nb = s // r

    def body(q_ref, k_ref, v_ref, sm_ref, wa_ref, b_ref, do_ref, st_ref, prev_ref,
             dq_ref, dk_ref, dv_ref, dza_ref, db_ref, carry):
        step = pl.program_id(1)

        @pl.when(step == 0)
        def _():
            carry[...] = jnp.zeros_like(carry)
            db_ref[...] = jnp.zeros_like(db_ref)

        tri = _tri(CHUNK, False)
        tri_strict = _tri(CHUNK, True)
        db = jnp.zeros((1, 128), F32)
        for c in reversed(range(nc)):
            rows = slice(c * CHUNK, (c + 1) * CHUNK)
            z = _gla_gate(sm_ref[rows, :], wa_ref, b_ref)
            la = _log_sigmoid(z) * (1.0 / GLA_TEMP)
            cum = _dot_exact01(la, tri, tri_first=True)
            total = jnp.sum(la, axis=0, keepdims=True)
            w = jnp.exp(total - cum)
            decay = jnp.exp(total)
            kdec = k_ref[rows, :].astype(F32) * w
            dov = do_ref[rows, :]
            qs = (q_ref[rows, :].astype(F32) * GLA_SCALE).astype(BF16)
            dq_ref[rows, :] = jnp.dot(dov, st_ref[c], preferred_element_type=F32) * GLA_SCALE
            gt = lax.dot_general(dov, qs, (TN, ((), ())), preferred_element_type=F32) + carry[...]
            gtb = gt.astype(BF16)
            dv_ref[rows, :] = lax.dot_general(kdec.astype(BF16), gtb, (NT, ((), ())), preferred_element_type=F32)
            dkdec = jnp.dot(v_ref[rows, :], gtb, preferred_element_type=F32)
            dk_ref[rows, :] = dkdec * w
            e = dkdec * kdec
            if c > 0:
                prev = st_ref[c - 1].astype(F32)
            else:
                prev = jnp.where(step == nb - 1, 0.0, prev_ref[0].astype(F32))
            dtot = jnp.sum(gt * prev, axis=0, keepdims=True) * decay
            dla = dtot + _dot_exact01(e, tri_strict, tri_first=True)
            dza = dla * (1.0 / GLA_TEMP) * _sigmoid(-z)
            dza_ref[rows, :] = dza.astype(BF16)
            db = db + jnp.sum(dza, axis=0, keepdims=True)
            carry[...] = gt * decay
        db_ref[...] += db

    blk = lambda h, i: nb - 1 - i
    return _call(
        body, name="gla_bwd", grid=(GLA_HEADS, nb),
        in_specs=[pl.BlockSpec((r, 128), lambda h, i: (blk(h, i), GLA_Q_BLK + h)),
                  pl.BlockSpec((r, 128), lambda h, i: (blk(h, i), GLA_K_BLK + h)),
                  pl.BlockSpec((r, 256), lambda h, i: (blk(h, i), GLA_V_BLK + h)),
                  pl.BlockSpec((r, W_SMALL), lambda h, i: (blk(h, i), 0)),
                  pl.BlockSpec((W_SMALL, 128), lambda h, i: (0, h)),
                  pl.BlockSpec((1, 128), lambda h, i: (0, h)),
                  pl.BlockSpec((r, 256), lambda h, i: (blk(h, i), h)),
                  pl.BlockSpec((nc, None, GLA_DV, GLA_DK), lambda h, i: (blk(h, i), h, 0, 0)),
                  pl.BlockSpec((1, None, GLA_DV, GLA_DK),
                               lambda h, i: (jnp.maximum(blk(h, i) * nc - 1, 0), h, 0, 0))],
        out_specs=[pl.BlockSpec((r, 128), lambda h, i: (blk(h, i), h)),
                   pl.BlockSpec((r, 128), lambda h, i: (blk(h, i), h)),
                   pl.BlockSpec((r, 256), lambda h, i: (blk(h, i), h)),
                   pl.BlockSpec((r, 128), lambda h, i: (blk(h, i), h)),
                   pl.BlockSpec((1, 128), lambda h, i: (0, h))],
        out_shape=[jax.ShapeDtypeStruct((s, 512), F32), jax.ShapeDtypeStruct((s, 512), F32),
                   jax.ShapeDtypeStruct((s, 1024), F32), jax.ShapeDtypeStruct((s, 512), BF16),
                   jax.ShapeDtypeStruct((1, 512), F32)],
        scratch_shapes=[pltpu.VMEM((GLA_DV, GLA_DK), F32)],
        compiler_params=_params(("parallel", "arbitrary")),
    )(pm, pm, pm, small, wa_pad, b_a2, do, states, states)


def _modulation(c_all, w_ada):
    n = w_ada.shape[1]
    tn = _tile(n, 512)

    def body(c_ref, w_ref, out_ref, ca_ref):
        cv = c_ref[...]
        ca = cv * _sigmoid(cv)
        ca_ref[...] = ca
        out_ref[...] = jnp.dot(ca.astype(BF16), w_ref[...].astype(BF16), preferred_element_type=F32)

    return _call(body, name="modulation", grid=(n // tn,),
                 in_specs=[pl.BlockSpec((N_DEV, D_MODEL), lambda j: (0, 0)),
                           pl.BlockSpec((D_MODEL, tn), lambda j: (0, j))],
                 out_specs=[pl.BlockSpec((N_DEV, tn), lambda j: (0, j)),
                            pl.BlockSpec((N_DEV, D_MODEL), lambda j: (0, 0))],
                 out_shape=[jax.ShapeDtypeStruct((N_DEV, n), F32), jax.ShapeDtypeStruct((N_DEV, D_MODEL), F32)],
                 compiler_params=_params(("arbitrary",)))(c_all, w_ada)


def _adamw_math(w, g, m, v):
    m = ADAM_B1 * m + (1.0 - ADAM_B1) * g
    v = ADAM_B2 * v + (1.0 - ADAM_B2) * (g * g)
    m_hat = m / (1.0 - ADAM_B1 ** ADAM_STEP)
    v_hat = v / (1.0 - ADAM_B2 ** ADAM_STEP)
    delta = -ADAM_LR * (m_hat / (jnp.sqrt(v_hat) + ADAM_EPS) + ADAM_WD * w)
    return delta, m, v


def _adamw_slabs(name, w, slabs, m, v, tr=256):
    rr, cc = w.shape
    tr = _tile(rr, tr)

    def body(w_ref, s_ref, m_ref, v_ref, g_ref, d_ref, nm_ref, nv_ref):
        g = s_ref[0].astype(F32)
        for r in range(1, N_DEV):
            g = g + s_ref[r].astype(F32)
        g_ref[...] = g
        d, nm, nv = _adamw_math(w_ref[...], g, m_ref[...], v_ref[...])
        d_ref[...] = d
        nm_ref[...] = nm
        nv_ref[...] = nv

    spec = pl.BlockSpec((tr, cc), lambda i: (i, 0))
    return _call(body, name=name, grid=(rr // tr,),
                 in_specs=[spec, pl.BlockSpec((N_DEV, tr, cc), lambda i: (0, i, 0)), spec, spec],
                 out_specs=[spec] * 4, out_shape=[jax.ShapeDtypeStruct((rr, cc), F32)] * 4,
                 compiler_params=_params(("parallel",)))(w, slabs, m, v)


def _adamw_ada(w, cat, dm, m, v, tr=256):
    rr, cc = w.shape
    tr = _tile(rr, tr)

    def body(w_ref, ca_ref, dm_ref, m_ref, v_ref, g_ref, d_ref, nm_ref, nv_ref):
        g = ca_ref[:, 0:1] * dm_ref[0:1, :]
        for b in range(1, N_DEV):
            g = g + ca_ref[:, b:b + 1] * dm_ref[b:b + 1, :]
        g_ref[...] = g
        d, nm, nv = _adamw_math(w_ref[...], g, m_ref[...], v_ref[...])
        d_ref[...] = d
        nm_ref[...] = nm
        nv_ref[...] = nv

    spec = pl.BlockSpec((tr, cc), lambda i: (i, 0))
    return _call(body, name="adamw_ada", grid=(rr // tr,),
                 in_specs=[spec, pl.BlockSpec((tr, N_DEV), lambda i: (i, 0)),
                           pl.BlockSpec((N_DEV, cc), lambda i: (0, 0)), spec, spec],
                 out_specs=[spec] * 4, out_shape=[jax.ShapeDtypeStruct((rr, cc), F32)] * 4,
                 compiler_params=_params(("parallel",)))(w, cat, dm, m, v)


def _sum_devices(gathered):
    ln = gathered.shape[-1]

    def body(g_ref, out_ref):
        acc = g_ref[0]
        for r in range(1, N_DEV):
            acc = acc + g_ref[r]
        out_ref[...] = acc

    return _call(body, name="sum_devices",
                 in_specs=[pl.BlockSpec(memory_space=pltpu.VMEM)], out_specs=pl.BlockSpec(memory_space=pltpu.VMEM),
                 out_shape=jax.ShapeDtypeStruct((1, ln), F32))(gathered)


def _adamw_flat(w, g, m, v):
    def body(w_ref, g_ref, m_ref, v_ref, d_ref, nm_ref, nv_ref):
        d, nm, nv = _adamw_math(w_ref[...], g_ref[...], m_ref[...], v_ref[...])
        d_ref[...] = d
        nm_ref[...] = nm
        nv_ref[...] = nv

    vm = pl.BlockSpec(memory_space=pltpu.VMEM)
    return _call(body, name="adamw_small", in_specs=[vm] * 4, out_specs=[vm] * 3,
                 out_shape=[jax.ShapeDtypeStruct(w.shape, F32)] * 3)(w, g, m, v)


def _from_col_shards(g):
    return jnp.transpose(g, (1, 0, 2)).reshape(g.shape[1], N_DEV * g.shape[2])


def _pad_lanes(v, n):
    return jnp.concatenate([v, jnp.zeros(v.shape[:-1] + (n - v.shape[-1],), v.dtype)], axis=-1)


def kernel(x, c, w_ada, b_ada, g_pre_mix, g_post_mix, w_in, b_fgate, w_gla_a2, b_gla_a2, g_fox_out, g_gla_out, w_out, g_pre_mlp, g_post_mlp, w_mlp_in, w_mlp_out, loss_target, m_w_ada, m_b_ada, m_g_pre_mix, m_g_post_mix, m_w_in, m_b_fgate, m_w_gla_a2, m_b_gla_a2, m_g_fox_out, m_g_gla_out, m_w_out, m_g_pre_mlp, m_g_post_mlp, m_w_mlp_in, m_w_mlp_out, v_w_ada, v_b_ada, v_g_pre_mix, v_g_post_mix, v_w_in, v_b_fgate, v_w_gla_a2, v_b_gla_a2, v_g_fox_out, v_g_gla_out, v_w_out, v_g_pre_mlp, v_g_post_mlp, v_w_mlp_in, v_w_mlp_out):
    rank = _my_rank()
    xs = x[0]
    s = xs.shape[0]
    target = loss_target[0]

    c_all, wa2_g, ggla_g, win_g = _all_gather("gather_first", [c, w_gla_a2[0], g_gla_out[0], w_in[0].astype(BF16)])
    rest = [_own_slot("own_w_out", w_out[0], True, rank), _own_slot("own_w_mlp_in", w_mlp_in[0], True, rank),
            _own_slot("own_w_mlp_out", w_mlp_out[0], True, rank)]
    gs_send, gs_recv, _, gs_land, gs_token = _exchange_start("gather_rest_start", rest, after=(c_all,))
    w_a2 = _from_col_shards(wa2_g)
    g_gla = _from_col_shards(ggla_g).reshape(1, 1024)
    g_fox = g_fox_out.reshape(1, 1024)
    win_full = _from_col_shards(win_g)
    w_main = jnp.concatenate([win_full[:, :3072], win_full[:, 3080:5128], win_full[:, 5144:6168]], axis=1)
    w_small = _pad_lanes(jnp.concatenate([win_full[:, 3072:3080], win_full[:, 5128:5144]], axis=1), W_SMALL)
    wa_pad =jnp.concatenate([jnp.zeros((8, 512), BF16), w_a2.astype(BF16), jnp.zeros((104, 512), BF16)], axis=0)
    bf_vec = _pad_lanes(b_fgate, W_SMALL)

    mod_part, c_act = _modulation(c_all.reshape(N_DEV, D_MODEL), w_ada[0])
    (mod_g,) = _all_gather("gather_mod", [mod_part])
    mod = lax.dynamic_slice_in_dim(mod_g, rank, 1, axis=1).reshape(1, 6 * D_MODEL) + b_ada
    shift_m, scale_m, gate_m, shift_f, scale_f, gate_f = [mod[:, i * D_MODEL:(i + 1) * D_MODEL] for i in range(6)]

    h = _premix(xs, g_pre_mix, scale_m, shift_m, deps=(gs_token,))
    pm = _mm_plain("proj_main", h, w_main, NN, BF16)
    small = _mm_plain("proj_small", h, w_small, NN, F32)
    crow = _fox_cum(small, bf_vec).reshape(FOX_HEADS, 1, s)
    o_fox, lse = _fox_fwd(pm, crow)
    o_gla, states = _gla_fwd(pm, small, wa_pad, b_gla_a2)
    mix = _mix_fwd(o_fox, o_gla, pm, g_fox, g_gla)
    wout_g, wmi_g, wmo_g = _exchange_wait("gather_rest_wait", gs_send, gs_recv, None, gs_land, mix)
    w_out_full = wout_g.reshape(D_MODEL, D_MODEL)
    w_mo_full = wmo_g.reshape(D_FF, D_MODEL)
    y = _mm_plain("out_proj", mix, w_out_full, NN, F32)
    x1, h2 = _postmix_premlp(xs, y, gate_m, g_post_mix, g_pre_mlp, scale_f, shift_f)

    tm, tn, tk = _tile(s, 1024), 1024, 2048
    nsh = 1024 // tn

    def relu2(acc):
        rl = jnp.maximum(acc, 0.0)
        return rl * rl, rl

    z, a_relu = _matmul(
        "mlp_in", h2, wmi_g, contract=NN, grid=(s // tm, D_FF // tn, D_MODEL // tk),
        a_spec=pl.BlockSpec((tm, tk), lambda i, j, k: (i, k)),
        b_spec=pl.BlockSpec((None, tk, tn), lambda i, j, k: (j // nsh, k, j % nsh)),
        out_specs=[pl.BlockSpec((tm, tn), lambda i, j, k: (i, j))] * 2,
        out_shapes=[jax.ShapeDtypeStruct((s, D_FF), BF16)] * 2, acc_shape=(tm, tn), epilogue=relu2)
    y2 = _mm_plain("mlp_out", z, w_mo_full, NN, F32)

    dx2, dy2, loss_vec, dgate_f, dg_post_mlp = _loss_postmlp_bwd(x1, y2, target, gate_f, g_post_mlp)
    loss = lax.psum(loss_vec[0, 0], ("x", "y", "c"))

    da = _mm_plain("mlp_out_dx", dy2, w_mo_full, NT, BF16, extra=(a_relu,),
                   epilogue=lambda acc, rl: (acc * (2.0 * rl.astype(F32)),))
    dw_mo = _mm_plain("mlp_out_dw", z, dy2, TN, BF16)
    dw_mo = dw_mo.reshape(N_DEV, 1024, D_MODEL)
    x_mo = _exchange_start("grad_mlp_out_start", [_own_slot("own_dw_mlp_out", dw_mo, False, rank)], [dw_mo])
    tkx = 1024
    (dh2,) = _matmul(
        "mlp_in_dx", da, wmi_g, contract=NT, grid=(s // tm, D_MODEL // tn, D_FF // tkx),
        a_spec=pl.BlockSpec((tm, tkx), lambda i, j, k: (i, k)),
        b_spec=pl.BlockSpec((None, tn, tkx), lambda i, j, k: (k, j, 0)),
        out_specs=[pl.BlockSpec((tm, tn), lambda i, j, k: (i, j))],
        out_shapes=[jax.ShapeDtypeStruct((s, D_MODEL), F32)], acc_shape=(tm, tn), deps=(x_mo[4],))
    ts = _tile(s, 2048)
    (dw_mi,) = _matmul(
        "mlp_in_dw", h2, da, contract=TN, grid=(D_MODEL // 1024, D_FF // tn, s // ts),
        a_spec=pl.BlockSpec((ts, 1024), lambda i, j, k: (k, i)),
        b_spec=pl.BlockSpec((ts, tn), lambda i, j, k: (k, j)),
        out_specs=[pl.BlockSpec((None, 1024, tn), lambda i, j, k: (j // nsh, i, j % nsh))],
        out_shapes=[jax.ShapeDtypeStruct((N_DEV, D_MODEL, 1024), BF16)], acc_shape=(1024, tn))
    x_mi = _exchange_start("grad_mlp_in_start", [_own_slot("own_dw_mlp_in", dw_mi, False, rank)], [dw_mi])

    dx1, dy, dscale_f, dshift_f, dg_pre_mlp, dgate_m, dg_post_mix = _premlp_postmix_bwd(
        dh2, dx2, x1, y, scale_f, g_pre_mlp, gate_m, g_post_mix, deps=(x_mi[4],))

    dmix = _mm_plain("out_proj_dx", dy, w_out_full, NT, F32)
    dw_out = _mm_plain("out_proj_dw", mix, dy, TN, BF16)
    dw_out = dw_out.reshape(N_DEV, 256, D_MODEL)
    x_out = _exchange_start("grad_out_start", [_own_slot("own_dw_out", dw_out, False, rank)], [dw_out])
    do_fox, do_gla, dgr, dg_fox, dg_gla = _mix_bwd(dmix, o_fox, o_gla, pm, g_fox, g_gla, deps=(x_out[4],))

    dq, dk, dv, dc, dcq = _fox_bwd(pm, crow, o_fox, lse, do_fox)
    dsmall_f, db_f = _fox_cum_bwd(dc.reshape(FOX_HEADS, s), dcq, small, bf_vec)
    dgq, dgk, dgv, dza, db_a2 = _gla_bwd(pm, small, wa_pad, b_gla_a2, states, do_gla)
    dsmall = _mm_plain("gate_dx", dza, wa_pad, NT, F32, tn=128, extra=(dsmall_f,),
                       epilogue=lambda acc, other: (acc + other,))
    dwa_pad = _mm_plain("gate_dw", small, dza, TN, F32, tm=128, tn=512)

    dpm = jnp.concatenate([dq.astype(BF16), dk.astype(BF16), dv.astype(BF16), dgq.astype(BF16), dgk.astype(BF16),
                           dgv.astype(BF16), dgr], axis=1)
    dw_main = _mm_plain("proj_main_dw", h, dpm, TN, BF16)
    dw_small = _mm_plain("proj_small_dw", h, dsmall, TN, BF16, tn=128)
    dwin_full = jnp.concatenate([dw_main[:, :3072], dw_small[:, 0:8], dw_main[:, 3072:5120], dw_small[:, 8:24],
                                 dw_main[:, 5120:6144]], axis=1)
    dwin_slabs = jnp.transpose(dwin_full.reshape(D_MODEL, N_DEV, 771), (1, 0, 2))
    x_in = _exchange_start("grad_in_start", [_own_slot("own_dw_in", dwin_slabs, False, rank)], [dwin_slabs])
    dh_small = _mm_plain("proj_small_dx", dsmall, w_small, NT, F32, tk=128)
    dh = _mm_plain("proj_main_dx", dpm, w_main, NT, F32, extra=(dh_small,),
                   epilogue=lambda acc, other: (acc + other,), deps=(x_in[4],))
    grad_x, dscale_m, dshift_m, dg_pre_mix = _premix_bwd(dh, dx1, xs, g_pre_mix, scale_m)

    dmod = jnp.concatenate([dshift_m, dscale_m, dgate_m, dshift_f, dscale_f, dgate_f], axis=1)
    flat = jnp.concatenate(
        [dmod, dg_pre_mix, dg_post_mix, dg_fox, dg_pre_mlp, dg_post_mlp, db_a2,
         dwa_pad[8:24, :].reshape(1, GLA_RANK * 512), dg_gla, _pad_lanes(db_f[:, 0].reshape(1, FOX_HEADS), 128)],
        axis=1)
    (flat_g,) = _all_gather("gather_small_grads", [flat])
    tot = _sum_devices(flat_g)

    (r_mo,) = _exchange_wait("grad_mlp_out_wait", *x_mo[:4], grad_x)
    g_mo, d_mo, nm_mo, nv_mo = _adamw_slabs("adamw_w_mlp_out", w_mlp_out[0], r_mo, m_w_mlp_out[0], v_w_mlp_out[0])
    (r_mi,) = _exchange_wait("grad_mlp_in_wait", *x_mi[:4], g_mo)
    g_mi, d_mi, nm_mi, nv_mi = _adamw_slabs("adamw_w_mlp_in", w_mlp_in[0], r_mi, m_w_mlp_in[0], v_w_mlp_in[0])
    (r_out,) = _exchange_wait("grad_out_wait", *x_out[:4], g_mi)
    g_out, d_out, nm_out, nv_out = _adamw_slabs("adamw_w_out", w_out[0], r_out, m_w_out[0], v_w_out[0])

    dm_cols = lax.dynamic_slice_in_dim(flat_g[:, 0, :6 * D_MODEL], rank * 1536, 1536, axis=1)
    g_ada, d_ada, nm_ada, nv_ada = _adamw_ada(w_ada[0], c_act.T, dm_cols, m_w_ada[0], v_w_ada[0])
    (r_in,) = _exchange_wait("grad_in_wait", *x_in[:4], g_ada)
    g_in, d_in, nm_in, nv_in = _adamw_slabs("adamw_w_in", w_in[0], r_in, m_w_in[0], v_w_in[0])

    o = 0
    seg = {}
    for name, n in (("b_ada", 12288), ("g_pre_mix", 2048), ("g_post_mix", 2048), ("g_fox_out", 1024),
                    ("g_pre_mlp", 2048), ("g_post_mlp", 2048), ("b_gla_a2", 512), ("w_gla_a2", 8192),
                    ("g_gla_out", 1024), ("b_fgate", 128)):
        seg[name] = tot[:, o:o + n]
        o += n
    g_wa2 = lax.dynamic_slice_in_dim(seg["w_gla_a2"].reshape(GLA_RANK, 512), rank * 64, 64, axis=1)
    g_ggla = lax.dynamic_slice_in_dim(seg["g_gla_out"].reshape(GLA_HEADS, GLA_DV), rank * 32, 32, axis=1)
    small_names = ["b_ada", "g_pre_mix", "g_post_mix", "g_fox_out", "g_pre_mlp", "g_post_mlp", "b_gla_a2",
                   "w_gla_a2", "g_gla_out", "b_fgate"]
    small_grads = {**seg, "w_gla_a2": g_wa2.reshape(1, 1024), "g_gla_out": g_ggla.reshape(1, 128)}
    weights = dict(b_ada=b_ada, g_pre_mix=g_pre_mix, g_post_mix=g_post_mix, g_fox_out=g_fox_out,
                   g_pre_mlp=g_pre_mlp, g_post_mlp=g_post_mlp, b_gla_a2=b_gla_a2, w_gla_a2=w_gla_a2,
                   g_gla_out=g_gla_out, b_fgate=b_fgate)
    moms = dict(b_ada=m_b_ada, g_pre_mix=m_g_pre_mix, g_post_mix=m_g_post_mix, g_fox_out=m_g_fox_out,
                g_pre_mlp=m_g_pre_mlp, g_post_mlp=m_g_post_mlp, b_gla_a2=m_b_gla_a2, w_gla_a2=m_w_gla_a2,
                g_gla_out=m_g_gla_out, b_fgate=m_b_fgate)
    vels = dict(b_ada=v_b_ada, g_pre_mix=v_g_pre_mix, g_post_mix=v_g_post_mix, g_fox_out=v_g_fox_out,
                g_pre_mlp=v_g_pre_mlp, g_post_mlp=v_g_post_mlp, b_gla_a2=v_b_gla_a2, w_gla_a2=v_w_gla_a2,
                g_gla_out=v_g_gla_out, b_fgate=v_b_fgate)

    def flatten(d, fill):
        parts = []
        for nm in small_names:
            p = d[nm].reshape(1, -1)
            if nm == "b_fgate":
                p = jnp.concatenate([p[:, :FOX_HEADS], jnp.full((1, 128 - FOX_HEADS), fill, F32)], axis=1)
            parts.append(p)
        return jnp.concatenate(parts, axis=1).reshape(-1, 128)

    fw, fg, fm, fv = flatten(weights, 0.0), flatten(small_grads, 0.0), flatten(moms, 0.0), flatten(vels, 1.0)
    fd, fnm, fnv = _adamw_flat(fw, fg, fm, fv)

    def unflatten(fl):
        fl = fl.reshape(1, -1)
        out = {}
        o = 0
        for nm in small_names:
            n = 128 if nm == "b_fgate" else weights[nm].size
            piece = fl[:, o:o + n]
            if nm == "b_fgate":
                piece = piece[:, :FOX_HEADS]
            out[nm] = piece.reshape(weights[nm].shape)
            o += n
        return out

    sg, sd, snm, snv = unflatten(fg), unflatten(fd), unflatten(fnm), unflatten(fnv)

    big = dict(w_ada=(g_ada, d_ada, nm_ada, nv_ada), w_in=(g_in, d_in, nm_in, nv_in),
               w_out=(g_out, d_out, nm_out, nv_out), w_mlp_in=(g_mi, d_mi, nm_mi, nv_mi),
               w_mlp_out=(g_mo, d_mo, nm_mo, nv_mo))
    order = ["w_ada", "b_ada", "g_pre_mix", "g_post_mix", "w_in", "b_fgate", "w_gla_a2", "b_gla_a2", "g_fox_out",
             "g_gla_out", "w_out", "g_pre_mlp", "g_post_mlp", "w_mlp_in", "w_mlp_out"]

    def pick(nm, idx):
        if nm in big:
            return big[nm][idx][None]
        return (sg, sd, snm, snv)[idx][nm]

    grads = [pick(nm, 0) for nm in order]
    deltas = [pick(nm, 1) for nm in order]
    new_m = [pick(nm, 2) for nm in order]
    new_v = [pick(nm, 3) for nm in order]
    return (loss, grad_x[None], *grads, *deltas, *new_m, *new_v)
```

```python
import functools

import numpy as np
import jax
import jax.numpy as jnp
from jax import lax
from jax.experimental import pallas as pl
from jax.experimental.pallas import tpu as pltpu

F32 = jnp.float32
BF16 = jnp.bfloat16
MESH = pl.DeviceIdType.MESH
N_DEV = 8

D_MODEL = 2048
FOX_HEADS = 8
FOX_HEAD_DIM = 128
GLA_HEADS = 4
GLA_DK = 128
GLA_DV = 256
GLA_RANK = 16
GLA_TEMP = 16.0
CHUNK = 64
D_FF = 8192
W_MAIN = 6144
W_SMALL = 128
EPS = 1e-6
NEG = float(np.finfo(np.float32).min)

ADAM_LR = 0.001
ADAM_B1 = 0.9
ADAM_B2 = 0.999
ADAM_EPS = 1e-08
ADAM_WD = 0.01
ADAM_STEP = 10

ROW_T = 256
FOX_T = 512
GLA_R = 512
CUM_T = 256
VMEM_LIMIT = 56 * 1024 * 1024


def _call(body, deps=(), **kw):
    if not deps:
        return pl.pallas_call(body, **kw)
    n_in, n_dep = len(kw["in_specs"]), len(deps)

    def with_deps(*refs):
        return body(*refs[:n_in], *refs[n_in + n_dep:])

    kw["in_specs"] = [*kw["in_specs"], *[pl.BlockSpec(memory_space=pl.ANY)] * n_dep]
    call = pl.pallas_call(with_deps, **kw)
    return lambda *args: call(*args, *deps)


def _params(sem=None):
    return pltpu.CompilerParams(dimension_semantics=sem, vmem_limit_bytes=VMEM_LIMIT)


def _my_pos():
    return lax.axis_index("x"), lax.axis_index("y"), lax.axis_index("c")


def _my_rank():
    x, y, c = _my_pos()
    return 4 * x + 2 * y + c


def _all_gather(name, arrays):
    n = len(arrays)

    def body(*refs):
        ins = refs[:n]
        outs = refs[n:2 * n]
        send_sems, recv_sems, local_sems = refs[2 * n:]
        x, y, c = _my_pos()
        me, sibling = (x, y, c), (x, y, 1 - c)
        chips = [(1 - x, y), (x, 1 - y), (1 - x, 1 - y)]

        def slot(a, px, py, pc):
            return outs[a].at[4 * px + 2 * py + pc]

        def copy(a, k, block, to, src=None):
            return pltpu.make_async_remote_copy(
                src_ref=slot(a, *block) if src is None else src, dst_ref=slot(a, *block),
                send_sem=send_sems.at[a, k], recv_sem=recv_sems.at[a, k],
                device_id=to, device_id_type=MESH)

        started = []
        for a in range(n):
            mine = pltpu.make_async_copy(ins[a], slot(a, *me), local_sems.at[a])
            mine.start()
            started.append(mine)
        first = []
        for a in range(n):
            first.append(copy(a, 0, me, sibling, src=ins[a]))
            first += [copy(a, 1 + j, me, (*chip, c), src=ins[a]) for j, chip in enumerate(chips)]
        for cp in first:
            cp.start()
        passed = []
        for j, chip in enumerate(chips):
            for a in range(n):
                copy(a, 1 + j, (*chip, c), me).wait_recv()
                fwd = copy(a, 4 + j, (*chip, c), sibling)
                fwd.start()
                passed.append(fwd)
        for a in range(n):
            copy(a, 0, sibling, me).wait_recv()
            for j, chip in enumerate(chips):
                copy(a, 4 + j, (*chip, 1 - c), me).wait_recv()
        for cp in first + passed:
            cp.wait_send()
        for mine in started:
            mine.wait()

    hbm = pl.BlockSpec(memory_space=pltpu.HBM)
    return _call(
        body, name=name,
        out_shape=[jax.ShapeDtypeStruct((N_DEV,) + a.shape, a.dtype) for a in arrays],
        in_specs=[hbm] * n, out_specs=[hbm] * n,
        scratch_shapes=[pltpu.SemaphoreType.DMA((n, 7)), pltpu.SemaphoreType.DMA((n, 7)),
                        pltpu.SemaphoreType.DMA((n,))],
    )(*arrays)


def _own_slot(name, src, gather, rank):
    shape = ((N_DEV,) + src.shape) if gather else src.shape
    rr, cc = shape[1], shape[2]
    tr = _tile(rr, 512)

    def body(rank_ref, s_ref, o_ref):
        o_ref[...] = s_ref[...].astype(o_ref.dtype)

    if gather:
        in_spec = pl.BlockSpec((tr, cc), lambda i, rk: (i, 0))
    else:
        in_spec = pl.BlockSpec((None, tr, cc), lambda i, rk: (rk[0], i, 0))
    grid_spec = pltpu.PrefetchScalarGridSpec(
        num_scalar_prefetch=1, grid=(rr // tr,), in_specs=[in_spec],
        out_specs=pl.BlockSpec((None, tr, cc), lambda i, rk: (rk[0], i, 0)))
    return _call(body, name=name, grid_spec=grid_spec, out_shape=jax.ShapeDtypeStruct(shape, BF16),
                 compiler_params=_params(("arbitrary",)))(jnp.reshape(rank, (1,)).astype(jnp.int32), src)


_HBM = pl.BlockSpec(memory_space=pltpu.HBM)
_SEM = pl.BlockSpec(memory_space=pltpu.SEMAPHORE)
_FLIPS = [(kx, ky, kc) for kx in (0, 1) for ky in (0, 1) for kc in (0, 1)][1:]


def _peers():
    x, y, c = _my_pos()
    out = []
    for kx, ky, kc in _FLIPS:
        px, py, pc = (1 - x if kx else x), (1 - y if ky else y), (1 - c if kc else c)
        out.append(((px, py, pc), 4 * px + 2 * py + pc))
    return out


def _exchange_copy(srcs, lands, send_sems, recv_sems, a, k, peer, peer_rank, slot):
    return pltpu.make_async_remote_copy(
        src_ref=lands[a].at[slot] if srcs is None else srcs[a].at[peer_rank],
        dst_ref=lands[a].at[slot],
        send_sem=send_sems[a].at[k], recv_sem=recv_sems[a].at[k],
        device_id=peer, device_id_type=MESH)


def _exchange_start(name, lands, srcs=None, after=()):
    n = len(lands)
    n_src = 0 if srcs is None else n
    n_in = n + n_src + len(after)

    def body(*refs):
        lnd = refs[:n]
        src = None if srcs is None else refs[n:2 * n]
        send_sems, recv_sems = refs[n_in:n_in + n], refs[n_in + n:n_in + 2 * n]
        token = refs[-1]
        me = _my_rank()
        for a in range(n):
            for k, (peer, peer_rank) in enumerate(_peers()):
                _exchange_copy(src, lnd, send_sems, recv_sems, a, k, peer, peer_rank, me).start()
        token[...] = jnp.zeros_like(token)

    sems = [pltpu.SemaphoreType.DMA((7,))] * (2 * n)
    thru = list(lands) + ([] if srcs is None else list(srcs))
    outs = pl.pallas_call(
        body, name=name,
        out_shape=(*sems, *[pltpu.HBM(t.shape, t.dtype) for t in thru], jax.ShapeDtypeStruct((8, 128), F32)),
        in_specs=[*[_HBM] * len(thru), *[pl.BlockSpec(memory_space=pl.ANY)] * len(after)],
        out_specs=(*[_SEM] * (2 * n), *[_HBM] * len(thru), pl.BlockSpec(memory_space=pltpu.VMEM)),
        input_output_aliases={i: 2 * n + i for i in range(len(thru))},
        compiler_params=pltpu.CompilerParams(has_side_effects=pltpu.SideEffectType.DATAFLOW_SIDE_EFFECTING),
    )(*[pltpu.with_memory_space_constraint(t, pltpu.HBM) for t in thru], *after)
    lands_thru = outs[2 * n:3 * n]
    srcs_thru = None if srcs is None else outs[3 * n:4 * n]
    return outs[:n], outs[n:2 * n], srcs_thru, lands_thru, outs[-1]


def _exchange_wait(name, send_sems, recv_sems, srcs, lands, after):
    n = len(lands)
    thru = list(lands) + ([] if srcs is None else list(srcs))

    def body(*refs):
        lnd = refs[:n]
        src = None if srcs is None else refs[n:2 * n]
        ssem, rsem = refs[len(thru):len(thru) + n], refs[len(thru) + n:len(thru) + 2 * n]
        for a in range(n):
            for k, (peer, peer_rank) in enumerate(_peers()):
                cp = _exchange_copy(src, lnd, ssem, rsem, a, k, peer, peer_rank, peer_rank)
                cp.wait_send()
                cp.wait_recv()

    outs = pl.pallas_call(
        body, name=name,
        out_shape=tuple(pltpu.HBM(t.shape, t.dtype) for t in thru),
        in_specs=[*[_HBM] * len(thru), *[_SEM] * (2 * n), pl.BlockSpec(memory_space=pl.ANY)],
        out_specs=tuple([_HBM] * len(thru)),
        input_output_aliases={i: i for i in range(len(thru))},
        compiler_params=pltpu.CompilerParams(has_side_effects=pltpu.SideEffectType.DATAFLOW_SIDE_EFFECTING),
    )(*thru, *send_sems, *recv_sems, after)
    return outs[:n]


NN = ((1,), (0,))
NT = ((1,), (1,))
TN = ((0,), (0,))


def _matmul(name, a, b, *, contract, grid, a_spec, b_spec, out_specs, out_shapes, acc_shape,
            extra=(), extra_specs=(), epilogue=None, deps=()):
    nk = grid[2]
    n_extra = len(extra)
    n_out = len(out_shapes)

    def body(*refs):
        a_ref, b_ref = refs[0], refs[1]
        extra_refs = refs[2:2 + n_extra]
        out_refs = refs[2 + n_extra:2 + n_extra + n_out]
        acc_ref = refs[-1]
        k = pl.program_id(2)

        def prod():
            return lax.dot_general(a_ref[...].astype(BF16), b_ref[...].astype(BF16), (contract, ((), ())),
                                   preferred_element_type=F32)

        def finish(acc):
            res = (acc,) if epilogue is None else epilogue(acc, *[r[...] for r in extra_refs])
            for o_ref, val in zip(out_refs, res):
                o_ref[...] = val.astype(o_ref.dtype)

        if nk == 1:
            finish(prod())
            return

        @pl.when(k == 0)
        def _():
            acc_ref[...] = prod()

        @pl.when((k > 0) & (k < nk - 1))
        def _():
            acc_ref[...] += prod()

        @pl.when(k == nk - 1)
        def _():
            finish(acc_ref[...] + prod())

    outs = _call(
        body, deps=deps, name=name, grid=grid,
        in_specs=[a_spec, b_spec, *extra_specs], out_specs=list(out_specs), out_shape=list(out_shapes),
        scratch_shapes=[pltpu.VMEM(acc_shape if nk > 1 else (8, 128), F32)],
        compiler_params=_params(("parallel", "parallel", "arbitrary")),
    )(a, b, *extra)
    return outs


def _tile(n, t):
    t = min(n, t)
    assert n % t == 0, (n, t)
    return t


def _mm_plain(name, a, b, contract, out_dtype, tm=1024, tn=1024, tk=2048, extra=(), epilogue=None,
              n_out=1, out_dtypes=None, deps=()):
    if contract == NN:
        (m, kd), (_, n) = a.shape, b.shape
    elif contract == NT:
        (m, kd), (n, _) = a.shape, b.shape
    else:
        (kd, m), (_, n) = a.shape, b.shape
    tm, tn, tk = _tile(m, tm), _tile(n, tn), _tile(kd, tk)
    if contract == NN:
        a_spec = pl.BlockSpec((tm, tk), lambda i, j, k: (i, k))
        b_spec = pl.BlockSpec((tk, tn), lambda i, j, k: (k, j))
    elif contract == NT:
        a_spec = pl.BlockSpec((tm, tk), lambda i, j, k: (i, k))
        b_spec = pl.BlockSpec((tn, tk), lambda i, j, k: (j, k))
    else:
        a_spec = pl.BlockSpec((tk, tm), lambda i, j, k: (k, i))
        b_spec = pl.BlockSpec((tk, tn), lambda i, j, k: (k, j))
    o_spec = pl.BlockSpec((tm, tn), lambda i, j, k: (i, j))
    out_dtypes = out_dtypes or [out_dtype] * n_out
    outs = _matmul(
        name, a, b, contract=contract, grid=(m // tm, n // tn, kd // tk), a_spec=a_spec, b_spec=b_spec,
        out_specs=[o_spec] * len(out_dtypes), out_shapes=[jax.ShapeDtypeStruct((m, n), dt) for dt in out_dtypes],
        acc_shape=(tm, tn), extra=extra, extra_specs=[o_spec] * len(extra), epilogue=epilogue, deps=deps)
    return outs[0] if len(out_dtypes) == 1 else outs


def _rows_call(name, body, row_in, vec_in, row_out, vec_out, s, deps=()):
    t = _tile(s, ROW_T)
    in_specs = []
    args = []
    for arr, width, cb in row_in:
        in_specs.append(pl.BlockSpec((t, width), functools.partial(lambda i, cb: (i, cb), cb=cb)))
        args.append(arr)
    for v in vec_in:
        in_specs.append(pl.BlockSpec(v.shape, lambda i: (0, 0)))
        args.append(v)
    out_specs = []
    out_shapes = []
    for width, dt in row_out:
        out_specs.append(pl.BlockSpec((t, width), lambda i: (i, 0)))
        out_shapes.append(jax.ShapeDtypeStruct((s, width), dt))
    for width in vec_out:
        out_specs.append(pl.BlockSpec((1, width), lambda i: (0, 0)))
        out_shapes.append(jax.ShapeDtypeStruct((1, width), F32))
    return _call(body, deps=deps, name=name, grid=(s // t,), in_specs=in_specs, out_specs=out_specs,
                 out_shape=out_shapes, compiler_params=_params(("arbitrary",)))(*args)


def _acc_vec(ref, val):
    _acc_row(ref, jnp.sum(val, axis=0, keepdims=True))


def _acc_row(ref, part):
    @pl.when(pl.program_id(0) == 0)
    def _():
        ref[...] = part

    @pl.when(pl.program_id(0) > 0)
    def _():
        ref[...] += part


def _rms(v):
    return lax.rsqrt(jnp.mean(v * v, axis=-1, keepdims=True) + EPS)


def _norm_bwd(dxn, xn, r):
    return r * (dxn - xn * jnp.mean(dxn * xn, axis=-1, keepdims=True))


def _premix(x, g, scale, shift, deps=()):
    s = x.shape[0]

    def body(x_ref, g_ref, sc_ref, sh_ref, h_ref):
        xv = x_ref[...]
        h_ref[...] = ((xv * _rms(xv) * g_ref[...]) * (1.0 + sc_ref[...]) + sh_ref[...]).astype(BF16)

    return _rows_call("premix", body, [(x, D_MODEL, 0)], [g, scale, shift], [(D_MODEL, BF16)], [], s, deps)[0]


def _sigmoid(z):
    return 1.0 / (1.0 + jnp.exp(-z))


def _mix_fwd(o_fox, o_gla, pm, g_fox, g_gla):
    s = o_fox.shape[0]

    def body(of_ref, og_ref, gr_ref, gf_ref, gg_ref, mix_ref):
        for h in range(FOX_HEADS):
            sl = slice(h * FOX_HEAD_DIM, (h + 1) * FOX_HEAD_DIM)
            seg = of_ref[:, sl]
            mix_ref[:, sl] = (seg * _rms(seg) * gf_ref[:, sl]).astype(BF16)
        for h in range(GLA_HEADS):
            sl = slice(h * GLA_DV, (h + 1) * GLA_DV)
            seg = og_ref[:, sl]
            gr = gr_ref[:, sl].astype(F32)
            val = (seg * _rms(seg) * gg_ref[:, sl]) * (gr * _sigmoid(gr))
            mix_ref[:, pl.ds(FOX_HEADS * FOX_HEAD_DIM + h * GLA_DV, GLA_DV)] = val.astype(BF16)

    return _rows_call("mix_fwd", body, [(o_fox, 1024, 0), (o_gla, 1024, 0), (pm, 1024, 5)], [g_fox, g_gla],
                      [(D_MODEL, BF16)], [], s)[0]


def _mix_bwd(dmix, o_fox, o_gla, pm, g_fox, g_gla, deps=()):
    s = o_fox.shape[0]

    def body(dm_ref, of_ref, og_ref, gr_ref, gf_ref, gg_ref, dof_ref, dog_ref, dgr_ref, dgf_ref, dgg_ref):
        dgf = []
        for h in range(FOX_HEADS):
            sl = slice(h * FOX_HEAD_DIM, (h + 1) * FOX_HEAD_DIM)
            seg = of_ref[:, sl]
            r = _rms(seg)
            segn = seg * r
            dout = dm_ref[:, sl]
            dgf.append(jnp.sum(dout * segn, axis=0, keepdims=True))
            dof_ref[:, sl] = _norm_bwd(dout * gf_ref[:, sl], segn, r).astype(BF16)
        dgg = []
        for h in range(GLA_HEADS):
            sl = slice(h * GLA_DV, (h + 1) * GLA_DV)
            seg = og_ref[:, sl]
            r = _rms(seg)
            segn = seg * r
            gl = segn * gg_ref[:, sl]
            gr = gr_ref[:, sl].astype(F32)
            sig = _sigmoid(gr)
            dout = dm_ref[:, pl.ds(FOX_HEADS * FOX_HEAD_DIM + h * GLA_DV, GLA_DV)]
            dgr_ref[:, sl] = (dout * gl * (sig * (1.0 + gr * (1.0 - sig)))).astype(BF16)
            dgl = dout * (gr * sig)
            dgg.append(jnp.sum(dgl * segn, axis=0, keepdims=True))
            dog_ref[:, sl] = _norm_bwd(dgl * gg_ref[:, sl], segn, r).astype(BF16)
        _acc_row(dgf_ref, jnp.concatenate(dgf, axis=1))
        _acc_row(dgg_ref, jnp.concatenate(dgg, axis=1))

    return _rows_call("mix_bwd", body, [(dmix, D_MODEL, 0), (o_fox, 1024, 0), (o_gla, 1024, 0), (pm, 1024, 5)],
                      [g_fox, g_gla], [(1024, BF16), (1024, BF16), (1024, BF16)], [1024, 1024], s, deps)


def _postmix_premlp(x, y, gate_m, g_post_mix, g_pre_mlp, scale_f, shift_f):
    s = x.shape[0]

    def body(x_ref, y_ref, gm_ref, gpm_ref, gpl_ref, sc_ref, sh_ref, x1_ref, h2_ref):
        yv = y_ref[...]
        x1 = x_ref[...] + gm_ref[...] * (yv * _rms(yv) * gpm_ref[...])
        x1_ref[...] = x1
        h2_ref[...] = ((x1 * _rms(x1) * gpl_ref[...]) * (1.0 + sc_ref[...]) + sh_ref[...]).astype(BF16)

    return _rows_call("postmix_premlp", body, [(x, D_MODEL, 0), (y, D_MODEL, 0)],
                      [gate_m, g_post_mix, g_pre_mlp, scale_f, shift_f], [(D_MODEL, F32), (D_MODEL, BF16)], [], s)


def _loss_postmlp_bwd(x1, y2, target, gate_f, g_post_mlp):
    s = x1.shape[0]

    def body(x1_ref, y2_ref, t_ref, gf_ref, g_ref, dx2_ref, dy2_ref, loss_ref, dgate_ref, dg_ref):
        yv = y2_ref[...]
        r = _rms(yv)
        yn = yv * r
        o = yn * g_ref[...]
        e = (x1_ref[...] + gf_ref[...] * o) - t_ref[...]
        part = 0.5 * jnp.sum(jnp.mean(e * e, axis=-1, keepdims=True), axis=0, keepdims=True)
        _acc_vec(loss_ref, jnp.broadcast_to(part, (1, 128)))
        dx2 = e * (1.0 / D_MODEL)
        dx2_ref[...] = dx2
        _acc_vec(dgate_ref, dx2 * o)
        do = dx2 * gf_ref[...]
        _acc_vec(dg_ref, do * yn)
        dy2_ref[...] = _norm_bwd(do * g_ref[...], yn, r).astype(BF16)

    return _rows_call("loss_postmlp_bwd", body, [(x1, D_MODEL, 0), (y2, D_MODEL, 0), (target, D_MODEL, 0)],
                      [gate_f, g_post_mlp], [(D_MODEL, F32), (D_MODEL, BF16)], [128, D_MODEL, D_MODEL], s)


def _premlp_postmix_bwd(dh2, dx2, x1, y, scale_f, g_pre_mlp, gate_m, g_post_mix, deps=()):
    s = x1.shape[0]

    def body(dh2_ref, dx2_ref, x1_ref, y_ref, sc_ref, gpl_ref, gm_ref, gpm_ref,
             dx1_ref, dy_ref, dsc_ref, dsh_ref, dgpl_ref, dgm_ref, dgpm_ref):
        x1 = x1_ref[...]
        r1 = _rms(x1)
        x1n = x1 * r1
        dh2 = dh2_ref[...]
        _acc_vec(dsc_ref, dh2 * (x1n * gpl_ref[...]))
        _acc_vec(dsh_ref, dh2)
        dn2 = dh2 * (1.0 + sc_ref[...])
        _acc_vec(dgpl_ref, dn2 * x1n)
        dx1 = dx2_ref[...] + _norm_bwd(dn2 * gpl_ref[...], x1n, r1)
        dx1_ref[...] = dx1
        yv = y_ref[...]
        ry = _rms(yv)
        yn = yv * ry
        _acc_vec(dgm_ref, dx1 * (yn * gpm_ref[...]))
        do = dx1 * gm_ref[...]
        _acc_vec(dgpm_ref, do * yn)
        dy_ref[...] = _norm_bwd(do * gpm_ref[...], yn, ry).astype(BF16)

    return _rows_call("premlp_postmix_bwd", body,
                      [(dh2, D_MODEL, 0), (dx2, D_MODEL, 0), (x1, D_MODEL, 0), (y, D_MODEL, 0)],
                      [scale_f, g_pre_mlp, gate_m, g_post_mix], [(D_MODEL, F32), (D_MODEL, BF16)],
                      [D_MODEL] * 5, s, deps)


def _premix_bwd(dh, dx1, x, g_pre_mix, scale_m):
    s = x.shape[0]

    def body(dh_ref, dx1_ref, x_ref, g_ref, sc_ref, gx_ref, dsc_ref, dsh_ref, dg_ref):
        xv = x_ref[...]
        r = _rms(xv)
        xn = xv * r
        dh = dh_ref[...]
        _acc_vec(dsc_ref, dh * (xn * g_ref[...]))
        _acc_vec(dsh_ref, dh)
        dn1 = dh * (1.0 + sc_ref[...])
        _acc_vec(dg_ref, dn1 * xn)
        gx_ref[...] = dx1_ref[...] + _norm_bwd(dn1 * g_ref[...], xn, r)

    return _rows_call("premix_bwd", body, [(dh, D_MODEL, 0), (dx1, D_MODEL, 0), (x, D_MODEL, 0)],
                      [g_pre_mix, scale_m], [(D_MODEL, F32)], [D_MODEL] * 3, s)


def _split3(v):
    hi = v.astype(BF16)
    r1 = v - hi.astype(F32)
    mid = r1.astype(BF16)
    lo = (r1 - mid.astype(F32)).astype(BF16)
    return hi, mid, lo


def _dot_exact01(v, tri, contract=NN, tri_first=False):
    acc = None
    for part in _split3(v):
        lhs, rhs = (tri, part) if tri_first else (part, tri)
        p = lax.dot_general(lhs, rhs, (contract, ((), ())), preferred_element_type=F32)
        acc = p if acc is None else acc + p
    return acc


def _log_sigmoid(z):
    return jnp.minimum(z, 0.0) - jnp.log(1.0 + jnp.exp(-jnp.abs(z)))


def _fox_cum(small, bvec):
    s = small.shape[0]
    t = _tile(s, CUM_T)

    def body(sm_ref, b_ref, out_ref, carry):
        @pl.when(pl.program_id(0) == 0)
        def _():
            carry[...] = jnp.zeros_like(carry)

        lf = _log_sigmoid(sm_ref[...] + b_ref[...])
        lft = lf.T[0:FOX_HEADS, :]
        row = lax.broadcasted_iota(jnp.int32, (t, t), 0)
        col = lax.broadcasted_iota(jnp.int32, (t, t), 1)
        upper = (row <= col).astype(BF16)
        cum = _dot_exact01(lft, upper) + carry[:, 0:1]
        out_ref[...] = cum
        carry[...] = carry[...] + jnp.sum(lft, axis=1, keepdims=True)

    return _call(body, name="fox_cum", grid=(s // t,),
                 in_specs=[pl.BlockSpec((t, W_SMALL), lambda i: (i, 0)), pl.BlockSpec((1, W_SMALL), lambda i: (0, 0))],
                 out_specs=pl.BlockSpec((FOX_HEADS, t), lambda i: (0, i)),
                 out_shape=jax.ShapeDtypeStruct((FOX_HEADS, s), F32),
                 scratch_shapes=[pltpu.VMEM((FOX_HEADS, 128), F32)],
                 compiler_params=_params(("arbitrary",)))(small, bvec)


def _fox_cum_bwd(dc, dcq, small, bvec):
    s = small.shape[0]
    t = _tile(s, CUM_T)
    nb = s // t

    def body(dc_ref, dcq_ref, sm_ref, b_ref, out_ref, db_ref, carry):
        @pl.when(pl.program_id(0) == 0)
        def _():
            carry[...] = jnp.zeros_like(carry)
            db_ref[...] = jnp.zeros_like(db_ref)

        lane = lax.broadcasted_iota(jnp.int32, (t, W_SMALL), 1)
        dcq = jnp.zeros((t, W_SMALL), F32)
        for hh in range(FOX_HEADS):
            dcq = jnp.where(lane == hh, dcq_ref[hh], dcq)
        dcv = dc_ref[...] + dcq.T[0:FOX_HEADS, :]
        row = lax.broadcasted_iota(jnp.int32, (t, t), 0)
        col = lax.broadcasted_iota(jnp.int32, (t, t), 1)
        lower = (row >= col).astype(BF16)
        dlf = _dot_exact01(dcv, lower) + carry[:, 0:1]
        carry[...] = carry[...] + jnp.sum(dcv, axis=1, keepdims=True)
        z = sm_ref[...] + b_ref[...]
        zt = z.T[0:FOX_HEADS, :]
        dff = dlf * _sigmoid(-zt)
        db_ref[...] = db_ref[...] + jnp.sum(dff, axis=1, keepdims=True)
        full = jnp.concatenate([dff, jnp.zeros((W_SMALL - FOX_HEADS, t), F32)], axis=0)
        out_ref[...] = full.T

    return _call(body, name="fox_cum_bwd", grid=(nb,),
                 in_specs=[pl.BlockSpec((FOX_HEADS, t), lambda i: (0, nb - 1 - i)),
                           pl.BlockSpec((FOX_HEADS, t, 1), lambda i: (0, nb - 1 - i, 0)),
                           pl.BlockSpec((t, W_SMALL), lambda i: (nb - 1 - i, 0)),
                           pl.BlockSpec((1, W_SMALL), lambda i: (0, 0))],
                 out_specs=[pl.BlockSpec((t, W_SMALL), lambda i: (nb - 1 - i, 0)),
                            pl.BlockSpec((FOX_HEADS, 128), lambda i: (0, 0))],
                 out_shape=[jax.ShapeDtypeStruct((s, W_SMALL), F32), jax.ShapeDtypeStruct((FOX_HEADS, 128), F32)],
                 scratch_shapes=[pltpu.VMEM((FOX_HEADS, 128), F32)],
                 compiler_params=_params(("arbitrary",)))(dc, dcq, small, bvec)


FOX_SCALE = FOX_HEAD_DIM ** -0.5


def _fox_fwd(pm, crow):
    s = pm.shape[0]
    t = _tile(s, FOX_T)
    nb = s // t
    parts = 2
    hq = t // parts

    def body(q_ref, k_ref, v_ref, c_ref, o_ref, lse_ref):
        i = pl.program_id(1)
        qs = [q_ref[g * hq:(g + 1) * hq, :] for g in range(parts)]

        def block(j, carry, diagonal):
            rows = pl.ds(pl.multiple_of(j * t, t), t)
            kb, vb, cb = k_ref[rows, :], v_ref[rows, :], c_ref[j]
            out = []
            for g, (m_prev, l_prev, acc) in enumerate(carry):
                sc = lax.dot_general(qs[g], kb, (NT, ((), ())), preferred_element_type=F32)
                sc = sc * FOX_SCALE - cb
                if diagonal:
                    row = lax.broadcasted_iota(jnp.int32, (hq, t), 0) + g * hq
                    col = lax.broadcasted_iota(jnp.int32, (hq, t), 1)
                    sc = jnp.where(row >= col, sc, NEG)
                m_new = jnp.maximum(m_prev, jnp.max(sc, axis=1, keepdims=True))
                alpha = jnp.exp(m_prev - m_new)
                p = jnp.exp(sc - m_new)
                l_new = alpha * l_prev + jnp.sum(p, axis=1, keepdims=True)
                p_hi = p.astype(BF16)
                p_lo = (p - p_hi.astype(F32)).astype(BF16)
                pv = jnp.dot(p_hi, vb, preferred_element_type=F32)
                pv = pv + jnp.dot(p_lo, vb, preferred_element_type=F32)
                out.append((m_new, l_new, alpha * acc + pv))
            return tuple(out)

        init = tuple((jnp.full((hq, 1), NEG, F32), jnp.zeros((hq, 1), F32), jnp.zeros((hq, 128), F32))
                     for _ in range(parts))
        carry = lax.fori_loop(0, i, lambda j, cr: block(j, cr, False), init)
        carry = block(i, carry, True)
        for g, (m_fin, l_fin, acc) in enumerate(carry):
            o_ref[g * hq:(g + 1) * hq, :] = acc / l_fin
            lse_ref[g * hq:(g + 1) * hq, :] = m_fin + jnp.log(l_fin)

    return _call(
        body, name="fox_fwd", grid=(FOX_HEADS, nb),
        in_specs=[pl.BlockSpec((t, 128), lambda h, i: (i, h)),
                  pl.BlockSpec((s, 128), lambda h, i: (0, FOX_HEADS + h)),
                  pl.BlockSpec((s, 128), lambda h, i: (0, 2 * FOX_HEADS + h)),
                  pl.BlockSpec((None, nb, 1, t), lambda h, i: (h, 0, 0, 0))],
        out_specs=[pl.BlockSpec((t, 128), lambda h, i: (i, h)),
                   pl.BlockSpec((None, t, 1), lambda h, i: (h, i, 0))],
        out_shape=[jax.ShapeDtypeStruct((s, FOX_HEADS * 128), F32), jax.ShapeDtypeStruct((FOX_HEADS, s, 1), F32)],
        compiler_params=_params(("parallel", "arbitrary")),
    )(pm, pm, pm, crow.reshape(FOX_HEADS, nb, 1, t))


def _fox_bwd(pm, crow, o, lse, do):
    s = pm.shape[0]
    t = _tile(s, FOX_T)
    nb = s // t

    parts = 2
    hq = t // parts

    def body(q_ref, do_ref, o_ref, lse_ref, k_ref, v_ref, c_ref, dq_ref, dk_ref, dv_ref, dc_ref, dcq_ref, delta_s):
        j = pl.program_id(1)

        @pl.when(j == 0)
        def _():
            dq_ref[...] = jnp.zeros_like(dq_ref)
            dcq_ref[...] = jnp.zeros_like(dcq_ref)
            delta_s[...] = jnp.sum(do_ref[...].astype(F32) * o_ref[...], axis=1, keepdims=True)

        kb, vb, cb = k_ref[...], v_ref[...], c_ref[...]

        def block(i, carry, diagonal):
            dk_acc, dv_acc, dc_acc = carry
            for g in range(parts):
                rows = pl.ds(pl.multiple_of(i * t + g * hq, hq), hq)
                q, dov = q_ref[rows, :], do_ref[rows, :]
                sc = lax.dot_general(q, kb, (NT, ((), ())), preferred_element_type=F32)
                p = jnp.exp(sc * FOX_SCALE - cb - lse_ref[rows, :])
                if diagonal:
                    row = lax.broadcasted_iota(jnp.int32, (hq, t), 0) + g * hq
                    col = lax.broadcasted_iota(jnp.int32, (hq, t), 1)
                    p = jnp.where(row >= col, p, 0.0)
                dp = lax.dot_general(dov, vb, (NT, ((), ())), preferred_element_type=F32)
                ds = p * (dp - delta_s[rows, :])
                dsb = ds.astype(BF16)
                dv_acc = dv_acc + lax.dot_general(p.astype(BF16), dov, (TN, ((), ())), preferred_element_type=F32)
                dk_acc = dk_acc + lax.dot_general(dsb, q, (TN, ((), ())), preferred_element_type=F32)
                dq_ref[rows, :] += jnp.dot(dsb, kb, preferred_element_type=F32) * FOX_SCALE
                dc_acc = dc_acc - jnp.sum(ds, axis=0, keepdims=True)
                dcq_ref[rows, :] += jnp.sum(ds, axis=1, keepdims=True)
            return dk_acc, dv_acc, dc_acc

        carry = (jnp.zeros((t, 128), F32), jnp.zeros((t, 128), F32), jnp.zeros((1, t), F32))
        carry = block(j, carry, True)
        dk_acc, dv_acc, dc_acc = lax.fori_loop(j + 1, nb, lambda i, cr: block(i, cr, False), carry)
        dk_ref[...] = dk_acc * FOX_SCALE
        dv_ref[...] = dv_acc
        dc_ref[...] = dc_acc

    whole = lambda h, j: (0, h)
    return _call(
        body, name="fox_bwd", grid=(FOX_HEADS, nb),
        in_specs=[pl.BlockSpec((s, 128), whole), pl.BlockSpec((s, 128), whole), pl.BlockSpec((s, 128), whole),
                  pl.BlockSpec((None, s, 1), lambda h, j: (h, 0, 0)),
                  pl.BlockSpec((t, 128), lambda h, j: (j, FOX_HEADS + h)),
                  pl.BlockSpec((t, 128), lambda h, j: (j, 2 * FOX_HEADS + h)),
                  pl.BlockSpec((None, 1, t), lambda h, j: (h, 0, j))],
        out_specs=[pl.BlockSpec((s, 128), whole),
                   pl.BlockSpec((t, 128), lambda h, j: (j, h)),
                   pl.BlockSpec((t, 128), lambda h, j: (j, h)),
                   pl.BlockSpec((None, 1, t), lambda h, j: (h, 0, j)),
                   pl.BlockSpec((None, s, 1), lambda h, j: (h, 0, 0))],
        out_shape=[jax.ShapeDtypeStruct((s, 1024), F32), jax.ShapeDtypeStruct((s, 1024), F32),
                   jax.ShapeDtypeStruct((s, 1024), F32), jax.ShapeDtypeStruct((FOX_HEADS, 1, s), F32),
                   jax.ShapeDtypeStruct((FOX_HEADS, s, 1), F32)],
        scratch_shapes=[pltpu.VMEM((s, 1), F32)],
        compiler_params=_params(("parallel", "arbitrary")),
    )(pm, do, o, lse, pm, pm, crow)


GLA_SCALE = GLA_DK ** -0.5
GLA_Q_BLK = 3072 // 128
GLA_K_BLK = 3584 // 128
GLA_V_BLK = 4096 // 256


def _gla_gate(sm, wa_ref, b_ref):
    return jnp.dot(sm.astype(BF16), wa_ref[...], preferred_element_type=F32) + b_ref[...]


def _tri(n, strict):
    row = lax.broadcasted_iota(jnp.int32, (n, n), 0)
    col = lax.broadcasted_iota(jnp.int32, (n, n), 1)
    return ((row > col) if strict else (row >= col)).astype(BF16)


def _gla_fwd(pm, small, wa_pad, b_a2):
    s = pm.shape[0]
    r = _tile(s, GLA_R)
    nc = r // CHUNK

    def body(q_ref, k_ref, v_ref, sm_ref, wa_ref, b_ref, o_ref, st_ref, state):
        @pl.when(pl.program_id(1) == 0)
        def _():
            state[...] = jnp.zeros_like(state)

        tri = _tri(CHUNK, False)
        for c in range(nc):
            rows = slice(c * CHUNK, (c + 1) * CHUNK)
            la = _log_sigmoid(_gla_gate(sm_ref[rows, :], wa_ref, b_ref)) * (1.0 / GLA_TEMP)
            cum = _dot_exact01(la, tri, tri_first=True)
            total = jnp.sum(la, axis=0, keepdims=True)
            kdec = k_ref[rows, :].astype(F32) * jnp.exp(total - cum)
            ut = lax.dot_general(v_ref[rows, :], kdec.astype(BF16), (TN, ((), ())), preferred_element_type=F32)
            new = state[...] * jnp.exp(total) + ut
            state[...] = new
            newb = new.astype(BF16)
            st_ref[c] = newb
            qs = (q_ref[rows, :].astype(F32) * GLA_SCALE).astype(BF16)
            o_ref[rows, :] = lax.dot_general(qs, newb, (NT, ((), ())), preferred_element_type=F32)

    return _call(
        body, name="gla_fwd", grid=(GLA_HEADS, s // r),
        in_specs=[pl.BlockSpec((r, 128), lambda h, i: (i, GLA_Q_BLK + h)),
                  pl.BlockSpec((r, 128), lambda h, i: (i, GLA_K_BLK + h)),
                  pl.BlockSpec((r, 256), lambda h, i: (i, GLA_V_BLK + h)),
                  pl.BlockSpec((r, W_SMALL), lambda h, i: (i, 0)),
                  pl.BlockSpec((W_SMALL, 128), lambda h, i: (0, h)),
                  pl.BlockSpec((1, 128), lambda h, i: (0, h))],
        out_specs=[pl.BlockSpec((r, 256), lambda h, i: (i, h)),
                   pl.BlockSpec((nc, None, GLA_DV, GLA_DK), lambda h, i: (i, h, 0, 0))],
        out_shape=[jax.ShapeDtypeStruct((s, 1024), F32),
                   jax.ShapeDtypeStruct((s // CHUNK, GLA_HEADS, GLA_DV, GLA_DK), BF16)],
        scratch_shapes=[pltpu.VMEM((GLA_DV, GLA_DK), F32)],
        compiler_params=_params(("parallel", "arbitrary")),
    )(pm, pm, pm, small, wa_pad, b_a2)


def _gla_bwd(pm, small, wa_pad, b_a2, states, do):
    s = pm.shape[0]
    r = _tile(s, GLA_R)
    nc = r // CHUNK
    nb = s // r

    def body(q_ref, k_ref, v_ref, sm_ref, wa_ref, b_ref, do_ref, st_ref, prev_ref,
             dq_ref, dk_ref, dv_ref, dza_ref, db_ref, carry):
        step = pl.program_id(1)

        @pl.when(step == 0)
        def _():
            carry[...] = jnp.zeros_like(carry)
            db_ref[...] = jnp.zeros_like(db_ref)

        tri = _tri(CHUNK, False)
        tri_strict = _tri(CHUNK, True)
        db = jnp.zeros((1, 128), F32)
        for c in reversed(range(nc)):
            rows = slice(c * CHUNK, (c + 1) * CHUNK)
            z = _gla_gate(sm_ref[rows, :], wa_ref, b_ref)
            la = _log_sigmoid(z) * (1.0 / GLA_TEMP)
            cum = _dot_exact01(la, tri, tri_first=True)
            total = jnp.sum(la, axis=0, keepdims=True)
            w = jnp.exp(total - cum)
            decay = jnp.exp(total)
            kdec = k_ref[rows, :].astype(F32) * w
            dov = do_ref[rows, :]
            qs = (q_ref[rows, :].astype(F32) * GLA_SCALE).astype(BF16)
            dq_ref[rows, :] = jnp.dot(dov, st_ref[c], preferred_element_type=F32) * GLA_SCALE
            gt = lax.dot_general(dov, qs, (TN, ((), ())), preferred_element_type=F32) + carry[...]
            gtb = gt.astype(BF16)
            dv_ref[rows, :] = lax.dot_general(kdec.astype(BF16), gtb, (NT, ((), ())), preferred_element_type=F32)
            dkdec = jnp.dot(v_ref[rows, :], gtb, preferred_element_type=F32)
            dk_ref[rows, :] = dkdec * w
            e = dkdec * kdec
            if c > 0:
                prev = st_ref[c - 1].astype(F32)
            else:
                prev = jnp.where(step == nb - 1, 0.0, prev_ref[0].astype(F32))
            dtot = jnp.sum(gt * prev, axis=0, keepdims=True) * decay
            dla = dtot + _dot_exact01(e, tri_strict, tri_first=True)
            dza = dla * (1.0 / GLA_TEMP) * _sigmoid(-z)
            dza_ref[rows, :] = dza.astype(BF16)
            db = db + jnp.sum(dza, axis=0, keepdims=True)
            carry[...] = gt * decay
        db_ref[...] += db

    blk = lambda h, i: nb - 1 - i
    return _call(
        body, name="gla_bwd", grid=(GLA_HEADS, nb),
        in_specs=[pl.BlockSpec((r, 128), lambda h, i: (blk(h, i), GLA_Q_BLK + h)),
                  pl.BlockSpec((r, 128), lambda h, i: (blk(h, i), GLA_K_BLK + h)),
                  pl.BlockSpec((r, 256), lambda h, i: (blk(h, i), GLA_V_BLK + h)),
                  pl.BlockSpec((r, W_SMALL), lambda h, i: (blk(h, i), 0)),
                  pl.BlockSpec((W_SMALL, 128), lambda h, i: (0, h)),
                  pl.BlockSpec((1, 128), lambda h, i: (0, h)),
                  pl.BlockSpec((r, 256), lambda h, i: (blk(h, i), h)),
                  pl.BlockSpec((nc, None, GLA_DV, GLA_DK), lambda h, i: (blk(h, i), h, 0, 0)),
                  pl.BlockSpec((1, None, GLA_DV, GLA_DK),
                               lambda h, i: (jnp.maximum(blk(h, i) * nc - 1, 0), h, 0, 0))],
        out_specs=[pl.BlockSpec((r, 128), lambda h, i: (blk(h, i), h)),
                   pl.BlockSpec((r, 128), lambda h, i: (blk(h, i), h)),
                   pl.BlockSpec((r, 256), lambda h, i: (blk(h, i), h)),
                   pl.BlockSpec((r, 128), lambda h, i: (blk(h, i), h)),
                   pl.BlockSpec((1, 128), lambda h, i: (0, h))],
        out_shape=[jax.ShapeDtypeStruct((s, 512), F32), jax.ShapeDtypeStruct((s, 512), F32),
                   jax.ShapeDtypeStruct((s, 1024), F32), jax.ShapeDtypeStruct((s, 512), BF16),
                   jax.ShapeDtypeStruct((1, 512), F32)],
        scratch_shapes=[pltpu.VMEM((GLA_DV, GLA_DK), F32)],
        compiler_params=_params(("parallel", "arbitrary")),
    )(pm, pm, pm, small, wa_pad, b_a2, do, states, states)


def _modulation(c_all, w_ada):
    n = w_ada.shape[1]
    tn = _tile(n, 512)

    def body(c_ref, w_ref, out_ref, ca_ref):
        cv = c_ref[...]
        ca = cv * _sigmoid(cv)
        ca_ref[...] = ca
        out_ref[...] = jnp.dot(ca.astype(BF16), w_ref[...].astype(BF16), preferred_element_type=F32)

    return _call(body, name="modulation", grid=(n // tn,),
                 in_specs=[pl.BlockSpec((N_DEV, D_MODEL), lambda j: (0, 0)),
                           pl.BlockSpec((D_MODEL, tn), lambda j: (0, j))],
                 out_specs=[pl.BlockSpec((N_DEV, tn), lambda j: (0, j)),
                            pl.BlockSpec((N_DEV, D_MODEL), lambda j: (0, 0))],
                 out_shape=[jax.ShapeDtypeStruct((N_DEV, n), F32), jax.ShapeDtypeStruct((N_DEV, D_MODEL), F32)],
                 compiler_params=_params(("arbitrary",)))(c_all, w_ada)


def _adamw_math(w, g, m, v):
    m = ADAM_B1 * m + (1.0 - ADAM_B1) * g
    v = ADAM_B2 * v + (1.0 - ADAM_B2) * (g * g)
    m_hat = m / (1.0 - ADAM_B1 ** ADAM_STEP)
    v_hat = v / (1.0 - ADAM_B2 ** ADAM_STEP)
    delta = -ADAM_LR * (m_hat / (jnp.sqrt(v_hat) + ADAM_EPS) + ADAM_WD * w)
    return delta, m, v


def _adamw_slabs(name, w, slabs, m, v, tr=256):
    rr, cc = w.shape
    tr = _tile(rr, tr)

    def body(w_ref, s_ref, m_ref, v_ref, g_ref, d_ref, nm_ref, nv_ref):
        g = s_ref[0].astype(F32)
        for r in range(1, N_DEV):
            g = g + s_ref[r].astype(F32)
        g_ref[...] = g
        d, nm, nv = _adamw_math(w_ref[...], g, m_ref[...], v_ref[...])
        d_ref[...] = d
        nm_ref[...] = nm
        nv_ref[...] = nv

    spec = pl.BlockSpec((tr, cc), lambda i: (i, 0))
    return _call(body, name=name, grid=(rr // tr,),
                 in_specs=[spec, pl.BlockSpec((N_DEV, tr, cc), lambda i: (0, i, 0)), spec, spec],
                 out_specs=[spec] * 4, out_shape=[jax.ShapeDtypeStruct((rr, cc), F32)] * 4,
                 compiler_params=_params(("parallel",)))(w, slabs, m, v)


def _adamw_ada(w, cat, dm, m, v, tr=256):
    rr, cc = w.shape
    tr = _tile(rr, tr)

    def body(w_ref, ca_ref, dm_ref, m_ref, v_ref, g_ref, d_ref, nm_ref, nv_ref):
        g = ca_ref[:, 0:1] * dm_ref[0:1, :]
        for b in range(1, N_DEV):
            g = g + ca_ref[:, b:b + 1] * dm_ref[b:b + 1, :]
        g_ref[...] = g
        d, nm, nv = _adamw_math(w_ref[...], g, m_ref[...], v_ref[...])
        d_ref[...] = d
        nm_ref[...] = nm
        nv_ref[...] = nv

    spec = pl.BlockSpec((tr, cc), lambda i: (i, 0))
    return _call(body, name="adamw_ada", grid=(rr // tr,),
                 in_specs=[spec, pl.BlockSpec((tr, N_DEV), lambda i: (i, 0)),
                           pl.BlockSpec((N_DEV, cc), lambda i: (0, 0)), spec, spec],
                 out_specs=[spec] * 4, out_shape=[jax.ShapeDtypeStruct((rr, cc), F32)] * 4,
                 compiler_params=_params(("parallel",)))(w, cat, dm, m, v)


def _sum_devices(gathered):
    ln = gathered.shape[-1]

    def body(g_ref, out_ref):
        acc = g_ref[0]
        for r in range(1, N_DEV):
            acc = acc + g_ref[r]
        out_ref[...] = acc

    return _call(body, name="sum_devices",
                 in_specs=[pl.BlockSpec(memory_space=pltpu.VMEM)], out_specs=pl.BlockSpec(memory_space=pltpu.VMEM),
                 out_shape=jax.ShapeDtypeStruct((1, ln), F32))(gathered)


def _adamw_flat(w, g, m, v):
    def body(w_ref, g_ref, m_ref, v_ref, d_ref, nm_ref, nv_ref):
        d, nm, nv = _adamw_math(w_ref[...], g_ref[...], m_ref[...], v_ref[...])
        d_ref[...] = d
        nm_ref[...] = nm
        nv_ref[...] = nv

    vm = pl.BlockSpec(memory_space=pltpu.VMEM)
    return _call(body, name="adamw_small", in_specs=[vm] * 4, out_specs=[vm] * 3,
                 out_shape=[jax.ShapeDtypeStruct(w.shape, F32)] * 3)(w, g, m, v)


def _from_col_shards(g):
    return jnp.transpose(g, (1, 0, 2)).reshape(g.shape[1], N_DEV * g.shape[2])


def _pad_lanes(v, n):
    return jnp.concatenate([v, jnp.zeros(v.shape[:-1] + (n - v.shape[-1],), v.dtype)], axis=-1)


def kernel(x, c, w_ada, b_ada, g_pre_mix, g_post_mix, w_in, b_fgate, w_gla_a2, b_gla_a2, g_fox_out, g_gla_out, w_out, g_pre_mlp, g_post_mlp, w_mlp_in, w_mlp_out, loss_target, m_w_ada, m_b_ada, m_g_pre_mix, m_g_post_mix, m_w_in, m_b_fgate, m_w_gla_a2, m_b_gla_a2, m_g_fox_out, m_g_gla_out, m_w_out, m_g_pre_mlp, m_g_post_mlp, m_w_mlp_in, m_w_mlp_out, v_w_ada, v_b_ada, v_g_pre_mix, v_g_post_mix, v_w_in, v_b_fgate, v_w_gla_a2, v_b_gla_a2, v_g_fox_out, v_g_gla_out, v_w_out, v_g_pre_mlp, v_g_post_mlp, v_w_mlp_in, v_w_mlp_out):
    rank = _my_rank()
    xs = x[0]
    s = xs.shape[0]
    target = loss_target[0]

    c_all, wa2_g, ggla_g, win_g = _all_gather("gather_first", [c, w_gla_a2[0], g_gla_out[0], w_in[0].astype(BF16)])
    rest = [_own_slot("own_w_out", w_out[0], True, rank), _own_slot("own_w_mlp_in", w_mlp_in[0], True, rank),
            _own_slot("own_w_mlp_out", w_mlp_out[0], True, rank)]
    gs_send, gs_recv, _, gs_land, gs_token = _exchange_start("gather_rest_start", rest, after=(c_all,))
    w_a2 = _from_col_shards(wa2_g)
    g_gla = _from_col_shards(ggla_g).reshape(1, 1024)
    g_fox = g_fox_out.reshape(1, 1024)
    win_full = _from_col_shards(win_g)
    w_main = jnp.concatenate([win_full[:, :3072], win_full[:, 3080:5128], win_full[:, 5144:6168]], axis=1)
    w_small = _pad_lanes(jnp.concatenate([win_full[:, 3072:3080], win_full[:, 5128:5144]], axis=1), W_SMALL)
    wa_pad =jnp.concatenate([jnp.zeros((8, 512), BF16), w_a2.astype(BF16), jnp.zeros((104, 512), BF16)], axis=0)
    bf_vec = _pad_lanes(b_fgate, W_SMALL)

    mod_part, c_act = _modulation(c_all.reshape(N_DEV, D_MODEL), w_ada[0])
    (mod_g,) = _all_gather("gather_mod", [mod_part])
    mod = lax.dynamic_slice_in_dim(mod_g, rank, 1, axis=1).reshape(1, 6 * D_MODEL) + b_ada
    shift_m, scale_m, gate_m, shift_f, scale_f, gate_f = [mod[:, i * D_MODEL:(i + 1) * D_MODEL] for i in range(6)]

    h = _premix(xs, g_pre_mix, scale_m, shift_m, deps=(gs_token,))
    pm = _mm_plain("proj_main", h, w_main, NN, BF16)
    small = _mm_plain("proj_small", h, w_small, NN, F32)
    crow = _fox_cum(small, bf_vec).reshape(FOX_HEADS, 1, s)
    o_fox, lse = _fox_fwd(pm, crow)
    o_gla, states = _gla_fwd(pm, small, wa_pad, b_gla_a2)
    mix = _mix_fwd(o_fox, o_gla, pm, g_fox, g_gla)
    wout_g, wmi_g, wmo_g = _exchange_wait("gather_rest_wait", gs_send, gs_recv, None, gs_land, mix)
    w_out_full = wout_g.reshape(D_MODEL, D_MODEL)
    w_mo_full = wmo_g.reshape(D_FF, D_MODEL)
    y = _mm_plain("out_proj", mix, w_out_full, NN, F32)
    x1, h2 = _postmix_premlp(xs, y, gate_m, g_post_mix, g_pre_mlp, scale_f, shift_f)

    tm, tn, tk = _tile(s, 1024), 1024, 2048
    nsh = 1024 // tn

    def relu2(acc):
        rl = jnp.maximum(acc, 0.0)
        return rl * rl, rl

    z, a_relu = _matmul(
        "mlp_in", h2, wmi_g, contract=NN, grid=(s // tm, D_FF // tn, D_MODEL // tk),
        a_spec=pl.BlockSpec((tm, tk), lambda i, j, k: (i, k)),
        b_spec=pl.BlockSpec((None, tk, tn), lambda i, j, k: (j // nsh, k, j % nsh)),
        out_specs=[pl.BlockSpec((tm, tn), lambda i, j, k: (i, j))] * 2,
        out_shapes=[jax.ShapeDtypeStruct((s, D_FF), BF16)] * 2, acc_shape=(tm, tn), epilogue=relu2)
    y2 = _mm_plain("mlp_out", z, w_mo_full, NN, F32)

    dx2, dy2, loss_vec, dgate_f, dg_post_mlp = _loss_postmlp_bwd(x1, y2, target, gate_f, g_post_mlp)
    loss = lax.psum(loss_vec[0, 0], ("x", "y", "c"))

    da = _mm_plain("mlp_out_dx", dy2, w_mo_full, NT, BF16, extra=(a_relu,),
                   epilogue=lambda acc, rl: (acc * (2.0 * rl.astype(F32)),))
    dw_mo = _mm_plain("mlp_out_dw", z, dy2, TN, BF16)
    dw_mo = dw_mo.reshape(N_DEV, 1024, D_MODEL)
    x_mo = _exchange_start("grad_mlp_out_start", [_own_slot("own_dw_mlp_out", dw_mo, False, rank)], [dw_mo])
    tkx = 1024
    (dh2,) = _matmul(
        "mlp_in_dx", da, wmi_g, contract=NT, grid=(s // tm, D_MODEL // tn, D_FF // tkx),
        a_spec=pl.BlockSpec((tm, tkx), lambda i, j, k: (i, k)),
        b_spec=pl.BlockSpec((None, tn, tkx), lambda i, j, k: (k, j, 0)),
        out_specs=[pl.BlockSpec((tm, tn), lambda i, j, k: (i, j))],
        out_shapes=[jax.ShapeDtypeStruct((s, D_MODEL), F32)], acc_shape=(tm, tn), deps=(x_mo[4],))
    ts = _tile(s, 2048)
    (dw_mi,) = _matmul(
        "mlp_in_dw", h2, da, contract=TN, grid=(D_MODEL // 1024, D_FF // tn, s // ts),
        a_spec=pl.BlockSpec((ts, 1024), lambda i, j, k: (k, i)),
        b_spec=pl.BlockSpec((ts, tn), lambda i, j, k: (k, j)),
        out_specs=[pl.BlockSpec((None, 1024, tn), lambda i, j, k: (j // nsh, i, j % nsh))],
        out_shapes=[jax.ShapeDtypeStruct((N_DEV, D_MODEL, 1024), BF16)], acc_shape=(1024, tn))
    x_mi = _exchange_start("grad_mlp_in_start", [_own_slot("own_dw_mlp_in", dw_mi, False, rank)], [dw_mi])

    dx1, dy, dscale_f, dshift_f, dg_pre_mlp, dgate_m, dg_post_mix = _premlp_postmix_bwd(
        dh2, dx2, x1, y, scale_f, g_pre_mlp, gate_m, g_post_mix, deps=(x_mi[4],))

    dmix = _mm_plain("out_proj_dx", dy, w_out_full, NT, F32)
    dw_out = _mm_plain("out_proj_dw", mix, dy, TN, BF16)
    dw_out = dw_out.reshape(N_DEV, 256, D_MODEL)
    x_out = _exchange_start("grad_out_start", [_own_slot("own_dw_out", dw_out, False, rank)], [dw_out])
    do_fox, do_gla, dgr, dg_fox, dg_gla = _mix_bwd(dmix, o_fox, o_gla, pm, g_fox, g_gla, deps=(x_out[4],))

    dq, dk, dv, dc, dcq = _fox_bwd(pm, crow, o_fox, lse, do_fox)
    dsmall_f, db_f = _fox_cum_bwd(dc.reshape(FOX_HEADS, s), dcq, small, bf_vec)
    dgq, dgk, dgv, dza, db_a2 = _gla_bwd(pm, small, wa_pad, b_gla_a2, states, do_gla)
    dsmall = _mm_plain("gate_dx", dza, wa_pad, NT, F32, tn=128, extra=(dsmall_f,),
                       epilogue=lambda acc, other: (acc + other,))
    dwa_pad = _mm_plain("gate_dw", small, dza, TN, F32, tm=128, tn=512)

    dpm = jnp.concatenate([dq.astype(BF16), dk.astype(BF16), dv.astype(BF16), dgq.astype(BF16), dgk.astype(BF16),
                           dgv.astype(BF16), dgr], axis=1)
    dw_main = _mm_plain("proj_main_dw", h, dpm, TN, BF16)
    dw_small = _mm_plain("proj_small_dw", h, dsmall, TN, BF16, tn=128)
    dwin_full = jnp.concatenate([dw_main[:, :3072], dw_small[:, 0:8], dw_main[:, 3072:5120], dw_small[:, 8:24],
                                 dw_main[:, 5120:6144]], axis=1)
    dwin_slabs = jnp.transpose(dwin_full.reshape(D_MODEL, N_DEV, 771), (1, 0, 2))
    x_in = _exchange_start("grad_in_start", [_own_slot("own_dw_in", dwin_slabs, False, rank)], [dwin_slabs])
    dh_small = _mm_plain("proj_small_dx", dsmall, w_small, NT, F32, tk=128)
    dh = _mm_plain("proj_main_dx", dpm, w_main, NT, F32, extra=(dh_small,),
                   epilogue=lambda acc, other: (acc + other,), deps=(x_in[4],))
    grad_x, dscale_m, dshift_m, dg_pre_mix = _premix_bwd(dh, dx1, xs, g_pre_mix, scale_m)

    dmod = jnp.concatenate([dshift_m, dscale_m, dgate_m, dshift_f, dscale_f, dgate_f], axis=1)
    flat = jnp.concatenate(
        [dmod, dg_pre_mix, dg_post_mix, dg_fox, dg_pre_mlp, dg_post_mlp, db_a2,
         dwa_pad[8:24, :].reshape(1, GLA_RANK * 512), dg_gla, _pad_lanes(db_f[:, 0].reshape(1, FOX_HEADS), 128)],
        axis=1)
    (flat_g,) = _all_gather("gather_small_grads", [flat])
    tot = _sum_devices(flat_g)

    (r_mo,) = _exchange_wait("grad_mlp_out_wait", *x_mo[:4], grad_x)
    g_mo, d_mo, nm_mo, nv_mo = _adamw_slabs("adamw_w_mlp_out", w_mlp_out[0], r_mo, m_w_mlp_out[0], v_w_mlp_out[0])
    (r_mi,) = _exchange_wait("grad_mlp_in_wait", *x_mi[:4], g_mo)
    g_mi, d_mi, nm_mi, nv_mi = _adamw_slabs("adamw_w_mlp_in", w_mlp_in[0], r_mi, m_w_mlp_in[0], v_w_mlp_in[0])
    (r_out,) = _exchange_wait("grad_out_wait", *x_out[:4], g_mi)
    g_out, d_out, nm_out, nv_out = _adamw_slabs("adamw_w_out", w_out[0], r_out, m_w_out[0], v_w_out[0])

    dm_cols = lax.dynamic_slice_in_dim(flat_g[:, 0, :6 * D_MODEL], rank * 1536, 1536, axis=1)
    g_ada, d_ada, nm_ada, nv_ada = _adamw_ada(w_ada[0], c_act.T, dm_cols, m_w_ada[0], v_w_ada[0])
    (r_in,) = _exchange_wait("grad_in_wait", *x_in[:4], g_ada)
    g_in, d_in, nm_in, nv_in = _adamw_slabs("adamw_w_in", w_in[0], r_in, m_w_in[0], v_w_in[0])

    o = 0
    seg = {}
    for name, n in (("b_ada", 12288), ("g_pre_mix", 2048), ("g_post_mix", 2048), ("g_fox_out", 1024),
                    ("g_pre_mlp", 2048), ("g_post_mlp", 2048), ("b_gla_a2", 512), ("w_gla_a2", 8192),
                    ("g_gla_out", 1024), ("b_fgate", 128)):
        seg[name] = tot[:, o:o + n]
        o += n
    g_wa2 = lax.dynamic_slice_in_dim(seg["w_gla_a2"].reshape(GLA_RANK, 512), rank * 64, 64, axis=1)
    g_ggla = lax.dynamic_slice_in_dim(seg["g_gla_out"].reshape(GLA_HEADS, GLA_DV), rank * 32, 32, axis=1)
    small_names = ["b_ada", "g_pre_mix", "g_post_mix", "g_fox_out", "g_pre_mlp", "g_post_mlp", "b_gla_a2",
                   "w_gla_a2", "g_gla_out", "b_fgate"]
    small_grads = {**seg, "w_gla_a2": g_wa2.reshape(1, 1024), "g_gla_out": g_ggla.reshape(1, 128)}
    weights = dict(b_ada=b_ada, g_pre_mix=g_pre_mix, g_post_mix=g_post_mix, g_fox_out=g_fox_out,
                   g_pre_mlp=g_pre_mlp, g_post_mlp=g_post_mlp, b_gla_a2=b_gla_a2, w_gla_a2=w_gla_a2,
                   g_gla_out=g_gla_out, b_fgate=b_fgate)
    moms = dict(b_ada=m_b_ada, g_pre_mix=m_g_pre_mix, g_post_mix=m_g_post_mix, g_fox_out=m_g_fox_out,
                g_pre_mlp=m_g_pre_mlp, g_post_mlp=m_g_post_mlp, b_gla_a2=m_b_gla_a2, w_gla_a2=m_w_gla_a2,
                g_gla_out=m_g_gla_out, b_fgate=m_b_fgate)
    vels = dict(b_ada=v_b_ada, g_pre_mix=v_g_pre_mix, g_post_mix=v_g_post_mix, g_fox_out=v_g_fox_out,
                g_pre_mlp=v_g_pre_mlp, g_post_mlp=v_g_post_mlp, b_gla_a2=v_b_gla_a2, w_gla_a2=v_w_gla_a2,
                g_gla_out=v_g_gla_out, b_fgate=v_b_fgate)

    def flatten(d, fill):
        parts = []
        for nm in small_names:
            p = d[nm].reshape(1, -1)
            if nm == "b_fgate":
                p = jnp.concatenate([p[:, :FOX_HEADS], jnp.full((1, 128 - FOX_HEADS), fill, F32)], axis=1)
            parts.append(p)
        return jnp.concatenate(parts, axis=1).reshape(-1, 128)

    fw, fg, fm, fv = flatten(weights, 0.0), flatten(small_grads, 0.0), flatten(moms, 0.0), flatten(vels, 1.0)
    fd, fnm, fnv = _adamw_flat(fw, fg, fm, fv)

    def unflatten(fl):
        fl = fl.reshape(1, -1)
        out = {}
        o = 0
        for nm in small_names:
            n = 128 if nm == "b_fgate" else weights[nm].size
            piece = fl[:, o:o + n]
            if nm == "b_fgate":
                piece = piece[:, :FOX_HEADS]
            out[nm] = piece.reshape(weights[nm].shape)
            o += n
        return out

    sg, sd, snm, snv = unflatten(fg), unflatten(fd), unflatten(fnm), unflatten(fnv)

    big = dict(w_ada=(g_ada, d_ada, nm_ada, nv_ada), w_in=(g_in, d_in, nm_in, nv_in),
               w_out=(g_out, d_out, nm_out, nv_out), w_mlp_in=(g_mi, d_mi, nm_mi, nv_mi),
               w_mlp_out=(g_mo, d_mo, nm_mo, nv_mo))
    order = ["w_ada", "b_ada", "g_pre_mix", "g_post_mix", "w_in", "b_fgate", "w_gla_a2", "b_gla_a2", "g_fox_out",
             "g_gla_out", "w_out", "g_pre_mlp", "g_post_mlp", "w_mlp_in", "w_mlp_out"]

    def pick(nm, idx):
        if nm in big:
            return big[nm][idx][None]
        return (sg, sd, snm, snv)[idx][nm]

    grads = [pick(nm, 0) for nm in order]
    deltas = [pick(nm, 1) for nm in order]
    new_m = [pick(nm, 2) for nm in order]
    new_v = [pick(nm, 3) for nm in order]
    return (loss, grad_x[None], *grads, *deltas, *new_m, *new_v)
```

```python
import functools

import numpy as np
import jax
import jax.numpy as jnp
from jax import lax
from jax.experimental import pallas as pl
from jax.experimental.pallas import tpu as pltpu

F32 = jnp.float32
BF16 = jnp.bfloat16
MESH = pl.DeviceIdType.MESH
N_DEV = 8

D_MODEL = 2048
FOX_HEADS = 8
FOX_HEAD_DIM = 128
GLA_HEADS = 4
GLA_DK = 128
GLA_DV = 256
GLA_RANK = 16
GLA_TEMP = 16.0
CHUNK = 64
D_FF = 8192
W_MAIN = 6144
W_SMALL = 128
EPS = 1e-6
NEG = float(np.finfo(np.float32).min)

ADAM_LR = 0.001
ADAM_B1 = 0.9
ADAM_B2 = 0.999
ADAM_EPS = 1e-08
ADAM_WD = 0.01
ADAM_STEP = 10

ROW_T = 256
FOX_T = 1024
GLA_R = 512
CUM_T = 256
VMEM_LIMIT = 56 * 1024 * 1024


def _call(body, deps=(), **kw):
    if not deps:
        return pl.pallas_call(body, **kw)
    n_in, n_dep = len(kw["in_specs"]), len(deps)

    def with_deps(*refs):
        return body(*refs[:n_in], *refs[n_in + n_dep:])

    kw["in_specs"] = [*kw["in_specs"], *[pl.BlockSpec(memory_space=pl.ANY)] * n_dep]
    call = pl.pallas_call(with_deps, **kw)
    return lambda *args: call(*args, *deps)


def _params(sem=None):
    return pltpu.CompilerParams(dimension_semantics=sem, vmem_limit_bytes=VMEM_LIMIT)


def _my_pos():
    return lax.axis_index("x"), lax.axis_index("y"), lax.axis_index("c")


def _my_rank():
    x, y, c = _my_pos()
    return 4 * x + 2 * y + c


def _all_gather(name, arrays):
    n = len(arrays)

    def body(*refs):
        ins = refs[:n]
        outs = refs[n:2 * n]
        send_sems, recv_sems, local_sems = refs[2 * n:]
        x, y, c = _my_pos()
        me, sibling = (x, y, c), (x, y, 1 - c)
        chips = [(1 - x, y), (x, 1 - y), (1 - x, 1 - y)]

        def slot(a, px, py, pc):
            return outs[a].at[4 * px + 2 * py + pc]

        def copy(a, k, block, to, src=None):
            return pltpu.make_async_remote_copy(
                src_ref=slot(a, *block) if src is None else src, dst_ref=slot(a, *block),
                send_sem=send_sems.at[a, k], recv_sem=recv_sems.at[a, k],
                device_id=to, device_id_type=MESH)

        started = []
        for a in range(n):
            mine = pltpu.make_async_copy(ins[a], slot(a, *me), local_sems.at[a])
            mine.start()
            started.append(mine)
        first = []
        for a in range(n):
            first.append(copy(a, 0, me, sibling, src=ins[a]))
            first += [copy(a, 1 + j, me, (*chip, c), src=ins[a]) for j, chip in enumerate(chips)]
        for cp in first:
            cp.start()
        passed = []
        for j, chip in enumerate(chips):
            for a in range(n):
                copy(a, 1 + j, (*chip, c), me).wait_recv()
                fwd = copy(a, 4 + j, (*chip, c), sibling)
                fwd.start()
                passed.append(fwd)
        for a in range(n):
            copy(a, 0, sibling, me).wait_recv()
            for j, chip in enumerate(chips):
                copy(a, 4 + j, (*chip, 1 - c), me).wait_recv()
        for cp in first + passed:
            cp.wait_send()
        for mine in started:
            mine.wait()

    hbm = pl.BlockSpec(memory_space=pltpu.HBM)
    return _call(
        body, name=name,
        out_shape=[jax.ShapeDtypeStruct((N_DEV,) + a.shape, a.dtype) for a in arrays],
        in_specs=[hbm] * n, out_specs=[hbm] * n,
        scratch_shapes=[pltpu.SemaphoreType.DMA((n, 7)), pltpu.SemaphoreType.DMA((n, 7)),
                        pltpu.SemaphoreType.DMA((n,))],
    )(*arrays)


def _own_slot(name, src, gather, rank):
    shape = ((N_DEV,) + src.shape) if gather else src.shape
    rr, cc = shape[1], shape[2]
    tr = _tile(rr, 512)

    def body(rank_ref, s_ref, o_ref):
        o_ref[...] = s_ref[...].astype(o_ref.dtype)

    if gather:
        in_spec = pl.BlockSpec((tr, cc), lambda i, rk: (i, 0))
    else:
        in_spec = pl.BlockSpec((None, tr, cc), lambda i, rk: (rk[0], i, 0))
    grid_spec = pltpu.PrefetchScalarGridSpec(
        num_scalar_prefetch=1, grid=(rr // tr,), in_specs=[in_spec],
        out_specs=pl.BlockSpec((None, tr, cc), lambda i, rk: (rk[0], i, 0)))
    return _call(body, name=name, grid_spec=grid_spec, out_shape=jax.ShapeDtypeStruct(shape, BF16),
                 compiler_params=_params(("arbitrary",)))(jnp.reshape(rank, (1,)).astype(jnp.int32), src)


_HBM = pl.BlockSpec(memory_space=pltpu.HBM)
_SEM = pl.BlockSpec(memory_space=pltpu.SEMAPHORE)
_FLIPS = [(kx, ky, kc) for kx in (0, 1) for ky in (0, 1) for kc in (0, 1)][1:]


def _peers():
    x, y, c = _my_pos()
    out = []
    for kx, ky, kc in _FLIPS:
        px, py, pc = (1 - x if kx else x), (1 - y if ky else y), (1 - c if kc else c)
        out.append(((px, py, pc), 4 * px + 2 * py + pc))
    return out


def _exchange_copy(srcs, lands, send_sems, recv_sems, a, k, peer, peer_rank, slot):
    return pltpu.make_async_remote_copy(
        src_ref=lands[a].at[slot] if srcs is None else srcs[a].at[peer_rank],
        dst_ref=lands[a].at[slot],
        send_sem=send_sems[a].at[k], recv_sem=recv_sems[a].at[k],
        device_id=peer, device_id_type=MESH)


def _exchange_start(name, lands, srcs=None, after=()):
    n = len(lands)
    n_src = 0 if srcs is None else n
    n_in = n + n_src + len(after)

    def body(*refs):
        lnd = refs[:n]
        src = None if srcs is None else refs[n:2 * n]
        send_sems, recv_sems = refs[n_in:n_in + n], refs[n_in + n:n_in + 2 * n]
        token = refs[-1]
        me = _my_rank()
        for a in range(n):
            for k, (peer, peer_rank) in enumerate(_peers()):
                _exchange_copy(src, lnd, send_sems, recv_sems, a, k, peer, peer_rank, me).start()
        token[...] = jnp.zeros_like(token)

    sems = [pltpu.SemaphoreType.DMA((7,))] * (2 * n)
    thru = list(lands) + ([] if srcs is None else list(srcs))
    outs = pl.pallas_call(
        body, name=name,
        out_shape=(*sems, *[pltpu.HBM(t.shape, t.dtype) for t in thru], jax.ShapeDtypeStruct((8, 128), F32)),
        in_specs=[*[_HBM] * len(thru), *[pl.BlockSpec(memory_space=pl.ANY)] * len(after)],
        out_specs=(*[_SEM] * (2 * n), *[_HBM] * len(thru), pl.BlockSpec(memory_space=pltpu.VMEM)),
        input_output_aliases={i: 2 * n + i for i in range(len(thru))},
        compiler_params=pltpu.CompilerParams(has_side_effects=pltpu.SideEffectType.DATAFLOW_SIDE_EFFECTING),
    )(*[pltpu.with_memory_space_constraint(t, pltpu.HBM) for t in thru], *after)
    lands_thru = outs[2 * n:3 * n]
    srcs_thru = None if srcs is None else outs[3 * n:4 * n]
    return outs[:n], outs[n:2 * n], srcs_thru, lands_thru, outs[-1]


def _exchange_wait(name, send_sems, recv_sems, srcs, lands, after):
    n = len(lands)
    thru = list(lands) + ([] if srcs is None else list(srcs))

    def body(*refs):
        lnd = refs[:n]
        src = None if srcs is None else refs[n:2 * n]
        ssem, rsem = refs[len(thru):len(thru) + n], refs[len(thru) + n:len(thru) + 2 * n]
        for a in range(n):
            for k, (peer, peer_rank) in enumerate(_peers()):
                cp = _exchange_copy(src, lnd, ssem, rsem, a, k, peer, peer_rank, peer_rank)
                cp.wait_send()
                cp.wait_recv()

    outs = pl.pallas_call(
        body, name=name,
        out_shape=tuple(pltpu.HBM(t.shape, t.dtype) for t in thru),
        in_specs=[*[_HBM] * len(thru), *[_SEM] * (2 * n), pl.BlockSpec(memory_space=pl.ANY)],
        out_specs=tuple([_HBM] * len(thru)),
        input_output_aliases={i: i for i in range(len(thru))},
        compiler_params=pltpu.CompilerParams(has_side_effects=pltpu.SideEffectType.DATAFLOW_SIDE_EFFECTING),
    )(*thru, *send_sems, *recv_sems, after)
    return outs[:n]


NN = ((1,), (0,))
NT = ((1,), (1,))
TN = ((0,), (0,))


def _matmul(name, a, b, *, contract, grid, a_spec, b_spec, out_specs, out_shapes, acc_shape,
            extra=(), extra_specs=(), epilogue=None, deps=()):
    nk = grid[2]
    n_extra = len(extra)
    n_out = len(out_shapes)

    def body(*refs):
        a_ref, b_ref = refs[0], refs[1]
        extra_refs = refs[2:2 + n_extra]
        out_refs = refs[2 + n_extra:2 + n_extra + n_out]
        acc_ref = refs[-1]
        k = pl.program_id(2)

        def prod():
            return lax.dot_general(a_ref[...].astype(BF16), b_ref[...].astype(BF16), (contract, ((), ())),
                                   preferred_element_type=F32)

        def finish(acc):
            res = (acc,) if epilogue is None else epilogue(acc, *[r[...] for r in extra_refs])
            for o_ref, val in zip(out_refs, res):
                o_ref[...] = val.astype(o_ref.dtype)

        if nk == 1:
            finish(prod())
            return

        @pl.when(k == 0)
        def _():
            acc_ref[...] = prod()

        @pl.when((k > 0) & (k < nk - 1))
        def _():
            acc_ref[...] += prod()

        @pl.when(k == nk - 1)
        def _():
            finish(acc_ref[...] + prod())

    outs = _call(
        body, deps=deps, name=name, grid=grid,
        in_specs=[a_spec, b_spec, *extra_specs], out_specs=list(out_specs), out_shape=list(out_shapes),
        scratch_shapes=[pltpu.VMEM(acc_shape if nk > 1 else (8, 128), F32)],
        compiler_params=_params(("parallel", "parallel", "arbitrary")),
    )(a, b, *extra)
    return outs


def _tile(n, t):
    t = min(n, t)
    assert n % t == 0, (n, t)
    return t


def _mm_plain(name, a, b, contract, out_dtype, tm=1024, tn=1024, tk=2048, extra=(), epilogue=None,
              n_out=1, out_dtypes=None, deps=()):
    if contract == NN:
        (m, kd), (_, n) = a.shape, b.shape
    elif contract == NT:
        (m, kd), (n, _) = a.shape, b.shape
    else:
        (kd, m), (_, n) = a.shape, b.shape
    tm, tn, tk = _tile(m, tm), _tile(n, tn), _tile(kd, tk)
    if contract == NN:
        a_spec = pl.BlockSpec((tm, tk), lambda i, j, k: (i, k))
        b_spec = pl.BlockSpec((tk, tn), lambda i, j, k: (k, j))
    elif contract == NT:
        a_spec = pl.BlockSpec((tm, tk), lambda i, j, k: (i, k))
        b_spec = pl.BlockSpec((tn, tk), lambda i, j, k: (j, k))
    else:
        a_spec = pl.BlockSpec((tk, tm), lambda i, j, k: (k, i))
        b_spec = pl.BlockSpec((tk, tn), lambda i, j, k: (k, j))
    o_spec = pl.BlockSpec((tm, tn), lambda i, j, k: (i, j))
    out_dtypes = out_dtypes or [out_dtype] * n_out
    outs = _matmul(
        name, a, b, contract=contract, grid=(m // tm, n // tn, kd // tk), a_spec=a_spec, b_spec=b_spec,
        out_specs=[o_spec] * len(out_dtypes), out_shapes=[jax.ShapeDtypeStruct((m, n), dt) for dt in out_dtypes],
        acc_shape=(tm, tn), extra=extra, extra_specs=[o_spec] * len(extra), epilogue=epilogue, deps=deps)
    return outs[0] if len(out_dtypes) == 1 else outs


def _rows_call(name, body, row_in, vec_in, row_out, vec_out, s, deps=()):
    t = _tile(s, ROW_T)
    in_specs = []
    args = []
    for arr, width, cb in row_in:
        in_specs.append(pl.BlockSpec((t, width), functools.partial(lambda i, cb: (i, cb), cb=cb)))
        args.append(arr)
    for v in vec_in:
        in_specs.append(pl.BlockSpec(v.shape, lambda i: (0, 0)))
        args.append(v)
    out_specs = []
    out_shapes = []
    for width, dt in row_out:
        out_specs.append(pl.BlockSpec((t, width), lambda i: (i, 0)))
        out_shapes.append(jax.ShapeDtypeStruct((s, width), dt))
    for width in vec_out:
        out_specs.append(pl.BlockSpec((1, width), lambda i: (0, 0)))
        out_shapes.append(jax.ShapeDtypeStruct((1, width), F32))
    return _call(body, deps=deps, name=name, grid=(s // t,), in_specs=in_specs, out_specs=out_specs,
                 out_shape=out_shapes, compiler_params=_params(("arbitrary",)))(*args)


def _acc_vec(ref, val):
    _acc_row(ref, jnp.sum(val, axis=0, keepdims=True))


def _acc_row(ref, part):
    @pl.when(pl.program_id(0) == 0)
    def _():
        ref[...] = part

    @pl.when(pl.program_id(0) > 0)
    def _():
        ref[...] += part


def _rms(v):
    return lax.rsqrt(jnp.mean(v * v, axis=-1, keepdims=True) + EPS)


def _norm_bwd(dxn, xn, r):
    return r * (dxn - xn * jnp.mean(dxn * xn, axis=-1, keepdims=True))


def _premix(x, g, scale, shift, deps=()):
    s = x.shape[0]

    def body(x_ref, g_ref, sc_ref, sh_ref, h_ref):
        xv = x_ref[...]
        h_ref[...] = ((xv * _rms(xv) * g_ref[...]) * (1.0 + sc_ref[...]) + sh_ref[...]).astype(BF16)

    return _rows_call("premix", body, [(x, D_MODEL, 0)], [g, scale, shift], [(D_MODEL, BF16)], [], s, deps)[0]


def _sigmoid(z):
    return 1.0 / (1.0 + jnp.exp(-z))


def _mix_fwd(o_fox, o_gla, pm, g_fox, g_gla):
    s = o_fox.shape[0]

    def body(of_ref, og_ref, gr_ref, gf_ref, gg_ref, mix_ref):
        for h in range(FOX_HEADS):
            sl = slice(h * FOX_HEAD_DIM, (h + 1) * FOX_HEAD_DIM)
            seg = of_ref[:, sl]
            mix_ref[:, sl] = (seg * _rms(seg) * gf_ref[:, sl]).astype(BF16)
        for h in range(GLA_HEADS):
            sl = slice(h * GLA_DV, (h + 1) * GLA_DV)
            seg = og_ref[:, sl]
            gr = gr_ref[:, sl].astype(F32)
            val = (seg * _rms(seg) * gg_ref[:, sl]) * (gr * _sigmoid(gr))
            mix_ref[:, pl.ds(FOX_HEADS * FOX_HEAD_DIM + h * GLA_DV, GLA_DV)] = val.astype(BF16)

    return _rows_call("mix_fwd", body, [(o_fox, 1024, 0), (o_gla, 1024, 0), (pm, 1024, 5)], [g_fox, g_gla],
                      [(D_MODEL, BF16)], [], s)[0]


def _mix_bwd(dmix, o_fox, o_gla, pm, g_fox, g_gla, deps=()):
    s = o_fox.shape[0]

    def body(dm_ref, of_ref, og_ref, gr_ref, gf_ref, gg_ref, dof_ref, dog_ref, dgr_ref, dgf_ref, dgg_ref):
        dgf = []
        for h in range(FOX_HEADS):
            sl = slice(h * FOX_HEAD_DIM, (h + 1) * FOX_HEAD_DIM)
            seg = of_ref[:, sl]
            r = _rms(seg)
            segn = seg * r
            dout = dm_ref[:, sl]
            dgf.append(jnp.sum(dout * segn, axis=0, keepdims=True))
            dof_ref[:, sl] = _norm_bwd(dout * gf_ref[:, sl], segn, r).astype(BF16)
        dgg = []
        for h in range(GLA_HEADS):
            sl = slice(h * GLA_DV, (h + 1) * GLA_DV)
            seg = og_ref[:, sl]
            r = _rms(seg)
            segn = seg * r
            gl = segn * gg_ref[:, sl]
            gr = gr_ref[:, sl].astype(F32)
            sig = _sigmoid(gr)
            dout = dm_ref[:, pl.ds(FOX_HEADS * FOX_HEAD_DIM + h * GLA_DV, GLA_DV)]
            dgr_ref[:, sl] = (dout * gl * (sig * (1.0 + gr * (1.0 - sig)))).astype(BF16)
            dgl = dout * (gr * sig)
            dgg.append(jnp.sum(dgl * segn, axis=0, keepdims=True))
            dog_ref[:, sl] = _norm_bwd(dgl * gg_ref[:, sl], segn, r).astype(BF16)
        _acc_row(dgf_ref, jnp.concatenate(dgf, axis=1))
        _acc_row(dgg_ref, jnp.concatenate(dgg, axis=1))

    return _rows_call("mix_bwd", body, [(dmix, D_MODEL, 0), (o_fox, 1024, 0), (o_gla, 1024, 0), (pm, 1024, 5)],
                      [g_fox, g_gla], [(1024, BF16), (1024, BF16), (1024, BF16)], [1024, 1024], s, deps)


def _postmix_premlp(x, y, gate_m, g_post_mix, g_pre_mlp, scale_f, shift_f):
    s = x.shape[0]

    def body(x_ref, y_ref, gm_ref, gpm_ref, gpl_ref, sc_ref, sh_ref, x1_ref, h2_ref):
        yv = y_ref[...]
        x1 = x_ref[...] + gm_ref[...] * (yv * _rms(yv) * gpm_ref[...])
        x1_ref[...] = x1
        h2_ref[...] = ((x1 * _rms(x1) * gpl_ref[...]) * (1.0 + sc_ref[...]) + sh_ref[...]).astype(BF16)

    return _rows_call("postmix_premlp", body, [(x, D_MODEL, 0), (y, D_MODEL, 0)],
                      [gate_m, g_post_mix, g_pre_mlp, scale_f, shift_f], [(D_MODEL, F32), (D_MODEL, BF16)], [], s)


def _loss_postmlp_bwd(x1, y2, target, gate_f, g_post_mlp):
    s = x1.shape[0]

    def body(x1_ref, y2_ref, t_ref, gf_ref, g_ref, dx2_ref, dy2_ref, loss_ref, dgate_ref, dg_ref):
        yv = y2_ref[...]
        r = _rms(yv)
        yn = yv * r
        o = yn * g_ref[...]
        e = (x1_ref[...] + gf_ref[...] * o) - t_ref[...]
        part = 0.5 * jnp.sum(jnp.mean(e * e, axis=-1, keepdims=True), axis=0, keepdims=True)
        _acc_vec(loss_ref, jnp.broadcast_to(part, (1, 128)))
        dx2 = e * (1.0 / D_MODEL)
        dx2_ref[...] = dx2
        _acc_vec(dgate_ref, dx2 * o)
        do = dx2 * gf_ref[...]
        _acc_vec(dg_ref, do * yn)
        dy2_ref[...] = _norm_bwd(do * g_ref[...], yn, r).astype(BF16)

    return _rows_call("loss_postmlp_bwd", body, [(x1, D_MODEL, 0), (y2, D_MODEL, 0), (target, D_MODEL, 0)],
                      [gate_f, g_post_mlp], [(D_MODEL, F32), (D_MODEL, BF16)], [128, D_MODEL, D_MODEL], s)


def _premlp_postmix_bwd(dh2, dx2, x1, y, scale_f, g_pre_mlp, gate_m, g_post_mix, deps=()):
    s = x1.shape[0]

    def body(dh2_ref, dx2_ref, x1_ref, y_ref, sc_ref, gpl_ref, gm_ref, gpm_ref,
             dx1_ref, dy_ref, dsc_ref, dsh_ref, dgpl_ref, dgm_ref, dgpm_ref):
        x1 = x1_ref[...]
        r1 = _rms(x1)
        x1n = x1 * r1
        dh2 = dh2_ref[...]
        _acc_vec(dsc_ref, dh2 * (x1n * gpl_ref[...]))
        _acc_vec(dsh_ref, dh2)
        dn2 = dh2 * (1.0 + sc_ref[...])
        _acc_vec(dgpl_ref, dn2 * x1n)
        dx1 = dx2_ref[...] + _norm_bwd(dn2 * gpl_ref[...], x1n, r1)
        dx1_ref[...] = dx1
        yv = y_ref[...]
        ry = _rms(yv)
        yn = yv * ry
        _acc_vec(dgm_ref, dx1 * (yn * gpm_ref[...]))
        do = dx1 * gm_ref[...]
        _acc_vec(dgpm_ref, do * yn)
        dy_ref[...] = _norm_bwd(do * gpm_ref[...], yn, ry).astype(BF16)

    return _rows_call("premlp_postmix_bwd", body,
                      [(dh2, D_MODEL, 0), (dx2, D_MODEL, 0), (x1, D_MODEL, 0), (y, D_MODEL, 0)],
                      [scale_f, g_pre_mlp, gate_m, g_post_mix], [(D_MODEL, F32), (D_MODEL, BF16)],
                      [D_MODEL] * 5, s, deps)


def _premix_bwd(dh, dx1, x, g_pre_mix, scale_m):
    s = x.shape[0]

    def body(dh_ref, dx1_ref, x_ref, g_ref, sc_ref, gx_ref, dsc_ref, dsh_ref, dg_ref):
        xv = x_ref[...]
        r = _rms(xv)
        xn = xv * r
        dh = dh_ref[...]
        _acc_vec(dsc_ref, dh * (xn * g_ref[...]))
        _acc_vec(dsh_ref, dh)
        dn1 = dh * (1.0 + sc_ref[...])
        _acc_vec(dg_ref, dn1 * xn)
        gx_ref[...] = dx1_ref[...] + _norm_bwd(dn1 * g_ref[...], xn, r)

    return _rows_call("premix_bwd", body, [(dh, D_MODEL, 0), (dx1, D_MODEL, 0), (x, D_MODEL, 0)],
                      [g_pre_mix, scale_m], [(D_MODEL, F32)], [D_MODEL] * 3, s)


def _split3(v):
    hi = v.astype(BF16)
    r1 = v - hi.astype(F32)
    mid = r1.astype(BF16)
    lo = (r1 - mid.astype(F32)).astype(BF16)
    return hi, mid, lo


def _dot_exact01(v, tri, contract=NN, tri_first=False):
    acc = None
    for part in _split3(v):
        lhs, rhs = (tri, part) if tri_first else (part, tri)
        p = lax.dot_general(lhs, rhs, (contract, ((), ())), preferred_element_type=F32)
        acc = p if acc is None else acc + p
    return acc


def _log_sigmoid(z):
    return jnp.minimum(z, 0.0) - jnp.log(1.0 + jnp.exp(-jnp.abs(z)))


def _fox_cum(small, bvec):
    s = small.shape[0]
    t = _tile(s, CUM_T)

    def body(sm_ref, b_ref, out_ref, carry):
        @pl.when(pl.program_id(0) == 0)
        def _():
            carry[...] = jnp.zeros_like(carry)

        lf = _log_sigmoid(sm_ref[...] + b_ref[...])
        lft = lf.T[0:FOX_HEADS, :]
        row = lax.broadcasted_iota(jnp.int32, (t, t), 0)
        col = lax.broadcasted_iota(jnp.int32, (t, t), 1)
        upper = (row <= col).astype(BF16)
        cum = _dot_exact01(lft, upper) + carry[:, 0:1]
        out_ref[...] = cum
        carry[...] = carry[...] + jnp.sum(lft, axis=1, keepdims=True)

    return _call(body, name="fox_cum", grid=(s // t,),
                 in_specs=[pl.BlockSpec((t, W_SMALL), lambda i: (i, 0)), pl.BlockSpec((1, W_SMALL), lambda i: (0, 0))],
                 out_specs=pl.BlockSpec((FOX_HEADS, t), lambda i: (0, i)),
                 out_shape=jax.ShapeDtypeStruct((FOX_HEADS, s), F32),
                 scratch_shapes=[pltpu.VMEM((FOX_HEADS, 128), F32)],
                 compiler_params=_params(("arbitrary",)))(small, bvec)


def _fox_cum_bwd(dc, dcq, small, bvec):
    s = small.shape[0]
    t = _tile(s, CUM_T)
    nb = s // t

    def body(dc_ref, dcq_ref, sm_ref, b_ref, out_ref, db_ref, carry):
        @pl.when(pl.program_id(0) == 0)
        def _():
            carry[...] = jnp.zeros_like(carry)
            db_ref[...] = jnp.zeros_like(db_ref)

        lane = lax.broadcasted_iota(jnp.int32, (t, W_SMALL), 1)
        dcq = jnp.zeros((t, W_SMALL), F32)
        for hh in range(FOX_HEADS):
            dcq = jnp.where(lane == hh, dcq_ref[hh], dcq)
        dcv = dc_ref[...] + dcq.T[0:FOX_HEADS, :]
        row = lax.broadcasted_iota(jnp.int32, (t, t), 0)
        col = lax.broadcasted_iota(jnp.int32, (t, t), 1)
        lower = (row >= col).astype(BF16)
        dlf = _dot_exact01(dcv, lower) + carry[:, 0:1]
        carry[...] = carry[...] + jnp.sum(dcv, axis=1, keepdims=True)
        z = sm_ref[...] + b_ref[...]
        zt = z.T[0:FOX_HEADS, :]
        dff = dlf * _sigmoid(-zt)
        db_ref[...] = db_ref[...] + jnp.sum(dff, axis=1, keepdims=True)
        full = jnp.concatenate([dff, jnp.zeros((W_SMALL - FOX_HEADS, t), F32)], axis=0)
        out_ref[...] = full.T

    return _call(body, name="fox_cum_bwd", grid=(nb,),
                 in_specs=[pl.BlockSpec((FOX_HEADS, t), lambda i: (0, nb - 1 - i)),
                           pl.BlockSpec((FOX_HEADS, t, 1), lambda i: (0, nb - 1 - i, 0)),
                           pl.BlockSpec((t, W_SMALL), lambda i: (nb - 1 - i, 0)),
                           pl.BlockSpec((1, W_SMALL), lambda i: (0, 0))],
                 out_specs=[pl.BlockSpec((t, W_SMALL), lambda i: (nb - 1 - i, 0)),
                            pl.BlockSpec((FOX_HEADS, 128), lambda i: (0, 0))],
                 out_shape=[jax.ShapeDtypeStruct((s, W_SMALL), F32), jax.ShapeDtypeStruct((FOX_HEADS, 128), F32)],
                 scratch_shapes=[pltpu.VMEM((FOX_HEADS, 128), F32)],
                 compiler_params=_params(("arbitrary",)))(dc, dcq, small, bvec)


FOX_SCALE = FOX_HEAD_DIM ** -0.5


def _fox_fwd(pm, crow):
    s = pm.shape[0]
    t = _tile(s, FOX_T)
    nb = s // t
    parts = 2
    hq = t // parts

    def body(q_ref, k_ref, v_ref, c_ref, o_ref, lse_ref):
        i = pl.program_id(1)
        qs = [q_ref[g * hq:(g + 1) * hq, :] for g in range(parts)]

        def block(j, carry, diagonal):
            rows = pl.ds(pl.multiple_of(j * t, t), t)
            k_all, v_all, c_all = k_ref[rows, :], v_ref[rows, :], c_ref[j]
            out = []
            for g, (m_prev, l_prev, acc) in enumerate(carry):
                nk = (g + 1) * hq if diagonal else t
                kb, vb, cb = k_all[:nk], v_all[:nk], c_all[:, :nk]
                sc = lax.dot_general(qs[g], kb, (NT, ((), ())), preferred_element_type=F32)
                sc = sc * FOX_SCALE - cb
                if diagonal:
                    row = lax.broadcasted_iota(jnp.int32, (hq, nk), 0) + g * hq
                    col = lax.broadcasted_iota(jnp.int32, (hq, nk), 1)
                    sc = jnp.where(row >= col, sc, NEG)
                m_new = jnp.maximum(m_prev, jnp.max(sc, axis=1, keepdims=True))
                alpha = jnp.exp(m_prev - m_new)
                p = jnp.exp(sc - m_new)
                l_new = alpha * l_prev + jnp.sum(p, axis=1, keepdims=True)
                p_hi = p.astype(BF16)
                p_lo = (p - p_hi.astype(F32)).astype(BF16)
                pv = jnp.dot(p_hi, vb, preferred_element_type=F32)
                pv = pv + jnp.dot(p_lo, vb, preferred_element_type=F32)
                out.append((m_new, l_new, alpha * acc + pv))
            return tuple(out)

        init = tuple((jnp.full((hq, 1), NEG, F32), jnp.zeros((hq, 1), F32), jnp.zeros((hq, 128), F32))
                     for _ in range(parts))
        carry = lax.fori_loop(0, i, lambda j, cr: block(j, cr, False), init)
        carry = block(i, carry, True)
        for g, (m_fin, l_fin, acc) in enumerate(carry):
            o_ref[g * hq:(g + 1) * hq, :] = acc / l_fin
            lse_ref[g * hq:(g + 1) * hq, :] = m_fin + jnp.log(l_fin)

    return _call(
        body, name="fox_fwd", grid=(FOX_HEADS, nb),
        in_specs=[pl.BlockSpec((t, 128), lambda h, i: (i, h)),
                  pl.BlockSpec((s, 128), lambda h, i: (0, FOX_HEADS + h)),
                  pl.BlockSpec((s, 128), lambda h, i: (0, 2 * FOX_HEADS + h)),
                  pl.BlockSpec((None, nb, 1, t), lambda h, i: (h, 0, 0, 0))],
        out_specs=[pl.BlockSpec((t, 128), lambda h, i: (i, h)),
                   pl.BlockSpec((None, t, 1), lambda h, i: (h, i, 0))],
        out_shape=[jax.ShapeDtypeStruct((s, FOX_HEADS * 128), F32), jax.ShapeDtypeStruct((FOX_HEADS, s, 1), F32)],
        compiler_params=_params(("parallel", "arbitrary")),
    )(pm, pm, pm, crow.reshape(FOX_HEADS, nb, 1, t))


def _fox_bwd(pm, crow, o, lse, do):
    s = pm.shape[0]
    t = _tile(s, FOX_T)
    nb = s // t

    parts = 2
    hq = t // parts

    def body(q_ref, do_ref, o_ref, lse_ref, k_ref, v_ref, c_ref, dq_ref, dk_ref, dv_ref, dc_ref, dcq_ref, delta_s):
        j = pl.program_id(1)

        @pl.when(j == 0)
        def _():
            dq_ref[...] = jnp.zeros_like(dq_ref)
            dcq_ref[...] = jnp.zeros_like(dcq_ref)
            delta_s[...] = jnp.sum(do_ref[...].astype(F32) * o_ref[...], axis=1, keepdims=True)

        k_all, v_all, c_all = k_ref[...], v_ref[...], c_ref[...]

        def grow(acc, part, axis):
            n = part.shape[axis]
            if n == acc.shape[axis]:
                return acc + part
            if axis == 0:
                return jnp.concatenate([acc[:n] + part, acc[n:]], axis=0)
            return jnp.concatenate([acc[:, :n] + part, acc[:, n:]], axis=1)

        def block(i, carry, diagonal):
            dk_acc, dv_acc, dc_acc = carry
            for g in range(parts):
                nk = (g + 1) * hq if diagonal else t
                kb, vb, cb = k_all[:nk], v_all[:nk], c_all[:, :nk]
                rows = pl.ds(pl.multiple_of(i * t + g * hq, hq), hq)
                q, dov = q_ref[rows, :], do_ref[rows, :]
                sc = lax.dot_general(q, kb, (NT, ((), ())), preferred_element_type=F32)
                p = jnp.exp(sc * FOX_SCALE - cb - lse_ref[rows, :])
                if diagonal:
                    row = lax.broadcasted_iota(jnp.int32, (hq, nk), 0) + g * hq
                    col = lax.broadcasted_iota(jnp.int32, (hq, nk), 1)
                    p = jnp.where(row >= col, p, 0.0)
                dp = lax.dot_general(dov, vb, (NT, ((), ())), preferred_element_type=F32)
                ds = p * (dp - delta_s[rows, :])
                dsb = ds.astype(BF16)
                dv_acc = grow(dv_acc, lax.dot_general(p.astype(BF16), dov, (TN, ((), ())),
                                                      preferred_element_type=F32), 0)
                dk_acc = grow(dk_acc, lax.dot_general(dsb, q, (TN, ((), ())), preferred_element_type=F32), 0)
                dq_ref[rows, :] += jnp.dot(dsb, kb, preferred_element_type=F32) * FOX_SCALE
                dc_acc = grow(dc_acc, -jnp.sum(ds, axis=0, keepdims=True), 1)
                dcq_ref[rows, :] += jnp.sum(ds, axis=1, keepdims=True)
            return dk_acc, dv_acc, dc_acc

        carry = (jnp.zeros((t, 128), F32), jnp.zeros((t, 128), F32), jnp.zeros((1, t), F32))
        carry = block(j, carry, True)
        dk_acc, dv_acc, dc_acc = lax.fori_loop(j + 1, nb, lambda i, cr: block(i, cr, False), carry)
        dk_ref[...] = dk_acc * FOX_SCALE
        dv_ref[...] = dv_acc
        dc_ref[...] = dc_acc

    whole = lambda h, j: (0, h)
    return _call(
        body, name="fox_bwd", grid=(FOX_HEADS, nb),
        in_specs=[pl.BlockSpec((s, 128), whole), pl.BlockSpec((s, 128), whole), pl.BlockSpec((s, 128), whole),
                  pl.BlockSpec((None, s, 1), lambda h, j: (h, 0, 0)),
                  pl.BlockSpec((t, 128), lambda h, j: (j, FOX_HEADS + h)),
                  pl.BlockSpec((t, 128), lambda h, j: (j, 2 * FOX_HEADS + h)),
                  pl.BlockSpec((None, 1, t), lambda h, j: (h, 0, j))],
        out_specs=[pl.BlockSpec((s, 128), whole),
                   pl.BlockSpec((t, 128), lambda h, j: (j, h)),
                   pl.BlockSpec((t, 128), lambda h, j: (j, h)),
                   pl.BlockSpec((None, 1, t), lambda h, j: (h, 0, j)),
                   pl.BlockSpec((None, s, 1), lambda h, j: (h, 0, 0))],
        out_shape=[jax.ShapeDtypeStruct((s, 1024), F32), jax.ShapeDtypeStruct((s, 1024), F32),
                   jax.ShapeDtypeStruct((s, 1024), F32), jax.ShapeDtypeStruct((FOX_HEADS, 1, s), F32),
                   jax.ShapeDtypeStruct((FOX_HEADS, s, 1), F32)],
        scratch_shapes=[pltpu.VMEM((s, 1), F32)],
        compiler_params=_params(("parallel", "arbitrary")),
    )(pm, do, o, lse, pm, pm, crow)


GLA_SCALE = GLA_DK ** -0.5
GLA_Q_BLK = 3072 // 128
GLA_K_BLK = 3584 // 128
GLA_V_BLK = 4096 // 256


def _gla_gate(sm, wa_ref, b_ref):
    return jnp.dot(sm.astype(BF16), wa_ref[...], preferred_element_type=F32) + b_ref[...]


def _tri(n, strict):
    row = lax.broadcasted_iota(jnp.int32, (n, n), 0)
    col = lax.broadcasted_iota(jnp.int32, (n, n), 1)
    return ((row > col) if strict else (row >= col)).astype(BF16)


def _gla_fwd(pm, small, wa_pad, b_a2):
    s = pm.shape[0]
    r = _tile(s, GLA_R)
    nc = r // CHUNK

    def body(q_ref, k_ref, v_ref, sm_ref, wa_ref, b_ref, o_ref, st_ref, state):
        @pl.when(pl.program_id(1) == 0)
        def _():
            state[...] = jnp.zeros_like(state)

        tri = _tri(CHUNK, False)
        for c in range(nc):
            rows = slice(c * CHUNK, (c + 1) * CHUNK)
            la = _log_sigmoid(_gla_gate(sm_ref[rows, :], wa_ref, b_ref)) * (1.0 / GLA_TEMP)
            cum = _dot_exact01(la, tri, tri_first=True)
            total = jnp.sum(la, axis=0, keepdims=True)
            kdec = k_ref[rows, :].astype(F32) * jnp.exp(total - cum)
            ut = lax.dot_general(v_ref[rows, :], kdec.astype(BF16), (TN, ((), ())), preferred_element_type=F32)
            new = state[...] * jnp.exp(total) + ut
            state[...] = new
            newb = new.astype(BF16)
            st_ref[c] = newb
            qs = (q_ref[rows, :].astype(F32) * GLA_SCALE).astype(BF16)
            o_ref[rows, :] = lax.dot_general(qs, newb, (NT, ((), ())), preferred_element_type=F32)

    return _call(
        body, name="gla_fwd", grid=(GLA_HEADS, s // r),
        in_specs=[pl.BlockSpec((r, 128), lambda h, i: (i, GLA_Q_BLK + h)),
                  pl.BlockSpec((r, 128), lambda h, i: (i, GLA_K_BLK + h)),
                  pl.BlockSpec((r, 256), lambda h, i: (i, GLA_V_BLK + h)),
                  pl.BlockSpec((r, W_SMALL), lambda h, i: (i, 0)),
                  pl.BlockSpec((W_SMALL, 128), lambda h, i: (0, h)),
                  pl.BlockSpec((1, 128), lambda h, i: (0, h))],
        out_specs=[pl.BlockSpec((r, 256), lambda h, i: (i, h)),
                   pl.BlockSpec((nc, None, GLA_DV, GLA_DK), lambda h, i: (i, h, 0, 0))],
        out_shape=[jax.ShapeDtypeStruct((s, 1024), F32),
                   jax.ShapeDtypeStruct((s // CHUNK, GLA_HEADS, GLA_DV, GLA_DK), BF16)],
        scratch_shapes=[pltpu.VMEM((GLA_DV, GLA_DK), F32)],
        compiler_params=_params(("parallel", "arbitrary")),
    )(pm, pm, pm, small, wa_pad, b_a2)


def _gla_bwd(pm, small, wa_pad, b_a2, states, do):
    s = pm.shape[0]
    r = _tile(s, GLA_R)
    nc = r // CHUNK
    nb = s // r

    def body(q_ref, k_ref, v_ref, sm_ref, wa_ref, b_ref, do_ref, st_ref, prev_ref,
             dq_ref, dk_ref, dv_ref, dza_ref, db_ref, carry):
        step = pl.program_id(1)

        @pl.when(step == 0)
        def _():
            carry[...] = jnp.zeros_like(carry)
            db_ref[...] = jnp.zeros_like(db_ref)

        tri = _tri(CHUNK, False)
        tri_strict = _tri(CHUNK, True)
        db = jnp.zeros((1, 128), F32)
        for c in reversed(range(nc)):
            rows = slice(c * CHUNK, (c + 1) * CHUNK)
            z = _gla_gate(sm_ref[rows, :], wa_ref, b_ref)
            la = _log_sigmoid(z) * (1.0 / GLA_TEMP)
            cum = _dot_exact01(la, tri, tri_first=True)
            total = jnp.sum(la, axis=0, keepdims=True)
            w = jnp.exp(total - cum)
            decay = jnp.exp(total)
            kdec = k_ref[rows, :].astype(F32) * w
            dov = do_ref[rows, :]
            qs = (q_ref[rows, :].astype(F32) * GLA_SCALE).astype(BF16)
            dq_ref[rows, :] = jnp.dot(dov, st_ref[c], preferred_element_type=F32) * GLA_SCALE
            gt = lax.dot_general(dov, qs, (TN, ((), ())), preferred_element_type=F32) + carry[...]
            gtb = gt.astype(BF16)
            dv_ref[rows, :] = lax.dot_general(kdec.astype(BF16), gtb, (NT, ((), ())), preferred_element_type=F32)
            dkdec = jnp.dot(v_ref[rows, :], gtb, preferred_element_type=F32)
            dk_ref[rows, :] = dkdec * w
            e = dkdec * kdec
            if c > 0:
                prev = st_ref[c - 1].astype(F32)
            else:
                prev = jnp.where(step == nb - 1, 0.0, prev_ref[0].astype(F32))
            dtot = jnp.sum(gt * prev, axis=0, keepdims=True) * decay
            dla = dtot + _dot_exact01(e, tri_strict, tri_first=True)
            dza = dla * (1.0 / GLA_TEMP) * _sigmoid(-z)
            dza_ref[rows, :] = dza.astype(BF16)
            db = db + jnp.sum(dza, axis=0, keepdims=True)
            carry[...] = gt * decay
        db_ref[...] += db

    blk = lambda h, i: nb - 1 - i
    return _call(
        body, name="gla_bwd", grid=(GLA_HEADS, nb),
        in_specs=[pl.BlockSpec((r, 128), lambda h, i: (blk(h, i), GLA_Q_BLK + h)),
                  pl.BlockSpec((r, 128), lambda h, i: (blk(h, i), GLA_K_BLK + h)),
                  pl.BlockSpec((r, 256), lambda h, i: (blk(h, i), GLA_V_BLK + h)),
                  pl.BlockSpec((r, W_SMALL), lambda h, i: (blk(h, i), 0)),
                  pl.BlockSpec((W_SMALL, 128), lambda h, i: (0, h)),
                  pl.BlockSpec((1, 128), lambda h, i: (0, h)),
                  pl.BlockSpec((r, 256), lambda h, i: (blk(h, i), h)),
                  pl.BlockSpec((nc, None, GLA_DV, GLA_DK), lambda h, i: (blk(h, i), h, 0, 0)),
                  pl.BlockSpec((1, None, GLA_DV, GLA_DK),
                               lambda h, i: (jnp.maximum(blk(h, i) * nc - 1, 0), h, 0, 0))],
        out_specs=[pl.BlockSpec((r, 128), lambda h, i: (blk(h, i), h)),
                   pl.BlockSpec((r, 128), lambda h, i: (blk(h, i), h)),
                   pl.BlockSpec((r, 256), lambda h, i: (blk(h, i), h)),
                   pl.BlockSpec((r, 128), lambda h, i: (blk(h, i), h)),
                   pl.BlockSpec((1, 128), lambda h, i: (0, h))],
        out_shape=[jax.ShapeDtypeStruct((s, 512), F32), jax.ShapeDtypeStruct((s, 512), F32),
                   jax.ShapeDtypeStruct((s, 1024), F32), jax.ShapeDtypeStruct((s, 512), BF16),
                   jax.ShapeDtypeStruct((1, 512), F32)],
        scratch_shapes=[pltpu.VMEM((GLA_DV, GLA_DK), F32)],
        compiler_params=_params(("parallel", "arbitrary")),
    )(pm, pm, pm, small, wa_pad, b_a2, do, states, states)


def _modulation(c_all, w_ada):
    n = w_ada.shape[1]
    tn = _tile(n, 512)

    def body(c_ref, w_ref, out_ref, ca_ref):
        cv = c_ref[...]
        ca = cv * _sigmoid(cv)
        ca_ref[...] = ca
        out_ref[...] = jnp.dot(ca.astype(BF16), w_ref[...].astype(BF16), preferred_element_type=F32)

    return _call(body, name="modulation", grid=(n // tn,),
                 in_specs=[pl.BlockSpec((N_DEV, D_MODEL), lambda j: (0, 0)),
                           pl.BlockSpec((D_MODEL, tn), lambda j: (0, j))],
                 out_specs=[pl.BlockSpec((N_DEV, tn), lambda j: (0, j)),
                            pl.BlockSpec((N_DEV, D_MODEL), lambda j: (0, 0))],
                 out_shape=[jax.ShapeDtypeStruct((N_DEV, n), F32), jax.ShapeDtypeStruct((N_DEV, D_MODEL), F32)],
                 compiler_params=_params(("arbitrary",)))(c_all, w_ada)


def _adamw_math(w, g, m, v):
    m = ADAM_B1 * m + (1.0 - ADAM_B1) * g
    v = ADAM_B2 * v + (1.0 - ADAM_B2) * (g * g)
    m_hat = m / (1.0 - ADAM_B1 ** ADAM_STEP)
    v_hat = v / (1.0 - ADAM_B2 ** ADAM_STEP)
    delta = -ADAM_LR * (m_hat / (jnp.sqrt(v_hat) + ADAM_EPS) + ADAM_WD * w)
    return delta, m, v


def _adamw_slabs(name, w, slabs, m, v, tr=256):
    rr, cc = w.shape
    tr = _tile(rr, tr)

    def body(w_ref, s_ref, m_ref, v_ref, g_ref, d_ref, nm_ref, nv_ref):
        g = s_ref[0].astype(F32)
        for r in range(1, N_DEV):
            g = g + s_ref[r].astype(F32)
        g_ref[...] = g
        d, nm, nv = _adamw_math(w_ref[...], g, m_ref[...], v_ref[...])
        d_ref[...] = d
        nm_ref[...] = nm
        nv_ref[...] = nv

    spec = pl.BlockSpec((tr, cc), lambda i: (i, 0))
    return _call(body, name=name, grid=(rr // tr,),
                 in_specs=[spec, pl.BlockSpec((N_DEV, tr, cc), lambda i: (0, i, 0)), spec, spec],
                 out_specs=[spec] * 4, out_shape=[jax.ShapeDtypeStruct((rr, cc), F32)] * 4,
                 compiler_params=_params(("parallel",)))(w, slabs, m, v)


def _adamw_ada(w, cat, dm, m, v, tr=256):
    rr, cc = w.shape
    tr = _tile(rr, tr)

    def body(w_ref, ca_ref, dm_ref, m_ref, v_ref, g_ref, d_ref, nm_ref, nv_ref):
        g = ca_ref[:, 0:1] * dm_ref[0:1, :]
        for b in range(1, N_DEV):
            g = g + ca_ref[:, b:b + 1] * dm_ref[b:b + 1, :]
        g_ref[...] = g
        d, nm, nv = _adamw_math(w_ref[...], g, m_ref[...], v_ref[...])
        d_ref[...] = d
        nm_ref[...] = nm
        nv_ref[...] = nv

    spec = pl.BlockSpec((tr, cc), lambda i: (i, 0))
    return _call(body, name="adamw_ada", grid=(rr // tr,),
                 in_specs=[spec, pl.BlockSpec((tr, N_DEV), lambda i: (i, 0)),
                           pl.BlockSpec((N_DEV, cc), lambda i: (0, 0)), spec, spec],
                 out_specs=[spec] * 4, out_shape=[jax.ShapeDtypeStruct((rr, cc), F32)] * 4,
                 compiler_params=_params(("parallel",)))(w, cat, dm, m, v)


def _sum_devices(gathered):
    ln = gathered.shape[-1]

    def body(g_ref, out_ref):
        acc = g_ref[0]
        for r in range(1, N_DEV):
            acc = acc + g_ref[r]
        out_ref[...] = acc

    return _call(body, name="sum_devices",
                 in_specs=[pl.BlockSpec(memory_space=pltpu.VMEM)], out_specs=pl.BlockSpec(memory_space=pltpu.VMEM),
                 out_shape=jax.ShapeDtypeStruct((1, ln), F32))(gathered)


def _adamw_flat(w, g, m, v):
    def body(w_ref, g_ref, m_ref, v_ref, d_ref, nm_ref, nv_ref):
        d, nm, nv = _adamw_math(w_ref[...], g_ref[...], m_ref[...], v_ref[...])
        d_ref[...] = d
        nm_ref[...] = nm
        nv_ref[...] = nv

    vm = pl.BlockSpec(memory_space=pltpu.VMEM)
    return _call(body, name="adamw_small", in_specs=[vm] * 4, out_specs=[vm] * 3,
                 out_shape=[jax.ShapeDtypeStruct(w.shape, F32)] * 3)(w, g, m, v)


def _from_col_shards(g):
    return jnp.transpose(g, (1, 0, 2)).reshape(g.shape[1], N_DEV * g.shape[2])


def _pad_lanes(v, n):
    return jnp.concatenate([v, jnp.zeros(v.shape[:-1] + (n - v.shape[-1],), v.dtype)], axis=-1)


def kernel(x, c, w_ada, b_ada, g_pre_mix, g_post_mix, w_in, b_fgate, w_gla_a2, b_gla_a2, g_fox_out, g_gla_out, w_out, g_pre_mlp, g_post_mlp, w_mlp_in, w_mlp_out, loss_target, m_w_ada, m_b_ada, m_g_pre_mix, m_g_post_mix, m_w_in, m_b_fgate, m_w_gla_a2, m_b_gla_a2, m_g_fox_out, m_g_gla_out, m_w_out, m_g_pre_mlp, m_g_post_mlp, m_w_mlp_in, m_w_mlp_out, v_w_ada, v_b_ada, v_g_pre_mix, v_g_post_mix, v_w_in, v_b_fgate, v_w_gla_a2, v_b_gla_a2, v_g_fox_out, v_g_gla_out, v_w_out, v_g_pre_mlp, v_g_post_mlp, v_w_mlp_in, v_w_mlp_out):
    rank = _my_rank()
    xs = x[0]
    s = xs.shape[0]
    target = loss_target[0]

    c_all, wa2_g, ggla_g, win_g = _all_gather("gather_first", [c, w_gla_a2[0], g_gla_out[0], w_in[0].astype(BF16)])
    rest = [_own_slot("own_w_out", w_out[0], True, rank), _own_slot("own_w_mlp_in", w_mlp_in[0], True, rank)]
    gs_send, gs_recv, _, gs_land, gs_token = _exchange_start("gather_rest_start", rest, after=(c_all,))
    last = [_own_slot("own_w_mlp_out", w_mlp_out[0], True, rank)]
    gl_send, gl_recv, _, gl_land, gl_token = _exchange_start("gather_last_start", last, after=(gs_token,))
    w_a2 = _from_col_shards(wa2_g)
    g_gla = _from_col_shards(ggla_g).reshape(1, 1024)
    g_fox = g_fox_out.reshape(1, 1024)
    win_full = _from_col_shards(win_g)
    w_main = jnp.concatenate([win_full[:, :3072], win_full[:, 3080:5128], win_full[:, 5144:6168]], axis=1)
    w_small = _pad_lanes(jnp.concatenate([win_full[:, 3072:3080], win_full[:, 5128:5144]], axis=1), W_SMALL)
    wa_pad =jnp.concatenate([jnp.zeros((8, 512), BF16), w_a2.astype(BF16), jnp.zeros((104, 512), BF16)], axis=0)
    bf_vec = _pad_lanes(b_fgate, W_SMALL)

    mod_part, c_act = _modulation(c_all.reshape(N_DEV, D_MODEL), w_ada[0])
    (mod_g,) = _all_gather("gather_mod", [mod_part])
    mod = lax.dynamic_slice_in_dim(mod_g, rank, 1, axis=1).reshape(1, 6 * D_MODEL) + b_ada
    shift_m, scale_m, gate_m, shift_f, scale_f, gate_f = [mod[:, i * D_MODEL:(i + 1) * D_MODEL] for i in range(6)]

    h = _premix(xs, g_pre_mix, scale_m, shift_m, deps=(gl_token,))
    pm = _mm_plain("proj_main", h, w_main, NN, BF16)
    small = _mm_plain("proj_small", h, w_small, NN, F32)
    crow = _fox_cum(small, bf_vec).reshape(FOX_HEADS, 1, s)
    o_fox, lse = _fox_fwd(pm, crow)
    o_gla, states = _gla_fwd(pm, small, wa_pad, b_gla_a2)
    mix = _mix_fwd(o_fox, o_gla, pm, g_fox, g_gla)
    wout_g, wmi_g = _exchange_wait("gather_rest_wait", gs_send, gs_recv, None, gs_land, mix)
    w_out_full = wout_g.reshape(D_MODEL, D_MODEL)
    y =_mm_plain("out_proj", mix, w_out_full, NN, F32)
    x1, h2 = _postmix_premlp(xs, y, gate_m, g_post_mix, g_pre_mlp, scale_f, shift_f)

    tm, tn, tk = _tile(s, 1024), 1024, 2048
    nsh = 1024 // tn

    def relu2(acc):
        rl = jnp.maximum(acc, 0.0)
        return rl * rl, rl

    z, a_relu = _matmul(
        "mlp_in", h2, wmi_g, contract=NN, grid=(s // tm, D_FF // tn, D_MODEL // tk),
        a_spec=pl.BlockSpec((tm, tk), lambda i, j, k: (i, k)),
        b_spec=pl.BlockSpec((None, tk, tn), lambda i, j, k: (j // nsh, k, j % nsh)),
        out_specs=[pl.BlockSpec((tm, tn), lambda i, j, k: (i, j))] * 2,
        out_shapes=[jax.ShapeDtypeStruct((s, D_FF), BF16)] * 2, acc_shape=(tm, tn), epilogue=relu2)
    (wmo_g,) = _exchange_wait("gather_last_wait", gl_send, gl_recv, None, gl_land, z)
    w_mo_full = wmo_g.reshape(D_FF, D_MODEL)
    y2 = _mm_plain("mlp_out", z, w_mo_full, NN, F32)

    dx2, dy2, loss_vec, dgate_f, dg_post_mlp = _loss_postmlp_bwd(x1, y2, target, gate_f, g_post_mlp)
    loss = lax.psum(loss_vec[0, 0], ("x", "y", "c"))

    da = _mm_plain("mlp_out_dx", dy2, w_mo_full, NT, BF16, extra=(a_relu,),
                   epilogue=lambda acc, rl: (acc * (2.0 * rl.astype(F32)),))
    dw_mo = _mm_plain("mlp_out_dw", z, dy2, TN, BF16)
    dw_mo = dw_mo.reshape(N_DEV, 1024, D_MODEL)
    x_mo = _exchange_start("grad_mlp_out_start", [_own_slot("own_dw_mlp_out", dw_mo, False, rank)], [dw_mo])
    tkx = 1024
    (dh2,) = _matmul(
        "mlp_in_dx", da, wmi_g, contract=NT, grid=(s // tm, D_MODEL // tn, D_FF // tkx),
        a_spec=pl.BlockSpec((tm, tkx), lambda i, j, k: (i, k)),
        b_spec=pl.BlockSpec((None, tn, tkx), lambda i, j, k: (k, j, 0)),
        out_specs=[pl.BlockSpec((tm, tn), lambda i, j, k: (i, j))],
        out_shapes=[jax.ShapeDtypeStruct((s, D_MODEL), F32)], acc_shape=(tm, tn), deps=(x_mo[4],))
    ts = _tile(s, 2048)
    (dw_mi,) = _matmul(
        "mlp_in_dw", h2, da, contract=TN, grid=(D_MODEL // 1024, D_FF // tn, s // ts),
        a_spec=pl.BlockSpec((ts, 1024), lambda i, j, k: (k, i)),
        b_spec=pl.BlockSpec((ts, tn), lambda i, j, k: (k, j)),
        out_specs=[pl.BlockSpec((None, 1024, tn), lambda i, j, k: (j // nsh, i, j % nsh))],
        out_shapes=[jax.ShapeDtypeStruct((N_DEV, D_MODEL, 1024), BF16)], acc_shape=(1024, tn))
    x_mi = _exchange_start("grad_mlp_in_start", [_own_slot("own_dw_mlp_in", dw_mi, False, rank)], [dw_mi])

    dx1, dy, dscale_f, dshift_f, dg_pre_mlp, dgate_m, dg_post_mix = _premlp_postmix_bwd(
        dh2, dx2, x1, y, scale_f, g_pre_mlp, gate_m, g_post_mix, deps=(x_mi[4],))

    dmix = _mm_plain("out_proj_dx", dy, w_out_full, NT, F32)
    dw_out = _mm_plain("out_proj_dw", mix, dy, TN, BF16)
    dw_out = dw_out.reshape(N_DEV, 256, D_MODEL)
    x_out = _exchange_start("grad_out_start", [_own_slot("own_dw_out", dw_out, False, rank)], [dw_out])
    do_fox, do_gla, dgr, dg_fox, dg_gla = _mix_bwd(dmix, o_fox, o_gla, pm, g_fox, g_gla, deps=(x_out[4],))

    dq, dk, dv, dc, dcq = _fox_bwd(pm, crow, o_fox, lse, do_fox)
    dsmall_f, db_f = _fox_cum_bwd(dc.reshape(FOX_HEADS, s), dcq, small, bf_vec)
    dgq, dgk, dgv, dza, db_a2 = _gla_bwd(pm, small, wa_pad, b_gla_a2, states, do_gla)
    dsmall = _mm_plain("gate_dx", dza, wa_pad, NT, F32, tn=128, extra=(dsmall_f,),
                       epilogue=lambda acc, other: (acc + other,))
    dwa_pad = _mm_plain("gate_dw", small, dza, TN, F32, tm=128, tn=512)

    dpm = jnp.concatenate([dq.astype(BF16), dk.astype(BF16), dv.astype(BF16), dgq.astype(BF16), dgk.astype(BF16),
                           dgv.astype(BF16), dgr], axis=1)
    dw_main = _mm_plain("proj_main_dw", h, dpm, TN, BF16)
    dw_small = _mm_plain("proj_small_dw", h, dsmall, TN, BF16, tn=128)
    dwin_full = jnp.concatenate([dw_main[:, :3072], dw_small[:, 0:8], dw_main[:, 3072:5120], dw_small[:, 8:24],
                                 dw_main[:, 5120:6144]], axis=1)
    dwin_slabs = jnp.transpose(dwin_full.reshape(D_MODEL, N_DEV, 771), (1, 0, 2))
    x_in = _exchange_start("grad_in_start", [_own_slot("own_dw_in", dwin_slabs, False, rank)], [dwin_slabs])
    dh_small = _mm_plain("proj_small_dx", dsmall, w_small, NT, F32, tk=128)
    dh = _mm_plain("proj_main_dx", dpm, w_main, NT, F32, extra=(dh_small,),
                   epilogue=lambda acc, other: (acc + other,), deps=(x_in[4],))
    grad_x, dscale_m, dshift_m, dg_pre_mix = _premix_bwd(dh, dx1, xs, g_pre_mix, scale_m)

    dmod = jnp.concatenate([dshift_m, dscale_m, dgate_m, dshift_f, dscale_f, dgate_f], axis=1)
    flat = jnp.concatenate(
        [dmod, dg_pre_mix, dg_post_mix, dg_fox, dg_pre_mlp, dg_post_mlp, db_a2,
         dwa_pad[8:24, :].reshape(1, GLA_RANK * 512), dg_gla, _pad_lanes(db_f[:, 0].reshape(1, FOX_HEADS), 128)],
        axis=1)
    (flat_g,) = _all_gather("gather_small_grads", [flat])
    tot = _sum_devices(flat_g)

    (r_mo,) = _exchange_wait("grad_mlp_out_wait", *x_mo[:4], grad_x)
    g_mo, d_mo, nm_mo, nv_mo = _adamw_slabs("adamw_w_mlp_out", w_mlp_out[0], r_mo, m_w_mlp_out[0], v_w_mlp_out[0])
    (r_mi,) = _exchange_wait("grad_mlp_in_wait", *x_mi[:4], g_mo)
    g_mi, d_mi, nm_mi, nv_mi = _adamw_slabs("adamw_w_mlp_in", w_mlp_in[0], r_mi, m_w_mlp_in[0], v_w_mlp_in[0])
    (r_out,) = _exchange_wait("grad_out_wait", *x_out[:4], g_mi)
    g_out, d_out, nm_out, nv_out = _adamw_slabs("adamw_w_out", w_out[0], r_out, m_w_out[0], v_w_out[0])

    dm_cols = lax.dynamic_slice_in_dim(flat_g[:, 0, :6 * D_MODEL], rank * 1536, 1536, axis=1)
    g_ada, d_ada, nm_ada, nv_ada = _adamw_ada(w_ada[0], c_act.T, dm_cols, m_w_ada[0], v_w_ada[0])
    (r_in,) = _exchange_wait("grad_in_wait", *x_in[:4], g_ada)
    g_in, d_in, nm_in, nv_in = _adamw_slabs("adamw_w_in", w_in[0], r_in, m_w_in[0], v_w_in[0])

    o = 0
    seg = {}
    for name, n in (("b_ada", 12288), ("g_pre_mix", 2048), ("g_post_mix", 2048), ("g_fox_out", 1024),
                    ("g_pre_mlp", 2048), ("g_post_mlp", 2048), ("b_gla_a2", 512), ("w_gla_a2", 8192),
                    ("g_gla_out", 1024), ("b_fgate", 128)):
        seg[name] = tot[:, o:o + n]
        o += n
    g_wa2 = lax.dynamic_slice_in_dim(seg["w_gla_a2"].reshape(GLA_RANK, 512), rank * 64, 64, axis=1)
    g_ggla = lax.dynamic_slice_in_dim(seg["g_gla_out"].reshape(GLA_HEADS, GLA_DV), rank * 32, 32, axis=1)
    small_names = ["b_ada", "g_pre_mix", "g_post_mix", "g_fox_out", "g_pre_mlp", "g_post_mlp", "b_gla_a2",
                   "w_gla_a2", "g_gla_out", "b_fgate"]
    small_grads = {**seg, "w_gla_a2": g_wa2.reshape(1, 1024), "g_gla_out": g_ggla.reshape(1, 128)}
    weights = dict(b_ada=b_ada, g_pre_mix=g_pre_mix, g_post_mix=g_post_mix, g_fox_out=g_fox_out,
                   g_pre_mlp=g_pre_mlp, g_post_mlp=g_post_mlp, b_gla_a2=b_gla_a2, w_gla_a2=w_gla_a2,
                   g_gla_out=g_gla_out, b_fgate=b_fgate)
    moms = dict(b_ada=m_b_ada, g_pre_mix=m_g_pre_mix, g_post_mix=m_g_post_mix, g_fox_out=m_g_fox_out,
                g_pre_mlp=m_g_pre_mlp, g_post_mlp=m_g_post_mlp, b_gla_a2=m_b_gla_a2, w_gla_a2=m_w_gla_a2,
                g_gla_out=m_g_gla_out, b_fgate=m_b_fgate)
    vels = dict(b_ada=v_b_ada, g_pre_mix=v_g_pre_mix, g_post_mix=v_g_post_mix, g_fox_out=v_g_fox_out,
                g_pre_mlp=v_g_pre_mlp, g_post_mlp=v_g_post_mlp, b_gla_a2=v_b_gla_a2, w_gla_a2=v_w_gla_a2,
                g_gla_out=v_g_gla_out, b_fgate=v_b_fgate)

    def flatten(d, fill):
        parts = []
        for nm in small_names:
            p = d[nm].reshape(1, -1)
            if nm == "b_fgate":
                p = jnp.concatenate([p[:, :FOX_HEADS], jnp.full((1, 128 - FOX_HEADS), fill, F32)], axis=1)
            parts.append(p)
        return jnp.concatenate(parts, axis=1).reshape(-1, 128)

    fw, fg, fm, fv = flatten(weights, 0.0), flatten(small_grads, 0.0), flatten(moms, 0.0), flatten(vels, 1.0)
    fd, fnm, fnv = _adamw_flat(fw, fg, fm, fv)

    def unflatten(fl):
        fl = fl.reshape(1, -1)
        out = {}
        o = 0
        for nm in small_names:
            n = 128 if nm == "b_fgate" else weights[nm].size
            piece = fl[:, o:o + n]
            if nm == "b_fgate":
                piece = piece[:, :FOX_HEADS]
            out[nm] = piece.reshape(weights[nm].shape)
            o += n
        return out

    sg, sd, snm, snv = unflatten(fg), unflatten(fd), unflatten(fnm), unflatten(fnv)

    big = dict(w_ada=(g_ada, d_ada, nm_ada, nv_ada), w_in=(g_in, d_in, nm_in, nv_in),
               w_out=(g_out, d_out, nm_out, nv_out), w_mlp_in=(g_mi, d_mi, nm_mi, nv_mi),
               w_mlp_out=(g_mo, d_mo, nm_mo, nv_mo))
    order = ["w_ada", "b_ada", "g_pre_mix", "g_post_mix", "w_in", "b_fgate", "w_gla_a2", "b_gla_a2", "g_fox_out",
             "g_gla_out", "w_out", "g_pre_mlp", "g_post_mlp", "w_mlp_in", "w_mlp_out"]

    def pick(nm, idx):
        if nm in big:
            return big[nm][idx][None]
        return (sg, sd, snm, snv)[idx][nm]

    grads = [pick(nm, 0) for nm in order]
    deltas = [pick(nm, 1) for nm in order]
    new_m = [pick(nm, 2) for nm in order]
    new_v = [pick(nm, 3) for nm in order]
    return (loss, grad_x[None], *grads, *deltas, *new_m, *new_v)
```

```python
import functools

import numpy as np
import jax
import jax.numpy as jnp
from jax import lax
from jax.experimental import pallas as pl
from jax.experimental.pallas import tpu as pltpu

F32 = jnp.float32
BF16 = jnp.bfloat16
MESH = pl.DeviceIdType.MESH
N_DEV = 8

D_MODEL = 2048
FOX_HEADS = 8
FOX_HEAD_DIM = 128
GLA_HEADS = 4
GLA_DK = 128
GLA_DV = 256
GLA_RANK = 16
GLA_TEMP = 16.0
CHUNK = 64
D_FF = 8192
W_MAIN = 6144
W_SMALL = 128
EPS = 1e-6
NEG = float(np.finfo(np.float32).min)

ADAM_LR = 0.001
ADAM_B1 = 0.9
ADAM_B2 = 0.999
ADAM_EPS = 1e-08
ADAM_WD = 0.01
ADAM_STEP = 10

ROW_T = 256
FOX_T = 1024
GLA_R = 512
CUM_T = 256
VMEM_LIMIT = 56 * 1024 * 1024


def _call(body, deps=(), **kw):
    if not deps:
        return pl.pallas_call(body, **kw)
    n_in, n_dep = len(kw["in_specs"]), len(deps)

    def with_deps(*refs):
        return body(*refs[:n_in], *refs[n_in + n_dep:])

    kw["in_specs"] = [*kw["in_specs"], *[pl.BlockSpec(memory_space=pl.ANY)] * n_dep]
    call = pl.pallas_call(with_deps, **kw)
    return lambda *args: call(*args, *deps)


def _params(sem=None):
    return pltpu.CompilerParams(dimension_semantics=sem, vmem_limit_bytes=VMEM_LIMIT)


def _my_pos():
    return lax.axis_index("x"), lax.axis_index("y"), lax.axis_index("c")


def _my_rank():
    x, y, c = _my_pos()
    return 4 * x + 2 * y + c


def _all_gather(name, arrays, deps=()):
    n = len(arrays)

    def body(*refs):
        ins = refs[:n]
        outs = refs[n:2 * n]
        send_sems, recv_sems, local_sems = refs[2 * n:]
        x, y, c = _my_pos()
        me, sibling = (x, y, c), (x, y, 1 - c)
        chips = [(1 - x, y), (x, 1 - y), (1 - x, 1 - y)]

        def slot(a, px, py, pc):
            return outs[a].at[4 * px + 2 * py + pc]

        def copy(a, k, block, to, src=None):
            return pltpu.make_async_remote_copy(
                src_ref=slot(a, *block) if src is None else src, dst_ref=slot(a, *block),
                send_sem=send_sems.at[a, k], recv_sem=recv_sems.at[a, k],
                device_id=to, device_id_type=MESH)

        started = []
        for a in range(n):
            mine = pltpu.make_async_copy(ins[a], slot(a, *me), local_sems.at[a])
            mine.start()
            started.append(mine)
        first = []
        for a in range(n):
            first.append(copy(a, 0, me, sibling, src=ins[a]))
            first += [copy(a, 1 + j, me, (*chip, c), src=ins[a]) for j, chip in enumerate(chips)]
        for cp in first:
            cp.start()
        passed = []
        for j, chip in enumerate(chips):
            for a in range(n):
                copy(a, 1 + j, (*chip, c), me).wait_recv()
                fwd = copy(a, 4 + j, (*chip, c), sibling)
                fwd.start()
                passed.append(fwd)
        for a in range(n):
            copy(a, 0, sibling, me).wait_recv()
            for j, chip in enumerate(chips):
                copy(a, 4 + j, (*chip, 1 - c), me).wait_recv()
        for cp in first + passed:
            cp.wait_send()
        for mine in started:
            mine.wait()

    hbm = pl.BlockSpec(memory_space=pltpu.HBM)
    return _call(
        body, deps=deps, name=name,
        out_shape=[jax.ShapeDtypeStruct((N_DEV,) + a.shape, a.dtype) for a in arrays],
        in_specs=[hbm] * n, out_specs=[hbm] * n,
        scratch_shapes=[pltpu.SemaphoreType.DMA((n, 7)), pltpu.SemaphoreType.DMA((n, 7)),
                        pltpu.SemaphoreType.DMA((n,))],
    )(*arrays)


def _plane_tiles(rr, cc, tr=512, tc=512):
    if rr % 8 == 0:
        tr = _tile(rr, tr)
        return (rr // tr, pl.BlockSpec((tr, cc), lambda i: (i, 0)),
                pl.BlockSpec((N_DEV, tr, cc), lambda i: (0, i, 0)))
    tc = _tile(cc, tc)
    return (cc // tc, pl.BlockSpec((rr, tc), lambda i: (0, i)),
            pl.BlockSpec((N_DEV, rr, tc), lambda i: (0, 0, i)))


def _own_slot(name, src, gather, rank):
    shape = ((N_DEV,) + src.shape) if gather else src.shape
    rr, cc = shape[1], shape[2]
    by_rows = rr % 8 == 0
    tr, tc = (_tile(rr, 512), cc) if by_rows else (rr, _tile(cc, 512))
    steps = rr // tr if by_rows else cc // tc

    def body(rank_ref, s_ref, o_ref):
        o_ref[...] = s_ref[...].astype(o_ref.dtype)

    def at(i):
        return (i, 0) if by_rows else (0, i)

    if gather:
        in_spec = pl.BlockSpec((tr, tc), lambda i, rk: at(i))
    else:
        in_spec = pl.BlockSpec((None, tr, tc), lambda i, rk: (rk[0], *at(i)))
    grid_spec = pltpu.PrefetchScalarGridSpec(
        num_scalar_prefetch=1, grid=(steps,), in_specs=[in_spec],
        out_specs=pl.BlockSpec((None, tr, tc), lambda i, rk: (rk[0], *at(i))))
    return _call(body, name=name, grid_spec=grid_spec, out_shape=jax.ShapeDtypeStruct(shape, BF16),
                 compiler_params=_params(("arbitrary",)))(jnp.reshape(rank, (1,)).astype(jnp.int32), src)


_HBM = pl.BlockSpec(memory_space=pltpu.HBM)
_SEM = pl.BlockSpec(memory_space=pltpu.SEMAPHORE)
_FLIPS = [(kx, ky, kc) for kx in (0, 1) for ky in (0, 1) for kc in (0, 1)][1:]


def _peers():
    x, y, c = _my_pos()
    out = []
    for kx, ky, kc in _FLIPS:
        px, py, pc = (1 - x if kx else x), (1 - y if ky else y), (1 - c if kc else c)
        out.append(((px, py, pc), 4 * px + 2 * py + pc))
    return out


def _exchange_copy(srcs, lands, send_sems, recv_sems, a, k, peer, peer_rank, slot):
    return pltpu.make_async_remote_copy(
        src_ref=lands[a].at[slot] if srcs is None else srcs[a].at[peer_rank],
        dst_ref=lands[a].at[slot],
        send_sem=send_sems[a].at[k], recv_sem=recv_sems[a].at[k],
        device_id=peer, device_id_type=MESH)


def _exchange_start(name, lands, srcs=None, after=()):
    n = len(lands)
    n_src = 0 if srcs is None else n
    n_in = n + n_src + len(after)

    def body(*refs):
        lnd = refs[:n]
        src = None if srcs is None else refs[n:2 * n]
        send_sems, recv_sems = refs[n_in:n_in + n], refs[n_in + n:n_in + 2 * n]
        token = refs[-1]
        me = _my_rank()
        for a in range(n):
            for k, (peer, peer_rank) in enumerate(_peers()):
                _exchange_copy(src, lnd, send_sems, recv_sems, a, k, peer, peer_rank, me).start()
        token[...] = jnp.zeros_like(token)

    sems = [pltpu.SemaphoreType.DMA((7,))] * (2 * n)
    thru = list(lands) + ([] if srcs is None else list(srcs))
    outs = pl.pallas_call(
        body, name=name,
        out_shape=(*sems, *[pltpu.HBM(t.shape, t.dtype) for t in thru], jax.ShapeDtypeStruct((8, 128), F32)),
        in_specs=[*[_HBM] * len(thru), *[pl.BlockSpec(memory_space=pl.ANY)] * len(after)],
        out_specs=(*[_SEM] * (2 * n), *[_HBM] * len(thru), pl.BlockSpec(memory_space=pltpu.VMEM)),
        input_output_aliases={i: 2 * n + i for i in range(len(thru))},
        compiler_params=pltpu.CompilerParams(has_side_effects=pltpu.SideEffectType.DATAFLOW_SIDE_EFFECTING),
    )(*[pltpu.with_memory_space_constraint(t, pltpu.HBM) for t in thru], *after)
    lands_thru = outs[2 * n:3 * n]
    srcs_thru = None if srcs is None else outs[3 * n:4 * n]
    return outs[:n], outs[n:2 * n], srcs_thru, lands_thru, outs[-1]


def _exchange_wait(name, send_sems, recv_sems, srcs, lands, after):
    n = len(lands)
    thru = list(lands) + ([] if srcs is None else list(srcs))

    def body(*refs):
        lnd = refs[:n]
        src = None if srcs is None else refs[n:2 * n]
        ssem, rsem = refs[len(thru):len(thru) + n], refs[len(thru) + n:len(thru) + 2 * n]
        for a in range(n):
            for k, (peer, peer_rank) in enumerate(_peers()):
                cp = _exchange_copy(src, lnd, ssem, rsem, a, k, peer, peer_rank, peer_rank)
                cp.wait_send()
                cp.wait_recv()

    outs = pl.pallas_call(
        body, name=name,
        out_shape=tuple(pltpu.HBM(t.shape, t.dtype) for t in thru),
        in_specs=[*[_HBM] * len(thru), *[_SEM] * (2 * n), pl.BlockSpec(memory_space=pl.ANY)],
        out_specs=tuple([_HBM] * len(thru)),
        input_output_aliases={i: i for i in range(len(thru))},
        compiler_params=pltpu.CompilerParams(has_side_effects=pltpu.SideEffectType.DATAFLOW_SIDE_EFFECTING),
    )(*thru, *send_sems, *recv_sems, after)
    return outs[:n]


NN = ((1,), (0,))
NT = ((1,), (1,))
TN = ((0,), (0,))


def _matmul(name, a, b, *, contract, grid, a_spec, b_spec, out_specs, out_shapes, acc_shape,
            extra=(), extra_specs=(), epilogue=None, deps=()):
    nk = grid[2]
    n_extra = len(extra)
    n_out = len(out_shapes)

    def body(*refs):
        a_ref, b_ref = refs[0], refs[1]
        extra_refs = refs[2:2 + n_extra]
        out_refs = refs[2 + n_extra:2 + n_extra + n_out]
        acc_ref = refs[-1]
        k = pl.program_id(2)

        def prod():
            return lax.dot_general(a_ref[...].astype(BF16), b_ref[...].astype(BF16), (contract, ((), ())),
                                   preferred_element_type=F32)

        def finish(acc):
            res = (acc,) if epilogue is None else epilogue(acc, *[r[...] for r in extra_refs])
            for o_ref, val in zip(out_refs, res):
                o_ref[...] = val.astype(o_ref.dtype)

        if nk == 1:
            finish(prod())
            return

        @pl.when(k == 0)
        def _():
            acc_ref[...] = prod()

        @pl.when((k > 0) & (k < nk - 1))
        def _():
            acc_ref[...] += prod()

        @pl.when(k == nk - 1)
        def _():
            finish(acc_ref[...] + prod())

    outs = _call(
        body, deps=deps, name=name, grid=grid,
        in_specs=[a_spec, b_spec, *extra_specs], out_specs=list(out_specs), out_shape=list(out_shapes),
        scratch_shapes=[pltpu.VMEM(acc_shape if nk > 1 else (8, 128), F32)],
        compiler_params=_params(("parallel", "parallel", "arbitrary")),
    )(a, b, *extra)
    return outs


def _tile(n, t):
    t = min(n, t)
    assert n % t == 0, (n, t)
    return t


def _mm_plain(name, a, b, contract, out_dtype, tm=1024, tn=1024, tk=2048, extra=(), epilogue=None,
              n_out=1, out_dtypes=None, deps=()):
    if contract == NN:
        (m, kd), (_, n) = a.shape, b.shape
    elif contract == NT:
        (m, kd), (n, _) = a.shape, b.shape
    else:
        (kd, m), (_, n) = a.shape, b.shape
    tm, tn, tk = _tile(m, tm), _tile(n, tn), _tile(kd, tk)
    if contract == NN:
        a_spec = pl.BlockSpec((tm, tk), lambda i, j, k: (i, k))
        b_spec = pl.BlockSpec((tk, tn), lambda i, j, k: (k, j))
    elif contract == NT:
        a_spec = pl.BlockSpec((tm, tk), lambda i, j, k: (i, k))
        b_spec = pl.BlockSpec((tn, tk), lambda i, j, k: (j, k))
    else:
        a_spec = pl.BlockSpec((tk, tm), lambda i, j, k: (k, i))
        b_spec = pl.BlockSpec((tk, tn), lambda i, j, k: (k, j))
    o_spec = pl.BlockSpec((tm, tn), lambda i, j, k: (i, j))
    out_dtypes = out_dtypes or [out_dtype] * n_out
    outs = _matmul(
        name, a, b, contract=contract, grid=(m // tm, n // tn, kd // tk), a_spec=a_spec, b_spec=b_spec,
        out_specs=[o_spec] * len(out_dtypes), out_shapes=[jax.ShapeDtypeStruct((m, n), dt) for dt in out_dtypes],
        acc_shape=(tm, tn), extra=extra, extra_specs=[o_spec] * len(extra), epilogue=epilogue, deps=deps)
    return outs[0] if len(out_dtypes) == 1 else outs


def _rows_call(name, body, row_in, vec_in, row_out, vec_out, s, deps=()):
    t = _tile(s, ROW_T)
    in_specs = []
    args = []
    for arr, width, cb in row_in:
        in_specs.append(pl.BlockSpec((t, width), functools.partial(lambda i, cb: (i, cb), cb=cb)))
        args.append(arr)
    for v in vec_in:
        in_specs.append(pl.BlockSpec(v.shape, lambda i: (0, 0)))
        args.append(v)
    out_specs = []
    out_shapes = []
    for width, dt in row_out:
        out_specs.append(pl.BlockSpec((t, width), lambda i: (i, 0)))
        out_shapes.append(jax.ShapeDtypeStruct((s, width), dt))
    for width in vec_out:
        out_specs.append(pl.BlockSpec((1, width), lambda i: (0, 0)))
        out_shapes.append(jax.ShapeDtypeStruct((1, width), F32))
    return _call(body, deps=deps, name=name, grid=(s // t,), in_specs=in_specs, out_specs=out_specs,
                 out_shape=out_shapes, compiler_params=_params(("arbitrary",)))(*args)


def _acc_vec(ref, val):
    _acc_row(ref, jnp.sum(val, axis=0, keepdims=True))


def _acc_row(ref, part):
    @pl.when(pl.program_id(0) == 0)
    def _():
        ref[...] = part

    @pl.when(pl.program_id(0) > 0)
    def _():
        ref[...] += part


def _rms(v):
    return lax.rsqrt(jnp.mean(v * v, axis=-1, keepdims=True) + EPS)


def _norm_bwd(dxn, xn, r):
    return r * (dxn - xn * jnp.mean(dxn * xn, axis=-1, keepdims=True))


def _premix(x, g, scale, shift, deps=()):
    s = x.shape[0]

    def body(x_ref, g_ref, sc_ref, sh_ref, h_ref):
        xv = x_ref[...]
        h_ref[...] = ((xv * _rms(xv) * g_ref[...]) * (1.0 + sc_ref[...]) + sh_ref[...]).astype(BF16)

    return _rows_call("premix", body, [(x, D_MODEL, 0)], [g, scale, shift], [(D_MODEL, BF16)], [], s, deps)[0]


def _sigmoid(z):
    return 1.0 / (1.0 + jnp.exp(-z))


def _mix_fwd(o_fox, o_gla, pm, g_fox, g_gla):
    s = o_fox.shape[0]

    def body(of_ref, og_ref, gr_ref, gf_ref, gg_ref, mix_ref):
        for h in range(FOX_HEADS):
            sl = slice(h * FOX_HEAD_DIM, (h + 1) * FOX_HEAD_DIM)
            seg = of_ref[:, sl]
            mix_ref[:, sl] = (seg * _rms(seg) * gf_ref[:, sl]).astype(BF16)
        for h in range(GLA_HEADS):
            sl = slice(h * GLA_DV, (h + 1) * GLA_DV)
            seg = og_ref[:, sl]
            gr = gr_ref[:, sl].astype(F32)
            val = (seg * _rms(seg) * gg_ref[:, sl]) * (gr * _sigmoid(gr))
            mix_ref[:, pl.ds(FOX_HEADS * FOX_HEAD_DIM + h * GLA_DV, GLA_DV)] = val.astype(BF16)

    return _rows_call("mix_fwd", body, [(o_fox, 1024, 0), (o_gla, 1024, 0), (pm, 1024, 5)], [g_fox, g_gla],
                      [(D_MODEL, BF16)], [], s)[0]


def _mix_bwd(dmix, o_fox, o_gla, pm, g_fox, g_gla, deps=()):
    s = o_fox.shape[0]

    def body(dm_ref, of_ref, og_ref, gr_ref, gf_ref, gg_ref, dof_ref, dog_ref, dgr_ref, dgf_ref, dgg_ref):
        dgf = []
        for h in range(FOX_HEADS):
            sl = slice(h * FOX_HEAD_DIM, (h + 1) * FOX_HEAD_DIM)
            seg = of_ref[:, sl]
            r = _rms(seg)
            segn = seg * r
            dout = dm_ref[:, sl]
            dgf.append(jnp.sum(dout * segn, axis=0, keepdims=True))
            dof_ref[:, sl] = _norm_bwd(dout * gf_ref[:, sl], segn, r).astype(BF16)
        dgg = []
        for h in range(GLA_HEADS):
            sl = slice(h * GLA_DV, (h + 1) * GLA_DV)
            seg = og_ref[:, sl]
            r = _rms(seg)
            segn = seg * r
            gl = segn * gg_ref[:, sl]
            gr = gr_ref[:, sl].astype(F32)
            sig = _sigmoid(gr)
            dout = dm_ref[:, pl.ds(FOX_HEADS * FOX_HEAD_DIM + h * GLA_DV, GLA_DV)]
            dgr_ref[:, sl] = (dout * gl * (sig * (1.0 + gr * (1.0 - sig)))).astype(BF16)
            dgl = dout * (gr * sig)
            dgg.append(jnp.sum(dgl * segn, axis=0, keepdims=True))
            dog_ref[:, sl] = _norm_bwd(dgl * gg_ref[:, sl], segn, r).astype(BF16)
        _acc_row(dgf_ref, jnp.concatenate(dgf, axis=1))
        _acc_row(dgg_ref, jnp.concatenate(dgg, axis=1))

    return _rows_call("mix_bwd", body, [(dmix, D_MODEL, 0), (o_fox, 1024, 0), (o_gla, 1024, 0), (pm, 1024, 5)],
                      [g_fox, g_gla], [(1024, BF16), (1024, BF16), (1024, BF16)], [1024, 1024], s, deps)


def _postmix_premlp(x, y, gate_m, g_post_mix, g_pre_mlp, scale_f, shift_f):
    s = x.shape[0]

    def body(x_ref, y_ref, gm_ref, gpm_ref, gpl_ref, sc_ref, sh_ref, x1_ref, h2_ref):
        yv = y_ref[...]
        x1 = x_ref[...] + gm_ref[...] * (yv * _rms(yv) * gpm_ref[...])
        x1_ref[...] = x1
        h2_ref[...] = ((x1 * _rms(x1) * gpl_ref[...]) * (1.0 + sc_ref[...]) + sh_ref[...]).astype(BF16)

    return _rows_call("postmix_premlp", body, [(x, D_MODEL, 0), (y, D_MODEL, 0)],
                      [gate_m, g_post_mix, g_pre_mlp, scale_f, shift_f], [(D_MODEL, F32), (D_MODEL, BF16)], [], s)


def _loss_postmlp_bwd(x1, y2, target, gate_f, g_post_mlp):
    s = x1.shape[0]

    def body(x1_ref, y2_ref, t_ref, gf_ref, g_ref, dx2_ref, dy2_ref, loss_ref, dgate_ref, dg_ref):
        yv = y2_ref[...]
        r = _rms(yv)
        yn = yv * r
        o = yn * g_ref[...]
        e = (x1_ref[...] + gf_ref[...] * o) - t_ref[...]
        part = 0.5 * jnp.sum(jnp.mean(e * e, axis=-1, keepdims=True), axis=0, keepdims=True)
        _acc_vec(loss_ref, jnp.broadcast_to(part, (1, 128)))
        dx2 = e * (1.0 / D_MODEL)
        dx2_ref[...] = dx2
        _acc_vec(dgate_ref, dx2 * o)
        do = dx2 * gf_ref[...]
        _acc_vec(dg_ref, do * yn)
        dy2_ref[...] = _norm_bwd(do * g_ref[...], yn, r).astype(BF16)

    return _rows_call("loss_postmlp_bwd", body, [(x1, D_MODEL, 0), (y2, D_MODEL, 0), (target, D_MODEL, 0)],
                      [gate_f, g_post_mlp], [(D_MODEL, F32), (D_MODEL, BF16)], [128, D_MODEL, D_MODEL], s)


def _premlp_postmix_bwd(dh2, dx2, x1, y, scale_f, g_pre_mlp, gate_m, g_post_mix, deps=()):
    s = x1.shape[0]

    def body(dh2_ref, dx2_ref, x1_ref, y_ref, sc_ref, gpl_ref, gm_ref, gpm_ref,
             dx1_ref, dy_ref, dsc_ref, dsh_ref, dgpl_ref, dgm_ref, dgpm_ref):
        x1 = x1_ref[...]
        r1 = _rms(x1)
        x1n = x1 * r1
        dh2 = dh2_ref[...]
        _acc_vec(dsc_ref, dh2 * (x1n * gpl_ref[...]))
        _acc_vec(dsh_ref, dh2)
        dn2 = dh2 * (1.0 + sc_ref[...])
        _acc_vec(dgpl_ref, dn2 * x1n)
        dx1 = dx2_ref[...] + _norm_bwd(dn2 * gpl_ref[...], x1n, r1)
        dx1_ref[...] = dx1
        yv = y_ref[...]
        ry = _rms(yv)
        yn = yv * ry
        _acc_vec(dgm_ref, dx1 * (yn * gpm_ref[...]))
        do = dx1 * gm_ref[...]
        _acc_vec(dgpm_ref, do * yn)
        dy_ref[...] = _norm_bwd(do * gpm_ref[...], yn, ry).astype(BF16)

    return _rows_call("premlp_postmix_bwd", body,
                      [(dh2, D_MODEL, 0), (dx2, D_MODEL, 0), (x1, D_MODEL, 0), (y, D_MODEL, 0)],
                      [scale_f, g_pre_mlp, gate_m, g_post_mix], [(D_MODEL, F32), (D_MODEL, BF16)],
                      [D_MODEL] * 5, s, deps)


def _premix_bwd(dh, dx1, x, g_pre_mix, scale_m):
    s = x.shape[0]

    def body(dh_ref, dx1_ref, x_ref, g_ref, sc_ref, gx_ref, dsc_ref, dsh_ref, dg_ref):
        xv = x_ref[...]
        r = _rms(xv)
        xn = xv * r
        dh = dh_ref[...]
        _acc_vec(dsc_ref, dh * (xn * g_ref[...]))
        _acc_vec(dsh_ref, dh)
        dn1 = dh * (1.0 + sc_ref[...])
        _acc_vec(dg_ref, dn1 * xn)
        gx_ref[...] = dx1_ref[...] + _norm_bwd(dn1 * g_ref[...], xn, r)

    return _rows_call("premix_bwd", body, [(dh, D_MODEL, 0), (dx1, D_MODEL, 0), (x, D_MODEL, 0)],
                      [g_pre_mix, scale_m], [(D_MODEL, F32)], [D_MODEL] * 3, s)


def _split3(v):
    hi = v.astype(BF16)
    r1 = v - hi.astype(F32)
    mid = r1.astype(BF16)
    lo = (r1 - mid.astype(F32)).astype(BF16)
    return hi, mid, lo


def _dot_exact01(v, tri, contract=NN, tri_first=False):
    acc = None
    for part in _split3(v):
        lhs, rhs = (tri, part) if tri_first else (part, tri)
        p = lax.dot_general(lhs, rhs, (contract, ((), ())), preferred_element_type=F32)
        acc = p if acc is None else acc + p
    return acc


def _log_sigmoid(z):
    return jnp.minimum(z, 0.0) - jnp.log(1.0 + jnp.exp(-jnp.abs(z)))


def _fox_cum(small, bvec):
    s = small.shape[0]
    t = _tile(s, CUM_T)

    def body(sm_ref, b_ref, out_ref, carry):
        @pl.when(pl.program_id(0) == 0)
        def _():
            carry[...] = jnp.zeros_like(carry)

        lf = _log_sigmoid(sm_ref[...] + b_ref[...])
        lft = lf.T[0:FOX_HEADS, :]
        row = lax.broadcasted_iota(jnp.int32, (t, t), 0)
        col = lax.broadcasted_iota(jnp.int32, (t, t), 1)
        upper = (row <= col).astype(BF16)
        cum = _dot_exact01(lft, upper) + carry[:, 0:1]
        out_ref[...] = cum
        carry[...] = carry[...] + jnp.sum(lft, axis=1, keepdims=True)

    return _call(body, name="fox_cum", grid=(s // t,),
                 in_specs=[pl.BlockSpec((t, W_SMALL), lambda i: (i, 0)), pl.BlockSpec((1, W_SMALL), lambda i: (0, 0))],
                 out_specs=pl.BlockSpec((FOX_HEADS, t), lambda i: (0, i)),
                 out_shape=jax.ShapeDtypeStruct((FOX_HEADS, s), F32),
                 scratch_shapes=[pltpu.VMEM((FOX_HEADS, 128), F32)],
                 compiler_params=_params(("arbitrary",)))(small, bvec)


def _fox_cum_bwd(dc, dcq, small, bvec):
    s = small.shape[0]
    t = _tile(s, CUM_T)
    nb = s // t

    def body(dc_ref, dcq_ref, sm_ref, b_ref, out_ref, db_ref, carry):
        @pl.when(pl.program_id(0) == 0)
        def _():
            carry[...] = jnp.zeros_like(carry)
            db_ref[...] = jnp.zeros_like(db_ref)

        lane = lax.broadcasted_iota(jnp.int32, (t, W_SMALL), 1)
        dcq = jnp.zeros((t, W_SMALL), F32)
        for hh in range(FOX_HEADS):
            dcq = jnp.where(lane == hh, dcq_ref[hh], dcq)
        dcv = dc_ref[...] + dcq.T[0:FOX_HEADS, :]
        row = lax.broadcasted_iota(jnp.int32, (t, t), 0)
        col = lax.broadcasted_iota(jnp.int32, (t, t), 1)
        lower = (row >= col).astype(BF16)
        dlf = _dot_exact01(dcv, lower) + carry[:, 0:1]
        carry[...] = carry[...] + jnp.sum(dcv, axis=1, keepdims=True)
        z = sm_ref[...] + b_ref[...]
        zt = z.T[0:FOX_HEADS, :]
        dff = dlf * _sigmoid(-zt)
        db_ref[...] = db_ref[...] + jnp.sum(dff, axis=1, keepdims=True)
        full = jnp.concatenate([dff, jnp.zeros((W_SMALL - FOX_HEADS, t), F32)], axis=0)
        out_ref[...] = full.T

    return _call(body, name="fox_cum_bwd", grid=(nb,),
                 in_specs=[pl.BlockSpec((FOX_HEADS, t), lambda i: (0, nb - 1 - i)),
                           pl.BlockSpec((FOX_HEADS, t, 1), lambda i: (0, nb - 1 - i, 0)),
                           pl.BlockSpec((t, W_SMALL), lambda i: (nb - 1 - i, 0)),
                           pl.BlockSpec((1, W_SMALL), lambda i: (0, 0))],
                 out_specs=[pl.BlockSpec((t, W_SMALL), lambda i: (nb - 1 - i, 0)),
                            pl.BlockSpec((FOX_HEADS, 128), lambda i: (0, 0))],
                 out_shape=[jax.ShapeDtypeStruct((s, W_SMALL), F32), jax.ShapeDtypeStruct((FOX_HEADS, 128), F32)],
                 scratch_shapes=[pltpu.VMEM((FOX_HEADS, 128), F32)],
                 compiler_params=_params(("arbitrary",)))(dc, dcq, small, bvec)


FOX_SCALE = FOX_HEAD_DIM ** -0.5


def _fox_fwd(pm, crow):
    s = pm.shape[0]
    t = _tile(s, FOX_T)
    nb = s // t
    parts = 2
    hq = t // parts

    def body(q_ref, k_ref, v_ref, c_ref, o_ref, lse_ref):
        i = pl.program_id(1)
        qs = [q_ref[g * hq:(g + 1) * hq, :] for g in range(parts)]

        def block(j, carry, diagonal):
            rows = pl.ds(pl.multiple_of(j * t, t), t)
            k_all, v_all, c_all = k_ref[rows, :], v_ref[rows, :], c_ref[j]
            out = []
            for g, (m_prev, l_prev, acc) in enumerate(carry):
                nk = (g + 1) * hq if diagonal else t
                kb, vb, cb = k_all[:nk], v_all[:nk], c_all[:, :nk]
                sc = lax.dot_general(qs[g], kb, (NT, ((), ())), preferred_element_type=F32)
                sc = sc * FOX_SCALE - cb
                if diagonal:
                    row = lax.broadcasted_iota(jnp.int32, (hq, nk), 0) + g * hq
                    col = lax.broadcasted_iota(jnp.int32, (hq, nk), 1)
                    sc = jnp.where(row >= col, sc, NEG)
                m_new = jnp.maximum(m_prev, jnp.max(sc, axis=1, keepdims=True))
                alpha = jnp.exp(m_prev - m_new)
                p = jnp.exp(sc - m_new)
                l_new = alpha * l_prev + jnp.sum(p, axis=1, keepdims=True)
                p_hi = p.astype(BF16)
                p_lo = (p - p_hi.astype(F32)).astype(BF16)
                pv = jnp.dot(p_hi, vb, preferred_element_type=F32)
                pv = pv + jnp.dot(p_lo, vb, preferred_element_type=F32)
                out.append((m_new, l_new, alpha * acc + pv))
            return tuple(out)

        init = tuple((jnp.full((hq, 1), NEG, F32), jnp.zeros((hq, 1), F32), jnp.zeros((hq, 128), F32))
                     for _ in range(parts))
        carry = lax.fori_loop(0, i, lambda j, cr: block(j, cr, False), init)
        carry = block(i, carry, True)
        for g, (m_fin, l_fin, acc) in enumerate(carry):
            o_ref[g * hq:(g + 1) * hq, :] = acc / l_fin
            lse_ref[g * hq:(g + 1) * hq, :] = m_fin + jnp.log(l_fin)

    return _call(
        body, name="fox_fwd", grid=(FOX_HEADS, nb),
        in_specs=[pl.BlockSpec((t, 128), lambda h, i: (i, h)),
                  pl.BlockSpec((s, 128), lambda h, i: (0, FOX_HEADS + h)),
                  pl.BlockSpec((s, 128), lambda h, i: (0, 2 * FOX_HEADS + h)),
                  pl.BlockSpec((None, nb, 1, t), lambda h, i: (h, 0, 0, 0))],
        out_specs=[pl.BlockSpec((t, 128), lambda h, i: (i, h)),
                   pl.BlockSpec((None, t, 1), lambda h, i: (h, i, 0))],
        out_shape=[jax.ShapeDtypeStruct((s, FOX_HEADS * 128), F32), jax.ShapeDtypeStruct((FOX_HEADS, s, 1), F32)],
        compiler_params=_params(("parallel", "arbitrary")),
    )(pm, pm, pm, crow.reshape(FOX_HEADS, nb, 1, t))


def _fox_bwd(pm, crow, o, lse, do):
    s = pm.shape[0]
    t = _tile(s, FOX_T)
    nb = s // t

    parts = 2
    hq = t // parts

    def body(q_ref, do_ref, o_ref, lse_ref, k_ref, v_ref, c_ref, dq_ref, dk_ref, dv_ref, dc_ref, dcq_ref, delta_s):
        j = pl.program_id(1)

        @pl.when(j == 0)
        def _():
            dq_ref[...] = jnp.zeros_like(dq_ref)
            dcq_ref[...] = jnp.zeros_like(dcq_ref)
            delta_s[...] = jnp.sum(do_ref[...].astype(F32) * o_ref[...], axis=1, keepdims=True)

        k_all, v_all, c_all = k_ref[...], v_ref[...], c_ref[...]

        def grow(acc, part, axis):
            n = part.shape[axis]
            if n == acc.shape[axis]:
                return acc + part
            if axis == 0:
                return jnp.concatenate([acc[:n] + part, acc[n:]], axis=0)
            return jnp.concatenate([acc[:, :n] + part, acc[:, n:]], axis=1)

        def block(i, carry, diagonal):
            dk_acc, dv_acc, dc_acc = carry
            for g in range(parts):
                nk = (g + 1) * hq if diagonal else t
                kb, vb, cb = k_all[:nk], v_all[:nk], c_all[:, :nk]
                rows = pl.ds(pl.multiple_of(i * t + g * hq, hq), hq)
                q, dov = q_ref[rows, :], do_ref[rows, :]
                sc = lax.dot_general(q, kb, (NT, ((), ())), preferred_element_type=F32)
                p = jnp.exp(sc * FOX_SCALE - cb - lse_ref[rows, :])
                if diagonal:
                    row = lax.broadcasted_iota(jnp.int32, (hq, nk), 0) + g * hq
                    col = lax.broadcasted_iota(jnp.int32, (hq, nk), 1)
                    p = jnp.where(row >= col, p, 0.0)
                dp = lax.dot_general(dov, vb, (NT, ((), ())), preferred_element_type=F32)
                ds = p * (dp - delta_s[rows, :])
                dsb = ds.astype(BF16)
                dv_acc = grow(dv_acc, lax.dot_general(p.astype(BF16), dov, (TN, ((), ())),
                                                      preferred_element_type=F32), 0)
                dk_acc = grow(dk_acc, lax.dot_general(dsb, q, (TN, ((), ())), preferred_element_type=F32), 0)
                dq_ref[rows, :] += jnp.dot(dsb, kb, preferred_element_type=F32) * FOX_SCALE
                dc_acc = grow(dc_acc, -jnp.sum(ds, axis=0, keepdims=True), 1)
                dcq_ref[rows, :] += jnp.sum(ds, axis=1, keepdims=True)
            return dk_acc, dv_acc, dc_acc

        carry = (jnp.zeros((t, 128), F32), jnp.zeros((t, 128), F32), jnp.zeros((1, t), F32))
        carry = block(j, carry, True)
        dk_acc, dv_acc, dc_acc = lax.fori_loop(j + 1, nb, lambda i, cr: block(i, cr, False), carry)
        dk_ref[...] = dk_acc * FOX_SCALE
        dv_ref[...] = dv_acc
        dc_ref[...] = dc_acc

    whole = lambda h, j: (0, h)
    return _call(
        body, name="fox_bwd", grid=(FOX_HEADS, nb),
        in_specs=[pl.BlockSpec((s, 128), whole), pl.BlockSpec((s, 128), whole), pl.BlockSpec((s, 128), whole),
                  pl.BlockSpec((None, s, 1), lambda h, j: (h, 0, 0)),
                  pl.BlockSpec((t, 128), lambda h, j: (j, FOX_HEADS + h)),
                  pl.BlockSpec((t, 128), lambda h, j: (j, 2 * FOX_HEADS + h)),
                  pl.BlockSpec((None, 1, t), lambda h, j: (h, 0, j))],
        out_specs=[pl.BlockSpec((s, 128), whole),
                   pl.BlockSpec((t, 128), lambda h, j: (j, h)),
                   pl.BlockSpec((t, 128), lambda h, j: (j, h)),
                   pl.BlockSpec((None, 1, t), lambda h, j: (h, 0, j)),
                   pl.BlockSpec((None, s, 1), lambda h, j: (h, 0, 0))],
        out_shape=[jax.ShapeDtypeStruct((s, 1024), F32), jax.ShapeDtypeStruct((s, 1024), F32),
                   jax.ShapeDtypeStruct((s, 1024), F32), jax.ShapeDtypeStruct((FOX_HEADS, 1, s), F32),
                   jax.ShapeDtypeStruct((FOX_HEADS, s, 1), F32)],
        scratch_shapes=[pltpu.VMEM((s, 1), F32)],
        compiler_params=_params(("parallel", "arbitrary")),
    )(pm, do, o, lse, pm, pm, crow)


GLA_SCALE = GLA_DK ** -0.5
GLA_Q_BLK = 3072 // 128
GLA_K_BLK = 3584 // 128
GLA_V_BLK = 4096 // 256


def _gla_gate(sm, wa_ref, b_ref):
    return jnp.dot(sm.astype(BF16), wa_ref[...], preferred_element_type=F32) + b_ref[...]


def _tri(n, strict):
    row = lax.broadcasted_iota(jnp.int32, (n, n), 0)
    col = lax.broadcasted_iota(jnp.int32, (n, n), 1)
    return ((row > col) if strict else (row >= col)).astype(BF16)


def _gla_fwd(pm, small, wa_pad, b_a2):
    s = pm.shape[0]
    r = _tile(s, GLA_R)
    nc = r // CHUNK

    def body(q_ref, k_ref, v_ref, sm_ref, wa_ref, b_ref, o_ref, st_ref, state):
        @pl.when(pl.program_id(1) == 0)
        def _():
            state[...] = jnp.zeros_like(state)

        tri = _tri(CHUNK, False)
        for c in range(nc):
            rows = slice(c * CHUNK, (c + 1) * CHUNK)
            la = _log_sigmoid(_gla_gate(sm_ref[rows, :], wa_ref, b_ref)) * (1.0 / GLA_TEMP)
            cum = _dot_exact01(la, tri, tri_first=True)
            total = jnp.sum(la, axis=0, keepdims=True)
            kdec = k_ref[rows, :].astype(F32) * jnp.exp(total - cum)
            ut = lax.dot_general(v_ref[rows, :], kdec.astype(BF16), (TN, ((), ())), preferred_element_type=F32)
            new = state[...] * jnp.exp(total) + ut
            state[...] = new
            newb = new.astype(BF16)
            st_ref[c] = newb
            qs = (q_ref[rows, :].astype(F32) * GLA_SCALE).astype(BF16)
            o_ref[rows, :] = lax.dot_general(qs, newb, (NT, ((), ())), preferred_element_type=F32)

    return _call(
        body, name="gla_fwd", grid=(GLA_HEADS, s // r),
        in_specs=[pl.BlockSpec((r, 128), lambda h, i: (i, GLA_Q_BLK + h)),
                  pl.BlockSpec((r, 128), lambda h, i: (i, GLA_K_BLK + h)),
                  pl.BlockSpec((r, 256), lambda h, i: (i, GLA_V_BLK + h)),
                  pl.BlockSpec((r, W_SMALL), lambda h, i: (i, 0)),
                  pl.BlockSpec((W_SMALL, 128), lambda h, i: (0, h)),
                  pl.BlockSpec((1, 128), lambda h, i: (0, h))],
        out_specs=[pl.BlockSpec((r, 256), lambda h, i: (i, h)),
                   pl.BlockSpec((nc, None, GLA_DV, GLA_DK), lambda h, i: (i, h, 0, 0))],
        out_shape=[jax.ShapeDtypeStruct((s, 1024), F32),
                   jax.ShapeDtypeStruct((s // CHUNK, GLA_HEADS, GLA_DV, GLA_DK), BF16)],
        scratch_shapes=[pltpu.VMEM((GLA_DV, GLA_DK), F32)],
        compiler_params=_params(("parallel", "arbitrary")),
    )(pm, pm, pm, small, wa_pad, b_a2)


def _gla_bwd(pm, small, wa_pad, b_a2, states, do):
    s = pm.shape[0]
    r = _tile(s, GLA_R)
    nc = r // CHUNK
    nb = s // r

    def body(q_ref, k_ref, v_ref, sm_ref, wa_ref, b_ref, do_ref, st_ref, prev_ref,
             dq_ref, dk_ref, dv_ref, dza_ref, db_ref, carry):
        step = pl.program_id(1)

        @pl.when(step == 0)
        def _():
            carry[...] = jnp.zeros_like(carry)
            db_ref[...] = jnp.zeros_like(db_ref)

        tri = _tri(CHUNK, False)
        tri_strict = _tri(CHUNK, True)
        db = jnp.zeros((1, 128), F32)
        for c in reversed(range(nc)):
            rows = slice(c * CHUNK, (c + 1) * CHUNK)
            z = _gla_gate(sm_ref[rows, :], wa_ref, b_ref)
            la = _log_sigmoid(z) * (1.0 / GLA_TEMP)
            cum = _dot_exact01(la, tri, tri_first=True)
            total = jnp.sum(la, axis=0, keepdims=True)
            w = jnp.exp(total - cum)
            decay = jnp.exp(total)
            kdec = k_ref[rows, :].astype(F32) * w
            dov = do_ref[rows, :]
            qs = (q_ref[rows, :].astype(F32) * GLA_SCALE).astype(BF16)
            dq_ref[rows, :] = jnp.dot(dov, st_ref[c], preferred_element_type=F32) * GLA_SCALE
            gt = lax.dot_general(dov, qs, (TN, ((), ())), preferred_element_type=F32) + carry[...]
            gtb = gt.astype(BF16)
            dv_ref[rows, :] = lax.dot_general(kdec.astype(BF16), gtb, (NT, ((), ())), preferred_element_type=F32)
            dkdec = jnp.dot(v_ref[rows, :], gtb, preferred_element_type=F32)
            dk_ref[rows, :] = dkdec * w
            e = dkdec * kdec
            if c > 0:
                prev = st_ref[c - 1].astype(F32)
            else:
                prev = jnp.where(step == nb - 1, 0.0, prev_ref[0].astype(F32))
            dtot = jnp.sum(gt * prev, axis=0, keepdims=True) * decay
            dla = dtot + _dot_exact01(e, tri_strict, tri_first=True)
            dza = dla * (1.0 / GLA_TEMP) * _sigmoid(-z)
            dza_ref[rows, :] = dza.astype(BF16)
            db = db + jnp.sum(dza, axis=0, keepdims=True)
            carry[...] = gt * decay
        db_ref[...] += db

    blk = lambda h, i: nb - 1 - i
    return _call(
        body, name="gla_bwd", grid=(GLA_HEADS, nb),
        in_specs=[pl.BlockSpec((r, 128), lambda h, i: (blk(h, i), GLA_Q_BLK + h)),
                  pl.BlockSpec((r, 128), lambda h, i: (blk(h, i), GLA_K_BLK + h)),
                  pl.BlockSpec((r, 256), lambda h, i: (blk(h, i), GLA_V_BLK + h)),
                  pl.BlockSpec((r, W_SMALL), lambda h, i: (blk(h, i), 0)),
                  pl.BlockSpec((W_SMALL, 128), lambda h, i: (0, h)),
                  pl.BlockSpec((1, 128), lambda h, i: (0, h)),
                  pl.BlockSpec((r, 256), lambda h, i: (blk(h, i), h)),
                  pl.BlockSpec((nc, None, GLA_DV, GLA_DK), lambda h, i: (blk(h, i), h, 0, 0)),
                  pl.BlockSpec((1, None, GLA_DV, GLA_DK),
                               lambda h, i: (jnp.maximum(blk(h, i) * nc - 1, 0), h, 0, 0))],
        out_specs=[pl.BlockSpec((r, 128), lambda h, i: (blk(h, i), h)),
                   pl.BlockSpec((r, 128), lambda h, i: (blk(h, i), h)),
                   pl.BlockSpec((r, 256), lambda h, i: (blk(h, i), h)),
                   pl.BlockSpec((r, 128), lambda h, i: (blk(h, i), h)),
                   pl.BlockSpec((1, 128), lambda h, i: (0, h))],
        out_shape=[jax.ShapeDtypeStruct((s, 512), F32), jax.ShapeDtypeStruct((s, 512), F32),
                   jax.ShapeDtypeStruct((s, 1024), F32), jax.ShapeDtypeStruct((s, 512), BF16),
                   jax.ShapeDtypeStruct((1, 512), F32)],
        scratch_shapes=[pltpu.VMEM((GLA_DV, GLA_DK), F32)],
        compiler_params=_params(("parallel", "arbitrary")),
    )(pm, pm, pm, small, wa_pad, b_a2, do, states, states)


def _modulation(c_all, w_ada):
    n = w_ada.shape[1]
    tn = _tile(n, 512)

    def body(c_ref, w_ref, out_ref, ca_ref):
        cv = c_ref[...]
        ca = cv * _sigmoid(cv)
        ca_ref[...] = ca
        out_ref[...] = jnp.dot(ca.astype(BF16), w_ref[...].astype(BF16), preferred_element_type=F32)

    return _call(body, name="modulation", grid=(n // tn,),
                 in_specs=[pl.BlockSpec((N_DEV, D_MODEL), lambda j: (0, 0)),
                           pl.BlockSpec((D_MODEL, tn), lambda j: (0, j))],
                 out_specs=[pl.BlockSpec((N_DEV, tn), lambda j: (0, j)),
                            pl.BlockSpec((N_DEV, D_MODEL), lambda j: (0, 0))],
                 out_shape=[jax.ShapeDtypeStruct((N_DEV, n), F32), jax.ShapeDtypeStruct((N_DEV, D_MODEL), F32)],
                 compiler_params=_params(("arbitrary",)))(c_all, w_ada)


def _adamw_math(w, g, m, v):
    m = ADAM_B1 * m + (1.0 - ADAM_B1) * g
    v = ADAM_B2 * v + (1.0 - ADAM_B2) * (g * g)
    m_hat = m / (1.0 - ADAM_B1 ** ADAM_STEP)
    v_hat = v / (1.0 - ADAM_B2 ** ADAM_STEP)
    delta = -ADAM_LR * (m_hat / (jnp.sqrt(v_hat) + ADAM_EPS) + ADAM_WD * w)
    return delta, m, v


def _adamw_slabs(name, w, slabs, m, v, tr=256):
    rr, cc = w.shape

    def body(w_ref, s_ref, m_ref, v_ref, g_ref, d_ref, nm_ref, nv_ref):
        g = s_ref[0].astype(F32)
        for r in range(1, N_DEV):
            g = g + s_ref[r].astype(F32)
        g_ref[...] = g
        d, nm, nv = _adamw_math(w_ref[...], g, m_ref[...], v_ref[...])
        d_ref[...] = d
        nm_ref[...] = nm
        nv_ref[...] = nv

    steps, spec, slab_spec = _plane_tiles(rr, cc, tr)
    return _call(body, name=name, grid=(steps,),
                 in_specs=[spec, slab_spec, spec, spec],
                 out_specs=[spec] * 4, out_shape=[jax.ShapeDtypeStruct((rr, cc), F32)] * 4,
                 compiler_params=_params(("parallel",)))(w, slabs, m, v)


def _adamw_ada(w, cat, dm, m, v, tr=256):
    rr, cc = w.shape
    tr = _tile(rr, tr)

    def body(w_ref, ca_ref, dm_ref, m_ref, v_ref, g_ref, d_ref, nm_ref, nv_ref):
        g = ca_ref[:, 0:1] * dm_ref[0:1, :]
        for b in range(1, N_DEV):
            g = g + ca_ref[:, b:b + 1] * dm_ref[b:b + 1, :]
        g_ref[...] = g
        d, nm, nv = _adamw_math(w_ref[...], g, m_ref[...], v_ref[...])
        d_ref[...] = d
        nm_ref[...] = nm
        nv_ref[...] = nv

    spec = pl.BlockSpec((tr, cc), lambda i: (i, 0))
    return _call(body, name="adamw_ada", grid=(rr // tr,),
                 in_specs=[spec, pl.BlockSpec((tr, N_DEV), lambda i: (i, 0)),
                           pl.BlockSpec((N_DEV, cc), lambda i: (0, 0)), spec, spec],
                 out_specs=[spec] * 4, out_shape=[jax.ShapeDtypeStruct((rr, cc), F32)] * 4,
                 compiler_params=_params(("parallel",)))(w, cat, dm, m, v)


def _sum_devices(gathered):
    ln = gathered.shape[-1]

    def body(g_ref, out_ref):
        acc = g_ref[0]
        for r in range(1, N_DEV):
            acc = acc + g_ref[r]
        out_ref[...] = acc

    return _call(body, name="sum_devices",
                 in_specs=[pl.BlockSpec(memory_space=pltpu.VMEM)], out_specs=pl.BlockSpec(memory_space=pltpu.VMEM),
                 out_shape=jax.ShapeDtypeStruct((1, ln), F32))(gathered)


def _adamw_flat(w, g, m, v):
    def body(w_ref, g_ref, m_ref, v_ref, d_ref, nm_ref, nv_ref):
        d, nm, nv = _adamw_math(w_ref[...], g_ref[...], m_ref[...], v_ref[...])
        d_ref[...] = d
        nm_ref[...] = nm
        nv_ref[...] = nv

    vm = pl.BlockSpec(memory_space=pltpu.VMEM)
    return _call(body, name="adamw_small", in_specs=[vm] * 4, out_specs=[vm] * 3,
                 out_shape=[jax.ShapeDtypeStruct(w.shape, F32)] * 3)(w, g, m, v)


def _from_col_shards(g):
    return jnp.transpose(g, (1, 0, 2)).reshape(g.shape[1], N_DEV * g.shape[2])


def _pad_lanes(v, n):
    return jnp.concatenate([v, jnp.zeros(v.shape[:-1] + (n - v.shape[-1],), v.dtype)], axis=-1)


def kernel(x, c, w_ada, b_ada, g_pre_mix, g_post_mix, w_in, b_fgate, w_gla_a2, b_gla_a2, g_fox_out, g_gla_out, w_out, g_pre_mlp, g_post_mlp, w_mlp_in, w_mlp_out, loss_target, m_w_ada, m_b_ada, m_g_pre_mix, m_g_post_mix, m_w_in, m_b_fgate, m_w_gla_a2, m_b_gla_a2, m_g_fox_out, m_g_gla_out, m_w_out, m_g_pre_mlp, m_g_post_mlp, m_w_mlp_in, m_w_mlp_out, v_w_ada, v_b_ada, v_g_pre_mix, v_g_post_mix, v_w_in, v_b_fgate, v_w_gla_a2, v_b_gla_a2, v_g_fox_out, v_g_gla_out, v_w_out, v_g_pre_mlp, v_g_post_mlp, v_w_mlp_in, v_w_mlp_out):
    rank = _my_rank()
    xs = x[0]
    s = xs.shape[0]
    target = loss_target[0]

    w_in_t, m_in_t, v_in_t = w_in[0].T, m_w_in[0].T, v_w_in[0].T
    c_all, wa2_g, ggla_g, win_g = _all_gather("gather_first", [c, w_gla_a2[0], g_gla_out[0], w_in_t.astype(BF16)])
    rest = [_own_slot("own_w_out", w_out[0], True, rank), _own_slot("own_w_mlp_in", w_mlp_in[0], True, rank)]
    gs_send, gs_recv, _, gs_land, gs_token = _exchange_start("gather_rest_start", rest, after=(c_all,))
    last = [_own_slot("own_w_mlp_out", w_mlp_out[0], True, rank)]
    gl_send, gl_recv, _, gl_land, gl_token = _exchange_start("gather_last_start", last, after=(gs_token,))
    w_a2 = _from_col_shards(wa2_g)
    g_gla = _from_col_shards(ggla_g).reshape(1, 1024)
    g_fox = g_fox_out.reshape(1, 1024)
    win_full = win_g.reshape(N_DEV * 771, D_MODEL)
    w_main = jnp.concatenate([win_full[:3072], win_full[3080:5128], win_full[5144:6168]], axis=0)
    w_small = jnp.concatenate([win_full[3072:3080], win_full[5128:5144],
                               jnp.zeros((W_SMALL - 24, D_MODEL), BF16)], axis=0)
    wa_pad =jnp.concatenate([jnp.zeros((8, 512), BF16), w_a2.astype(BF16), jnp.zeros((104, 512), BF16)], axis=0)
    bf_vec = _pad_lanes(b_fgate, W_SMALL)

    mod_part, c_act = _modulation(c_all.reshape(N_DEV, D_MODEL), w_ada[0])
    (mod_g,) = _all_gather("gather_mod", [mod_part])
    mod = lax.dynamic_slice_in_dim(mod_g, rank, 1, axis=1).reshape(1, 6 * D_MODEL) + b_ada
    shift_m, scale_m, gate_m, shift_f, scale_f, gate_f = [mod[:, i * D_MODEL:(i + 1) * D_MODEL] for i in range(6)]

    h = _premix(xs, g_pre_mix, scale_m, shift_m, deps=(gl_token,))
    pm = _mm_plain("proj_main", h, w_main, NT, BF16)
    small = _mm_plain("proj_small", h, w_small, NT, F32)
    crow = _fox_cum(small, bf_vec).reshape(FOX_HEADS, 1, s)
    o_fox, lse = _fox_fwd(pm, crow)
    o_gla, states = _gla_fwd(pm, small, wa_pad, b_gla_a2)
    mix = _mix_fwd(o_fox, o_gla, pm, g_fox, g_gla)
    wout_g, wmi_g = _exchange_wait("gather_rest_wait", gs_send, gs_recv, None, gs_land, mix)
    w_out_full = wout_g.reshape(D_MODEL, D_MODEL)
    y =_mm_plain("out_proj", mix, w_out_full, NN, F32)
    x1, h2 = _postmix_premlp(xs, y, gate_m, g_post_mix, g_pre_mlp, scale_f, shift_f)

    tm, tn, tk = _tile(s, 1024), 1024, 2048
    nsh = 1024 // tn

    def relu2(acc):
        rl = jnp.maximum(acc, 0.0)
        return rl * rl, rl

    z, a_relu = _matmul(
        "mlp_in", h2, wmi_g, contract=NN, grid=(s // tm, D_FF // tn, D_MODEL // tk),
        a_spec=pl.BlockSpec((tm, tk), lambda i, j, k: (i, k)),
        b_spec=pl.BlockSpec((None, tk, tn), lambda i, j, k: (j // nsh, k, j % nsh)),
        out_specs=[pl.BlockSpec((tm, tn), lambda i, j, k: (i, j))] * 2,
        out_shapes=[jax.ShapeDtypeStruct((s, D_FF), BF16)] * 2, acc_shape=(tm, tn), epilogue=relu2)
    (wmo_g,) = _exchange_wait("gather_last_wait", gl_send, gl_recv, None, gl_land, z)
    w_mo_full = wmo_g.reshape(D_FF, D_MODEL)
    y2 = _mm_plain("mlp_out", z, w_mo_full, NN, F32)

    dx2, dy2, loss_vec, dgate_f, dg_post_mlp = _loss_postmlp_bwd(x1, y2, target, gate_f, g_post_mlp)
    loss = lax.psum(loss_vec[0, 0], ("x", "y", "c"))

    da = _mm_plain("mlp_out_dx", dy2, w_mo_full, NT, BF16, extra=(a_relu,),
                   epilogue=lambda acc, rl: (acc * (2.0 * rl.astype(F32)),))
    dw_mo = _mm_plain("mlp_out_dw", z, dy2, TN, BF16)
    dw_mo = dw_mo.reshape(N_DEV, 1024, D_MODEL)
    x_mo = _exchange_start("grad_mlp_out_start", [_own_slot("own_dw_mlp_out", dw_mo, False, rank)], [dw_mo])
    tkx = 1024
    (dh2,) = _matmul(
        "mlp_in_dx", da, wmi_g, contract=NT, grid=(s // tm, D_MODEL // tn, D_FF // tkx),
        a_spec=pl.BlockSpec((tm, tkx), lambda i, j, k: (i, k)),
        b_spec=pl.BlockSpec((None, tn, tkx), lambda i, j, k: (k, j, 0)),
        out_specs=[pl.BlockSpec((tm, tn), lambda i, j, k: (i, j))],
        out_shapes=[jax.ShapeDtypeStruct((s, D_MODEL), F32)], acc_shape=(tm, tn), deps=(x_mo[4],))
    ts = _tile(s, 2048)
    (dw_mi,) = _matmul(
        "mlp_in_dw", h2, da, contract=TN, grid=(D_MODEL // 1024, D_FF // tn, s // ts),
        a_spec=pl.BlockSpec((ts, 1024), lambda i, j, k: (k, i)),
        b_spec=pl.BlockSpec((ts, tn), lambda i, j, k: (k, j)),
        out_specs=[pl.BlockSpec((None, 1024, tn), lambda i, j, k: (j // nsh, i, j % nsh))],
        out_shapes=[jax.ShapeDtypeStruct((N_DEV, D_MODEL, 1024), BF16)], acc_shape=(1024, tn))
    x_mi = _exchange_start("grad_mlp_in_start", [_own_slot("own_dw_mlp_in", dw_mi, False, rank)], [dw_mi])

    dx1, dy, dscale_f, dshift_f, dg_pre_mlp, dgate_m, dg_post_mix = _premlp_postmix_bwd(
        dh2, dx2, x1, y, scale_f, g_pre_mlp, gate_m, g_post_mix, deps=(x_mi[4],))

    dmix = _mm_plain("out_proj_dx", dy, w_out_full, NT, F32)
    dw_out = _mm_plain("out_proj_dw", mix, dy, TN, BF16)
    dw_out = dw_out.reshape(N_DEV, 256, D_MODEL)
    x_out = _exchange_start("grad_out_start", [_own_slot("own_dw_out", dw_out, False, rank)], [dw_out])
    do_fox, do_gla, dgr, dg_fox, dg_gla = _mix_bwd(dmix, o_fox, o_gla, pm, g_fox, g_gla, deps=(x_out[4],))

    dq, dk, dv, dc, dcq = _fox_bwd(pm, crow, o_fox, lse, do_fox)
    dsmall_f, db_f = _fox_cum_bwd(dc.reshape(FOX_HEADS, s), dcq, small, bf_vec)
    dgq, dgk, dgv, dza, db_a2 = _gla_bwd(pm, small, wa_pad, b_gla_a2, states, do_gla)
    dsmall = _mm_plain("gate_dx", dza, wa_pad, NT, F32, tn=128, extra=(dsmall_f,),
                       epilogue=lambda acc, other: (acc + other,))
    dwa_pad = _mm_plain("gate_dw", small, dza, TN, F32, tm=128, tn=512)

    dpm = jnp.concatenate([dq.astype(BF16), dk.astype(BF16), dv.astype(BF16), dgq.astype(BF16), dgk.astype(BF16),
                           dgv.astype(BF16), dgr], axis=1)
    dw_main = _mm_plain("proj_main_dw", dpm, h, TN, BF16)
    dw_small = _mm_plain("proj_small_dw", dsmall, h, TN, BF16, tm=128)
    dwin_full = jnp.concatenate([dw_main[:3072], dw_small[0:8], dw_main[3072:5120], dw_small[8:24],
                                 dw_main[5120:6144]], axis=0)
    dwin_slabs = dwin_full.reshape(N_DEV, 771, D_MODEL)
    x_in = _exchange_start("grad_in_start", [_own_slot("own_dw_in", dwin_slabs, False, rank)], [dwin_slabs])
    dh_small = _mm_plain("proj_small_dx", dsmall, w_small, NN, F32, tk=128)
    dh = _mm_plain("proj_main_dx", dpm, w_main, NN, F32, extra=(dh_small,),
                   epilogue=lambda acc, other: (acc + other,), deps=(x_in[4],))
    grad_x, dscale_m, dshift_m, dg_pre_mix = _premix_bwd(dh, dx1, xs, g_pre_mix, scale_m)

    dmod = jnp.concatenate([dshift_m, dscale_m, dgate_m, dshift_f, dscale_f, dgate_f], axis=1)
    flat = jnp.concatenate(
        [dmod, dg_pre_mix, dg_post_mix, dg_fox, dg_pre_mlp, dg_post_mlp, db_a2,
         dwa_pad[8:24, :].reshape(1, GLA_RANK * 512), dg_gla, _pad_lanes(db_f[:, 0].reshape(1, FOX_HEADS), 128)],
        axis=1)

    (r_mo,) = _exchange_wait("grad_mlp_out_wait", *x_mo[:4], grad_x)
    g_mo, d_mo, nm_mo, nv_mo = _adamw_slabs("adamw_w_mlp_out", w_mlp_out[0], r_mo, m_w_mlp_out[0], v_w_mlp_out[0])
    (r_mi,) = _exchange_wait("grad_mlp_in_wait", *x_mi[:4], g_mo)
    g_mi, d_mi, nm_mi, nv_mi = _adamw_slabs("adamw_w_mlp_in", w_mlp_in[0], r_mi, m_w_mlp_in[0], v_w_mlp_in[0])
    (r_out,) = _exchange_wait("grad_out_wait", *x_out[:4], g_mi)
    g_out, d_out, nm_out, nv_out = _adamw_slabs("adamw_w_out", w_out[0], r_out, m_w_out[0], v_w_out[0])

    (flat_g,) = _all_gather("gather_small_grads", [flat], deps=(g_out,))
    tot = _sum_devices(flat_g)
    dm_cols = lax.dynamic_slice_in_dim(flat_g[:, 0, :6 * D_MODEL], rank * 1536, 1536, axis=1)
    g_ada, d_ada, nm_ada, nv_ada = _adamw_ada(w_ada[0], c_act.T, dm_cols, m_w_ada[0], v_w_ada[0])
    (r_in,) = _exchange_wait("grad_in_wait", *x_in[:4], g_ada)
    g_in, d_in, nm_in, nv_in = [a.T for a in _adamw_slabs("adamw_w_in", w_in_t, r_in, m_in_t, v_in_t)]

    o = 0
    seg = {}
    for name, n in (("b_ada", 12288), ("g_pre_mix", 2048), ("g_post_mix", 2048), ("g_fox_out", 1024),
                    ("g_pre_mlp", 2048), ("g_post_mlp", 2048), ("b_gla_a2", 512), ("w_gla_a2", 8192),
                    ("g_gla_out", 1024), ("b_fgate", 128)):
        seg[name] = tot[:, o:o + n]
        o += n
    g_wa2 = lax.dynamic_slice_in_dim(seg["w_gla_a2"].reshape(GLA_RANK, 512), rank * 64, 64, axis=1)
    g_ggla = lax.dynamic_slice_in_dim(seg["g_gla_out"].reshape(GLA_HEADS, GLA_DV), rank * 32, 32, axis=1)
    small_names = ["b_ada", "g_pre_mix", "g_post_mix", "g_fox_out", "g_pre_mlp", "g_post_mlp", "b_gla_a2",
                   "w_gla_a2", "g_gla_out", "b_fgate"]
    small_grads = {**seg, "w_gla_a2": g_wa2.reshape(1, 1024), "g_gla_out": g_ggla.reshape(1, 128)}
    weights = dict(b_ada=b_ada, g_pre_mix=g_pre_mix, g_post_mix=g_post_mix, g_fox_out=g_fox_out,
                   g_pre_mlp=g_pre_mlp, g_post_mlp=g_post_mlp, b_gla_a2=b_gla_a2, w_gla_a2=w_gla_a2,
                   g_gla_out=g_gla_out, b_fgate=b_fgate)
    moms = dict(b_ada=m_b_ada, g_pre_mix=m_g_pre_mix, g_post_mix=m_g_post_mix, g_fox_out=m_g_fox_out,
                g_pre_mlp=m_g_pre_mlp, g_post_mlp=m_g_post_mlp, b_gla_a2=m_b_gla_a2, w_gla_a2=m_w_gla_a2,
                g_gla_out=m_g_gla_out, b_fgate=m_b_fgate)
    vels = dict(b_ada=v_b_ada, g_pre_mix=v_g_pre_mix, g_post_mix=v_g_post_mix, g_fox_out=v_g_fox_out,
                g_pre_mlp=v_g_pre_mlp, g_post_mlp=v_g_post_mlp, b_gla_a2=v_b_gla_a2, w_gla_a2=v_w_gla_a2,
                g_gla_out=v_g_gla_out, b_fgate=v_b_fgate)

    def flatten(d, fill):
        parts = []
        for nm in small_names:
            p = d[nm].reshape(1, -1)
            if nm == "b_fgate":
                p = jnp.concatenate([p[:, :FOX_HEADS], jnp.full((1, 128 - FOX_HEADS), fill, F32)], axis=1)
            parts.append(p)
        return jnp.concatenate(parts, axis=1).reshape(-1, 128)

    fw, fg, fm, fv = flatten(weights, 0.0), flatten(small_grads, 0.0), flatten(moms, 0.0), flatten(vels, 1.0)
    fd, fnm, fnv = _adamw_flat(fw, fg, fm, fv)

    def unflatten(fl):
        fl = fl.reshape(1, -1)
        out = {}
        o = 0
        for nm in small_names:
            n = 128 if nm == "b_fgate" else weights[nm].size
            piece = fl[:, o:o + n]
            if nm == "b_fgate":
                piece = piece[:, :FOX_HEADS]
            out[nm] = piece.reshape(weights[nm].shape)
            o += n
        return out

    sg, sd, snm, snv = unflatten(fg), unflatten(fd), unflatten(fnm), unflatten(fnv)

    big = dict(w_ada=(g_ada, d_ada, nm_ada, nv_ada), w_in=(g_in, d_in, nm_in, nv_in),
               w_out=(g_out, d_out, nm_out, nv_out), w_mlp_in=(g_mi, d_mi, nm_mi, nv_mi),
               w_mlp_out=(g_mo, d_mo, nm_mo, nv_mo))
    order = ["w_ada", "b_ada", "g_pre_mix", "g_post_mix", "w_in", "b_fgate", "w_gla_a2", "b_gla_a2", "g_fox_out",
             "g_gla_out", "w_out", "g_pre_mlp", "g_post_mlp", "w_mlp_in", "w_mlp_out"]

    def pick(nm, idx):
        if nm in big:
            return big[nm][idx][None]
        return (sg, sd, snm, snv)[idx][nm]

    grads = [pick(nm, 0) for nm in order]
    deltas = [pick(nm, 1) for nm in order]
    new_m = [pick(nm, 2) for nm in order]
    new_v = [pick(nm, 3) for nm in order]
    return (loss, grad_x[None], *grads, *deltas, *new_m, *new_v)
```

```python
import functools

import numpy as np
import jax
import jax.numpy as jnp
from jax import lax
from jax.experimental import pallas as pl
from jax.experimental.pallas import tpu as pltpu

F32 = jnp.float32
BF16 = jnp.bfloat16
MESH = pl.DeviceIdType.MESH
N_DEV = 8

D_MODEL = 2048
FOX_HEADS = 8
FOX_HEAD_DIM = 128
GLA_HEADS = 4
GLA_DK = 128
GLA_DV = 256
GLA_RANK = 16
GLA_TEMP = 16.0
CHUNK = 64
D_FF = 8192
W_MAIN = 6144
W_SMALL = 128
EPS = 1e-6
NEG = float(np.finfo(np.float32).min)

ADAM_LR = 0.001
ADAM_B1 = 0.9
ADAM_B2 = 0.999
ADAM_EPS = 1e-08
ADAM_WD = 0.01
ADAM_STEP = 10

ROW_T = 256
FOX_T = 1024
GLA_R = 512
CUM_T = 256
VMEM_LIMIT = 56 * 1024 * 1024


def _call(body, deps=(), **kw):
    if not deps:
        return pl.pallas_call(body, **kw)
    n_in, n_dep = len(kw["in_specs"]), len(deps)

    def with_deps(*refs):
        return body(*refs[:n_in], *refs[n_in + n_dep:])

    kw["in_specs"] = [*kw["in_specs"], *[pl.BlockSpec(memory_space=pl.ANY)] * n_dep]
    call = pl.pallas_call(with_deps, **kw)
    return lambda *args: call(*args, *deps)


def _params(sem=None):
    return pltpu.CompilerParams(dimension_semantics=sem, vmem_limit_bytes=VMEM_LIMIT)


def _my_pos():
    return lax.axis_index("x"), lax.axis_index("y"), lax.axis_index("c")


def _my_rank():
    x, y, c = _my_pos()
    return 4 * x + 2 * y + c


def _all_gather(name, arrays, deps=()):
    n = len(arrays)

    def body(*refs):
        ins = refs[:n]
        outs = refs[n:2 * n]
        send_sems, recv_sems, local_sems = refs[2 * n:]
        x, y, c = _my_pos()
        me, sibling = (x, y, c), (x, y, 1 - c)
        chips = [(1 - x, y), (x, 1 - y), (1 - x, 1 - y)]

        def slot(a, px, py, pc):
            return outs[a].at[4 * px + 2 * py + pc]

        def copy(a, k, block, to, src=None):
            return pltpu.make_async_remote_copy(
                src_ref=slot(a, *block) if src is None else src, dst_ref=slot(a, *block),
                send_sem=send_sems.at[a, k], recv_sem=recv_sems.at[a, k],
                device_id=to, device_id_type=MESH)

        started = []
        for a in range(n):
            mine = pltpu.make_async_copy(ins[a], slot(a, *me), local_sems.at[a])
            mine.start()
            started.append(mine)
        first = []
        for a in range(n):
            first.append(copy(a, 0, me, sibling, src=ins[a]))
            first += [copy(a, 1 + j, me, (*chip, c), src=ins[a]) for j, chip in enumerate(chips)]
        for cp in first:
            cp.start()
        passed = []
        for j, chip in enumerate(chips):
            for a in range(n):
                copy(a, 1 + j, (*chip, c), me).wait_recv()
                fwd = copy(a, 4 + j, (*chip, c), sibling)
                fwd.start()
                passed.append(fwd)
        for a in range(n):
            copy(a, 0, sibling, me).wait_recv()
            for j, chip in enumerate(chips):
                copy(a, 4 + j, (*chip, 1 - c), me).wait_recv()
        for cp in first + passed:
            cp.wait_send()
        for mine in started:
            mine.wait()

    hbm = pl.BlockSpec(memory_space=pltpu.HBM)
    return _call(
        body, deps=deps, name=name,
        out_shape=[jax.ShapeDtypeStruct((N_DEV,) + a.shape, a.dtype) for a in arrays],
        in_specs=[hbm] * n, out_specs=[hbm] * n,
        scratch_shapes=[pltpu.SemaphoreType.DMA((n, 7)), pltpu.SemaphoreType.DMA((n, 7)),
                        pltpu.SemaphoreType.DMA((n,))],
    )(*arrays)


def _plane_tiles(rr, cc, tr=512, tc=512):
    if rr % 8 == 0:
        tr = _tile(rr, tr)
        return (rr // tr, pl.BlockSpec((tr, cc), lambda i: (i, 0)),
                pl.BlockSpec((N_DEV, tr, cc), lambda i: (0, i, 0)))
    tc = _tile(cc, tc)
    return (cc // tc, pl.BlockSpec((rr, tc), lambda i: (0, i)),
            pl.BlockSpec((N_DEV, rr, tc), lambda i: (0, 0, i)))


def _own_slot(name, src, gather, rank):
    shape = ((N_DEV,) + src.shape) if gather else src.shape
    rr, cc = shape[1], shape[2]
    by_rows = rr % 8 == 0
    tr, tc = (_tile(rr, 512), cc) if by_rows else (rr, _tile(cc, 512))
    steps = rr // tr if by_rows else cc // tc

    def body(rank_ref, s_ref, o_ref):
        o_ref[...] = s_ref[...].astype(o_ref.dtype)

    def at(i):
        return (i, 0) if by_rows else (0, i)

    if gather:
        in_spec = pl.BlockSpec((tr, tc), lambda i, rk: at(i))
    else:
        in_spec = pl.BlockSpec((None, tr, tc), lambda i, rk: (rk[0], *at(i)))
    grid_spec = pltpu.PrefetchScalarGridSpec(
        num_scalar_prefetch=1, grid=(steps,), in_specs=[in_spec],
        out_specs=pl.BlockSpec((None, tr, tc), lambda i, rk: (rk[0], *at(i))))
    return _call(body, name=name, grid_spec=grid_spec, out_shape=jax.ShapeDtypeStruct(shape, BF16),
                 compiler_params=_params(("arbitrary",)))(jnp.reshape(rank, (1,)).astype(jnp.int32), src)


_HBM = pl.BlockSpec(memory_space=pltpu.HBM)
_SEM = pl.BlockSpec(memory_space=pltpu.SEMAPHORE)
_FLIPS = [(kx, ky, kc) for kx in (0, 1) for ky in (0, 1) for kc in (0, 1)][1:]


def _peers():
    x, y, c = _my_pos()
    out = []
    for kx, ky, kc in _FLIPS:
        px, py, pc = (1 - x if kx else x), (1 - y if ky else y), (1 - c if kc else c)
        out.append(((px, py, pc), 4 * px + 2 * py + pc))
    return out


def _exchange_copy(srcs, lands, send_sems, recv_sems, a, k, peer, peer_rank, slot):
    return pltpu.make_async_remote_copy(
        src_ref=lands[a].at[slot] if srcs is None else srcs[a].at[peer_rank],
        dst_ref=lands[a].at[slot],
        send_sem=send_sems[a].at[k], recv_sem=recv_sems[a].at[k],
        device_id=peer, device_id_type=MESH)


def _exchange_start(name, lands, srcs=None, after=()):
    n = len(lands)
    n_src = 0 if srcs is None else n
    n_in = n + n_src + len(after)

    def body(*refs):
        lnd = refs[:n]
        src = None if srcs is None else refs[n:2 * n]
        send_sems, recv_sems = refs[n_in:n_in + n], refs[n_in + n:n_in + 2 * n]
        token = refs[-1]
        me = _my_rank()
        for a in range(n):
            for k, (peer, peer_rank) in enumerate(_peers()):
                _exchange_copy(src, lnd, send_sems, recv_sems, a, k, peer, peer_rank, me).start()
        token[...] = jnp.zeros_like(token)

    sems = [pltpu.SemaphoreType.DMA((7,))] * (2 * n)
    thru = list(lands) + ([] if srcs is None else list(srcs))
    outs = pl.pallas_call(
        body, name=name,
        out_shape=(*sems, *[pltpu.HBM(t.shape, t.dtype) for t in thru], jax.ShapeDtypeStruct((8, 128), F32)),
        in_specs=[*[_HBM] * len(thru), *[pl.BlockSpec(memory_space=pl.ANY)] * len(after)],
        out_specs=(*[_SEM] * (2 * n), *[_HBM] * len(thru), pl.BlockSpec(memory_space=pltpu.VMEM)),
        input_output_aliases={i: 2 * n + i for i in range(len(thru))},
        compiler_params=pltpu.CompilerParams(has_side_effects=pltpu.SideEffectType.DATAFLOW_SIDE_EFFECTING),
    )(*[pltpu.with_memory_space_constraint(t, pltpu.HBM) for t in thru], *after)
    lands_thru = outs[2 * n:3 * n]
    srcs_thru = None if srcs is None else outs[3 * n:4 * n]
    return outs[:n], outs[n:2 * n], srcs_thru, lands_thru, outs[-1]


def _exchange_wait(name, send_sems, recv_sems, srcs, lands, after):
    n = len(lands)
    thru = list(lands) + ([] if srcs is None else list(srcs))

    def body(*refs):
        lnd = refs[:n]
        src = None if srcs is None else refs[n:2 * n]
        ssem, rsem = refs[len(thru):len(thru) + n], refs[len(thru) + n:len(thru) + 2 * n]
        for a in range(n):
            for k, (peer, peer_rank) in enumerate(_peers()):
                cp = _exchange_copy(src, lnd, ssem, rsem, a, k, peer, peer_rank, peer_rank)
                cp.wait_send()
                cp.wait_recv()

    outs = pl.pallas_call(
        body, name=name,
        out_shape=tuple(pltpu.HBM(t.shape, t.dtype) for t in thru),
        in_specs=[*[_HBM] * len(thru), *[_SEM] * (2 * n), pl.BlockSpec(memory_space=pl.ANY)],
        out_specs=tuple([_HBM] * len(thru)),
        input_output_aliases={i: i for i in range(len(thru))},
        compiler_params=pltpu.CompilerParams(has_side_effects=pltpu.SideEffectType.DATAFLOW_SIDE_EFFECTING),
    )(*thru, *send_sems, *recv_sems, after)
    return outs[:n]


NN = ((1,), (0,))
NT = ((1,), (1,))
TN = ((0,), (0,))


def _matmul(name, a, b, *, contract, grid, a_spec, b_spec, out_specs, out_shapes, acc_shape,
            extra=(), extra_specs=(), epilogue=None, deps=()):
    nk = grid[2]
    n_extra = len(extra)
    n_out = len(out_shapes)

    def body(*refs):
        a_ref, b_ref = refs[0], refs[1]
        extra_refs = refs[2:2 + n_extra]
        out_refs = refs[2 + n_extra:2 + n_extra + n_out]
        acc_ref = refs[-1]
        k = pl.program_id(2)

        def prod():
            return lax.dot_general(a_ref[...].astype(BF16), b_ref[...].astype(BF16), (contract, ((), ())),
                                   preferred_element_type=F32)

        def finish(acc):
            res = (acc,) if epilogue is None else epilogue(acc, *[r[...] for r in extra_refs])
            for o_ref, val in zip(out_refs, res):
                o_ref[...] = val.astype(o_ref.dtype)

        if nk == 1:
            finish(prod())
            return

        @pl.when(k == 0)
        def _():
            acc_ref[...] = prod()

        @pl.when((k > 0) & (k < nk - 1))
        def _():
            acc_ref[...] += prod()

        @pl.when(k == nk - 1)
        def _():
            finish(acc_ref[...] + prod())

    outs = _call(
        body, deps=deps, name=name, grid=grid,
        in_specs=[a_spec, b_spec, *extra_specs], out_specs=list(out_specs), out_shape=list(out_shapes),
        scratch_shapes=[pltpu.VMEM(acc_shape if nk > 1 else (8, 128), F32)],
        compiler_params=_params(("parallel", "parallel", "arbitrary")),
    )(a, b, *extra)
    return outs


def _tile(n, t):
    t = min(n, t)
    assert n % t == 0, (n, t)
    return t


def _mm_plain(name, a, b, contract, out_dtype, tm=1024, tn=1024, tk=2048, extra=(), epilogue=None,
              n_out=1, out_dtypes=None, deps=()):
    if contract == NN:
        (m, kd), (_, n) = a.shape, b.shape
    elif contract == NT:
        (m, kd), (n, _) = a.shape, b.shape
    else:
        (kd, m), (_, n) = a.shape, b.shape
    tm, tn, tk = _tile(m, tm), _tile(n, tn), _tile(kd, tk)
    if contract == NN:
        a_spec = pl.BlockSpec((tm, tk), lambda i, j, k: (i, k))
        b_spec = pl.BlockSpec((tk, tn), lambda i, j, k: (k, j))
    elif contract == NT:
        a_spec = pl.BlockSpec((tm, tk), lambda i, j, k: (i, k))
        b_spec = pl.BlockSpec((tn, tk), lambda i, j, k: (j, k))
    else:
        a_spec = pl.BlockSpec((tk, tm), lambda i, j, k: (k, i))
        b_spec = pl.BlockSpec((tk, tn), lambda i, j, k: (k, j))
    o_spec = pl.BlockSpec((tm, tn), lambda i, j, k: (i, j))
    out_dtypes = out_dtypes or [out_dtype] * n_out
    outs = _matmul(
        name, a, b, contract=contract, grid=(m // tm, n // tn, kd // tk), a_spec=a_spec, b_spec=b_spec,
        out_specs=[o_spec] * len(out_dtypes), out_shapes=[jax.ShapeDtypeStruct((m, n), dt) for dt in out_dtypes],
        acc_shape=(tm, tn), extra=extra, extra_specs=[o_spec] * len(extra), epilogue=epilogue, deps=deps)
    return outs[0] if len(out_dtypes) == 1 else outs


def _rows_call(name, body, row_in, vec_in, row_out, vec_out, s, deps=()):
    t = _tile(s, ROW_T)
    in_specs = []
    args = []
    for arr, width, cb in row_in:
        in_specs.append(pl.BlockSpec((t, width), functools.partial(lambda i, cb: (i, cb), cb=cb)))
        args.append(arr)
    for v in vec_in:
        in_specs.append(pl.BlockSpec(v.shape, lambda i: (0, 0)))
        args.append(v)
    out_specs = []
    out_shapes = []
    for width, dt in row_out:
        out_specs.append(pl.BlockSpec((t, width), lambda i: (i, 0)))
        out_shapes.append(jax.ShapeDtypeStruct((s, width), dt))
    for width in vec_out:
        out_specs.append(pl.BlockSpec((1, width), lambda i: (0, 0)))
        out_shapes.append(jax.ShapeDtypeStruct((1, width), F32))
    return _call(body, deps=deps, name=name, grid=(s // t,), in_specs=in_specs, out_specs=out_specs,
                 out_shape=out_shapes, compiler_params=_params(("arbitrary",)))(*args)


def _acc_vec(ref, val):
    _acc_row(ref, jnp.sum(val, axis=0, keepdims=True))


def _acc_row(ref, part):
    @pl.when(pl.program_id(0) == 0)
    def _():
        ref[...] = part

    @pl.when(pl.program_id(0) > 0)
    def _():
        ref[...] += part


def _rms(v):
    return lax.rsqrt(jnp.mean(v * v, axis=-1, keepdims=True) + EPS)


def _norm_bwd(dxn, xn, r):
    return r * (dxn - xn * jnp.mean(dxn * xn, axis=-1, keepdims=True))


def _premix(x, g, scale, shift, deps=()):
    s = x.shape[0]

    def body(x_ref, g_ref, sc_ref, sh_ref, h_ref):
        xv = x_ref[...]
        h_ref[...] = ((xv * _rms(xv) * g_ref[...]) * (1.0 + sc_ref[...]) + sh_ref[...]).astype(BF16)

    return _rows_call("premix", body, [(x, D_MODEL, 0)], [g, scale, shift], [(D_MODEL, BF16)], [], s, deps)[0]


def _sigmoid(z):
    return 1.0 / (1.0 + jnp.exp(-z))


def _mix_fwd(o_fox, o_gla, pm, g_fox, g_gla):
    s = o_fox.shape[0]

    def body(of_ref, og_ref, gr_ref, gf_ref, gg_ref, mix_ref):
        for h in range(FOX_HEADS):
            sl = slice(h * FOX_HEAD_DIM, (h + 1) * FOX_HEAD_DIM)
            seg = of_ref[:, sl]
            mix_ref[:, sl] = (seg * _rms(seg) * gf_ref[:, sl]).astype(BF16)
        for h in range(GLA_HEADS):
            sl = slice(h * GLA_DV, (h + 1) * GLA_DV)
            seg = og_ref[:, sl]
            gr = gr_ref[:, sl].astype(F32)
            val = (seg * _rms(seg) * gg_ref[:, sl]) * (gr * _sigmoid(gr))
            mix_ref[:, pl.ds(FOX_HEADS * FOX_HEAD_DIM + h * GLA_DV, GLA_DV)] = val.astype(BF16)

    return _rows_call("mix_fwd", body, [(o_fox, 1024, 0), (o_gla, 1024, 0), (pm, 1024, 5)], [g_fox, g_gla],
                      [(D_MODEL, BF16)], [], s)[0]


def _mix_bwd(dmix, o_fox, o_gla, pm, g_fox, g_gla, deps=()):
    s = o_fox.shape[0]

    def body(dm_ref, of_ref, og_ref, gr_ref, gf_ref, gg_ref, dof_ref, dog_ref, dgr_ref, dgf_ref, dgg_ref):
        dgf = []
        for h in range(FOX_HEADS):
            sl = slice(h * FOX_HEAD_DIM, (h + 1) * FOX_HEAD_DIM)
            seg = of_ref[:, sl]
            r = _rms(seg)
            segn = seg * r
            dout = dm_ref[:, sl]
            dgf.append(jnp.sum(dout * segn, axis=0, keepdims=True))
            dof_ref[:, sl] = _norm_bwd(dout * gf_ref[:, sl], segn, r).astype(BF16)
        dgg = []
        for h in range(GLA_HEADS):
            sl = slice(h * GLA_DV, (h + 1) * GLA_DV)
            seg = og_ref[:, sl]
            r = _rms(seg)
            segn = seg * r
            gl = segn * gg_ref[:, sl]
            gr = gr_ref[:, sl].astype(F32)
            sig = _sigmoid(gr)
            dout = dm_ref[:, pl.ds(FOX_HEADS * FOX_HEAD_DIM + h * GLA_DV, GLA_DV)]
            dgr_ref[:, sl] = (dout * gl * (sig * (1.0 + gr * (1.0 - sig)))).astype(BF16)
            dgl = dout * (gr * sig)
            dgg.append(jnp.sum(dgl * segn, axis=0, keepdims=True))
            dog_ref[:, sl] = _norm_bwd(dgl * gg_ref[:, sl], segn, r).astype(BF16)
        _acc_row(dgf_ref, jnp.concatenate(dgf, axis=1))
        _acc_row(dgg_ref, jnp.concatenate(dgg, axis=1))

    return _rows_call("mix_bwd", body, [(dmix, D_MODEL, 0), (o_fox, 1024, 0), (o_gla, 1024, 0), (pm, 1024, 5)],
                      [g_fox, g_gla], [(1024, BF16), (1024, BF16), (1024, BF16)], [1024, 1024], s, deps)


def _postmix_premlp(x, y, gate_m, g_post_mix, g_pre_mlp, scale_f, shift_f):
    s = x.shape[0]

    def body(x_ref, y_ref, gm_ref, gpm_ref, gpl_ref, sc_ref, sh_ref, x1_ref, h2_ref):
        yv = y_ref[...]
        x1 = x_ref[...] + gm_ref[...] * (yv * _rms(yv) * gpm_ref[...])
        x1_ref[...] = x1
        h2_ref[...] = ((x1 * _rms(x1) * gpl_ref[...]) * (1.0 + sc_ref[...]) + sh_ref[...]).astype(BF16)

    return _rows_call("postmix_premlp", body, [(x, D_MODEL, 0), (y, D_MODEL, 0)],
                      [gate_m, g_post_mix, g_pre_mlp, scale_f, shift_f], [(D_MODEL, F32), (D_MODEL, BF16)], [], s)


def _loss_postmlp_bwd(x1, y2, target, gate_f, g_post_mlp):
    s = x1.shape[0]

    def body(x1_ref, y2_ref, t_ref, gf_ref, g_ref, dx2_ref, dy2_ref, loss_ref, dgate_ref, dg_ref):
        yv = y2_ref[...]
        r = _rms(yv)
        yn = yv * r
        o = yn * g_ref[...]
        e = (x1_ref[...] + gf_ref[...] * o) - t_ref[...]
        part = 0.5 * jnp.sum(jnp.mean(e * e, axis=-1, keepdims=True), axis=0, keepdims=True)
        _acc_vec(loss_ref, jnp.broadcast_to(part, (1, 128)))
        dx2 = e * (1.0 / D_MODEL)
        dx2_ref[...] = dx2
        _acc_vec(dgate_ref, dx2 * o)
        do = dx2 * gf_ref[...]
        _acc_vec(dg_ref, do * yn)
        dy2_ref[...] = _norm_bwd(do * g_ref[...], yn, r).astype(BF16)

    return _rows_call("loss_postmlp_bwd", body, [(x1, D_MODEL, 0), (y2, D_MODEL, 0), (target, D_MODEL, 0)],
                      [gate_f, g_post_mlp], [(D_MODEL, F32), (D_MODEL, BF16)], [128, D_MODEL, D_MODEL], s)


def _premlp_postmix_bwd(dh2, dx2, x1, y, scale_f, g_pre_mlp, gate_m, g_post_mix, deps=()):
    s = x1.shape[0]

    def body(dh2_ref, dx2_ref, x1_ref, y_ref, sc_ref, gpl_ref, gm_ref, gpm_ref,
             dx1_ref, dy_ref, dsc_ref, dsh_ref, dgpl_ref, dgm_ref, dgpm_ref):
        x1 = x1_ref[...]
        r1 = _rms(x1)
        x1n = x1 * r1
        dh2 = dh2_ref[...]
        _acc_vec(dsc_ref, dh2 * (x1n * gpl_ref[...]))
        _acc_vec(dsh_ref, dh2)
        dn2 = dh2 * (1.0 + sc_ref[...])
        _acc_vec(dgpl_ref, dn2 * x1n)
        dx1 = dx2_ref[...] + _norm_bwd(dn2 * gpl_ref[...], x1n, r1)
        dx1_ref[...] = dx1
        yv = y_ref[...]
        ry = _rms(yv)
        yn = yv * ry
        _acc_vec(dgm_ref, dx1 * (yn * gpm_ref[...]))
        do = dx1 * gm_ref[...]
        _acc_vec(dgpm_ref, do * yn)
        dy_ref[...] = _norm_bwd(do * gpm_ref[...], yn, ry).astype(BF16)

    return _rows_call("premlp_postmix_bwd", body,
                      [(dh2, D_MODEL, 0), (dx2, D_MODEL, 0), (x1, D_MODEL, 0), (y, D_MODEL, 0)],
                      [scale_f, g_pre_mlp, gate_m, g_post_mix], [(D_MODEL, F32), (D_MODEL, BF16)],
                      [D_MODEL] * 5, s, deps)


def _premix_bwd(dh, dx1, x, g_pre_mix, scale_m):
    s = x.shape[0]

    def body(dh_ref, dx1_ref, x_ref, g_ref, sc_ref, gx_ref, dsc_ref, dsh_ref, dg_ref):
        xv = x_ref[...]
        r = _rms(xv)
        xn = xv * r
        dh = dh_ref[...]
        _acc_vec(dsc_ref, dh * (xn * g_ref[...]))
        _acc_vec(dsh_ref, dh)
        dn1 = dh * (1.0 + sc_ref[...])
        _acc_vec(dg_ref, dn1 * xn)
        gx_ref[...] = dx1_ref[...] + _norm_bwd(dn1 * g_ref[...], xn, r)

    return _rows_call("premix_bwd", body, [(dh, D_MODEL, 0), (dx1, D_MODEL, 0), (x, D_MODEL, 0)],
                      [g_pre_mix, scale_m], [(D_MODEL, F32)], [D_MODEL] * 3, s)


def _split3(v):
    hi = v.astype(BF16)
    r1 = v - hi.astype(F32)
    mid = r1.astype(BF16)
    lo = (r1 - mid.astype(F32)).astype(BF16)
    return hi, mid, lo


def _dot_exact01(v, tri, contract=NN, tri_first=False):
    acc = None
    for part in _split3(v):
        lhs, rhs = (tri, part) if tri_first else (part, tri)
        p = lax.dot_general(lhs, rhs, (contract, ((), ())), preferred_element_type=F32)
        acc = p if acc is None else acc + p
    return acc


def _log_sigmoid(z):
    return jnp.minimum(z, 0.0) - jnp.log(1.0 + jnp.exp(-jnp.abs(z)))


def _fox_cum(small, bvec):
    s = small.shape[0]
    t = _tile(s, CUM_T)

    def body(sm_ref, b_ref, out_ref, carry):
        @pl.when(pl.program_id(0) == 0)
        def _():
            carry[...] = jnp.zeros_like(carry)

        lf = _log_sigmoid(sm_ref[...] + b_ref[...])
        lft = lf.T[0:FOX_HEADS, :]
        row = lax.broadcasted_iota(jnp.int32, (t, t), 0)
        col = lax.broadcasted_iota(jnp.int32, (t, t), 1)
        upper = (row <= col).astype(BF16)
        cum = _dot_exact01(lft, upper) + carry[:, 0:1]
        out_ref[...] = cum
        carry[...] = carry[...] + jnp.sum(lft, axis=1, keepdims=True)

    return _call(body, name="fox_cum", grid=(s // t,),
                 in_specs=[pl.BlockSpec((t, W_SMALL), lambda i: (i, 0)), pl.BlockSpec((1, W_SMALL), lambda i: (0, 0))],
                 out_specs=pl.BlockSpec((FOX_HEADS, t), lambda i: (0, i)),
                 out_shape=jax.ShapeDtypeStruct((FOX_HEADS, s), F32),
                 scratch_shapes=[pltpu.VMEM((FOX_HEADS, 128), F32)],
                 compiler_params=_params(("arbitrary",)))(small, bvec)


def _fox_cum_bwd(dc, dcq, small, bvec):
    s = small.shape[0]
    t = _tile(s, CUM_T)
    nb = s // t

    def body(dc_ref, dcq_ref, sm_ref, b_ref, out_ref, db_ref, carry):
        @pl.when(pl.program_id(0) == 0)
        def _():
            carry[...] = jnp.zeros_like(carry)
            db_ref[...] = jnp.zeros_like(db_ref)

        lane = lax.broadcasted_iota(jnp.int32, (t, W_SMALL), 1)
        dcq = jnp.zeros((t, W_SMALL), F32)
        for hh in range(FOX_HEADS):
            dcq = jnp.where(lane == hh, dcq_ref[hh], dcq)
        dcv = dc_ref[...] + dcq.T[0:FOX_HEADS, :]
        row = lax.broadcasted_iota(jnp.int32, (t, t), 0)
        col = lax.broadcasted_iota(jnp.int32, (t, t), 1)
        lower = (row >= col).astype(BF16)
        dlf = _dot_exact01(dcv, lower) + carry[:, 0:1]
        carry[...] = carry[...] + jnp.sum(dcv, axis=1, keepdims=True)
        z = sm_ref[...] + b_ref[...]
        zt = z.T[0:FOX_HEADS, :]
        dff = dlf * _sigmoid(-zt)
        db_ref[...] = db_ref[...] + jnp.sum(dff, axis=1, keepdims=True)
        full = jnp.concatenate([dff, jnp.zeros((W_SMALL - FOX_HEADS, t), F32)], axis=0)
        out_ref[...] = full.T

    return _call(body, name="fox_cum_bwd", grid=(nb,),
                 in_specs=[pl.BlockSpec((FOX_HEADS, t), lambda i: (0, nb - 1 - i)),
                           pl.BlockSpec((FOX_HEADS, t, 1), lambda i: (0, nb - 1 - i, 0)),
                           pl.BlockSpec((t, W_SMALL), lambda i: (nb - 1 - i, 0)),
                           pl.BlockSpec((1, W_SMALL), lambda i: (0, 0))],
                 out_specs=[pl.BlockSpec((t, W_SMALL), lambda i: (nb - 1 - i, 0)),
                            pl.BlockSpec((FOX_HEADS, 128), lambda i: (0, 0))],
                 out_shape=[jax.ShapeDtypeStruct((s, W_SMALL), F32), jax.ShapeDtypeStruct((FOX_HEADS, 128), F32)],
                 scratch_shapes=[pltpu.VMEM((FOX_HEADS, 128), F32)],
                 compiler_params=_params(("arbitrary",)))(dc, dcq, small, bvec)


FOX_SCALE = FOX_HEAD_DIM ** -0.5


def _fox_fwd(pm, crow):
    s = pm.shape[0]
    t = _tile(s, FOX_T)
    nb = s // t
    parts = 2
    hq = t // parts

    def body(q_ref, k_ref, v_ref, c_ref, o_ref, lse_ref):
        i = pl.program_id(1)
        qs = [q_ref[g * hq:(g + 1) * hq, :] for g in range(parts)]

        def block(j, carry, diagonal):
            rows = pl.ds(pl.multiple_of(j * t, t), t)
            k_all, v_all, c_all = k_ref[rows, :], v_ref[rows, :], c_ref[j]
            out = []
            for g, (m_prev, l_prev, acc) in enumerate(carry):
                nk = (g + 1) * hq if diagonal else t
                kb, vb, cb = k_all[:nk], v_all[:nk], c_all[:, :nk]
                sc = lax.dot_general(qs[g], kb, (NT, ((), ())), preferred_element_type=F32)
                sc = sc * FOX_SCALE - cb
                if diagonal:
                    row = lax.broadcasted_iota(jnp.int32, (hq, nk), 0) + g * hq
                    col = lax.broadcasted_iota(jnp.int32, (hq, nk), 1)
                    sc = jnp.where(row >= col, sc, NEG)
                m_new = jnp.maximum(m_prev, jnp.max(sc, axis=1, keepdims=True))
                alpha = jnp.exp(m_prev - m_new)
                p = jnp.exp(sc - m_new)
                l_new = alpha * l_prev + jnp.sum(p, axis=1, keepdims=True)
                pv = jnp.dot(p.astype(BF16), vb, preferred_element_type=F32)
                out.append((m_new, l_new, alpha * acc + pv))
            return tuple(out)

        init = tuple((jnp.full((hq, 1), NEG, F32), jnp.zeros((hq, 1), F32), jnp.zeros((hq, 128), F32))
                     for _ in range(parts))
        carry = lax.fori_loop(0, i, lambda j, cr: block(j, cr, False), init)
        carry = block(i, carry, True)
        for g, (m_fin, l_fin, acc) in enumerate(carry):
            o_ref[g * hq:(g + 1) * hq, :] = acc / l_fin
            lse_ref[g * hq:(g + 1) * hq, :] = m_fin + jnp.log(l_fin)

    return _call(
        body, name="fox_fwd", grid=(FOX_HEADS, nb),
        in_specs=[pl.BlockSpec((t, 128), lambda h, i: (i, h)),
                  pl.BlockSpec((s, 128), lambda h, i: (0, FOX_HEADS + h)),
                  pl.BlockSpec((s, 128), lambda h, i: (0, 2 * FOX_HEADS + h)),
                  pl.BlockSpec((None, nb, 1, t), lambda h, i: (h, 0, 0, 0))],
        out_specs=[pl.BlockSpec((t, 128), lambda h, i: (i, h)),
                   pl.BlockSpec((None, t, 1), lambda h, i: (h, i, 0))],
        out_shape=[jax.ShapeDtypeStruct((s, FOX_HEADS * 128), F32), jax.ShapeDtypeStruct((FOX_HEADS, s, 1), F32)],
        compiler_params=_params(("parallel", "arbitrary")),
    )(pm, pm, pm, crow.reshape(FOX_HEADS, nb, 1, t))


def _fox_bwd(pm, crow, o, lse, do):
    s = pm.shape[0]
    t = _tile(s, FOX_T)
    nb = s // t

    parts = 2
    hq = t // parts

    def body(q_ref, do_ref, o_ref, lse_ref, k_ref, v_ref, c_ref, dq_ref, dk_ref, dv_ref, dc_ref, dcq_ref, delta_s):
        j = pl.program_id(1)

        @pl.when(j == 0)
        def _():
            dq_ref[...] = jnp.zeros_like(dq_ref)
            dcq_ref[...] = jnp.zeros_like(dcq_ref)
            delta_s[...] = jnp.sum(do_ref[...].astype(F32) * o_ref[...], axis=1, keepdims=True)

        k_all, v_all, c_all = k_ref[...], v_ref[...], c_ref[...]

        def grow(acc, part, axis):
            n = part.shape[axis]
            if n == acc.shape[axis]:
                return acc + part
            if axis == 0:
                return jnp.concatenate([acc[:n] + part, acc[n:]], axis=0)
            return jnp.concatenate([acc[:, :n] + part, acc[:, n:]], axis=1)

        def block(i, carry, diagonal):
            dk_acc, dv_acc, dc_acc = carry
            for g in range(parts):
                nk = (g + 1) * hq if diagonal else t
                kb, vb, cb = k_all[:nk], v_all[:nk], c_all[:, :nk]
                rows = pl.ds(pl.multiple_of(i * t + g * hq, hq), hq)
                q, dov = q_ref[rows, :], do_ref[rows, :]
                sc = lax.dot_general(q, kb, (NT, ((), ())), preferred_element_type=F32)
                p = jnp.exp(sc * FOX_SCALE - cb - lse_ref[rows, :])
                if diagonal:
                    row = lax.broadcasted_iota(jnp.int32, (hq, nk), 0) + g * hq
                    col = lax.broadcasted_iota(jnp.int32, (hq, nk), 1)
                    p = jnp.where(row >= col, p, 0.0)
                dp = lax.dot_general(dov, vb, (NT, ((), ())), preferred_element_type=F32)
                ds = p * (dp - delta_s[rows, :])
                dsb = ds.astype(BF16)
                dv_acc = grow(dv_acc, lax.dot_general(p.astype(BF16), dov, (TN, ((), ())),
                                                      preferred_element_type=F32), 0)
                dk_acc = grow(dk_acc, lax.dot_general(dsb, q, (TN, ((), ())), preferred_element_type=F32), 0)
                dq_ref[rows, :] += jnp.dot(dsb, kb, preferred_element_type=F32) * FOX_SCALE
                dc_acc = grow(dc_acc, -jnp.sum(ds, axis=0, keepdims=True), 1)
                dcq_ref[rows, :] += jnp.sum(ds, axis=1, keepdims=True)
            return dk_acc, dv_acc, dc_acc

        carry = (jnp.zeros((t, 128), F32), jnp.zeros((t, 128), F32), jnp.zeros((1, t), F32))
        carry = block(j, carry, True)
        dk_acc, dv_acc, dc_acc = lax.fori_loop(j + 1, nb, lambda i, cr: block(i, cr, False), carry)
        dk_ref[...] = dk_acc * FOX_SCALE
        dv_ref[...] = dv_acc
        dc_ref[...] = dc_acc

    whole = lambda h, j: (0, h)
    return _call(
        body, name="fox_bwd", grid=(FOX_HEADS, nb),
        in_specs=[pl.BlockSpec((s, 128), whole), pl.BlockSpec((s, 128), whole), pl.BlockSpec((s, 128), whole),
                  pl.BlockSpec((None, s, 1), lambda h, j: (h, 0, 0)),
                  pl.BlockSpec((t, 128), lambda h, j: (j, FOX_HEADS + h)),
                  pl.BlockSpec((t, 128), lambda h, j: (j, 2 * FOX_HEADS + h)),
                  pl.BlockSpec((None, 1, t), lambda h, j: (h, 0, j))],
        out_specs=[pl.BlockSpec((s, 128), whole),
                   pl.BlockSpec((t, 128), lambda h, j: (j, h)),
                   pl.BlockSpec((t, 128), lambda h, j: (j, h)),
                   pl.BlockSpec((None, 1, t), lambda h, j: (h, 0, j)),
                   pl.BlockSpec((None, s, 1), lambda h, j: (h, 0, 0))],
        out_shape=[jax.ShapeDtypeStruct((s, 1024), F32), jax.ShapeDtypeStruct((s, 1024), F32),
                   jax.ShapeDtypeStruct((s, 1024), F32), jax.ShapeDtypeStruct((FOX_HEADS, 1, s), F32),
                   jax.ShapeDtypeStruct((FOX_HEADS, s, 1), F32)],
        scratch_shapes=[pltpu.VMEM((s, 1), F32)],
        compiler_params=_params(("parallel", "arbitrary")),
    )(pm, do, o, lse, pm, pm, crow)


GLA_SCALE = GLA_DK ** -0.5
GLA_Q_BLK = 3072 // 128
GLA_K_BLK = 3584 // 128
GLA_V_BLK = 4096 // 256


def _gla_gate(sm, wa_ref, b_ref):
    return jnp.dot(sm.astype(BF16), wa_ref[...], preferred_element_type=F32) + b_ref[...]


def _chunk_tri(n, kind):
    row = lax.broadcasted_iota(jnp.int32, (n, n), 0)
    col = lax.broadcasted_iota(jnp.int32, (n, n), 1)
    shift = CHUNK.bit_length() - 1
    same = (row >> shift) == (col >> shift)
    if kind == "upto":
        same = same & (row >= col)
    elif kind == "before":
        same = same & (row > col)
    return same.astype(BF16)


def _gla_fwd(pm, small, wa_pad, b_a2):
    s = pm.shape[0]
    r = _tile(s, GLA_R)
    nc = r // CHUNK

    def body(q_ref, k_ref, v_ref, sm_ref, wa_ref, b_ref, o_ref, st_ref, state):
        @pl.when(pl.program_id(1) == 0)
        def _():
            state[...] = jnp.zeros_like(state)

        la_all = _log_sigmoid(_gla_gate(sm_ref[...], wa_ref, b_ref)) * (1.0 / GLA_TEMP)
        tri = _chunk_tri(CHUNK, "upto")
        uts, decays = [], []
        for c in range(nc):
            rows = slice(c * CHUNK, (c + 1) * CHUNK)
            la = la_all[rows]
            cum = _dot_exact01(la, tri, tri_first=True)
            total = jnp.sum(la, axis=0, keepdims=True)
            kdec = k_ref[rows, :].astype(F32) * jnp.exp(total - cum)
            uts.append(lax.dot_general(v_ref[rows, :], kdec.astype(BF16), (TN, ((), ())),
                                       preferred_element_type=F32))
            decays.append(jnp.exp(total))
        cur = state[...]
        ends = []
        for c in range(nc):
            cur = cur * decays[c] + uts[c]
            ends.append(cur.astype(BF16))
        state[...] = cur
        for c in range(nc):
            rows = slice(c * CHUNK, (c + 1) * CHUNK)
            st_ref[c] = ends[c]
            qs = (q_ref[rows, :].astype(F32) * GLA_SCALE).astype(BF16)
            o_ref[rows, :] = lax.dot_general(qs, ends[c], (NT, ((), ())), preferred_element_type=F32)

    return _call(
        body, name="gla_fwd", grid=(GLA_HEADS, s // r),
        in_specs=[pl.BlockSpec((r, 128), lambda h, i: (i, GLA_Q_BLK + h)),
                  pl.BlockSpec((r, 128), lambda h, i: (i, GLA_K_BLK + h)),
                  pl.BlockSpec((r, 256), lambda h, i: (i, GLA_V_BLK + h)),
                  pl.BlockSpec((r, W_SMALL), lambda h, i: (i, 0)),
                  pl.BlockSpec((W_SMALL, 128), lambda h, i: (0, h)),
                  pl.BlockSpec((1, 128), lambda h, i: (0, h))],
        out_specs=[pl.BlockSpec((r, 256), lambda h, i: (i, h)),
                   pl.BlockSpec((nc, None, GLA_DV, GLA_DK), lambda h, i: (i, h, 0, 0))],
        out_shape=[jax.ShapeDtypeStruct((s, 1024), F32),
                   jax.ShapeDtypeStruct((s // CHUNK, GLA_HEADS, GLA_DV, GLA_DK), BF16)],
        scratch_shapes=[pltpu.VMEM((GLA_DV, GLA_DK), F32)],
        compiler_params=_params(("parallel", "arbitrary")),
    )(pm, pm, pm, small, wa_pad, b_a2)


def _gla_bwd(pm, small, wa_pad, b_a2, states, do):
    s = pm.shape[0]
    r = _tile(s, GLA_R)
    nc = r // CHUNK
    nb = s // r

    def body(q_ref, k_ref, v_ref, sm_ref, wa_ref, b_ref, do_ref, st_ref, prev_ref,
             dq_ref, dk_ref, dv_ref, dza_ref, db_ref, carry):
        step = pl.program_id(1)

        @pl.when(step == 0)
        def _():
            carry[...] = jnp.zeros_like(carry)
            db_ref[...] = jnp.zeros_like(db_ref)

        z_all = _gla_gate(sm_ref[...], wa_ref, b_ref)
        la_all = _log_sigmoid(z_all) * (1.0 / GLA_TEMP)
        tri = _chunk_tri(CHUNK, "upto")
        tri_strict = _chunk_tri(CHUNK, "before")
        ws, decays, kdecs, gouts = [], [], [], []
        for c in range(nc):
            rows = slice(c * CHUNK, (c + 1) * CHUNK)
            la = la_all[rows]
            cum = _dot_exact01(la, tri, tri_first=True)
            total = jnp.sum(la, axis=0, keepdims=True)
            w = jnp.exp(total - cum)
            ws.append(w)
            decays.append(jnp.exp(total))
            kdecs.append(k_ref[rows, :].astype(F32) * w)
            dov = do_ref[rows, :]
            qs = (q_ref[rows, :].astype(F32) * GLA_SCALE).astype(BF16)
            dq_ref[rows, :] = jnp.dot(dov, st_ref[c], preferred_element_type=F32) * GLA_SCALE
            gouts.append(lax.dot_general(dov, qs, (TN, ((), ())), preferred_element_type=F32))
        cur = carry[...]
        gts = [None] * nc
        for c in reversed(range(nc)):
            gts[c] = gouts[c] + cur
            cur = gts[c] * decays[c]
        carry[...] = cur
        db = jnp.zeros((1, 128), F32)
        for c in range(nc):
            rows = slice(c * CHUNK, (c + 1) * CHUNK)
            gtb = gts[c].astype(BF16)
            dv_ref[rows, :] = lax.dot_general(kdecs[c].astype(BF16), gtb, (NT, ((), ())),
                                              preferred_element_type=F32)
            dkdec = jnp.dot(v_ref[rows, :], gtb, preferred_element_type=F32)
            dk_ref[rows, :] = dkdec * ws[c]
            e = dkdec * kdecs[c]
            if c > 0:
                prev = st_ref[c - 1].astype(F32)
            else:
                prev = jnp.where(step == nb - 1, 0.0, prev_ref[0].astype(F32))
            dtot = jnp.sum(gts[c] * prev, axis=0, keepdims=True) * decays[c]
            dla = dtot + _dot_exact01(e, tri_strict, tri_first=True)
            dza = dla * (1.0 / GLA_TEMP) * _sigmoid(-z_all[rows])
            dza_ref[rows, :] = dza.astype(BF16)
            db = db + jnp.sum(dza, axis=0, keepdims=True)
        db_ref[...] += db

    blk = lambda h, i: nb - 1 - i
    return _call(
        body, name="gla_bwd", grid=(GLA_HEADS, nb),
        in_specs=[pl.BlockSpec((r, 128), lambda h, i: (blk(h, i), GLA_Q_BLK + h)),
                  pl.BlockSpec((r, 128), lambda h, i: (blk(h, i), GLA_K_BLK + h)),
                  pl.BlockSpec((r, 256), lambda h, i: (blk(h, i), GLA_V_BLK + h)),
                  pl.BlockSpec((r, W_SMALL), lambda h, i: (blk(h, i), 0)),
                  pl.BlockSpec((W_SMALL, 128), lambda h, i: (0, h)),
                  pl.BlockSpec((1, 128), lambda h, i: (0, h)),
                  pl.BlockSpec((r, 256), lambda h, i: (blk(h, i), h)),
                  pl.BlockSpec((nc, None, GLA_DV, GLA_DK), lambda h, i: (blk(h, i), h, 0, 0)),
                  pl.BlockSpec((1, None, GLA_DV, GLA_DK),
                               lambda h, i: (jnp.maximum(blk(h, i) * nc - 1, 0), h, 0, 0))],
        out_specs=[pl.BlockSpec((r, 128), lambda h, i: (blk(h, i), h)),
                   pl.BlockSpec((r, 128), lambda h, i: (blk(h, i), h)),
                   pl.BlockSpec((r, 256), lambda h, i: (blk(h, i), h)),
                   pl.BlockSpec((r, 128), lambda h, i: (blk(h, i), h)),
                   pl.BlockSpec((1, 128), lambda h, i: (0, h))],
        out_shape=[jax.ShapeDtypeStruct((s, 512), F32), jax.ShapeDtypeStruct((s, 512), F32),
                   jax.ShapeDtypeStruct((s, 1024), F32), jax.ShapeDtypeStruct((s, 512), BF16),
                   jax.ShapeDtypeStruct((1, 512), F32)],
        scratch_shapes=[pltpu.VMEM((GLA_DV, GLA_DK), F32)],
        compiler_params=_params(("parallel", "arbitrary")),
    )(pm, pm, pm, small, wa_pad, b_a2, do, states, states)


def _modulation(c_all, w_ada):
    n = w_ada.shape[1]
    tn = _tile(n, 512)

    def body(c_ref, w_ref, out_ref, ca_ref):
        cv = c_ref[...]
        ca = cv * _sigmoid(cv)
        ca_ref[...] = ca
        out_ref[...] = jnp.dot(ca.astype(BF16), w_ref[...].astype(BF16), preferred_element_type=F32)

    return _call(body, name="modulation", grid=(n // tn,),
                 in_specs=[pl.BlockSpec((N_DEV, D_MODEL), lambda j: (0, 0)),
                           pl.BlockSpec((D_MODEL, tn), lambda j: (0, j))],
                 out_specs=[pl.BlockSpec((N_DEV, tn), lambda j: (0, j)),
                            pl.BlockSpec((N_DEV, D_MODEL), lambda j: (0, 0))],
                 out_shape=[jax.ShapeDtypeStruct((N_DEV, n), F32), jax.ShapeDtypeStruct((N_DEV, D_MODEL), F32)],
                 compiler_params=_params(("arbitrary",)))(c_all, w_ada)


def _adamw_math(w, g, m, v):
    m = ADAM_B1 * m + (1.0 - ADAM_B1) * g
    v = ADAM_B2 * v + (1.0 - ADAM_B2) * (g * g)
    m_hat = m / (1.0 - ADAM_B1 ** ADAM_STEP)
    v_hat = v / (1.0 - ADAM_B2 ** ADAM_STEP)
    delta = -ADAM_LR * (m_hat / (jnp.sqrt(v_hat) + ADAM_EPS) + ADAM_WD * w)
    return delta, m, v


def _adamw_slabs(name, w, slabs, m, v, tr=256):
    rr, cc = w.shape

    def body(w_ref, s_ref, m_ref, v_ref, g_ref, d_ref, nm_ref, nv_ref):
        g = s_ref[0].astype(F32)
        for r in range(1, N_DEV):
            g = g + s_ref[r].astype(F32)
        g_ref[...] = g
        d, nm, nv = _adamw_math(w_ref[...], g, m_ref[...], v_ref[...])
        d_ref[...] = d
        nm_ref[...] = nm
        nv_ref[...] = nv

    steps, spec, slab_spec = _plane_tiles(rr, cc, tr)
    return _call(body, name=name, grid=(steps,),
                 in_specs=[spec, slab_spec, spec, spec],
                 out_specs=[spec] * 4, out_shape=[jax.ShapeDtypeStruct((rr, cc), F32)] * 4,
                 compiler_params=_params(("parallel",)))(w, slabs, m, v)


def _adamw_ada(w, cat, dm, m, v, tr=256):
    rr, cc = w.shape
    tr = _tile(rr, tr)

    def body(w_ref, ca_ref, dm_ref, m_ref, v_ref, g_ref, d_ref, nm_ref, nv_ref):
        g = ca_ref[:, 0:1] * dm_ref[0:1, :]
        for b in range(1, N_DEV):
            g = g + ca_ref[:, b:b + 1] * dm_ref[b:b + 1, :]
        g_ref[...] = g
        d, nm, nv = _adamw_math(w_ref[...], g, m_ref[...], v_ref[...])
        d_ref[...] = d
        nm_ref[...] = nm
        nv_ref[...] = nv

    spec = pl.BlockSpec((tr, cc), lambda i: (i, 0))
    return _call(body, name="adamw_ada", grid=(rr // tr,),
                 in_specs=[spec, pl.BlockSpec((tr, N_DEV), lambda i: (i, 0)),
                           pl.BlockSpec((N_DEV, cc), lambda i: (0, 0)), spec, spec],
                 out_specs=[spec] * 4, out_shape=[jax.ShapeDtypeStruct((rr, cc), F32)] * 4,
                 compiler_params=_params(("parallel",)))(w, cat, dm, m, v)


def _sum_devices(gathered):
    ln = gathered.shape[-1]

    def body(g_ref, out_ref):
        acc = g_ref[0]
        for r in range(1, N_DEV):
            acc = acc + g_ref[r]
        out_ref[...] = acc

    return _call(body, name="sum_devices",
                 in_specs=[pl.BlockSpec(memory_space=pltpu.VMEM)], out_specs=pl.BlockSpec(memory_space=pltpu.VMEM),
                 out_shape=jax.ShapeDtypeStruct((1, ln), F32))(gathered)


def _adamw_flat(w, g, m, v):
    def body(w_ref, g_ref, m_ref, v_ref, d_ref, nm_ref, nv_ref):
        d, nm, nv = _adamw_math(w_ref[...], g_ref[...], m_ref[...], v_ref[...])
        d_ref[...] = d
        nm_ref[...] = nm
        nv_ref[...] = nv

    vm = pl.BlockSpec(memory_space=pltpu.VMEM)
    return _call(body, name="adamw_small", in_specs=[vm] * 4, out_specs=[vm] * 3,
                 out_shape=[jax.ShapeDtypeStruct(w.shape, F32)] * 3)(w, g, m, v)


def _from_col_shards(g):
    return jnp.transpose(g, (1, 0, 2)).reshape(g.shape[1], N_DEV * g.shape[2])


def _pad_lanes(v, n):
    return jnp.concatenate([v, jnp.zeros(v.shape[:-1] + (n - v.shape[-1],), v.dtype)], axis=-1)


def kernel(x, c, w_ada, b_ada, g_pre_mix, g_post_mix, w_in, b_fgate, w_gla_a2, b_gla_a2, g_fox_out, g_gla_out, w_out, g_pre_mlp, g_post_mlp, w_mlp_in, w_mlp_out, loss_target, m_w_ada, m_b_ada, m_g_pre_mix, m_g_post_mix, m_w_in, m_b_fgate, m_w_gla_a2, m_b_gla_a2, m_g_fox_out, m_g_gla_out, m_w_out, m_g_pre_mlp, m_g_post_mlp, m_w_mlp_in, m_w_mlp_out, v_w_ada, v_b_ada, v_g_pre_mix, v_g_post_mix, v_w_in, v_b_fgate, v_w_gla_a2, v_b_gla_a2, v_g_fox_out, v_g_gla_out, v_w_out, v_g_pre_mlp, v_g_post_mlp, v_w_mlp_in, v_w_mlp_out):
    rank = _my_rank()
    xs = x[0]
    s = xs.shape[0]
    target = loss_target[0]

    w_in_t, m_in_t, v_in_t = w_in[0].T, m_w_in[0].T, v_w_in[0].T
    c_all, wa2_g, ggla_g, win_g = _all_gather("gather_first", [c, w_gla_a2[0], g_gla_out[0], w_in_t.astype(BF16)])
    rest = [_own_slot("own_w_out", w_out[0], True, rank), _own_slot("own_w_mlp_in", w_mlp_in[0], True, rank)]
    gs_send, gs_recv, _, gs_land, gs_token = _exchange_start("gather_rest_start", rest, after=(c_all,))
    last = [_own_slot("own_w_mlp_out", w_mlp_out[0], True, rank)]
    gl_send, gl_recv, _, gl_land, gl_token = _exchange_start("gather_last_start", last, after=(gs_token,))
    w_a2 = _from_col_shards(wa2_g)
    g_gla = _from_col_shards(ggla_g).reshape(1, 1024)
    g_fox = g_fox_out.reshape(1, 1024)
    win_full = win_g.reshape(N_DEV * 771, D_MODEL)
    w_main = jnp.concatenate([win_full[:3072], win_full[3080:5128], win_full[5144:6168]], axis=0)
    w_small = jnp.concatenate([win_full[3072:3080], win_full[5128:5144],
                               jnp.zeros((W_SMALL - 24, D_MODEL), BF16)], axis=0)
    wa_pad =jnp.concatenate([jnp.zeros((8, 512), BF16), w_a2.astype(BF16), jnp.zeros((104, 512), BF16)], axis=0)
    bf_vec = _pad_lanes(b_fgate, W_SMALL)

    mod_part, c_act = _modulation(c_all.reshape(N_DEV, D_MODEL), w_ada[0])
    (mod_g,) = _all_gather("gather_mod", [mod_part])
    mod = lax.dynamic_slice_in_dim(mod_g, rank, 1, axis=1).reshape(1, 6 * D_MODEL) + b_ada
    shift_m, scale_m, gate_m, shift_f, scale_f, gate_f = [mod[:, i * D_MODEL:(i + 1) * D_MODEL] for i in range(6)]

    h = _premix(xs, g_pre_mix, scale_m, shift_m, deps=(gl_token,))
    pm = _mm_plain("proj_main", h, w_main, NT, BF16)
    small = _mm_plain("proj_small", h, w_small, NT, F32)
    crow = _fox_cum(small, bf_vec).reshape(FOX_HEADS, 1, s)
    o_fox, lse = _fox_fwd(pm, crow)
    o_gla, states = _gla_fwd(pm, small, wa_pad, b_gla_a2)
    mix = _mix_fwd(o_fox, o_gla, pm, g_fox, g_gla)
    wout_g, wmi_g = _exchange_wait("gather_rest_wait", gs_send, gs_recv, None, gs_land, mix)
    w_out_full = wout_g.reshape(D_MODEL, D_MODEL)
    y =_mm_plain("out_proj", mix, w_out_full, NN, F32)
    x1, h2 = _postmix_premlp(xs, y, gate_m, g_post_mix, g_pre_mlp, scale_f, shift_f)

    tm, tn, tk = _tile(s, 1024), 1024, 2048
    nsh = 1024 // tn

    def relu2(acc):
        rl = jnp.maximum(acc, 0.0)
        return rl * rl, rl

    z, a_relu = _matmul(
        "mlp_in", h2, wmi_g, contract=NN, grid=(s // tm, D_FF // tn, D_MODEL // tk),
        a_spec=pl.BlockSpec((tm, tk), lambda i, j, k: (i, k)),
        b_spec=pl.BlockSpec((None, tk, tn), lambda i, j, k: (j // nsh, k, j % nsh)),
        out_specs=[pl.BlockSpec((tm, tn), lambda i, j, k: (i, j))] * 2,
        out_shapes=[jax.ShapeDtypeStruct((s, D_FF), BF16)] * 2, acc_shape=(tm, tn), epilogue=relu2)
    (wmo_g,) = _exchange_wait("gather_last_wait", gl_send, gl_recv, None, gl_land, z)
    w_mo_full = wmo_g.reshape(D_FF, D_MODEL)
    y2 = _mm_plain("mlp_out", z, w_mo_full, NN, F32)

    dx2, dy2, loss_vec, dgate_f, dg_post_mlp = _loss_postmlp_bwd(x1, y2, target, gate_f, g_post_mlp)
    loss = lax.psum(loss_vec[0, 0], ("x", "y", "c"))

    da = _mm_plain("mlp_out_dx", dy2, w_mo_full, NT, BF16, extra=(a_relu,),
                   epilogue=lambda acc, rl: (acc * (2.0 * rl.astype(F32)),))
    dw_mo = _mm_plain("mlp_out_dw", z, dy2, TN, BF16)
    dw_mo = dw_mo.reshape(N_DEV, 1024, D_MODEL)
    x_mo = _exchange_start("grad_mlp_out_start", [_own_slot("own_dw_mlp_out", dw_mo, False, rank)], [dw_mo])
    tkx = 1024
    (dh2,) = _matmul(
        "mlp_in_dx", da, wmi_g, contract=NT, grid=(s // tm, D_MODEL // tn, D_FF // tkx),
        a_spec=pl.BlockSpec((tm, tkx), lambda i, j, k: (i, k)),
        b_spec=pl.BlockSpec((None, tn, tkx), lambda i, j, k: (k, j, 0)),
        out_specs=[pl.BlockSpec((tm, tn), lambda i, j, k: (i, j))],
        out_shapes=[jax.ShapeDtypeStruct((s, D_MODEL), F32)], acc_shape=(tm, tn), deps=(x_mo[4],))
    ts = _tile(s, 2048)
    (dw_mi,) = _matmul(
        "mlp_in_dw", h2, da, contract=TN, grid=(D_MODEL // 1024, D_FF // tn, s // ts),
        a_spec=pl.BlockSpec((ts, 1024), lambda i, j, k: (k, i)),
        b_spec=pl.BlockSpec((ts, tn), lambda i, j, k: (k, j)),
        out_specs=[pl.BlockSpec((None, 1024, tn), lambda i, j, k: (j // nsh, i, j % nsh))],
        out_shapes=[jax.ShapeDtypeStruct((N_DEV, D_MODEL, 1024), BF16)], acc_shape=(1024, tn))
    x_mi = _exchange_start("grad_mlp_in_start", [_own_slot("own_dw_mlp_in", dw_mi, False, rank)], [dw_mi])

    dx1, dy, dscale_f, dshift_f, dg_pre_mlp, dgate_m, dg_post_mix = _premlp_postmix_bwd(
        dh2, dx2, x1, y, scale_f, g_pre_mlp, gate_m, g_post_mix, deps=(x_mi[4],))

    dmix = _mm_plain("out_proj_dx", dy, w_out_full, NT, F32)
    dw_out = _mm_plain("out_proj_dw", mix, dy, TN, BF16)
    dw_out = dw_out.reshape(N_DEV, 256, D_MODEL)
    x_out = _exchange_start("grad_out_start", [_own_slot("own_dw_out", dw_out, False, rank)], [dw_out])
    do_fox, do_gla, dgr, dg_fox, dg_gla = _mix_bwd(dmix, o_fox, o_gla, pm, g_fox, g_gla, deps=(x_out[4],))

    dq, dk, dv, dc, dcq = _fox_bwd(pm, crow, o_fox, lse, do_fox)
    dsmall_f, db_f = _fox_cum_bwd(dc.reshape(FOX_HEADS, s), dcq, small, bf_vec)
    dgq, dgk, dgv, dza, db_a2 = _gla_bwd(pm, small, wa_pad, b_gla_a2, states, do_gla)
    dsmall = _mm_plain("gate_dx", dza, wa_pad, NT, F32, tn=128, extra=(dsmall_f,),
                       epilogue=lambda acc, other: (acc + other,))
    dwa_pad = _mm_plain("gate_dw", small, dza, TN, F32, tm=128, tn=512)

    dpm = jnp.concatenate([dq.astype(BF16), dk.astype(BF16), dv.astype(BF16), dgq.astype(BF16), dgk.astype(BF16),
                           dgv.astype(BF16), dgr], axis=1)
    dw_main = _mm_plain("proj_main_dw", dpm, h, TN, BF16)
    dw_small = _mm_plain("proj_small_dw", dsmall, h, TN, BF16, tm=128)
    dwin_full = jnp.concatenate([dw_main[:3072], dw_small[0:8], dw_main[3072:5120], dw_small[8:24],
                                 dw_main[5120:6144]], axis=0)
    dwin_slabs = dwin_full.reshape(N_DEV, 771, D_MODEL)
    x_in = _exchange_start("grad_in_start", [_own_slot("own_dw_in", dwin_slabs, False, rank)], [dwin_slabs])
    dh_small = _mm_plain("proj_small_dx", dsmall, w_small, NN, F32, tk=128)
    dh = _mm_plain("proj_main_dx", dpm, w_main, NN, F32, extra=(dh_small,),
                   epilogue=lambda acc, other: (acc + other,), deps=(x_in[4],))
    grad_x, dscale_m, dshift_m, dg_pre_mix = _premix_bwd(dh, dx1, xs, g_pre_mix, scale_m)

    dmod = jnp.concatenate([dshift_m, dscale_m, dgate_m, dshift_f, dscale_f, dgate_f], axis=1)
    flat = jnp.concatenate(
        [dmod, dg_pre_mix, dg_post_mix, dg_fox, dg_pre_mlp, dg_post_mlp, db_a2,
         dwa_pad[8:24, :].reshape(1, GLA_RANK * 512), dg_gla, _pad_lanes(db_f[:, 0].reshape(1, FOX_HEADS), 128)],
        axis=1)

    (r_mo,) = _exchange_wait("grad_mlp_out_wait", *x_mo[:4], grad_x)
    g_mo, d_mo, nm_mo, nv_mo = _adamw_slabs("adamw_w_mlp_out", w_mlp_out[0], r_mo, m_w_mlp_out[0], v_w_mlp_out[0])
    (r_mi,) = _exchange_wait("grad_mlp_in_wait", *x_mi[:4], g_mo)
    g_mi, d_mi, nm_mi, nv_mi = _adamw_slabs("adamw_w_mlp_in", w_mlp_in[0], r_mi, m_w_mlp_in[0], v_w_mlp_in[0])
    (r_out,) = _exchange_wait("grad_out_wait", *x_out[:4], g_mi)
    g_out, d_out, nm_out, nv_out = _adamw_slabs("adamw_w_out", w_out[0], r_out, m_w_out[0], v_w_out[0])

    (flat_g,) = _all_gather("gather_small_grads", [flat], deps=(g_out,))
    tot = _sum_devices(flat_g)
    dm_cols = lax.dynamic_slice_in_dim(flat_g[:, 0, :6 * D_MODEL], rank * 1536, 1536, axis=1)
    g_ada, d_ada, nm_ada, nv_ada = _adamw_ada(w_ada[0], c_act.T, dm_cols, m_w_ada[0], v_w_ada[0])
    (r_in,) = _exchange_wait("grad_in_wait", *x_in[:4], g_ada)
    g_in, d_in, nm_in, nv_in = [a.T for a in _adamw_slabs("adamw_w_in", w_in_t, r_in, m_in_t, v_in_t)]

    o = 0
    seg = {}
    for name, n in (("b_ada", 12288), ("g_pre_mix", 2048), ("g_post_mix", 2048), ("g_fox_out", 1024),
                    ("g_pre_mlp", 2048), ("g_post_mlp", 2048), ("b_gla_a2", 512), ("w_gla_a2", 8192),
                    ("g_gla_out", 1024), ("b_fgate", 128)):
        seg[name] = tot[:, o:o + n]
        o += n
    g_wa2 = lax.dynamic_slice_in_dim(seg["w_gla_a2"].reshape(GLA_RANK, 512), rank * 64, 64, axis=1)
    g_ggla = lax.dynamic_slice_in_dim(seg["g_gla_out"].reshape(GLA_HEADS, GLA_DV), rank * 32, 32, axis=1)
    small_names = ["b_ada", "g_pre_mix", "g_post_mix", "g_fox_out", "g_pre_mlp", "g_post_mlp", "b_gla_a2",
                   "w_gla_a2", "g_gla_out", "b_fgate"]
    small_grads = {**seg, "w_gla_a2": g_wa2.reshape(1, 1024), "g_gla_out": g_ggla.reshape(1, 128)}
    weights = dict(b_ada=b_ada, g_pre_mix=g_pre_mix, g_post_mix=g_post_mix, g_fox_out=g_fox_out,
                   g_pre_mlp=g_pre_mlp, g_post_mlp=g_post_mlp, b_gla_a2=b_gla_a2, w_gla_a2=w_gla_a2,
                   g_gla_out=g_gla_out, b_fgate=b_fgate)
    moms = dict(b_ada=m_b_ada, g_pre_mix=m_g_pre_mix, g_post_mix=m_g_post_mix, g_fox_out=m_g_fox_out,
                g_pre_mlp=m_g_pre_mlp, g_post_mlp=m_g_post_mlp, b_gla_a2=m_b_gla_a2, w_gla_a2=m_w_gla_a2,
                g_gla_out=m_g_gla_out, b_fgate=m_b_fgate)
    vels = dict(b_ada=v_b_ada, g_pre_mix=v_g_pre_mix, g_post_mix=v_g_post_mix, g_fox_out=v_g_fox_out,
                g_pre_mlp=v_g_pre_mlp, g_post_mlp=v_g_post_mlp, b_gla_a2=v_b_gla_a2, w_gla_a2=v_w_gla_a2,
                g_gla_out=v_g_gla_out, b_fgate=v_b_fgate)

    def flatten(d, fill):
        parts = []
        for nm in small_names:
            p = d[nm].reshape(1, -1)
            if nm == "b_fgate":
                p = jnp.concatenate([p[:, :FOX_HEADS], jnp.full((1, 128 - FOX_HEADS), fill, F32)], axis=1)
            parts.append(p)
        return jnp.concatenate(parts, axis=1).reshape(-1, 128)

    fw, fg, fm, fv = flatten(weights, 0.0), flatten(small_grads, 0.0), flatten(moms, 0.0), flatten(vels, 1.0)
    fd, fnm, fnv = _adamw_flat(fw, fg, fm, fv)

    def unflatten(fl):
        fl = fl.reshape(1, -1)
        out = {}
        o = 0
        for nm in small_names:
            n = 128 if nm == "b_fgate" else weights[nm].size
            piece = fl[:, o:o + n]
            if nm == "b_fgate":
                piece = piece[:, :FOX_HEADS]
            out[nm] = piece.reshape(weights[nm].shape)
            o += n
        return out

    sg, sd, snm, snv = unflatten(fg), unflatten(fd), unflatten(fnm), unflatten(fnv)

    big = dict(w_ada=(g_ada, d_ada, nm_ada, nv_ada), w_in=(g_in, d_in, nm_in, nv_in),
               w_out=(g_out, d_out, nm_out, nv_out), w_mlp_in=(g_mi, d_mi, nm_mi, nv_mi),
               w_mlp_out=(g_mo, d_mo, nm_mo, nv_mo))
    order = ["w_ada", "b_ada", "g_pre_mix", "g_post_mix", "w_in", "b_fgate", "w_gla_a2", "b_gla_a2", "g_fox_out",
             "g_gla_out", "w_out", "g_pre_mlp", "g_post_mlp", "w_mlp_in", "w_mlp_out"]

    def pick(nm, idx):
        if nm in big:
            return big[nm][idx][None]
        return (sg, sd, snm, snv)[idx][nm]

    grads = [pick(nm, 0) for nm in order]
    deltas = [pick(nm, 1) for nm in order]
    new_m = [pick(nm, 2) for nm in order]
    new_v = [pick(nm, 3) for nm in order]
    return (loss, grad_x[None], *grads, *deltas, *new_m, *new_v)
```

```python
import functools

import numpy as np
import jax
import jax.numpy as jnp
from jax import lax
from jax.experimental import pallas as pl
from jax.experimental.pallas import tpu as pltpu

F32 = jnp.float32
BF16 = jnp.bfloat16
MESH = pl.DeviceIdType.MESH
N_DEV = 8

D_MODEL = 2048
FOX_HEADS = 8
FOX_HEAD_DIM = 128
GLA_HEADS = 4
GLA_DK = 128
GLA_DV = 256
GLA_RANK = 16
GLA_TEMP = 16.0
CHUNK = 64
D_FF = 8192
W_MAIN = 6144
W_SMALL = 128
EPS = 1e-6
NEG = float(np.finfo(np.float32).min)

ADAM_LR = 0.001
ADAM_B1 = 0.9
ADAM_B2 = 0.999
ADAM_EPS = 1e-08
ADAM_WD = 0.01
ADAM_STEP = 10

ROW_T = 256
FOX_T = 1024
GLA_R = 512
CUM_T = 256
VMEM_LIMIT = 56 * 1024 * 1024


def _call(body, deps=(), **kw):
    if not deps:
        return pl.pallas_call(body, **kw)
    n_in, n_dep = len(kw["in_specs"]), len(deps)

    def with_deps(*refs):
        return body(*refs[:n_in], *refs[n_in + n_dep:])

    kw["in_specs"] = [*kw["in_specs"], *[pl.BlockSpec(memory_space=pl.ANY)] * n_dep]
    call = pl.pallas_call(with_deps, **kw)
    return lambda *args: call(*args, *deps)


def _params(sem=None):
    return pltpu.CompilerParams(dimension_semantics=sem, vmem_limit_bytes=VMEM_LIMIT)


def _my_pos():
    return lax.axis_index("x"), lax.axis_index("y"), lax.axis_index("c")


def _my_rank():
    x, y, c = _my_pos()
    return 4 * x + 2 * y + c


def _all_gather(name, arrays, deps=()):
    n = len(arrays)

    def body(*refs):
        ins = refs[:n]
        outs = refs[n:2 * n]
        send_sems, recv_sems, local_sems = refs[2 * n:]
        x, y, c = _my_pos()
        me, sibling = (x, y, c), (x, y, 1 - c)
        chips = [(1 - x, y), (x, 1 - y), (1 - x, 1 - y)]

        def slot(a, px, py, pc):
            return outs[a].at[4 * px + 2 * py + pc]

        def copy(a, k, block, to, src=None):
            return pltpu.make_async_remote_copy(
                src_ref=slot(a, *block) if src is None else src, dst_ref=slot(a, *block),
                send_sem=send_sems.at[a, k], recv_sem=recv_sems.at[a, k],
                device_id=to, device_id_type=MESH)

        started = []
        for a in range(n):
            mine = pltpu.make_async_copy(ins[a], slot(a, *me), local_sems.at[a])
            mine.start()
            started.append(mine)
        first = []
        for a in range(n):
            first.append(copy(a, 0, me, sibling, src=ins[a]))
            first += [copy(a, 1 + j, me, (*chip, c), src=ins[a]) for j, chip in enumerate(chips)]
        for cp in first:
            cp.start()
        passed = []
        for j, chip in enumerate(chips):
            for a in range(n):
                copy(a, 1 + j, (*chip, c), me).wait_recv()
                fwd = copy(a, 4 + j, (*chip, c), sibling)
                fwd.start()
                passed.append(fwd)
        for a in range(n):
            copy(a, 0, sibling, me).wait_recv()
            for j, chip in enumerate(chips):
                copy(a, 4 + j, (*chip, 1 - c), me).wait_recv()
        for cp in first + passed:
            cp.wait_send()
        for mine in started:
            mine.wait()

    hbm = pl.BlockSpec(memory_space=pltpu.HBM)
    return _call(
        body, deps=deps, name=name,
        out_shape=[jax.ShapeDtypeStruct((N_DEV,) + a.shape, a.dtype) for a in arrays],
        in_specs=[hbm] * n, out_specs=[hbm] * n,
        scratch_shapes=[pltpu.SemaphoreType.DMA((n, 7)), pltpu.SemaphoreType.DMA((n, 7)),
                        pltpu.SemaphoreType.DMA((n,))],
    )(*arrays)


def _plane_tiles(rr, cc, tr=512, tc=512):
    if rr % 8 == 0:
        tr = _tile(rr, tr)
        return (rr // tr, pl.BlockSpec((tr, cc), lambda i: (i, 0)),
                pl.BlockSpec((N_DEV, tr, cc), lambda i: (0, i, 0)))
    tc = _tile(cc, tc)
    return (cc // tc, pl.BlockSpec((rr, tc), lambda i: (0, i)),
            pl.BlockSpec((N_DEV, rr, tc), lambda i: (0, 0, i)))


def _own_slot(name, src, gather, rank):
    shape = ((N_DEV,) + src.shape) if gather else src.shape
    rr, cc = shape[1], shape[2]
    by_rows = rr % 8 == 0
    tr, tc = (_tile(rr, 512), cc) if by_rows else (rr, _tile(cc, 512))
    steps = rr // tr if by_rows else cc // tc

    def body(rank_ref, s_ref, o_ref):
        o_ref[...] = s_ref[...].astype(o_ref.dtype)

    def at(i):
        return (i, 0) if by_rows else (0, i)

    if gather:
        in_spec = pl.BlockSpec((tr, tc), lambda i, rk: at(i))
    else:
        in_spec = pl.BlockSpec((None, tr, tc), lambda i, rk: (rk[0], *at(i)))
    grid_spec = pltpu.PrefetchScalarGridSpec(
        num_scalar_prefetch=1, grid=(steps,), in_specs=[in_spec],
        out_specs=pl.BlockSpec((None, tr, tc), lambda i, rk: (rk[0], *at(i))))
    return _call(body, name=name, grid_spec=grid_spec, out_shape=jax.ShapeDtypeStruct(shape, BF16),
                 compiler_params=_params(("arbitrary",)))(jnp.reshape(rank, (1,)).astype(jnp.int32), src)


_HBM = pl.BlockSpec(memory_space=pltpu.HBM)
_SEM = pl.BlockSpec(memory_space=pltpu.SEMAPHORE)
_FLIPS = [(kx, ky, kc) for kx in (0, 1) for ky in (0, 1) for kc in (0, 1)][1:]


def _peers():
    x, y, c = _my_pos()
    out = []
    for kx, ky, kc in _FLIPS:
        px, py, pc = (1 - x if kx else x), (1 - y if ky else y), (1 - c if kc else c)
        out.append(((px, py, pc), 4 * px + 2 * py + pc))
    return out


def _exchange_copy(srcs, lands, send_sems, recv_sems, a, k, peer, peer_rank, slot):
    return pltpu.make_async_remote_copy(
        src_ref=lands[a].at[slot] if srcs is None else srcs[a].at[peer_rank],
        dst_ref=lands[a].at[slot],
        send_sem=send_sems[a].at[k], recv_sem=recv_sems[a].at[k],
        device_id=peer, device_id_type=MESH)


def _exchange_start(name, lands, srcs=None, after=()):
    n = len(lands)
    n_src = 0 if srcs is None else n
    n_in = n + n_src + len(after)

    def body(*refs):
        lnd = refs[:n]
        src = None if srcs is None else refs[n:2 * n]
        send_sems, recv_sems = refs[n_in:n_in + n], refs[n_in + n:n_in + 2 * n]
        token = refs[-1]
        me = _my_rank()
        for a in range(n):
            for k, (peer, peer_rank) in enumerate(_peers()):
                _exchange_copy(src, lnd, send_sems, recv_sems, a, k, peer, peer_rank, me).start()
        token[...] = jnp.zeros_like(token)

    sems = [pltpu.SemaphoreType.DMA((7,))] * (2 * n)
    thru = list(lands) + ([] if srcs is None else list(srcs))
    outs = pl.pallas_call(
        body, name=name,
        out_shape=(*sems, *[pltpu.HBM(t.shape, t.dtype) for t in thru], jax.ShapeDtypeStruct((8, 128), F32)),
        in_specs=[*[_HBM] * len(thru), *[pl.BlockSpec(memory_space=pl.ANY)] * len(after)],
        out_specs=(*[_SEM] * (2 * n), *[_HBM] * len(thru), pl.BlockSpec(memory_space=pltpu.VMEM)),
        input_output_aliases={i: 2 * n + i for i in range(len(thru))},
        compiler_params=pltpu.CompilerParams(has_side_effects=pltpu.SideEffectType.DATAFLOW_SIDE_EFFECTING),
    )(*[pltpu.with_memory_space_constraint(t, pltpu.HBM) for t in thru], *after)
    lands_thru = outs[2 * n:3 * n]
    srcs_thru = None if srcs is None else outs[3 * n:4 * n]
    return outs[:n], outs[n:2 * n], srcs_thru, lands_thru, outs[-1]


def _exchange_wait(name, send_sems, recv_sems, srcs, lands, after):
    n = len(lands)
    thru = list(lands) + ([] if srcs is None else list(srcs))

    def body(*refs):
        lnd = refs[:n]
        src = None if srcs is None else refs[n:2 * n]
        ssem, rsem = refs[len(thru):len(thru) + n], refs[len(thru) + n:len(thru) + 2 * n]
        for a in range(n):
            for k, (peer, peer_rank) in enumerate(_peers()):
                cp = _exchange_copy(src, lnd, ssem, rsem, a, k, peer, peer_rank, peer_rank)
                cp.wait_send()
                cp.wait_recv()

    outs = pl.pallas_call(
        body, name=name,
        out_shape=tuple(pltpu.HBM(t.shape, t.dtype) for t in thru),
        in_specs=[*[_HBM] * len(thru), *[_SEM] * (2 * n), pl.BlockSpec(memory_space=pl.ANY)],
        out_specs=tuple([_HBM] * len(thru)),
        input_output_aliases={i: i for i in range(len(thru))},
        compiler_params=pltpu.CompilerParams(has_side_effects=pltpu.SideEffectType.DATAFLOW_SIDE_EFFECTING),
    )(*thru, *send_sems, *recv_sems, after)
    return outs[:n]


_SIDE = pltpu.CompilerParams(has_side_effects=pltpu.SideEffectType.DATAFLOW_SIDE_EFFECTING)
_ANY = pl.BlockSpec(memory_space=pl.ANY)


def _chips():
    x, y, _ = _my_pos()
    return [(1 - x, y), (x, 1 - y), (1 - x, 1 - y)]


def _slot_copy(lnd, slot, send_sem, recv_sem, to):
    return pltpu.make_async_remote_copy(src_ref=lnd.at[slot], dst_ref=lnd.at[slot], send_sem=send_sem,
                                        recv_sem=recv_sem, device_id=to, device_id_type=MESH)


def _gather2_start(name, lands, after=()):
    n = len(lands)
    n_in = n + len(after)

    def body(*refs):
        lnd = refs[:n]
        send, recv_sib, recv_ici = refs[n_in:n_in + n], refs[n_in + n:n_in + 2 * n], refs[n_in + 2 * n:n_in + 3 * n]
        x, y, c = _my_pos()
        me = 4 * x + 2 * y + c
        for a in range(n):
            _slot_copy(lnd[a], me, send[a].at[0], recv_sib[a].at[0], (x, y, 1 - c)).start()
            for j, chip in enumerate(_chips()):
                _slot_copy(lnd[a], me, send[a].at[1 + j], recv_ici[a].at[j], (*chip, c)).start()
        refs[-1][...] = jnp.zeros_like(refs[-1])

    sems = [pltpu.SemaphoreType.DMA((4,))] * n + [pltpu.SemaphoreType.DMA((1,))] * n + [pltpu.SemaphoreType.DMA((3,))] * n
    outs = pl.pallas_call(
        body, name=name,
        out_shape=(*sems, *[pltpu.HBM(t.shape, t.dtype) for t in lands], jax.ShapeDtypeStruct((8, 128), F32)),
        in_specs=[*[_HBM] * n, *[_ANY] * len(after)],
        out_specs=(*[_SEM] * (3 * n), *[_HBM] * n, pl.BlockSpec(memory_space=pltpu.VMEM)),
        input_output_aliases={i: 3 * n + i for i in range(n)}, compiler_params=_SIDE,
    )(*[pltpu.with_memory_space_constraint(t, pltpu.HBM) for t in lands], *after)
    return outs[:n], outs[n:2 * n], outs[2 * n:3 * n], outs[3 * n:4 * n], outs[-1]


def _gather2_forward(name, lands, recv_ici, after):
    n = len(lands)

    def body(*refs):
        lnd, arrived = refs[:n], refs[n:2 * n]
        send, recv = refs[2 * n + 1:3 * n + 1], refs[3 * n + 1:4 * n + 1]
        x, y, c = _my_pos()
        for j, (cx, cy) in enumerate(_chips()):
            slot = 4 * cx + 2 * cy + c
            for a in range(n):
                _slot_copy(lnd[a], slot, send[a].at[j], arrived[a].at[j], (cx, cy, c)).wait_recv()
                _slot_copy(lnd[a], slot, send[a].at[j], recv[a].at[j], (x, y, 1 - c)).start()
        refs[-1][...] = jnp.zeros_like(refs[-1])

    sems = [pltpu.SemaphoreType.DMA((3,))] * (2 * n)
    outs = pl.pallas_call(
        body, name=name,
        out_shape=(*sems, *[pltpu.HBM(t.shape, t.dtype) for t in lands], jax.ShapeDtypeStruct((8, 128), F32)),
        in_specs=[*[_HBM] * n, *[_SEM] * n, _ANY],
        out_specs=(*[_SEM] * (2 * n), *[_HBM] * n, pl.BlockSpec(memory_space=pltpu.VMEM)),
        input_output_aliases={i: 2 * n + i for i in range(n)}, compiler_params=_SIDE,
    )(*lands, *recv_ici, after)
    return outs[:n], outs[n:2 * n], outs[2 * n:3 * n], outs[-1]


def _gather2_wait(name, lands, send_a, recv_sib, send_b, recv_b, after):
    n = len(lands)

    def body(*refs):
        lnd = refs[:n]
        sa, rs, sb, rb = (refs[(1 + i) * n:(2 + i) * n] for i in range(4))
        x, y, c = _my_pos()
        me = 4 * x + 2 * y + c
        for a in range(n):
            for k in range(4):
                _slot_copy(lnd[a], me, sa[a].at[k], rs[a].at[0], (x, y, 1 - c)).wait_send()
            _slot_copy(lnd[a], me - c + (1 - c), sa[a].at[0], rs[a].at[0], (x, y, 1 - c)).wait_recv()
            for j, (cx, cy) in enumerate(_chips()):
                _slot_copy(lnd[a], 4 * cx + 2 * cy + c, sb[a].at[j], rb[a].at[j], (x, y, 1 - c)).wait_send()
                _slot_copy(lnd[a], 4 * cx + 2 * cy + (1 - c), sb[a].at[j], rb[a].at[j], (x, y, 1 - c)).wait_recv()

    outs = pl.pallas_call(
        body, name=name,
        out_shape=tuple(pltpu.HBM(t.shape, t.dtype) for t in lands),
        in_specs=[*[_HBM] * n, *[_SEM] * (4 * n), _ANY],
        out_specs=tuple([_HBM] * n),
        input_output_aliases={i: i for i in range(n)}, compiler_params=_SIDE,
    )(*lands, *send_a, *recv_sib, *send_b, *recv_b, after)
    return outs


NN = ((1,), (0,))
NT = ((1,), (1,))
TN = ((0,), (0,))


def _matmul(name, a, b, *, contract, grid, a_spec, b_spec, out_specs, out_shapes, acc_shape,
            extra=(), extra_specs=(), epilogue=None, deps=()):
    nk = grid[2]
    n_extra = len(extra)
    n_out = len(out_shapes)

    def body(*refs):
        a_ref, b_ref = refs[0], refs[1]
        extra_refs = refs[2:2 + n_extra]
        out_refs = refs[2 + n_extra:2 + n_extra + n_out]
        acc_ref = refs[-1]
        k = pl.program_id(2)

        def prod():
            return lax.dot_general(a_ref[...].astype(BF16), b_ref[...].astype(BF16), (contract, ((), ())),
                                   preferred_element_type=F32)

        def finish(acc):
            res = (acc,) if epilogue is None else epilogue(acc, *[r[...] for r in extra_refs])
            for o_ref, val in zip(out_refs, res):
                o_ref[...] = val.astype(o_ref.dtype)

        if nk == 1:
            finish(prod())
            return

        @pl.when(k == 0)
        def _():
            acc_ref[...] = prod()

        @pl.when((k > 0) & (k < nk - 1))
        def _():
            acc_ref[...] += prod()

        @pl.when(k == nk - 1)
        def _():
            finish(acc_ref[...] + prod())

    outs = _call(
        body, deps=deps, name=name, grid=grid,
        in_specs=[a_spec, b_spec, *extra_specs], out_specs=list(out_specs), out_shape=list(out_shapes),
        scratch_shapes=[pltpu.VMEM(acc_shape if nk > 1 else (8, 128), F32)],
        compiler_params=_params(("parallel", "parallel", "arbitrary")),
    )(a, b, *extra)
    return outs


def _tile(n, t):
    t = min(n, t)
    assert n % t == 0, (n, t)
    return t


def _mm_plain(name, a, b, contract, out_dtype, tm=1024, tn=1024, tk=2048, extra=(), epilogue=None,
              n_out=1, out_dtypes=None, deps=()):
    if contract == NN:
        (m, kd), (_, n) = a.shape, b.shape
    elif contract == NT:
        (m, kd), (n, _) = a.shape, b.shape
    else:
        (kd, m), (_, n) = a.shape, b.shape
    tm, tn, tk = _tile(m, tm), _tile(n, tn), _tile(kd, tk)
    if contract == NN:
        a_spec = pl.BlockSpec((tm, tk), lambda i, j, k: (i, k))
        b_spec = pl.BlockSpec((tk, tn), lambda i, j, k: (k, j))
    elif contract == NT:
        a_spec = pl.BlockSpec((tm, tk), lambda i, j, k: (i, k))
        b_spec = pl.BlockSpec((tn, tk), lambda i, j, k: (j, k))
    else:
        a_spec = pl.BlockSpec((tk, tm), lambda i, j, k: (k, i))
        b_spec = pl.BlockSpec((tk, tn), lambda i, j, k: (k, j))
    o_spec = pl.BlockSpec((tm, tn), lambda i, j, k: (i, j))
    out_dtypes = out_dtypes or [out_dtype] * n_out
    outs = _matmul(
        name, a, b, contract=contract, grid=(m // tm, n // tn, kd // tk), a_spec=a_spec, b_spec=b_spec,
        out_specs=[o_spec] * len(out_dtypes), out_shapes=[jax.ShapeDtypeStruct((m, n), dt) for dt in out_dtypes],
        acc_shape=(tm, tn), extra=extra, extra_specs=[o_spec] * len(extra), epilogue=epilogue, deps=deps)
    return outs[0] if len(out_dtypes) == 1 else outs


def _rows_call(name, body, row_in, vec_in, row_out, vec_out, s, deps=()):
    t = _tile(s, ROW_T)
    in_specs = []
    args = []
    for arr, width, cb in row_in:
        in_specs.append(pl.BlockSpec((t, width), functools.partial(lambda i, cb: (i, cb), cb=cb)))
        args.append(arr)
    for v in vec_in:
        in_specs.append(pl.BlockSpec(v.shape, lambda i: (0, 0)))
        args.append(v)
    out_specs = []
    out_shapes = []
    for width, dt in row_out:
        out_specs.append(pl.BlockSpec((t, width), lambda i: (i, 0)))
        out_shapes.append(jax.ShapeDtypeStruct((s, width), dt))
    for width in vec_out:
        out_specs.append(pl.BlockSpec((1, width), lambda i: (0, 0)))
        out_shapes.append(jax.ShapeDtypeStruct((1, width), F32))
    return _call(body, deps=deps, name=name, grid=(s // t,), in_specs=in_specs, out_specs=out_specs,
                 out_shape=out_shapes, compiler_params=_params(("arbitrary",)))(*args)


def _acc_vec(ref, val):
    _acc_row(ref, jnp.sum(val, axis=0, keepdims=True))


def _acc_row(ref, part):
    @pl.when(pl.program_id(0) == 0)
    def _():
        ref[...] = part

    @pl.when(pl.program_id(0) > 0)
    def _():
        ref[...] += part


def _rms(v):
    return lax.rsqrt(jnp.mean(v * v, axis=-1, keepdims=True) + EPS)


def _norm_bwd(dxn, xn, r):
    return r * (dxn - xn * jnp.mean(dxn * xn, axis=-1, keepdims=True))


def _premix(x, g, scale, shift, deps=()):
    s = x.shape[0]

    def body(x_ref, g_ref, sc_ref, sh_ref, h_ref):
        xv = x_ref[...]
        h_ref[...] = ((xv * _rms(xv) * g_ref[...]) * (1.0 + sc_ref[...]) + sh_ref[...]).astype(BF16)

    return _rows_call("premix", body, [(x, D_MODEL, 0)], [g, scale, shift], [(D_MODEL, BF16)], [], s, deps)[0]


def _sigmoid(z):
    return 1.0 / (1.0 + jnp.exp(-z))


def _mix_fwd(o_fox, o_gla, pm, g_fox, g_gla):
    s = o_fox.shape[0]

    def body(of_ref, og_ref, gr_ref, gf_ref, gg_ref, mix_ref):
        for h in range(FOX_HEADS):
            sl = slice(h * FOX_HEAD_DIM, (h + 1) * FOX_HEAD_DIM)
            seg = of_ref[:, sl]
            mix_ref[:, sl] = (seg * _rms(seg) * gf_ref[:, sl]).astype(BF16)
        for h in range(GLA_HEADS):
            sl = slice(h * GLA_DV, (h + 1) * GLA_DV)
            seg = og_ref[:, sl]
            gr = gr_ref[:, sl].astype(F32)
            val = (seg * _rms(seg) * gg_ref[:, sl]) * (gr * _sigmoid(gr))
            mix_ref[:, pl.ds(FOX_HEADS * FOX_HEAD_DIM + h * GLA_DV, GLA_DV)] = val.astype(BF16)

    return _rows_call("mix_fwd", body, [(o_fox, 1024, 0), (o_gla, 1024, 0), (pm, 1024, 5)], [g_fox, g_gla],
                      [(D_MODEL, BF16)], [], s)[0]


def _mix_bwd(dmix, o_fox, o_gla, pm, g_fox, g_gla, deps=()):
    s = o_fox.shape[0]

    def body(dm_ref, of_ref, og_ref, gr_ref, gf_ref, gg_ref, dof_ref, dog_ref, dgr_ref, dgf_ref, dgg_ref):
        dgf = []
        for h in range(FOX_HEADS):
            sl = slice(h * FOX_HEAD_DIM, (h + 1) * FOX_HEAD_DIM)
            seg = of_ref[:, sl]
            r = _rms(seg)
            segn = seg * r
            dout = dm_ref[:, sl]
            dgf.append(jnp.sum(dout * segn, axis=0, keepdims=True))
            dof_ref[:, sl] = _norm_bwd(dout * gf_ref[:, sl], segn, r).astype(BF16)
        dgg = []
        for h in range(GLA_HEADS):
            sl = slice(h * GLA_DV, (h + 1) * GLA_DV)
            seg = og_ref[:, sl]
            r = _rms(seg)
            segn = seg * r
            gl = segn * gg_ref[:, sl]
            gr = gr_ref[:, sl].astype(F32)
            sig = _sigmoid(gr)
            dout = dm_ref[:, pl.ds(FOX_HEADS * FOX_HEAD_DIM + h * GLA_DV, GLA_DV)]
            dgr_ref[:, sl] = (dout * gl * (sig * (1.0 + gr * (1.0 - sig)))).astype(BF16)
            dgl = dout * (gr * sig)
            dgg.append(jnp.sum(dgl * segn, axis=0, keepdims=True))
            dog_ref[:, sl] = _norm_bwd(dgl * gg_ref[:, sl], segn, r).astype(BF16)
        _acc_row(dgf_ref, jnp.concatenate(dgf, axis=1))
        _acc_row(dgg_ref, jnp.concatenate(dgg, axis=1))

    return _rows_call("mix_bwd", body, [(dmix, D_MODEL, 0), (o_fox, 1024, 0), (o_gla, 1024, 0), (pm, 1024, 5)],
                      [g_fox, g_gla], [(1024, BF16), (1024, BF16), (1024, BF16)], [1024, 1024], s, deps)


def _postmix_premlp(x, y, gate_m, g_post_mix, g_pre_mlp, scale_f, shift_f):
    s = x.shape[0]

    def body(x_ref, y_ref, gm_ref, gpm_ref, gpl_ref, sc_ref, sh_ref, x1_ref, h2_ref):
        yv = y_ref[...]
        x1 = x_ref[...] + gm_ref[...] * (yv * _rms(yv) * gpm_ref[...])
        x1_ref[...] = x1
        h2_ref[...] = ((x1 * _rms(x1) * gpl_ref[...]) * (1.0 + sc_ref[...]) + sh_ref[...]).astype(BF16)

    return _rows_call("postmix_premlp", body, [(x, D_MODEL, 0), (y, D_MODEL, 0)],
                      [gate_m, g_post_mix, g_pre_mlp, scale_f, shift_f], [(D_MODEL, F32), (D_MODEL, BF16)], [], s)


def _loss_postmlp_bwd(x1, y2, target, gate_f, g_post_mlp):
    s = x1.shape[0]

    def body(x1_ref, y2_ref, t_ref, gf_ref, g_ref, dx2_ref, dy2_ref, loss_ref, dgate_ref, dg_ref):
        yv = y2_ref[...]
        r = _rms(yv)
        yn = yv * r
        o = yn * g_ref[...]
        e = (x1_ref[...] + gf_ref[...] * o) - t_ref[...]
        part = 0.5 * jnp.sum(jnp.mean(e * e, axis=-1, keepdims=True), axis=0, keepdims=True)
        _acc_vec(loss_ref, jnp.broadcast_to(part, (1, 128)))
        dx2 = e * (1.0 / D_MODEL)
        dx2_ref[...] = dx2
        _acc_vec(dgate_ref, dx2 * o)
        do = dx2 * gf_ref[...]
        _acc_vec(dg_ref, do * yn)
        dy2_ref[...] = _norm_bwd(do * g_ref[...], yn, r).astype(BF16)

    return _rows_call("loss_postmlp_bwd", body, [(x1, D_MODEL, 0), (y2, D_MODEL, 0), (target, D_MODEL, 0)],
                      [gate_f, g_post_mlp], [(D_MODEL, F32), (D_MODEL, BF16)], [128, D_MODEL, D_MODEL], s)


def _premlp_postmix_bwd(dh2, dx2, x1, y, scale_f, g_pre_mlp, gate_m, g_post_mix, deps=()):
    s = x1.shape[0]

    def body(dh2_ref, dx2_ref, x1_ref, y_ref, sc_ref, gpl_ref, gm_ref, gpm_ref,
             dx1_ref, dy_ref, dsc_ref, dsh_ref, dgpl_ref, dgm_ref, dgpm_ref):
        x1 = x1_ref[...]
        r1 = _rms(x1)
        x1n = x1 * r1
        dh2 = dh2_ref[...]
        _acc_vec(dsc_ref, dh2 * (x1n * gpl_ref[...]))
        _acc_vec(dsh_ref, dh2)
        dn2 = dh2 * (1.0 + sc_ref[...])
        _acc_vec(dgpl_ref, dn2 * x1n)
        dx1 = dx2_ref[...] + _norm_bwd(dn2 * gpl_ref[...], x1n, r1)
        dx1_ref[...] = dx1
        yv = y_ref[...]
        ry = _rms(yv)
        yn = yv * ry
        _acc_vec(dgm_ref, dx1 * (yn * gpm_ref[...]))
        do = dx1 * gm_ref[...]
        _acc_vec(dgpm_ref, do * yn)
        dy_ref[...] = _norm_bwd(do * gpm_ref[...], yn, ry).astype(BF16)

    return _rows_call("premlp_postmix_bwd", body,
                      [(dh2, D_MODEL, 0), (dx2, D_MODEL, 0), (x1, D_MODEL, 0), (y, D_MODEL, 0)],
                      [scale_f, g_pre_mlp, gate_m, g_post_mix], [(D_MODEL, F32), (D_MODEL, BF16)],
                      [D_MODEL] * 5, s, deps)


def _premix_bwd(dh, dx1, x, g_pre_mix, scale_m):
    s = x.shape[0]

    def body(dh_ref, dx1_ref, x_ref, g_ref, sc_ref, gx_ref, dsc_ref, dsh_ref, dg_ref):
        xv = x_ref[...]
        r = _rms(xv)
        xn = xv * r
        dh = dh_ref[...]
        _acc_vec(dsc_ref, dh * (xn * g_ref[...]))
        _acc_vec(dsh_ref, dh)
        dn1 = dh * (1.0 + sc_ref[...])
        _acc_vec(dg_ref, dn1 * xn)
        gx_ref[...] = dx1_ref[...] + _norm_bwd(dn1 * g_ref[...], xn, r)

    return _rows_call("premix_bwd", body, [(dh, D_MODEL, 0), (dx1, D_MODEL, 0), (x, D_MODEL, 0)],
                      [g_pre_mix, scale_m], [(D_MODEL, F32)], [D_MODEL] * 3, s)


def _split3(v):
    hi = v.astype(BF16)
    r1 = v - hi.astype(F32)
    mid = r1.astype(BF16)
    lo = (r1 - mid.astype(F32)).astype(BF16)
    return hi, mid, lo


def _dot_exact01(v, tri, contract=NN, tri_first=False):
    acc = None
    for part in _split3(v):
        lhs, rhs = (tri, part) if tri_first else (part, tri)
        p = lax.dot_general(lhs, rhs, (contract, ((), ())), preferred_element_type=F32)
        acc = p if acc is None else acc + p
    return acc


def _log_sigmoid(z):
    return jnp.minimum(z, 0.0) - jnp.log(1.0 + jnp.exp(-jnp.abs(z)))


def _fox_cum(small, bvec):
    s = small.shape[0]
    t = _tile(s, CUM_T)

    def body(sm_ref, b_ref, out_ref, carry):
        @pl.when(pl.program_id(0) == 0)
        def _():
            carry[...] = jnp.zeros_like(carry)

        lf = _log_sigmoid(sm_ref[...] + b_ref[...])
        lft = lf.T[0:FOX_HEADS, :]
        row = lax.broadcasted_iota(jnp.int32, (t, t), 0)
        col = lax.broadcasted_iota(jnp.int32, (t, t), 1)
        upper = (row <= col).astype(BF16)
        cum = _dot_exact01(lft, upper) + carry[:, 0:1]
        out_ref[...] = cum
        carry[...] = carry[...] + jnp.sum(lft, axis=1, keepdims=True)

    return _call(body, name="fox_cum", grid=(s // t,),
                 in_specs=[pl.BlockSpec((t, W_SMALL), lambda i: (i, 0)), pl.BlockSpec((1, W_SMALL), lambda i: (0, 0))],
                 out_specs=pl.BlockSpec((FOX_HEADS, t), lambda i: (0, i)),
                 out_shape=jax.ShapeDtypeStruct((FOX_HEADS, s), F32),
                 scratch_shapes=[pltpu.VMEM((FOX_HEADS, 128), F32)],
                 compiler_params=_params(("arbitrary",)))(small, bvec)


def _fox_cum_bwd(dc, dcq, small, bvec):
    s = small.shape[0]
    t = _tile(s, CUM_T)
    nb = s // t

    def body(dc_ref, dcq_ref, sm_ref, b_ref, out_ref, db_ref, carry):
        @pl.when(pl.program_id(0) == 0)
        def _():
            carry[...] = jnp.zeros_like(carry)
            db_ref[...] = jnp.zeros_like(db_ref)

        lane = lax.broadcasted_iota(jnp.int32, (t, W_SMALL), 1)
        dcq = jnp.zeros((t, W_SMALL), F32)
        for hh in range(FOX_HEADS):
            dcq = jnp.where(lane == hh, dcq_ref[hh], dcq)
        dcv = dc_ref[...] + dcq.T[0:FOX_HEADS, :]
        row = lax.broadcasted_iota(jnp.int32, (t, t), 0)
        col = lax.broadcasted_iota(jnp.int32, (t, t), 1)
        lower = (row >= col).astype(BF16)
        dlf = _dot_exact01(dcv, lower) + carry[:, 0:1]
        carry[...] = carry[...] + jnp.sum(dcv, axis=1, keepdims=True)
        z = sm_ref[...] + b_ref[...]
        zt = z.T[0:FOX_HEADS, :]
        dff = dlf * _sigmoid(-zt)
        db_ref[...] = db_ref[...] + jnp.sum(dff, axis=1, keepdims=True)
        full = jnp.concatenate([dff, jnp.zeros((W_SMALL - FOX_HEADS, t), F32)], axis=0)
        out_ref[...] = full.T

    return _call(body, name="fox_cum_bwd", grid=(nb,),
                 in_specs=[pl.BlockSpec((FOX_HEADS, t), lambda i: (0, nb - 1 - i)),
                           pl.BlockSpec((FOX_HEADS, t, 1), lambda i: (0, nb - 1 - i, 0)),
                           pl.BlockSpec((t, W_SMALL), lambda i: (nb - 1 - i, 0)),
                           pl.BlockSpec((1, W_SMALL), lambda i: (0, 0))],
                 out_specs=[pl.BlockSpec((t, W_SMALL), lambda i: (nb - 1 - i, 0)),
                            pl.BlockSpec((FOX_HEADS, 128), lambda i: (0, 0))],
                 out_shape=[jax.ShapeDtypeStruct((s, W_SMALL), F32), jax.ShapeDtypeStruct((FOX_HEADS, 128), F32)],
                 scratch_shapes=[pltpu.VMEM((FOX_HEADS, 128), F32)],
                 compiler_params=_params(("arbitrary",)))(dc, dcq, small, bvec)


FOX_SCALE = FOX_HEAD_DIM ** -0.5


def _fox_fwd(pm, crow):
    s = pm.shape[0]
    t = _tile(s, FOX_T)
    nb = s // t
    parts = 2
    hq = t // parts

    def body(q_ref, k_ref, v_ref, c_ref, o_ref, lse_ref):
        i = pl.program_id(1)
        qs = [q_ref[g * hq:(g + 1) * hq, :] for g in range(parts)]

        def block(j, carry, diagonal):
            rows = pl.ds(pl.multiple_of(j * t, t), t)
            k_all, v_all, c_all = k_ref[rows, :], v_ref[rows, :], c_ref[j]
            out = []
            for g, (m_prev, l_prev, acc) in enumerate(carry):
                nk = (g + 1) * hq if diagonal else t
                kb, vb, cb = k_all[:nk], v_all[:nk], c_all[:, :nk]
                sc = lax.dot_general(qs[g], kb, (NT, ((), ())), preferred_element_type=F32)
                sc = sc * FOX_SCALE - cb
                if diagonal:
                    row = lax.broadcasted_iota(jnp.int32, (hq, nk), 0) + g * hq
                    col = lax.broadcasted_iota(jnp.int32, (hq, nk), 1)
                    sc = jnp.where(row >= col, sc, NEG)
                m_new = jnp.maximum(m_prev, jnp.max(sc, axis=1, keepdims=True))
                alpha = jnp.exp(m_prev - m_new)
                p = jnp.exp(sc - m_new)
                l_new = alpha * l_prev + jnp.sum(p, axis=1, keepdims=True)
                pv = jnp.dot(p.astype(BF16), vb, preferred_element_type=F32)
                out.append((m_new, l_new, alpha * acc + pv))
            return tuple(out)

        init = tuple((jnp.full((hq, 1), NEG, F32), jnp.zeros((hq, 1), F32), jnp.zeros((hq, 128), F32))
                     for _ in range(parts))
        carry = lax.fori_loop(0, i, lambda j, cr: block(j, cr, False), init)
        carry = block(i, carry, True)
        for g, (m_fin, l_fin, acc) in enumerate(carry):
            o_ref[g * hq:(g + 1) * hq, :] = acc / l_fin
            lse_ref[g * hq:(g + 1) * hq, :] = m_fin + jnp.log(l_fin)

    return _call(
        body, name="fox_fwd", grid=(FOX_HEADS, nb),
        in_specs=[pl.BlockSpec((t, 128), lambda h, i: (i, h)),
                  pl.BlockSpec((s, 128), lambda h, i: (0, FOX_HEADS + h)),
                  pl.BlockSpec((s, 128), lambda h, i: (0, 2 * FOX_HEADS + h)),
                  pl.BlockSpec((None, nb, 1, t), lambda h, i: (h, 0, 0, 0))],
        out_specs=[pl.BlockSpec((t, 128), lambda h, i: (i, h)),
                   pl.BlockSpec((None, t, 1), lambda h, i: (h, i, 0))],
        out_shape=[jax.ShapeDtypeStruct((s, FOX_HEADS * 128), F32), jax.ShapeDtypeStruct((FOX_HEADS, s, 1), F32)],
        compiler_params=_params(("parallel", "arbitrary")),
    )(pm, pm, pm, crow.reshape(FOX_HEADS, nb, 1, t))


def _fox_bwd(pm, crow, o, lse, do):
    s = pm.shape[0]
    t = _tile(s, FOX_T)
    nb = s // t

    parts = 2
    hq = t // parts

    def body(q_ref, do_ref, o_ref, lse_ref, k_ref, v_ref, c_ref, dq_ref, dk_ref, dv_ref, dc_ref, dcq_ref, delta_s):
        j = pl.program_id(1)

        @pl.when(j == 0)
        def _():
            dq_ref[...] = jnp.zeros_like(dq_ref)
            dcq_ref[...] = jnp.zeros_like(dcq_ref)
            delta_s[...] = jnp.sum(do_ref[...].astype(F32) * o_ref[...], axis=1, keepdims=True)

        k_all, v_all, c_all = k_ref[...], v_ref[...], c_ref[...]

        def grow(acc, part, axis):
            n = part.shape[axis]
            if n == acc.shape[axis]:
                return acc + part
            if axis == 0:
                return jnp.concatenate([acc[:n] + part, acc[n:]], axis=0)
            return jnp.concatenate([acc[:, :n] + part, acc[:, n:]], axis=1)

        def block(i, carry, diagonal):
            dk_acc, dv_acc, dc_acc = carry
            for g in range(parts):
                nk = (g + 1) * hq if diagonal else t
                kb, vb, cb = k_all[:nk], v_all[:nk], c_all[:, :nk]
                rows = pl.ds(pl.multiple_of(i * t + g * hq, hq), hq)
                q, dov = q_ref[rows, :], do_ref[rows, :]
                sc = lax.dot_general(q, kb, (NT, ((), ())), preferred_element_type=F32)
                p = jnp.exp(sc * FOX_SCALE - cb - lse_ref[rows, :])
                if diagonal:
                    row = lax.broadcasted_iota(jnp.int32, (hq, nk), 0) + g * hq
                    col = lax.broadcasted_iota(jnp.int32, (hq, nk), 1)
                    p = jnp.where(row >= col, p, 0.0)
                dp = lax.dot_general(dov, vb, (NT, ((), ())), preferred_element_type=F32)
                ds = p * (dp - delta_s[rows, :])
                dsb = ds.astype(BF16)
                dv_acc = grow(dv_acc, lax.dot_general(p.astype(BF16), dov, (TN, ((), ())),
                                                      preferred_element_type=F32), 0)
                dk_acc = grow(dk_acc, lax.dot_general(dsb, q, (TN, ((), ())), preferred_element_type=F32), 0)
                dq_ref[rows, :] += jnp.dot(dsb, kb, preferred_element_type=F32) * FOX_SCALE
                dc_acc = grow(dc_acc, -jnp.sum(ds, axis=0, keepdims=True), 1)
                dcq_ref[rows, :] += jnp.sum(ds, axis=1, keepdims=True)
            return dk_acc, dv_acc, dc_acc

        carry = (jnp.zeros((t, 128), F32), jnp.zeros((t, 128), F32), jnp.zeros((1, t), F32))
        carry = block(j, carry, True)
        dk_acc, dv_acc, dc_acc = lax.fori_loop(j + 1, nb, lambda i, cr: block(i, cr, False), carry)
        dk_ref[...] = dk_acc * FOX_SCALE
        dv_ref[...] = dv_acc
        dc_ref[...] = dc_acc

    whole = lambda h, j: (0, h)
    return _call(
        body, name="fox_bwd", grid=(FOX_HEADS, nb),
        in_specs=[pl.BlockSpec((s, 128), whole), pl.BlockSpec((s, 128), whole), pl.BlockSpec((s, 128), whole),
                  pl.BlockSpec((None, s, 1), lambda h, j: (h, 0, 0)),
                  pl.BlockSpec((t, 128), lambda h, j: (j, FOX_HEADS + h)),
                  pl.BlockSpec((t, 128), lambda h, j: (j, 2 * FOX_HEADS + h)),
                  pl.BlockSpec((None, 1, t), lambda h, j: (h, 0, j))],
        out_specs=[pl.BlockSpec((s, 128), whole),
                   pl.BlockSpec((t, 128), lambda h, j: (j, h)),
                   pl.BlockSpec((t, 128), lambda h, j: (j, h)),
                   pl.BlockSpec((None, 1, t), lambda h, j: (h, 0, j)),
                   pl.BlockSpec((None, s, 1), lambda h, j: (h, 0, 0))],
        out_shape=[jax.ShapeDtypeStruct((s, 1024), F32), jax.ShapeDtypeStruct((s, 1024), F32),
                   jax.ShapeDtypeStruct((s, 1024), F32), jax.ShapeDtypeStruct((FOX_HEADS, 1, s), F32),
                   jax.ShapeDtypeStruct((FOX_HEADS, s, 1), F32)],
        scratch_shapes=[pltpu.VMEM((s, 1), F32)],
        compiler_params=_params(("parallel", "arbitrary")),
    )(pm, do, o, lse, pm, pm, crow)


GLA_SCALE = GLA_DK ** -0.5
GLA_Q_BLK = 3072 // 128
GLA_K_BLK = 3584 // 128
GLA_V_BLK = 4096 // 256


def _gla_gate(sm, wa_ref, b_ref):
    return jnp.dot(sm.astype(BF16), wa_ref[...], preferred_element_type=F32) + b_ref[...]


def _chunk_tri(n, kind):
    row = lax.broadcasted_iota(jnp.int32, (n, n), 0)
    col = lax.broadcasted_iota(jnp.int32, (n, n), 1)
    shift = CHUNK.bit_length() - 1
    same = (row >> shift) == (col >> shift)
    if kind == "upto":
        same = same & (row >= col)
    elif kind == "before":
        same = same & (row > col)
    return same.astype(BF16)


def _gla_fwd(pm, small, wa_pad, b_a2, deps=()):
    s = pm.shape[0]
    r = _tile(s, GLA_R)
    nc = r // CHUNK

    def body(q_ref, k_ref, v_ref, sm_ref, wa_ref, b_ref, o_ref, st_ref, state):
        @pl.when(pl.program_id(1) == 0)
        def _():
            state[...] = jnp.zeros_like(state)

        la_all = _log_sigmoid(_gla_gate(sm_ref[...], wa_ref, b_ref)) * (1.0 / GLA_TEMP)
        tri = _chunk_tri(CHUNK, "upto")
        uts, decays = [], []
        for c in range(nc):
            rows = slice(c * CHUNK, (c + 1) * CHUNK)
            la = la_all[rows]
            cum = _dot_exact01(la, tri, tri_first=True)
            total = jnp.sum(la, axis=0, keepdims=True)
            kdec = k_ref[rows, :].astype(F32) * jnp.exp(total - cum)
            uts.append(lax.dot_general(v_ref[rows, :], kdec.astype(BF16), (TN, ((), ())),
                                       preferred_element_type=F32))
            decays.append(jnp.exp(total))
        cur = state[...]
        ends = []
        for c in range(nc):
            cur = cur * decays[c] + uts[c]
            ends.append(cur.astype(BF16))
        state[...] = cur
        for c in range(nc):
            rows = slice(c * CHUNK, (c + 1) * CHUNK)
            st_ref[c] = ends[c]
            qs = (q_ref[rows, :].astype(F32) * GLA_SCALE).astype(BF16)
            o_ref[rows, :] = lax.dot_general(qs, ends[c], (NT, ((), ())), preferred_element_type=F32)

    return _call(
        body, deps=deps, name="gla_fwd", grid=(GLA_HEADS, s // r),
        in_specs=[pl.BlockSpec((r, 128), lambda h, i: (i, GLA_Q_BLK + h)),
                  pl.BlockSpec((r, 128), lambda h, i: (i, GLA_K_BLK + h)),
                  pl.BlockSpec((r, 256), lambda h, i: (i, GLA_V_BLK + h)),
                  pl.BlockSpec((r, W_SMALL), lambda h, i: (i, 0)),
                  pl.BlockSpec((W_SMALL, 128), lambda h, i: (0, h)),
                  pl.BlockSpec((1, 128), lambda h, i: (0, h))],
        out_specs=[pl.BlockSpec((r, 256), lambda h, i: (i, h)),
                   pl.BlockSpec((nc, None, GLA_DV, GLA_DK), lambda h, i: (i, h, 0, 0))],
        out_shape=[jax.ShapeDtypeStruct((s, 1024), F32),
                   jax.ShapeDtypeStruct((s // CHUNK, GLA_HEADS, GLA_DV, GLA_DK), BF16)],
        scratch_shapes=[pltpu.VMEM((GLA_DV, GLA_DK), F32)],
        compiler_params=_params(("parallel", "arbitrary")),
    )(pm, pm, pm, small, wa_pad, b_a2)


def _gla_bwd(pm, small, wa_pad, b_a2, states, do):
    s = pm.shape[0]
    r = _tile(s, GLA_R)
    nc = r // CHUNK
    nb = s // r

    def body(q_ref, k_ref, v_ref, sm_ref, wa_ref, b_ref, do_ref, st_ref, prev_ref,
             dq_ref, dk_ref, dv_ref, dza_ref, db_ref, carry):
        step = pl.program_id(1)

        @pl.when(step == 0)
        def _():
            carry[...] = jnp.zeros_like(carry)
            db_ref[...] = jnp.zeros_like(db_ref)

        z_all = _gla_gate(sm_ref[...], wa_ref, b_ref)
        la_all = _log_sigmoid(z_all) * (1.0 / GLA_TEMP)
        tri = _chunk_tri(CHUNK, "upto")
        tri_strict = _chunk_tri(CHUNK, "before")
        ws, decays, kdecs, gouts = [], [], [], []
        for c in range(nc):
            rows = slice(c * CHUNK, (c + 1) * CHUNK)
            la = la_all[rows]
            cum = _dot_exact01(la, tri, tri_first=True)
            total = jnp.sum(la, axis=0, keepdims=True)
            w = jnp.exp(total - cum)
            ws.append(w)
            decays.append(jnp.exp(total))
            kdecs.append(k_ref[rows, :].astype(F32) * w)
            dov = do_ref[rows, :]
            qs = (q_ref[rows, :].astype(F32) * GLA_SCALE).astype(BF16)
            dq_ref[rows, :] = jnp.dot(dov, st_ref[c], preferred_element_type=F32) * GLA_SCALE
            gouts.append(lax.dot_general(dov, qs, (TN, ((), ())), preferred_element_type=F32))
        cur = carry[...]
        gts = [None] * nc
        for c in reversed(range(nc)):
            gts[c] = gouts[c] + cur
            cur = gts[c] * decays[c]
        carry[...] = cur
        db = jnp.zeros((1, 128), F32)
        for c in range(nc):
            rows = slice(c * CHUNK, (c + 1) * CHUNK)
            gtb = gts[c].astype(BF16)
            dv_ref[rows, :] = lax.dot_general(kdecs[c].astype(BF16), gtb, (NT, ((), ())),
                                              preferred_element_type=F32)
            dkdec = jnp.dot(v_ref[rows, :], gtb, preferred_element_type=F32)
            dk_ref[rows, :] = dkdec * ws[c]
            e = dkdec * kdecs[c]
            if c > 0:
                prev = st_ref[c - 1].astype(F32)
            else:
                prev = jnp.where(step == nb - 1, 0.0, prev_ref[0].astype(F32))
            dtot = jnp.sum(gts[c] * prev, axis=0, keepdims=True) * decays[c]
            dla = dtot + _dot_exact01(e, tri_strict, tri_first=True)
            dza = dla * (1.0 / GLA_TEMP) * _sigmoid(-z_all[rows])
            dza_ref[rows, :] = dza.astype(BF16)
            db = db + jnp.sum(dza, axis=0, keepdims=True)
        db_ref[...] += db

    blk = lambda h, i: nb - 1 - i
    return _call(
        body, name="gla_bwd", grid=(GLA_HEADS, nb),
        in_specs=[pl.BlockSpec((r, 128), lambda h, i: (blk(h, i), GLA_Q_BLK + h)),
                  pl.BlockSpec((r, 128), lambda h, i: (blk(h, i), GLA_K_BLK + h)),
                  pl.BlockSpec((r, 256), lambda h, i: (blk(h, i), GLA_V_BLK + h)),
                  pl.BlockSpec((r, W_SMALL), lambda h, i: (blk(h, i), 0)),
                  pl.BlockSpec((W_SMALL, 128), lambda h, i: (0, h)),
                  pl.BlockSpec((1, 128), lambda h, i: (0, h)),
                  pl.BlockSpec((r, 256), lambda h, i: (blk(h, i), h)),
                  pl.BlockSpec((nc, None, GLA_DV, GLA_DK), lambda h, i: (blk(h, i), h, 0, 0)),
                  pl.BlockSpec((1, None, GLA_DV, GLA_DK),
                               lambda h, i: (jnp.maximum(blk(h, i) * nc - 1, 0), h, 0, 0))],
        out_specs=[pl.BlockSpec((r, 128), lambda h, i: (blk(h, i), h)),
                   pl.BlockSpec((r, 128), lambda h, i: (blk(h, i), h)),
                   pl.BlockSpec((r, 256), lambda h, i: (blk(h, i), h)),
                   pl.BlockSpec((r, 128), lambda h, i: (blk(h, i), h)),
                   pl.BlockSpec((1, 128), lambda h, i: (0, h))],
        out_shape=[jax.ShapeDtypeStruct((s, 512), F32), jax.ShapeDtypeStruct((s, 512), F32),
                   jax.ShapeDtypeStruct((s, 1024), F32), jax.ShapeDtypeStruct((s, 512), BF16),
                   jax.ShapeDtypeStruct((1, 512), F32)],
        scratch_shapes=[pltpu.VMEM((GLA_DV, GLA_DK), F32)],
        compiler_params=_params(("parallel", "arbitrary")),
    )(pm, pm, pm, small, wa_pad, b_a2, do, states, states)


def _modulation(c_all, w_ada):
    n = w_ada.shape[1]
    tn = _tile(n, 512)

    def body(c_ref, w_ref, out_ref, ca_ref):
        cv = c_ref[...]
        ca = cv * _sigmoid(cv)
        ca_ref[...] = ca
        out_ref[...] = jnp.dot(ca.astype(BF16), w_ref[...].astype(BF16), preferred_element_type=F32)

    return _call(body, name="modulation", grid=(n // tn,),
                 in_specs=[pl.BlockSpec((N_DEV, D_MODEL), lambda j: (0, 0)),
                           pl.BlockSpec((D_MODEL, tn), lambda j: (0, j))],
                 out_specs=[pl.BlockSpec((N_DEV, tn), lambda j: (0, j)),
                            pl.BlockSpec((N_DEV, D_MODEL), lambda j: (0, 0))],
                 out_shape=[jax.ShapeDtypeStruct((N_DEV, n), F32), jax.ShapeDtypeStruct((N_DEV, D_MODEL), F32)],
                 compiler_params=_params(("arbitrary",)))(c_all, w_ada)


def _adamw_math(w, g, m, v):
    m = ADAM_B1 * m + (1.0 - ADAM_B1) * g
    v = ADAM_B2 * v + (1.0 - ADAM_B2) * (g * g)
    m_hat = m / (1.0 - ADAM_B1 ** ADAM_STEP)
    v_hat = v / (1.0 - ADAM_B2 ** ADAM_STEP)
    delta = -ADAM_LR * (m_hat / (jnp.sqrt(v_hat) + ADAM_EPS) + ADAM_WD * w)
    return delta, m, v


def _adamw_slabs(name, w, slabs, m, v, tr=256):
    rr, cc = w.shape

    def body(w_ref, s_ref, m_ref, v_ref, g_ref, d_ref, nm_ref, nv_ref):
        g = s_ref[0].astype(F32)
        for r in range(1, N_DEV):
            g = g + s_ref[r].astype(F32)
        g_ref[...] = g
        d, nm, nv = _adamw_math(w_ref[...], g, m_ref[...], v_ref[...])
        d_ref[...] = d
        nm_ref[...] = nm
        nv_ref[...] = nv

    steps, spec, slab_spec = _plane_tiles(rr, cc, tr)
    return _call(body, name=name, grid=(steps,),
                 in_specs=[spec, slab_spec, spec, spec],
                 out_specs=[spec] * 4, out_shape=[jax.ShapeDtypeStruct((rr, cc), F32)] * 4,
                 compiler_params=_params(("parallel",)))(w, slabs, m, v)


def _adamw_ada(w, cat, dm, m, v, tr=256):
    rr, cc = w.shape
    tr = _tile(rr, tr)

    def body(w_ref, ca_ref, dm_ref, m_ref, v_ref, g_ref, d_ref, nm_ref, nv_ref):
        g = ca_ref[:, 0:1] * dm_ref[0:1, :]
        for b in range(1, N_DEV):
            g = g + ca_ref[:, b:b + 1] * dm_ref[b:b + 1, :]
        g_ref[...] = g
        d, nm, nv = _adamw_math(w_ref[...], g, m_ref[...], v_ref[...])
        d_ref[...] = d
        nm_ref[...] = nm
        nv_ref[...] = nv

    spec = pl.BlockSpec((tr, cc), lambda i: (i, 0))
    return _call(body, name="adamw_ada", grid=(rr // tr,),
                 in_specs=[spec, pl.BlockSpec((tr, N_DEV), lambda i: (i, 0)),
                           pl.BlockSpec((N_DEV, cc), lambda i: (0, 0)), spec, spec],
                 out_specs=[spec] * 4, out_shape=[jax.ShapeDtypeStruct((rr, cc), F32)] * 4,
                 compiler_params=_params(("parallel",)))(w, cat, dm, m, v)


def _sum_devices(gathered):
    ln = gathered.shape[-1]

    def body(g_ref, out_ref):
        acc = g_ref[0]
        for r in range(1, N_DEV):
            acc = acc + g_ref[r]
        out_ref[...] = acc

    return _call(body, name="sum_devices",
                 in_specs=[pl.BlockSpec(memory_space=pltpu.VMEM)], out_specs=pl.BlockSpec(memory_space=pltpu.VMEM),
                 out_shape=jax.ShapeDtypeStruct((1, ln), F32))(gathered)


def _adamw_flat(w, g, m, v):
    def body(w_ref, g_ref, m_ref, v_ref, d_ref, nm_ref, nv_ref):
        d, nm, nv = _adamw_math(w_ref[...], g_ref[...], m_ref[...], v_ref[...])
        d_ref[...] = d
        nm_ref[...] = nm
        nv_ref[...] = nv

    vm = pl.BlockSpec(memory_space=pltpu.VMEM)
    return _call(body, name="adamw_small", in_specs=[vm] * 4, out_specs=[vm] * 3,
                 out_shape=[jax.ShapeDtypeStruct(w.shape, F32)] * 3)(w, g, m, v)


def _from_col_shards(g):
    return jnp.transpose(g, (1, 0, 2)).reshape(g.shape[1], N_DEV * g.shape[2])


def _pad_lanes(v, n):
    return jnp.concatenate([v, jnp.zeros(v.shape[:-1] + (n - v.shape[-1],), v.dtype)], axis=-1)


def kernel(x, c, w_ada, b_ada, g_pre_mix, g_post_mix, w_in, b_fgate, w_gla_a2, b_gla_a2, g_fox_out, g_gla_out, w_out, g_pre_mlp, g_post_mlp, w_mlp_in, w_mlp_out, loss_target, m_w_ada, m_b_ada, m_g_pre_mix, m_g_post_mix, m_w_in, m_b_fgate, m_w_gla_a2, m_b_gla_a2, m_g_fox_out, m_g_gla_out, m_w_out, m_g_pre_mlp, m_g_post_mlp, m_w_mlp_in, m_w_mlp_out, v_w_ada, v_b_ada, v_g_pre_mix, v_g_post_mix, v_w_in, v_b_fgate, v_w_gla_a2, v_b_gla_a2, v_g_fox_out, v_g_gla_out, v_w_out, v_g_pre_mlp, v_g_post_mlp, v_w_mlp_in, v_w_mlp_out):
    rank = _my_rank()
    xs = x[0]
    s = xs.shape[0]
    target = loss_target[0]

    w_in_t, m_in_t, v_in_t = w_in[0].T, m_w_in[0].T, v_w_in[0].T
    c_all, wa2_g, ggla_g, win_g = _all_gather("gather_first", [c, w_gla_a2[0], g_gla_out[0], w_in_t.astype(BF16)])
    rest = [_own_slot("own_w_out", w_out[0], True, rank), _own_slot("own_w_mlp_in", w_mlp_in[0], True, rank)]
    gs_send, gs_sib, gs_ici, gs_land, gs_token = _gather2_start("gather_rest_start", rest, after=(c_all,))
    last = [_own_slot("own_w_mlp_out", w_mlp_out[0], True, rank)]
    gl_send, gl_sib, gl_ici, gl_land, gl_token = _gather2_start("gather_last_start", last, after=(gs_token,))
    w_a2 = _from_col_shards(wa2_g)
    g_gla = _from_col_shards(ggla_g).reshape(1, 1024)
    g_fox = g_fox_out.reshape(1, 1024)
    win_full = win_g.reshape(N_DEV * 771, D_MODEL)
    w_main = jnp.concatenate([win_full[:3072], win_full[3080:5128], win_full[5144:6168]], axis=0)
    w_small = jnp.concatenate([win_full[3072:3080], win_full[5128:5144],
                               jnp.zeros((W_SMALL - 24, D_MODEL), BF16)], axis=0)
    wa_pad =jnp.concatenate([jnp.zeros((8, 512), BF16), w_a2.astype(BF16), jnp.zeros((104, 512), BF16)], axis=0)
    bf_vec = _pad_lanes(b_fgate, W_SMALL)

    mod_part, c_act = _modulation(c_all.reshape(N_DEV, D_MODEL), w_ada[0])
    (mod_g,) = _all_gather("gather_mod", [mod_part])
    mod = lax.dynamic_slice_in_dim(mod_g, rank, 1, axis=1).reshape(1, 6 * D_MODEL) + b_ada
    shift_m, scale_m, gate_m, shift_f, scale_f, gate_f = [mod[:, i * D_MODEL:(i + 1) * D_MODEL] for i in range(6)]

    h = _premix(xs, g_pre_mix, scale_m, shift_m, deps=(gl_token,))
    pm = _mm_plain("proj_main", h, w_main, NT, BF16)
    small = _mm_plain("proj_small", h, w_small, NT, F32)
    crow = _fox_cum(small, bf_vec).reshape(FOX_HEADS, 1, s)
    o_fox, lse = _fox_fwd(pm, crow)
    gs_fsend, gs_frecv, gs_land, gs_ftoken = _gather2_forward("gather_rest_forward", gs_land, gs_ici, o_fox)
    o_gla, states = _gla_fwd(pm, small, wa_pad, b_gla_a2, deps=(gs_ftoken,))
    mix = _mix_fwd(o_fox, o_gla, pm, g_fox, g_gla)
    wout_g, wmi_g = _gather2_wait("gather_rest_wait", gs_land, gs_send, gs_sib, gs_fsend, gs_frecv, mix)
    w_out_full = wout_g.reshape(D_MODEL, D_MODEL)
    y = _mm_plain("out_proj", mix, w_out_full, NN, F32)
    x1, h2 = _postmix_premlp(xs, y, gate_m, g_post_mix, g_pre_mlp, scale_f, shift_f)
    gl_fsend, gl_frecv, gl_land, gl_ftoken = _gather2_forward("gather_last_forward", gl_land, gl_ici, h2)

    tm, tn, tk = _tile(s, 1024), 1024, 2048
    nsh = 1024 // tn

    def relu2(acc):
        rl = jnp.maximum(acc, 0.0)
        return rl * rl, rl

    z, a_relu = _matmul(
        "mlp_in", h2, wmi_g, contract=NN, grid=(s // tm, D_FF // tn, D_MODEL // tk),
        a_spec=pl.BlockSpec((tm, tk), lambda i, j, k: (i, k)),
        b_spec=pl.BlockSpec((None, tk, tn), lambda i, j, k: (j // nsh, k, j % nsh)),
        out_specs=[pl.BlockSpec((tm, tn), lambda i, j, k: (i, j))] * 2,
        out_shapes=[jax.ShapeDtypeStruct((s, D_FF), BF16)] * 2, acc_shape=(tm, tn), epilogue=relu2,
        deps=(gl_ftoken,))
    (wmo_g,) = _gather2_wait("gather_last_wait", gl_land, gl_send, gl_sib, gl_fsend, gl_frecv, z)
    w_mo_full = wmo_g.reshape(D_FF, D_MODEL)
    y2 = _mm_plain("mlp_out", z, w_mo_full, NN, F32)

    dx2, dy2, loss_vec, dgate_f, dg_post_mlp = _loss_postmlp_bwd(x1, y2, target, gate_f, g_post_mlp)
    loss = lax.psum(loss_vec[0, 0], ("x", "y", "c"))

    da = _mm_plain("mlp_out_dx", dy2, w_mo_full, NT, BF16, extra=(a_relu,),
                   epilogue=lambda acc, rl: (acc * (2.0 * rl.astype(F32)),))
    dw_mo = _mm_plain("mlp_out_dw", z, dy2, TN, BF16)
    dw_mo = dw_mo.reshape(N_DEV, 1024, D_MODEL)
    x_mo = _exchange_start("grad_mlp_out_start", [_own_slot("own_dw_mlp_out", dw_mo, False, rank)], [dw_mo])
    tkx = 1024
    (dh2,) = _matmul(
        "mlp_in_dx", da, wmi_g, contract=NT, grid=(s // tm, D_MODEL // tn, D_FF // tkx),
        a_spec=pl.BlockSpec((tm, tkx), lambda i, j, k: (i, k)),
        b_spec=pl.BlockSpec((None, tn, tkx), lambda i, j, k: (k, j, 0)),
        out_specs=[pl.BlockSpec((tm, tn), lambda i, j, k: (i, j))],
        out_shapes=[jax.ShapeDtypeStruct((s, D_MODEL), F32)], acc_shape=(tm, tn), deps=(x_mo[4],))
    ts = _tile(s, 2048)
    (dw_mi,) = _matmul(
        "mlp_in_dw", h2, da, contract=TN, grid=(D_MODEL // 1024, D_FF // tn, s // ts),
        a_spec=pl.BlockSpec((ts, 1024), lambda i, j, k: (k, i)),
        b_spec=pl.BlockSpec((ts, tn), lambda i, j, k: (k, j)),
        out_specs=[pl.BlockSpec((None, 1024, tn), lambda i, j, k: (j // nsh, i, j % nsh))],
        out_shapes=[jax.ShapeDtypeStruct((N_DEV, D_MODEL, 1024), BF16)], acc_shape=(1024, tn))
    x_mi = _exchange_start("grad_mlp_in_start", [_own_slot("own_dw_mlp_in", dw_mi, False, rank)], [dw_mi])

    dx1, dy, dscale_f, dshift_f, dg_pre_mlp, dgate_m, dg_post_mix = _premlp_postmix_bwd(
        dh2, dx2, x1, y, scale_f, g_pre_mlp, gate_m, g_post_mix, deps=(x_mi[4],))

    dmix = _mm_plain("out_proj_dx", dy, w_out_full, NT, F32)
    dw_out = _mm_plain("out_proj_dw", mix, dy, TN, BF16)
    dw_out = dw_out.reshape(N_DEV, 256, D_MODEL)
    x_out = _exchange_start("grad_out_start", [_own_slot("own_dw_out", dw_out, False, rank)], [dw_out])
    do_fox, do_gla, dgr, dg_fox, dg_gla = _mix_bwd(dmix, o_fox, o_gla, pm, g_fox, g_gla, deps=(x_out[4],))

    dq, dk, dv, dc, dcq = _fox_bwd(pm, crow, o_fox, lse, do_fox)
    dsmall_f, db_f = _fox_cum_bwd(dc.reshape(FOX_HEADS, s), dcq, small, bf_vec)
    dgq, dgk, dgv, dza, db_a2 = _gla_bwd(pm, small, wa_pad, b_gla_a2, states, do_gla)
    dsmall = _mm_plain("gate_dx", dza, wa_pad, NT, F32, tn=128, extra=(dsmall_f,),
                       epilogue=lambda acc, other: (acc + other,))
    dwa_pad = _mm_plain("gate_dw", small, dza, TN, F32, tm=128, tn=512)

    dpm = jnp.concatenate([dq.astype(BF16), dk.astype(BF16), dv.astype(BF16), dgq.astype(BF16), dgk.astype(BF16),
                           dgv.astype(BF16), dgr], axis=1)
    dw_main = _mm_plain("proj_main_dw", dpm, h, TN, BF16)
    dw_small = _mm_plain("proj_small_dw", dsmall, h, TN, BF16, tm=128)
    dwin_full = jnp.concatenate([dw_main[:3072], dw_small[0:8], dw_main[3072:5120], dw_small[8:24],
                                 dw_main[5120:6144]], axis=0)
    dwin_slabs = dwin_full.reshape(N_DEV, 771, D_MODEL)
    x_in = _exchange_start("grad_in_start", [_own_slot("own_dw_in", dwin_slabs, False, rank)], [dwin_slabs])
    dh_small = _mm_plain("proj_small_dx", dsmall, w_small, NN, F32, tk=128)
    dh = _mm_plain("proj_main_dx", dpm, w_main, NN, F32, extra=(dh_small,),
                   epilogue=lambda acc, other: (acc + other,), deps=(x_in[4],))
    grad_x, dscale_m, dshift_m, dg_pre_mix = _premix_bwd(dh, dx1, xs, g_pre_mix, scale_m)

    dmod = jnp.concatenate([dshift_m, dscale_m, dgate_m, dshift_f, dscale_f, dgate_f], axis=1)
    flat = jnp.concatenate(
        [dmod, dg_pre_mix, dg_post_mix, dg_fox, dg_pre_mlp, dg_post_mlp, db_a2,
         dwa_pad[8:24, :].reshape(1, GLA_RANK * 512), dg_gla, _pad_lanes(db_f[:, 0].reshape(1, FOX_HEADS), 128)],
        axis=1)

    (r_mo,) = _exchange_wait("grad_mlp_out_wait", *x_mo[:4], grad_x)
    g_mo, d_mo, nm_mo, nv_mo = _adamw_slabs("adamw_w_mlp_out", w_mlp_out[0], r_mo, m_w_mlp_out[0], v_w_mlp_out[0])
    (r_mi,) = _exchange_wait("grad_mlp_in_wait", *x_mi[:4], g_mo)
    g_mi, d_mi, nm_mi, nv_mi = _adamw_slabs("adamw_w_mlp_in", w_mlp_in[0], r_mi, m_w_mlp_in[0], v_w_mlp_in[0])
    (r_out,) = _exchange_wait("grad_out_wait", *x_out[:4], g_mi)
    g_out, d_out, nm_out, nv_out = _adamw_slabs("adamw_w_out", w_out[0], r_out, m_w_out[0], v_w_out[0])

    (flat_g,) = _all_gather("gather_small_grads", [flat], deps=(g_out,))
    tot = _sum_devices(flat_g)
    dm_cols = lax.dynamic_slice_in_dim(flat_g[:, 0, :6 * D_MODEL], rank * 1536, 1536, axis=1)
    g_ada, d_ada, nm_ada, nv_ada = _adamw_ada(w_ada[0], c_act.T, dm_cols, m_w_ada[0], v_w_ada[0])
    (r_in,) = _exchange_wait("grad_in_wait", *x_in[:4], g_ada)
    g_in, d_in, nm_in, nv_in = [a.T for a in _adamw_slabs("adamw_w_in", w_in_t, r_in, m_in_t, v_in_t)]

    o = 0
    seg = {}
    for name, n in (("b_ada", 12288), ("g_pre_mix", 2048), ("g_post_mix", 2048), ("g_fox_out", 1024),
                    ("g_pre_mlp", 2048), ("g_post_mlp", 2048), ("b_gla_a2", 512), ("w_gla_a2", 8192),
                    ("g_gla_out", 1024), ("b_fgate", 128)):
        seg[name] = tot[:, o:o + n]
        o += n
    g_wa2 = lax.dynamic_slice_in_dim(seg["w_gla_a2"].reshape(GLA_RANK, 512), rank * 64, 64, axis=1)
    g_ggla = lax.dynamic_slice_in_dim(seg["g_gla_out"].reshape(GLA_HEADS, GLA_DV), rank * 32, 32, axis=1)
    small_names = ["b_ada", "g_pre_mix", "g_post_mix", "g_fox_out", "g_pre_mlp", "g_post_mlp", "b_gla_a2",
                   "w_gla_a2", "g_gla_out", "b_fgate"]
    small_grads = {**seg, "w_gla_a2": g_wa2.reshape(1, 1024), "g_gla_out": g_ggla.reshape(1, 128)}
    weights = dict(b_ada=b_ada, g_pre_mix=g_pre_mix, g_post_mix=g_post_mix, g_fox_out=g_fox_out,
                   g_pre_mlp=g_pre_mlp, g_post_mlp=g_post_mlp, b_gla_a2=b_gla_a2, w_gla_a2=w_gla_a2,
                   g_gla_out=g_gla_out, b_fgate=b_fgate)
    moms = dict(b_ada=m_b_ada, g_pre_mix=m_g_pre_mix, g_post_mix=m_g_post_mix, g_fox_out=m_g_fox_out,
                g_pre_mlp=m_g_pre_mlp, g_post_mlp=m_g_post_mlp, b_gla_a2=m_b_gla_a2, w_gla_a2=m_w_gla_a2,
                g_gla_out=m_g_gla_out, b_fgate=m_b_fgate)
    vels = dict(b_ada=v_b_ada, g_pre_mix=v_g_pre_mix, g_post_mix=v_g_post_mix, g_fox_out=v_g_fox_out,
                g_pre_mlp=v_g_pre_mlp, g_post_mlp=v_g_post_mlp, b_gla_a2=v_b_gla_a2, w_gla_a2=v_w_gla_a2,
                g_gla_out=v_g_gla_out, b_fgate=v_b_fgate)

    def flatten(d, fill):
        parts = []
        for nm in small_names:
            p = d[nm].reshape(1, -1)
            if nm == "b_fgate":
                p = jnp.concatenate([p[:, :FOX_HEADS], jnp.full((1, 128 - FOX_HEADS), fill, F32)], axis=1)
            parts.append(p)
        return jnp.concatenate(parts, axis=1).reshape(-1, 128)

    fw, fg, fm, fv = flatten(weights, 0.0), flatten(small_grads, 0.0), flatten(moms, 0.0), flatten(vels, 1.0)
    fd, fnm, fnv = _adamw_flat(fw, fg, fm, fv)

    def unflatten(fl):
        fl = fl.reshape(1, -1)
        out = {}
        o = 0
        for nm in small_names:
            n = 128 if nm == "b_fgate" else weights[nm].size
            piece = fl[:, o:o + n]
            if nm == "b_fgate":
                piece = piece[:, :FOX_HEADS]
            out[nm] = piece.reshape(weights[nm].shape)
            o += n
        return out

    sg, sd, snm, snv = unflatten(fg), unflatten(fd), unflatten(fnm), unflatten(fnv)

    big = dict(w_ada=(g_ada, d_ada, nm_ada, nv_ada), w_in=(g_in, d_in, nm_in, nv_in),
               w_out=(g_out, d_out, nm_out, nv_out), w_mlp_in=(g_mi, d_mi, nm_mi, nv_mi),
               w_mlp_out=(g_mo, d_mo, nm_mo, nv_mo))
    order = ["w_ada", "b_ada", "g_pre_mix", "g_post_mix", "w_in", "b_fgate", "w_gla_a2", "b_gla_a2", "g_fox_out",
             "g_gla_out", "w_out", "g_pre_mlp", "g_post_mlp", "w_mlp_in", "w_mlp_out"]

    def pick(nm, idx):
        if nm in big:
            return big[nm][idx][None]
        return (sg, sd, snm, snv)[idx][nm]

    grads = [pick(nm, 0) for nm in order]
    deltas = [pick(nm, 1) for nm in order]
    new_m = [pick(nm, 2) for nm in order]
    new_v = [pick(nm, 3) for nm in order]
    return (loss, grad_x[None], *grads, *deltas, *new_m, *new_v)
```

```python
import functools

import numpy as np
import jax
import jax.numpy as jnp
from jax import lax
from jax.experimental import pallas as pl
from jax.experimental.pallas import tpu as pltpu

F32 = jnp.float32
BF16 = jnp.bfloat16
MESH = pl.DeviceIdType.MESH
N_DEV = 8

D_MODEL = 2048
FOX_HEADS = 8
FOX_HEAD_DIM = 128
GLA_HEADS = 4
GLA_DK = 128
GLA_DV = 256
GLA_RANK = 16
GLA_TEMP = 16.0
CHUNK = 64
D_FF = 8192
W_MAIN = 6144
W_SMALL = 128
EPS = 1e-6
NEG = float(np.finfo(np.float32).min)

ADAM_LR = 0.001
ADAM_B1 = 0.9
ADAM_B2 = 0.999
ADAM_EPS = 1e-08
ADAM_WD = 0.01
ADAM_STEP = 10

ROW_T = 256
FOX_T = 1024
GLA_R = 512
CUM_T = 256
VMEM_LIMIT = 56 * 1024 * 1024


def _call(body, deps=(), **kw):
    if not deps:
        return pl.pallas_call(body, **kw)
    n_in, n_dep = len(kw["in_specs"]), len(deps)

    def with_deps(*refs):
        return body(*refs[:n_in], *refs[n_in + n_dep:])

    kw["in_specs"] = [*kw["in_specs"], *[pl.BlockSpec(memory_space=pl.ANY)] * n_dep]
    call = pl.pallas_call(with_deps, **kw)
    return lambda *args: call(*args, *deps)


def _params(sem=None):
    return pltpu.CompilerParams(dimension_semantics=sem, vmem_limit_bytes=VMEM_LIMIT)


def _my_pos():
    return lax.axis_index("x"), lax.axis_index("y"), lax.axis_index("c")


def _my_rank():
    x, y, c = _my_pos()
    return 4 * x + 2 * y + c


def _all_gather(name, arrays, deps=()):
    n = len(arrays)

    def body(*refs):
        ins = refs[:n]
        outs = refs[n:2 * n]
        send_sems, recv_sems, local_sems = refs[2 * n:]
        x, y, c = _my_pos()
        me, sibling = (x, y, c), (x, y, 1 - c)
        chips = [(1 - x, y), (x, 1 - y), (1 - x, 1 - y)]

        def slot(a, px, py, pc):
            return outs[a].at[4 * px + 2 * py + pc]

        def copy(a, k, block, to, src=None):
            return pltpu.make_async_remote_copy(
                src_ref=slot(a, *block) if src is None else src, dst_ref=slot(a, *block),
                send_sem=send_sems.at[a, k], recv_sem=recv_sems.at[a, k],
                device_id=to, device_id_type=MESH)

        started = []
        for a in range(n):
            mine = pltpu.make_async_copy(ins[a], slot(a, *me), local_sems.at[a])
            mine.start()
            started.append(mine)
        first = []
        for a in range(n):
            first.append(copy(a, 0, me, sibling, src=ins[a]))
            first += [copy(a, 1 + j, me, (*chip, c), src=ins[a]) for j, chip in enumerate(chips)]
        for cp in first:
            cp.start()
        passed = []
        for j, chip in enumerate(chips):
            for a in range(n):
                copy(a, 1 + j, (*chip, c), me).wait_recv()
                fwd = copy(a, 4 + j, (*chip, c), sibling)
                fwd.start()
                passed.append(fwd)
        for a in range(n):
            copy(a, 0, sibling, me).wait_recv()
            for j, chip in enumerate(chips):
                copy(a, 4 + j, (*chip, 1 - c), me).wait_recv()
        for cp in first + passed:
            cp.wait_send()
        for mine in started:
            mine.wait()

    hbm = pl.BlockSpec(memory_space=pltpu.HBM)
    return _call(
        body, deps=deps, name=name,
        out_shape=[jax.ShapeDtypeStruct((N_DEV,) + a.shape, a.dtype) for a in arrays],
        in_specs=[hbm] * n, out_specs=[hbm] * n,
        scratch_shapes=[pltpu.SemaphoreType.DMA((n, 7)), pltpu.SemaphoreType.DMA((n, 7)),
                        pltpu.SemaphoreType.DMA((n,))],
    )(*arrays)


def _plane_tiles(rr, cc, tr=512, tc=512):
    if rr % 8 == 0:
        tr = _tile(rr, tr)
        return (rr // tr, pl.BlockSpec((tr, cc), lambda i: (i, 0)),
                pl.BlockSpec((N_DEV, tr, cc), lambda i: (0, i, 0)))
    tc = _tile(cc, tc)
    return (cc // tc, pl.BlockSpec((rr, tc), lambda i: (0, i)),
            pl.BlockSpec((N_DEV, rr, tc), lambda i: (0, 0, i)))


def _own_slot(name, src, gather, rank):
    shape = ((N_DEV,) + src.shape) if gather else src.shape
    rr, cc = shape[1], shape[2]
    by_rows = rr % 8 == 0
    tr, tc = (_tile(rr, 512), cc) if by_rows else (rr, _tile(cc, 512))
    steps = rr // tr if by_rows else cc // tc

    def body(rank_ref, s_ref, o_ref):
        o_ref[...] = s_ref[...].astype(o_ref.dtype)

    def at(i):
        return (i, 0) if by_rows else (0, i)

    if gather:
        in_spec = pl.BlockSpec((tr, tc), lambda i, rk: at(i))
    else:
        in_spec = pl.BlockSpec((None, tr, tc), lambda i, rk: (rk[0], *at(i)))
    grid_spec = pltpu.PrefetchScalarGridSpec(
        num_scalar_prefetch=1, grid=(steps,), in_specs=[in_spec],
        out_specs=pl.BlockSpec((None, tr, tc), lambda i, rk: (rk[0], *at(i))))
    return _call(body, name=name, grid_spec=grid_spec, out_shape=jax.ShapeDtypeStruct(shape, BF16),
                 compiler_params=_params(("arbitrary",)))(jnp.reshape(rank, (1,)).astype(jnp.int32), src)


_HBM = pl.BlockSpec(memory_space=pltpu.HBM)
_SEM = pl.BlockSpec(memory_space=pltpu.SEMAPHORE)
_FLIPS = [(kx, ky, kc) for kx in (0, 1) for ky in (0, 1) for kc in (0, 1)][1:]


def _peers():
    x, y, c = _my_pos()
    out = []
    for kx, ky, kc in _FLIPS:
        px, py, pc = (1 - x if kx else x), (1 - y if ky else y), (1 - c if kc else c)
        out.append(((px, py, pc), 4 * px + 2 * py + pc))
    return out


def _exchange_copy(srcs, lands, send_sems, recv_sems, a, k, peer, peer_rank, slot):
    return pltpu.make_async_remote_copy(
        src_ref=lands[a].at[slot] if srcs is None else srcs[a].at[peer_rank],
        dst_ref=lands[a].at[slot],
        send_sem=send_sems[a].at[k], recv_sem=recv_sems[a].at[k],
        device_id=peer, device_id_type=MESH)


def _exchange_start(name, lands, srcs=None, after=()):
    n = len(lands)
    n_src = 0 if srcs is None else n
    n_in = n + n_src + len(after)

    def body(*refs):
        lnd = refs[:n]
        src = None if srcs is None else refs[n:2 * n]
        send_sems, recv_sems = refs[n_in:n_in + n], refs[n_in + n:n_in + 2 * n]
        token = refs[-1]
        me = _my_rank()
        for a in range(n):
            for k, (peer, peer_rank) in enumerate(_peers()):
                _exchange_copy(src, lnd, send_sems, recv_sems, a, k, peer, peer_rank, me).start()
        token[...] = jnp.zeros_like(token)

    sems = [pltpu.SemaphoreType.DMA((7,))] * (2 * n)
    thru = list(lands) + ([] if srcs is None else list(srcs))
    outs = pl.pallas_call(
        body, name=name,
        out_shape=(*sems, *[pltpu.HBM(t.shape, t.dtype) for t in thru], jax.ShapeDtypeStruct((8, 128), F32)),
        in_specs=[*[_HBM] * len(thru), *[pl.BlockSpec(memory_space=pl.ANY)] * len(after)],
        out_specs=(*[_SEM] * (2 * n), *[_HBM] * len(thru), pl.BlockSpec(memory_space=pltpu.VMEM)),
        input_output_aliases={i: 2 * n + i for i in range(len(thru))},
        compiler_params=pltpu.CompilerParams(has_side_effects=pltpu.SideEffectType.DATAFLOW_SIDE_EFFECTING),
    )(*[pltpu.with_memory_space_constraint(t, pltpu.HBM) for t in thru], *after)
    lands_thru = outs[2 * n:3 * n]
    srcs_thru = None if srcs is None else outs[3 * n:4 * n]
    return outs[:n], outs[n:2 * n], srcs_thru, lands_thru, outs[-1]


def _exchange_wait(name, send_sems, recv_sems, srcs, lands, after):
    n = len(lands)
    thru = list(lands) + ([] if srcs is None else list(srcs))

    def body(*refs):
        lnd = refs[:n]
        src = None if srcs is None else refs[n:2 * n]
        ssem, rsem = refs[len(thru):len(thru) + n], refs[len(thru) + n:len(thru) + 2 * n]
        for a in range(n):
            for k, (peer, peer_rank) in enumerate(_peers()):
                cp = _exchange_copy(src, lnd, ssem, rsem, a, k, peer, peer_rank, peer_rank)
                cp.wait_send()
                cp.wait_recv()

    outs = pl.pallas_call(
        body, name=name,
        out_shape=tuple(pltpu.HBM(t.shape, t.dtype) for t in thru),
        in_specs=[*[_HBM] * len(thru), *[_SEM] * (2 * n), pl.BlockSpec(memory_space=pl.ANY)],
        out_specs=tuple([_HBM] * len(thru)),
        input_output_aliases={i: i for i in range(len(thru))},
        compiler_params=pltpu.CompilerParams(has_side_effects=pltpu.SideEffectType.DATAFLOW_SIDE_EFFECTING),
    )(*thru, *send_sems, *recv_sems, after)
    return outs[:n]


_SIDE = pltpu.CompilerParams(has_side_effects=pltpu.SideEffectType.DATAFLOW_SIDE_EFFECTING)
_ANY = pl.BlockSpec(memory_space=pl.ANY)


def _chips():
    x, y, _ = _my_pos()
    return [(1 - x, y), (x, 1 - y), (1 - x, 1 - y)]


def _slot_copy(lnd, slot, send_sem, recv_sem, to):
    return pltpu.make_async_remote_copy(src_ref=lnd.at[slot], dst_ref=lnd.at[slot], send_sem=send_sem,
                                        recv_sem=recv_sem, device_id=to, device_id_type=MESH)


def _gather2_start(name, lands, after=()):
    n = len(lands)
    n_in = n + len(after)

    def body(*refs):
        lnd = refs[:n]
        send, recv_sib, recv_ici = refs[n_in:n_in + n], refs[n_in + n:n_in + 2 * n], refs[n_in + 2 * n:n_in + 3 * n]
        x, y, c = _my_pos()
        me = 4 * x + 2 * y + c
        for a in range(n):
            _slot_copy(lnd[a], me, send[a].at[0], recv_sib[a].at[0], (x, y, 1 - c)).start()
            for j, chip in enumerate(_chips()):
                _slot_copy(lnd[a], me, send[a].at[1 + j], recv_ici[a].at[j], (*chip, c)).start()
        refs[-1][...] = jnp.zeros_like(refs[-1])

    sems = [pltpu.SemaphoreType.DMA((4,))] * n + [pltpu.SemaphoreType.DMA((1,))] * n + [pltpu.SemaphoreType.DMA((3,))] * n
    outs = pl.pallas_call(
        body, name=name,
        out_shape=(*sems, *[pltpu.HBM(t.shape, t.dtype) for t in lands], jax.ShapeDtypeStruct((8, 128), F32)),
        in_specs=[*[_HBM] * n, *[_ANY] * len(after)],
        out_specs=(*[_SEM] * (3 * n), *[_HBM] * n, pl.BlockSpec(memory_space=pltpu.VMEM)),
        input_output_aliases={i: 3 * n + i for i in range(n)}, compiler_params=_SIDE,
    )(*[pltpu.with_memory_space_constraint(t, pltpu.HBM) for t in lands], *after)
    return outs[:n], outs[n:2 * n], outs[2 * n:3 * n], outs[3 * n:4 * n], outs[-1]


def _gather2_forward(name, lands, recv_ici, after):
    n = len(lands)

    def body(*refs):
        lnd, arrived = refs[:n], refs[n:2 * n]
        send, recv = refs[2 * n + 1:3 * n + 1], refs[3 * n + 1:4 * n + 1]
        x, y, c = _my_pos()
        for j, (cx, cy) in enumerate(_chips()):
            slot = 4 * cx + 2 * cy + c
            for a in range(n):
                _slot_copy(lnd[a], slot, send[a].at[j], arrived[a].at[j], (cx, cy, c)).wait_recv()
                _slot_copy(lnd[a], slot, send[a].at[j], recv[a].at[j], (x, y, 1 - c)).start()
        refs[-1][...] = jnp.zeros_like(refs[-1])

    sems = [pltpu.SemaphoreType.DMA((3,))] * (2 * n)
    outs = pl.pallas_call(
        body, name=name,
        out_shape=(*sems, *[pltpu.HBM(t.shape, t.dtype) for t in lands], jax.ShapeDtypeStruct((8, 128), F32)),
        in_specs=[*[_HBM] * n, *[_SEM] * n, _ANY],
        out_specs=(*[_SEM] * (2 * n), *[_HBM] * n, pl.BlockSpec(memory_space=pltpu.VMEM)),
        input_output_aliases={i: 2 * n + i for i in range(n)}, compiler_params=_SIDE,
    )(*lands, *recv_ici, after)
    return outs[:n], outs[n:2 * n], outs[2 * n:3 * n], outs[-1]


def _gather2_wait(name, lands, send_a, recv_sib, send_b, recv_b, after):
    n = len(lands)

    def body(*refs):
        lnd = refs[:n]
        sa, rs, sb, rb = (refs[(1 + i) * n:(2 + i) * n] for i in range(4))
        x, y, c = _my_pos()
        me = 4 * x + 2 * y + c
        for a in range(n):
            for k in range(4):
                _slot_copy(lnd[a], me, sa[a].at[k], rs[a].at[0], (x, y, 1 - c)).wait_send()
            _slot_copy(lnd[a], me - c + (1 - c), sa[a].at[0], rs[a].at[0], (x, y, 1 - c)).wait_recv()
            for j, (cx, cy) in enumerate(_chips()):
                _slot_copy(lnd[a], 4 * cx + 2 * cy + c, sb[a].at[j], rb[a].at[j], (x, y, 1 - c)).wait_send()
                _slot_copy(lnd[a], 4 * cx + 2 * cy + (1 - c), sb[a].at[j], rb[a].at[j], (x, y, 1 - c)).wait_recv()

    outs = pl.pallas_call(
        body, name=name,
        out_shape=tuple(pltpu.HBM(t.shape, t.dtype) for t in lands),
        in_specs=[*[_HBM] * n, *[_SEM] * (4 * n), _ANY],
        out_specs=tuple([_HBM] * n),
        input_output_aliases={i: i for i in range(n)}, compiler_params=_SIDE,
    )(*lands, *send_a, *recv_sib, *send_b, *recv_b, after)
    return outs


NN = ((1,), (0,))
NT = ((1,), (1,))
TN = ((0,), (0,))


def _matmul(name, a, b, *, contract, grid, a_spec, b_spec, out_specs, out_shapes, acc_shape,
            extra=(), extra_specs=(), epilogue=None, deps=()):
    nk = grid[2]
    n_extra = len(extra)
    n_out = len(out_shapes)

    def body(*refs):
        a_ref, b_ref = refs[0], refs[1]
        extra_refs = refs[2:2 + n_extra]
        out_refs = refs[2 + n_extra:2 + n_extra + n_out]
        acc_ref = refs[-1]
        k = pl.program_id(2)

        def prod():
            return lax.dot_general(a_ref[...].astype(BF16), b_ref[...].astype(BF16), (contract, ((), ())),
                                   preferred_element_type=F32)

        def finish(acc):
            res = (acc,) if epilogue is None else epilogue(acc, *[r[...] for r in extra_refs])
            for o_ref, val in zip(out_refs, res):
                o_ref[...] = val.astype(o_ref.dtype)

        if nk == 1:
            finish(prod())
            return

        @pl.when(k == 0)
        def _():
            acc_ref[...] = prod()

        @pl.when((k > 0) & (k < nk - 1))
        def _():
            acc_ref[...] += prod()

        @pl.when(k == nk - 1)
        def _():
            finish(acc_ref[...] + prod())

    outs = _call(
        body, deps=deps, name=name, grid=grid,
        in_specs=[a_spec, b_spec, *extra_specs], out_specs=list(out_specs), out_shape=list(out_shapes),
        scratch_shapes=[pltpu.VMEM(acc_shape if nk > 1 else (8, 128), F32)],
        compiler_params=_params(("parallel", "parallel", "arbitrary")),
    )(a, b, *extra)
    return outs


def _tile(n, t):
    t = min(n, t)
    assert n % t == 0, (n, t)
    return t


def _mm_plain(name, a, b, contract, out_dtype, tm=1024, tn=1024, tk=2048, extra=(), epilogue=None,
              n_out=1, out_dtypes=None, deps=()):
    if contract == NN:
        (m, kd), (_, n) = a.shape, b.shape
    elif contract == NT:
        (m, kd), (n, _) = a.shape, b.shape
    else:
        (kd, m), (_, n) = a.shape, b.shape
    tm, tn, tk = _tile(m, tm), _tile(n, tn), _tile(kd, tk)
    if contract == NN:
        a_spec = pl.BlockSpec((tm, tk), lambda i, j, k: (i, k))
        b_spec = pl.BlockSpec((tk, tn), lambda i, j, k: (k, j))
    elif contract == NT:
        a_spec = pl.BlockSpec((tm, tk), lambda i, j, k: (i, k))
        b_spec = pl.BlockSpec((tn, tk), lambda i, j, k: (j, k))
    else:
        a_spec = pl.BlockSpec((tk, tm), lambda i, j, k: (k, i))
        b_spec = pl.BlockSpec((tk, tn), lambda i, j, k: (k, j))
    o_spec = pl.BlockSpec((tm, tn), lambda i, j, k: (i, j))
    out_dtypes = out_dtypes or [out_dtype] * n_out
    outs = _matmul(
        name, a, b, contract=contract, grid=(m // tm, n // tn, kd // tk), a_spec=a_spec, b_spec=b_spec,
        out_specs=[o_spec] * len(out_dtypes), out_shapes=[jax.ShapeDtypeStruct((m, n), dt) for dt in out_dtypes],
        acc_shape=(tm, tn), extra=extra, extra_specs=[o_spec] * len(extra), epilogue=epilogue, deps=deps)
    return outs[0] if len(out_dtypes) == 1 else outs


def _rows_call(name, body, row_in, vec_in, row_out, vec_out, s, deps=()):
    t = _tile(s, ROW_T)
    in_specs = []
    args = []
    for arr, width, cb in row_in:
        in_specs.append(pl.BlockSpec((t, width), functools.partial(lambda i, cb: (i, cb), cb=cb)))
        args.append(arr)
    for v in vec_in:
        in_specs.append(pl.BlockSpec(v.shape, lambda i: (0, 0)))
        args.append(v)
    out_specs = []
    out_shapes = []
    for width, dt in row_out:
        out_specs.append(pl.BlockSpec((t, width), lambda i: (i, 0)))
        out_shapes.append(jax.ShapeDtypeStruct((s, width), dt))
    for width in vec_out:
        out_specs.append(pl.BlockSpec((1, width), lambda i: (0, 0)))
        out_shapes.append(jax.ShapeDtypeStruct((1, width), F32))
    return _call(body, deps=deps, name=name, grid=(s // t,), in_specs=in_specs, out_specs=out_specs,
                 out_shape=out_shapes, compiler_params=_params(("arbitrary",)))(*args)


def _acc_vec(ref, val):
    _acc_row(ref, jnp.sum(val, axis=0, keepdims=True))


def _acc_row(ref, part):
    @pl.when(pl.program_id(0) == 0)
    def _():
        ref[...] = part

    @pl.when(pl.program_id(0) > 0)
    def _():
        ref[...] += part


def _rms(v):
    return lax.rsqrt(jnp.mean(v * v, axis=-1, keepdims=True) + EPS)


def _norm_bwd(dxn, xn, r):
    return r * (dxn - xn * jnp.mean(dxn * xn, axis=-1, keepdims=True))


def _premix(x, g, scale, shift, deps=()):
    s = x.shape[0]

    def body(x_ref, g_ref, sc_ref, sh_ref, h_ref):
        xv = x_ref[...]
        h_ref[...] = ((xv * _rms(xv) * g_ref[...]) * (1.0 + sc_ref[...]) + sh_ref[...]).astype(BF16)

    return _rows_call("premix", body, [(x, D_MODEL, 0)], [g, scale, shift], [(D_MODEL, BF16)], [], s, deps)[0]


def _sigmoid(z):
    return 1.0 / (1.0 + jnp.exp(-z))


def _mix_fwd(o_fox, o_gla, pm, g_fox, g_gla):
    s = o_fox.shape[0]

    def body(of_ref, og_ref, gr_ref, gf_ref, gg_ref, mix_ref):
        for h in range(FOX_HEADS):
            sl = slice(h * FOX_HEAD_DIM, (h + 1) * FOX_HEAD_DIM)
            seg = of_ref[:, sl]
            mix_ref[:, sl] = (seg * _rms(seg) * gf_ref[:, sl]).astype(BF16)
        for h in range(GLA_HEADS):
            sl = slice(h * GLA_DV, (h + 1) * GLA_DV)
            seg = og_ref[:, sl]
            gr = gr_ref[:, sl].astype(F32)
            val = (seg * _rms(seg) * gg_ref[:, sl]) * (gr * _sigmoid(gr))
            mix_ref[:, pl.ds(FOX_HEADS * FOX_HEAD_DIM + h * GLA_DV, GLA_DV)] = val.astype(BF16)

    return _rows_call("mix_fwd", body, [(o_fox, 1024, 0), (o_gla, 1024, 0), (pm, 1024, 5)], [g_fox, g_gla],
                      [(D_MODEL, BF16)], [], s)[0]


def _mix_bwd(dmix, o_fox, o_gla, pm, g_fox, g_gla, deps=()):
    s = o_fox.shape[0]

    def body(dm_ref, of_ref, og_ref, gr_ref, gf_ref, gg_ref, dof_ref, dog_ref, dgr_ref, dgf_ref, dgg_ref):
        dgf = []
        for h in range(FOX_HEADS):
            sl = slice(h * FOX_HEAD_DIM, (h + 1) * FOX_HEAD_DIM)
            seg = of_ref[:, sl]
            r = _rms(seg)
            segn = seg * r
            dout = dm_ref[:, sl]
            dgf.append(jnp.sum(dout * segn, axis=0, keepdims=True))
            dof_ref[:, sl] = _norm_bwd(dout * gf_ref[:, sl], segn, r).astype(BF16)
        dgg = []
        for h in range(GLA_HEADS):
            sl = slice(h * GLA_DV, (h + 1) * GLA_DV)
            seg = og_ref[:, sl]
            r = _rms(seg)
            segn = seg * r
            gl = segn * gg_ref[:, sl]
            gr = gr_ref[:, sl].astype(F32)
            sig = _sigmoid(gr)
            dout = dm_ref[:, pl.ds(FOX_HEADS * FOX_HEAD_DIM + h * GLA_DV, GLA_DV)]
            dgr_ref[:, sl] = (dout * gl * (sig * (1.0 + gr * (1.0 - sig)))).astype(BF16)
            dgl = dout * (gr * sig)
            dgg.append(jnp.sum(dgl * segn, axis=0, keepdims=True))
            dog_ref[:, sl] = _norm_bwd(dgl * gg_ref[:, sl], segn, r).astype(BF16)
        _acc_row(dgf_ref, jnp.concatenate(dgf, axis=1))
        _acc_row(dgg_ref, jnp.concatenate(dgg, axis=1))

    return _rows_call("mix_bwd", body, [(dmix, D_MODEL, 0), (o_fox, 1024, 0), (o_gla, 1024, 0), (pm, 1024, 5)],
                      [g_fox, g_gla], [(1024, BF16), (1024, BF16), (1024, BF16)], [1024, 1024], s, deps)


def _postmix_premlp(x, y, gate_m, g_post_mix, g_pre_mlp, scale_f, shift_f):
    s = x.shape[0]

    def body(x_ref, y_ref, gm_ref, gpm_ref, gpl_ref, sc_ref, sh_ref, x1_ref, h2_ref):
        yv = y_ref[...]
        x1 = x_ref[...] + gm_ref[...] * (yv * _rms(yv) * gpm_ref[...])
        x1_ref[...] = x1
        h2_ref[...] = ((x1 * _rms(x1) * gpl_ref[...]) * (1.0 + sc_ref[...]) + sh_ref[...]).astype(BF16)

    return _rows_call("postmix_premlp", body, [(x, D_MODEL, 0), (y, D_MODEL, 0)],
                      [gate_m, g_post_mix, g_pre_mlp, scale_f, shift_f], [(D_MODEL, F32), (D_MODEL, BF16)], [], s)


def _loss_postmlp_bwd(x1, y2, target, gate_f, g_post_mlp):
    s = x1.shape[0]

    def body(x1_ref, y2_ref, t_ref, gf_ref, g_ref, dx2_ref, dy2_ref, loss_ref, dgate_ref, dg_ref):
        yv = y2_ref[...]
        r = _rms(yv)
        yn = yv * r
        o = yn * g_ref[...]
        e = (x1_ref[...] + gf_ref[...] * o) - t_ref[...]
        part = 0.5 * jnp.sum(jnp.mean(e * e, axis=-1, keepdims=True), axis=0, keepdims=True)
        _acc_vec(loss_ref, jnp.broadcast_to(part, (1, 128)))
        dx2 = e * (1.0 / D_MODEL)
        dx2_ref[...] = dx2
        _acc_vec(dgate_ref, dx2 * o)
        do = dx2 * gf_ref[...]
        _acc_vec(dg_ref, do * yn)
        dy2_ref[...] = _norm_bwd(do * g_ref[...], yn, r).astype(BF16)

    return _rows_call("loss_postmlp_bwd", body, [(x1, D_MODEL, 0), (y2, D_MODEL, 0), (target, D_MODEL, 0)],
                      [gate_f, g_post_mlp], [(D_MODEL, F32), (D_MODEL, BF16)], [128, D_MODEL, D_MODEL], s)


def _premlp_postmix_bwd(dh2, dx2, x1, y, scale_f, g_pre_mlp, gate_m, g_post_mix, deps=()):
    s = x1.shape[0]

    def body(dh2_ref, dx2_ref, x1_ref, y_ref, sc_ref, gpl_ref, gm_ref, gpm_ref,
             dx1_ref, dy_ref, dsc_ref, dsh_ref, dgpl_ref, dgm_ref, dgpm_ref):
        x1 = x1_ref[...]
        r1 = _rms(x1)
        x1n = x1 * r1
        dh2 = dh2_ref[...]
        _acc_vec(dsc_ref, dh2 * (x1n * gpl_ref[...]))
        _acc_vec(dsh_ref, dh2)
        dn2 = dh2 * (1.0 + sc_ref[...])
        _acc_vec(dgpl_ref, dn2 * x1n)
        dx1 = dx2_ref[...] + _norm_bwd(dn2 * gpl_ref[...], x1n, r1)
        dx1_ref[...] = dx1
        yv = y_ref[...]
        ry = _rms(yv)
        yn = yv * ry
        _acc_vec(dgm_ref, dx1 * (yn * gpm_ref[...]))
        do = dx1 * gm_ref[...]
        _acc_vec(dgpm_ref, do * yn)
        dy_ref[...] = _norm_bwd(do * gpm_ref[...], yn, ry).astype(BF16)

    return _rows_call("premlp_postmix_bwd", body,
                      [(dh2, D_MODEL, 0), (dx2, D_MODEL, 0), (x1, D_MODEL, 0), (y, D_MODEL, 0)],
                      [scale_f, g_pre_mlp, gate_m, g_post_mix], [(D_MODEL, F32), (D_MODEL, BF16)],
                      [D_MODEL] * 5, s, deps)


def _premix_bwd(dh, dx1, x, g_pre_mix, scale_m):
    s = x.shape[0]

    def body(dh_ref, dx1_ref, x_ref, g_ref, sc_ref, gx_ref, dsc_ref, dsh_ref, dg_ref):
        xv = x_ref[...]
        r = _rms(xv)
        xn = xv * r
        dh = dh_ref[...]
        _acc_vec(dsc_ref, dh * (xn * g_ref[...]))
        _acc_vec(dsh_ref, dh)
        dn1 = dh * (1.0 + sc_ref[...])
        _acc_vec(dg_ref, dn1 * xn)
        gx_ref[...] = dx1_ref[...] + _norm_bwd(dn1 * g_ref[...], xn, r)

    return _rows_call("premix_bwd", body, [(dh, D_MODEL, 0), (dx1, D_MODEL, 0), (x, D_MODEL, 0)],
                      [g_pre_mix, scale_m], [(D_MODEL, F32)], [D_MODEL] * 3, s)


def _split3(v):
    hi = v.astype(BF16)
    r1 = v - hi.astype(F32)
    mid = r1.astype(BF16)
    lo = (r1 - mid.astype(F32)).astype(BF16)
    return hi, mid, lo


def _dot_exact01(v, tri, contract=NN, tri_first=False):
    acc = None
    for part in _split3(v):
        lhs, rhs = (tri, part) if tri_first else (part, tri)
        p = lax.dot_general(lhs, rhs, (contract, ((), ())), preferred_element_type=F32)
        acc = p if acc is None else acc + p
    return acc


def _log_sigmoid(z):
    return jnp.minimum(z, 0.0) - jnp.log(1.0 + jnp.exp(-jnp.abs(z)))


def _fox_cum(small, bvec):
    s = small.shape[0]
    t = _tile(s, CUM_T)

    def body(sm_ref, b_ref, out_ref, carry):
        @pl.when(pl.program_id(0) == 0)
        def _():
            carry[...] = jnp.zeros_like(carry)

        lf = _log_sigmoid(sm_ref[...] + b_ref[...])
        lft = lf.T[0:FOX_HEADS, :]
        row = lax.broadcasted_iota(jnp.int32, (t, t), 0)
        col = lax.broadcasted_iota(jnp.int32, (t, t), 1)
        upper = (row <= col).astype(BF16)
        cum = _dot_exact01(lft, upper) + carry[:, 0:1]
        out_ref[...] = cum
        carry[...] = carry[...] + jnp.sum(lft, axis=1, keepdims=True)

    return _call(body, name="fox_cum", grid=(s // t,),
                 in_specs=[pl.BlockSpec((t, W_SMALL), lambda i: (i, 0)), pl.BlockSpec((1, W_SMALL), lambda i: (0, 0))],
                 out_specs=pl.BlockSpec((FOX_HEADS, t), lambda i: (0, i)),
                 out_shape=jax.ShapeDtypeStruct((FOX_HEADS, s), F32),
                 scratch_shapes=[pltpu.VMEM((FOX_HEADS, 128), F32)],
                 compiler_params=_params(("arbitrary",)))(small, bvec)


def _fox_cum_bwd(dc, dcq, small, bvec):
    s = small.shape[0]
    t = _tile(s, CUM_T)
    nb = s // t

    def body(dc_ref, dcq_ref, sm_ref, b_ref, out_ref, db_ref, carry):
        @pl.when(pl.program_id(0) == 0)
        def _():
            carry[...] = jnp.zeros_like(carry)
            db_ref[...] = jnp.zeros_like(db_ref)

        lane = lax.broadcasted_iota(jnp.int32, (t, W_SMALL), 1)
        dcq = jnp.zeros((t, W_SMALL), F32)
        for hh in range(FOX_HEADS):
            dcq = jnp.where(lane == hh, dcq_ref[hh], dcq)
        dcv = dc_ref[...] + dcq.T[0:FOX_HEADS, :]
        row = lax.broadcasted_iota(jnp.int32, (t, t), 0)
        col = lax.broadcasted_iota(jnp.int32, (t, t), 1)
        lower = (row >= col).astype(BF16)
        dlf = _dot_exact01(dcv, lower) + carry[:, 0:1]
        carry[...] = carry[...] + jnp.sum(dcv, axis=1, keepdims=True)
        z = sm_ref[...] + b_ref[...]
        zt = z.T[0:FOX_HEADS, :]
        dff = dlf * _sigmoid(-zt)
        db_ref[...] = db_ref[...] + jnp.sum(dff, axis=1, keepdims=True)
        full = jnp.concatenate([dff, jnp.zeros((W_SMALL - FOX_HEADS, t), F32)], axis=0)
        out_ref[...] = full.T

    return _call(body, name="fox_cum_bwd", grid=(nb,),
                 in_specs=[pl.BlockSpec((FOX_HEADS, t), lambda i: (0, nb - 1 - i)),
                           pl.BlockSpec((FOX_HEADS, t, 1), lambda i: (0, nb - 1 - i, 0)),
                           pl.BlockSpec((t, W_SMALL), lambda i: (nb - 1 - i, 0)),
                           pl.BlockSpec((1, W_SMALL), lambda i: (0, 0))],
                 out_specs=[pl.BlockSpec((t, W_SMALL), lambda i: (nb - 1 - i, 0)),
                            pl.BlockSpec((FOX_HEADS, 128), lambda i: (0, 0))],
                 out_shape=[jax.ShapeDtypeStruct((s, W_SMALL), F32), jax.ShapeDtypeStruct((FOX_HEADS, 128), F32)],
                 scratch_shapes=[pltpu.VMEM((FOX_HEADS, 128), F32)],
                 compiler_params=_params(("arbitrary",)))(dc, dcq, small, bvec)


FOX_SCALE = FOX_HEAD_DIM ** -0.5


def _fox_fwd(pm, crow):
    s = pm.shape[0]
    t = _tile(s, FOX_T)
    nb = s // t
    parts = 2
    hq = t // parts

    def body(q_ref, k_ref, v_ref, c_ref, o_ref, lse_ref):
        i = pl.program_id(1)
        qs = [q_ref[g * hq:(g + 1) * hq, :] for g in range(parts)]

        def block(j, carry, diagonal):
            rows = pl.ds(pl.multiple_of(j * t, t), t)
            k_all, v_all, c_all = k_ref[rows, :], v_ref[rows, :], c_ref[j]
            out = []
            for g, (m_prev, l_prev, acc) in enumerate(carry):
                nk = (g + 1) * hq if diagonal else t
                kb, vb, cb = k_all[:nk], v_all[:nk], c_all[:, :nk]
                sc = lax.dot_general(qs[g], kb, (NT, ((), ())), preferred_element_type=F32)
                sc = sc * FOX_SCALE - cb
                if diagonal:
                    row = lax.broadcasted_iota(jnp.int32, (hq, nk), 0) + g * hq
                    col = lax.broadcasted_iota(jnp.int32, (hq, nk), 1)
                    sc = jnp.where(row >= col, sc, NEG)
                m_new = jnp.maximum(m_prev, jnp.max(sc, axis=1, keepdims=True))
                alpha = jnp.exp(m_prev - m_new)
                p = jnp.exp(sc - m_new)
                l_new = alpha * l_prev + jnp.sum(p, axis=1, keepdims=True)
                pv = jnp.dot(p.astype(BF16), vb, preferred_element_type=F32)
                out.append((m_new, l_new, alpha * acc + pv))
            return tuple(out)

        init = tuple((jnp.full((hq, 1), NEG, F32), jnp.zeros((hq, 1), F32), jnp.zeros((hq, 128), F32))
                     for _ in range(parts))
        carry = lax.fori_loop(0, i, lambda j, cr: block(j, cr, False), init)
        carry = block(i, carry, True)
        for g, (m_fin, l_fin, acc) in enumerate(carry):
            o_ref[g * hq:(g + 1) * hq, :] = acc / l_fin
            lse_ref[g * hq:(g + 1) * hq, :] = m_fin + jnp.log(l_fin)

    return _call(
        body, name="fox_fwd", grid=(FOX_HEADS, nb),
        in_specs=[pl.BlockSpec((t, 128), lambda h, i: (i, h)),
                  pl.BlockSpec((s, 128), lambda h, i: (0, FOX_HEADS + h)),
                  pl.BlockSpec((s, 128), lambda h, i: (0, 2 * FOX_HEADS + h)),
                  pl.BlockSpec((None, nb, 1, t), lambda h, i: (h, 0, 0, 0))],
        out_specs=[pl.BlockSpec((t, 128), lambda h, i: (i, h)),
                   pl.BlockSpec((None, t, 1), lambda h, i: (h, i, 0))],
        out_shape=[jax.ShapeDtypeStruct((s, FOX_HEADS * 128), F32), jax.ShapeDtypeStruct((FOX_HEADS, s, 1), F32)],
        compiler_params=_params(("parallel", "arbitrary")),
    )(pm, pm, pm, crow.reshape(FOX_HEADS, nb, 1, t))


def _fox_bwd(pm, crow, o, lse, do):
    s = pm.shape[0]
    t = _tile(s, FOX_T)
    nb = s // t

    parts = 2
    hq = t // parts

    def body(q_ref, do_ref, o_ref, lse_ref, k_ref, v_ref, c_ref, dq_ref, dk_ref, dv_ref, dc_ref, dcq_ref, delta_s):
        j = pl.program_id(1)

        @pl.when(j == 0)
        def _():
            dq_ref[...] = jnp.zeros_like(dq_ref)
            dcq_ref[...] = jnp.zeros_like(dcq_ref)
            delta_s[...] = jnp.sum(do_ref[...].astype(F32) * o_ref[...], axis=1, keepdims=True)

        k_all, v_all, c_all = k_ref[...], v_ref[...], c_ref[...]

        def grow(acc, part, axis):
            n = part.shape[axis]
            if n == acc.shape[axis]:
                return acc + part
            if axis == 0:
                return jnp.concatenate([acc[:n] + part, acc[n:]], axis=0)
            return jnp.concatenate([acc[:, :n] + part, acc[:, n:]], axis=1)

        def block(i, carry, diagonal):
            dk_acc, dv_acc, dc_acc = carry
            for g in range(parts):
                nk = (g + 1) * hq if diagonal else t
                kb, vb, cb = k_all[:nk], v_all[:nk], c_all[:, :nk]
                rows = pl.ds(pl.multiple_of(i * t + g * hq, hq), hq)
                q, dov = q_ref[rows, :], do_ref[rows, :]
                sc = lax.dot_general(q, kb, (NT, ((), ())), preferred_element_type=F32)
                p = jnp.exp(sc * FOX_SCALE - cb - lse_ref[rows, :])
                if diagonal:
                    row = lax.broadcasted_iota(jnp.int32, (hq, nk), 0) + g * hq
                    col = lax.broadcasted_iota(jnp.int32, (hq, nk), 1)
                    p = jnp.where(row >= col, p, 0.0)
                dp = lax.dot_general(dov, vb, (NT, ((), ())), preferred_element_type=F32)
                ds = p * (dp - delta_s[rows, :])
                dsb = ds.astype(BF16)
                dv_acc = grow(dv_acc, lax.dot_general(p.astype(BF16), dov, (TN, ((), ())),
                                                      preferred_element_type=F32), 0)
                dk_acc = grow(dk_acc, lax.dot_general(dsb, q, (TN, ((), ())), preferred_element_type=F32), 0)
                dq_ref[rows, :] += jnp.dot(dsb, kb, preferred_element_type=F32) * FOX_SCALE
                dc_acc = grow(dc_acc, -jnp.sum(ds, axis=0, keepdims=True), 1)
                dcq_ref[rows, :] += jnp.sum(ds, axis=1, keepdims=True)
            return dk_acc, dv_acc, dc_acc

        carry = (jnp.zeros((t, 128), F32), jnp.zeros((t, 128), F32), jnp.zeros((1, t), F32))
        carry = block(j, carry, True)
        dk_acc, dv_acc, dc_acc = lax.fori_loop(j + 1, nb, lambda i, cr: block(i, cr, False), carry)
        dk_ref[...] = (dk_acc * FOX_SCALE).astype(dk_ref.dtype)
        dv_ref[...] = dv_acc.astype(dv_ref.dtype)
        dc_ref[...] = dc_acc

    whole = lambda h, j: (0, h)
    return _call(
        body, name="fox_bwd", grid=(FOX_HEADS, nb),
        in_specs=[pl.BlockSpec((s, 128), whole), pl.BlockSpec((s, 128), whole), pl.BlockSpec((s, 128), whole),
                  pl.BlockSpec((None, s, 1), lambda h, j: (h, 0, 0)),
                  pl.BlockSpec((t, 128), lambda h, j: (j, FOX_HEADS + h)),
                  pl.BlockSpec((t, 128), lambda h, j: (j, 2 * FOX_HEADS + h)),
                  pl.BlockSpec((None, 1, t), lambda h, j: (h, 0, j))],
        out_specs=[pl.BlockSpec((s, 128), whole),
                   pl.BlockSpec((t, 128), lambda h, j: (j, h)),
                   pl.BlockSpec((t, 128), lambda h, j: (j, h)),
                   pl.BlockSpec((None, 1, t), lambda h, j: (h, 0, j)),
                   pl.BlockSpec((None, s, 1), lambda h, j: (h, 0, 0))],
        out_shape=[jax.ShapeDtypeStruct((s, 1024), F32), jax.ShapeDtypeStruct((s, 1024), BF16),
                   jax.ShapeDtypeStruct((s, 1024), BF16), jax.ShapeDtypeStruct((FOX_HEADS, 1, s), F32),
                   jax.ShapeDtypeStruct((FOX_HEADS, s, 1), F32)],
        scratch_shapes=[pltpu.VMEM((s, 1), F32)],
        compiler_params=_params(("parallel", "arbitrary")),
    )(pm, do, o, lse, pm, pm, crow)


GLA_SCALE = GLA_DK ** -0.5
GLA_Q_BLK = 3072 // 128
GLA_K_BLK = 3584 // 128
GLA_V_BLK = 4096 // 256


def _gla_gate(sm, wa_ref, b_ref):
    return jnp.dot(sm.astype(BF16), wa_ref[...], preferred_element_type=F32) + b_ref[...]


def _chunk_tri(n, kind):
    row = lax.broadcasted_iota(jnp.int32, (n, n), 0)
    col = lax.broadcasted_iota(jnp.int32, (n, n), 1)
    shift = CHUNK.bit_length() - 1
    same = (row >> shift) == (col >> shift)
    if kind == "upto":
        same = same & (row >= col)
    elif kind == "before":
        same = same & (row > col)
    return same.astype(BF16)


def _gla_fwd(pm, small, wa_pad, b_a2, deps=()):
    s = pm.shape[0]
    r = _tile(s, GLA_R)
    nc = r // CHUNK

    def body(q_ref, k_ref, v_ref, sm_ref, wa_ref, b_ref, o_ref, st_ref, state):
        @pl.when(pl.program_id(1) == 0)
        def _():
            state[...] = jnp.zeros_like(state)

        la_all = _log_sigmoid(_gla_gate(sm_ref[...], wa_ref, b_ref)) * (1.0 / GLA_TEMP)
        tri = _chunk_tri(CHUNK, "upto")
        uts, decays = [], []
        for c in range(nc):
            rows = slice(c * CHUNK, (c + 1) * CHUNK)
            la = la_all[rows]
            cum = _dot_exact01(la, tri, tri_first=True)
            total = jnp.sum(la, axis=0, keepdims=True)
            kdec = k_ref[rows, :].astype(F32) * jnp.exp(total - cum)
            uts.append(lax.dot_general(v_ref[rows, :], kdec.astype(BF16), (TN, ((), ())),
                                       preferred_element_type=F32))
            decays.append(jnp.exp(total))
        cur = state[...]
        ends = []
        for c in range(nc):
            cur = cur * decays[c] + uts[c]
            ends.append(cur.astype(BF16))
        state[...] = cur
        for c in range(nc):
            rows = slice(c * CHUNK, (c + 1) * CHUNK)
            st_ref[c] = ends[c]
            qs = (q_ref[rows, :].astype(F32) * GLA_SCALE).astype(BF16)
            o_ref[rows, :] = lax.dot_general(qs, ends[c], (NT, ((), ())), preferred_element_type=F32)

    return _call(
        body, deps=deps, name="gla_fwd", grid=(GLA_HEADS, s // r),
        in_specs=[pl.BlockSpec((r, 128), lambda h, i: (i, GLA_Q_BLK + h)),
                  pl.BlockSpec((r, 128), lambda h, i: (i, GLA_K_BLK + h)),
                  pl.BlockSpec((r, 256), lambda h, i: (i, GLA_V_BLK + h)),
                  pl.BlockSpec((r, W_SMALL), lambda h, i: (i, 0)),
                  pl.BlockSpec((W_SMALL, 128), lambda h, i: (0, h)),
                  pl.BlockSpec((1, 128), lambda h, i: (0, h))],
        out_specs=[pl.BlockSpec((r, 256), lambda h, i: (i, h)),
                   pl.BlockSpec((nc, None, GLA_DV, GLA_DK), lambda h, i: (i, h, 0, 0))],
        out_shape=[jax.ShapeDtypeStruct((s, 1024), F32),
                   jax.ShapeDtypeStruct((s // CHUNK, GLA_HEADS, GLA_DV, GLA_DK), BF16)],
        scratch_shapes=[pltpu.VMEM((GLA_DV, GLA_DK), F32)],
        compiler_params=_params(("parallel", "arbitrary")),
    )(pm, pm, pm, small, wa_pad, b_a2)


def _gla_bwd(pm, small, wa_pad, b_a2, states, do):
    s = pm.shape[0]
    r = _tile(s, GLA_R)
    nc = r // CHUNK
    nb = s // r

    def body(q_ref, k_ref, v_ref, sm_ref, wa_ref, b_ref, do_ref, st_ref, prev_ref,
             dq_ref, dk_ref, dv_ref, dza_ref, db_ref, carry):
        step = pl.program_id(1)

        @pl.when(step == 0)
        def _():
            carry[...] = jnp.zeros_like(carry)
            db_ref[...] = jnp.zeros_like(db_ref)

        z_all = _gla_gate(sm_ref[...], wa_ref, b_ref)
        la_all = _log_sigmoid(z_all) * (1.0 / GLA_TEMP)
        tri = _chunk_tri(CHUNK, "upto")
        tri_strict = _chunk_tri(CHUNK, "before")
        ws, decays, kdecs, gouts = [], [], [], []
        for c in range(nc):
            rows = slice(c * CHUNK, (c + 1) * CHUNK)
            la = la_all[rows]
            cum = _dot_exact01(la, tri, tri_first=True)
            total = jnp.sum(la, axis=0, keepdims=True)
            w = jnp.exp(total - cum)
            ws.append(w)
            decays.append(jnp.exp(total))
            kdecs.append(k_ref[rows, :].astype(F32) * w)
            dov = do_ref[rows, :]
            qs = (q_ref[rows, :].astype(F32) * GLA_SCALE).astype(BF16)
            dq_ref[rows, :] = (jnp.dot(dov, st_ref[c], preferred_element_type=F32) * GLA_SCALE).astype(BF16)
            gouts.append(lax.dot_general(dov, qs, (TN, ((), ())), preferred_element_type=F32))
        cur = carry[...]
        gts = [None] * nc
        for c in reversed(range(nc)):
            gts[c] = gouts[c] + cur
            cur = gts[c] * decays[c]
        carry[...] = cur
        db = jnp.zeros((1, 128), F32)
        for c in range(nc):
            rows = slice(c * CHUNK, (c + 1) * CHUNK)
            gtb = gts[c].astype(BF16)
            dv_ref[rows, :] = lax.dot_general(kdecs[c].astype(BF16), gtb, (NT, ((), ())),
                                              preferred_element_type=F32).astype(BF16)
            dkdec = jnp.dot(v_ref[rows, :], gtb, preferred_element_type=F32)
            dk_ref[rows, :] = (dkdec * ws[c]).astype(BF16)
            e = dkdec * kdecs[c]
            if c > 0:
                prev = st_ref[c - 1].astype(F32)
            else:
                prev = jnp.where(step == nb - 1, 0.0, prev_ref[0].astype(F32))
            dtot = jnp.sum(gts[c] * prev, axis=0, keepdims=True) * decays[c]
            dla = dtot + _dot_exact01(e, tri_strict, tri_first=True)
            dza = dla * (1.0 / GLA_TEMP) * _sigmoid(-z_all[rows])
            dza_ref[rows, :] = dza.astype(BF16)
            db = db + jnp.sum(dza, axis=0, keepdims=True)
        db_ref[...] += db

    blk = lambda h, i: nb - 1 - i
    return _call(
        body, name="gla_bwd", grid=(GLA_HEADS, nb),
        in_specs=[pl.BlockSpec((r, 128), lambda h, i: (blk(h, i), GLA_Q_BLK + h)),
                  pl.BlockSpec((r, 128), lambda h, i: (blk(h, i), GLA_K_BLK + h)),
                  pl.BlockSpec((r, 256), lambda h, i: (blk(h, i), GLA_V_BLK + h)),
                  pl.BlockSpec((r, W_SMALL), lambda h, i: (blk(h, i), 0)),
                  pl.BlockSpec((W_SMALL, 128), lambda h, i: (0, h)),
                  pl.BlockSpec((1, 128), lambda h, i: (0, h)),
                  pl.BlockSpec((r, 256), lambda h, i: (blk(h, i), h)),
                  pl.BlockSpec((nc, None, GLA_DV, GLA_DK), lambda h, i: (blk(h, i), h, 0, 0)),
                  pl.BlockSpec((1, None, GLA_DV, GLA_DK),
                               lambda h, i: (jnp.maximum(blk(h, i) * nc - 1, 0), h, 0, 0))],
        out_specs=[pl.BlockSpec((r, 128), lambda h, i: (blk(h, i), h)),
                   pl.BlockSpec((r, 128), lambda h, i: (blk(h, i), h)),
                   pl.BlockSpec((r, 256), lambda h, i: (blk(h, i), h)),
                   pl.BlockSpec((r, 128), lambda h, i: (blk(h, i), h)),
                   pl.BlockSpec((1, 128), lambda h, i: (0, h))],
        out_shape=[jax.ShapeDtypeStruct((s, 512), BF16), jax.ShapeDtypeStruct((s, 512), BF16),
                   jax.ShapeDtypeStruct((s, 1024), BF16), jax.ShapeDtypeStruct((s, 512), BF16),
                   jax.ShapeDtypeStruct((1, 512), F32)],
        scratch_shapes=[pltpu.VMEM((GLA_DV, GLA_DK), F32)],
        compiler_params=_params(("parallel", "arbitrary")),
    )(pm, pm, pm, small, wa_pad, b_a2, do, states, states)


def _modulation(c_all, w_ada):
    n = w_ada.shape[1]
    tn = _tile(n, 512)

    def body(c_ref, w_ref, out_ref, ca_ref):
        cv = c_ref[...]
        ca = cv * _sigmoid(cv)
        ca_ref[...] = ca
        out_ref[...] = jnp.dot(ca.astype(BF16), w_ref[...].astype(BF16), preferred_element_type=F32)

    return _call(body, name="modulation", grid=(n // tn,),
                 in_specs=[pl.BlockSpec((N_DEV, D_MODEL), lambda j: (0, 0)),
                           pl.BlockSpec((D_MODEL, tn), lambda j: (0, j))],
                 out_specs=[pl.BlockSpec((N_DEV, tn), lambda j: (0, j)),
                            pl.BlockSpec((N_DEV, D_MODEL), lambda j: (0, 0))],
                 out_shape=[jax.ShapeDtypeStruct((N_DEV, n), F32), jax.ShapeDtypeStruct((N_DEV, D_MODEL), F32)],
                 compiler_params=_params(("arbitrary",)))(c_all, w_ada)


def _adamw_math(w, g, m, v):
    m = ADAM_B1 * m + (1.0 - ADAM_B1) * g
    v = ADAM_B2 * v + (1.0 - ADAM_B2) * (g * g)
    m_hat = m / (1.0 - ADAM_B1 ** ADAM_STEP)
    v_hat = v / (1.0 - ADAM_B2 ** ADAM_STEP)
    delta = -ADAM_LR * (m_hat / (jnp.sqrt(v_hat) + ADAM_EPS) + ADAM_WD * w)
    return delta, m, v


def _adamw_slabs(name, w, slabs, m, v, tr=256):
    rr, cc = w.shape

    def body(w_ref, s_ref, m_ref, v_ref, g_ref, d_ref, nm_ref, nv_ref):
        g = s_ref[0].astype(F32)
        for r in range(1, N_DEV):
            g = g + s_ref[r].astype(F32)
        g_ref[...] = g
        d, nm, nv = _adamw_math(w_ref[...], g, m_ref[...], v_ref[...])
        d_ref[...] = d
        nm_ref[...] = nm
        nv_ref[...] = nv

    steps, spec, slab_spec = _plane_tiles(rr, cc, tr)
    return _call(body, name=name, grid=(steps,),
                 in_specs=[spec, slab_spec, spec, spec],
                 out_specs=[spec] * 4, out_shape=[jax.ShapeDtypeStruct((rr, cc), F32)] * 4,
                 compiler_params=_params(("parallel",)))(w, slabs, m, v)


def _adamw_ada(w, cat, dm, m, v, tr=256, deps=()):
    rr, cc = w.shape
    tr = _tile(rr, tr)

    def body(w_ref, ca_ref, dm_ref, m_ref, v_ref, g_ref, d_ref, nm_ref, nv_ref):
        g = ca_ref[:, 0:1] * dm_ref[0:1, :]
        for b in range(1, N_DEV):
            g = g + ca_ref[:, b:b + 1] * dm_ref[b:b + 1, :]
        g_ref[...] = g
        d, nm, nv = _adamw_math(w_ref[...], g, m_ref[...], v_ref[...])
        d_ref[...] = d
        nm_ref[...] = nm
        nv_ref[...] = nv

    spec = pl.BlockSpec((tr, cc), lambda i: (i, 0))
    return _call(body, deps=deps, name="adamw_ada", grid=(rr // tr,),
                 in_specs=[spec, pl.BlockSpec((tr, N_DEV), lambda i: (i, 0)),
                           pl.BlockSpec((N_DEV, cc), lambda i: (0, 0)), spec, spec],
                 out_specs=[spec] * 4, out_shape=[jax.ShapeDtypeStruct((rr, cc), F32)] * 4,
                 compiler_params=_params(("parallel",)))(w, cat, dm, m, v)


def _sum_devices(gathered):
    ln = gathered.shape[-1]

    def body(g_ref, out_ref):
        acc = g_ref[0]
        for r in range(1, N_DEV):
            acc = acc + g_ref[r]
        out_ref[...] = acc

    return _call(body, name="sum_devices",
                 in_specs=[pl.BlockSpec(memory_space=pltpu.VMEM)], out_specs=pl.BlockSpec(memory_space=pltpu.VMEM),
                 out_shape=jax.ShapeDtypeStruct((1, ln), F32))(gathered)


def _adamw_flat(w, g, m, v):
    def body(w_ref, g_ref, m_ref, v_ref, d_ref, nm_ref, nv_ref):
        d, nm, nv = _adamw_math(w_ref[...], g_ref[...], m_ref[...], v_ref[...])
        d_ref[...] = d
        nm_ref[...] = nm
        nv_ref[...] = nv

    vm = pl.BlockSpec(memory_space=pltpu.VMEM)
    return _call(body, name="adamw_small", in_specs=[vm] * 4, out_specs=[vm] * 3,
                 out_shape=[jax.ShapeDtypeStruct(w.shape, F32)] * 3)(w, g, m, v)


IN_ROWS = 771
IN_ROWS_PAD = 784


def _padded_rows(lo, hi):
    out = []
    while lo < hi:
        dev, i = divmod(lo, IN_ROWS)
        n = min(hi - lo, IN_ROWS - i)
        out.append((IN_ROWS_PAD * dev + i, IN_ROWS_PAD * dev + i + n))
        lo += n
    return out


def _from_col_shards(g):
    return jnp.transpose(g, (1, 0, 2)).reshape(g.shape[1], N_DEV * g.shape[2])


def _pad_lanes(v, n):
    return jnp.concatenate([v, jnp.zeros(v.shape[:-1] + (n - v.shape[-1],), v.dtype)], axis=-1)


def kernel(x, c, w_ada, b_ada, g_pre_mix, g_post_mix, w_in, b_fgate, w_gla_a2, b_gla_a2, g_fox_out, g_gla_out, w_out, g_pre_mlp, g_post_mlp, w_mlp_in, w_mlp_out, loss_target, m_w_ada, m_b_ada, m_g_pre_mix, m_g_post_mix, m_w_in, m_b_fgate, m_w_gla_a2, m_b_gla_a2, m_g_fox_out, m_g_gla_out, m_w_out, m_g_pre_mlp, m_g_post_mlp, m_w_mlp_in, m_w_mlp_out, v_w_ada, v_b_ada, v_g_pre_mix, v_g_post_mix, v_w_in, v_b_fgate, v_w_gla_a2, v_b_gla_a2, v_g_fox_out, v_g_gla_out, v_w_out, v_g_pre_mlp, v_g_post_mlp, v_w_mlp_in, v_w_mlp_out):
    rank = _my_rank()
    xs = x[0]
    s = xs.shape[0]
    target = loss_target[0]

    w_in_t, m_in_t, v_in_t = w_in[0].T, m_w_in[0].T, v_w_in[0].T
    win_blk = jnp.pad(w_in_t.astype(BF16), ((0, IN_ROWS_PAD - IN_ROWS), (0, 0)))
    c_all, wa2_g, ggla_g, win_g = _all_gather("gather_first", [c, w_gla_a2[0], g_gla_out[0], win_blk])
    rest = [_own_slot("own_w_out", w_out[0], True, rank), _own_slot("own_w_mlp_in", w_mlp_in[0], True, rank)]
    gs_send, gs_sib, gs_ici, gs_land, gs_token = _gather2_start("gather_rest_start", rest, after=(c_all,))
    last = [_own_slot("own_w_mlp_out", w_mlp_out[0], True, rank)]
    gl_send, gl_sib, gl_ici, gl_land, gl_token = _gather2_start("gather_last_start", last, after=(gs_token,))
    w_a2 = _from_col_shards(wa2_g)
    g_gla = _from_col_shards(ggla_g).reshape(1, 1024)
    g_fox = g_fox_out.reshape(1, 1024)
    win_stack = win_g.reshape(N_DEV * IN_ROWS_PAD, D_MODEL)

    def in_rows(lo, hi):
        return [win_stack[a:b] for a, b in _padded_rows(lo, hi)]

    w_main = jnp.concatenate(in_rows(0, 3072) + in_rows(3080, 5128) + in_rows(5144, 6168), axis=0)
    w_small = jnp.concatenate(in_rows(3072, 3080) + in_rows(5128, 5144)
                              + [jnp.zeros((W_SMALL - 24, D_MODEL), BF16)], axis=0)
    wa_pad =jnp.concatenate([jnp.zeros((8, 512), BF16), w_a2.astype(BF16), jnp.zeros((104, 512), BF16)], axis=0)
    bf_vec = _pad_lanes(b_fgate, W_SMALL)

    mod_part, c_act = _modulation(c_all.reshape(N_DEV, D_MODEL), w_ada[0])
    (mod_g,) = _all_gather("gather_mod", [mod_part])
    mod = lax.dynamic_slice_in_dim(mod_g, rank, 1, axis=1).reshape(1, 6 * D_MODEL) + b_ada
    shift_m, scale_m, gate_m, shift_f, scale_f, gate_f = [mod[:, i * D_MODEL:(i + 1) * D_MODEL] for i in range(6)]

    h = _premix(xs, g_pre_mix, scale_m, shift_m, deps=(gl_token,))
    pm = _mm_plain("proj_main", h, w_main, NT, BF16)
    small = _mm_plain("proj_small", h, w_small, NT, F32)
    crow = _fox_cum(small, bf_vec).reshape(FOX_HEADS, 1, s)
    o_fox, lse = _fox_fwd(pm, crow)
    gs_fsend, gs_frecv, gs_land, gs_ftoken = _gather2_forward("gather_rest_forward", gs_land, gs_ici, o_fox)
    o_gla, states = _gla_fwd(pm, small, wa_pad, b_gla_a2, deps=(gs_ftoken,))
    mix = _mix_fwd(o_fox, o_gla, pm, g_fox, g_gla)
    wout_g, wmi_g = _gather2_wait("gather_rest_wait", gs_land, gs_send, gs_sib, gs_fsend, gs_frecv, mix)
    w_out_full = wout_g.reshape(D_MODEL, D_MODEL)
    y = _mm_plain("out_proj", mix, w_out_full, NN, F32)
    x1, h2 = _postmix_premlp(xs, y, gate_m, g_post_mix, g_pre_mlp, scale_f, shift_f)
    gl_fsend, gl_frecv, gl_land, gl_ftoken = _gather2_forward("gather_last_forward", gl_land, gl_ici, h2)

    tm, tn, tk = _tile(s, 1024), 1024, 2048
    nsh = 1024 // tn

    def relu2(acc):
        rl = jnp.maximum(acc, 0.0)
        return rl * rl, rl

    z, a_relu = _matmul(
        "mlp_in", h2, wmi_g, contract=NN, grid=(s // tm, D_FF // tn, D_MODEL // tk),
        a_spec=pl.BlockSpec((tm, tk), lambda i, j, k: (i, k)),
        b_spec=pl.BlockSpec((None, tk, tn), lambda i, j, k: (j // nsh, k, j % nsh)),
        out_specs=[pl.BlockSpec((tm, tn), lambda i, j, k: (i, j))] * 2,
        out_shapes=[jax.ShapeDtypeStruct((s, D_FF), BF16)] * 2, acc_shape=(tm, tn), epilogue=relu2,
        deps=(gl_ftoken,))
    (wmo_g,) = _gather2_wait("gather_last_wait", gl_land, gl_send, gl_sib, gl_fsend, gl_frecv, z)
    w_mo_full = wmo_g.reshape(D_FF, D_MODEL)
    y2 = _mm_plain("mlp_out", z, w_mo_full, NN, F32)

    dx2, dy2, loss_vec, dgate_f, dg_post_mlp = _loss_postmlp_bwd(x1, y2, target, gate_f, g_post_mlp)
    loss = lax.psum(loss_vec[0, 0], ("x", "y", "c"))

    da = _mm_plain("mlp_out_dx", dy2, w_mo_full, NT, BF16, extra=(a_relu,),
                   epilogue=lambda acc, rl: (acc * (2.0 * rl.astype(F32)),))
    dw_mo = _mm_plain("mlp_out_dw", z, dy2, TN, BF16)
    dw_mo = dw_mo.reshape(N_DEV, 1024, D_MODEL)
    x_mo = _exchange_start("grad_mlp_out_start", [_own_slot("own_dw_mlp_out", dw_mo, False, rank)], [dw_mo])
    tkx = 1024
    (dh2,) = _matmul(
        "mlp_in_dx", da, wmi_g, contract=NT, grid=(s // tm, D_MODEL // tn, D_FF // tkx),
        a_spec=pl.BlockSpec((tm, tkx), lambda i, j, k: (i, k)),
        b_spec=pl.BlockSpec((None, tn, tkx), lambda i, j, k: (k, j, 0)),
        out_specs=[pl.BlockSpec((tm, tn), lambda i, j, k: (i, j))],
        out_shapes=[jax.ShapeDtypeStruct((s, D_MODEL), F32)], acc_shape=(tm, tn), deps=(x_mo[4],))
    ts = _tile(s, 2048)
    (dw_mi,) = _matmul(
        "mlp_in_dw", h2, da, contract=TN, grid=(D_MODEL // 1024, D_FF // tn, s // ts),
        a_spec=pl.BlockSpec((ts, 1024), lambda i, j, k: (k, i)),
        b_spec=pl.BlockSpec((ts, tn), lambda i, j, k: (k, j)),
        out_specs=[pl.BlockSpec((None, 1024, tn), lambda i, j, k: (j // nsh, i, j % nsh))],
        out_shapes=[jax.ShapeDtypeStruct((N_DEV, D_MODEL, 1024), BF16)], acc_shape=(1024, tn))
    x_mi = _exchange_start("grad_mlp_in_start", [_own_slot("own_dw_mlp_in", dw_mi, False, rank)], [dw_mi])

    dx1, dy, dscale_f, dshift_f, dg_pre_mlp, dgate_m, dg_post_mix = _premlp_postmix_bwd(
        dh2, dx2, x1, y, scale_f, g_pre_mlp, gate_m, g_post_mix, deps=(x_mi[4],))

    dmix = _mm_plain("out_proj_dx", dy, w_out_full, NT, F32)
    dw_out = _mm_plain("out_proj_dw", mix, dy, TN, BF16)
    dw_out = dw_out.reshape(N_DEV, 256, D_MODEL)
    x_out = _exchange_start("grad_out_start", [_own_slot("own_dw_out", dw_out, False, rank)], [dw_out])
    do_fox, do_gla, dgr, dg_fox, dg_gla = _mix_bwd(dmix, o_fox, o_gla, pm, g_fox, g_gla, deps=(x_out[4],))

    dq, dk, dv, dc, dcq = _fox_bwd(pm, crow, o_fox, lse, do_fox)
    dsmall_f, db_f = _fox_cum_bwd(dc.reshape(FOX_HEADS, s), dcq, small, bf_vec)
    dgq, dgk, dgv, dza, db_a2 = _gla_bwd(pm, small, wa_pad, b_gla_a2, states, do_gla)
    dsmall = _mm_plain("gate_dx", dza, wa_pad, NT, F32, tn=128, extra=(dsmall_f,),
                       epilogue=lambda acc, other: (acc + other,))
    dwa_pad = _mm_plain("gate_dw", small, dza, TN, F32, tm=128, tn=512)

    dpm = jnp.concatenate([dq.astype(BF16), dk.astype(BF16), dv.astype(BF16), dgq.astype(BF16), dgk.astype(BF16),
                           dgv.astype(BF16), dgr], axis=1)
    dw_main = _mm_plain("proj_main_dw", dpm, h, TN, BF16)
    dw_small = _mm_plain("proj_small_dw", dsmall, h, TN, BF16, tm=128)
    dwin_full = jnp.concatenate([dw_main[:3072], dw_small[0:8], dw_main[3072:5120], dw_small[8:24],
                                 dw_main[5120:6144]], axis=0)
    dwin_slabs = dwin_full.reshape(N_DEV, 771, D_MODEL)
    x_in = _exchange_start("grad_in_start", [_own_slot("own_dw_in", dwin_slabs, False, rank)], [dwin_slabs])
    dh_small = _mm_plain("proj_small_dx", dsmall, w_small, NN, F32, tk=128)
    dh = _mm_plain("proj_main_dx", dpm, w_main, NN, F32, extra=(dh_small,),
                   epilogue=lambda acc, other: (acc + other,), deps=(x_in[4],))
    grad_x, dscale_m, dshift_m, dg_pre_mix = _premix_bwd(dh, dx1, xs, g_pre_mix, scale_m)

    dmod = jnp.concatenate([dshift_m, dscale_m, dgate_m, dshift_f, dscale_f, dgate_f], axis=1)
    flat = jnp.concatenate(
        [dmod, dg_pre_mix, dg_post_mix, dg_fox, dg_pre_mlp, dg_post_mlp, db_a2,
         dwa_pad[8:24, :].reshape(1, GLA_RANK * 512), dg_gla, _pad_lanes(db_f[:, 0].reshape(1, FOX_HEADS), 128)],
        axis=1)

    (r_mo,) = _exchange_wait("grad_mlp_out_wait", *x_mo[:4], grad_x)
    g_mo, d_mo, nm_mo, nv_mo = _adamw_slabs("adamw_w_mlp_out", w_mlp_out[0], r_mo, m_w_mlp_out[0], v_w_mlp_out[0])
    (r_mi,) = _exchange_wait("grad_mlp_in_wait", *x_mi[:4], g_mo)
    g_mi, d_mi, nm_mi, nv_mi = _adamw_slabs("adamw_w_mlp_in", w_mlp_in[0], r_mi, m_w_mlp_in[0], v_w_mlp_in[0])
    (r_out,) = _exchange_wait("grad_out_wait", *x_out[:4], g_mi)
    g_out, d_out, nm_out, nv_out = _adamw_slabs("adamw_w_out", w_out[0], r_out, m_w_out[0], v_w_out[0])

    (flat_g,) = _all_gather("gather_small_grads", [flat], deps=(g_out,))
    tot = _sum_devices(flat_g)
    dm_cols = lax.dynamic_slice_in_dim(flat_g[:, 0, :6 * D_MODEL], rank * 1536, 1536, axis=1)
    (r_in,) = _exchange_wait("grad_in_wait", *x_in[:4], tot)
    in_t = _adamw_slabs("adamw_w_in", w_in_t, r_in, m_in_t, v_in_t)
    g_in, d_in, nm_in, nv_in = [a.T for a in in_t]
    g_ada, d_ada, nm_ada, nv_ada = _adamw_ada(w_ada[0], c_act.T, dm_cols, m_w_ada[0], v_w_ada[0], deps=(in_t[0],))

    o = 0
    seg = {}
    for name, n in (("b_ada", 12288), ("g_pre_mix", 2048), ("g_post_mix", 2048), ("g_fox_out", 1024),
                    ("g_pre_mlp", 2048), ("g_post_mlp", 2048), ("b_gla_a2", 512), ("w_gla_a2", 8192),
                    ("g_gla_out", 1024), ("b_fgate", 128)):
        seg[name] = tot[:, o:o + n]
        o += n
    g_wa2 = lax.dynamic_slice_in_dim(seg["w_gla_a2"].reshape(GLA_RANK, 512), rank * 64, 64, axis=1)
    g_ggla = lax.dynamic_slice_in_dim(seg["g_gla_out"].reshape(GLA_HEADS, GLA_DV), rank * 32, 32, axis=1)
    small_names = ["b_ada", "g_pre_mix", "g_post_mix", "g_fox_out", "g_pre_mlp", "g_post_mlp", "b_gla_a2",
                   "w_gla_a2", "g_gla_out", "b_fgate"]
    small_grads = {**seg, "w_gla_a2": g_wa2.reshape(1, 1024), "g_gla_out": g_ggla.reshape(1, 128)}
    weights = dict(b_ada=b_ada, g_pre_mix=g_pre_mix, g_post_mix=g_post_mix, g_fox_out=g_fox_out,
                   g_pre_mlp=g_pre_mlp, g_post_mlp=g_post_mlp, b_gla_a2=b_gla_a2, w_gla_a2=w_gla_a2,
                   g_gla_out=g_gla_out, b_fgate=b_fgate)
    moms = dict(b_ada=m_b_ada, g_pre_mix=m_g_pre_mix, g_post_mix=m_g_post_mix, g_fox_out=m_g_fox_out,
                g_pre_mlp=m_g_pre_mlp, g_post_mlp=m_g_post_mlp, b_gla_a2=m_b_gla_a2, w_gla_a2=m_w_gla_a2,
                g_gla_out=m_g_gla_out, b_fgate=m_b_fgate)
    vels = dict(b_ada=v_b_ada, g_pre_mix=v_g_pre_mix, g_post_mix=v_g_post_mix, g_fox_out=v_g_fox_out,
                g_pre_mlp=v_g_pre_mlp, g_post_mlp=v_g_post_mlp, b_gla_a2=v_b_gla_a2, w_gla_a2=v_w_gla_a2,
                g_gla_out=v_g_gla_out, b_fgate=v_b_fgate)

    def flatten(d, fill):
        parts = []
        for nm in small_names:
            p = d[nm].reshape(1, -1)
            if nm == "b_fgate":
                p = jnp.concatenate([p[:, :FOX_HEADS], jnp.full((1, 128 - FOX_HEADS), fill, F32)], axis=1)
            parts.append(p)
        return jnp.concatenate(parts, axis=1).reshape(-1, 128)

    fw, fg, fm, fv = flatten(weights, 0.0), flatten(small_grads, 0.0), flatten(moms, 0.0), flatten(vels, 1.0)
    fd, fnm, fnv = _adamw_flat(fw, fg, fm, fv)

    def unflatten(fl):
        fl = fl.reshape(1, -1)
        out = {}
        o = 0
        for nm in small_names:
            n = 128 if nm == "b_fgate" else weights[nm].size
            piece = fl[:, o:o + n]
            if nm == "b_fgate":
                piece = piece[:, :FOX_HEADS]
            out[nm] = piece.reshape(weights[nm].shape)
            o += n
        return out

    sg, sd, snm, snv = unflatten(fg), unflatten(fd), unflatten(fnm), unflatten(fnv)

    big = dict(w_ada=(g_ada, d_ada, nm_ada, nv_ada), w_in=(g_in, d_in, nm_in, nv_in),
               w_out=(g_out, d_out, nm_out, nv_out), w_mlp_in=(g_mi, d_mi, nm_mi, nv_mi),
               w_mlp_out=(g_mo, d_mo, nm_mo, nv_mo))
    order = ["w_ada", "b_ada", "g_pre_mix", "g_post_mix", "w_in", "b_fgate", "w_gla_a2", "b_gla_a2", "g_fox_out",
             "g_gla_out", "w_out", "g_pre_mlp", "g_post_mlp", "w_mlp_in", "w_mlp_out"]

    def pick(nm, idx):
        if nm in big:
            return big[nm][idx][None]
        return (sg, sd, snm, snv)[idx][nm]

    grads = [pick(nm, 0) for nm in order]
    deltas = [pick(nm, 1) for nm in order]
    new_m = [pick(nm, 2) for nm in order]
    new_v = [pick(nm, 3) for nm in order]
    return (loss, grad_x[None], *grads, *deltas, *new_m, *new_v)
```

```python
import functools

import numpy as np
import jax
import jax.numpy as jnp
from jax import lax
from jax.experimental import pallas as pl
from jax.experimental.pallas import tpu as pltpu

F32 = jnp.float32
BF16 = jnp.bfloat16
MESH = pl.DeviceIdType.MESH
N_DEV = 8

D_MODEL = 2048
FOX_HEADS = 8
FOX_HEAD_DIM = 128
GLA_HEADS = 4
GLA_DK = 128
GLA_DV = 256
GLA_RANK = 16
GLA_TEMP = 16.0
CHUNK = 64
D_FF = 8192
W_MAIN = 6144
W_SMALL = 128
EPS = 1e-6
NEG = float(np.finfo(np.float32).min)

ADAM_LR = 0.001
ADAM_B1 = 0.9
ADAM_B2 = 0.999
ADAM_EPS = 1e-08
ADAM_WD = 0.01
ADAM_STEP = 10

ROW_T = 256
FOX_T = 1024
GLA_R = 512
CUM_T = 256
VMEM_LIMIT = 56 * 1024 * 1024


def _call(body, deps=(), **kw):
    if not deps:
        return pl.pallas_call(body, **kw)
    n_in, n_dep = len(kw["in_specs"]), len(deps)

    def with_deps(*refs):
        return body(*refs[:n_in], *refs[n_in + n_dep:])

    kw["in_specs"] = [*kw["in_specs"], *[pl.BlockSpec(memory_space=pl.ANY)] * n_dep]
    call = pl.pallas_call(with_deps, **kw)
    return lambda *args: call(*args, *deps)


def _params(sem=None):
    return pltpu.CompilerParams(dimension_semantics=sem, vmem_limit_bytes=VMEM_LIMIT)


def _my_pos():
    return lax.axis_index("x"), lax.axis_index("y"), lax.axis_index("c")


def _my_rank():
    x, y, c = _my_pos()
    return 4 * x + 2 * y + c


def _all_gather(name, arrays, deps=()):
    n = len(arrays)

    def body(*refs):
        ins = refs[:n]
        outs = refs[n:2 * n]
        send_sems, recv_sems, local_sems = refs[2 * n:]
        x, y, c = _my_pos()
        me, sibling = (x, y, c), (x, y, 1 - c)
        chips = [(1 - x, y), (x, 1 - y), (1 - x, 1 - y)]

        def slot(a, px, py, pc):
            return outs[a].at[4 * px + 2 * py + pc]

        def copy(a, k, block, to, src=None):
            return pltpu.make_async_remote_copy(
                src_ref=slot(a, *block) if src is None else src, dst_ref=slot(a, *block),
                send_sem=send_sems.at[a, k], recv_sem=recv_sems.at[a, k],
                device_id=to, device_id_type=MESH)

        started = []
        for a in range(n):
            mine = pltpu.make_async_copy(ins[a], slot(a, *me), local_sems.at[a])
            mine.start()
            started.append(mine)
        first = []
        for a in range(n):
            first.append(copy(a, 0, me, sibling, src=ins[a]))
            first += [copy(a, 1 + j, me, (*chip, c), src=ins[a]) for j, chip in enumerate(chips)]
        for cp in first:
            cp.start()
        passed = []
        for j, chip in enumerate(chips):
            for a in range(n):
                copy(a, 1 + j, (*chip, c), me).wait_recv()
                fwd = copy(a, 4 + j, (*chip, c), sibling)
                fwd.start()
                passed.append(fwd)
        for a in range(n):
            copy(a, 0, sibling, me).wait_recv()
            for j, chip in enumerate(chips):
                copy(a, 4 + j, (*chip, 1 - c), me).wait_recv()
        for cp in first + passed:
            cp.wait_send()
        for mine in started:
            mine.wait()

    hbm = pl.BlockSpec(memory_space=pltpu.HBM)
    return _call(
        body, deps=deps, name=name,
        out_shape=[jax.ShapeDtypeStruct((N_DEV,) + a.shape, a.dtype) for a in arrays],
        in_specs=[hbm] * n, out_specs=[hbm] * n,
        scratch_shapes=[pltpu.SemaphoreType.DMA((n, 7)), pltpu.SemaphoreType.DMA((n, 7)),
                        pltpu.SemaphoreType.DMA((n,))],
    )(*arrays)


def _plane_tiles(rr, cc, tr=512, tc=512):
    if rr % 8 == 0:
        tr = _tile(rr, tr)
        return (rr // tr, pl.BlockSpec((tr, cc), lambda i: (i, 0)),
                pl.BlockSpec((N_DEV, tr, cc), lambda i: (0, i, 0)))
    tc = _tile(cc, tc)
    return (cc // tc, pl.BlockSpec((rr, tc), lambda i: (0, i)),
            pl.BlockSpec((N_DEV, rr, tc), lambda i: (0, 0, i)))


def _own_slot(name, src, gather, rank):
    shape = ((N_DEV,) + src.shape) if gather else src.shape
    rr, cc = shape[1], shape[2]
    by_rows = rr % 8 == 0
    tr, tc = (_tile(rr, 512), cc) if by_rows else (rr, _tile(cc, 512))
    steps = rr // tr if by_rows else cc // tc

    def body(rank_ref, s_ref, o_ref):
        o_ref[...] = s_ref[...].astype(o_ref.dtype)

    def at(i):
        return (i, 0) if by_rows else (0, i)

    if gather:
        in_spec = pl.BlockSpec((tr, tc), lambda i, rk: at(i))
    else:
        in_spec = pl.BlockSpec((None, tr, tc), lambda i, rk: (rk[0], *at(i)))
    grid_spec = pltpu.PrefetchScalarGridSpec(
        num_scalar_prefetch=1, grid=(steps,), in_specs=[in_spec],
        out_specs=pl.BlockSpec((None, tr, tc), lambda i, rk: (rk[0], *at(i))))
    return _call(body, name=name, grid_spec=grid_spec, out_shape=jax.ShapeDtypeStruct(shape, BF16),
                 compiler_params=_params(("arbitrary",)))(jnp.reshape(rank, (1,)).astype(jnp.int32), src)


_HBM = pl.BlockSpec(memory_space=pltpu.HBM)
_SEM = pl.BlockSpec(memory_space=pltpu.SEMAPHORE)
_FLIPS = [(kx, ky, kc) for kx in (0, 1) for ky in (0, 1) for kc in (0, 1)][1:]


def _peers():
    x, y, c = _my_pos()
    out = []
    for kx, ky, kc in _FLIPS:
        px, py, pc = (1 - x if kx else x), (1 - y if ky else y), (1 - c if kc else c)
        out.append(((px, py, pc), 4 * px + 2 * py + pc))
    return out


def _exchange_copy(srcs, lands, send_sems, recv_sems, a, k, peer, peer_rank, slot):
    return pltpu.make_async_remote_copy(
        src_ref=lands[a].at[slot] if srcs is None else srcs[a].at[peer_rank],
        dst_ref=lands[a].at[slot],
        send_sem=send_sems[a].at[k], recv_sem=recv_sems[a].at[k],
        device_id=peer, device_id_type=MESH)


def _exchange_start(name, lands, srcs=None, after=()):
    n = len(lands)
    n_src = 0 if srcs is None else n
    n_in = n + n_src + len(after)

    def body(*refs):
        lnd = refs[:n]
        src = None if srcs is None else refs[n:2 * n]
        send_sems, recv_sems = refs[n_in:n_in + n], refs[n_in + n:n_in + 2 * n]
        token = refs[-1]
        me = _my_rank()
        for a in range(n):
            for k, (peer, peer_rank) in enumerate(_peers()):
                _exchange_copy(src, lnd, send_sems, recv_sems, a, k, peer, peer_rank, me).start()
        token[...] = jnp.zeros_like(token)

    sems = [pltpu.SemaphoreType.DMA((7,))] * (2 * n)
    thru = list(lands) + ([] if srcs is None else list(srcs))
    outs = pl.pallas_call(
        body, name=name,
        out_shape=(*sems, *[pltpu.HBM(t.shape, t.dtype) for t in thru], jax.ShapeDtypeStruct((8, 128), F32)),
        in_specs=[*[_HBM] * len(thru), *[pl.BlockSpec(memory_space=pl.ANY)] * len(after)],
        out_specs=(*[_SEM] * (2 * n), *[_HBM] * len(thru), pl.BlockSpec(memory_space=pltpu.VMEM)),
        input_output_aliases={i: 2 * n + i for i in range(len(thru))},
        compiler_params=pltpu.CompilerParams(has_side_effects=pltpu.SideEffectType.DATAFLOW_SIDE_EFFECTING),
    )(*[pltpu.with_memory_space_constraint(t, pltpu.HBM) for t in thru], *after)
    lands_thru = outs[2 * n:3 * n]
    srcs_thru = None if srcs is None else outs[3 * n:4 * n]
    return outs[:n], outs[n:2 * n], srcs_thru, lands_thru, outs[-1]


def _exchange_wait(name, send_sems, recv_sems, srcs, lands, after):
    n = len(lands)
    thru = list(lands) + ([] if srcs is None else list(srcs))

    def body(*refs):
        lnd = refs[:n]
        src = None if srcs is None else refs[n:2 * n]
        ssem, rsem = refs[len(thru):len(thru) + n], refs[len(thru) + n:len(thru) + 2 * n]
        for a in range(n):
            for k, (peer, peer_rank) in enumerate(_peers()):
                cp = _exchange_copy(src, lnd, ssem, rsem, a, k, peer, peer_rank, peer_rank)
                cp.wait_send()
                cp.wait_recv()

    outs = pl.pallas_call(
        body, name=name,
        out_shape=tuple(pltpu.HBM(t.shape, t.dtype) for t in thru),
        in_specs=[*[_HBM] * len(thru), *[_SEM] * (2 * n), pl.BlockSpec(memory_space=pl.ANY)],
        out_specs=tuple([_HBM] * len(thru)),
        input_output_aliases={i: i for i in range(len(thru))},
        compiler_params=pltpu.CompilerParams(has_side_effects=pltpu.SideEffectType.DATAFLOW_SIDE_EFFECTING),
    )(*thru, *send_sems, *recv_sems, after)
    return outs[:n]


_SIDE = pltpu.CompilerParams(has_side_effects=pltpu.SideEffectType.DATAFLOW_SIDE_EFFECTING)
_ANY = pl.BlockSpec(memory_space=pl.ANY)


def _chips():
    x, y, _ = _my_pos()
    return [(1 - x, y), (x, 1 - y), (1 - x, 1 - y)]


def _slot_copy(lnd, slot, send_sem, recv_sem, to):
    return pltpu.make_async_remote_copy(src_ref=lnd.at[slot], dst_ref=lnd.at[slot], send_sem=send_sem,
                                        recv_sem=recv_sem, device_id=to, device_id_type=MESH)


def _gather2_start(name, lands, after=()):
    n = len(lands)
    n_in = n + len(after)

    def body(*refs):
        lnd = refs[:n]
        send, recv_sib, recv_ici = refs[n_in:n_in + n], refs[n_in + n:n_in + 2 * n], refs[n_in + 2 * n:n_in + 3 * n]
        x, y, c = _my_pos()
        me = 4 * x + 2 * y + c
        for a in range(n):
            _slot_copy(lnd[a], me, send[a].at[0], recv_sib[a].at[0], (x, y, 1 - c)).start()
            for j, chip in enumerate(_chips()):
                _slot_copy(lnd[a], me, send[a].at[1 + j], recv_ici[a].at[j], (*chip, c)).start()
        refs[-1][...] = jnp.zeros_like(refs[-1])

    sems = [pltpu.SemaphoreType.DMA((4,))] * n + [pltpu.SemaphoreType.DMA((1,))] * n + [pltpu.SemaphoreType.DMA((3,))] * n
    outs = pl.pallas_call(
        body, name=name,
        out_shape=(*sems, *[pltpu.HBM(t.shape, t.dtype) for t in lands], jax.ShapeDtypeStruct((8, 128), F32)),
        in_specs=[*[_HBM] * n, *[_ANY] * len(after)],
        out_specs=(*[_SEM] * (3 * n), *[_HBM] * n, pl.BlockSpec(memory_space=pltpu.VMEM)),
        input_output_aliases={i: 3 * n + i for i in range(n)}, compiler_params=_SIDE,
    )(*[pltpu.with_memory_space_constraint(t, pltpu.HBM) for t in lands], *after)
    return outs[:n], outs[n:2 * n], outs[2 * n:3 * n], outs[3 * n:4 * n], outs[-1]


def _gather2_forward(name, lands, recv_ici, after):
    n = len(lands)

    def body(*refs):
        lnd, arrived = refs[:n], refs[n:2 * n]
        send, recv = refs[2 * n + 1:3 * n + 1], refs[3 * n + 1:4 * n + 1]
        x, y, c = _my_pos()
        for j, (cx, cy) in enumerate(_chips()):
            slot = 4 * cx + 2 * cy + c
            for a in range(n):
                _slot_copy(lnd[a], slot, send[a].at[j], arrived[a].at[j], (cx, cy, c)).wait_recv()
                _slot_copy(lnd[a], slot, send[a].at[j], recv[a].at[j], (x, y, 1 - c)).start()
        refs[-1][...] = jnp.zeros_like(refs[-1])

    sems = [pltpu.SemaphoreType.DMA((3,))] * (2 * n)
    outs = pl.pallas_call(
        body, name=name,
        out_shape=(*sems, *[pltpu.HBM(t.shape, t.dtype) for t in lands], jax.ShapeDtypeStruct((8, 128), F32)),
        in_specs=[*[_HBM] * n, *[_SEM] * n, _ANY],
        out_specs=(*[_SEM] * (2 * n), *[_HBM] * n, pl.BlockSpec(memory_space=pltpu.VMEM)),
        input_output_aliases={i: 2 * n + i for i in range(n)}, compiler_params=_SIDE,
    )(*lands, *recv_ici, after)
    return outs[:n], outs[n:2 * n], outs[2 * n:3 * n], outs[-1]


def _gather2_wait(name, lands, send_a, recv_sib, send_b, recv_b, after):
    n = len(lands)

    def body(*refs):
        lnd = refs[:n]
        sa, rs, sb, rb = (refs[(1 + i) * n:(2 + i) * n] for i in range(4))
        x, y, c = _my_pos()
        me = 4 * x + 2 * y + c
        for a in range(n):
            for k in range(4):
                _slot_copy(lnd[a], me, sa[a].at[k], rs[a].at[0], (x, y, 1 - c)).wait_send()
            _slot_copy(lnd[a], me - c + (1 - c), sa[a].at[0], rs[a].at[0], (x, y, 1 - c)).wait_recv()
            for j, (cx, cy) in enumerate(_chips()):
                _slot_copy(lnd[a], 4 * cx + 2 * cy + c, sb[a].at[j], rb[a].at[j], (x, y, 1 - c)).wait_send()
                _slot_copy(lnd[a], 4 * cx + 2 * cy + (1 - c), sb[a].at[j], rb[a].at[j], (x, y, 1 - c)).wait_recv()

    outs = pl.pallas_call(
        body, name=name,
        out_shape=tuple(pltpu.HBM(t.shape, t.dtype) for t in lands),
        in_specs=[*[_HBM] * n, *[_SEM] * (4 * n), _ANY],
        out_specs=tuple([_HBM] * n),
        input_output_aliases={i: i for i in range(n)}, compiler_params=_SIDE,
    )(*lands, *send_a, *recv_sib, *send_b, *recv_b, after)
    return outs


NN = ((1,), (0,))
NT = ((1,), (1,))
TN = ((0,), (0,))


def _matmul(name, a, b, *, contract, grid, a_spec, b_spec, out_specs, out_shapes, acc_shape,
            extra=(), extra_specs=(), epilogue=None, deps=()):
    nk = grid[2]
    n_extra = len(extra)
    n_out = len(out_shapes)

    def body(*refs):
        a_ref, b_ref = refs[0], refs[1]
        extra_refs = refs[2:2 + n_extra]
        out_refs = refs[2 + n_extra:2 + n_extra + n_out]
        acc_ref = refs[-1]
        k = pl.program_id(2)

        def prod():
            if len(b_ref.shape) == 2:
                return lax.dot_general(a_ref[...].astype(BF16), b_ref[...].astype(BF16), (contract, ((), ())),
                                       preferred_element_type=F32)
            kk = a_ref.shape[1] // b_ref.shape[0]
            acc = None
            for u in range(b_ref.shape[0]):
                part = lax.dot_general(a_ref[:, u * kk:(u + 1) * kk].astype(BF16), b_ref[u].astype(BF16),
                                       (contract, ((), ())), preferred_element_type=F32)
                acc = part if acc is None else acc + part
            return acc

        def finish(acc):
            res = (acc,) if epilogue is None else epilogue(acc, *[r[...] for r in extra_refs])
            for o_ref, val in zip(out_refs, res):
                o_ref[...] = val.astype(o_ref.dtype)

        if nk == 1:
            finish(prod())
            return

        @pl.when(k == 0)
        def _():
            acc_ref[...] = prod()

        @pl.when((k > 0) & (k < nk - 1))
        def _():
            acc_ref[...] += prod()

        @pl.when(k == nk - 1)
        def _():
            finish(acc_ref[...] + prod())

    outs = _call(
        body, deps=deps, name=name, grid=grid,
        in_specs=[a_spec, b_spec, *extra_specs], out_specs=list(out_specs), out_shape=list(out_shapes),
        scratch_shapes=[pltpu.VMEM(acc_shape if nk > 1 else (8, 128), F32)],
        compiler_params=_params(("parallel", "parallel", "arbitrary")),
    )(a, b, *extra)
    return outs


def _tile(n, t):
    t = min(n, t)
    assert n % t == 0, (n, t)
    return t


def _mm_plain(name, a, b, contract, out_dtype, tm=1024, tn=1024, tk=2048, extra=(), epilogue=None,
              n_out=1, out_dtypes=None, deps=()):
    if contract == NN:
        (m, kd), (_, n) = a.shape, b.shape
    elif contract == NT:
        (m, kd), (n, _) = a.shape, b.shape
    else:
        (kd, m), (_, n) = a.shape, b.shape
    tm, tn, tk = _tile(m, tm), _tile(n, tn), _tile(kd, tk)
    if contract == NN:
        a_spec = pl.BlockSpec((tm, tk), lambda i, j, k: (i, k))
        b_spec = pl.BlockSpec((tk, tn), lambda i, j, k: (k, j))
    elif contract == NT:
        a_spec = pl.BlockSpec((tm, tk), lambda i, j, k: (i, k))
        b_spec = pl.BlockSpec((tn, tk), lambda i, j, k: (j, k))
    else:
        a_spec = pl.BlockSpec((tk, tm), lambda i, j, k: (k, i))
        b_spec = pl.BlockSpec((tk, tn), lambda i, j, k: (k, j))
    o_spec = pl.BlockSpec((tm, tn), lambda i, j, k: (i, j))
    out_dtypes = out_dtypes or [out_dtype] * n_out
    outs = _matmul(
        name, a, b, contract=contract, grid=(m // tm, n // tn, kd // tk), a_spec=a_spec, b_spec=b_spec,
        out_specs=[o_spec] * len(out_dtypes), out_shapes=[jax.ShapeDtypeStruct((m, n), dt) for dt in out_dtypes],
        acc_shape=(tm, tn), extra=extra, extra_specs=[o_spec] * len(extra), epilogue=epilogue, deps=deps)
    return outs[0] if len(out_dtypes) == 1 else outs


def _rows_call(name, body, row_in, vec_in, row_out, vec_out, s, deps=()):
    t = _tile(s, ROW_T)
    in_specs = []
    args = []
    for arr, width, cb in row_in:
        in_specs.append(pl.BlockSpec((t, width), functools.partial(lambda i, cb: (i, cb), cb=cb)))
        args.append(arr)
    for v in vec_in:
        in_specs.append(pl.BlockSpec(v.shape, lambda i: (0, 0)))
        args.append(v)
    out_specs = []
    out_shapes = []
    for width, dt in row_out:
        out_specs.append(pl.BlockSpec((t, width), lambda i: (i, 0)))
        out_shapes.append(jax.ShapeDtypeStruct((s, width), dt))
    for width in vec_out:
        out_specs.append(pl.BlockSpec((1, width), lambda i: (0, 0)))
        out_shapes.append(jax.ShapeDtypeStruct((1, width), F32))
    return _call(body, deps=deps, name=name, grid=(s // t,), in_specs=in_specs, out_specs=out_specs,
                 out_shape=out_shapes, compiler_params=_params(("arbitrary",)))(*args)


def _acc_vec(ref, val):
    _acc_row(ref, jnp.sum(val, axis=0, keepdims=True))


def _acc_row(ref, part):
    @pl.when(pl.program_id(0) == 0)
    def _():
        ref[...] = part

    @pl.when(pl.program_id(0) > 0)
    def _():
        ref[...] += part


def _rms(v):
    return lax.rsqrt(jnp.mean(v * v, axis=-1, keepdims=True) + EPS)


def _norm_bwd(dxn, xn, r):
    return r * (dxn - xn * jnp.mean(dxn * xn, axis=-1, keepdims=True))


def _premix(x, g, scale, shift, deps=()):
    s = x.shape[0]

    def body(x_ref, g_ref, sc_ref, sh_ref, h_ref):
        xv = x_ref[...]
        h_ref[...] = ((xv * _rms(xv) * g_ref[...]) * (1.0 + sc_ref[...]) + sh_ref[...]).astype(BF16)

    return _rows_call("premix", body, [(x, D_MODEL, 0)], [g, scale, shift], [(D_MODEL, BF16)], [], s, deps)[0]


def _sigmoid(z):
    return 1.0 / (1.0 + jnp.exp(-z))


def _mix_fwd(o_fox, o_gla, pm, g_fox, g_gla):
    s = o_fox.shape[0]

    def body(of_ref, og_ref, gr_ref, gf_ref, gg_ref, mix_ref):
        for h in range(FOX_HEADS):
            sl = slice(h * FOX_HEAD_DIM, (h + 1) * FOX_HEAD_DIM)
            seg = of_ref[:, sl]
            mix_ref[:, sl] = (seg * _rms(seg) * gf_ref[:, sl]).astype(BF16)
        for h in range(GLA_HEADS):
            sl = slice(h * GLA_DV, (h + 1) * GLA_DV)
            seg = og_ref[:, sl]
            gr = gr_ref[:, sl].astype(F32)
            val = (seg * _rms(seg) * gg_ref[:, sl]) * (gr * _sigmoid(gr))
            mix_ref[:, pl.ds(FOX_HEADS * FOX_HEAD_DIM + h * GLA_DV, GLA_DV)] = val.astype(BF16)

    return _rows_call("mix_fwd", body, [(o_fox, 1024, 0), (o_gla, 1024, 0), (pm, 1024, 5)], [g_fox, g_gla],
                      [(D_MODEL, BF16)], [], s)[0]


def _mix_bwd(dmix, o_fox, o_gla, pm, g_fox, g_gla, deps=()):
    s = o_fox.shape[0]

    def body(dm_ref, of_ref, og_ref, gr_ref, gf_ref, gg_ref, dof_ref, dog_ref, dgr_ref, dgf_ref, dgg_ref):
        dgf = []
        for h in range(FOX_HEADS):
            sl = slice(h * FOX_HEAD_DIM, (h + 1) * FOX_HEAD_DIM)
            seg = of_ref[:, sl]
            r = _rms(seg)
            segn = seg * r
            dout = dm_ref[:, sl]
            dgf.append(jnp.sum(dout * segn, axis=0, keepdims=True))
            dof_ref[:, sl] = _norm_bwd(dout * gf_ref[:, sl], segn, r).astype(BF16)
        dgg = []
        for h in range(GLA_HEADS):
            sl = slice(h * GLA_DV, (h + 1) * GLA_DV)
            seg = og_ref[:, sl]
            r = _rms(seg)
            segn = seg * r
            gl = segn * gg_ref[:, sl]
            gr = gr_ref[:, sl].astype(F32)
            sig = _sigmoid(gr)
            dout = dm_ref[:, pl.ds(FOX_HEADS * FOX_HEAD_DIM + h * GLA_DV, GLA_DV)]
            dgr_ref[:, sl] = (dout * gl * (sig * (1.0 + gr * (1.0 - sig)))).astype(BF16)
            dgl = dout * (gr * sig)
            dgg.append(jnp.sum(dgl * segn, axis=0, keepdims=True))
            dog_ref[:, sl] = _norm_bwd(dgl * gg_ref[:, sl], segn, r).astype(BF16)
        _acc_row(dgf_ref, jnp.concatenate(dgf, axis=1))
        _acc_row(dgg_ref, jnp.concatenate(dgg, axis=1))

    return _rows_call("mix_bwd", body, [(dmix, D_MODEL, 0), (o_fox, 1024, 0), (o_gla, 1024, 0), (pm, 1024, 5)],
                      [g_fox, g_gla], [(1024, BF16), (1024, BF16), (1024, BF16)], [1024, 1024], s, deps)


def _postmix_premlp(x, y, gate_m, g_post_mix, g_pre_mlp, scale_f, shift_f):
    s = x.shape[0]

    def body(x_ref, y_ref, gm_ref, gpm_ref, gpl_ref, sc_ref, sh_ref, x1_ref, h2_ref):
        yv = y_ref[...]
        x1 = x_ref[...] + gm_ref[...] * (yv * _rms(yv) * gpm_ref[...])
        x1_ref[...] = x1
        h2_ref[...] = ((x1 * _rms(x1) * gpl_ref[...]) * (1.0 + sc_ref[...]) + sh_ref[...]).astype(BF16)

    return _rows_call("postmix_premlp", body, [(x, D_MODEL, 0), (y, D_MODEL, 0)],
                      [gate_m, g_post_mix, g_pre_mlp, scale_f, shift_f], [(D_MODEL, F32), (D_MODEL, BF16)], [], s)


def _loss_postmlp_bwd(x1, y2, target, gate_f, g_post_mlp):
    s = x1.shape[0]

    def body(x1_ref, y2_ref, t_ref, gf_ref, g_ref, dx2_ref, dy2_ref, loss_ref, dgate_ref, dg_ref):
        yv = y2_ref[...]
        r = _rms(yv)
        yn = yv * r
        o = yn * g_ref[...]
        e = (x1_ref[...] + gf_ref[...] * o) - t_ref[...]
        part = 0.5 * jnp.sum(jnp.mean(e * e, axis=-1, keepdims=True), axis=0, keepdims=True)
        _acc_vec(loss_ref, jnp.broadcast_to(part, (1, 128)))
        dx2 = e * (1.0 / D_MODEL)
        dx2_ref[...] = dx2
        _acc_vec(dgate_ref, dx2 * o)
        do = dx2 * gf_ref[...]
        _acc_vec(dg_ref, do * yn)
        dy2_ref[...] = _norm_bwd(do * g_ref[...], yn, r).astype(BF16)

    return _rows_call("loss_postmlp_bwd", body, [(x1, D_MODEL, 0), (y2, D_MODEL, 0), (target, D_MODEL, 0)],
                      [gate_f, g_post_mlp], [(D_MODEL, F32), (D_MODEL, BF16)], [128, D_MODEL, D_MODEL], s)


def _premlp_postmix_bwd(dh2, dx2, x1, y, scale_f, g_pre_mlp, gate_m, g_post_mix, deps=()):
    s = x1.shape[0]

    def body(dh2_ref, dx2_ref, x1_ref, y_ref, sc_ref, gpl_ref, gm_ref, gpm_ref,
             dx1_ref, dy_ref, dsc_ref, dsh_ref, dgpl_ref, dgm_ref, dgpm_ref):
        x1 = x1_ref[...]
        r1 = _rms(x1)
        x1n = x1 * r1
        dh2 = dh2_ref[...]
        _acc_vec(dsc_ref, dh2 * (x1n * gpl_ref[...]))
        _acc_vec(dsh_ref, dh2)
        dn2 = dh2 * (1.0 + sc_ref[...])
        _acc_vec(dgpl_ref, dn2 * x1n)
        dx1 = dx2_ref[...] + _norm_bwd(dn2 * gpl_ref[...], x1n, r1)
        dx1_ref[...] = dx1
        yv = y_ref[...]
        ry = _rms(yv)
        yn = yv * ry
        _acc_vec(dgm_ref, dx1 * (yn * gpm_ref[...]))
        do = dx1 * gm_ref[...]
        _acc_vec(dgpm_ref, do * yn)
        dy_ref[...] = _norm_bwd(do * gpm_ref[...], yn, ry).astype(BF16)

    return _rows_call("premlp_postmix_bwd", body,
                      [(dh2, D_MODEL, 0), (dx2, D_MODEL, 0), (x1, D_MODEL, 0), (y, D_MODEL, 0)],
                      [scale_f, g_pre_mlp, gate_m, g_post_mix], [(D_MODEL, F32), (D_MODEL, BF16)],
                      [D_MODEL] * 5, s, deps)


def _premix_bwd(dh, dx1, x, g_pre_mix, scale_m):
    s = x.shape[0]

    def body(dh_ref, dx1_ref, x_ref, g_ref, sc_ref, gx_ref, dsc_ref, dsh_ref, dg_ref):
        xv = x_ref[...]
        r = _rms(xv)
        xn = xv * r
        dh = dh_ref[...]
        _acc_vec(dsc_ref, dh * (xn * g_ref[...]))
        _acc_vec(dsh_ref, dh)
        dn1 = dh * (1.0 + sc_ref[...])
        _acc_vec(dg_ref, dn1 * xn)
        gx_ref[...] = dx1_ref[...] + _norm_bwd(dn1 * g_ref[...], xn, r)

    return _rows_call("premix_bwd", body, [(dh, D_MODEL, 0), (dx1, D_MODEL, 0), (x, D_MODEL, 0)],
                      [g_pre_mix, scale_m], [(D_MODEL, F32)], [D_MODEL] * 3, s)


def _split3(v):
    hi = v.astype(BF16)
    r1 = v - hi.astype(F32)
    mid = r1.astype(BF16)
    lo = (r1 - mid.astype(F32)).astype(BF16)
    return hi, mid, lo


def _dot_exact01(v, tri, contract=NN, tri_first=False):
    acc = None
    for part in _split3(v):
        lhs, rhs = (tri, part) if tri_first else (part, tri)
        p = lax.dot_general(lhs, rhs, (contract, ((), ())), preferred_element_type=F32)
        acc = p if acc is None else acc + p
    return acc


def _log_sigmoid(z):
    return jnp.minimum(z, 0.0) - jnp.log(1.0 + jnp.exp(-jnp.abs(z)))


def _fox_cum(small, bvec):
    s = small.shape[0]
    t = _tile(s, CUM_T)

    def body(sm_ref, b_ref, out_ref, carry):
        @pl.when(pl.program_id(0) == 0)
        def _():
            carry[...] = jnp.zeros_like(carry)

        lf = _log_sigmoid(sm_ref[...] + b_ref[...])
        lft = lf.T[0:FOX_HEADS, :]
        row = lax.broadcasted_iota(jnp.int32, (t, t), 0)
        col = lax.broadcasted_iota(jnp.int32, (t, t), 1)
        upper = (row <= col).astype(BF16)
        cum = _dot_exact01(lft, upper) + carry[:, 0:1]
        out_ref[...] = cum
        carry[...] = carry[...] + jnp.sum(lft, axis=1, keepdims=True)

    return _call(body, name="fox_cum", grid=(s // t,),
                 in_specs=[pl.BlockSpec((t, W_SMALL), lambda i: (i, 0)), pl.BlockSpec((1, W_SMALL), lambda i: (0, 0))],
                 out_specs=pl.BlockSpec((FOX_HEADS, t), lambda i: (0, i)),
                 out_shape=jax.ShapeDtypeStruct((FOX_HEADS, s), F32),
                 scratch_shapes=[pltpu.VMEM((FOX_HEADS, 128), F32)],
                 compiler_params=_params(("arbitrary",)))(small, bvec)


def _fox_cum_bwd(dc, dcq, small, bvec):
    s = small.shape[0]
    t = _tile(s, CUM_T)
    nb = s // t

    def body(dc_ref, dcq_ref, sm_ref, b_ref, out_ref, db_ref, carry):
        @pl.when(pl.program_id(0) == 0)
        def _():
            carry[...] = jnp.zeros_like(carry)
            db_ref[...] = jnp.zeros_like(db_ref)

        lane = lax.broadcasted_iota(jnp.int32, (t, W_SMALL), 1)
        dcq = jnp.zeros((t, W_SMALL), F32)
        for hh in range(FOX_HEADS):
            dcq = jnp.where(lane == hh, dcq_ref[hh], dcq)
        dcv = dc_ref[...] + dcq.T[0:FOX_HEADS, :]
        row = lax.broadcasted_iota(jnp.int32, (t, t), 0)
        col = lax.broadcasted_iota(jnp.int32, (t, t), 1)
        lower = (row >= col).astype(BF16)
        dlf = _dot_exact01(dcv, lower) + carry[:, 0:1]
        carry[...] = carry[...] + jnp.sum(dcv, axis=1, keepdims=True)
        z = sm_ref[...] + b_ref[...]
        zt = z.T[0:FOX_HEADS, :]
        dff = dlf * _sigmoid(-zt)
        db_ref[...] = db_ref[...] + jnp.sum(dff, axis=1, keepdims=True)
        full = jnp.concatenate([dff, jnp.zeros((W_SMALL - FOX_HEADS, t), F32)], axis=0)
        out_ref[...] = full.T

    return _call(body, name="fox_cum_bwd", grid=(nb,),
                 in_specs=[pl.BlockSpec((FOX_HEADS, t), lambda i: (0, nb - 1 - i)),
                           pl.BlockSpec((FOX_HEADS, t, 1), lambda i: (0, nb - 1 - i, 0)),
                           pl.BlockSpec((t, W_SMALL), lambda i: (nb - 1 - i, 0)),
                           pl.BlockSpec((1, W_SMALL), lambda i: (0, 0))],
                 out_specs=[pl.BlockSpec((t, W_SMALL), lambda i: (nb - 1 - i, 0)),
                            pl.BlockSpec((FOX_HEADS, 128), lambda i: (0, 0))],
                 out_shape=[jax.ShapeDtypeStruct((s, W_SMALL), F32), jax.ShapeDtypeStruct((FOX_HEADS, 128), F32)],
                 scratch_shapes=[pltpu.VMEM((FOX_HEADS, 128), F32)],
                 compiler_params=_params(("arbitrary",)))(dc, dcq, small, bvec)


FOX_SCALE = FOX_HEAD_DIM ** -0.5


def _fox_fwd(pm, crow):
    s = pm.shape[0]
    t = _tile(s, FOX_T)
    nb = s // t
    parts = 2
    hq = t // parts

    def body(q_ref, k_ref, v_ref, c_ref, o_ref, lse_ref):
        i = pl.program_id(1)
        qs = [q_ref[g * hq:(g + 1) * hq, :] for g in range(parts)]

        def block(j, carry, diagonal):
            rows = pl.ds(pl.multiple_of(j * t, t), t)
            k_all, v_all, c_all = k_ref[rows, :], v_ref[rows, :], c_ref[j]
            out = []
            for g, (m_prev, l_prev, acc) in enumerate(carry):
                nk = (g + 1) * hq if diagonal else t
                kb, vb, cb = k_all[:nk], v_all[:nk], c_all[:, :nk]
                sc = lax.dot_general(qs[g], kb, (NT, ((), ())), preferred_element_type=F32)
                sc = sc * FOX_SCALE - cb
                if diagonal:
                    row = lax.broadcasted_iota(jnp.int32, (hq, nk), 0) + g * hq
                    col = lax.broadcasted_iota(jnp.int32, (hq, nk), 1)
                    sc = jnp.where(row >= col, sc, NEG)
                m_new = jnp.maximum(m_prev, jnp.max(sc, axis=1, keepdims=True))
                alpha = jnp.exp(m_prev - m_new)
                p = jnp.exp(sc - m_new)
                l_new = alpha * l_prev + jnp.sum(p, axis=1, keepdims=True)
                pv = jnp.dot(p.astype(BF16), vb, preferred_element_type=F32)
                out.append((m_new, l_new, alpha * acc + pv))
            return tuple(out)

        init = tuple((jnp.full((hq, 1), NEG, F32), jnp.zeros((hq, 1), F32), jnp.zeros((hq, 128), F32))
                     for _ in range(parts))
        carry = lax.fori_loop(0, i, lambda j, cr: block(j, cr, False), init)
        carry = block(i, carry, True)
        for g, (m_fin, l_fin, acc) in enumerate(carry):
            o_ref[g * hq:(g + 1) * hq, :] = acc / l_fin
            lse_ref[g * hq:(g + 1) * hq, :] = m_fin + jnp.log(l_fin)

    return _call(
        body, name="fox_fwd", grid=(FOX_HEADS, nb),
        in_specs=[pl.BlockSpec((t, 128), lambda h, i: (i, h)),
                  pl.BlockSpec((s, 128), lambda h, i: (0, FOX_HEADS + h)),
                  pl.BlockSpec((s, 128), lambda h, i: (0, 2 * FOX_HEADS + h)),
                  pl.BlockSpec((None, nb, 1, t), lambda h, i: (h, 0, 0, 0))],
        out_specs=[pl.BlockSpec((t, 128), lambda h, i: (i, h)),
                   pl.BlockSpec((None, t, 1), lambda h, i: (h, i, 0))],
        out_shape=[jax.ShapeDtypeStruct((s, FOX_HEADS * 128), F32), jax.ShapeDtypeStruct((FOX_HEADS, s, 1), F32)],
        compiler_params=_params(("parallel", "arbitrary")),
    )(pm, pm, pm, crow.reshape(FOX_HEADS, nb, 1, t))


def _fox_bwd(pm, crow, o, lse, do):
    s = pm.shape[0]
    t = _tile(s, FOX_T)
    nb = s // t

    parts = 2
    hq = t // parts

    def body(q_ref, do_ref, o_ref, lse_ref, k_ref, v_ref, c_ref, dq_ref, dk_ref, dv_ref, dc_ref, dcq_ref, delta_s):
        j = pl.program_id(1)

        @pl.when(j == 0)
        def _():
            dq_ref[...] = jnp.zeros_like(dq_ref)
            dcq_ref[...] = jnp.zeros_like(dcq_ref)
            delta_s[...] = jnp.sum(do_ref[...].astype(F32) * o_ref[...], axis=1, keepdims=True)

        k_all, v_all, c_all = k_ref[...], v_ref[...], c_ref[...]

        def grow(acc, part, axis):
            n = part.shape[axis]
            if n == acc.shape[axis]:
                return acc + part
            if axis == 0:
                return jnp.concatenate([acc[:n] + part, acc[n:]], axis=0)
            return jnp.concatenate([acc[:, :n] + part, acc[:, n:]], axis=1)

        def block(i, carry, diagonal):
            dk_acc, dv_acc, dc_acc = carry
            for g in range(parts):
                nk = (g + 1) * hq if diagonal else t
                kb, vb, cb = k_all[:nk], v_all[:nk], c_all[:, :nk]
                rows = pl.ds(pl.multiple_of(i * t + g * hq, hq), hq)
                q, dov = q_ref[rows, :], do_ref[rows, :]
                sc = lax.dot_general(q, kb, (NT, ((), ())), preferred_element_type=F32)
                p = jnp.exp(sc * FOX_SCALE - cb - lse_ref[rows, :])
                if diagonal:
                    row = lax.broadcasted_iota(jnp.int32, (hq, nk), 0) + g * hq
                    col = lax.broadcasted_iota(jnp.int32, (hq, nk), 1)
                    p = jnp.where(row >= col, p, 0.0)
                dp = lax.dot_general(dov, vb, (NT, ((), ())), preferred_element_type=F32)
                ds = p * (dp - delta_s[rows, :])
                dsb = ds.astype(BF16)
                dv_acc = grow(dv_acc, lax.dot_general(p.astype(BF16), dov, (TN, ((), ())),
                                                      preferred_element_type=F32), 0)
                dk_acc = grow(dk_acc, lax.dot_general(dsb, q, (TN, ((), ())), preferred_element_type=F32), 0)
                dq_ref[rows, :] += jnp.dot(dsb, kb, preferred_element_type=F32) * FOX_SCALE
                dc_acc = grow(dc_acc, -jnp.sum(ds, axis=0, keepdims=True), 1)
                dcq_ref[rows, :] += jnp.sum(ds, axis=1, keepdims=True)
            return dk_acc, dv_acc, dc_acc

        carry = (jnp.zeros((t, 128), F32), jnp.zeros((t, 128), F32), jnp.zeros((1, t), F32))
        carry = block(j, carry, True)
        dk_acc, dv_acc, dc_acc = lax.fori_loop(j + 1, nb, lambda i, cr: block(i, cr, False), carry)
        dk_ref[...] = (dk_acc * FOX_SCALE).astype(dk_ref.dtype)
        dv_ref[...] = dv_acc.astype(dv_ref.dtype)
        dc_ref[...] = dc_acc

    whole = lambda h, j: (0, h)
    return _call(
        body, name="fox_bwd", grid=(FOX_HEADS, nb),
        in_specs=[pl.BlockSpec((s, 128), whole), pl.BlockSpec((s, 128), whole), pl.BlockSpec((s, 128), whole),
                  pl.BlockSpec((None, s, 1), lambda h, j: (h, 0, 0)),
                  pl.BlockSpec((t, 128), lambda h, j: (j, FOX_HEADS + h)),
                  pl.BlockSpec((t, 128), lambda h, j: (j, 2 * FOX_HEADS + h)),
                  pl.BlockSpec((None, 1, t), lambda h, j: (h, 0, j))],
        out_specs=[pl.BlockSpec((s, 128), whole),
                   pl.BlockSpec((t, 128), lambda h, j: (j, h)),
                   pl.BlockSpec((t, 128), lambda h, j: (j, h)),
                   pl.BlockSpec((None, 1, t), lambda h, j: (h, 0, j)),
                   pl.BlockSpec((None, s, 1), lambda h, j: (h, 0, 0))],
        out_shape=[jax.ShapeDtypeStruct((s, 1024), F32), jax.ShapeDtypeStruct((s, 1024), BF16),
                   jax.ShapeDtypeStruct((s, 1024), BF16), jax.ShapeDtypeStruct((FOX_HEADS, 1, s), F32),
                   jax.ShapeDtypeStruct((FOX_HEADS, s, 1), F32)],
        scratch_shapes=[pltpu.VMEM((s, 1), F32)],
        compiler_params=_params(("parallel", "arbitrary")),
    )(pm, do, o, lse, pm, pm, crow)


GLA_SCALE = GLA_DK ** -0.5
GLA_Q_BLK = 3072 // 128
GLA_K_BLK = 3584 // 128
GLA_V_BLK = 4096 // 256


def _gla_gate(sm, wa_ref, b_ref):
    return jnp.dot(sm.astype(BF16), wa_ref[...], preferred_element_type=F32) + b_ref[...]


def _chunk_tri(n, kind):
    row = lax.broadcasted_iota(jnp.int32, (n, n), 0)
    col = lax.broadcasted_iota(jnp.int32, (n, n), 1)
    shift = CHUNK.bit_length() - 1
    same = (row >> shift) == (col >> shift)
    if kind == "upto":
        same = same & (row >= col)
    elif kind == "before":
        same = same & (row > col)
    return same.astype(BF16)


def _gla_fwd(pm, small, wa_pad, b_a2, deps=()):
    s = pm.shape[0]
    r = _tile(s, GLA_R)
    nc = r // CHUNK

    def body(q_ref, k_ref, v_ref, sm_ref, wa_ref, b_ref, o_ref, st_ref, state):
        @pl.when(pl.program_id(1) == 0)
        def _():
            state[...] = jnp.zeros_like(state)

        la_all = _log_sigmoid(_gla_gate(sm_ref[...], wa_ref, b_ref)) * (1.0 / GLA_TEMP)
        tri = _chunk_tri(CHUNK, "upto")
        uts, decays = [], []
        for c in range(nc):
            rows = slice(c * CHUNK, (c + 1) * CHUNK)
            la = la_all[rows]
            cum = _dot_exact01(la, tri, tri_first=True)
            total = jnp.sum(la, axis=0, keepdims=True)
            kdec = k_ref[rows, :].astype(F32) * jnp.exp(total - cum)
            uts.append(lax.dot_general(v_ref[rows, :], kdec.astype(BF16), (TN, ((), ())),
                                       preferred_element_type=F32))
            decays.append(jnp.exp(total))
        cur = state[...]
        ends = []
        for c in range(nc):
            cur = cur * decays[c] + uts[c]
            ends.append(cur.astype(BF16))
        state[...] = cur
        for c in range(nc):
            rows = slice(c * CHUNK, (c + 1) * CHUNK)
            st_ref[c] = ends[c]
            qs = (q_ref[rows, :].astype(F32) * GLA_SCALE).astype(BF16)
            o_ref[rows, :] = lax.dot_general(qs, ends[c], (NT, ((), ())), preferred_element_type=F32)

    return _call(
        body, deps=deps, name="gla_fwd", grid=(GLA_HEADS, s // r),
        in_specs=[pl.BlockSpec((r, 128), lambda h, i: (i, GLA_Q_BLK + h)),
                  pl.BlockSpec((r, 128), lambda h, i: (i, GLA_K_BLK + h)),
                  pl.BlockSpec((r, 256), lambda h, i: (i, GLA_V_BLK + h)),
                  pl.BlockSpec((r, W_SMALL), lambda h, i: (i, 0)),
                  pl.BlockSpec((W_SMALL, 128), lambda h, i: (0, h)),
                  pl.BlockSpec((1, 128), lambda h, i: (0, h))],
        out_specs=[pl.BlockSpec((r, 256), lambda h, i: (i, h)),
                   pl.BlockSpec((nc, None, GLA_DV, GLA_DK), lambda h, i: (i, h, 0, 0))],
        out_shape=[jax.ShapeDtypeStruct((s, 1024), F32),
                   jax.ShapeDtypeStruct((s // CHUNK, GLA_HEADS, GLA_DV, GLA_DK), BF16)],
        scratch_shapes=[pltpu.VMEM((GLA_DV, GLA_DK), F32)],
        compiler_params=_params(("parallel", "arbitrary")),
    )(pm, pm, pm, small, wa_pad, b_a2)


def _gla_bwd(pm, small, wa_pad, b_a2, states, do):
    s = pm.shape[0]
    r = _tile(s, GLA_R)
    nc = r // CHUNK
    nb = s // r

    def body(q_ref, k_ref, v_ref, sm_ref, wa_ref, b_ref, do_ref, st_ref, prev_ref,
             dq_ref, dk_ref, dv_ref, dza_ref, db_ref, carry):
        step = pl.program_id(1)

        @pl.when(step == 0)
        def _():
            carry[...] = jnp.zeros_like(carry)
            db_ref[...] = jnp.zeros_like(db_ref)

        z_all = _gla_gate(sm_ref[...], wa_ref, b_ref)
        la_all = _log_sigmoid(z_all) * (1.0 / GLA_TEMP)
        tri = _chunk_tri(CHUNK, "upto")
        tri_strict = _chunk_tri(CHUNK, "before")
        ws, decays, kdecs, gouts = [], [], [], []
        for c in range(nc):
            rows = slice(c * CHUNK, (c + 1) * CHUNK)
            la = la_all[rows]
            cum = _dot_exact01(la, tri, tri_first=True)
            total = jnp.sum(la, axis=0, keepdims=True)
            w = jnp.exp(total - cum)
            ws.append(w)
            decays.append(jnp.exp(total))
            kdecs.append(k_ref[rows, :].astype(F32) * w)
            dov = do_ref[rows, :]
            qs = (q_ref[rows, :].astype(F32) * GLA_SCALE).astype(BF16)
            dq_ref[rows, :] = (jnp.dot(dov, st_ref[c], preferred_element_type=F32) * GLA_SCALE).astype(BF16)
            gouts.append(lax.dot_general(dov, qs, (TN, ((), ())), preferred_element_type=F32))
        cur = carry[...]
        gts = [None] * nc
        for c in reversed(range(nc)):
            gts[c] = gouts[c] + cur
            cur = gts[c] * decays[c]
        carry[...] = cur
        db = jnp.zeros((1, 128), F32)
        for c in range(nc):
            rows = slice(c * CHUNK, (c + 1) * CHUNK)
            gtb = gts[c].astype(BF16)
            dv_ref[rows, :] = lax.dot_general(kdecs[c].astype(BF16), gtb, (NT, ((), ())),
                                              preferred_element_type=F32).astype(BF16)
            dkdec = jnp.dot(v_ref[rows, :], gtb, preferred_element_type=F32)
            dk_ref[rows, :] = (dkdec * ws[c]).astype(BF16)
            e = dkdec * kdecs[c]
            if c > 0:
                prev = st_ref[c - 1].astype(F32)
            else:
                prev = jnp.where(step == nb - 1, 0.0, prev_ref[0].astype(F32))
            dtot = jnp.sum(gts[c] * prev, axis=0, keepdims=True) * decays[c]
            dla = dtot + _dot_exact01(e, tri_strict, tri_first=True)
            dza = dla * (1.0 / GLA_TEMP) * _sigmoid(-z_all[rows])
            dza_ref[rows, :] = dza.astype(BF16)
            db = db + jnp.sum(dza, axis=0, keepdims=True)
        db_ref[...] += db

    blk = lambda h, i: nb - 1 - i
    return _call(
        body, name="gla_bwd", grid=(GLA_HEADS, nb),
        in_specs=[pl.BlockSpec((r, 128), lambda h, i: (blk(h, i), GLA_Q_BLK + h)),
                  pl.BlockSpec((r, 128), lambda h, i: (blk(h, i), GLA_K_BLK + h)),
                  pl.BlockSpec((r, 256), lambda h, i: (blk(h, i), GLA_V_BLK + h)),
                  pl.BlockSpec((r, W_SMALL), lambda h, i: (blk(h, i), 0)),
                  pl.BlockSpec((W_SMALL, 128), lambda h, i: (0, h)),
                  pl.BlockSpec((1, 128), lambda h, i: (0, h)),
                  pl.BlockSpec((r, 256), lambda h, i: (blk(h, i), h)),
                  pl.BlockSpec((nc, None, GLA_DV, GLA_DK), lambda h, i: (blk(h, i), h, 0, 0)),
                  pl.BlockSpec((1, None, GLA_DV, GLA_DK),
                               lambda h, i: (jnp.maximum(blk(h, i) * nc - 1, 0), h, 0, 0))],
        out_specs=[pl.BlockSpec((r, 128), lambda h, i: (blk(h, i), h)),
                   pl.BlockSpec((r, 128), lambda h, i: (blk(h, i), h)),
                   pl.BlockSpec((r, 256), lambda h, i: (blk(h, i), h)),
                   pl.BlockSpec((r, 128), lambda h, i: (blk(h, i), h)),
                   pl.BlockSpec((1, 128), lambda h, i: (0, h))],
        out_shape=[jax.ShapeDtypeStruct((s, 512), BF16), jax.ShapeDtypeStruct((s, 512), BF16),
                   jax.ShapeDtypeStruct((s, 1024), BF16), jax.ShapeDtypeStruct((s, 512), BF16),
                   jax.ShapeDtypeStruct((1, 512), F32)],
        scratch_shapes=[pltpu.VMEM((GLA_DV, GLA_DK), F32)],
        compiler_params=_params(("parallel", "arbitrary")),
    )(pm, pm, pm, small, wa_pad, b_a2, do, states, states)


def _modulation(c_all, w_ada):
    n = w_ada.shape[1]
    tn = _tile(n, 512)

    def body(c_ref, w_ref, out_ref, ca_ref):
        cv = c_ref[...]
        ca = cv * _sigmoid(cv)
        ca_ref[...] = ca
        out_ref[...] = jnp.dot(ca.astype(BF16), w_ref[...].astype(BF16), preferred_element_type=F32)

    return _call(body, name="modulation", grid=(n // tn,),
                 in_specs=[pl.BlockSpec((N_DEV, D_MODEL), lambda j: (0, 0)),
                           pl.BlockSpec((D_MODEL, tn), lambda j: (0, j))],
                 out_specs=[pl.BlockSpec((N_DEV, tn), lambda j: (0, j)),
                            pl.BlockSpec((N_DEV, D_MODEL), lambda j: (0, 0))],
                 out_shape=[jax.ShapeDtypeStruct((N_DEV, n), F32), jax.ShapeDtypeStruct((N_DEV, D_MODEL), F32)],
                 compiler_params=_params(("arbitrary",)))(c_all, w_ada)


def _adamw_math(w, g, m, v):
    m = ADAM_B1 * m + (1.0 - ADAM_B1) * g
    v = ADAM_B2 * v + (1.0 - ADAM_B2) * (g * g)
    m_hat = m / (1.0 - ADAM_B1 ** ADAM_STEP)
    v_hat = v / (1.0 - ADAM_B2 ** ADAM_STEP)
    delta = -ADAM_LR * (m_hat / (jnp.sqrt(v_hat) + ADAM_EPS) + ADAM_WD * w)
    return delta, m, v


def _adamw_slabs(name, w, slabs, m, v, tr=256):
    rr, cc = w.shape

    def body(w_ref, s_ref, m_ref, v_ref, g_ref, d_ref, nm_ref, nv_ref):
        g = s_ref[0].astype(F32)
        for r in range(1, N_DEV):
            g = g + s_ref[r].astype(F32)
        g_ref[...] = g
        d, nm, nv = _adamw_math(w_ref[...], g, m_ref[...], v_ref[...])
        d_ref[...] = d
        nm_ref[...] = nm
        nv_ref[...] = nv

    steps, spec, slab_spec = _plane_tiles(rr, cc, tr)
    return _call(body, name=name, grid=(steps,),
                 in_specs=[spec, slab_spec, spec, spec],
                 out_specs=[spec] * 4, out_shape=[jax.ShapeDtypeStruct((rr, cc), F32)] * 4,
                 compiler_params=_params(("parallel",)))(w, slabs, m, v)


def _adamw_ada(w, cat, dm, m, v, tr=256, deps=()):
    rr, cc = w.shape
    tr = _tile(rr, tr)

    def body(w_ref, ca_ref, dm_ref, m_ref, v_ref, g_ref, d_ref, nm_ref, nv_ref):
        g = ca_ref[:, 0:1] * dm_ref[0:1, :]
        for b in range(1, N_DEV):
            g = g + ca_ref[:, b:b + 1] * dm_ref[b:b + 1, :]
        g_ref[...] = g
        d, nm, nv = _adamw_math(w_ref[...], g, m_ref[...], v_ref[...])
        d_ref[...] = d
        nm_ref[...] = nm
        nv_ref[...] = nv

    spec = pl.BlockSpec((tr, cc), lambda i: (i, 0))
    return _call(body, deps=deps, name="adamw_ada", grid=(rr // tr,),
                 in_specs=[spec, pl.BlockSpec((tr, N_DEV), lambda i: (i, 0)),
                           pl.BlockSpec((N_DEV, cc), lambda i: (0, 0)), spec, spec],
                 out_specs=[spec] * 4, out_shape=[jax.ShapeDtypeStruct((rr, cc), F32)] * 4,
                 compiler_params=_params(("parallel",)))(w, cat, dm, m, v)


def _sum_devices(gathered):
    ln = gathered.shape[-1]

    def body(g_ref, out_ref):
        acc = g_ref[0]
        for r in range(1, N_DEV):
            acc = acc + g_ref[r]
        out_ref[...] = acc

    return _call(body, name="sum_devices",
                 in_specs=[pl.BlockSpec(memory_space=pltpu.VMEM)], out_specs=pl.BlockSpec(memory_space=pltpu.VMEM),
                 out_shape=jax.ShapeDtypeStruct((1, ln), F32))(gathered)


def _adamw_flat(w, g, m, v):
    def body(w_ref, g_ref, m_ref, v_ref, d_ref, nm_ref, nv_ref):
        d, nm, nv = _adamw_math(w_ref[...], g_ref[...], m_ref[...], v_ref[...])
        d_ref[...] = d
        nm_ref[...] = nm
        nv_ref[...] = nv

    vm = pl.BlockSpec(memory_space=pltpu.VMEM)
    return _call(body, name="adamw_small", in_specs=[vm] * 4, out_specs=[vm] * 3,
                 out_shape=[jax.ShapeDtypeStruct(w.shape, F32)] * 3)(w, g, m, v)


def _from_col_shards(g):
    return jnp.transpose(g, (1, 0, 2)).reshape(g.shape[1], N_DEV * g.shape[2])


def _pad_lanes(v, n):
    return jnp.concatenate([v, jnp.zeros(v.shape[:-1] + (n - v.shape[-1],), v.dtype)], axis=-1)


def kernel(x, c, w_ada, b_ada, g_pre_mix, g_post_mix, w_in, b_fgate, w_gla_a2, b_gla_a2, g_fox_out, g_gla_out, w_out, g_pre_mlp, g_post_mlp, w_mlp_in, w_mlp_out, loss_target, m_w_ada, m_b_ada, m_g_pre_mix, m_g_post_mix, m_w_in, m_b_fgate, m_w_gla_a2, m_b_gla_a2, m_g_fox_out, m_g_gla_out, m_w_out, m_g_pre_mlp, m_g_post_mlp, m_w_mlp_in, m_w_mlp_out, v_w_ada, v_b_ada, v_g_pre_mix, v_g_post_mix, v_w_in, v_b_fgate, v_w_gla_a2, v_b_gla_a2, v_g_fox_out, v_g_gla_out, v_w_out, v_g_pre_mlp, v_g_post_mlp, v_w_mlp_in, v_w_mlp_out):
    rank = _my_rank()
    xs = x[0]
    s = xs.shape[0]
    target = loss_target[0]

    w_in_t, m_in_t, v_in_t = w_in[0].T, m_w_in[0].T, v_w_in[0].T
    c_all, wa2_g, ggla_g, win_g = _all_gather("gather_first", [c, w_gla_a2[0], g_gla_out[0], w_in_t.astype(BF16)])
    rest = [_own_slot("own_w_out", w_out[0], True, rank), _own_slot("own_w_mlp_in", w_mlp_in[0], True, rank)]
    gs_send, gs_sib, gs_ici, gs_land, gs_token = _gather2_start("gather_rest_start", rest, after=(c_all,))
    last = [_own_slot("own_w_mlp_out", w_mlp_out[0], True, rank)]
    gl_send, gl_sib, gl_ici, gl_land, gl_token = _gather2_start("gather_last_start", last, after=(gs_token,))
    w_a2 = _from_col_shards(wa2_g)
    g_gla = _from_col_shards(ggla_g).reshape(1, 1024)
    g_fox = g_fox_out.reshape(1, 1024)
    win_full = win_g.reshape(N_DEV * 771, D_MODEL)
    w_main = jnp.concatenate([win_full[:3072], win_full[3080:5128], win_full[5144:6168]], axis=0)
    w_small = jnp.concatenate([win_full[3072:3080], win_full[5128:5144],
                               jnp.zeros((W_SMALL - 24, D_MODEL), BF16)], axis=0)
    wa_pad =jnp.concatenate([jnp.zeros((8, 512), BF16), w_a2.astype(BF16), jnp.zeros((104, 512), BF16)], axis=0)
    bf_vec = _pad_lanes(b_fgate, W_SMALL)

    mod_part, c_act = _modulation(c_all.reshape(N_DEV, D_MODEL), w_ada[0])
    (mod_g,) = _all_gather("gather_mod", [mod_part])
    mod = lax.dynamic_slice_in_dim(mod_g, rank, 1, axis=1).reshape(1, 6 * D_MODEL) + b_ada
    shift_m, scale_m, gate_m, shift_f, scale_f, gate_f = [mod[:, i * D_MODEL:(i + 1) * D_MODEL] for i in range(6)]

    h = _premix(xs, g_pre_mix, scale_m, shift_m, deps=(gl_token,))
    pm = _mm_plain("proj_main", h, w_main, NT, BF16)
    small = _mm_plain("proj_small", h, w_small, NT, F32)
    crow = _fox_cum(small, bf_vec).reshape(FOX_HEADS, 1, s)
    o_fox, lse = _fox_fwd(pm, crow)
    gs_fsend, gs_frecv, gs_land, gs_ftoken = _gather2_forward("gather_rest_forward", gs_land, gs_ici, o_fox)
    o_gla, states = _gla_fwd(pm, small, wa_pad, b_gla_a2, deps=(gs_ftoken,))
    mix = _mix_fwd(o_fox, o_gla, pm, g_fox, g_gla)
    wout_g, wmi_g = _gather2_wait("gather_rest_wait", gs_land, gs_send, gs_sib, gs_fsend, gs_frecv, mix)
    w_out_full = wout_g.reshape(D_MODEL, D_MODEL)
    y = _mm_plain("out_proj", mix, w_out_full, NN, F32)
    x1, h2 = _postmix_premlp(xs, y, gate_m, g_post_mix, g_pre_mlp, scale_f, shift_f)
    gl_fsend, gl_frecv, gl_land, gl_ftoken = _gather2_forward("gather_last_forward", gl_land, gl_ici, h2)

    tm, tn, tk = _tile(s, 1024), 1024, 2048
    nsh = 1024 // tn

    def relu2(acc):
        rl = jnp.maximum(acc, 0.0)
        return rl * rl, rl

    z, a_relu = _matmul(
        "mlp_in", h2, wmi_g, contract=NN, grid=(s // tm, D_FF // tn, D_MODEL // tk),
        a_spec=pl.BlockSpec((tm, tk), lambda i, j, k: (i, k)),
        b_spec=pl.BlockSpec((None, tk, tn), lambda i, j, k: (j // nsh, k, j % nsh)),
        out_specs=[pl.BlockSpec((tm, tn), lambda i, j, k: (i, j))] * 2,
        out_shapes=[jax.ShapeDtypeStruct((s, D_FF), BF16)] * 2, acc_shape=(tm, tn), epilogue=relu2,
        deps=(gl_ftoken,))
    (wmo_g,) = _gather2_wait("gather_last_wait", gl_land, gl_send, gl_sib, gl_fsend, gl_frecv, z)
    w_mo_full = wmo_g.reshape(D_FF, D_MODEL)
    y2 = _mm_plain("mlp_out", z, w_mo_full, NN, F32)

    dx2, dy2, loss_vec, dgate_f, dg_post_mlp = _loss_postmlp_bwd(x1, y2, target, gate_f, g_post_mlp)
    loss = lax.psum(loss_vec[0, 0], ("x", "y", "c"))

    da = _mm_plain("mlp_out_dx", dy2, w_mo_full, NT, BF16, extra=(a_relu,),
                   epilogue=lambda acc, rl: (acc * (2.0 * rl.astype(F32)),))
    dw_mo = _mm_plain("mlp_out_dw", z, dy2, TN, BF16)
    dw_mo = dw_mo.reshape(N_DEV, 1024, D_MODEL)
    x_mo = _exchange_start("grad_mlp_out_start", [_own_slot("own_dw_mlp_out", dw_mo, False, rank)], [dw_mo])
    tkx = 2048
    (dh2,) = _matmul(
        "mlp_in_dx", da, wmi_g, contract=NT, grid=(s // tm, D_MODEL // tn, D_FF // tkx),
        a_spec=pl.BlockSpec((tm, tkx), lambda i, j, k: (i, k)),
        b_spec=pl.BlockSpec((tkx // 1024, tn, 1024), lambda i, j, k: (k, j, 0)),
        out_specs=[pl.BlockSpec((tm, tn), lambda i, j, k: (i, j))],
        out_shapes=[jax.ShapeDtypeStruct((s, D_MODEL), F32)], acc_shape=(tm, tn), deps=(x_mo[4],))
    ts = _tile(s, 2048)
    (dw_mi,) = _matmul(
        "mlp_in_dw", h2, da, contract=TN, grid=(D_MODEL // 1024, D_FF // tn, s // ts),
        a_spec=pl.BlockSpec((ts, 1024), lambda i, j, k: (k, i)),
        b_spec=pl.BlockSpec((ts, tn), lambda i, j, k: (k, j)),
        out_specs=[pl.BlockSpec((None, 1024, tn), lambda i, j, k: (j // nsh, i, j % nsh))],
        out_shapes=[jax.ShapeDtypeStruct((N_DEV, D_MODEL, 1024), BF16)], acc_shape=(1024, tn))
    x_mi = _exchange_start("grad_mlp_in_start", [_own_slot("own_dw_mlp_in", dw_mi, False, rank)], [dw_mi])

    dx1, dy, dscale_f, dshift_f, dg_pre_mlp, dgate_m, dg_post_mix = _premlp_postmix_bwd(
        dh2, dx2, x1, y, scale_f, g_pre_mlp, gate_m, g_post_mix, deps=(x_mi[4],))

    dmix = _mm_plain("out_proj_dx", dy, w_out_full, NT, F32)
    dw_out = _mm_plain("out_proj_dw", mix, dy, TN, BF16)
    dw_out = dw_out.reshape(N_DEV, 256, D_MODEL)
    x_out = _exchange_start("grad_out_start", [_own_slot("own_dw_out", dw_out, False, rank)], [dw_out])
    do_fox, do_gla, dgr, dg_fox, dg_gla = _mix_bwd(dmix, o_fox, o_gla, pm, g_fox, g_gla, deps=(x_out[4],))

    dq, dk, dv, dc, dcq = _fox_bwd(pm, crow, o_fox, lse, do_fox)
    dsmall_f, db_f = _fox_cum_bwd(dc.reshape(FOX_HEADS, s), dcq, small, bf_vec)
    dgq, dgk, dgv, dza, db_a2 = _gla_bwd(pm, small, wa_pad, b_gla_a2, states, do_gla)
    dsmall = _mm_plain("gate_dx", dza, wa_pad, NT, F32, tn=128, extra=(dsmall_f,),
                       epilogue=lambda acc, other: (acc + other,))
    dwa_pad = _mm_plain("gate_dw", small, dza, TN, F32, tm=128, tn=512)

    dpm = jnp.concatenate([dq.astype(BF16), dk.astype(BF16), dv.astype(BF16), dgq.astype(BF16), dgk.astype(BF16),
                           dgv.astype(BF16), dgr], axis=1)
    dw_main = _mm_plain("proj_main_dw", dpm, h, TN, BF16)
    dw_small = _mm_plain("proj_small_dw", dsmall, h, TN, BF16, tm=128)
    dwin_full = jnp.concatenate([dw_main[:3072], dw_small[0:8], dw_main[3072:5120], dw_small[8:24],
                                 dw_main[5120:6144]], axis=0)
    dwin_slabs = dwin_full.reshape(N_DEV, 771, D_MODEL)
    x_in = _exchange_start("grad_in_start", [_own_slot("own_dw_in", dwin_slabs, False, rank)], [dwin_slabs])
    dh_small = _mm_plain("proj_small_dx", dsmall, w_small, NN, F32, tk=128)
    dh = _mm_plain("proj_main_dx", dpm, w_main, NN, F32, extra=(dh_small,),
                   epilogue=lambda acc, other: (acc + other,), deps=(x_in[4],))
    grad_x, dscale_m, dshift_m, dg_pre_mix = _premix_bwd(dh, dx1, xs, g_pre_mix, scale_m)

    dmod = jnp.concatenate([dshift_m, dscale_m, dgate_m, dshift_f, dscale_f, dgate_f], axis=1)
    flat = jnp.concatenate(
        [dmod, dg_pre_mix, dg_post_mix, dg_fox, dg_pre_mlp, dg_post_mlp, db_a2,
         dwa_pad[8:24, :].reshape(1, GLA_RANK * 512), dg_gla, _pad_lanes(db_f[:, 0].reshape(1, FOX_HEADS), 128)],
        axis=1)

    (r_mo,) = _exchange_wait("grad_mlp_out_wait", *x_mo[:4], grad_x)
    g_mo, d_mo, nm_mo, nv_mo = _adamw_slabs("adamw_w_mlp_out", w_mlp_out[0], r_mo, m_w_mlp_out[0], v_w_mlp_out[0])
    (r_mi,) = _exchange_wait("grad_mlp_in_wait", *x_mi[:4], g_mo)
    g_mi, d_mi, nm_mi, nv_mi = _adamw_slabs("adamw_w_mlp_in", w_mlp_in[0], r_mi, m_w_mlp_in[0], v_w_mlp_in[0])
    (r_out,) = _exchange_wait("grad_out_wait", *x_out[:4], g_mi)
    g_out, d_out, nm_out, nv_out = _adamw_slabs("adamw_w_out", w_out[0], r_out, m_w_out[0], v_w_out[0])

    (flat_g,) = _all_gather("gather_small_grads", [flat], deps=(g_out,))
    tot = _sum_devices(flat_g)
    dm_cols = lax.dynamic_slice_in_dim(flat_g[:, 0, :6 * D_MODEL], rank * 1536, 1536, axis=1)
    (r_in,) = _exchange_wait("grad_in_wait", *x_in[:4], tot)
    in_t = _adamw_slabs("adamw_w_in", w_in_t, r_in, m_in_t, v_in_t)
    g_in, d_in, nm_in, nv_in = [a.T for a in in_t]
    g_ada, d_ada, nm_ada, nv_ada = _adamw_ada(w_ada[0], c_act.T, dm_cols, m_w_ada[0], v_w_ada[0], deps=(in_t[0],))

    o = 0
    seg = {}
    for name, n in (("b_ada", 12288), ("g_pre_mix", 2048), ("g_post_mix", 2048), ("g_fox_out", 1024),
                    ("g_pre_mlp", 2048), ("g_post_mlp", 2048), ("b_gla_a2", 512), ("w_gla_a2", 8192),
                    ("g_gla_out", 1024), ("b_fgate", 128)):
        seg[name] = tot[:, o:o + n]
        o += n
    g_wa2 = lax.dynamic_slice_in_dim(seg["w_gla_a2"].reshape(GLA_RANK, 512), rank * 64, 64, axis=1)
    g_ggla = lax.dynamic_slice_in_dim(seg["g_gla_out"].reshape(GLA_HEADS, GLA_DV), rank * 32, 32, axis=1)
    small_names = ["b_ada", "g_pre_mix", "g_post_mix", "g_fox_out", "g_pre_mlp", "g_post_mlp", "b_gla_a2",
                   "w_gla_a2", "g_gla_out", "b_fgate"]
    small_grads = {**seg, "w_gla_a2": g_wa2.reshape(1, 1024), "g_gla_out": g_ggla.reshape(1, 128)}
    weights = dict(b_ada=b_ada, g_pre_mix=g_pre_mix, g_post_mix=g_post_mix, g_fox_out=g_fox_out,
                   g_pre_mlp=g_pre_mlp, g_post_mlp=g_post_mlp, b_gla_a2=b_gla_a2, w_gla_a2=w_gla_a2,
                   g_gla_out=g_gla_out, b_fgate=b_fgate)
    moms = dict(b_ada=m_b_ada, g_pre_mix=m_g_pre_mix, g_post_mix=m_g_post_mix, g_fox_out=m_g_fox_out,
                g_pre_mlp=m_g_pre_mlp, g_post_mlp=m_g_post_mlp, b_gla_a2=m_b_gla_a2, w_gla_a2=m_w_gla_a2,
                g_gla_out=m_g_gla_out, b_fgate=m_b_fgate)
    vels = dict(b_ada=v_b_ada, g_pre_mix=v_g_pre_mix, g_post_mix=v_g_post_mix, g_fox_out=v_g_fox_out,
                g_pre_mlp=v_g_pre_mlp, g_post_mlp=v_g_post_mlp, b_gla_a2=v_b_gla_a2, w_gla_a2=v_w_gla_a2,
                g_gla_out=v_g_gla_out, b_fgate=v_b_fgate)

    def flatten(d, fill):
        parts = []
        for nm in small_names:
            p = d[nm].reshape(1, -1)
            if nm == "b_fgate":
                p = jnp.concatenate([p[:, :FOX_HEADS], jnp.full((1, 128 - FOX_HEADS), fill, F32)], axis=1)
            parts.append(p)
        return jnp.concatenate(parts, axis=1).reshape(-1, 128)

    fw, fg, fm, fv = flatten(weights, 0.0), flatten(small_grads, 0.0), flatten(moms, 0.0), flatten(vels, 1.0)
    fd, fnm, fnv = _adamw_flat(fw, fg, fm, fv)

    def unflatten(fl):
        fl = fl.reshape(1, -1)
        out = {}
        o = 0
        for nm in small_names:
            n = 128 if nm == "b_fgate" else weights[nm].size
            piece = fl[:, o:o + n]
            if nm == "b_fgate":
                piece = piece[:, :FOX_HEADS]
            out[nm] = piece.reshape(weights[nm].shape)
            o += n
        return out

    sg, sd, snm, snv = unflatten(fg), unflatten(fd), unflatten(fnm), unflatten(fnv)

    big = dict(w_ada=(g_ada, d_ada, nm_ada, nv_ada), w_in=(g_in, d_in, nm_in, nv_in),
               w_out=(g_out, d_out, nm_out, nv_out), w_mlp_in=(g_mi, d_mi, nm_mi, nv_mi),
               w_mlp_out=(g_mo, d_mo, nm_mo, nv_mo))
    order = ["w_ada", "b_ada", "g_pre_mix", "g_post_mix", "w_in", "b_fgate", "w_gla_a2", "b_gla_a2", "g_fox_out",
             "g_gla_out", "w_out", "g_pre_mlp", "g_post_mlp", "w_mlp_in", "w_mlp_out"]

    def pick(nm, idx):
        if nm in big:
            return big[nm][idx][None]
        return (sg, sd, snm, snv)[idx][nm]

    grads = [pick(nm, 0) for nm in order]
    deltas = [pick(nm, 1) for nm in order]
    new_m = [pick(nm, 2) for nm in order]
    new_v = [pick(nm, 3) for nm in order]
    return (loss, grad_x[None], *grads, *deltas, *new_m, *new_v)
```

```python
import functools

import numpy as np
import jax
import jax.numpy as jnp
from jax import lax
from jax.experimental import pallas as pl
from jax.experimental.pallas import tpu as pltpu

F32 = jnp.float32
BF16 = jnp.bfloat16
MESH = pl.DeviceIdType.MESH
N_DEV = 8

D_MODEL = 2048
FOX_HEADS = 8
FOX_HEAD_DIM = 128
GLA_HEADS = 4
GLA_DK = 128
GLA_DV = 256
GLA_RANK = 16
GLA_TEMP = 16.0
CHUNK = 64
D_FF = 8192
W_MAIN = 6144
W_SMALL = 128
EPS = 1e-6
NEG = float(np.finfo(np.float32).min)

ADAM_LR = 0.001
ADAM_B1 = 0.9
ADAM_B2 = 0.999
ADAM_EPS = 1e-08
ADAM_WD = 0.01
ADAM_STEP = 10

ROW_T = 256
FOX_T = 1024
GLA_R = 512
CUM_T = 256
VMEM_LIMIT = 56 * 1024 * 1024


def _call(body, deps=(), **kw):
    if not deps:
        return pl.pallas_call(body, **kw)
    n_in, n_dep = len(kw["in_specs"]), len(deps)

    def with_deps(*refs):
        return body(*refs[:n_in], *refs[n_in + n_dep:])

    kw["in_specs"] = [*kw["in_specs"], *[pl.BlockSpec(memory_space=pl.ANY)] * n_dep]
    call = pl.pallas_call(with_deps, **kw)
    return lambda *args: call(*args, *deps)


def _params(sem=None):
    return pltpu.CompilerParams(dimension_semantics=sem, vmem_limit_bytes=VMEM_LIMIT)


def _my_pos():
    return lax.axis_index("x"), lax.axis_index("y"), lax.axis_index("c")


def _my_rank():
    x, y, c = _my_pos()
    return 4 * x + 2 * y + c


def _all_gather(name, arrays, deps=()):
    n = len(arrays)

    def body(*refs):
        ins = refs[:n]
        outs = refs[n:2 * n]
        send_sems, recv_sems, local_sems = refs[2 * n:]
        x, y, c = _my_pos()
        me, sibling = (x, y, c), (x, y, 1 - c)
        chips = [(1 - x, y), (x, 1 - y), (1 - x, 1 - y)]

        def slot(a, px, py, pc):
            return outs[a].at[4 * px + 2 * py + pc]

        def copy(a, k, block, to, src=None):
            return pltpu.make_async_remote_copy(
                src_ref=slot(a, *block) if src is None else src, dst_ref=slot(a, *block),
                send_sem=send_sems.at[a, k], recv_sem=recv_sems.at[a, k],
                device_id=to, device_id_type=MESH)

        started = []
        for a in range(n):
            mine = pltpu.make_async_copy(ins[a], slot(a, *me), local_sems.at[a])
            mine.start()
            started.append(mine)
        first = []
        for a in range(n):
            first.append(copy(a, 0, me, sibling, src=ins[a]))
            first += [copy(a, 1 + j, me, (*chip, c), src=ins[a]) for j, chip in enumerate(chips)]
        for cp in first:
            cp.start()
        passed = []
        for j, chip in enumerate(chips):
            for a in range(n):
                copy(a, 1 + j, (*chip, c), me).wait_recv()
                fwd = copy(a, 4 + j, (*chip, c), sibling)
                fwd.start()
                passed.append(fwd)
        for a in range(n):
            copy(a, 0, sibling, me).wait_recv()
            for j, chip in enumerate(chips):
                copy(a, 4 + j, (*chip, 1 - c), me).wait_recv()
        for cp in first + passed:
            cp.wait_send()
        for mine in started:
            mine.wait()

    hbm = pl.BlockSpec(memory_space=pltpu.HBM)
    return _call(
        body, deps=deps, name=name,
        out_shape=[jax.ShapeDtypeStruct((N_DEV,) + a.shape, a.dtype) for a in arrays],
        in_specs=[hbm] * n, out_specs=[hbm] * n,
        scratch_shapes=[pltpu.SemaphoreType.DMA((n, 7)), pltpu.SemaphoreType.DMA((n, 7)),
                        pltpu.SemaphoreType.DMA((n,))],
    )(*arrays)


def _plane_tiles(rr, cc, tr=512, tc=512):
    if rr % 8 == 0:
        tr = _tile(rr, tr)
        return (rr // tr, pl.BlockSpec((tr, cc), lambda i: (i, 0)),
                pl.BlockSpec((N_DEV, tr, cc), lambda i: (0, i, 0)))
    tc = _tile(cc, tc)
    return (cc // tc, pl.BlockSpec((rr, tc), lambda i: (0, i)),
            pl.BlockSpec((N_DEV, rr, tc), lambda i: (0, 0, i)))


def _own_slot(name, src, gather, rank):
    shape = ((N_DEV,) + src.shape) if gather else src.shape
    rr, cc = shape[1], shape[2]
    by_rows = rr % 8 == 0
    tr, tc = (_tile(rr, 512), cc) if by_rows else (rr, _tile(cc, 512))
    steps = rr // tr if by_rows else cc // tc

    def body(rank_ref, s_ref, o_ref):
        o_ref[...] = s_ref[...].astype(o_ref.dtype)

    def at(i):
        return (i, 0) if by_rows else (0, i)

    if gather:
        in_spec = pl.BlockSpec((tr, tc), lambda i, rk: at(i))
    else:
        in_spec = pl.BlockSpec((None, tr, tc), lambda i, rk: (rk[0], *at(i)))
    grid_spec = pltpu.PrefetchScalarGridSpec(
        num_scalar_prefetch=1, grid=(steps,), in_specs=[in_spec],
        out_specs=pl.BlockSpec((None, tr, tc), lambda i, rk: (rk[0], *at(i))))
    return _call(body, name=name, grid_spec=grid_spec, out_shape=jax.ShapeDtypeStruct(shape, BF16),
                 compiler_params=_params(("arbitrary",)))(jnp.reshape(rank, (1,)).astype(jnp.int32), src)


_HBM = pl.BlockSpec(memory_space=pltpu.HBM)
_SEM = pl.BlockSpec(memory_space=pltpu.SEMAPHORE)
_FLIPS = [(kx, ky, kc) for kx in (0, 1) for ky in (0, 1) for kc in (0, 1)][1:]


def _peers():
    x, y, c = _my_pos()
    out = []
    for kx, ky, kc in _FLIPS:
        px, py, pc = (1 - x if kx else x), (1 - y if ky else y), (1 - c if kc else c)
        out.append(((px, py, pc), 4 * px + 2 * py + pc))
    return out


def _exchange_copy(srcs, lands, send_sems, recv_sems, a, k, peer, peer_rank, slot):
    return pltpu.make_async_remote_copy(
        src_ref=lands[a].at[slot] if srcs is None else srcs[a].at[peer_rank],
        dst_ref=lands[a].at[slot],
        send_sem=send_sems[a].at[k], recv_sem=recv_sems[a].at[k],
        device_id=peer, device_id_type=MESH)


def _exchange_start(name, lands, srcs=None, after=()):
    n = len(lands)
    n_src = 0 if srcs is None else n
    n_in = n + n_src + len(after)

    def body(*refs):
        lnd = refs[:n]
        src = None if srcs is None else refs[n:2 * n]
        send_sems, recv_sems = refs[n_in:n_in + n], refs[n_in + n:n_in + 2 * n]
        token = refs[-1]
        me = _my_rank()
        for a in range(n):
            for k, (peer, peer_rank) in enumerate(_peers()):
                _exchange_copy(src, lnd, send_sems, recv_sems, a, k, peer, peer_rank, me).start()
        token[...] = jnp.zeros_like(token)

    sems = [pltpu.SemaphoreType.DMA((7,))] * (2 * n)
    thru = list(lands) + ([] if srcs is None else list(srcs))
    outs = pl.pallas_call(
        body, name=name,
        out_shape=(*sems, *[pltpu.HBM(t.shape, t.dtype) for t in thru], jax.ShapeDtypeStruct((8, 128), F32)),
        in_specs=[*[_HBM] * len(thru), *[pl.BlockSpec(memory_space=pl.ANY)] * len(after)],
        out_specs=(*[_SEM] * (2 * n), *[_HBM] * len(thru), pl.BlockSpec(memory_space=pltpu.VMEM)),
        input_output_aliases={i: 2 * n + i for i in range(len(thru))},
        compiler_params=pltpu.CompilerParams(has_side_effects=pltpu.SideEffectType.DATAFLOW_SIDE_EFFECTING),
    )(*[pltpu.with_memory_space_constraint(t, pltpu.HBM) for t in thru], *after)
    lands_thru = outs[2 * n:3 * n]
    srcs_thru = None if srcs is None else outs[3 * n:4 * n]
    return outs[:n], outs[n:2 * n], srcs_thru, lands_thru, outs[-1]


def _exchange_wait(name, send_sems, recv_sems, srcs, lands, after):
    n = len(lands)
    thru = list(lands) + ([] if srcs is None else list(srcs))

    def body(*refs):
        lnd = refs[:n]
        src = None if srcs is None else refs[n:2 * n]
        ssem, rsem = refs[len(thru):len(thru) + n], refs[len(thru) + n:len(thru) + 2 * n]
        for a in range(n):
            for k, (peer, peer_rank) in enumerate(_peers()):
                cp = _exchange_copy(src, lnd, ssem, rsem, a, k, peer, peer_rank, peer_rank)
                cp.wait_send()
                cp.wait_recv()

    outs = pl.pallas_call(
        body, name=name,
        out_shape=tuple(pltpu.HBM(t.shape, t.dtype) for t in thru),
        in_specs=[*[_HBM] * len(thru), *[_SEM] * (2 * n), pl.BlockSpec(memory_space=pl.ANY)],
        out_specs=tuple([_HBM] * len(thru)),
        input_output_aliases={i: i for i in range(len(thru))},
        compiler_params=pltpu.CompilerParams(has_side_effects=pltpu.SideEffectType.DATAFLOW_SIDE_EFFECTING),
    )(*thru, *send_sems, *recv_sems, after)
    return outs[:n]


_SIDE = pltpu.CompilerParams(has_side_effects=pltpu.SideEffectType.DATAFLOW_SIDE_EFFECTING)
_ANY = pl.BlockSpec(memory_space=pl.ANY)


def _chips():
    x, y, _ = _my_pos()
    return [(1 - x, y), (x, 1 - y), (1 - x, 1 - y)]


def _slot_copy(lnd, slot, send_sem, recv_sem, to):
    return pltpu.make_async_remote_copy(src_ref=lnd.at[slot], dst_ref=lnd.at[slot], send_sem=send_sem,
                                        recv_sem=recv_sem, device_id=to, device_id_type=MESH)


def _gather2_start(name, lands, after=()):
    n = len(lands)
    n_in = n + len(after)

    def body(*refs):
        lnd = refs[:n]
        send, recv_sib, recv_ici = refs[n_in:n_in + n], refs[n_in + n:n_in + 2 * n], refs[n_in + 2 * n:n_in + 3 * n]
        x, y, c = _my_pos()
        me = 4 * x + 2 * y + c
        for a in range(n):
            _slot_copy(lnd[a], me, send[a].at[0], recv_sib[a].at[0], (x, y, 1 - c)).start()
            for j, chip in enumerate(_chips()):
                _slot_copy(lnd[a], me, send[a].at[1 + j], recv_ici[a].at[j], (*chip, c)).start()
        refs[-1][...] = jnp.zeros_like(refs[-1])

    sems = [pltpu.SemaphoreType.DMA((4,))] * n + [pltpu.SemaphoreType.DMA((1,))] * n + [pltpu.SemaphoreType.DMA((3,))] * n
    outs = pl.pallas_call(
        body, name=name,
        out_shape=(*sems, *[pltpu.HBM(t.shape, t.dtype) for t in lands], jax.ShapeDtypeStruct((8, 128), F32)),
        in_specs=[*[_HBM] * n, *[_ANY] * len(after)],
        out_specs=(*[_SEM] * (3 * n), *[_HBM] * n, pl.BlockSpec(memory_space=pltpu.VMEM)),
        input_output_aliases={i: 3 * n + i for i in range(n)}, compiler_params=_SIDE,
    )(*[pltpu.with_memory_space_constraint(t, pltpu.HBM) for t in lands], *after)
    return outs[:n], outs[n:2 * n], outs[2 * n:3 * n], outs[3 * n:4 * n], outs[-1]


def _gather2_forward(name, lands, recv_ici, after):
    n = len(lands)

    def body(*refs):
        lnd, arrived = refs[:n], refs[n:2 * n]
        send, recv = refs[2 * n + 1:3 * n + 1], refs[3 * n + 1:4 * n + 1]
        x, y, c = _my_pos()
        for j, (cx, cy) in enumerate(_chips()):
            slot = 4 * cx + 2 * cy + c
            for a in range(n):
                _slot_copy(lnd[a], slot, send[a].at[j], arrived[a].at[j], (cx, cy, c)).wait_recv()
                _slot_copy(lnd[a], slot, send[a].at[j], recv[a].at[j], (x, y, 1 - c)).start()
        refs[-1][...] = jnp.zeros_like(refs[-1])

    sems = [pltpu.SemaphoreType.DMA((3,))] * (2 * n)
    outs = pl.pallas_call(
        body, name=name,
        out_shape=(*sems, *[pltpu.HBM(t.shape, t.dtype) for t in lands], jax.ShapeDtypeStruct((8, 128), F32)),
        in_specs=[*[_HBM] * n, *[_SEM] * n, _ANY],
        out_specs=(*[_SEM] * (2 * n), *[_HBM] * n, pl.BlockSpec(memory_space=pltpu.VMEM)),
        input_output_aliases={i: 2 * n + i for i in range(n)}, compiler_params=_SIDE,
    )(*lands, *recv_ici, after)
    return outs[:n], outs[n:2 * n], outs[2 * n:3 * n], outs[-1]


def _gather2_wait(name, lands, send_a, recv_sib, send_b, recv_b, after):
    n = len(lands)

    def body(*refs):
        lnd = refs[:n]
        sa, rs, sb, rb = (refs[(1 + i) * n:(2 + i) * n] for i in range(4))
        x, y, c = _my_pos()
        me = 4 * x + 2 * y + c
        for a in range(n):
            for k in range(4):
                _slot_copy(lnd[a], me, sa[a].at[k], rs[a].at[0], (x, y, 1 - c)).wait_send()
            _slot_copy(lnd[a], me - c + (1 - c), sa[a].at[0], rs[a].at[0], (x, y, 1 - c)).wait_recv()
            for j, (cx, cy) in enumerate(_chips()):
                _slot_copy(lnd[a], 4 * cx + 2 * cy + c, sb[a].at[j], rb[a].at[j], (x, y, 1 - c)).wait_send()
                _slot_copy(lnd[a], 4 * cx + 2 * cy + (1 - c), sb[a].at[j], rb[a].at[j], (x, y, 1 - c)).wait_recv()

    outs = pl.pallas_call(
        body, name=name,
        out_shape=tuple(pltpu.HBM(t.shape, t.dtype) for t in lands),
        in_specs=[*[_HBM] * n, *[_SEM] * (4 * n), _ANY],
        out_specs=tuple([_HBM] * n),
        input_output_aliases={i: i for i in range(n)}, compiler_params=_SIDE,
    )(*lands, *send_a, *recv_sib, *send_b, *recv_b, after)
    return outs


NN = ((1,), (0,))
NT = ((1,), (1,))
TN = ((0,), (0,))


def _matmul(name, a, b, *, contract, grid, a_spec, b_spec, out_specs, out_shapes, acc_shape,
            extra=(), extra_specs=(), epilogue=None, deps=()):
    nk = grid[2]
    n_extra = len(extra)
    n_out = len(out_shapes)

    def body(*refs):
        a_ref, b_ref = refs[0], refs[1]
        extra_refs = refs[2:2 + n_extra]
        out_refs = refs[2 + n_extra:2 + n_extra + n_out]
        acc_ref = refs[-1]
        k = pl.program_id(2)

        def prod():
            if len(b_ref.shape) == 2:
                return lax.dot_general(a_ref[...].astype(BF16), b_ref[...].astype(BF16), (contract, ((), ())),
                                       preferred_element_type=F32)
            kk = a_ref.shape[1] // b_ref.shape[0]
            acc = None
            for u in range(b_ref.shape[0]):
                part = lax.dot_general(a_ref[:, u * kk:(u + 1) * kk].astype(BF16), b_ref[u].astype(BF16),
                                       (contract, ((), ())), preferred_element_type=F32)
                acc = part if acc is None else acc + part
            return acc

        def finish(acc):
            res = (acc,) if epilogue is None else epilogue(acc, *[r[...] for r in extra_refs])
            for o_ref, val in zip(out_refs, res):
                o_ref[...] = val.astype(o_ref.dtype)

        if nk == 1:
            finish(prod())
            return

        @pl.when(k == 0)
        def _():
            acc_ref[...] = prod()

        @pl.when((k > 0) & (k < nk - 1))
        def _():
            acc_ref[...] += prod()

        @pl.when(k == nk - 1)
        def _():
            finish(acc_ref[...] + prod())

    outs = _call(
        body, deps=deps, name=name, grid=grid,
        in_specs=[a_spec, b_spec, *extra_specs], out_specs=list(out_specs), out_shape=list(out_shapes),
        scratch_shapes=[pltpu.VMEM(acc_shape if nk > 1 else (8, 128), F32)],
        compiler_params=_params(("parallel", "parallel", "arbitrary")),
    )(a, b, *extra)
    return outs


def _tile(n, t):
    t = min(n, t)
    assert n % t == 0, (n, t)
    return t


def _mm_plain(name, a, b, contract, out_dtype, tm=1024, tn=1024, tk=2048, extra=(), epilogue=None,
              n_out=1, out_dtypes=None, deps=()):
    if contract == NN:
        (m, kd), (_, n) = a.shape, b.shape
    elif contract == NT:
        (m, kd), (n, _) = a.shape, b.shape
    else:
        (kd, m), (_, n) = a.shape, b.shape
    tm, tn, tk = _tile(m, tm), _tile(n, tn), _tile(kd, tk)
    if contract == NN:
        a_spec = pl.BlockSpec((tm, tk), lambda i, j, k: (i, k))
        b_spec = pl.BlockSpec((tk, tn), lambda i, j, k: (k, j))
    elif contract == NT:
        a_spec = pl.BlockSpec((tm, tk), lambda i, j, k: (i, k))
        b_spec = pl.BlockSpec((tn, tk), lambda i, j, k: (j, k))
    else:
        a_spec = pl.BlockSpec((tk, tm), lambda i, j, k: (k, i))
        b_spec = pl.BlockSpec((tk, tn), lambda i, j, k: (k, j))
    o_spec = pl.BlockSpec((tm, tn), lambda i, j, k: (i, j))
    out_dtypes = out_dtypes or [out_dtype] * n_out
    outs = _matmul(
        name, a, b, contract=contract, grid=(m // tm, n // tn, kd // tk), a_spec=a_spec, b_spec=b_spec,
        out_specs=[o_spec] * len(out_dtypes), out_shapes=[jax.ShapeDtypeStruct((m, n), dt) for dt in out_dtypes],
        acc_shape=(tm, tn), extra=extra, extra_specs=[o_spec] * len(extra), epilogue=epilogue, deps=deps)
    return outs[0] if len(out_dtypes) == 1 else outs


def _rows_call(name, body, row_in, vec_in, row_out, vec_out, s, deps=()):
    t = _tile(s, ROW_T)
    in_specs = []
    args = []
    for arr, width, cb in row_in:
        in_specs.append(pl.BlockSpec((t, width), functools.partial(lambda i, cb: (i, cb), cb=cb)))
        args.append(arr)
    for v in vec_in:
        in_specs.append(pl.BlockSpec(v.shape, lambda i: (0, 0)))
        args.append(v)
    out_specs = []
    out_shapes = []
    for width, dt in row_out:
        out_specs.append(pl.BlockSpec((t, width), lambda i: (i, 0)))
        out_shapes.append(jax.ShapeDtypeStruct((s, width), dt))
    for width in vec_out:
        out_specs.append(pl.BlockSpec((1, width), lambda i: (0, 0)))
        out_shapes.append(jax.ShapeDtypeStruct((1, width), F32))
    return _call(body, deps=deps, name=name, grid=(s // t,), in_specs=in_specs, out_specs=out_specs,
                 out_shape=out_shapes, compiler_params=_params(("arbitrary",)))(*args)


def _acc_vec(ref, val):
    _acc_row(ref, jnp.sum(val, axis=0, keepdims=True))


def _acc_row(ref, part):
    @pl.when(pl.program_id(0) == 0)
    def _():
        ref[...] = part

    @pl.when(pl.program_id(0) > 0)
    def _():
        ref[...] += part


def _rms(v):
    return lax.rsqrt(jnp.mean(v * v, axis=-1, keepdims=True) + EPS)


def _norm_bwd(dxn, xn, r):
    return r * (dxn - xn * jnp.mean(dxn * xn, axis=-1, keepdims=True))


def _premix(x, g, scale, shift, deps=()):
    s = x.shape[0]

    def body(x_ref, g_ref, sc_ref, sh_ref, h_ref):
        xv = x_ref[...]
        h_ref[...] = ((xv * _rms(xv) * g_ref[...]) * (1.0 + sc_ref[...]) + sh_ref[...]).astype(BF16)

    return _rows_call("premix", body, [(x, D_MODEL, 0)], [g, scale, shift], [(D_MODEL, BF16)], [], s, deps)[0]


def _sigmoid(z):
    return 1.0 / (1.0 + jnp.exp(-z))


def _mix_fwd(o_fox, o_gla, pm, g_fox, g_gla):
    s = o_fox.shape[0]

    def body(of_ref, og_ref, gr_ref, gf_ref, gg_ref, mix_ref):
        for h in range(FOX_HEADS):
            sl = slice(h * FOX_HEAD_DIM, (h + 1) * FOX_HEAD_DIM)
            seg = of_ref[:, sl]
            mix_ref[:, sl] = (seg * _rms(seg) * gf_ref[:, sl]).astype(BF16)
        for h in range(GLA_HEADS):
            sl = slice(h * GLA_DV, (h + 1) * GLA_DV)
            seg = og_ref[:, sl]
            gr = gr_ref[:, sl].astype(F32)
            val = (seg * _rms(seg) * gg_ref[:, sl]) * (gr * _sigmoid(gr))
            mix_ref[:, pl.ds(FOX_HEADS * FOX_HEAD_DIM + h * GLA_DV, GLA_DV)] = val.astype(BF16)

    return _rows_call("mix_fwd", body, [(o_fox, 1024, 0), (o_gla, 1024, 0), (pm, 1024, 5)], [g_fox, g_gla],
                      [(D_MODEL, BF16)], [], s)[0]


def _mix_bwd(dmix, o_fox, o_gla, pm, g_fox, g_gla, deps=()):
    s = o_fox.shape[0]

    def body(dm_ref, of_ref, og_ref, gr_ref, gf_ref, gg_ref, dof_ref, dog_ref, dgr_ref, dgf_ref, dgg_ref):
        dgf = []
        for h in range(FOX_HEADS):
            sl = slice(h * FOX_HEAD_DIM, (h + 1) * FOX_HEAD_DIM)
            seg = of_ref[:, sl]
            r = _rms(seg)
            segn = seg * r
            dout = dm_ref[:, sl].astype(F32)
            dgf.append(jnp.sum(dout * segn, axis=0, keepdims=True))
            dof_ref[:, sl] = _norm_bwd(dout * gf_ref[:, sl], segn, r).astype(BF16)
        dgg = []
        for h in range(GLA_HEADS):
            sl = slice(h * GLA_DV, (h + 1) * GLA_DV)
            seg = og_ref[:, sl]
            r = _rms(seg)
            segn = seg * r
            gl = segn * gg_ref[:, sl]
            gr = gr_ref[:, sl].astype(F32)
            sig = _sigmoid(gr)
            dout = dm_ref[:, pl.ds(FOX_HEADS * FOX_HEAD_DIM + h * GLA_DV, GLA_DV)].astype(F32)
            dgr_ref[:, sl] = (dout * gl * (sig * (1.0 + gr * (1.0 - sig)))).astype(BF16)
            dgl = dout * (gr * sig)
            dgg.append(jnp.sum(dgl * segn, axis=0, keepdims=True))
            dog_ref[:, sl] = _norm_bwd(dgl * gg_ref[:, sl], segn, r).astype(BF16)
        _acc_row(dgf_ref, jnp.concatenate(dgf, axis=1))
        _acc_row(dgg_ref, jnp.concatenate(dgg, axis=1))

    return _rows_call("mix_bwd", body, [(dmix, D_MODEL, 0), (o_fox, 1024, 0), (o_gla, 1024, 0), (pm, 1024, 5)],
                      [g_fox, g_gla], [(1024, BF16), (1024, BF16), (1024, BF16)], [1024, 1024], s, deps)


def _postmix_premlp(x, y, gate_m, g_post_mix, g_pre_mlp, scale_f, shift_f):
    s = x.shape[0]

    def body(x_ref, y_ref, gm_ref, gpm_ref, gpl_ref, sc_ref, sh_ref, x1_ref, h2_ref):
        yv = y_ref[...].astype(F32)
        x1 = x_ref[...] + gm_ref[...] * (yv * _rms(yv) * gpm_ref[...])
        x1_ref[...] = x1
        h2_ref[...] = ((x1 * _rms(x1) * gpl_ref[...]) * (1.0 + sc_ref[...]) + sh_ref[...]).astype(BF16)

    return _rows_call("postmix_premlp", body, [(x, D_MODEL, 0), (y, D_MODEL, 0)],
                      [gate_m, g_post_mix, g_pre_mlp, scale_f, shift_f], [(D_MODEL, F32), (D_MODEL, BF16)], [], s)


def _loss_postmlp_bwd(x1, y2, target, gate_f, g_post_mlp):
    s = x1.shape[0]

    def body(x1_ref, y2_ref, t_ref, gf_ref, g_ref, dx2_ref, dy2_ref, loss_ref, dgate_ref, dg_ref):
        yv = y2_ref[...].astype(F32)
        r = _rms(yv)
        yn = yv * r
        o = yn * g_ref[...]
        e = (x1_ref[...] + gf_ref[...] * o) - t_ref[...]
        part = 0.5 * jnp.sum(jnp.mean(e * e, axis=-1, keepdims=True), axis=0, keepdims=True)
        _acc_vec(loss_ref, jnp.broadcast_to(part, (1, 128)))
        dx2 = e * (1.0 / D_MODEL)
        dx2_ref[...] = dx2
        _acc_vec(dgate_ref, dx2 * o)
        do = dx2 * gf_ref[...]
        _acc_vec(dg_ref, do * yn)
        dy2_ref[...] = _norm_bwd(do * g_ref[...], yn, r).astype(BF16)

    return _rows_call("loss_postmlp_bwd", body, [(x1, D_MODEL, 0), (y2, D_MODEL, 0), (target, D_MODEL, 0)],
                      [gate_f, g_post_mlp], [(D_MODEL, F32), (D_MODEL, BF16)], [128, D_MODEL, D_MODEL], s)


def _premlp_postmix_bwd(dh2, dx2, x1, y, scale_f, g_pre_mlp, gate_m, g_post_mix, deps=()):
    s = x1.shape[0]

    def body(dh2_ref, dx2_ref, x1_ref, y_ref, sc_ref, gpl_ref, gm_ref, gpm_ref,
             dx1_ref, dy_ref, dsc_ref, dsh_ref, dgpl_ref, dgm_ref, dgpm_ref):
        x1 = x1_ref[...]
        r1 = _rms(x1)
        x1n = x1 * r1
        dh2 = dh2_ref[...].astype(F32)
        _acc_vec(dsc_ref, dh2 * (x1n * gpl_ref[...]))
        _acc_vec(dsh_ref, dh2)
        dn2 = dh2 * (1.0 + sc_ref[...])
        _acc_vec(dgpl_ref, dn2 * x1n)
        dx1 = dx2_ref[...] + _norm_bwd(dn2 * gpl_ref[...], x1n, r1)
        dx1_ref[...] = dx1
        yv = y_ref[...].astype(F32)
        ry = _rms(yv)
        yn = yv * ry
        _acc_vec(dgm_ref, dx1 * (yn * gpm_ref[...]))
        do = dx1 * gm_ref[...]
        _acc_vec(dgpm_ref, do * yn)
        dy_ref[...] = _norm_bwd(do * gpm_ref[...], yn, ry).astype(BF16)

    return _rows_call("premlp_postmix_bwd", body,
                      [(dh2, D_MODEL, 0), (dx2, D_MODEL, 0), (x1, D_MODEL, 0), (y, D_MODEL, 0)],
                      [scale_f, g_pre_mlp, gate_m, g_post_mix], [(D_MODEL, F32), (D_MODEL, BF16)],
                      [D_MODEL] * 5, s, deps)


def _premix_bwd(dh, dx1, x, g_pre_mix, scale_m):
    s = x.shape[0]

    def body(dh_ref, dx1_ref, x_ref, g_ref, sc_ref, gx_ref, dsc_ref, dsh_ref, dg_ref):
        xv = x_ref[...]
        r = _rms(xv)
        xn = xv * r
        dh = dh_ref[...].astype(F32)
        _acc_vec(dsc_ref, dh * (xn * g_ref[...]))
        _acc_vec(dsh_ref, dh)
        dn1 = dh * (1.0 + sc_ref[...])
        _acc_vec(dg_ref, dn1 * xn)
        gx_ref[...] = dx1_ref[...] + _norm_bwd(dn1 * g_ref[...], xn, r)

    return _rows_call("premix_bwd", body, [(dh, D_MODEL, 0), (dx1, D_MODEL, 0), (x, D_MODEL, 0)],
                      [g_pre_mix, scale_m], [(D_MODEL, F32)], [D_MODEL] * 3, s)


def _split3(v):
    hi = v.astype(BF16)
    r1 = v - hi.astype(F32)
    mid = r1.astype(BF16)
    lo = (r1 - mid.astype(F32)).astype(BF16)
    return hi, mid, lo


def _dot_exact01(v, tri, contract=NN, tri_first=False):
    acc = None
    for part in _split3(v):
        lhs, rhs = (tri, part) if tri_first else (part, tri)
        p = lax.dot_general(lhs, rhs, (contract, ((), ())), preferred_element_type=F32)
        acc = p if acc is None else acc + p
    return acc


def _log_sigmoid(z):
    return jnp.minimum(z, 0.0) - jnp.log(1.0 + jnp.exp(-jnp.abs(z)))


def _fox_cum(small, bvec):
    s = small.shape[0]
    t = _tile(s, CUM_T)

    def body(sm_ref, b_ref, out_ref, carry):
        @pl.when(pl.program_id(0) == 0)
        def _():
            carry[...] = jnp.zeros_like(carry)

        lf = _log_sigmoid(sm_ref[...] + b_ref[...])
        lft = lf.T[0:FOX_HEADS, :]
        row = lax.broadcasted_iota(jnp.int32, (t, t), 0)
        col = lax.broadcasted_iota(jnp.int32, (t, t), 1)
        upper = (row <= col).astype(BF16)
        cum = _dot_exact01(lft, upper) + carry[:, 0:1]
        out_ref[...] = cum
        carry[...] = carry[...] + jnp.sum(lft, axis=1, keepdims=True)

    return _call(body, name="fox_cum", grid=(s // t,),
                 in_specs=[pl.BlockSpec((t, W_SMALL), lambda i: (i, 0)), pl.BlockSpec((1, W_SMALL), lambda i: (0, 0))],
                 out_specs=pl.BlockSpec((FOX_HEADS, t), lambda i: (0, i)),
                 out_shape=jax.ShapeDtypeStruct((FOX_HEADS, s), F32),
                 scratch_shapes=[pltpu.VMEM((FOX_HEADS, 128), F32)],
                 compiler_params=_params(("arbitrary",)))(small, bvec)


def _fox_cum_bwd(dc, dcq, small, bvec):
    s = small.shape[0]
    t = _tile(s, CUM_T)
    nb = s // t

    def body(dc_ref, dcq_ref, sm_ref, b_ref, out_ref, db_ref, carry):
        @pl.when(pl.program_id(0) == 0)
        def _():
            carry[...] = jnp.zeros_like(carry)
            db_ref[...] = jnp.zeros_like(db_ref)

        lane = lax.broadcasted_iota(jnp.int32, (t, W_SMALL), 1)
        dcq = jnp.zeros((t, W_SMALL), F32)
        for hh in range(FOX_HEADS):
            dcq = jnp.where(lane == hh, dcq_ref[hh], dcq)
        dcv = dc_ref[...] + dcq.T[0:FOX_HEADS, :]
        row = lax.broadcasted_iota(jnp.int32, (t, t), 0)
        col = lax.broadcasted_iota(jnp.int32, (t, t), 1)
        lower = (row >= col).astype(BF16)
        dlf = _dot_exact01(dcv, lower) + carry[:, 0:1]
        carry[...] = carry[...] + jnp.sum(dcv, axis=1, keepdims=True)
        z = sm_ref[...] + b_ref[...]
        zt = z.T[0:FOX_HEADS, :]
        dff = dlf * _sigmoid(-zt)
        db_ref[...] = db_ref[...] + jnp.sum(dff, axis=1, keepdims=True)
        full = jnp.concatenate([dff, jnp.zeros((W_SMALL - FOX_HEADS, t), F32)], axis=0)
        out_ref[...] = full.T

    return _call(body, name="fox_cum_bwd", grid=(nb,),
                 in_specs=[pl.BlockSpec((FOX_HEADS, t), lambda i: (0, nb - 1 - i)),
                           pl.BlockSpec((FOX_HEADS, t, 1), lambda i: (0, nb - 1 - i, 0)),
                           pl.BlockSpec((t, W_SMALL), lambda i: (nb - 1 - i, 0)),
                           pl.BlockSpec((1, W_SMALL), lambda i: (0, 0))],
                 out_specs=[pl.BlockSpec((t, W_SMALL), lambda i: (nb - 1 - i, 0)),
                            pl.BlockSpec((FOX_HEADS, 128), lambda i: (0, 0))],
                 out_shape=[jax.ShapeDtypeStruct((s, W_SMALL), F32), jax.ShapeDtypeStruct((FOX_HEADS, 128), F32)],
                 scratch_shapes=[pltpu.VMEM((FOX_HEADS, 128), F32)],
                 compiler_params=_params(("arbitrary",)))(dc, dcq, small, bvec)


FOX_SCALE = FOX_HEAD_DIM ** -0.5


def _fox_fwd(pm, crow):
    s = pm.shape[0]
    t = _tile(s, FOX_T)
    nb = s // t
    parts = 2
    hq = t // parts

    def body(q_ref, k_ref, v_ref, c_ref, o_ref, lse_ref):
        i = pl.program_id(1)
        qs = [q_ref[g * hq:(g + 1) * hq, :] for g in range(parts)]

        def block(j, carry, diagonal):
            rows = pl.ds(pl.multiple_of(j * t, t), t)
            k_all, v_all, c_all = k_ref[rows, :], v_ref[rows, :], c_ref[j]
            out = []
            for g, (m_prev, l_prev, acc) in enumerate(carry):
                nk = (g + 1) * hq if diagonal else t
                kb, vb, cb = k_all[:nk], v_all[:nk], c_all[:, :nk]
                sc = lax.dot_general(qs[g], kb, (NT, ((), ())), preferred_element_type=F32)
                sc = sc * FOX_SCALE - cb
                if diagonal:
                    row = lax.broadcasted_iota(jnp.int32, (hq, nk), 0) + g * hq
                    col = lax.broadcasted_iota(jnp.int32, (hq, nk), 1)
                    sc = jnp.where(row >= col, sc, NEG)
                m_new = jnp.maximum(m_prev, jnp.max(sc, axis=1, keepdims=True))
                alpha = jnp.exp(m_prev - m_new)
                p = jnp.exp(sc - m_new)
                l_new = alpha * l_prev + jnp.sum(p, axis=1, keepdims=True)
                pv = jnp.dot(p.astype(BF16), vb, preferred_element_type=F32)
                out.append((m_new, l_new, alpha * acc + pv))
            return tuple(out)

        init = tuple((jnp.full((hq, 1), NEG, F32), jnp.zeros((hq, 1), F32), jnp.zeros((hq, 128), F32))
                     for _ in range(parts))
        carry = lax.fori_loop(0, i, lambda j, cr: block(j, cr, False), init)
        carry = block(i, carry, True)
        for g, (m_fin, l_fin, acc) in enumerate(carry):
            o_ref[g * hq:(g + 1) * hq, :] = acc / l_fin
            lse_ref[g * hq:(g + 1) * hq, :] = m_fin + jnp.log(l_fin)

    return _call(
        body, name="fox_fwd", grid=(FOX_HEADS, nb),
        in_specs=[pl.BlockSpec((t, 128), lambda h, i: (i, h)),
                  pl.BlockSpec((s, 128), lambda h, i: (0, FOX_HEADS + h)),
                  pl.BlockSpec((s, 128), lambda h, i: (0, 2 * FOX_HEADS + h)),
                  pl.BlockSpec((None, nb, 1, t), lambda h, i: (h, 0, 0, 0))],
        out_specs=[pl.BlockSpec((t, 128), lambda h, i: (i, h)),
                   pl.BlockSpec((None, t, 1), lambda h, i: (h, i, 0))],
        out_shape=[jax.ShapeDtypeStruct((s, FOX_HEADS * 128), F32), jax.ShapeDtypeStruct((FOX_HEADS, s, 1), F32)],
        compiler_params=_params(("parallel", "arbitrary")),
    )(pm, pm, pm, crow.reshape(FOX_HEADS, nb, 1, t))


def _fox_bwd(pm, crow, o, lse, do):
    s = pm.shape[0]
    t = _tile(s, FOX_T)
    nb = s // t

    parts = 2
    hq = t // parts

    def body(q_ref, do_ref, o_ref, lse_ref, k_ref, v_ref, c_ref, dq_ref, dk_ref, dv_ref, dc_ref, dcq_ref, delta_s):
        j = pl.program_id(1)

        @pl.when(j == 0)
        def _():
            dq_ref[...] = jnp.zeros_like(dq_ref)
            dcq_ref[...] = jnp.zeros_like(dcq_ref)
            delta_s[...] = jnp.sum(do_ref[...].astype(F32) * o_ref[...], axis=1, keepdims=True)

        k_all, v_all, c_all = k_ref[...], v_ref[...], c_ref[...]

        def grow(acc, part, axis):
            n = part.shape[axis]
            if n == acc.shape[axis]:
                return acc + part
            if axis == 0:
                return jnp.concatenate([acc[:n] + part, acc[n:]], axis=0)
            return jnp.concatenate([acc[:, :n] + part, acc[:, n:]], axis=1)

        def block(i, carry, diagonal):
            dk_acc, dv_acc, dc_acc = carry
            for g in range(parts):
                nk = (g + 1) * hq if diagonal else t
                kb, vb, cb = k_all[:nk], v_all[:nk], c_all[:, :nk]
                rows = pl.ds(pl.multiple_of(i * t + g * hq, hq), hq)
                q, dov = q_ref[rows, :], do_ref[rows, :]
                sc = lax.dot_general(q, kb, (NT, ((), ())), preferred_element_type=F32)
                p = jnp.exp(sc * FOX_SCALE - cb - lse_ref[rows, :])
                if diagonal:
                    row = lax.broadcasted_iota(jnp.int32, (hq, nk), 0) + g * hq
                    col = lax.broadcasted_iota(jnp.int32, (hq, nk), 1)
                    p = jnp.where(row >= col, p, 0.0)
                dp = lax.dot_general(dov, vb, (NT, ((), ())), preferred_element_type=F32)
                ds = p * (dp - delta_s[rows, :])
                dsb = ds.astype(BF16)
                dv_acc = grow(dv_acc, lax.dot_general(p.astype(BF16), dov, (TN, ((), ())),
                                                      preferred_element_type=F32), 0)
                dk_acc = grow(dk_acc, lax.dot_general(dsb, q, (TN, ((), ())), preferred_element_type=F32), 0)
                dq_ref[rows, :] += jnp.dot(dsb, kb, preferred_element_type=F32) * FOX_SCALE
                dc_acc = grow(dc_acc, -jnp.sum(ds, axis=0, keepdims=True), 1)
                dcq_ref[rows, :] += jnp.sum(ds, axis=1, keepdims=True)
            return dk_acc, dv_acc, dc_acc

        carry = (jnp.zeros((t, 128), F32), jnp.zeros((t, 128), F32), jnp.zeros((1, t), F32))
        carry = block(j, carry, True)
        dk_acc, dv_acc, dc_acc = lax.fori_loop(j + 1, nb, lambda i, cr: block(i, cr, False), carry)
        dk_ref[...] = (dk_acc * FOX_SCALE).astype(dk_ref.dtype)
        dv_ref[...] = dv_acc.astype(dv_ref.dtype)
        dc_ref[...] = dc_acc

    whole = lambda h, j: (0, h)
    return _call(
        body, name="fox_bwd", grid=(FOX_HEADS, nb),
        in_specs=[pl.BlockSpec((s, 128), whole), pl.BlockSpec((s, 128), whole), pl.BlockSpec((s, 128), whole),
                  pl.BlockSpec((None, s, 1), lambda h, j: (h, 0, 0)),
                  pl.BlockSpec((t, 128), lambda h, j: (j, FOX_HEADS + h)),
                  pl.BlockSpec((t, 128), lambda h, j: (j, 2 * FOX_HEADS + h)),
                  pl.BlockSpec((None, 1, t), lambda h, j: (h, 0, j))],
        out_specs=[pl.BlockSpec((s, 128), whole),
                   pl.BlockSpec((t, 128), lambda h, j: (j, h)),
                   pl.BlockSpec((t, 128), lambda h, j: (j, h)),
                   pl.BlockSpec((None, 1, t), lambda h, j: (h, 0, j)),
                   pl.BlockSpec((None, s, 1), lambda h, j: (h, 0, 0))],
        out_shape=[jax.ShapeDtypeStruct((s, 1024), F32), jax.ShapeDtypeStruct((s, 1024), BF16),
                   jax.ShapeDtypeStruct((s, 1024), BF16), jax.ShapeDtypeStruct((FOX_HEADS, 1, s), F32),
                   jax.ShapeDtypeStruct((FOX_HEADS, s, 1), F32)],
        scratch_shapes=[pltpu.VMEM((s, 1), F32)],
        compiler_params=_params(("parallel", "arbitrary")),
    )(pm, do, o, lse, pm, pm, crow)


GLA_SCALE = GLA_DK ** -0.5
GLA_Q_BLK = 3072 // 128
GLA_K_BLK = 3584 // 128
GLA_V_BLK = 4096 // 256


def _gla_gate(sm, wa_ref, b_ref):
    return jnp.dot(sm.astype(BF16), wa_ref[...], preferred_element_type=F32) + b_ref[...]


def _chunk_tri(n, kind):
    row = lax.broadcasted_iota(jnp.int32, (n, n), 0)
    col = lax.broadcasted_iota(jnp.int32, (n, n), 1)
    shift = CHUNK.bit_length() - 1
    same = (row >> shift) == (col >> shift)
    if kind == "upto":
        same = same & (row >= col)
    elif kind == "before":
        same = same & (row > col)
    return same.astype(BF16)


def _gla_fwd(pm, small, wa_pad, b_a2, deps=()):
    s = pm.shape[0]
    r = _tile(s, GLA_R)
    nc = r // CHUNK

    def body(q_ref, k_ref, v_ref, sm_ref, wa_ref, b_ref, o_ref, st_ref, state):
        @pl.when(pl.program_id(1) == 0)
        def _():
            state[...] = jnp.zeros_like(state)

        la_all = _log_sigmoid(_gla_gate(sm_ref[...], wa_ref, b_ref)) * (1.0 / GLA_TEMP)
        tri = _chunk_tri(CHUNK, "upto")
        uts, decays = [], []
        for c in range(nc):
            rows = slice(c * CHUNK, (c + 1) * CHUNK)
            la = la_all[rows]
            cum = _dot_exact01(la, tri, tri_first=True)
            total = jnp.sum(la, axis=0, keepdims=True)
            kdec = k_ref[rows, :].astype(F32) * jnp.exp(total - cum)
            uts.append(lax.dot_general(v_ref[rows, :], kdec.astype(BF16), (TN, ((), ())),
                                       preferred_element_type=F32))
            decays.append(jnp.exp(total))
        cur = state[...]
        ends = []
        for c in range(nc):
            cur = cur * decays[c] + uts[c]
            ends.append(cur.astype(BF16))
        state[...] = cur
        for c in range(nc):
            rows = slice(c * CHUNK, (c + 1) * CHUNK)
            st_ref[c] = ends[c]
            qs = (q_ref[rows, :].astype(F32) * GLA_SCALE).astype(BF16)
            o_ref[rows, :] = lax.dot_general(qs, ends[c], (NT, ((), ())), preferred_element_type=F32)

    return _call(
        body, deps=deps, name="gla_fwd", grid=(GLA_HEADS, s // r),
        in_specs=[pl.BlockSpec((r, 128), lambda h, i: (i, GLA_Q_BLK + h)),
                  pl.BlockSpec((r, 128), lambda h, i: (i, GLA_K_BLK + h)),
                  pl.BlockSpec((r, 256), lambda h, i: (i, GLA_V_BLK + h)),
                  pl.BlockSpec((r, W_SMALL), lambda h, i: (i, 0)),
                  pl.BlockSpec((W_SMALL, 128), lambda h, i: (0, h)),
                  pl.BlockSpec((1, 128), lambda h, i: (0, h))],
        out_specs=[pl.BlockSpec((r, 256), lambda h, i: (i, h)),
                   pl.BlockSpec((nc, None, GLA_DV, GLA_DK), lambda h, i: (i, h, 0, 0))],
        out_shape=[jax.ShapeDtypeStruct((s, 1024), F32),
                   jax.ShapeDtypeStruct((s // CHUNK, GLA_HEADS, GLA_DV, GLA_DK), BF16)],
        scratch_shapes=[pltpu.VMEM((GLA_DV, GLA_DK), F32)],
        compiler_params=_params(("parallel", "arbitrary")),
    )(pm, pm, pm, small, wa_pad, b_a2)


def _gla_bwd(pm, small, wa_pad, b_a2, states, do):
    s = pm.shape[0]
    r = _tile(s, GLA_R)
    nc = r // CHUNK
    nb = s // r

    def body(q_ref, k_ref, v_ref, sm_ref, wa_ref, b_ref, do_ref, st_ref, prev_ref,
             dq_ref, dk_ref, dv_ref, dza_ref, db_ref, carry):
        step = pl.program_id(1)

        @pl.when(step == 0)
        def _():
            carry[...] = jnp.zeros_like(carry)
            db_ref[...] = jnp.zeros_like(db_ref)

        z_all = _gla_gate(sm_ref[...], wa_ref, b_ref)
        la_all = _log_sigmoid(z_all) * (1.0 / GLA_TEMP)
        tri = _chunk_tri(CHUNK, "upto")
        tri_strict = _chunk_tri(CHUNK, "before")
        ws, decays, kdecs, gouts = [], [], [], []
        for c in range(nc):
            rows = slice(c * CHUNK, (c + 1) * CHUNK)
            la = la_all[rows]
            cum = _dot_exact01(la, tri, tri_first=True)
            total = jnp.sum(la, axis=0, keepdims=True)
            w = jnp.exp(total - cum)
            ws.append(w)
            decays.append(jnp.exp(total))
            kdecs.append(k_ref[rows, :].astype(F32) * w)
            dov = do_ref[rows, :]
            qs = (q_ref[rows, :].astype(F32) * GLA_SCALE).astype(BF16)
            dq_ref[rows, :] = (jnp.dot(dov, st_ref[c], preferred_element_type=F32) * GLA_SCALE).astype(BF16)
            gouts.append(lax.dot_general(dov, qs, (TN, ((), ())), preferred_element_type=F32))
        cur = carry[...]
        gts = [None] * nc
        for c in reversed(range(nc)):
            gts[c] = gouts[c] + cur
            cur = gts[c] * decays[c]
        carry[...] = cur
        db = jnp.zeros((1, 128), F32)
        for c in range(nc):
            rows = slice(c * CHUNK, (c + 1) * CHUNK)
            gtb = gts[c].astype(BF16)
            dv_ref[rows, :] = lax.dot_general(kdecs[c].astype(BF16), gtb, (NT, ((), ())),
                                              preferred_element_type=F32).astype(BF16)
            dkdec = jnp.dot(v_ref[rows, :], gtb, preferred_element_type=F32)
            dk_ref[rows, :] = (dkdec * ws[c]).astype(BF16)
            e = dkdec * kdecs[c]
            if c > 0:
                prev = st_ref[c - 1].astype(F32)
            else:
                prev = jnp.where(step == nb - 1, 0.0, prev_ref[0].astype(F32))
            dtot = jnp.sum(gts[c] * prev, axis=0, keepdims=True) * decays[c]
            dla = dtot + _dot_exact01(e, tri_strict, tri_first=True)
            dza = dla * (1.0 / GLA_TEMP) * _sigmoid(-z_all[rows])
            dza_ref[rows, :] = dza.astype(BF16)
            db = db + jnp.sum(dza, axis=0, keepdims=True)
        db_ref[...] += db

    blk = lambda h, i: nb - 1 - i
    return _call(
        body, name="gla_bwd", grid=(GLA_HEADS, nb),
        in_specs=[pl.BlockSpec((r, 128), lambda h, i: (blk(h, i), GLA_Q_BLK + h)),
                  pl.BlockSpec((r, 128), lambda h, i: (blk(h, i), GLA_K_BLK + h)),
                  pl.BlockSpec((r, 256), lambda h, i: (blk(h, i), GLA_V_BLK + h)),
                  pl.BlockSpec((r, W_SMALL), lambda h, i: (blk(h, i), 0)),
                  pl.BlockSpec((W_SMALL, 128), lambda h, i: (0, h)),
                  pl.BlockSpec((1, 128), lambda h, i: (0, h)),
                  pl.BlockSpec((r, 256), lambda h, i: (blk(h, i), h)),
                  pl.BlockSpec((nc, None, GLA_DV, GLA_DK), lambda h, i: (blk(h, i), h, 0, 0)),
                  pl.BlockSpec((1, None, GLA_DV, GLA_DK),
                               lambda h, i: (jnp.maximum(blk(h, i) * nc - 1, 0), h, 0, 0))],
        out_specs=[pl.BlockSpec((r, 128), lambda h, i: (blk(h, i), h)),
                   pl.BlockSpec((r, 128), lambda h, i: (blk(h, i), h)),
                   pl.BlockSpec((r, 256), lambda h, i: (blk(h, i), h)),
                   pl.BlockSpec((r, 128), lambda h, i: (blk(h, i), h)),
                   pl.BlockSpec((1, 128), lambda h, i: (0, h))],
        out_shape=[jax.ShapeDtypeStruct((s, 512), BF16), jax.ShapeDtypeStruct((s, 512), BF16),
                   jax.ShapeDtypeStruct((s, 1024), BF16), jax.ShapeDtypeStruct((s, 512), BF16),
                   jax.ShapeDtypeStruct((1, 512), F32)],
        scratch_shapes=[pltpu.VMEM((GLA_DV, GLA_DK), F32)],
        compiler_params=_params(("parallel", "arbitrary")),
    )(pm, pm, pm, small, wa_pad, b_a2, do, states, states)


def _modulation(c_all, w_ada):
    n = w_ada.shape[1]
    tn = _tile(n, 512)

    def body(c_ref, w_ref, out_ref, ca_ref):
        cv = c_ref[...]
        ca = cv * _sigmoid(cv)
        ca_ref[...] = ca
        out_ref[...] = jnp.dot(ca.astype(BF16), w_ref[...].astype(BF16), preferred_element_type=F32)

    return _call(body, name="modulation", grid=(n // tn,),
                 in_specs=[pl.BlockSpec((N_DEV, D_MODEL), lambda j: (0, 0)),
                           pl.BlockSpec((D_MODEL, tn), lambda j: (0, j))],
                 out_specs=[pl.BlockSpec((N_DEV, tn), lambda j: (0, j)),
                            pl.BlockSpec((N_DEV, D_MODEL), lambda j: (0, 0))],
                 out_shape=[jax.ShapeDtypeStruct((N_DEV, n), F32), jax.ShapeDtypeStruct((N_DEV, D_MODEL), F32)],
                 compiler_params=_params(("arbitrary",)))(c_all, w_ada)


def _adamw_math(w, g, m, v):
    m = ADAM_B1 * m + (1.0 - ADAM_B1) * g
    v = ADAM_B2 * v + (1.0 - ADAM_B2) * (g * g)
    m_hat = m / (1.0 - ADAM_B1 ** ADAM_STEP)
    v_hat = v / (1.0 - ADAM_B2 ** ADAM_STEP)
    delta = -ADAM_LR * (m_hat / (jnp.sqrt(v_hat) + ADAM_EPS) + ADAM_WD * w)
    return delta, m, v


def _adamw_slabs(name, w, slabs, m, v, tr=256):
    rr, cc = w.shape

    def body(w_ref, s_ref, m_ref, v_ref, g_ref, d_ref, nm_ref, nv_ref):
        g = s_ref[0].astype(F32)
        for r in range(1, N_DEV):
            g = g + s_ref[r].astype(F32)
        g_ref[...] = g
        d, nm, nv = _adamw_math(w_ref[...], g, m_ref[...], v_ref[...])
        d_ref[...] = d
        nm_ref[...] = nm
        nv_ref[...] = nv

    steps, spec, slab_spec = _plane_tiles(rr, cc, tr)
    return _call(body, name=name, grid=(steps,),
                 in_specs=[spec, slab_spec, spec, spec],
                 out_specs=[spec] * 4, out_shape=[jax.ShapeDtypeStruct((rr, cc), F32)] * 4,
                 compiler_params=_params(("parallel",)))(w, slabs, m, v)


def _adamw_ada(w, cat, dm, m, v, tr=256, deps=()):
    rr, cc = w.shape
    tr = _tile(rr, tr)

    def body(w_ref, ca_ref, dm_ref, m_ref, v_ref, g_ref, d_ref, nm_ref, nv_ref):
        g = ca_ref[:, 0:1] * dm_ref[0:1, :]
        for b in range(1, N_DEV):
            g = g + ca_ref[:, b:b + 1] * dm_ref[b:b + 1, :]
        g_ref[...] = g
        d, nm, nv = _adamw_math(w_ref[...], g, m_ref[...], v_ref[...])
        d_ref[...] = d
        nm_ref[...] = nm
        nv_ref[...] = nv

    spec = pl.BlockSpec((tr, cc), lambda i: (i, 0))
    return _call(body, deps=deps, name="adamw_ada", grid=(rr // tr,),
                 in_specs=[spec, pl.BlockSpec((tr, N_DEV), lambda i: (i, 0)),
                           pl.BlockSpec((N_DEV, cc), lambda i: (0, 0)), spec, spec],
                 out_specs=[spec] * 4, out_shape=[jax.ShapeDtypeStruct((rr, cc), F32)] * 4,
                 compiler_params=_params(("parallel",)))(w, cat, dm, m, v)


def _sum_devices(gathered):
    ln = gathered.shape[-1]

    def body(g_ref, out_ref):
        acc = g_ref[0]
        for r in range(1, N_DEV):
            acc = acc + g_ref[r]
        out_ref[...] = acc

    return _call(body, name="sum_devices",
                 in_specs=[pl.BlockSpec(memory_space=pltpu.VMEM)], out_specs=pl.BlockSpec(memory_space=pltpu.VMEM),
                 out_shape=jax.ShapeDtypeStruct((1, ln), F32))(gathered)


def _adamw_flat(w, g, m, v):
    def body(w_ref, g_ref, m_ref, v_ref, d_ref, nm_ref, nv_ref):
        d, nm, nv = _adamw_math(w_ref[...], g_ref[...], m_ref[...], v_ref[...])
        d_ref[...] = d
        nm_ref[...] = nm
        nv_ref[...] = nv

    vm = pl.BlockSpec(memory_space=pltpu.VMEM)
    return _call(body, name="adamw_small", in_specs=[vm] * 4, out_specs=[vm] * 3,
                 out_shape=[jax.ShapeDtypeStruct(w.shape, F32)] * 3)(w, g, m, v)


def _from_col_shards(g):
    return jnp.transpose(g, (1, 0, 2)).reshape(g.shape[1], N_DEV * g.shape[2])


def _pad_lanes(v, n):
    return jnp.concatenate([v, jnp.zeros(v.shape[:-1] + (n - v.shape[-1],), v.dtype)], axis=-1)


def kernel(x, c, w_ada, b_ada, g_pre_mix, g_post_mix, w_in, b_fgate, w_gla_a2, b_gla_a2, g_fox_out, g_gla_out, w_out, g_pre_mlp, g_post_mlp, w_mlp_in, w_mlp_out, loss_target, m_w_ada, m_b_ada, m_g_pre_mix, m_g_post_mix, m_w_in, m_b_fgate, m_w_gla_a2, m_b_gla_a2, m_g_fox_out, m_g_gla_out, m_w_out, m_g_pre_mlp, m_g_post_mlp, m_w_mlp_in, m_w_mlp_out, v_w_ada, v_b_ada, v_g_pre_mix, v_g_post_mix, v_w_in, v_b_fgate, v_w_gla_a2, v_b_gla_a2, v_g_fox_out, v_g_gla_out, v_w_out, v_g_pre_mlp, v_g_post_mlp, v_w_mlp_in, v_w_mlp_out):
    rank = _my_rank()
    xs = x[0]
    s = xs.shape[0]
    target = loss_target[0]

    w_in_t, m_in_t, v_in_t = w_in[0].T, m_w_in[0].T, v_w_in[0].T
    c_all, wa2_g, ggla_g, win_g = _all_gather("gather_first", [c, w_gla_a2[0], g_gla_out[0], w_in_t.astype(BF16)])
    rest = [_own_slot("own_w_out", w_out[0], True, rank), _own_slot("own_w_mlp_in", w_mlp_in[0], True, rank)]
    gs_send, gs_sib, gs_ici, gs_land, gs_token = _gather2_start("gather_rest_start", rest, after=(c_all,))
    last = [_own_slot("own_w_mlp_out", w_mlp_out[0], True, rank)]
    gl_send, gl_sib, gl_ici, gl_land, gl_token = _gather2_start("gather_last_start", last, after=(gs_token,))
    w_a2 = _from_col_shards(wa2_g)
    g_gla = _from_col_shards(ggla_g).reshape(1, 1024)
    g_fox = g_fox_out.reshape(1, 1024)
    win_full = win_g.reshape(N_DEV * 771, D_MODEL)
    w_main = jnp.concatenate([win_full[:3072], win_full[3080:5128], win_full[5144:6168]], axis=0)
    w_small = jnp.concatenate([win_full[3072:3080], win_full[5128:5144],
                               jnp.zeros((W_SMALL - 24, D_MODEL), BF16)], axis=0)
    wa_pad =jnp.concatenate([jnp.zeros((8, 512), BF16), w_a2.astype(BF16), jnp.zeros((104, 512), BF16)], axis=0)
    bf_vec = _pad_lanes(b_fgate, W_SMALL)

    mod_part, c_act = _modulation(c_all.reshape(N_DEV, D_MODEL), w_ada[0])
    (mod_g,) = _all_gather("gather_mod", [mod_part])
    mod = lax.dynamic_slice_in_dim(mod_g, rank, 1, axis=1).reshape(1, 6 * D_MODEL) + b_ada
    shift_m, scale_m, gate_m, shift_f, scale_f, gate_f = [mod[:, i * D_MODEL:(i + 1) * D_MODEL] for i in range(6)]

    h = _premix(xs, g_pre_mix, scale_m, shift_m, deps=(gl_token,))
    pm = _mm_plain("proj_main", h, w_main, NT, BF16)
    small = _mm_plain("proj_small", h, w_small, NT, F32)
    crow = _fox_cum(small, bf_vec).reshape(FOX_HEADS, 1, s)
    o_fox, lse = _fox_fwd(pm, crow)
    gs_fsend, gs_frecv, gs_land, gs_ftoken = _gather2_forward("gather_rest_forward", gs_land, gs_ici, o_fox)
    o_gla, states = _gla_fwd(pm, small, wa_pad, b_gla_a2, deps=(gs_ftoken,))
    mix = _mix_fwd(o_fox, o_gla, pm, g_fox, g_gla)
    wout_g, wmi_g = _gather2_wait("gather_rest_wait", gs_land, gs_send, gs_sib, gs_fsend, gs_frecv, mix)
    w_out_full = wout_g.reshape(D_MODEL, D_MODEL)
    y = _mm_plain("out_proj", mix, w_out_full, NN, BF16)
    x1, h2 = _postmix_premlp(xs, y, gate_m, g_post_mix, g_pre_mlp, scale_f, shift_f)
    gl_fsend, gl_frecv, gl_land, gl_ftoken = _gather2_forward("gather_last_forward", gl_land, gl_ici, h2)

    tm, tn, tk = _tile(s, 1024), 1024, 2048
    nsh = 1024 // tn

    def relu2(acc):
        rl = jnp.maximum(acc, 0.0)
        return rl * rl, rl

    z, a_relu = _matmul(
        "mlp_in", h2, wmi_g, contract=NN, grid=(s // tm, D_FF // tn, D_MODEL // tk),
        a_spec=pl.BlockSpec((tm, tk), lambda i, j, k: (i, k)),
        b_spec=pl.BlockSpec((None, tk, tn), lambda i, j, k: (j // nsh, k, j % nsh)),
        out_specs=[pl.BlockSpec((tm, tn), lambda i, j, k: (i, j))] * 2,
        out_shapes=[jax.ShapeDtypeStruct((s, D_FF), BF16)] * 2, acc_shape=(tm, tn), epilogue=relu2,
        deps=(gl_ftoken,))
    (wmo_g,) = _gather2_wait("gather_last_wait", gl_land, gl_send, gl_sib, gl_fsend, gl_frecv, z)
    w_mo_full = wmo_g.reshape(D_FF, D_MODEL)
    y2 = _mm_plain("mlp_out", z, w_mo_full, NN, BF16)

    dx2, dy2, loss_vec, dgate_f, dg_post_mlp = _loss_postmlp_bwd(x1, y2, target, gate_f, g_post_mlp)
    loss = lax.psum(loss_vec[0, 0], ("x", "y", "c"))

    da = _mm_plain("mlp_out_dx", dy2, w_mo_full, NT, BF16, extra=(a_relu,),
                   epilogue=lambda acc, rl: (acc * (2.0 * rl.astype(F32)),))
    dw_mo = _mm_plain("mlp_out_dw", z, dy2, TN, BF16)
    dw_mo = dw_mo.reshape(N_DEV, 1024, D_MODEL)
    x_mo = _exchange_start("grad_mlp_out_start", [_own_slot("own_dw_mlp_out", dw_mo, False, rank)], [dw_mo])
    tkx = 2048
    (dh2,) = _matmul(
        "mlp_in_dx", da, wmi_g, contract=NT, grid=(s // tm, D_MODEL // tn, D_FF // tkx),
        a_spec=pl.BlockSpec((tm, tkx), lambda i, j, k: (i, k)),
        b_spec=pl.BlockSpec((tkx // 1024, tn, 1024), lambda i, j, k: (k, j, 0)),
        out_specs=[pl.BlockSpec((tm, tn), lambda i, j, k: (i, j))],
        out_shapes=[jax.ShapeDtypeStruct((s, D_MODEL), BF16)], acc_shape=(tm, tn), deps=(x_mo[4],))
    ts = _tile(s, 2048)
    (dw_mi,) = _matmul(
        "mlp_in_dw", h2, da, contract=TN, grid=(D_MODEL // 1024, D_FF // tn, s // ts),
        a_spec=pl.BlockSpec((ts, 1024), lambda i, j, k: (k, i)),
        b_spec=pl.BlockSpec((ts, tn), lambda i, j, k: (k, j)),
        out_specs=[pl.BlockSpec((None, 1024, tn), lambda i, j, k: (j // nsh, i, j % nsh))],
        out_shapes=[jax.ShapeDtypeStruct((N_DEV, D_MODEL, 1024), BF16)], acc_shape=(1024, tn))
    x_mi = _exchange_start("grad_mlp_in_start", [_own_slot("own_dw_mlp_in", dw_mi, False, rank)], [dw_mi])

    dx1, dy, dscale_f, dshift_f, dg_pre_mlp, dgate_m, dg_post_mix = _premlp_postmix_bwd(
        dh2, dx2, x1, y, scale_f, g_pre_mlp, gate_m, g_post_mix, deps=(x_mi[4],))

    dmix = _mm_plain("out_proj_dx", dy, w_out_full, NT, BF16)
    dw_out = _mm_plain("out_proj_dw", mix, dy, TN, BF16)
    dw_out = dw_out.reshape(N_DEV, 256, D_MODEL)
    x_out = _exchange_start("grad_out_start", [_own_slot("own_dw_out", dw_out, False, rank)], [dw_out])
    do_fox, do_gla, dgr, dg_fox, dg_gla = _mix_bwd(dmix, o_fox, o_gla, pm, g_fox, g_gla, deps=(x_out[4],))

    dq, dk, dv, dc, dcq = _fox_bwd(pm, crow, o_fox, lse, do_fox)
    dsmall_f, db_f = _fox_cum_bwd(dc.reshape(FOX_HEADS, s), dcq, small, bf_vec)
    dgq, dgk, dgv, dza, db_a2 = _gla_bwd(pm, small, wa_pad, b_gla_a2, states, do_gla)
    dsmall = _mm_plain("gate_dx", dza, wa_pad, NT, F32, tn=128, extra=(dsmall_f,),
                       epilogue=lambda acc, other: (acc + other,))
    dwa_pad = _mm_plain("gate_dw", small, dza, TN, F32, tm=128, tn=512)

    dpm = jnp.concatenate([dq.astype(BF16), dk.astype(BF16), dv.astype(BF16), dgq.astype(BF16), dgk.astype(BF16),
                           dgv.astype(BF16), dgr], axis=1)
    dw_main = _mm_plain("proj_main_dw", dpm, h, TN, BF16)
    dw_small = _mm_plain("proj_small_dw", dsmall, h, TN, BF16, tm=128)
    dwin_full = jnp.concatenate([dw_main[:3072], dw_small[0:8], dw_main[3072:5120], dw_small[8:24],
                                 dw_main[5120:6144]], axis=0)
    dwin_slabs = dwin_full.reshape(N_DEV, 771, D_MODEL)
    x_in = _exchange_start("grad_in_start", [_own_slot("own_dw_in", dwin_slabs, False, rank)], [dwin_slabs])
    tmx = _tile(s, 1024)
    (dh,) = _matmul(
        "proj_main_dx", dpm, w_main, contract=NN, grid=(s // tmx, D_MODEL // 1024, W_MAIN // 2048),
        a_spec=pl.BlockSpec((tmx, 2048), lambda i, j, k: (i, k)),
        b_spec=pl.BlockSpec((2048, 1024), lambda i, j, k: (k, j)),
        out_specs=[pl.BlockSpec((tmx, 1024), lambda i, j, k: (i, j))],
        out_shapes=[jax.ShapeDtypeStruct((s, D_MODEL), BF16)], acc_shape=(tmx, 1024),
        extra=(dsmall, w_small),
        extra_specs=[pl.BlockSpec((tmx, W_SMALL), lambda i, j, k: (i, 0)),
                     pl.BlockSpec((W_SMALL, 1024), lambda i, j, k: (0, j))],
        epilogue=lambda acc, dsm, wsm: (acc + jnp.dot(dsm.astype(BF16), wsm, preferred_element_type=F32),),
        deps=(x_in[4],))
    grad_x, dscale_m, dshift_m, dg_pre_mix = _premix_bwd(dh, dx1, xs, g_pre_mix, scale_m)

    dmod = jnp.concatenate([dshift_m, dscale_m, dgate_m, dshift_f, dscale_f, dgate_f], axis=1)
    flat = jnp.concatenate(
        [dmod, dg_pre_mix, dg_post_mix, dg_fox, dg_pre_mlp, dg_post_mlp, db_a2,
         dwa_pad[8:24, :].reshape(1, GLA_RANK * 512), dg_gla, _pad_lanes(db_f[:, 0].reshape(1, FOX_HEADS), 128)],
        axis=1)

    (r_mo,) = _exchange_wait("grad_mlp_out_wait", *x_mo[:4], grad_x)
    g_mo, d_mo, nm_mo, nv_mo = _adamw_slabs("adamw_w_mlp_out", w_mlp_out[0], r_mo, m_w_mlp_out[0], v_w_mlp_out[0])
    (r_mi,) = _exchange_wait("grad_mlp_in_wait", *x_mi[:4], g_mo)
    g_mi, d_mi, nm_mi, nv_mi = _adamw_slabs("adamw_w_mlp_in", w_mlp_in[0], r_mi, m_w_mlp_in[0], v_w_mlp_in[0])
    (r_out,) = _exchange_wait("grad_out_wait", *x_out[:4], g_mi)
    g_out, d_out, nm_out, nv_out = _adamw_slabs("adamw_w_out", w_out[0], r_out, m_w_out[0], v_w_out[0])

    (flat_g,) = _all_gather("gather_small_grads", [flat], deps=(g_out,))
    tot = _sum_devices(flat_g)
    dm_cols = lax.dynamic_slice_in_dim(flat_g[:, 0, :6 * D_MODEL], rank * 1536, 1536, axis=1)
    (r_in,) = _exchange_wait("grad_in_wait", *x_in[:4], tot)
    in_t = _adamw_slabs("adamw_w_in", w_in_t, r_in, m_in_t, v_in_t)
    g_in, d_in, nm_in, nv_in = [a.T for a in in_t]
    g_ada, d_ada, nm_ada, nv_ada = _adamw_ada(w_ada[0], c_act.T, dm_cols, m_w_ada[0], v_w_ada[0], deps=(in_t[0],))

    o = 0
    seg = {}
    for name, n in (("b_ada", 12288), ("g_pre_mix", 2048), ("g_post_mix", 2048), ("g_fox_out", 1024),
                    ("g_pre_mlp", 2048), ("g_post_mlp", 2048), ("b_gla_a2", 512), ("w_gla_a2", 8192),
                    ("g_gla_out", 1024), ("b_fgate", 128)):
        seg[name] = tot[:, o:o + n]
        o += n
    g_wa2 = lax.dynamic_slice_in_dim(seg["w_gla_a2"].reshape(GLA_RANK, 512), rank * 64, 64, axis=1)
    g_ggla = lax.dynamic_slice_in_dim(seg["g_gla_out"].reshape(GLA_HEADS, GLA_DV), rank * 32, 32, axis=1)
    small_names = ["b_ada", "g_pre_mix", "g_post_mix", "g_fox_out", "g_pre_mlp", "g_post_mlp", "b_gla_a2",
                   "w_gla_a2", "g_gla_out", "b_fgate"]
    small_grads = {**seg, "w_gla_a2": g_wa2.reshape(1, 1024), "g_gla_out": g_ggla.reshape(1, 128)}
    weights = dict(b_ada=b_ada, g_pre_mix=g_pre_mix, g_post_mix=g_post_mix, g_fox_out=g_fox_out,
                   g_pre_mlp=g_pre_mlp, g_post_mlp=g_post_mlp, b_gla_a2=b_gla_a2, w_gla_a2=w_gla_a2,
                   g_gla_out=g_gla_out, b_fgate=b_fgate)
    moms = dict(b_ada=m_b_ada, g_pre_mix=m_g_pre_mix, g_post_mix=m_g_post_mix, g_fox_out=m_g_fox_out,
                g_pre_mlp=m_g_pre_mlp, g_post_mlp=m_g_post_mlp, b_gla_a2=m_b_gla_a2, w_gla_a2=m_w_gla_a2,
                g_gla_out=m_g_gla_out, b_fgate=m_b_fgate)
    vels = dict(b_ada=v_b_ada, g_pre_mix=v_g_pre_mix, g_post_mix=v_g_post_mix, g_fox_out=v_g_fox_out,
                g_pre_mlp=v_g_pre_mlp, g_post_mlp=v_g_post_mlp, b_gla_a2=v_b_gla_a2, w_gla_a2=v_w_gla_a2,
                g_gla_out=v_g_gla_out, b_fgate=v_b_fgate)

    def flatten(d, fill):
        parts = []
        for nm in small_names:
            p = d[nm].reshape(1, -1)
            if nm == "b_fgate":
                p = jnp.concatenate([p[:, :FOX_HEADS], jnp.full((1, 128 - FOX_HEADS), fill, F32)], axis=1)
            parts.append(p)
        return jnp.concatenate(parts, axis=1).reshape(-1, 128)

    fw, fg, fm, fv = flatten(weights, 0.0), flatten(small_grads, 0.0), flatten(moms, 0.0), flatten(vels, 1.0)
    fd, fnm, fnv = _adamw_flat(fw, fg, fm, fv)

    def unflatten(fl):
        fl = fl.reshape(1, -1)
        out = {}
        o = 0
        for nm in small_names:
            n = 128 if nm == "b_fgate" else weights[nm].size
            piece = fl[:, o:o + n]
            if nm == "b_fgate":
                piece = piece[:, :FOX_HEADS]
            out[nm] = piece.reshape(weights[nm].shape)
            o += n
        return out

    sg, sd, snm, snv = unflatten(fg), unflatten(fd), unflatten(fnm), unflatten(fnv)

    big = dict(w_ada=(g_ada, d_ada, nm_ada, nv_ada), w_in=(g_in, d_in, nm_in, nv_in),
               w_out=(g_out, d_out, nm_out, nv_out), w_mlp_in=(g_mi, d_mi, nm_mi, nv_mi),
               w_mlp_out=(g_mo, d_mo, nm_mo, nv_mo))
    order = ["w_ada", "b_ada", "g_pre_mix", "g_post_mix", "w_in", "b_fgate", "w_gla_a2", "b_gla_a2", "g_fox_out",
             "g_gla_out", "w_out", "g_pre_mlp", "g_post_mlp", "w_mlp_in", "w_mlp_out"]

    def pick(nm, idx):
        if nm in big:
            return big[nm][idx][None]
        return (sg, sd, snm, snv)[idx][nm]

    grads = [pick(nm, 0) for nm in order]
    deltas = [pick(nm, 1) for nm in order]
    new_m = [pick(nm, 2) for nm in order]
    new_v = [pick(nm, 3) for nm in order]
    return (loss, grad_x[None], *grads, *deltas, *new_m, *new_v)
```

```python
import functools

import numpy as np
import jax
import jax.numpy as jnp
from jax import lax
from jax.experimental import pallas as pl
from jax.experimental.pallas import tpu as pltpu

F32 = jnp.float32
BF16 = jnp.bfloat16
MESH = pl.DeviceIdType.MESH
N_DEV = 8

D_MODEL = 2048
FOX_HEADS = 8
FOX_HEAD_DIM = 128
GLA_HEADS = 4
GLA_DK = 128
GLA_DV = 256
GLA_RANK = 16
GLA_TEMP = 16.0
CHUNK = 64
D_FF = 8192
W_MAIN = 6144
W_SMALL = 128
EPS = 1e-6
NEG = float(np.finfo(np.float32).min)

ADAM_LR = 0.001
ADAM_B1 = 0.9
ADAM_B2 = 0.999
ADAM_EPS = 1e-08
ADAM_WD = 0.01
ADAM_STEP = 10

ROW_T = 256
FOX_T = 1024
GLA_R = 512
CUM_T = 256
VMEM_LIMIT = 56 * 1024 * 1024


def _call(body, deps=(), **kw):
    if not deps:
        return pl.pallas_call(body, **kw)
    n_in, n_dep = len(kw["in_specs"]), len(deps)

    def with_deps(*refs):
        return body(*refs[:n_in], *refs[n_in + n_dep:])

    kw["in_specs"] = [*kw["in_specs"], *[pl.BlockSpec(memory_space=pl.ANY)] * n_dep]
    call = pl.pallas_call(with_deps, **kw)
    return lambda *args: call(*args, *deps)


def _params(sem=None):
    return pltpu.CompilerParams(dimension_semantics=sem, vmem_limit_bytes=VMEM_LIMIT)


def _my_pos():
    return lax.axis_index("x"), lax.axis_index("y"), lax.axis_index("c")


def _my_rank():
    x, y, c = _my_pos()
    return 4 * x + 2 * y + c


def _all_gather(name, arrays, deps=()):
    n = len(arrays)

    def body(*refs):
        ins = refs[:n]
        outs = refs[n:2 * n]
        send_sems, recv_sems, local_sems = refs[2 * n:]
        x, y, c = _my_pos()
        me, sibling = (x, y, c), (x, y, 1 - c)
        chips = [(1 - x, y), (x, 1 - y), (1 - x, 1 - y)]

        def slot(a, px, py, pc):
            return outs[a].at[4 * px + 2 * py + pc]

        def copy(a, k, block, to, src=None):
            return pltpu.make_async_remote_copy(
                src_ref=slot(a, *block) if src is None else src, dst_ref=slot(a, *block),
                send_sem=send_sems.at[a, k], recv_sem=recv_sems.at[a, k],
                device_id=to, device_id_type=MESH)

        started = []
        for a in range(n):
            mine = pltpu.make_async_copy(ins[a], slot(a, *me), local_sems.at[a])
            mine.start()
            started.append(mine)
        first = []
        for a in range(n):
            first.append(copy(a, 0, me, sibling, src=ins[a]))
            first += [copy(a, 1 + j, me, (*chip, c), src=ins[a]) for j, chip in enumerate(chips)]
        for cp in first:
            cp.start()
        passed = []
        for j, chip in enumerate(chips):
            for a in range(n):
                copy(a, 1 + j, (*chip, c), me).wait_recv()
                fwd = copy(a, 4 + j, (*chip, c), sibling)
                fwd.start()
                passed.append(fwd)
        for a in range(n):
            copy(a, 0, sibling, me).wait_recv()
            for j, chip in enumerate(chips):
                copy(a, 4 + j, (*chip, 1 - c), me).wait_recv()
        for cp in first + passed:
            cp.wait_send()
        for mine in started:
            mine.wait()

    hbm = pl.BlockSpec(memory_space=pltpu.HBM)
    return _call(
        body, deps=deps, name=name,
        out_shape=[jax.ShapeDtypeStruct((N_DEV,) + a.shape, a.dtype) for a in arrays],
        in_specs=[hbm] * n, out_specs=[hbm] * n,
        scratch_shapes=[pltpu.SemaphoreType.DMA((n, 7)), pltpu.SemaphoreType.DMA((n, 7)),
                        pltpu.SemaphoreType.DMA((n,))],
    )(*arrays)


def _plane_tiles(rr, cc, tr=512, tc=512):
    if rr % 8 == 0:
        tr = _tile(rr, tr)
        return (rr // tr, pl.BlockSpec((tr, cc), lambda i: (i, 0)),
                pl.BlockSpec((N_DEV, tr, cc), lambda i: (0, i, 0)))
    tc = _tile(cc, tc)
    return (cc // tc, pl.BlockSpec((rr, tc), lambda i: (0, i)),
            pl.BlockSpec((N_DEV, rr, tc), lambda i: (0, 0, i)))


def _own_slot(name, src, gather, rank):
    shape = ((N_DEV,) + src.shape) if gather else src.shape
    rr, cc = shape[1], shape[2]
    by_rows = rr % 8 == 0
    tr, tc = (_tile(rr, 512), cc) if by_rows else (rr, _tile(cc, 512))
    steps = rr // tr if by_rows else cc // tc

    def body(rank_ref, s_ref, o_ref):
        o_ref[...] = s_ref[...].astype(o_ref.dtype)

    def at(i):
        return (i, 0) if by_rows else (0, i)

    if gather:
        in_spec = pl.BlockSpec((tr, tc), lambda i, rk: at(i))
    else:
        in_spec = pl.BlockSpec((None, tr, tc), lambda i, rk: (rk[0], *at(i)))
    grid_spec = pltpu.PrefetchScalarGridSpec(
        num_scalar_prefetch=1, grid=(steps,), in_specs=[in_spec],
        out_specs=pl.BlockSpec((None, tr, tc), lambda i, rk: (rk[0], *at(i))))
    return _call(body, name=name, grid_spec=grid_spec, out_shape=jax.ShapeDtypeStruct(shape, BF16),
                 compiler_params=_params(("arbitrary",)))(jnp.reshape(rank, (1,)).astype(jnp.int32), src)


_HBM = pl.BlockSpec(memory_space=pltpu.HBM)
_SEM = pl.BlockSpec(memory_space=pltpu.SEMAPHORE)
_FLIPS = [(kx, ky, kc) for kx in (0, 1) for ky in (0, 1) for kc in (0, 1)][1:]


def _peers():
    x, y, c = _my_pos()
    out = []
    for kx, ky, kc in _FLIPS:
        px, py, pc = (1 - x if kx else x), (1 - y if ky else y), (1 - c if kc else c)
        out.append(((px, py, pc), 4 * px + 2 * py + pc))
    return out


def _exchange_copy(srcs, lands, send_sems, recv_sems, a, k, peer, peer_rank, slot):
    return pltpu.make_async_remote_copy(
        src_ref=lands[a].at[slot] if srcs is None else srcs[a].at[peer_rank],
        dst_ref=lands[a].at[slot],
        send_sem=send_sems[a].at[k], recv_sem=recv_sems[a].at[k],
        device_id=peer, device_id_type=MESH)


def _exchange_start(name, lands, srcs=None, after=()):
    n = len(lands)
    n_src = 0 if srcs is None else n
    n_in = n + n_src + len(after)

    def body(*refs):
        lnd = refs[:n]
        src = None if srcs is None else refs[n:2 * n]
        send_sems, recv_sems = refs[n_in:n_in + n], refs[n_in + n:n_in + 2 * n]
        token = refs[-1]
        me = _my_rank()
        for a in range(n):
            for k, (peer, peer_rank) in enumerate(_peers()):
                _exchange_copy(src, lnd, send_sems, recv_sems, a, k, peer, peer_rank, me).start()
        token[...] = jnp.zeros_like(token)

    sems = [pltpu.SemaphoreType.DMA((7,))] * (2 * n)
    thru = list(lands) + ([] if srcs is None else list(srcs))
    outs = pl.pallas_call(
        body, name=name,
        out_shape=(*sems, *[pltpu.HBM(t.shape, t.dtype) for t in thru], jax.ShapeDtypeStruct((8, 128), F32)),
        in_specs=[*[_HBM] * len(thru), *[pl.BlockSpec(memory_space=pl.ANY)] * len(after)],
        out_specs=(*[_SEM] * (2 * n), *[_HBM] * len(thru), pl.BlockSpec(memory_space=pltpu.VMEM)),
        input_output_aliases={i: 2 * n + i for i in range(len(thru))},
        compiler_params=pltpu.CompilerParams(has_side_effects=pltpu.SideEffectType.DATAFLOW_SIDE_EFFECTING),
    )(*[pltpu.with_memory_space_constraint(t, pltpu.HBM) for t in thru], *after)
    lands_thru = outs[2 * n:3 * n]
    srcs_thru = None if srcs is None else outs[3 * n:4 * n]
    return outs[:n], outs[n:2 * n], srcs_thru, lands_thru, outs[-1]


def _exchange_wait(name, send_sems, recv_sems, srcs, lands, after):
    n = len(lands)
    thru = list(lands) + ([] if srcs is None else list(srcs))

    def body(*refs):
        lnd = refs[:n]
        src = None if srcs is None else refs[n:2 * n]
        ssem, rsem = refs[len(thru):len(thru) + n], refs[len(thru) + n:len(thru) + 2 * n]
        for a in range(n):
            for k, (peer, peer_rank) in enumerate(_peers()):
                cp = _exchange_copy(src, lnd, ssem, rsem, a, k, peer, peer_rank, peer_rank)
                cp.wait_send()
                cp.wait_recv()

    outs = pl.pallas_call(
        body, name=name,
        out_shape=tuple(pltpu.HBM(t.shape, t.dtype) for t in thru),
        in_specs=[*[_HBM] * len(thru), *[_SEM] * (2 * n), pl.BlockSpec(memory_space=pl.ANY)],
        out_specs=tuple([_HBM] * len(thru)),
        input_output_aliases={i: i for i in range(len(thru))},
        compiler_params=pltpu.CompilerParams(has_side_effects=pltpu.SideEffectType.DATAFLOW_SIDE_EFFECTING),
    )(*thru, *send_sems, *recv_sems, after)
    return outs[:n]


_SIDE = pltpu.CompilerParams(has_side_effects=pltpu.SideEffectType.DATAFLOW_SIDE_EFFECTING)
_ANY = pl.BlockSpec(memory_space=pl.ANY)


def _chips():
    x, y, _ = _my_pos()
    return [(1 - x, y), (x, 1 - y), (1 - x, 1 - y)]


def _slot_copy(lnd, slot, send_sem, recv_sem, to):
    return pltpu.make_async_remote_copy(src_ref=lnd.at[slot], dst_ref=lnd.at[slot], send_sem=send_sem,
                                        recv_sem=recv_sem, device_id=to, device_id_type=MESH)


def _gather2_start(name, lands, after=()):
    n = len(lands)
    n_in = n + len(after)

    def body(*refs):
        lnd = refs[:n]
        send, recv_sib, recv_ici = refs[n_in:n_in + n], refs[n_in + n:n_in + 2 * n], refs[n_in + 2 * n:n_in + 3 * n]
        x, y, c = _my_pos()
        me = 4 * x + 2 * y + c
        for a in range(n):
            _slot_copy(lnd[a], me, send[a].at[0], recv_sib[a].at[0], (x, y, 1 - c)).start()
            for j, chip in enumerate(_chips()):
                _slot_copy(lnd[a], me, send[a].at[1 + j], recv_ici[a].at[j], (*chip, c)).start()
        refs[-1][...] = jnp.zeros_like(refs[-1])

    sems = [pltpu.SemaphoreType.DMA((4,))] * n + [pltpu.SemaphoreType.DMA((1,))] * n + [pltpu.SemaphoreType.DMA((3,))] * n
    outs = pl.pallas_call(
        body, name=name,
        out_shape=(*sems, *[pltpu.HBM(t.shape, t.dtype) for t in lands], jax.ShapeDtypeStruct((8, 128), F32)),
        in_specs=[*[_HBM] * n, *[_ANY] * len(after)],
        out_specs=(*[_SEM] * (3 * n), *[_HBM] * n, pl.BlockSpec(memory_space=pltpu.VMEM)),
        input_output_aliases={i: 3 * n + i for i in range(n)}, compiler_params=_SIDE,
    )(*[pltpu.with_memory_space_constraint(t, pltpu.HBM) for t in lands], *after)
    return outs[:n], outs[n:2 * n], outs[2 * n:3 * n], outs[3 * n:4 * n], outs[-1]


def _gather2_forward(name, lands, recv_ici, after):
    n = len(lands)

    def body(*refs):
        lnd, arrived = refs[:n], refs[n:2 * n]
        send, recv = refs[2 * n + 1:3 * n + 1], refs[3 * n + 1:4 * n + 1]
        x, y, c = _my_pos()
        for j, (cx, cy) in enumerate(_chips()):
            slot = 4 * cx + 2 * cy + c
            for a in range(n):
                _slot_copy(lnd[a], slot, send[a].at[j], arrived[a].at[j], (cx, cy, c)).wait_recv()
                _slot_copy(lnd[a], slot, send[a].at[j], recv[a].at[j], (x, y, 1 - c)).start()
        refs[-1][...] = jnp.zeros_like(refs[-1])

    sems = [pltpu.SemaphoreType.DMA((3,))] * (2 * n)
    outs = pl.pallas_call(
        body, name=name,
        out_shape=(*sems, *[pltpu.HBM(t.shape, t.dtype) for t in lands], jax.ShapeDtypeStruct((8, 128), F32)),
        in_specs=[*[_HBM] * n, *[_SEM] * n, _ANY],
        out_specs=(*[_SEM] * (2 * n), *[_HBM] * n, pl.BlockSpec(memory_space=pltpu.VMEM)),
        input_output_aliases={i: 2 * n + i for i in range(n)}, compiler_params=_SIDE,
    )(*lands, *recv_ici, after)
    return outs[:n], outs[n:2 * n], outs[2 * n:3 * n], outs[-1]


def _gather2_wait(name, lands, send_a, recv_sib, send_b, recv_b, after):
    n = len(lands)

    def body(*refs):
        lnd = refs[:n]
        sa, rs, sb, rb = (refs[(1 + i) * n:(2 + i) * n] for i in range(4))
        x, y, c = _my_pos()
        me = 4 * x + 2 * y + c
        for a in range(n):
            for k in range(4):
                _slot_copy(lnd[a], me, sa[a].at[k], rs[a].at[0], (x, y, 1 - c)).wait_send()
            _slot_copy(lnd[a], me - c + (1 - c), sa[a].at[0], rs[a].at[0], (x, y, 1 - c)).wait_recv()
            for j, (cx, cy) in enumerate(_chips()):
                _slot_copy(lnd[a], 4 * cx + 2 * cy + c, sb[a].at[j], rb[a].at[j], (x, y, 1 - c)).wait_send()
                _slot_copy(lnd[a], 4 * cx + 2 * cy + (1 - c), sb[a].at[j], rb[a].at[j], (x, y, 1 - c)).wait_recv()

    outs = pl.pallas_call(
        body, name=name,
        out_shape=tuple(pltpu.HBM(t.shape, t.dtype) for t in lands),
        in_specs=[*[_HBM] * n, *[_SEM] * (4 * n), _ANY],
        out_specs=tuple([_HBM] * n),
        input_output_aliases={i: i for i in range(n)}, compiler_params=_SIDE,
    )(*lands, *send_a, *recv_sib, *send_b, *recv_b, after)
    return outs


NN = ((1,), (0,))
NT = ((1,), (1,))
TN = ((0,), (0,))


def _matmul(name, a, b, *, contract, grid, a_spec, b_spec, out_specs, out_shapes, acc_shape,
            extra=(), extra_specs=(), epilogue=None, deps=()):
    nk = grid[2]
    n_extra = len(extra)
    n_out = len(out_shapes)

    def body(*refs):
        a_ref, b_ref = refs[0], refs[1]
        extra_refs = refs[2:2 + n_extra]
        out_refs = refs[2 + n_extra:2 + n_extra + n_out]
        acc_ref = refs[-1]
        k = pl.program_id(2)

        def prod():
            if len(b_ref.shape) == 2:
                return lax.dot_general(a_ref[...].astype(BF16), b_ref[...].astype(BF16), (contract, ((), ())),
                                       preferred_element_type=F32)
            kk = a_ref.shape[1] // b_ref.shape[0]
            acc = None
            for u in range(b_ref.shape[0]):
                part = lax.dot_general(a_ref[:, u * kk:(u + 1) * kk].astype(BF16), b_ref[u].astype(BF16),
                                       (contract, ((), ())), preferred_element_type=F32)
                acc = part if acc is None else acc + part
            return acc

        def finish(acc):
            res = (acc,) if epilogue is None else epilogue(acc, *[r[...] for r in extra_refs])
            for o_ref, val in zip(out_refs, res):
                o_ref[...] = val.astype(o_ref.dtype)

        if nk == 1:
            finish(prod())
            return

        @pl.when(k == 0)
        def _():
            acc_ref[...] = prod()

        @pl.when((k > 0) & (k < nk - 1))
        def _():
            acc_ref[...] += prod()

        @pl.when(k == nk - 1)
        def _():
            finish(acc_ref[...] + prod())

    outs = _call(
        body, deps=deps, name=name, grid=grid,
        in_specs=[a_spec, b_spec, *extra_specs], out_specs=list(out_specs), out_shape=list(out_shapes),
        scratch_shapes=[pltpu.VMEM(acc_shape if nk > 1 else (8, 128), F32)],
        compiler_params=_params(("parallel", "parallel", "arbitrary")),
    )(a, b, *extra)
    return outs


def _tile(n, t):
    t = min(n, t)
    assert n % t == 0, (n, t)
    return t


def _mm_plain(name, a, b, contract, out_dtype, tm=1024, tn=1024, tk=2048, extra=(), epilogue=None,
              n_out=1, out_dtypes=None, deps=()):
    if contract == NN:
        (m, kd), (_, n) = a.shape, b.shape
    elif contract == NT:
        (m, kd), (n, _) = a.shape, b.shape
    else:
        (kd, m), (_, n) = a.shape, b.shape
    tm, tn, tk = _tile(m, tm), _tile(n, tn), _tile(kd, tk)
    if contract == NN:
        a_spec = pl.BlockSpec((tm, tk), lambda i, j, k: (i, k))
        b_spec = pl.BlockSpec((tk, tn), lambda i, j, k: (k, j))
    elif contract == NT:
        a_spec = pl.BlockSpec((tm, tk), lambda i, j, k: (i, k))
        b_spec = pl.BlockSpec((tn, tk), lambda i, j, k: (j, k))
    else:
        a_spec = pl.BlockSpec((tk, tm), lambda i, j, k: (k, i))
        b_spec = pl.BlockSpec((tk, tn), lambda i, j, k: (k, j))
    o_spec = pl.BlockSpec((tm, tn), lambda i, j, k: (i, j))
    out_dtypes = out_dtypes or [out_dtype] * n_out
    outs = _matmul(
        name, a, b, contract=contract, grid=(m // tm, n // tn, kd // tk), a_spec=a_spec, b_spec=b_spec,
        out_specs=[o_spec] * len(out_dtypes), out_shapes=[jax.ShapeDtypeStruct((m, n), dt) for dt in out_dtypes],
        acc_shape=(tm, tn), extra=extra, extra_specs=[o_spec] * len(extra), epilogue=epilogue, deps=deps)
    return outs[0] if len(out_dtypes) == 1 else outs


def _rows_call(name, body, row_in, vec_in, row_out, vec_out, s, deps=()):
    t = _tile(s, ROW_T)
    in_specs = []
    args = []
    for arr, width, cb in row_in:
        in_specs.append(pl.BlockSpec((t, width), functools.partial(lambda i, cb: (i, cb), cb=cb)))
        args.append(arr)
    for v in vec_in:
        in_specs.append(pl.BlockSpec(v.shape, lambda i: (0, 0)))
        args.append(v)
    out_specs = []
    out_shapes = []
    for width, dt in row_out:
        out_specs.append(pl.BlockSpec((t, width), lambda i: (i, 0)))
        out_shapes.append(jax.ShapeDtypeStruct((s, width), dt))
    for width in vec_out:
        out_specs.append(pl.BlockSpec((1, width), lambda i: (0, 0)))
        out_shapes.append(jax.ShapeDtypeStruct((1, width), F32))
    return _call(body, deps=deps, name=name, grid=(s // t,), in_specs=in_specs, out_specs=out_specs,
                 out_shape=out_shapes, compiler_params=_params(("arbitrary",)))(*args)


def _acc_vec(ref, val):
    _acc_row(ref, jnp.sum(val, axis=0, keepdims=True))


def _acc_row(ref, part):
    @pl.when(pl.program_id(0) == 0)
    def _():
        ref[...] = part

    @pl.when(pl.program_id(0) > 0)
    def _():
        ref[...] += part


def _rms(v):
    return lax.rsqrt(jnp.mean(v * v, axis=-1, keepdims=True) + EPS)


def _norm_bwd(dxn, xn, r):
    return r * (dxn - xn * jnp.mean(dxn * xn, axis=-1, keepdims=True))


def _premix(x, g, scale, shift, deps=()):
    s = x.shape[0]

    def body(x_ref, g_ref, sc_ref, sh_ref, h_ref):
        xv = x_ref[...]
        h_ref[...] = ((xv * _rms(xv) * g_ref[...]) * (1.0 + sc_ref[...]) + sh_ref[...]).astype(BF16)

    return _rows_call("premix", body, [(x, D_MODEL, 0)], [g, scale, shift], [(D_MODEL, BF16)], [], s, deps)[0]


def _sigmoid(z):
    return 1.0 / (1.0 + jnp.exp(-z))


def _mix_fwd(o_fox, o_gla, pm, g_fox, g_gla):
    s = o_fox.shape[0]

    def body(of_ref, og_ref, gr_ref, gf_ref, gg_ref, mix_ref):
        for h in range(FOX_HEADS):
            sl = slice(h * FOX_HEAD_DIM, (h + 1) * FOX_HEAD_DIM)
            seg = of_ref[:, sl]
            mix_ref[:, sl] = (seg * _rms(seg) * gf_ref[:, sl]).astype(BF16)
        for h in range(GLA_HEADS):
            sl = slice(h * GLA_DV, (h + 1) * GLA_DV)
            seg = og_ref[:, sl]
            gr = gr_ref[:, sl].astype(F32)
            val = (seg * _rms(seg) * gg_ref[:, sl]) * (gr * _sigmoid(gr))
            mix_ref[:, pl.ds(FOX_HEADS * FOX_HEAD_DIM + h * GLA_DV, GLA_DV)] = val.astype(BF16)

    return _rows_call("mix_fwd", body, [(o_fox, 1024, 0), (o_gla, 1024, 0), (pm, 1024, 5)], [g_fox, g_gla],
                      [(D_MODEL, BF16)], [], s)[0]


def _mix_bwd(dmix, o_fox, o_gla, pm, g_fox, g_gla, deps=()):
    s = o_fox.shape[0]

    def body(dm_ref, of_ref, og_ref, gr_ref, gf_ref, gg_ref, dof_ref, dog_ref, dgr_ref, dgf_ref, dgg_ref):
        dgf = []
        for h in range(FOX_HEADS):
            sl = slice(h * FOX_HEAD_DIM, (h + 1) * FOX_HEAD_DIM)
            seg = of_ref[:, sl]
            r = _rms(seg)
            segn = seg * r
            dout = dm_ref[:, sl].astype(F32)
            dgf.append(jnp.sum(dout * segn, axis=0, keepdims=True))
            dof_ref[:, sl] = _norm_bwd(dout * gf_ref[:, sl], segn, r).astype(BF16)
        dgg = []
        for h in range(GLA_HEADS):
            sl = slice(h * GLA_DV, (h + 1) * GLA_DV)
            seg = og_ref[:, sl]
            r = _rms(seg)
            segn = seg * r
            gl = segn * gg_ref[:, sl]
            gr = gr_ref[:, sl].astype(F32)
            sig = _sigmoid(gr)
            dout = dm_ref[:, pl.ds(FOX_HEADS * FOX_HEAD_DIM + h * GLA_DV, GLA_DV)].astype(F32)
            dgr_ref[:, sl] = (dout * gl * (sig * (1.0 + gr * (1.0 - sig)))).astype(BF16)
            dgl = dout * (gr * sig)
            dgg.append(jnp.sum(dgl * segn, axis=0, keepdims=True))
            dog_ref[:, sl] = _norm_bwd(dgl * gg_ref[:, sl], segn, r).astype(BF16)
        _acc_row(dgf_ref, jnp.concatenate(dgf, axis=1))
        _acc_row(dgg_ref, jnp.concatenate(dgg, axis=1))

    return _rows_call("mix_bwd", body, [(dmix, D_MODEL, 0), (o_fox, 1024, 0), (o_gla, 1024, 0), (pm, 1024, 5)],
                      [g_fox, g_gla], [(1024, BF16), (1024, BF16), (1024, BF16)], [1024, 1024], s, deps)


def _postmix_premlp(x, y, gate_m, g_post_mix, g_pre_mlp, scale_f, shift_f):
    s = x.shape[0]

    def body(x_ref, y_ref, gm_ref, gpm_ref, gpl_ref, sc_ref, sh_ref, x1_ref, h2_ref):
        yv = y_ref[...].astype(F32)
        x1 = x_ref[...] + gm_ref[...] * (yv * _rms(yv) * gpm_ref[...])
        x1_ref[...] = x1
        h2_ref[...] = ((x1 * _rms(x1) * gpl_ref[...]) * (1.0 + sc_ref[...]) + sh_ref[...]).astype(BF16)

    return _rows_call("postmix_premlp", body, [(x, D_MODEL, 0), (y, D_MODEL, 0)],
                      [gate_m, g_post_mix, g_pre_mlp, scale_f, shift_f], [(D_MODEL, F32), (D_MODEL, BF16)], [], s)


def _loss_postmlp_bwd(x1, y2, target, gate_f, g_post_mlp):
    s = x1.shape[0]

    def body(x1_ref, y2_ref, t_ref, gf_ref, g_ref, dx2_ref, dy2_ref, loss_ref, dgate_ref, dg_ref):
        yv = y2_ref[...].astype(F32)
        r = _rms(yv)
        yn = yv * r
        o = yn * g_ref[...]
        e = (x1_ref[...] + gf_ref[...] * o) - t_ref[...]
        part = 0.5 * jnp.sum(jnp.mean(e * e, axis=-1, keepdims=True), axis=0, keepdims=True)
        _acc_vec(loss_ref, jnp.broadcast_to(part, (1, 128)))
        dx2 = e * (1.0 / D_MODEL)
        dx2_ref[...] = dx2.astype(BF16)
        _acc_vec(dgate_ref, dx2 * o)
        do = dx2 * gf_ref[...]
        _acc_vec(dg_ref, do * yn)
        dy2_ref[...] = _norm_bwd(do * g_ref[...], yn, r).astype(BF16)

    return _rows_call("loss_postmlp_bwd", body, [(x1, D_MODEL, 0), (y2, D_MODEL, 0), (target, D_MODEL, 0)],
                      [gate_f, g_post_mlp], [(D_MODEL, BF16), (D_MODEL, BF16)], [128, D_MODEL, D_MODEL], s)


def _premlp_postmix_bwd(dh2, dx2, x1, y, scale_f, g_pre_mlp, gate_m, g_post_mix, deps=()):
    s = x1.shape[0]

    def body(dh2_ref, dx2_ref, x1_ref, y_ref, sc_ref, gpl_ref, gm_ref, gpm_ref,
             dx1_ref, dy_ref, dsc_ref, dsh_ref, dgpl_ref, dgm_ref, dgpm_ref):
        x1 = x1_ref[...]
        r1 = _rms(x1)
        x1n = x1 * r1
        dh2 = dh2_ref[...].astype(F32)
        _acc_vec(dsc_ref, dh2 * (x1n * gpl_ref[...]))
        _acc_vec(dsh_ref, dh2)
        dn2 = dh2 * (1.0 + sc_ref[...])
        _acc_vec(dgpl_ref, dn2 * x1n)
        dx1 = dx2_ref[...].astype(F32) + _norm_bwd(dn2 * gpl_ref[...], x1n, r1)
        dx1_ref[...] = dx1.astype(BF16)
        yv = y_ref[...].astype(F32)
        ry = _rms(yv)
        yn = yv * ry
        _acc_vec(dgm_ref, dx1 * (yn * gpm_ref[...]))
        do = dx1 * gm_ref[...]
        _acc_vec(dgpm_ref, do * yn)
        dy_ref[...] = _norm_bwd(do * gpm_ref[...], yn, ry).astype(BF16)

    return _rows_call("premlp_postmix_bwd", body,
                      [(dh2, D_MODEL, 0), (dx2, D_MODEL, 0), (x1, D_MODEL, 0), (y, D_MODEL, 0)],
                      [scale_f, g_pre_mlp, gate_m, g_post_mix], [(D_MODEL, BF16), (D_MODEL, BF16)],
                      [D_MODEL] * 5, s, deps)


def _premix_bwd(dh, dx1, x, g_pre_mix, scale_m):
    s = x.shape[0]

    def body(dh_ref, dx1_ref, x_ref, g_ref, sc_ref, gx_ref, dsc_ref, dsh_ref, dg_ref):
        xv = x_ref[...]
        r = _rms(xv)
        xn = xv * r
        dh = dh_ref[...].astype(F32)
        _acc_vec(dsc_ref, dh * (xn * g_ref[...]))
        _acc_vec(dsh_ref, dh)
        dn1 = dh * (1.0 + sc_ref[...])
        _acc_vec(dg_ref, dn1 * xn)
        gx_ref[...] = dx1_ref[...].astype(F32) + _norm_bwd(dn1 * g_ref[...], xn, r)

    return _rows_call("premix_bwd", body, [(dh, D_MODEL, 0), (dx1, D_MODEL, 0), (x, D_MODEL, 0)],
                      [g_pre_mix, scale_m], [(D_MODEL, F32)], [D_MODEL] * 3, s)


def _split3(v):
    hi = v.astype(BF16)
    r1 = v - hi.astype(F32)
    mid = r1.astype(BF16)
    lo = (r1 - mid.astype(F32)).astype(BF16)
    return hi, mid, lo


def _dot_exact01(v, tri, contract=NN, tri_first=False):
    acc = None
    for part in _split3(v):
        lhs, rhs = (tri, part) if tri_first else (part, tri)
        p = lax.dot_general(lhs, rhs, (contract, ((), ())), preferred_element_type=F32)
        acc = p if acc is None else acc + p
    return acc


def _log_sigmoid(z):
    return jnp.minimum(z, 0.0) - jnp.log(1.0 + jnp.exp(-jnp.abs(z)))


def _fox_cum(small, bvec):
    s = small.shape[0]
    t = _tile(s, CUM_T)

    def body(sm_ref, b_ref, out_ref, carry):
        @pl.when(pl.program_id(0) == 0)
        def _():
            carry[...] = jnp.zeros_like(carry)

        lf = _log_sigmoid(sm_ref[...] + b_ref[...])
        lft = lf.T[0:FOX_HEADS, :]
        row = lax.broadcasted_iota(jnp.int32, (t, t), 0)
        col = lax.broadcasted_iota(jnp.int32, (t, t), 1)
        upper = (row <= col).astype(BF16)
        cum = _dot_exact01(lft, upper) + carry[:, 0:1]
        out_ref[...] = cum
        carry[...] = carry[...] + jnp.sum(lft, axis=1, keepdims=True)

    return _call(body, name="fox_cum", grid=(s // t,),
                 in_specs=[pl.BlockSpec((t, W_SMALL), lambda i: (i, 0)), pl.BlockSpec((1, W_SMALL), lambda i: (0, 0))],
                 out_specs=pl.BlockSpec((FOX_HEADS, t), lambda i: (0, i)),
                 out_shape=jax.ShapeDtypeStruct((FOX_HEADS, s), F32),
                 scratch_shapes=[pltpu.VMEM((FOX_HEADS, 128), F32)],
                 compiler_params=_params(("arbitrary",)))(small, bvec)


def _fox_cum_bwd(dc, dcq, small, bvec):
    s = small.shape[0]
    t = _tile(s, CUM_T)
    nb = s // t

    def body(dc_ref, dcq_ref, sm_ref, b_ref, out_ref, db_ref, carry):
        @pl.when(pl.program_id(0) == 0)
        def _():
            carry[...] = jnp.zeros_like(carry)
            db_ref[...] = jnp.zeros_like(db_ref)

        lane = lax.broadcasted_iota(jnp.int32, (t, W_SMALL), 1)
        dcq = jnp.zeros((t, W_SMALL), F32)
        for hh in range(FOX_HEADS):
            dcq = jnp.where(lane == hh, dcq_ref[hh], dcq)
        dcv = dc_ref[...] + dcq.T[0:FOX_HEADS, :]
        row = lax.broadcasted_iota(jnp.int32, (t, t), 0)
        col = lax.broadcasted_iota(jnp.int32, (t, t), 1)
        lower = (row >= col).astype(BF16)
        dlf = _dot_exact01(dcv, lower) + carry[:, 0:1]
        carry[...] = carry[...] + jnp.sum(dcv, axis=1, keepdims=True)
        z = sm_ref[...] + b_ref[...]
        zt = z.T[0:FOX_HEADS, :]
        dff = dlf * _sigmoid(-zt)
        db_ref[...] = db_ref[...] + jnp.sum(dff, axis=1, keepdims=True)
        full = jnp.concatenate([dff, jnp.zeros((W_SMALL - FOX_HEADS, t), F32)], axis=0)
        out_ref[...] = full.T

    return _call(body, name="fox_cum_bwd", grid=(nb,),
                 in_specs=[pl.BlockSpec((FOX_HEADS, t), lambda i: (0, nb - 1 - i)),
                           pl.BlockSpec((FOX_HEADS, t, 1), lambda i: (0, nb - 1 - i, 0)),
                           pl.BlockSpec((t, W_SMALL), lambda i: (nb - 1 - i, 0)),
                           pl.BlockSpec((1, W_SMALL), lambda i: (0, 0))],
                 out_specs=[pl.BlockSpec((t, W_SMALL), lambda i: (nb - 1 - i, 0)),
                            pl.BlockSpec((FOX_HEADS, 128), lambda i: (0, 0))],
                 out_shape=[jax.ShapeDtypeStruct((s, W_SMALL), F32), jax.ShapeDtypeStruct((FOX_HEADS, 128), F32)],
                 scratch_shapes=[pltpu.VMEM((FOX_HEADS, 128), F32)],
                 compiler_params=_params(("arbitrary",)))(dc, dcq, small, bvec)


FOX_SCALE = FOX_HEAD_DIM ** -0.5


def _fox_fwd(pm, crow):
    s = pm.shape[0]
    t = _tile(s, FOX_T)
    nb = s // t
    parts = 2
    hq = t // parts

    def body(q_ref, k_ref, v_ref, c_ref, o_ref, lse_ref):
        i = pl.program_id(1)
        qs = [q_ref[g * hq:(g + 1) * hq, :] for g in range(parts)]

        def block(j, carry, diagonal):
            rows = pl.ds(pl.multiple_of(j * t, t), t)
            k_all, v_all, c_all = k_ref[rows, :], v_ref[rows, :], c_ref[j]
            out = []
            for g, (m_prev, l_prev, acc) in enumerate(carry):
                nk = (g + 1) * hq if diagonal else t
                kb, vb, cb = k_all[:nk], v_all[:nk], c_all[:, :nk]
                sc = lax.dot_general(qs[g], kb, (NT, ((), ())), preferred_element_type=F32)
                sc = sc * FOX_SCALE - cb
                if diagonal:
                    row = lax.broadcasted_iota(jnp.int32, (hq, nk), 0) + g * hq
                    col = lax.broadcasted_iota(jnp.int32, (hq, nk), 1)
                    sc = jnp.where(row >= col, sc, NEG)
                m_new = jnp.maximum(m_prev, jnp.max(sc, axis=1, keepdims=True))
                alpha = jnp.exp(m_prev - m_new)
                p = jnp.exp(sc - m_new)
                l_new = alpha * l_prev + jnp.sum(p, axis=1, keepdims=True)
                pv = jnp.dot(p.astype(BF16), vb, preferred_element_type=F32)
                out.append((m_new, l_new, alpha * acc + pv))
            return tuple(out)

        init = tuple((jnp.full((hq, 1), NEG, F32), jnp.zeros((hq, 1), F32), jnp.zeros((hq, 128), F32))
                     for _ in range(parts))
        carry = lax.fori_loop(0, i, lambda j, cr: block(j, cr, False), init)
        carry = block(i, carry, True)
        for g, (m_fin, l_fin, acc) in enumerate(carry):
            o_ref[g * hq:(g + 1) * hq, :] = acc / l_fin
            lse_ref[g * hq:(g + 1) * hq, :] = m_fin + jnp.log(l_fin)

    return _call(
        body, name="fox_fwd", grid=(FOX_HEADS, nb),
        in_specs=[pl.BlockSpec((t, 128), lambda h, i: (i, h)),
                  pl.BlockSpec((s, 128), lambda h, i: (0, FOX_HEADS + h)),
                  pl.BlockSpec((s, 128), lambda h, i: (0, 2 * FOX_HEADS + h)),
                  pl.BlockSpec((None, nb, 1, t), lambda h, i: (h, 0, 0, 0))],
        out_specs=[pl.BlockSpec((t, 128), lambda h, i: (i, h)),
                   pl.BlockSpec((None, t, 1), lambda h, i: (h, i, 0))],
        out_shape=[jax.ShapeDtypeStruct((s, FOX_HEADS * 128), F32), jax.ShapeDtypeStruct((FOX_HEADS, s, 1), F32)],
        compiler_params=_params(("parallel", "arbitrary")),
    )(pm, pm, pm, crow.reshape(FOX_HEADS, nb, 1, t))


def _fox_bwd(pm, crow, o, lse, do):
    s = pm.shape[0]
    t = _tile(s, FOX_T)
    nb = s // t

    parts = 2
    hq = t // parts

    def body(q_ref, do_ref, o_ref, lse_ref, k_ref, v_ref, c_ref, dq_ref, dk_ref, dv_ref, dc_ref, dcq_ref, delta_s):
        j = pl.program_id(1)

        @pl.when(j == 0)
        def _():
            dq_ref[...] = jnp.zeros_like(dq_ref)
            dcq_ref[...] = jnp.zeros_like(dcq_ref)
            delta_s[...] = jnp.sum(do_ref[...].astype(F32) * o_ref[...], axis=1, keepdims=True)

        k_all, v_all, c_all = k_ref[...], v_ref[...], c_ref[...]

        def grow(acc, part, axis):
            n = part.shape[axis]
            if n == acc.shape[axis]:
                return acc + part
            if axis == 0:
                return jnp.concatenate([acc[:n] + part, acc[n:]], axis=0)
            return jnp.concatenate([acc[:, :n] + part, acc[:, n:]], axis=1)

        def block(i, carry, diagonal):
            dk_acc, dv_acc, dc_acc = carry
            for g in range(parts):
                nk = (g + 1) * hq if diagonal else t
                kb, vb, cb = k_all[:nk], v_all[:nk], c_all[:, :nk]
                rows = pl.ds(pl.multiple_of(i * t + g * hq, hq), hq)
                q, dov = q_ref[rows, :], do_ref[rows, :]
                sc = lax.dot_general(q, kb, (NT, ((), ())), preferred_element_type=F32)
                p = jnp.exp(sc * FOX_SCALE - cb - lse_ref[rows, :])
                if diagonal:
                    row = lax.broadcasted_iota(jnp.int32, (hq, nk), 0) + g * hq
                    col = lax.broadcasted_iota(jnp.int32, (hq, nk), 1)
                    p = jnp.where(row >= col, p, 0.0)
                dp = lax.dot_general(dov, vb, (NT, ((), ())), preferred_element_type=F32)
                ds = p * (dp - delta_s[rows, :])
                dsb = ds.astype(BF16)
                dv_acc = grow(dv_acc, lax.dot_general(p.astype(BF16), dov, (TN, ((), ())),
                                                      preferred_element_type=F32), 0)
                dk_acc = grow(dk_acc, lax.dot_general(dsb, q, (TN, ((), ())), preferred_element_type=F32), 0)
                dq_ref[rows, :] += jnp.dot(dsb, kb, preferred_element_type=F32) * FOX_SCALE
                dc_acc = grow(dc_acc, -jnp.sum(ds, axis=0, keepdims=True), 1)
                dcq_ref[rows, :] += jnp.sum(ds, axis=1, keepdims=True)
            return dk_acc, dv_acc, dc_acc

        carry = (jnp.zeros((t, 128), F32), jnp.zeros((t, 128), F32), jnp.zeros((1, t), F32))
        carry = block(j, carry, True)
        dk_acc, dv_acc, dc_acc = lax.fori_loop(j + 1, nb, lambda i, cr: block(i, cr, False), carry)
        dk_ref[...] = (dk_acc * FOX_SCALE).astype(dk_ref.dtype)
        dv_ref[...] = dv_acc.astype(dv_ref.dtype)
        dc_ref[...] = dc_acc

    whole = lambda h, j: (0, h)
    return _call(
        body, name="fox_bwd", grid=(FOX_HEADS, nb),
        in_specs=[pl.BlockSpec((s, 128), whole), pl.BlockSpec((s, 128), whole), pl.BlockSpec((s, 128), whole),
                  pl.BlockSpec((None, s, 1), lambda h, j: (h, 0, 0)),
                  pl.BlockSpec((t, 128), lambda h, j: (j, FOX_HEADS + h)),
                  pl.BlockSpec((t, 128), lambda h, j: (j, 2 * FOX_HEADS + h)),
                  pl.BlockSpec((None, 1, t), lambda h, j: (h, 0, j))],
        out_specs=[pl.BlockSpec((s, 128), whole),
                   pl.BlockSpec((t, 128), lambda h, j: (j, h)),
                   pl.BlockSpec((t, 128), lambda h, j: (j, h)),
                   pl.BlockSpec((None, 1, t), lambda h, j: (h, 0, j)),
                   pl.BlockSpec((None, s, 1), lambda h, j: (h, 0, 0))],
        out_shape=[jax.ShapeDtypeStruct((s, 1024), F32), jax.ShapeDtypeStruct((s, 1024), BF16),
                   jax.ShapeDtypeStruct((s, 1024), BF16), jax.ShapeDtypeStruct((FOX_HEADS, 1, s), F32),
                   jax.ShapeDtypeStruct((FOX_HEADS, s, 1), F32)],
        scratch_shapes=[pltpu.VMEM((s, 1), F32)],
        compiler_params=_params(("parallel", "arbitrary")),
    )(pm, do, o, lse, pm, pm, crow)


GLA_SCALE = GLA_DK ** -0.5
GLA_Q_BLK = 3072 // 128
GLA_K_BLK = 3584 // 128
GLA_V_BLK = 4096 // 256


def _gla_gate(sm, wa_ref, b_ref):
    return jnp.dot(sm.astype(BF16), wa_ref[...], preferred_element_type=F32) + b_ref[...]


def _chunk_tri(n, kind):
    row = lax.broadcasted_iota(jnp.int32, (n, n), 0)
    col = lax.broadcasted_iota(jnp.int32, (n, n), 1)
    shift = CHUNK.bit_length() - 1
    same = (row >> shift) == (col >> shift)
    if kind == "upto":
        same = same & (row >= col)
    elif kind == "before":
        same = same & (row > col)
    return same.astype(BF16)


def _gla_fwd(pm, small, wa_pad, b_a2, deps=()):
    s = pm.shape[0]
    r = _tile(s, GLA_R)
    nc = r // CHUNK

    def body(q_ref, k_ref, v_ref, sm_ref, wa_ref, b_ref, o_ref, st_ref, state):
        @pl.when(pl.program_id(1) == 0)
        def _():
            state[...] = jnp.zeros_like(state)

        la_all = _log_sigmoid(_gla_gate(sm_ref[...], wa_ref, b_ref)) * (1.0 / GLA_TEMP)
        tri = _chunk_tri(CHUNK, "upto")
        uts, decays = [], []
        for c in range(nc):
            rows = slice(c * CHUNK, (c + 1) * CHUNK)
            la = la_all[rows]
            cum = _dot_exact01(la, tri, tri_first=True)
            total = jnp.sum(la, axis=0, keepdims=True)
            kdec = k_ref[rows, :].astype(F32) * jnp.exp(total - cum)
            uts.append(lax.dot_general(v_ref[rows, :], kdec.astype(BF16), (TN, ((), ())),
                                       preferred_element_type=F32))
            decays.append(jnp.exp(total))
        cur = state[...]
        ends = []
        for c in range(nc):
            cur = cur * decays[c] + uts[c]
            ends.append(cur.astype(BF16))
        state[...] = cur
        for c in range(nc):
            rows = slice(c * CHUNK, (c + 1) * CHUNK)
            st_ref[c] = ends[c]
            qs = (q_ref[rows, :].astype(F32) * GLA_SCALE).astype(BF16)
            o_ref[rows, :] = lax.dot_general(qs, ends[c], (NT, ((), ())), preferred_element_type=F32)

    return _call(
        body, deps=deps, name="gla_fwd", grid=(GLA_HEADS, s // r),
        in_specs=[pl.BlockSpec((r, 128), lambda h, i: (i, GLA_Q_BLK + h)),
                  pl.BlockSpec((r, 128), lambda h, i: (i, GLA_K_BLK + h)),
                  pl.BlockSpec((r, 256), lambda h, i: (i, GLA_V_BLK + h)),
                  pl.BlockSpec((r, W_SMALL), lambda h, i: (i, 0)),
                  pl.BlockSpec((W_SMALL, 128), lambda h, i: (0, h)),
                  pl.BlockSpec((1, 128), lambda h, i: (0, h))],
        out_specs=[pl.BlockSpec((r, 256), lambda h, i: (i, h)),
                   pl.BlockSpec((nc, None, GLA_DV, GLA_DK), lambda h, i: (i, h, 0, 0))],
        out_shape=[jax.ShapeDtypeStruct((s, 1024), F32),
                   jax.ShapeDtypeStruct((s // CHUNK, GLA_HEADS, GLA_DV, GLA_DK), BF16)],
        scratch_shapes=[pltpu.VMEM((GLA_DV, GLA_DK), F32)],
        compiler_params=_params(("parallel", "arbitrary")),
    )(pm, pm, pm, small, wa_pad, b_a2)


def _gla_bwd(pm, small, wa_pad, b_a2, states, do):
    s = pm.shape[0]
    r = _tile(s, GLA_R)
    nc = r // CHUNK
    nb = s // r

    def body(q_ref, k_ref, v_ref, sm_ref, wa_ref, b_ref, do_ref, st_ref, prev_ref,
             dq_ref, dk_ref, dv_ref, dza_ref, db_ref, carry):
        step = pl.program_id(1)

        @pl.when(step == 0)
        def _():
            carry[...] = jnp.zeros_like(carry)
            db_ref[...] = jnp.zeros_like(db_ref)

        z_all = _gla_gate(sm_ref[...], wa_ref, b_ref)
        la_all = _log_sigmoid(z_all) * (1.0 / GLA_TEMP)
        tri = _chunk_tri(CHUNK, "upto")
        tri_strict = _chunk_tri(CHUNK, "before")
        ws, decays, kdecs, gouts = [], [], [], []
        for c in range(nc):
            rows = slice(c * CHUNK, (c + 1) * CHUNK)
            la = la_all[rows]
            cum = _dot_exact01(la, tri, tri_first=True)
            total = jnp.sum(la, axis=0, keepdims=True)
            w = jnp.exp(total - cum)
            ws.append(w)
            decays.append(jnp.exp(total))
            kdecs.append(k_ref[rows, :].astype(F32) * w)
            dov = do_ref[rows, :]
            qs = (q_ref[rows, :].astype(F32) * GLA_SCALE).astype(BF16)
            dq_ref[rows, :] = (jnp.dot(dov, st_ref[c], preferred_element_type=F32) * GLA_SCALE).astype(BF16)
            gouts.append(lax.dot_general(dov, qs, (TN, ((), ())), preferred_element_type=F32))
        cur = carry[...]
        gts = [None] * nc
        for c in reversed(range(nc)):
            gts[c] = gouts[c] + cur
            cur = gts[c] * decays[c]
        carry[...] = cur
        db = jnp.zeros((1, 128), F32)
        for c in range(nc):
            rows = slice(c * CHUNK, (c + 1) * CHUNK)
            gtb = gts[c].astype(BF16)
            dv_ref[rows, :] = lax.dot_general(kdecs[c].astype(BF16), gtb, (NT, ((), ())),
                                              preferred_element_type=F32).astype(BF16)
            dkdec = jnp.dot(v_ref[rows, :], gtb, preferred_element_type=F32)
            dk_ref[rows, :] = (dkdec * ws[c]).astype(BF16)
            e = dkdec * kdecs[c]
            if c > 0:
                prev = st_ref[c - 1].astype(F32)
            else:
                prev = jnp.where(step == nb - 1, 0.0, prev_ref[0].astype(F32))
            dtot = jnp.sum(gts[c] * prev, axis=0, keepdims=True) * decays[c]
            dla = dtot + _dot_exact01(e, tri_strict, tri_first=True)
            dza = dla * (1.0 / GLA_TEMP) * _sigmoid(-z_all[rows])
            dza_ref[rows, :] = dza.astype(BF16)
            db = db + jnp.sum(dza, axis=0, keepdims=True)
        db_ref[...] += db

    blk = lambda h, i: nb - 1 - i
    return _call(
        body, name="gla_bwd", grid=(GLA_HEADS, nb),
        in_specs=[pl.BlockSpec((r, 128), lambda h, i: (blk(h, i), GLA_Q_BLK + h)),
                  pl.BlockSpec((r, 128), lambda h, i: (blk(h, i), GLA_K_BLK + h)),
                  pl.BlockSpec((r, 256), lambda h, i: (blk(h, i), GLA_V_BLK + h)),
                  pl.BlockSpec((r, W_SMALL), lambda h, i: (blk(h, i), 0)),
                  pl.BlockSpec((W_SMALL, 128), lambda h, i: (0, h)),
                  pl.BlockSpec((1, 128), lambda h, i: (0, h)),
                  pl.BlockSpec((r, 256), lambda h, i: (blk(h, i), h)),
                  pl.BlockSpec((nc, None, GLA_DV, GLA_DK), lambda h, i: (blk(h, i), h, 0, 0)),
                  pl.BlockSpec((1, None, GLA_DV, GLA_DK),
                               lambda h, i: (jnp.maximum(blk(h, i) * nc - 1, 0), h, 0, 0))],
        out_specs=[pl.BlockSpec((r, 128), lambda h, i: (blk(h, i), h)),
                   pl.BlockSpec((r, 128), lambda h, i: (blk(h, i), h)),
                   pl.BlockSpec((r, 256), lambda h, i: (blk(h, i), h)),
                   pl.BlockSpec((r, 128), lambda h, i: (blk(h, i), h)),
                   pl.BlockSpec((1, 128), lambda h, i: (0, h))],
        out_shape=[jax.ShapeDtypeStruct((s, 512), BF16), jax.ShapeDtypeStruct((s, 512), BF16),
                   jax.ShapeDtypeStruct((s, 1024), BF16), jax.ShapeDtypeStruct((s, 512), BF16),
                   jax.ShapeDtypeStruct((1, 512), F32)],
        scratch_shapes=[pltpu.VMEM((GLA_DV, GLA_DK), F32)],
        compiler_params=_params(("parallel", "arbitrary")),
    )(pm, pm, pm, small, wa_pad, b_a2, do, states, states)


def _modulation(c_all, w_ada):
    n = w_ada.shape[1]
    tn = _tile(n, 512)

    def body(c_ref, w_ref, out_ref, ca_ref):
        cv = c_ref[...]
        ca = cv * _sigmoid(cv)
        ca_ref[...] = ca
        out_ref[...] = jnp.dot(ca.astype(BF16), w_ref[...].astype(BF16), preferred_element_type=F32)

    return _call(body, name="modulation", grid=(n // tn,),
                 in_specs=[pl.BlockSpec((N_DEV, D_MODEL), lambda j: (0, 0)),
                           pl.BlockSpec((D_MODEL, tn), lambda j: (0, j))],
                 out_specs=[pl.BlockSpec((N_DEV, tn), lambda j: (0, j)),
                            pl.BlockSpec((N_DEV, D_MODEL), lambda j: (0, 0))],
                 out_shape=[jax.ShapeDtypeStruct((N_DEV, n), F32), jax.ShapeDtypeStruct((N_DEV, D_MODEL), F32)],
                 compiler_params=_params(("arbitrary",)))(c_all, w_ada)


def _adamw_math(w, g, m, v):
    m = ADAM_B1 * m + (1.0 - ADAM_B1) * g
    v = ADAM_B2 * v + (1.0 - ADAM_B2) * (g * g)
    m_hat = m / (1.0 - ADAM_B1 ** ADAM_STEP)
    v_hat = v / (1.0 - ADAM_B2 ** ADAM_STEP)
    delta = -ADAM_LR * (m_hat / (jnp.sqrt(v_hat) + ADAM_EPS) + ADAM_WD * w)
    return delta, m, v


def _adamw_slabs(name, w, slabs, m, v, tr=256):
    rr, cc = w.shape

    def body(w_ref, s_ref, m_ref, v_ref, g_ref, d_ref, nm_ref, nv_ref):
        g = s_ref[0].astype(F32)
        for r in range(1, N_DEV):
            g = g + s_ref[r].astype(F32)
        g_ref[...] = g
        d, nm, nv = _adamw_math(w_ref[...], g, m_ref[...], v_ref[...])
        d_ref[...] = d
        nm_ref[...] = nm
        nv_ref[...] = nv

    steps, spec, slab_spec = _plane_tiles(rr, cc, tr)
    return _call(body, name=name, grid=(steps,),
                 in_specs=[spec, slab_spec, spec, spec],
                 out_specs=[spec] * 4, out_shape=[jax.ShapeDtypeStruct((rr, cc), F32)] * 4,
                 compiler_params=_params(("parallel",)))(w, slabs, m, v)


def _adamw_ada(w, cat, dm, m, v, tr=256, deps=()):
    rr, cc = w.shape
    tr = _tile(rr, tr)

    def body(w_ref, ca_ref, dm_ref, m_ref, v_ref, g_ref, d_ref, nm_ref, nv_ref):
        g = ca_ref[:, 0:1] * dm_ref[0:1, :]
        for b in range(1, N_DEV):
            g = g + ca_ref[:, b:b + 1] * dm_ref[b:b + 1, :]
        g_ref[...] = g
        d, nm, nv = _adamw_math(w_ref[...], g, m_ref[...], v_ref[...])
        d_ref[...] = d
        nm_ref[...] = nm
        nv_ref[...] = nv

    spec = pl.BlockSpec((tr, cc), lambda i: (i, 0))
    return _call(body, deps=deps, name="adamw_ada", grid=(rr // tr,),
                 in_specs=[spec, pl.BlockSpec((tr, N_DEV), lambda i: (i, 0)),
                           pl.BlockSpec((N_DEV, cc), lambda i: (0, 0)), spec, spec],
                 out_specs=[spec] * 4, out_shape=[jax.ShapeDtypeStruct((rr, cc), F32)] * 4,
                 compiler_params=_params(("parallel",)))(w, cat, dm, m, v)


def _sum_devices(gathered):
    ln = gathered.shape[-1]

    def body(g_ref, out_ref):
        acc = g_ref[0]
        for r in range(1, N_DEV):
            acc = acc + g_ref[r]
        out_ref[...] = acc

    return _call(body, name="sum_devices",
                 in_specs=[pl.BlockSpec(memory_space=pltpu.VMEM)], out_specs=pl.BlockSpec(memory_space=pltpu.VMEM),
                 out_shape=jax.ShapeDtypeStruct((1, ln), F32))(gathered)


def _adamw_flat(w, g, m, v):
    def body(w_ref, g_ref, m_ref, v_ref, d_ref, nm_ref, nv_ref):
        d, nm, nv = _adamw_math(w_ref[...], g_ref[...], m_ref[...], v_ref[...])
        d_ref[...] = d
        nm_ref[...] = nm
        nv_ref[...] = nv

    vm = pl.BlockSpec(memory_space=pltpu.VMEM)
    return _call(body, name="adamw_small", in_specs=[vm] * 4, out_specs=[vm] * 3,
                 out_shape=[jax.ShapeDtypeStruct(w.shape, F32)] * 3)(w, g, m, v)


def _from_col_shards(g):
    return jnp.transpose(g, (1, 0, 2)).reshape(g.shape[1], N_DEV * g.shape[2])


def _pad_lanes(v, n):
    return jnp.concatenate([v, jnp.zeros(v.shape[:-1] + (n - v.shape[-1],), v.dtype)], axis=-1)


def kernel(x, c, w_ada, b_ada, g_pre_mix, g_post_mix, w_in, b_fgate, w_gla_a2, b_gla_a2, g_fox_out, g_gla_out, w_out, g_pre_mlp, g_post_mlp, w_mlp_in, w_mlp_out, loss_target, m_w_ada, m_b_ada, m_g_pre_mix, m_g_post_mix, m_w_in, m_b_fgate, m_w_gla_a2, m_b_gla_a2, m_g_fox_out, m_g_gla_out, m_w_out, m_g_pre_mlp, m_g_post_mlp, m_w_mlp_in, m_w_mlp_out, v_w_ada, v_b_ada, v_g_pre_mix, v_g_post_mix, v_w_in, v_b_fgate, v_w_gla_a2, v_b_gla_a2, v_g_fox_out, v_g_gla_out, v_w_out, v_g_pre_mlp, v_g_post_mlp, v_w_mlp_in, v_w_mlp_out):
    rank = _my_rank()
    xs = x[0]
    s = xs.shape[0]
    target = loss_target[0]

    w_in_t, m_in_t, v_in_t = w_in[0].T, m_w_in[0].T, v_w_in[0].T
    c_all, wa2_g, ggla_g, win_g = _all_gather("gather_first", [c, w_gla_a2[0], g_gla_out[0], w_in_t.astype(BF16)])
    rest = [_own_slot("own_w_out", w_out[0], True, rank), _own_slot("own_w_mlp_in", w_mlp_in[0], True, rank)]
    gs_send, gs_sib, gs_ici, gs_land, gs_token = _gather2_start("gather_rest_start", rest, after=(c_all,))
    last = [_own_slot("own_w_mlp_out", w_mlp_out[0], True, rank)]
    gl_send, gl_sib, gl_ici, gl_land, gl_token = _gather2_start("gather_last_start", last, after=(gs_token,))
    w_a2 = _from_col_shards(wa2_g)
    g_gla = _from_col_shards(ggla_g).reshape(1, 1024)
    g_fox = g_fox_out.reshape(1, 1024)
    win_full = win_g.reshape(N_DEV * 771, D_MODEL)
    w_main = jnp.concatenate([win_full[:3072], win_full[3080:5128], win_full[5144:6168]], axis=0)
    w_small = jnp.concatenate([win_full[3072:3080], win_full[5128:5144],
                               jnp.zeros((W_SMALL - 24, D_MODEL), BF16)], axis=0)
    wa_pad =jnp.concatenate([jnp.zeros((8, 512), BF16), w_a2.astype(BF16), jnp.zeros((104, 512), BF16)], axis=0)
    bf_vec = _pad_lanes(b_fgate, W_SMALL)

    mod_part, c_act = _modulation(c_all.reshape(N_DEV, D_MODEL), w_ada[0])
    (mod_g,) = _all_gather("gather_mod", [mod_part])
    mod = lax.dynamic_slice_in_dim(mod_g, rank, 1, axis=1).reshape(1, 6 * D_MODEL) + b_ada
    shift_m, scale_m, gate_m, shift_f, scale_f, gate_f = [mod[:, i * D_MODEL:(i + 1) * D_MODEL] for i in range(6)]

    h = _premix(xs, g_pre_mix, scale_m, shift_m, deps=(gl_token,))
    pm = _mm_plain("proj_main", h, w_main, NT, BF16)
    small = _mm_plain("proj_small", h, w_small, NT, F32)
    crow = _fox_cum(small, bf_vec).reshape(FOX_HEADS, 1, s)
    o_fox, lse = _fox_fwd(pm, crow)
    gs_fsend, gs_frecv, gs_land, gs_ftoken = _gather2_forward("gather_rest_forward", gs_land, gs_ici, o_fox)
    o_gla, states = _gla_fwd(pm, small, wa_pad, b_gla_a2, deps=(gs_ftoken,))
    mix = _mix_fwd(o_fox, o_gla, pm, g_fox, g_gla)
    wout_g, wmi_g = _gather2_wait("gather_rest_wait", gs_land, gs_send, gs_sib, gs_fsend, gs_frecv, mix)
    w_out_full = wout_g.reshape(D_MODEL, D_MODEL)
    y = _mm_plain("out_proj", mix, w_out_full, NN, BF16)
    x1, h2 = _postmix_premlp(xs, y, gate_m, g_post_mix, g_pre_mlp, scale_f, shift_f)
    gl_fsend, gl_frecv, gl_land, gl_ftoken = _gather2_forward("gather_last_forward", gl_land, gl_ici, h2)

    tm, tn, tk = _tile(s, 1024), 1024, 2048
    nsh = 1024 // tn

    def relu2(acc):
        rl = jnp.maximum(acc, 0.0)
        return rl * rl, rl

    z, a_relu = _matmul(
        "mlp_in", h2, wmi_g, contract=NN, grid=(s // tm, D_FF // tn, D_MODEL // tk),
        a_spec=pl.BlockSpec((tm, tk), lambda i, j, k: (i, k)),
        b_spec=pl.BlockSpec((None, tk, tn), lambda i, j, k: (j // nsh, k, j % nsh)),
        out_specs=[pl.BlockSpec((tm, tn), lambda i, j, k: (i, j))] * 2,
        out_shapes=[jax.ShapeDtypeStruct((s, D_FF), BF16)] * 2, acc_shape=(tm, tn), epilogue=relu2,
        deps=(gl_ftoken,))
    (wmo_g,) = _gather2_wait("gather_last_wait", gl_land, gl_send, gl_sib, gl_fsend, gl_frecv, z)
    w_mo_full = wmo_g.reshape(D_FF, D_MODEL)
    y2 = _mm_plain("mlp_out", z, w_mo_full, NN, BF16)

    dx2, dy2, loss_vec, dgate_f, dg_post_mlp = _loss_postmlp_bwd(x1, y2, target, gate_f, g_post_mlp)
    loss = lax.psum(loss_vec[0, 0], ("x", "y", "c"))

    da = _mm_plain("mlp_out_dx", dy2, w_mo_full, NT, BF16, extra=(a_relu,),
                   epilogue=lambda acc, rl: (acc * (2.0 * rl.astype(F32)),))
    dw_mo = _mm_plain("mlp_out_dw", z, dy2, TN, BF16)
    dw_mo = dw_mo.reshape(N_DEV, 1024, D_MODEL)
    x_mo = _exchange_start("grad_mlp_out_start", [_own_slot("own_dw_mlp_out", dw_mo, False, rank)], [dw_mo])
    tkx = 2048
    (dh2,) = _matmul(
        "mlp_in_dx", da, wmi_g, contract=NT, grid=(s // tm, D_MODEL // tn, D_FF // tkx),
        a_spec=pl.BlockSpec((tm, tkx), lambda i, j, k: (i, k)),
        b_spec=pl.BlockSpec((tkx // 1024, tn, 1024), lambda i, j, k: (k, j, 0)),
        out_specs=[pl.BlockSpec((tm, tn), lambda i, j, k: (i, j))],
        out_shapes=[jax.ShapeDtypeStruct((s, D_MODEL), BF16)], acc_shape=(tm, tn), deps=(x_mo[4],))
    ts = _tile(s, 2048)
    (dw_mi,) = _matmul(
        "mlp_in_dw", h2, da, contract=TN, grid=(D_MODEL // 1024, D_FF // tn, s // ts),
        a_spec=pl.BlockSpec((ts, 1024), lambda i, j, k: (k, i)),
        b_spec=pl.BlockSpec((ts, tn), lambda i, j, k: (k, j)),
        out_specs=[pl.BlockSpec((None, 1024, tn), lambda i, j, k: (j // nsh, i, j % nsh))],
        out_shapes=[jax.ShapeDtypeStruct((N_DEV, D_MODEL, 1024), BF16)], acc_shape=(1024, tn))
    x_mi = _exchange_start("grad_mlp_in_start", [_own_slot("own_dw_mlp_in", dw_mi, False, rank)], [dw_mi])

    dx1, dy, dscale_f, dshift_f, dg_pre_mlp, dgate_m, dg_post_mix = _premlp_postmix_bwd(
        dh2, dx2, x1, y, scale_f, g_pre_mlp, gate_m, g_post_mix, deps=(x_mi[4],))

    dmix = _mm_plain("out_proj_dx", dy, w_out_full, NT, BF16)
    dw_out = _mm_plain("out_proj_dw", mix, dy, TN, BF16)
    dw_out = dw_out.reshape(N_DEV, 256, D_MODEL)
    x_out = _exchange_start("grad_out_start", [_own_slot("own_dw_out", dw_out, False, rank)], [dw_out])
    do_fox, do_gla, dgr, dg_fox, dg_gla = _mix_bwd(dmix, o_fox, o_gla, pm, g_fox, g_gla, deps=(x_out[4],))

    dq, dk, dv, dc, dcq = _fox_bwd(pm, crow, o_fox, lse, do_fox)
    dsmall_f, db_f = _fox_cum_bwd(dc.reshape(FOX_HEADS, s), dcq, small, bf_vec)
    dgq, dgk, dgv, dza, db_a2 = _gla_bwd(pm, small, wa_pad, b_gla_a2, states, do_gla)
    dsmall = _mm_plain("gate_dx", dza, wa_pad, NT, F32, tn=128, extra=(dsmall_f,),
                       epilogue=lambda acc, other: (acc + other,))
    dwa_pad = _mm_plain("gate_dw", small, dza, TN, F32, tm=128, tn=512)

    dpm = jnp.concatenate([dq.astype(BF16), dk.astype(BF16), dv.astype(BF16), dgq.astype(BF16), dgk.astype(BF16),
                           dgv.astype(BF16), dgr], axis=1)
    dw_main = _mm_plain("proj_main_dw", dpm, h, TN, BF16)
    dw_small = _mm_plain("proj_small_dw", dsmall, h, TN, BF16, tm=128)
    dwin_full = jnp.concatenate([dw_main[:3072], dw_small[0:8], dw_main[3072:5120], dw_small[8:24],
                                 dw_main[5120:6144]], axis=0)
    dwin_slabs = dwin_full.reshape(N_DEV, 771, D_MODEL)
    x_in = _exchange_start("grad_in_start", [_own_slot("own_dw_in", dwin_slabs, False, rank)], [dwin_slabs])
    tmx = _tile(s, 1024)
    (dh,) = _matmul(
        "proj_main_dx", dpm, w_main, contract=NN, grid=(s // tmx, D_MODEL // 1024, W_MAIN // 2048),
        a_spec=pl.BlockSpec((tmx, 2048), lambda i, j, k: (i, k)),
        b_spec=pl.BlockSpec((2048, 1024), lambda i, j, k: (k, j)),
        out_specs=[pl.BlockSpec((tmx, 1024), lambda i, j, k: (i, j))],
        out_shapes=[jax.ShapeDtypeStruct((s, D_MODEL), BF16)], acc_shape=(tmx, 1024),
        extra=(dsmall, w_small),
        extra_specs=[pl.BlockSpec((tmx, W_SMALL), lambda i, j, k: (i, 0)),
                     pl.BlockSpec((W_SMALL, 1024), lambda i, j, k: (0, j))],
        epilogue=lambda acc, dsm, wsm: (acc + jnp.dot(dsm.astype(BF16), wsm, preferred_element_type=F32),),
        deps=(x_in[4],))
    grad_x, dscale_m, dshift_m, dg_pre_mix = _premix_bwd(dh, dx1, xs, g_pre_mix, scale_m)

    dmod = jnp.concatenate([dshift_m, dscale_m, dgate_m, dshift_f, dscale_f, dgate_f], axis=1)
    flat = jnp.concatenate(
        [dmod, dg_pre_mix, dg_post_mix, dg_fox, dg_pre_mlp, dg_post_mlp, db_a2,
         dwa_pad[8:24, :].reshape(1, GLA_RANK * 512), dg_gla, _pad_lanes(db_f[:, 0].reshape(1, FOX_HEADS), 128)],
        axis=1)

    (r_mo,) = _exchange_wait("grad_mlp_out_wait", *x_mo[:4], grad_x)
    g_mo, d_mo, nm_mo, nv_mo = _adamw_slabs("adamw_w_mlp_out", w_mlp_out[0], r_mo, m_w_mlp_out[0], v_w_mlp_out[0])
    (r_mi,) = _exchange_wait("grad_mlp_in_wait", *x_mi[:4], g_mo)
    g_mi, d_mi, nm_mi, nv_mi = _adamw_slabs("adamw_w_mlp_in", w_mlp_in[0], r_mi, m_w_mlp_in[0], v_w_mlp_in[0])
    (r_out,) = _exchange_wait("grad_out_wait", *x_out[:4], g_mi)
    g_out, d_out, nm_out, nv_out = _adamw_slabs("adamw_w_out", w_out[0], r_out, m_w_out[0], v_w_out[0])

    (flat_g,) = _all_gather("gather_small_grads", [flat], deps=(g_out,))
    tot = _sum_devices(flat_g)
    dm_cols = lax.dynamic_slice_in_dim(flat_g[:, 0, :6 * D_MODEL], rank * 1536, 1536, axis=1)
    (r_in,) = _exchange_wait("grad_in_wait", *x_in[:4], tot)
    in_t = _adamw_slabs("adamw_w_in", w_in_t, r_in, m_in_t, v_in_t)
    g_in, d_in, nm_in, nv_in = [a.T for a in in_t]
    g_ada, d_ada, nm_ada, nv_ada = _adamw_ada(w_ada[0], c_act.T, dm_cols, m_w_ada[0], v_w_ada[0], deps=(in_t[0],))

    o = 0
    seg = {}
    for name, n in (("b_ada", 12288), ("g_pre_mix", 2048), ("g_post_mix", 2048), ("g_fox_out", 1024),
                    ("g_pre_mlp", 2048), ("g_post_mlp", 2048), ("b_gla_a2", 512), ("w_gla_a2", 8192),
                    ("g_gla_out", 1024), ("b_fgate", 128)):
        seg[name] = tot[:, o:o + n]
        o += n
    g_wa2 = lax.dynamic_slice_in_dim(seg["w_gla_a2"].reshape(GLA_RANK, 512), rank * 64, 64, axis=1)
    g_ggla = lax.dynamic_slice_in_dim(seg["g_gla_out"].reshape(GLA_HEADS, GLA_DV), rank * 32, 32, axis=1)
    small_names = ["b_ada", "g_pre_mix", "g_post_mix", "g_fox_out", "g_pre_mlp", "g_post_mlp", "b_gla_a2",
                   "w_gla_a2", "g_gla_out", "b_fgate"]
    small_grads = {**seg, "w_gla_a2": g_wa2.reshape(1, 1024), "g_gla_out": g_ggla.reshape(1, 128)}
    weights = dict(b_ada=b_ada, g_pre_mix=g_pre_mix, g_post_mix=g_post_mix, g_fox_out=g_fox_out,
                   g_pre_mlp=g_pre_mlp, g_post_mlp=g_post_mlp, b_gla_a2=b_gla_a2, w_gla_a2=w_gla_a2,
                   g_gla_out=g_gla_out, b_fgate=b_fgate)
    moms = dict(b_ada=m_b_ada, g_pre_mix=m_g_pre_mix, g_post_mix=m_g_post_mix, g_fox_out=m_g_fox_out,
                g_pre_mlp=m_g_pre_mlp, g_post_mlp=m_g_post_mlp, b_gla_a2=m_b_gla_a2, w_gla_a2=m_w_gla_a2,
                g_gla_out=m_g_gla_out, b_fgate=m_b_fgate)
    vels = dict(b_ada=v_b_ada, g_pre_mix=v_g_pre_mix, g_post_mix=v_g_post_mix, g_fox_out=v_g_fox_out,
                g_pre_mlp=v_g_pre_mlp, g_post_mlp=v_g_post_mlp, b_gla_a2=v_b_gla_a2, w_gla_a2=v_w_gla_a2,
                g_gla_out=v_g_gla_out, b_fgate=v_b_fgate)

    def flatten(d, fill):
        parts = []
        for nm in small_names:
            p = d[nm].reshape(1, -1)
            if nm == "b_fgate":
                p = jnp.concatenate([p[:, :FOX_HEADS], jnp.full((1, 128 - FOX_HEADS), fill, F32)], axis=1)
            parts.append(p)
        return jnp.concatenate(parts, axis=1).reshape(-1, 128)

    fw, fg, fm, fv = flatten(weights, 0.0), flatten(small_grads, 0.0), flatten(moms, 0.0), flatten(vels, 1.0)
    fd, fnm, fnv = _adamw_flat(fw, fg, fm, fv)

    def unflatten(fl):
        fl = fl.reshape(1, -1)
        out = {}
        o = 0
        for nm in small_names:
            n = 128 if nm == "b_fgate" else weights[nm].size
            piece = fl[:, o:o + n]
            if nm == "b_fgate":
                piece = piece[:, :FOX_HEADS]
            out[nm] = piece.reshape(weights[nm].shape)
            o += n
        return out

    sg, sd, snm, snv = unflatten(fg), unflatten(fd), unflatten(fnm), unflatten(fnv)

    big = dict(w_ada=(g_ada, d_ada, nm_ada, nv_ada), w_in=(g_in, d_in, nm_in, nv_in),
               w_out=(g_out, d_out, nm_out, nv_out), w_mlp_in=(g_mi, d_mi, nm_mi, nv_mi),
               w_mlp_out=(g_mo, d_mo, nm_mo, nv_mo))
    order = ["w_ada", "b_ada", "g_pre_mix", "g_post_mix", "w_in", "b_fgate", "w_gla_a2", "b_gla_a2", "g_fox_out",
             "g_gla_out", "w_out", "g_pre_mlp", "g_post_mlp", "w_mlp_in", "w_mlp_out"]

    def pick(nm, idx):
        if nm in big:
            return big[nm][idx][None]
        return (sg, sd, snm, snv)[idx][nm]

    grads = [pick(nm, 0) for nm in order]
    deltas = [pick(nm, 1) for nm in order]
    new_m = [pick(nm, 2) for nm in order]
    new_v = [pick(nm, 3) for nm in order]
    return (loss, grad_x[None], *grads, *deltas, *new_m, *new_v)
```

```python
import functools

import numpy as np
import jax
import jax.numpy as jnp
from jax import lax
from jax.experimental import pallas as pl
from jax.experimental.pallas import tpu as pltpu

F32 = jnp.float32
BF16 = jnp.bfloat16
MESH = pl.DeviceIdType.MESH
N_DEV = 8

D_MODEL = 2048
FOX_HEADS = 8
FOX_HEAD_DIM = 128
GLA_HEADS = 4
GLA_DK = 128
GLA_DV = 256
GLA_RANK = 16
GLA_TEMP = 16.0
CHUNK = 64
D_FF = 8192
W_MAIN = 6144
W_SMALL = 128
EPS = 1e-6
NEG = float(np.finfo(np.float32).min)

ADAM_LR = 0.001
ADAM_B1 = 0.9
ADAM_B2 = 0.999
ADAM_EPS = 1e-08
ADAM_WD = 0.01
ADAM_STEP = 10

ROW_T = 512
FOX_T = 1024
GLA_R = 512
CUM_T = 256
VMEM_LIMIT = 56 * 1024 * 1024


def _call(body, deps=(), **kw):
    if not deps:
        return pl.pallas_call(body, **kw)
    n_in, n_dep = len(kw["in_specs"]), len(deps)

    def with_deps(*refs):
        return body(*refs[:n_in], *refs[n_in + n_dep:])

    kw["in_specs"] = [*kw["in_specs"], *[pl.BlockSpec(memory_space=pl.ANY)] * n_dep]
    call = pl.pallas_call(with_deps, **kw)
    return lambda *args: call(*args, *deps)


def _params(sem=None):
    return pltpu.CompilerParams(dimension_semantics=sem, vmem_limit_bytes=VMEM_LIMIT)


def _my_pos():
    return lax.axis_index("x"), lax.axis_index("y"), lax.axis_index("c")


def _my_rank():
    x, y, c = _my_pos()
    return 4 * x + 2 * y + c


def _all_gather(name, arrays, deps=()):
    n = len(arrays)

    def body(*refs):
        ins = refs[:n]
        outs = refs[n:2 * n]
        send_sems, recv_sems, local_sems = refs[2 * n:]
        x, y, c = _my_pos()
        me, sibling = (x, y, c), (x, y, 1 - c)
        chips = [(1 - x, y), (x, 1 - y), (1 - x, 1 - y)]

        def slot(a, px, py, pc):
            return outs[a].at[4 * px + 2 * py + pc]

        def copy(a, k, block, to, src=None):
            return pltpu.make_async_remote_copy(
                src_ref=slot(a, *block) if src is None else src, dst_ref=slot(a, *block),
                send_sem=send_sems.at[a, k], recv_sem=recv_sems.at[a, k],
                device_id=to, device_id_type=MESH)

        started = []
        for a in range(n):
            mine = pltpu.make_async_copy(ins[a], slot(a, *me), local_sems.at[a])
            mine.start()
            started.append(mine)
        first = []
        for a in range(n):
            first.append(copy(a, 0, me, sibling, src=ins[a]))
            first += [copy(a, 1 + j, me, (*chip, c), src=ins[a]) for j, chip in enumerate(chips)]
        for cp in first:
            cp.start()
        passed = []
        for j, chip in enumerate(chips):
            for a in range(n):
                copy(a, 1 + j, (*chip, c), me).wait_recv()
                fwd = copy(a, 4 + j, (*chip, c), sibling)
                fwd.start()
                passed.append(fwd)
        for a in range(n):
            copy(a, 0, sibling, me).wait_recv()
            for j, chip in enumerate(chips):
                copy(a, 4 + j, (*chip, 1 - c), me).wait_recv()
        for cp in first + passed:
            cp.wait_send()
        for mine in started:
            mine.wait()

    hbm = pl.BlockSpec(memory_space=pltpu.HBM)
    return _call(
        body, deps=deps, name=name,
        out_shape=[jax.ShapeDtypeStruct((N_DEV,) + a.shape, a.dtype) for a in arrays],
        in_specs=[hbm] * n, out_specs=[hbm] * n,
        scratch_shapes=[pltpu.SemaphoreType.DMA((n, 7)), pltpu.SemaphoreType.DMA((n, 7)),
                        pltpu.SemaphoreType.DMA((n,))],
    )(*arrays)


def _plane_tiles(rr, cc, tr=512, tc=512):
    if rr % 8 == 0:
        tr = _tile(rr, tr)
        return (rr // tr, pl.BlockSpec((tr, cc), lambda i: (i, 0)),
                pl.BlockSpec((N_DEV, tr, cc), lambda i: (0, i, 0)))
    tc = _tile(cc, tc)
    return (cc // tc, pl.BlockSpec((rr, tc), lambda i: (0, i)),
            pl.BlockSpec((N_DEV, rr, tc), lambda i: (0, 0, i)))


def _own_slot(name, src, gather, rank):
    shape = ((N_DEV,) + src.shape) if gather else src.shape
    rr, cc = shape[1], shape[2]
    by_rows = rr % 8 == 0
    tr, tc = (_tile(rr, 512), cc) if by_rows else (rr, _tile(cc, 512))
    steps = rr // tr if by_rows else cc // tc

    def body(rank_ref, s_ref, o_ref):
        o_ref[...] = s_ref[...].astype(o_ref.dtype)

    def at(i):
        return (i, 0) if by_rows else (0, i)

    if gather:
        in_spec = pl.BlockSpec((tr, tc), lambda i, rk: at(i))
    else:
        in_spec = pl.BlockSpec((None, tr, tc), lambda i, rk: (rk[0], *at(i)))
    grid_spec = pltpu.PrefetchScalarGridSpec(
        num_scalar_prefetch=1, grid=(steps,), in_specs=[in_spec],
        out_specs=pl.BlockSpec((None, tr, tc), lambda i, rk: (rk[0], *at(i))))
    return _call(body, name=name, grid_spec=grid_spec, out_shape=jax.ShapeDtypeStruct(shape, BF16),
                 compiler_params=_params(("arbitrary",)))(jnp.reshape(rank, (1,)).astype(jnp.int32), src)


_HBM = pl.BlockSpec(memory_space=pltpu.HBM)
_SEM = pl.BlockSpec(memory_space=pltpu.SEMAPHORE)
_FLIPS = [(kx, ky, kc) for kx in (0, 1) for ky in (0, 1) for kc in (0, 1)][1:]


def _peers():
    x, y, c = _my_pos()
    out = []
    for kx, ky, kc in _FLIPS:
        px, py, pc = (1 - x if kx else x), (1 - y if ky else y), (1 - c if kc else c)
        out.append(((px, py, pc), 4 * px + 2 * py + pc))
    return out


def _exchange_copy(srcs, lands, send_sems, recv_sems, a, k, peer, peer_rank, slot):
    return pltpu.make_async_remote_copy(
        src_ref=lands[a].at[slot] if srcs is None else srcs[a].at[peer_rank],
        dst_ref=lands[a].at[slot],
        send_sem=send_sems[a].at[k], recv_sem=recv_sems[a].at[k],
        device_id=peer, device_id_type=MESH)


def _exchange_start(name, lands, srcs=None, after=()):
    n = len(lands)
    n_src = 0 if srcs is None else n
    n_in = n + n_src + len(after)

    def body(*refs):
        lnd = refs[:n]
        src = None if srcs is None else refs[n:2 * n]
        send_sems, recv_sems = refs[n_in:n_in + n], refs[n_in + n:n_in + 2 * n]
        token = refs[-1]
        me = _my_rank()
        for a in range(n):
            for k, (peer, peer_rank) in enumerate(_peers()):
                _exchange_copy(src, lnd, send_sems, recv_sems, a, k, peer, peer_rank, me).start()
        token[...] = jnp.zeros_like(token)

    sems = [pltpu.SemaphoreType.DMA((7,))] * (2 * n)
    thru = list(lands) + ([] if srcs is None else list(srcs))
    outs = pl.pallas_call(
        body, name=name,
        out_shape=(*sems, *[pltpu.HBM(t.shape, t.dtype) for t in thru], jax.ShapeDtypeStruct((8, 128), F32)),
        in_specs=[*[_HBM] * len(thru), *[pl.BlockSpec(memory_space=pl.ANY)] * len(after)],
        out_specs=(*[_SEM] * (2 * n), *[_HBM] * len(thru), pl.BlockSpec(memory_space=pltpu.VMEM)),
        input_output_aliases={i: 2 * n + i for i in range(len(thru))},
        compiler_params=pltpu.CompilerParams(has_side_effects=pltpu.SideEffectType.DATAFLOW_SIDE_EFFECTING),
    )(*[pltpu.with_memory_space_constraint(t, pltpu.HBM) for t in thru], *after)
    lands_thru = outs[2 * n:3 * n]
    srcs_thru = None if srcs is None else outs[3 * n:4 * n]
    return outs[:n], outs[n:2 * n], srcs_thru, lands_thru, outs[-1]


def _exchange_wait(name, send_sems, recv_sems, srcs, lands, after):
    n = len(lands)
    thru = list(lands) + ([] if srcs is None else list(srcs))

    def body(*refs):
        lnd = refs[:n]
        src = None if srcs is None else refs[n:2 * n]
        ssem, rsem = refs[len(thru):len(thru) + n], refs[len(thru) + n:len(thru) + 2 * n]
        for a in range(n):
            for k, (peer, peer_rank) in enumerate(_peers()):
                cp = _exchange_copy(src, lnd, ssem, rsem, a, k, peer, peer_rank, peer_rank)
                cp.wait_send()
                cp.wait_recv()

    outs = pl.pallas_call(
        body, name=name,
        out_shape=tuple(pltpu.HBM(t.shape, t.dtype) for t in thru),
        in_specs=[*[_HBM] * len(thru), *[_SEM] * (2 * n), pl.BlockSpec(memory_space=pl.ANY)],
        out_specs=tuple([_HBM] * len(thru)),
        input_output_aliases={i: i for i in range(len(thru))},
        compiler_params=pltpu.CompilerParams(has_side_effects=pltpu.SideEffectType.DATAFLOW_SIDE_EFFECTING),
    )(*thru, *send_sems, *recv_sems, after)
    return outs[:n]


_SIDE = pltpu.CompilerParams(has_side_effects=pltpu.SideEffectType.DATAFLOW_SIDE_EFFECTING)
_ANY = pl.BlockSpec(memory_space=pl.ANY)


def _chips():
    x, y, _ = _my_pos()
    return [(1 - x, y), (x, 1 - y), (1 - x, 1 - y)]


def _slot_copy(lnd, slot, send_sem, recv_sem, to):
    return pltpu.make_async_remote_copy(src_ref=lnd.at[slot], dst_ref=lnd.at[slot], send_sem=send_sem,
                                        recv_sem=recv_sem, device_id=to, device_id_type=MESH)


def _gather2_start(name, lands, after=()):
    n = len(lands)
    n_in = n + len(after)

    def body(*refs):
        lnd = refs[:n]
        send, recv_sib, recv_ici = refs[n_in:n_in + n], refs[n_in + n:n_in + 2 * n], refs[n_in + 2 * n:n_in + 3 * n]
        x, y, c = _my_pos()
        me = 4 * x + 2 * y + c
        for a in range(n):
            _slot_copy(lnd[a], me, send[a].at[0], recv_sib[a].at[0], (x, y, 1 - c)).start()
            for j, chip in enumerate(_chips()):
                _slot_copy(lnd[a], me, send[a].at[1 + j], recv_ici[a].at[j], (*chip, c)).start()
        refs[-1][...] = jnp.zeros_like(refs[-1])

    sems = [pltpu.SemaphoreType.DMA((4,))] * n + [pltpu.SemaphoreType.DMA((1,))] * n + [pltpu.SemaphoreType.DMA((3,))] * n
    outs = pl.pallas_call(
        body, name=name,
        out_shape=(*sems, *[pltpu.HBM(t.shape, t.dtype) for t in lands], jax.ShapeDtypeStruct((8, 128), F32)),
        in_specs=[*[_HBM] * n, *[_ANY] * len(after)],
        out_specs=(*[_SEM] * (3 * n), *[_HBM] * n, pl.BlockSpec(memory_space=pltpu.VMEM)),
        input_output_aliases={i: 3 * n + i for i in range(n)}, compiler_params=_SIDE,
    )(*[pltpu.with_memory_space_constraint(t, pltpu.HBM) for t in lands], *after)
    return outs[:n], outs[n:2 * n], outs[2 * n:3 * n], outs[3 * n:4 * n], outs[-1]


def _gather2_forward(name, lands, recv_ici, after):
    n = len(lands)

    def body(*refs):
        lnd, arrived = refs[:n], refs[n:2 * n]
        send, recv = refs[2 * n + 1:3 * n + 1], refs[3 * n + 1:4 * n + 1]
        x, y, c = _my_pos()
        for j, (cx, cy) in enumerate(_chips()):
            slot = 4 * cx + 2 * cy + c
            for a in range(n):
                _slot_copy(lnd[a], slot, send[a].at[j], arrived[a].at[j], (cx, cy, c)).wait_recv()
                _slot_copy(lnd[a], slot, send[a].at[j], recv[a].at[j], (x, y, 1 - c)).start()
        refs[-1][...] = jnp.zeros_like(refs[-1])

    sems = [pltpu.SemaphoreType.DMA((3,))] * (2 * n)
    outs = pl.pallas_call(
        body, name=name,
        out_shape=(*sems, *[pltpu.HBM(t.shape, t.dtype) for t in lands], jax.ShapeDtypeStruct((8, 128), F32)),
        in_specs=[*[_HBM] * n, *[_SEM] * n, _ANY],
        out_specs=(*[_SEM] * (2 * n), *[_HBM] * n, pl.BlockSpec(memory_space=pltpu.VMEM)),
        input_output_aliases={i: 2 * n + i for i in range(n)}, compiler_params=_SIDE,
    )(*lands, *recv_ici, after)
    return outs[:n], outs[n:2 * n], outs[2 * n:3 * n], outs[-1]


def _gather2_wait(name, lands, send_a, recv_sib, send_b, recv_b, after):
    n = len(lands)

    def body(*refs):
        lnd = refs[:n]
        sa, rs, sb, rb = (refs[(1 + i) * n:(2 + i) * n] for i in range(4))
        x, y, c = _my_pos()
        me = 4 * x + 2 * y + c
        for a in range(n):
            for k in range(4):
                _slot_copy(lnd[a], me, sa[a].at[k], rs[a].at[0], (x, y, 1 - c)).wait_send()
            _slot_copy(lnd[a], me - c + (1 - c), sa[a].at[0], rs[a].at[0], (x, y, 1 - c)).wait_recv()
            for j, (cx, cy) in enumerate(_chips()):
                _slot_copy(lnd[a], 4 * cx + 2 * cy + c, sb[a].at[j], rb[a].at[j], (x, y, 1 - c)).wait_send()
                _slot_copy(lnd[a], 4 * cx + 2 * cy + (1 - c), sb[a].at[j], rb[a].at[j], (x, y, 1 - c)).wait_recv()

    outs = pl.pallas_call(
        body, name=name,
        out_shape=tuple(pltpu.HBM(t.shape, t.dtype) for t in lands),
        in_specs=[*[_HBM] * n, *[_SEM] * (4 * n), _ANY],
        out_specs=tuple([_HBM] * n),
        input_output_aliases={i: i for i in range(n)}, compiler_params=_SIDE,
    )(*lands, *send_a, *recv_sib, *send_b, *recv_b, after)
    return outs


NN = ((1,), (0,))
NT = ((1,), (1,))
TN = ((0,), (0,))


def _matmul(name, a, b, *, contract, grid, a_spec, b_spec, out_specs, out_shapes, acc_shape,
            extra=(), extra_specs=(), epilogue=None, deps=()):
    nk = grid[2]
    n_extra = len(extra)
    n_out = len(out_shapes)

    def body(*refs):
        a_ref, b_ref = refs[0], refs[1]
        extra_refs = refs[2:2 + n_extra]
        out_refs = refs[2 + n_extra:2 + n_extra + n_out]
        acc_ref = refs[-1]
        k = pl.program_id(2)

        def prod():
            if len(b_ref.shape) == 2:
                return lax.dot_general(a_ref[...].astype(BF16), b_ref[...].astype(BF16), (contract, ((), ())),
                                       preferred_element_type=F32)
            kk = a_ref.shape[1] // b_ref.shape[0]
            acc = None
            for u in range(b_ref.shape[0]):
                part = lax.dot_general(a_ref[:, u * kk:(u + 1) * kk].astype(BF16), b_ref[u].astype(BF16),
                                       (contract, ((), ())), preferred_element_type=F32)
                acc = part if acc is None else acc + part
            return acc

        def finish(acc):
            res = (acc,) if epilogue is None else epilogue(acc, *[r[...] for r in extra_refs])
            for o_ref, val in zip(out_refs, res):
                o_ref[...] = val.astype(o_ref.dtype)

        if nk == 1:
            finish(prod())
            return

        @pl.when(k == 0)
        def _():
            acc_ref[...] = prod()

        @pl.when((k > 0) & (k < nk - 1))
        def _():
            acc_ref[...] += prod()

        @pl.when(k == nk - 1)
        def _():
            finish(acc_ref[...] + prod())

    outs = _call(
        body, deps=deps, name=name, grid=grid,
        in_specs=[a_spec, b_spec, *extra_specs], out_specs=list(out_specs), out_shape=list(out_shapes),
        scratch_shapes=[pltpu.VMEM(acc_shape if nk > 1 else (8, 128), F32)],
        compiler_params=_params(("parallel", "parallel", "arbitrary")),
    )(a, b, *extra)
    return outs


def _tile(n, t):
    t = min(n, t)
    assert n % t == 0, (n, t)
    return t


def _mm_plain(name, a, b, contract, out_dtype, tm=1024, tn=1024, tk=2048, extra=(), epilogue=None,
              n_out=1, out_dtypes=None, deps=()):
    if contract == NN:
        (m, kd), (_, n) = a.shape, b.shape
    elif contract == NT:
        (m, kd), (n, _) = a.shape, b.shape
    else:
        (kd, m), (_, n) = a.shape, b.shape
    tm, tn, tk = _tile(m, tm), _tile(n, tn), _tile(kd, tk)
    if contract == NN:
        a_spec = pl.BlockSpec((tm, tk), lambda i, j, k: (i, k))
        b_spec = pl.BlockSpec((tk, tn), lambda i, j, k: (k, j))
    elif contract == NT:
        a_spec = pl.BlockSpec((tm, tk), lambda i, j, k: (i, k))
        b_spec = pl.BlockSpec((tn, tk), lambda i, j, k: (j, k))
    else:
        a_spec = pl.BlockSpec((tk, tm), lambda i, j, k: (k, i))
        b_spec = pl.BlockSpec((tk, tn), lambda i, j, k: (k, j))
    o_spec = pl.BlockSpec((tm, tn), lambda i, j, k: (i, j))
    out_dtypes = out_dtypes or [out_dtype] * n_out
    outs = _matmul(
        name, a, b, contract=contract, grid=(m // tm, n // tn, kd // tk), a_spec=a_spec, b_spec=b_spec,
        out_specs=[o_spec] * len(out_dtypes), out_shapes=[jax.ShapeDtypeStruct((m, n), dt) for dt in out_dtypes],
        acc_shape=(tm, tn), extra=extra, extra_specs=[o_spec] * len(extra), epilogue=epilogue, deps=deps)
    return outs[0] if len(out_dtypes) == 1 else outs


def _rows_call(name, body, row_in, vec_in, row_out, vec_out, s, deps=()):
    t = _tile(s, ROW_T)
    in_specs = []
    args = []
    for arr, width, cb in row_in:
        in_specs.append(pl.BlockSpec((t, width), functools.partial(lambda i, cb: (i, cb), cb=cb)))
        args.append(arr)
    for v in vec_in:
        in_specs.append(pl.BlockSpec(v.shape, lambda i: (0, 0)))
        args.append(v)
    out_specs = []
    out_shapes = []
    for width, dt in row_out:
        out_specs.append(pl.BlockSpec((t, width), lambda i: (i, 0)))
        out_shapes.append(jax.ShapeDtypeStruct((s, width), dt))
    for width in vec_out:
        out_specs.append(pl.BlockSpec((1, width), lambda i: (0, 0)))
        out_shapes.append(jax.ShapeDtypeStruct((1, width), F32))
    return _call(body, deps=deps, name=name, grid=(s // t,), in_specs=in_specs, out_specs=out_specs,
                 out_shape=out_shapes, compiler_params=_params(("arbitrary",)))(*args)


def _acc_vec(ref, val):
    _acc_row(ref, jnp.sum(val, axis=0, keepdims=True))


def _acc_row(ref, part):
    @pl.when(pl.program_id(0) == 0)
    def _():
        ref[...] = part

    @pl.when(pl.program_id(0) > 0)
    def _():
        ref[...] += part


def _rms(v):
    return lax.rsqrt(jnp.mean(v * v, axis=-1, keepdims=True) + EPS)


def _norm_bwd(dxn, xn, r):
    return r * (dxn - xn * jnp.mean(dxn * xn, axis=-1, keepdims=True))


def _premix(x, g, scale, shift, deps=()):
    s = x.shape[0]

    def body(x_ref, g_ref, sc_ref, sh_ref, h_ref):
        xv = x_ref[...]
        h_ref[...] = ((xv * _rms(xv) * g_ref[...]) * (1.0 + sc_ref[...]) + sh_ref[...]).astype(BF16)

    return _rows_call("premix", body, [(x, D_MODEL, 0)], [g, scale, shift], [(D_MODEL, BF16)], [], s, deps)[0]


def _sigmoid(z):
    return 1.0 / (1.0 + jnp.exp(-z))


def _mix_fwd(o_fox, o_gla, pm, g_fox, g_gla):
    s = o_fox.shape[0]

    def body(of_ref, og_ref, gr_ref, gf_ref, gg_ref, mix_ref):
        for h in range(FOX_HEADS):
            sl = slice(h * FOX_HEAD_DIM, (h + 1) * FOX_HEAD_DIM)
            seg = of_ref[:, sl]
            mix_ref[:, sl] = (seg * _rms(seg) * gf_ref[:, sl]).astype(BF16)
        for h in range(GLA_HEADS):
            sl = slice(h * GLA_DV, (h + 1) * GLA_DV)
            seg = og_ref[:, sl]
            gr = gr_ref[:, sl].astype(F32)
            val = (seg * _rms(seg) * gg_ref[:, sl]) * (gr * _sigmoid(gr))
            mix_ref[:, pl.ds(FOX_HEADS * FOX_HEAD_DIM + h * GLA_DV, GLA_DV)] = val.astype(BF16)

    return _rows_call("mix_fwd", body, [(o_fox, 1024, 0), (o_gla, 1024, 0), (pm, 1024, 5)], [g_fox, g_gla],
                      [(D_MODEL, BF16)], [], s)[0]


def _mix_bwd(dmix, o_fox, o_gla, pm, g_fox, g_gla, deps=()):
    s = o_fox.shape[0]

    def body(dm_ref, of_ref, og_ref, gr_ref, gf_ref, gg_ref, dof_ref, dog_ref, dgr_ref, dgf_ref, dgg_ref):
        dgf = []
        for h in range(FOX_HEADS):
            sl = slice(h * FOX_HEAD_DIM, (h + 1) * FOX_HEAD_DIM)
            seg = of_ref[:, sl]
            r = _rms(seg)
            segn = seg * r
            dout = dm_ref[:, sl].astype(F32)
            dgf.append(jnp.sum(dout * segn, axis=0, keepdims=True))
            dof_ref[:, sl] = _norm_bwd(dout * gf_ref[:, sl], segn, r).astype(BF16)
        dgg = []
        for h in range(GLA_HEADS):
            sl = slice(h * GLA_DV, (h + 1) * GLA_DV)
            seg = og_ref[:, sl]
            r = _rms(seg)
            segn = seg * r
            gl = segn * gg_ref[:, sl]
            gr = gr_ref[:, sl].astype(F32)
            sig = _sigmoid(gr)
            dout = dm_ref[:, pl.ds(FOX_HEADS * FOX_HEAD_DIM + h * GLA_DV, GLA_DV)].astype(F32)
            dgr_ref[:, sl] = (dout * gl * (sig * (1.0 + gr * (1.0 - sig)))).astype(BF16)
            dgl = dout * (gr * sig)
            dgg.append(jnp.sum(dgl * segn, axis=0, keepdims=True))
            dog_ref[:, sl] = _norm_bwd(dgl * gg_ref[:, sl], segn, r).astype(BF16)
        _acc_row(dgf_ref, jnp.concatenate(dgf, axis=1))
        _acc_row(dgg_ref, jnp.concatenate(dgg, axis=1))

    return _rows_call("mix_bwd", body, [(dmix, D_MODEL, 0), (o_fox, 1024, 0), (o_gla, 1024, 0), (pm, 1024, 5)],
                      [g_fox, g_gla], [(1024, BF16), (1024, BF16), (1024, BF16)], [1024, 1024], s, deps)


def _postmix_premlp(x, y, gate_m, g_post_mix, g_pre_mlp, scale_f, shift_f):
    s = x.shape[0]

    def body(x_ref, y_ref, gm_ref, gpm_ref, gpl_ref, sc_ref, sh_ref, x1_ref, h2_ref):
        yv = y_ref[...].astype(F32)
        x1 = x_ref[...] + gm_ref[...] * (yv * _rms(yv) * gpm_ref[...])
        x1_ref[...] = x1
        h2_ref[...] = ((x1 * _rms(x1) * gpl_ref[...]) * (1.0 + sc_ref[...]) + sh_ref[...]).astype(BF16)

    return _rows_call("postmix_premlp", body, [(x, D_MODEL, 0), (y, D_MODEL, 0)],
                      [gate_m, g_post_mix, g_pre_mlp, scale_f, shift_f], [(D_MODEL, F32), (D_MODEL, BF16)], [], s)


def _loss_postmlp_bwd(x1, y2, target, gate_f, g_post_mlp):
    s = x1.shape[0]

    def body(x1_ref, y2_ref, t_ref, gf_ref, g_ref, dx2_ref, dy2_ref, loss_ref, dgate_ref, dg_ref):
        yv = y2_ref[...].astype(F32)
        r = _rms(yv)
        yn = yv * r
        o = yn * g_ref[...]
        e = (x1_ref[...] + gf_ref[...] * o) - t_ref[...]
        part = 0.5 * jnp.sum(jnp.mean(e * e, axis=-1, keepdims=True), axis=0, keepdims=True)
        _acc_vec(loss_ref, jnp.broadcast_to(part, (1, 128)))
        dx2 = e * (1.0 / D_MODEL)
        dx2_ref[...] = dx2
        _acc_vec(dgate_ref, dx2 * o)
        do = dx2 * gf_ref[...]
        _acc_vec(dg_ref, do * yn)
        dy2_ref[...] = _norm_bwd(do * g_ref[...], yn, r).astype(BF16)

    return _rows_call("loss_postmlp_bwd", body, [(x1, D_MODEL, 0), (y2, D_MODEL, 0), (target, D_MODEL, 0)],
                      [gate_f, g_post_mlp], [(D_MODEL, F32), (D_MODEL, BF16)], [128, D_MODEL, D_MODEL], s)


def _premlp_postmix_bwd(dh2, dx2, x1, y, scale_f, g_pre_mlp, gate_m, g_post_mix, deps=()):
    s = x1.shape[0]

    def body(dh2_ref, dx2_ref, x1_ref, y_ref, sc_ref, gpl_ref, gm_ref, gpm_ref,
             dx1_ref, dy_ref, dsc_ref, dsh_ref, dgpl_ref, dgm_ref, dgpm_ref):
        x1 = x1_ref[...]
        r1 = _rms(x1)
        x1n = x1 * r1
        dh2 = dh2_ref[...].astype(F32)
        _acc_vec(dsc_ref, dh2 * (x1n * gpl_ref[...]))
        _acc_vec(dsh_ref, dh2)
        dn2 = dh2 * (1.0 + sc_ref[...])
        _acc_vec(dgpl_ref, dn2 * x1n)
        dx1 = dx2_ref[...] + _norm_bwd(dn2 * gpl_ref[...], x1n, r1)
        dx1_ref[...] = dx1
        yv = y_ref[...].astype(F32)
        ry = _rms(yv)
        yn = yv * ry
        _acc_vec(dgm_ref, dx1 * (yn * gpm_ref[...]))
        do = dx1 * gm_ref[...]
        _acc_vec(dgpm_ref, do * yn)
        dy_ref[...] = _norm_bwd(do * gpm_ref[...], yn, ry).astype(BF16)

    return _rows_call("premlp_postmix_bwd", body,
                      [(dh2, D_MODEL, 0), (dx2, D_MODEL, 0), (x1, D_MODEL, 0), (y, D_MODEL, 0)],
                      [scale_f, g_pre_mlp, gate_m, g_post_mix], [(D_MODEL, F32), (D_MODEL, BF16)],
                      [D_MODEL] * 5, s, deps)


def _premix_bwd(dh, dx1, x, g_pre_mix, scale_m):
    s = x.shape[0]

    def body(dh_ref, dx1_ref, x_ref, g_ref, sc_ref, gx_ref, dsc_ref, dsh_ref, dg_ref):
        xv = x_ref[...]
        r = _rms(xv)
        xn = xv * r
        dh = dh_ref[...].astype(F32)
        _acc_vec(dsc_ref, dh * (xn * g_ref[...]))
        _acc_vec(dsh_ref, dh)
        dn1 = dh * (1.0 + sc_ref[...])
        _acc_vec(dg_ref, dn1 * xn)
        gx_ref[...] = dx1_ref[...] + _norm_bwd(dn1 * g_ref[...], xn, r)

    return _rows_call("premix_bwd", body, [(dh, D_MODEL, 0), (dx1, D_MODEL, 0), (x, D_MODEL, 0)],
                      [g_pre_mix, scale_m], [(D_MODEL, F32)], [D_MODEL] * 3, s)


def _split3(v):
    hi = v.astype(BF16)
    r1 = v - hi.astype(F32)
    mid = r1.astype(BF16)
    lo = (r1 - mid.astype(F32)).astype(BF16)
    return hi, mid, lo


def _dot_exact01(v, tri, contract=NN, tri_first=False):
    acc = None
    for part in _split3(v):
        lhs, rhs = (tri, part) if tri_first else (part, tri)
        p = lax.dot_general(lhs, rhs, (contract, ((), ())), preferred_element_type=F32)
        acc = p if acc is None else acc + p
    return acc


def _log_sigmoid(z):
    return jnp.minimum(z, 0.0) - jnp.log(1.0 + jnp.exp(-jnp.abs(z)))


def _fox_cum(small, bvec):
    s = small.shape[0]
    t = _tile(s, CUM_T)

    def body(sm_ref, b_ref, out_ref, carry):
        @pl.when(pl.program_id(0) == 0)
        def _():
            carry[...] = jnp.zeros_like(carry)

        lf = _log_sigmoid(sm_ref[...] + b_ref[...])
        lft = lf.T[0:FOX_HEADS, :]
        row = lax.broadcasted_iota(jnp.int32, (t, t), 0)
        col = lax.broadcasted_iota(jnp.int32, (t, t), 1)
        upper = (row <= col).astype(BF16)
        cum = _dot_exact01(lft, upper) + carry[:, 0:1]
        out_ref[...] = cum
        carry[...] = carry[...] + jnp.sum(lft, axis=1, keepdims=True)

    return _call(body, name="fox_cum", grid=(s // t,),
                 in_specs=[pl.BlockSpec((t, W_SMALL), lambda i: (i, 0)), pl.BlockSpec((1, W_SMALL), lambda i: (0, 0))],
                 out_specs=pl.BlockSpec((FOX_HEADS, t), lambda i: (0, i)),
                 out_shape=jax.ShapeDtypeStruct((FOX_HEADS, s), F32),
                 scratch_shapes=[pltpu.VMEM((FOX_HEADS, 128), F32)],
                 compiler_params=_params(("arbitrary",)))(small, bvec)


def _fox_cum_bwd(dc, dcq, small, bvec):
    s = small.shape[0]
    t = _tile(s, CUM_T)
    nb = s // t

    def body(dc_ref, dcq_ref, sm_ref, b_ref, out_ref, db_ref, carry):
        @pl.when(pl.program_id(0) == 0)
        def _():
            carry[...] = jnp.zeros_like(carry)
            db_ref[...] = jnp.zeros_like(db_ref)

        lane = lax.broadcasted_iota(jnp.int32, (t, W_SMALL), 1)
        dcq = jnp.zeros((t, W_SMALL), F32)
        for hh in range(FOX_HEADS):
            dcq = jnp.where(lane == hh, dcq_ref[hh], dcq)
        dcv = dc_ref[...] + dcq.T[0:FOX_HEADS, :]
        row = lax.broadcasted_iota(jnp.int32, (t, t), 0)
        col = lax.broadcasted_iota(jnp.int32, (t, t), 1)
        lower = (row >= col).astype(BF16)
        dlf = _dot_exact01(dcv, lower) + carry[:, 0:1]
        carry[...] = carry[...] + jnp.sum(dcv, axis=1, keepdims=True)
        z = sm_ref[...] + b_ref[...]
        zt = z.T[0:FOX_HEADS, :]
        dff = dlf * _sigmoid(-zt)
        db_ref[...] = db_ref[...] + jnp.sum(dff, axis=1, keepdims=True)
        full = jnp.concatenate([dff, jnp.zeros((W_SMALL - FOX_HEADS, t), F32)], axis=0)
        out_ref[...] = full.T

    return _call(body, name="fox_cum_bwd", grid=(nb,),
                 in_specs=[pl.BlockSpec((FOX_HEADS, t), lambda i: (0, nb - 1 - i)),
                           pl.BlockSpec((FOX_HEADS, t, 1), lambda i: (0, nb - 1 - i, 0)),
                           pl.BlockSpec((t, W_SMALL), lambda i: (nb - 1 - i, 0)),
                           pl.BlockSpec((1, W_SMALL), lambda i: (0, 0))],
                 out_specs=[pl.BlockSpec((t, W_SMALL), lambda i: (nb - 1 - i, 0)),
                            pl.BlockSpec((FOX_HEADS, 128), lambda i: (0, 0))],
                 out_shape=[jax.ShapeDtypeStruct((s, W_SMALL), F32), jax.ShapeDtypeStruct((FOX_HEADS, 128), F32)],
                 scratch_shapes=[pltpu.VMEM((FOX_HEADS, 128), F32)],
                 compiler_params=_params(("arbitrary",)))(dc, dcq, small, bvec)


FOX_SCALE = FOX_HEAD_DIM ** -0.5


def _fox_fwd(pm, crow):
    s = pm.shape[0]
    t = _tile(s, FOX_T)
    nb = s // t
    parts = 2
    hq = t // parts

    def body(q_ref, k_ref, v_ref, c_ref, o_ref, lse_ref):
        i = pl.program_id(1)
        qs = [q_ref[g * hq:(g + 1) * hq, :] for g in range(parts)]

        def block(j, carry, diagonal):
            rows = pl.ds(pl.multiple_of(j * t, t), t)
            k_all, v_all, c_all = k_ref[rows, :], v_ref[rows, :], c_ref[j]
            out = []
            for g, (m_prev, l_prev, acc) in enumerate(carry):
                nk = (g + 1) * hq if diagonal else t
                kb, vb, cb = k_all[:nk], v_all[:nk], c_all[:, :nk]
                sc = lax.dot_general(qs[g], kb, (NT, ((), ())), preferred_element_type=F32)
                sc = sc * FOX_SCALE - cb
                if diagonal:
                    row = lax.broadcasted_iota(jnp.int32, (hq, nk), 0) + g * hq
                    col = lax.broadcasted_iota(jnp.int32, (hq, nk), 1)
                    sc = jnp.where(row >= col, sc, NEG)
                m_new = jnp.maximum(m_prev, jnp.max(sc, axis=1, keepdims=True))
                alpha = jnp.exp(m_prev - m_new)
                p = jnp.exp(sc - m_new)
                l_new = alpha * l_prev + jnp.sum(p, axis=1, keepdims=True)
                pv = jnp.dot(p.astype(BF16), vb, preferred_element_type=F32)
                out.append((m_new, l_new, alpha * acc + pv))
            return tuple(out)

        init = tuple((jnp.full((hq, 1), NEG, F32), jnp.zeros((hq, 1), F32), jnp.zeros((hq, 128), F32))
                     for _ in range(parts))
        carry = lax.fori_loop(0, i, lambda j, cr: block(j, cr, False), init)
        carry = block(i, carry, True)
        for g, (m_fin, l_fin, acc) in enumerate(carry):
            o_ref[g * hq:(g + 1) * hq, :] = acc / l_fin
            lse_ref[g * hq:(g + 1) * hq, :] = m_fin + jnp.log(l_fin)

    return _call(
        body, name="fox_fwd", grid=(FOX_HEADS, nb),
        in_specs=[pl.BlockSpec((t, 128), lambda h, i: (i, h)),
                  pl.BlockSpec((s, 128), lambda h, i: (0, FOX_HEADS + h)),
                  pl.BlockSpec((s, 128), lambda h, i: (0, 2 * FOX_HEADS + h)),
                  pl.BlockSpec((None, nb, 1, t), lambda h, i: (h, 0, 0, 0))],
        out_specs=[pl.BlockSpec((t, 128), lambda h, i: (i, h)),
                   pl.BlockSpec((None, t, 1), lambda h, i: (h, i, 0))],
        out_shape=[jax.ShapeDtypeStruct((s, FOX_HEADS * 128), F32), jax.ShapeDtypeStruct((FOX_HEADS, s, 1), F32)],
        compiler_params=_params(("parallel", "arbitrary")),
    )(pm, pm, pm, crow.reshape(FOX_HEADS, nb, 1, t))


def _fox_bwd(pm, crow, o, lse, do):
    s = pm.shape[0]
    t = _tile(s, FOX_T)
    nb = s // t

    parts = 2
    hq = t // parts

    def body(q_ref, do_ref, o_ref, lse_ref, k_ref, v_ref, c_ref, dq_ref, dk_ref, dv_ref, dc_ref, dcq_ref, delta_s):
        j = pl.program_id(1)

        @pl.when(j == 0)
        def _():
            dq_ref[...] = jnp.zeros_like(dq_ref)
            dcq_ref[...] = jnp.zeros_like(dcq_ref)
            delta_s[...] = jnp.sum(do_ref[...].astype(F32) * o_ref[...], axis=1, keepdims=True)

        k_all, v_all, c_all = k_ref[...], v_ref[...], c_ref[...]

        def grow(acc, part, axis):
            n = part.shape[axis]
            if n == acc.shape[axis]:
                return acc + part
            if axis == 0:
                return jnp.concatenate([acc[:n] + part, acc[n:]], axis=0)
            return jnp.concatenate([acc[:, :n] + part, acc[:, n:]], axis=1)

        def block(i, carry, diagonal):
            dk_acc, dv_acc, dc_acc = carry
            for g in range(parts):
                nk = (g + 1) * hq if diagonal else t
                kb, vb, cb = k_all[:nk], v_all[:nk], c_all[:, :nk]
                rows = pl.ds(pl.multiple_of(i * t + g * hq, hq), hq)
                q, dov = q_ref[rows, :], do_ref[rows, :]
                sc = lax.dot_general(q, kb, (NT, ((), ())), preferred_element_type=F32)
                p = jnp.exp(sc * FOX_SCALE - cb - lse_ref[rows, :])
                if diagonal:
                    row = lax.broadcasted_iota(jnp.int32, (hq, nk), 0) + g * hq
                    col = lax.broadcasted_iota(jnp.int32, (hq, nk), 1)
                    p = jnp.where(row >= col, p, 0.0)
                dp = lax.dot_general(dov, vb, (NT, ((), ())), preferred_element_type=F32)
                ds = p * (dp - delta_s[rows, :])
                dsb = ds.astype(BF16)
                dv_acc = grow(dv_acc, lax.dot_general(p.astype(BF16), dov, (TN, ((), ())),
                                                      preferred_element_type=F32), 0)
                dk_acc = grow(dk_acc, lax.dot_general(dsb, q, (TN, ((), ())), preferred_element_type=F32), 0)
                dq_ref[rows, :] += jnp.dot(dsb, kb, preferred_element_type=F32) * FOX_SCALE
                dc_acc = grow(dc_acc, -jnp.sum(ds, axis=0, keepdims=True), 1)
                dcq_ref[rows, :] += jnp.sum(ds, axis=1, keepdims=True)
            return dk_acc, dv_acc, dc_acc

        carry = (jnp.zeros((t, 128), F32), jnp.zeros((t, 128), F32), jnp.zeros((1, t), F32))
        carry = block(j, carry, True)
        dk_acc, dv_acc, dc_acc = lax.fori_loop(j + 1, nb, lambda i, cr: block(i, cr, False), carry)
        dk_ref[...] = (dk_acc * FOX_SCALE).astype(dk_ref.dtype)
        dv_ref[...] = dv_acc.astype(dv_ref.dtype)
        dc_ref[...] = dc_acc

    whole = lambda h, j: (0, h)
    return _call(
        body, name="fox_bwd", grid=(FOX_HEADS, nb),
        in_specs=[pl.BlockSpec((s, 128), whole), pl.BlockSpec((s, 128), whole), pl.BlockSpec((s, 128), whole),
                  pl.BlockSpec((None, s, 1), lambda h, j: (h, 0, 0)),
                  pl.BlockSpec((t, 128), lambda h, j: (j, FOX_HEADS + h)),
                  pl.BlockSpec((t, 128), lambda h, j: (j, 2 * FOX_HEADS + h)),
                  pl.BlockSpec((None, 1, t), lambda h, j: (h, 0, j))],
        out_specs=[pl.BlockSpec((s, 128), whole),
                   pl.BlockSpec((t, 128), lambda h, j: (j, h)),
                   pl.BlockSpec((t, 128), lambda h, j: (j, h)),
                   pl.BlockSpec((None, 1, t), lambda h, j: (h, 0, j)),
                   pl.BlockSpec((None, s, 1), lambda h, j: (h, 0, 0))],
        out_shape=[jax.ShapeDtypeStruct((s, 1024), F32), jax.ShapeDtypeStruct((s, 1024), BF16),
                   jax.ShapeDtypeStruct((s, 1024), BF16), jax.ShapeDtypeStruct((FOX_HEADS, 1, s), F32),
                   jax.ShapeDtypeStruct((FOX_HEADS, s, 1), F32)],
        scratch_shapes=[pltpu.VMEM((s, 1), F32)],
        compiler_params=_params(("parallel", "arbitrary")),
    )(pm, do, o, lse, pm, pm, crow)


GLA_SCALE = GLA_DK ** -0.5
GLA_Q_BLK = 3072 // 128
GLA_K_BLK = 3584 // 128
GLA_V_BLK = 4096 // 256


def _gla_gate(sm, wa_ref, b_ref):
    return jnp.dot(sm.astype(BF16), wa_ref[...], preferred_element_type=F32) + b_ref[...]


def _chunk_tri(n, kind):
    row = lax.broadcasted_iota(jnp.int32, (n, n), 0)
    col = lax.broadcasted_iota(jnp.int32, (n, n), 1)
    shift = CHUNK.bit_length() - 1
    same = (row >> shift) == (col >> shift)
    if kind == "upto":
        same = same & (row >= col)
    elif kind == "before":
        same = same & (row > col)
    return same.astype(BF16)


def _gla_fwd(pm, small, wa_pad, b_a2, deps=()):
    s = pm.shape[0]
    r = _tile(s, GLA_R)
    nc = r // CHUNK

    def body(q_ref, k_ref, v_ref, sm_ref, wa_ref, b_ref, o_ref, st_ref, state):
        @pl.when(pl.program_id(1) == 0)
        def _():
            state[...] = jnp.zeros_like(state)

        la_all = _log_sigmoid(_gla_gate(sm_ref[...], wa_ref, b_ref)) * (1.0 / GLA_TEMP)
        tri = _chunk_tri(CHUNK, "upto")
        uts, decays = [], []
        for c in range(nc):
            rows = slice(c * CHUNK, (c + 1) * CHUNK)
            la = la_all[rows]
            cum = _dot_exact01(la, tri, tri_first=True)
            total = jnp.sum(la, axis=0, keepdims=True)
            kdec = k_ref[rows, :].astype(F32) * jnp.exp(total - cum)
            uts.append(lax.dot_general(v_ref[rows, :], kdec.astype(BF16), (TN, ((), ())),
                                       preferred_element_type=F32))
            decays.append(jnp.exp(total))
        cur = state[...]
        ends = []
        for c in range(nc):
            cur = cur * decays[c] + uts[c]
            ends.append(cur.astype(BF16))
        state[...] = cur
        for c in range(nc):
            rows = slice(c * CHUNK, (c + 1) * CHUNK)
            st_ref[c] = ends[c]
            qs = (q_ref[rows, :].astype(F32) * GLA_SCALE).astype(BF16)
            o_ref[rows, :] = lax.dot_general(qs, ends[c], (NT, ((), ())), preferred_element_type=F32)

    return _call(
        body, deps=deps, name="gla_fwd", grid=(GLA_HEADS, s // r),
        in_specs=[pl.BlockSpec((r, 128), lambda h, i: (i, GLA_Q_BLK + h)),
                  pl.BlockSpec((r, 128), lambda h, i: (i, GLA_K_BLK + h)),
                  pl.BlockSpec((r, 256), lambda h, i: (i, GLA_V_BLK + h)),
                  pl.BlockSpec((r, W_SMALL), lambda h, i: (i, 0)),
                  pl.BlockSpec((W_SMALL, 128), lambda h, i: (0, h)),
                  pl.BlockSpec((1, 128), lambda h, i: (0, h))],
        out_specs=[pl.BlockSpec((r, 256), lambda h, i: (i, h)),
                   pl.BlockSpec((nc, None, GLA_DV, GLA_DK), lambda h, i: (i, h, 0, 0))],
        out_shape=[jax.ShapeDtypeStruct((s, 1024), F32),
                   jax.ShapeDtypeStruct((s // CHUNK, GLA_HEADS, GLA_DV, GLA_DK), BF16)],
        scratch_shapes=[pltpu.VMEM((GLA_DV, GLA_DK), F32)],
        compiler_params=_params(("parallel", "arbitrary")),
    )(pm, pm, pm, small, wa_pad, b_a2)


def _gla_bwd(pm, small, wa_pad, b_a2, states, do):
    s = pm.shape[0]
    r = _tile(s, GLA_R)
    nc = r // CHUNK
    nb = s // r

    def body(q_ref, k_ref, v_ref, sm_ref, wa_ref, b_ref, do_ref, st_ref, prev_ref,
             dq_ref, dk_ref, dv_ref, dza_ref, db_ref, carry):
        step = pl.program_id(1)

        @pl.when(step == 0)
        def _():
            carry[...] = jnp.zeros_like(carry)
            db_ref[...] = jnp.zeros_like(db_ref)

        z_all = _gla_gate(sm_ref[...], wa_ref, b_ref)
        la_all = _log_sigmoid(z_all) * (1.0 / GLA_TEMP)
        tri = _chunk_tri(CHUNK, "upto")
        tri_strict = _chunk_tri(CHUNK, "before")
        ws, decays, kdecs, gouts = [], [], [], []
        for c in range(nc):
            rows = slice(c * CHUNK, (c + 1) * CHUNK)
            la = la_all[rows]
            cum = _dot_exact01(la, tri, tri_first=True)
            total = jnp.sum(la, axis=0, keepdims=True)
            w = jnp.exp(total - cum)
            ws.append(w)
            decays.append(jnp.exp(total))
            kdecs.append(k_ref[rows, :].astype(F32) * w)
            dov = do_ref[rows, :]
            qs = (q_ref[rows, :].astype(F32) * GLA_SCALE).astype(BF16)
            dq_ref[rows, :] = (jnp.dot(dov, st_ref[c], preferred_element_type=F32) * GLA_SCALE).astype(BF16)
            gouts.append(lax.dot_general(dov, qs, (TN, ((), ())), preferred_element_type=F32))
        cur = carry[...]
        gts = [None] * nc
        for c in reversed(range(nc)):
            gts[c] = gouts[c] + cur
            cur = gts[c] * decays[c]
        carry[...] = cur
        db = jnp.zeros((1, 128), F32)
        for c in range(nc):
            rows = slice(c * CHUNK, (c + 1) * CHUNK)
            gtb = gts[c].astype(BF16)
            dv_ref[rows, :] = lax.dot_general(kdecs[c].astype(BF16), gtb, (NT, ((), ())),
                                              preferred_element_type=F32).astype(BF16)
            dkdec = jnp.dot(v_ref[rows, :], gtb, preferred_element_type=F32)
            dk_ref[rows, :] = (dkdec * ws[c]).astype(BF16)
            e = dkdec * kdecs[c]
            if c > 0:
                prev = st_ref[c - 1].astype(F32)
            else:
                prev = jnp.where(step == nb - 1, 0.0, prev_ref[0].astype(F32))
            dtot = jnp.sum(gts[c] * prev, axis=0, keepdims=True) * decays[c]
            dla = dtot + _dot_exact01(e, tri_strict, tri_first=True)
            dza = dla * (1.0 / GLA_TEMP) * _sigmoid(-z_all[rows])
            dza_ref[rows, :] = dza.astype(BF16)
            db = db + jnp.sum(dza, axis=0, keepdims=True)
        db_ref[...] += db

    blk = lambda h, i: nb - 1 - i
    return _call(
        body, name="gla_bwd", grid=(GLA_HEADS, nb),
        in_specs=[pl.BlockSpec((r, 128), lambda h, i: (blk(h, i), GLA_Q_BLK + h)),
                  pl.BlockSpec((r, 128), lambda h, i: (blk(h, i), GLA_K_BLK + h)),
                  pl.BlockSpec((r, 256), lambda h, i: (blk(h, i), GLA_V_BLK + h)),
                  pl.BlockSpec((r, W_SMALL), lambda h, i: (blk(h, i), 0)),
                  pl.BlockSpec((W_SMALL, 128), lambda h, i: (0, h)),
                  pl.BlockSpec((1, 128), lambda h, i: (0, h)),
                  pl.BlockSpec((r, 256), lambda h, i: (blk(h, i), h)),
                  pl.BlockSpec((nc, None, GLA_DV, GLA_DK), lambda h, i: (blk(h, i), h, 0, 0)),
                  pl.BlockSpec((1, None, GLA_DV, GLA_DK),
                               lambda h, i: (jnp.maximum(blk(h, i) * nc - 1, 0), h, 0, 0))],
        out_specs=[pl.BlockSpec((r, 128), lambda h, i: (blk(h, i), h)),
                   pl.BlockSpec((r, 128), lambda h, i: (blk(h, i), h)),
                   pl.BlockSpec((r, 256), lambda h, i: (blk(h, i), h)),
                   pl.BlockSpec((r, 128), lambda h, i: (blk(h, i), h)),
                   pl.BlockSpec((1, 128), lambda h, i: (0, h))],
        out_shape=[jax.ShapeDtypeStruct((s, 512), BF16), jax.ShapeDtypeStruct((s, 512), BF16),
                   jax.ShapeDtypeStruct((s, 1024), BF16), jax.ShapeDtypeStruct((s, 512), BF16),
                   jax.ShapeDtypeStruct((1, 512), F32)],
        scratch_shapes=[pltpu.VMEM((GLA_DV, GLA_DK), F32)],
        compiler_params=_params(("parallel", "arbitrary")),
    )(pm, pm, pm, small, wa_pad, b_a2, do, states, states)


def _modulation(c_all, w_ada):
    n = w_ada.shape[1]
    tn = _tile(n, 512)

    def body(c_ref, w_ref, out_ref, ca_ref):
        cv = c_ref[...]
        ca = cv * _sigmoid(cv)
        ca_ref[...] = ca
        out_ref[...] = jnp.dot(ca.astype(BF16), w_ref[...].astype(BF16), preferred_element_type=F32)

    return _call(body, name="modulation", grid=(n // tn,),
                 in_specs=[pl.BlockSpec((N_DEV, D_MODEL), lambda j: (0, 0)),
                           pl.BlockSpec((D_MODEL, tn), lambda j: (0, j))],
                 out_specs=[pl.BlockSpec((N_DEV, tn), lambda j: (0, j)),
                            pl.BlockSpec((N_DEV, D_MODEL), lambda j: (0, 0))],
                 out_shape=[jax.ShapeDtypeStruct((N_DEV, n), F32), jax.ShapeDtypeStruct((N_DEV, D_MODEL), F32)],
                 compiler_params=_params(("arbitrary",)))(c_all, w_ada)


def _adamw_math(w, g, m, v):
    m = ADAM_B1 * m + (1.0 - ADAM_B1) * g
    v = ADAM_B2 * v + (1.0 - ADAM_B2) * (g * g)
    m_hat = m / (1.0 - ADAM_B1 ** ADAM_STEP)
    v_hat = v / (1.0 - ADAM_B2 ** ADAM_STEP)
    delta = -ADAM_LR * (m_hat / (jnp.sqrt(v_hat) + ADAM_EPS) + ADAM_WD * w)
    return delta, m, v


def _adamw_slabs(name, w, slabs, m, v, tr=256):
    rr, cc = w.shape

    def body(w_ref, s_ref, m_ref, v_ref, g_ref, d_ref, nm_ref, nv_ref):
        g = s_ref[0].astype(F32)
        for r in range(1, N_DEV):
            g = g + s_ref[r].astype(F32)
        g_ref[...] = g
        d, nm, nv = _adamw_math(w_ref[...], g, m_ref[...], v_ref[...])
        d_ref[...] = d
        nm_ref[...] = nm
        nv_ref[...] = nv

    steps, spec, slab_spec = _plane_tiles(rr, cc, tr)
    return _call(body, name=name, grid=(steps,),
                 in_specs=[spec, slab_spec, spec, spec],
                 out_specs=[spec] * 4, out_shape=[jax.ShapeDtypeStruct((rr, cc), F32)] * 4,
                 compiler_params=_params(("parallel",)))(w, slabs, m, v)


def _adamw_ada(w, cat, dm, m, v, tr=256, deps=()):
    rr, cc = w.shape
    tr = _tile(rr, tr)

    def body(w_ref, ca_ref, dm_ref, m_ref, v_ref, g_ref, d_ref, nm_ref, nv_ref):
        g = ca_ref[:, 0:1] * dm_ref[0:1, :]
        for b in range(1, N_DEV):
            g = g + ca_ref[:, b:b + 1] * dm_ref[b:b + 1, :]
        g_ref[...] = g
        d, nm, nv = _adamw_math(w_ref[...], g, m_ref[...], v_ref[...])
        d_ref[...] = d
        nm_ref[...] = nm
        nv_ref[...] = nv

    spec = pl.BlockSpec((tr, cc), lambda i: (i, 0))
    return _call(body, deps=deps, name="adamw_ada", grid=(rr // tr,),
                 in_specs=[spec, pl.BlockSpec((tr, N_DEV), lambda i: (i, 0)),
                           pl.BlockSpec((N_DEV, cc), lambda i: (0, 0)), spec, spec],
                 out_specs=[spec] * 4, out_shape=[jax.ShapeDtypeStruct((rr, cc), F32)] * 4,
                 compiler_params=_params(("parallel",)))(w, cat, dm, m, v)


def _sum_devices(gathered):
    ln = gathered.shape[-1]

    def body(g_ref, out_ref):
        acc = g_ref[0]
        for r in range(1, N_DEV):
            acc = acc + g_ref[r]
        out_ref[...] = acc

    return _call(body, name="sum_devices",
                 in_specs=[pl.BlockSpec(memory_space=pltpu.VMEM)], out_specs=pl.BlockSpec(memory_space=pltpu.VMEM),
                 out_shape=jax.ShapeDtypeStruct((1, ln), F32))(gathered)


def _adamw_flat(w, g, m, v):
    def body(w_ref, g_ref, m_ref, v_ref, d_ref, nm_ref, nv_ref):
        d, nm, nv = _adamw_math(w_ref[...], g_ref[...], m_ref[...], v_ref[...])
        d_ref[...] = d
        nm_ref[...] = nm
        nv_ref[...] = nv

    vm = pl.BlockSpec(memory_space=pltpu.VMEM)
    return _call(body, name="adamw_small", in_specs=[vm] * 4, out_specs=[vm] * 3,
                 out_shape=[jax.ShapeDtypeStruct(w.shape, F32)] * 3)(w, g, m, v)


def _from_col_shards(g):
    return jnp.transpose(g, (1, 0, 2)).reshape(g.shape[1], N_DEV * g.shape[2])


def _pad_lanes(v, n):
    return jnp.concatenate([v, jnp.zeros(v.shape[:-1] + (n - v.shape[-1],), v.dtype)], axis=-1)


def kernel(x, c, w_ada, b_ada, g_pre_mix, g_post_mix, w_in, b_fgate, w_gla_a2, b_gla_a2, g_fox_out, g_gla_out, w_out, g_pre_mlp, g_post_mlp, w_mlp_in, w_mlp_out, loss_target, m_w_ada, m_b_ada, m_g_pre_mix, m_g_post_mix, m_w_in, m_b_fgate, m_w_gla_a2, m_b_gla_a2, m_g_fox_out, m_g_gla_out, m_w_out, m_g_pre_mlp, m_g_post_mlp, m_w_mlp_in, m_w_mlp_out, v_w_ada, v_b_ada, v_g_pre_mix, v_g_post_mix, v_w_in, v_b_fgate, v_w_gla_a2, v_b_gla_a2, v_g_fox_out, v_g_gla_out, v_w_out, v_g_pre_mlp, v_g_post_mlp, v_w_mlp_in, v_w_mlp_out):
    rank = _my_rank()
    xs = x[0]
    s = xs.shape[0]
    target = loss_target[0]

    w_in_t, m_in_t, v_in_t = w_in[0].T, m_w_in[0].T, v_w_in[0].T
    c_all, wa2_g, ggla_g, win_g = _all_gather("gather_first", [c, w_gla_a2[0], g_gla_out[0], w_in_t.astype(BF16)])
    rest = [_own_slot("own_w_out", w_out[0], True, rank), _own_slot("own_w_mlp_in", w_mlp_in[0], True, rank)]
    gs_send, gs_sib, gs_ici, gs_land, gs_token = _gather2_start("gather_rest_start", rest, after=(c_all,))
    last = [_own_slot("own_w_mlp_out", w_mlp_out[0], True, rank)]
    gl_send, gl_sib, gl_ici, gl_land, gl_token = _gather2_start("gather_last_start", last, after=(gs_token,))
    w_a2 = _from_col_shards(wa2_g)
    g_gla = _from_col_shards(ggla_g).reshape(1, 1024)
    g_fox = g_fox_out.reshape(1, 1024)
    win_full = win_g.reshape(N_DEV * 771, D_MODEL)
    w_main = jnp.concatenate([win_full[:3072], win_full[3080:5128], win_full[5144:6168]], axis=0)
    w_small = jnp.concatenate([win_full[3072:3080], win_full[5128:5144],
                               jnp.zeros((W_SMALL - 24, D_MODEL), BF16)], axis=0)
    wa_pad =jnp.concatenate([jnp.zeros((8, 512), BF16), w_a2.astype(BF16), jnp.zeros((104, 512), BF16)], axis=0)
    bf_vec = _pad_lanes(b_fgate, W_SMALL)

    mod_part, c_act = _modulation(c_all.reshape(N_DEV, D_MODEL), w_ada[0])
    (mod_g,) = _all_gather("gather_mod", [mod_part])
    mod = lax.dynamic_slice_in_dim(mod_g, rank, 1, axis=1).reshape(1, 6 * D_MODEL) + b_ada
    shift_m, scale_m, gate_m, shift_f, scale_f, gate_f = [mod[:, i * D_MODEL:(i + 1) * D_MODEL] for i in range(6)]

    h = _premix(xs, g_pre_mix, scale_m, shift_m, deps=(gl_token,))
    pm = _mm_plain("proj_main", h, w_main, NT, BF16)
    small = _mm_plain("proj_small", h, w_small, NT, F32)
    crow = _fox_cum(small, bf_vec).reshape(FOX_HEADS, 1, s)
    o_fox, lse = _fox_fwd(pm, crow)
    gs_fsend, gs_frecv, gs_land, gs_ftoken = _gather2_forward("gather_rest_forward", gs_land, gs_ici, o_fox)
    o_gla, states = _gla_fwd(pm, small, wa_pad, b_gla_a2, deps=(gs_ftoken,))
    mix = _mix_fwd(o_fox, o_gla, pm, g_fox, g_gla)
    wout_g, wmi_g = _gather2_wait("gather_rest_wait", gs_land, gs_send, gs_sib, gs_fsend, gs_frecv, mix)
    w_out_full = wout_g.reshape(D_MODEL, D_MODEL)
    y = _mm_plain("out_proj", mix, w_out_full, NN, BF16)
    x1, h2 = _postmix_premlp(xs, y, gate_m, g_post_mix, g_pre_mlp, scale_f, shift_f)
    gl_fsend, gl_frecv, gl_land, gl_ftoken = _gather2_forward("gather_last_forward", gl_land, gl_ici, h2)

    tm, tn, tk = _tile(s, 1024), 1024, 2048
    nsh = 1024 // tn

    def relu2(acc):
        rl = jnp.maximum(acc, 0.0)
        return rl * rl, rl

    z, a_relu = _matmul(
        "mlp_in", h2, wmi_g, contract=NN, grid=(s // tm, D_FF // tn, D_MODEL // tk),
        a_spec=pl.BlockSpec((tm, tk), lambda i, j, k: (i, k)),
        b_spec=pl.BlockSpec((None, tk, tn), lambda i, j, k: (j // nsh, k, j % nsh)),
        out_specs=[pl.BlockSpec((tm, tn), lambda i, j, k: (i, j))] * 2,
        out_shapes=[jax.ShapeDtypeStruct((s, D_FF), BF16)] * 2, acc_shape=(tm, tn), epilogue=relu2,
        deps=(gl_ftoken,))
    (wmo_g,) = _gather2_wait("gather_last_wait", gl_land, gl_send, gl_sib, gl_fsend, gl_frecv, z)
    w_mo_full = wmo_g.reshape(D_FF, D_MODEL)
    y2 = _mm_plain("mlp_out", z, w_mo_full, NN, BF16)

    dx2, dy2, loss_vec, dgate_f, dg_post_mlp = _loss_postmlp_bwd(x1, y2, target, gate_f, g_post_mlp)
    loss = lax.psum(loss_vec[0, 0], ("x", "y", "c"))

    da = _mm_plain("mlp_out_dx", dy2, w_mo_full, NT, BF16, extra=(a_relu,),
                   epilogue=lambda acc, rl: (acc * (2.0 * rl.astype(F32)),))
    dw_mo = _mm_plain("mlp_out_dw", z, dy2, TN, BF16)
    dw_mo = dw_mo.reshape(N_DEV, 1024, D_MODEL)
    x_mo = _exchange_start("grad_mlp_out_start", [_own_slot("own_dw_mlp_out", dw_mo, False, rank)], [dw_mo])
    tkx = 2048
    (dh2,) = _matmul(
        "mlp_in_dx", da, wmi_g, contract=NT, grid=(s // tm, D_MODEL // tn, D_FF // tkx),
        a_spec=pl.BlockSpec((tm, tkx), lambda i, j, k: (i, k)),
        b_spec=pl.BlockSpec((tkx // 1024, tn, 1024), lambda i, j, k: (k, j, 0)),
        out_specs=[pl.BlockSpec((tm, tn), lambda i, j, k: (i, j))],
        out_shapes=[jax.ShapeDtypeStruct((s, D_MODEL), BF16)], acc_shape=(tm, tn), deps=(x_mo[4],))
    ts = _tile(s, 2048)
    (dw_mi,) = _matmul(
        "mlp_in_dw", h2, da, contract=TN, grid=(D_MODEL // 1024, D_FF // tn, s // ts),
        a_spec=pl.BlockSpec((ts, 1024), lambda i, j, k: (k, i)),
        b_spec=pl.BlockSpec((ts, tn), lambda i, j, k: (k, j)),
        out_specs=[pl.BlockSpec((None, 1024, tn), lambda i, j, k: (j // nsh, i, j % nsh))],
        out_shapes=[jax.ShapeDtypeStruct((N_DEV, D_MODEL, 1024), BF16)], acc_shape=(1024, tn))
    x_mi = _exchange_start("grad_mlp_in_start", [_own_slot("own_dw_mlp_in", dw_mi, False, rank)], [dw_mi])

    dx1, dy, dscale_f, dshift_f, dg_pre_mlp, dgate_m, dg_post_mix = _premlp_postmix_bwd(
        dh2, dx2, x1, y, scale_f, g_pre_mlp, gate_m, g_post_mix, deps=(x_mi[4],))

    dmix = _mm_plain("out_proj_dx", dy, w_out_full, NT, BF16)
    dw_out = _mm_plain("out_proj_dw", mix, dy, TN, BF16)
    dw_out = dw_out.reshape(N_DEV, 256, D_MODEL)
    x_out = _exchange_start("grad_out_start", [_own_slot("own_dw_out", dw_out, False, rank)], [dw_out])
    do_fox, do_gla, dgr, dg_fox, dg_gla = _mix_bwd(dmix, o_fox, o_gla, pm, g_fox, g_gla, deps=(x_out[4],))

    dq, dk, dv, dc, dcq = _fox_bwd(pm, crow, o_fox, lse, do_fox)
    dsmall_f, db_f = _fox_cum_bwd(dc.reshape(FOX_HEADS, s), dcq, small, bf_vec)
    dgq, dgk, dgv, dza, db_a2 = _gla_bwd(pm, small, wa_pad, b_gla_a2, states, do_gla)
    dsmall = _mm_plain("gate_dx", dza, wa_pad, NT, F32, tn=128, extra=(dsmall_f,),
                       epilogue=lambda acc, other: (acc + other,))
    dwa_pad = _mm_plain("gate_dw", small, dza, TN, F32, tm=128, tn=512)

    dpm = jnp.concatenate([dq.astype(BF16), dk.astype(BF16), dv.astype(BF16), dgq.astype(BF16), dgk.astype(BF16),
                           dgv.astype(BF16), dgr], axis=1)
    dw_main = _mm_plain("proj_main_dw", dpm, h, TN, BF16)
    dw_small = _mm_plain("proj_small_dw", dsmall, h, TN, BF16, tm=128)
    dwin_full = jnp.concatenate([dw_main[:3072], dw_small[0:8], dw_main[3072:5120], dw_small[8:24],
                                 dw_main[5120:6144]], axis=0)
    dwin_slabs = dwin_full.reshape(N_DEV, 771, D_MODEL)
    x_in = _exchange_start("grad_in_start", [_own_slot("own_dw_in", dwin_slabs, False, rank)], [dwin_slabs])
    tmx = _tile(s, 1024)
    (dh,) = _matmul(
        "proj_main_dx", dpm, w_main, contract=NN, grid=(s // tmx, D_MODEL // 1024, W_MAIN // 2048),
        a_spec=pl.BlockSpec((tmx, 2048), lambda i, j, k: (i, k)),
        b_spec=pl.BlockSpec((2048, 1024), lambda i, j, k: (k, j)),
        out_specs=[pl.BlockSpec((tmx, 1024), lambda i, j, k: (i, j))],
        out_shapes=[jax.ShapeDtypeStruct((s, D_MODEL), BF16)], acc_shape=(tmx, 1024),
        extra=(dsmall, w_small),
        extra_specs=[pl.BlockSpec((tmx, W_SMALL), lambda i, j, k: (i, 0)),
                     pl.BlockSpec((W_SMALL, 1024), lambda i, j, k: (0, j))],
        epilogue=lambda acc, dsm, wsm: (acc + jnp.dot(dsm.astype(BF16), wsm, preferred_element_type=F32),),
        deps=(x_in[4],))
    grad_x, dscale_m, dshift_m, dg_pre_mix = _premix_bwd(dh, dx1, xs, g_pre_mix, scale_m)

    dmod = jnp.concatenate([dshift_m, dscale_m, dgate_m, dshift_f, dscale_f, dgate_f], axis=1)
    flat = jnp.concatenate(
        [dmod, dg_pre_mix, dg_post_mix, dg_fox, dg_pre_mlp, dg_post_mlp, db_a2,
         dwa_pad[8:24, :].reshape(1, GLA_RANK * 512), dg_gla, _pad_lanes(db_f[:, 0].reshape(1, FOX_HEADS), 128)],
        axis=1)

    (r_mo,) = _exchange_wait("grad_mlp_out_wait", *x_mo[:4], grad_x)
    g_mo, d_mo, nm_mo, nv_mo = _adamw_slabs("adamw_w_mlp_out", w_mlp_out[0], r_mo, m_w_mlp_out[0], v_w_mlp_out[0])
    (r_mi,) = _exchange_wait("grad_mlp_in_wait", *x_mi[:4], g_mo)
    g_mi, d_mi, nm_mi, nv_mi = _adamw_slabs("adamw_w_mlp_in", w_mlp_in[0], r_mi, m_w_mlp_in[0], v_w_mlp_in[0])
    (r_out,) = _exchange_wait("grad_out_wait", *x_out[:4], g_mi)
    g_out, d_out, nm_out, nv_out = _adamw_slabs("adamw_w_out", w_out[0], r_out, m_w_out[0], v_w_out[0])

    (flat_g,) = _all_gather("gather_small_grads", [flat], deps=(g_out,))
    tot = _sum_devices(flat_g)
    dm_cols = lax.dynamic_slice_in_dim(flat_g[:, 0, :6 * D_MODEL], rank * 1536, 1536, axis=1)
    (r_in,) = _exchange_wait("grad_in_wait", *x_in[:4], tot)
    in_t = _adamw_slabs("adamw_w_in", w_in_t, r_in, m_in_t, v_in_t)
    g_in, d_in, nm_in, nv_in = [a.T for a in in_t]
    g_ada, d_ada, nm_ada, nv_ada = _adamw_ada(w_ada[0], c_act.T, dm_cols, m_w_ada[0], v_w_ada[0], deps=(in_t[0],))

    o = 0
    seg = {}
    for name, n in (("b_ada", 12288), ("g_pre_mix", 2048), ("g_post_mix", 2048), ("g_fox_out", 1024),
                    ("g_pre_mlp", 2048), ("g_post_mlp", 2048), ("b_gla_a2", 512), ("w_gla_a2", 8192),
                    ("g_gla_out", 1024), ("b_fgate", 128)):
        seg[name] = tot[:, o:o + n]
        o += n
    g_wa2 = lax.dynamic_slice_in_dim(seg["w_gla_a2"].reshape(GLA_RANK, 512), rank * 64, 64, axis=1)
    g_ggla = lax.dynamic_slice_in_dim(seg["g_gla_out"].reshape(GLA_HEADS, GLA_DV), rank * 32, 32, axis=1)
    small_names = ["b_ada", "g_pre_mix", "g_post_mix", "g_fox_out", "g_pre_mlp", "g_post_mlp", "b_gla_a2",
                   "w_gla_a2", "g_gla_out", "b_fgate"]
    small_grads = {**seg, "w_gla_a2": g_wa2.reshape(1, 1024), "g_gla_out": g_ggla.reshape(1, 128)}
    weights = dict(b_ada=b_ada, g_pre_mix=g_pre_mix, g_post_mix=g_post_mix, g_fox_out=g_fox_out,
                   g_pre_mlp=g_pre_mlp, g_post_mlp=g_post_mlp, b_gla_a2=b_gla_a2, w_gla_a2=w_gla_a2,
                   g_gla_out=g_gla_out, b_fgate=b_fgate)
    moms = dict(b_ada=m_b_ada, g_pre_mix=m_g_pre_mix, g_post_mix=m_g_post_mix, g_fox_out=m_g_fox_out,
                g_pre_mlp=m_g_pre_mlp, g_post_mlp=m_g_post_mlp, b_gla_a2=m_b_gla_a2, w_gla_a2=m_w_gla_a2,
                g_gla_out=m_g_gla_out, b_fgate=m_b_fgate)
    vels = dict(b_ada=v_b_ada, g_pre_mix=v_g_pre_mix, g_post_mix=v_g_post_mix, g_fox_out=v_g_fox_out,
                g_pre_mlp=v_g_pre_mlp, g_post_mlp=v_g_post_mlp, b_gla_a2=v_b_gla_a2, w_gla_a2=v_w_gla_a2,
                g_gla_out=v_g_gla_out, b_fgate=v_b_fgate)

    def flatten(d, fill):
        parts = []
        for nm in small_names:
            p = d[nm].reshape(1, -1)
            if nm == "b_fgate":
                p = jnp.concatenate([p[:, :FOX_HEADS], jnp.full((1, 128 - FOX_HEADS), fill, F32)], axis=1)
            parts.append(p)
        return jnp.concatenate(parts, axis=1).reshape(-1, 128)

    fw, fg, fm, fv = flatten(weights, 0.0), flatten(small_grads, 0.0), flatten(moms, 0.0), flatten(vels, 1.0)
    fd, fnm, fnv = _adamw_flat(fw, fg, fm, fv)

    def unflatten(fl):
        fl = fl.reshape(1, -1)
        out = {}
        o = 0
        for nm in small_names:
            n = 128 if nm == "b_fgate" else weights[nm].size
            piece = fl[:, o:o + n]
            if nm == "b_fgate":
                piece = piece[:, :FOX_HEADS]
            out[nm] = piece.reshape(weights[nm].shape)
            o += n
        return out

    sg, sd, snm, snv = unflatten(fg), unflatten(fd), unflatten(fnm), unflatten(fnv)

    big = dict(w_ada=(g_ada, d_ada, nm_ada, nv_ada), w_in=(g_in, d_in, nm_in, nv_in),
               w_out=(g_out, d_out, nm_out, nv_out), w_mlp_in=(g_mi, d_mi, nm_mi, nv_mi),
               w_mlp_out=(g_mo, d_mo, nm_mo, nv_mo))
    order = ["w_ada", "b_ada", "g_pre_mix", "g_post_mix", "w_in", "b_fgate", "w_gla_a2", "b_gla_a2", "g_fox_out",
             "g_gla_out", "w_out", "g_pre_mlp", "g_post_mlp", "w_mlp_in", "w_mlp_out"]

    def pick(nm, idx):
        if nm in big:
            return big[nm][idx][None]
        return (sg, sd, snm, snv)[idx][nm]

    grads = [pick(nm, 0) for nm in order]
    deltas = [pick(nm, 1) for nm in order]
    new_m = [pick(nm, 2) for nm in order]
    new_v = [pick(nm, 3) for nm in order]
    return (loss, grad_x[None], *grads, *deltas, *new_m, *new_v)
```

```python
import functools

import numpy as np
import jax
import jax.numpy as jnp
from jax import lax
from jax.experimental import pallas as pl
from jax.experimental.pallas import tpu as pltpu

F32 = jnp.float32
BF16 = jnp.bfloat16
MESH = pl.DeviceIdType.MESH
N_DEV = 8

D_MODEL = 2048
FOX_HEADS = 8
FOX_HEAD_DIM = 128
GLA_HEADS = 4
GLA_DK = 128
GLA_DV = 256
GLA_RANK = 16
GLA_TEMP = 16.0
CHUNK = 64
D_FF = 8192
W_MAIN = 6144
W_SMALL = 128
EPS = 1e-6
NEG = float(np.finfo(np.float32).min)

ADAM_LR = 0.001
ADAM_B1 = 0.9
ADAM_B2 = 0.999
ADAM_EPS = 1e-08
ADAM_WD = 0.01
ADAM_STEP = 10

ROW_T = 512
FOX_T = 1024
GLA_R = 1024
CUM_T = 256
VMEM_LIMIT = 56 * 1024 * 1024


def _call(body, deps=(), **kw):
    if not deps:
        return pl.pallas_call(body, **kw)
    n_in, n_dep = len(kw["in_specs"]), len(deps)

    def with_deps(*refs):
        return body(*refs[:n_in], *refs[n_in + n_dep:])

    kw["in_specs"] = [*kw["in_specs"], *[pl.BlockSpec(memory_space=pl.ANY)] * n_dep]
    call = pl.pallas_call(with_deps, **kw)
    return lambda *args: call(*args, *deps)


def _params(sem=None):
    return pltpu.CompilerParams(dimension_semantics=sem, vmem_limit_bytes=VMEM_LIMIT)


def _my_pos():
    return lax.axis_index("x"), lax.axis_index("y"), lax.axis_index("c")


def _my_rank():
    x, y, c = _my_pos()
    return 4 * x + 2 * y + c


def _all_gather(name, arrays, deps=()):
    n = len(arrays)

    def body(*refs):
        ins = refs[:n]
        outs = refs[n:2 * n]
        send_sems, recv_sems, local_sems = refs[2 * n:]
        x, y, c = _my_pos()
        me, sibling = (x, y, c), (x, y, 1 - c)
        chips = [(1 - x, y), (x, 1 - y), (1 - x, 1 - y)]

        def slot(a, px, py, pc):
            return outs[a].at[4 * px + 2 * py + pc]

        def copy(a, k, block, to, src=None):
            return pltpu.make_async_remote_copy(
                src_ref=slot(a, *block) if src is None else src, dst_ref=slot(a, *block),
                send_sem=send_sems.at[a, k], recv_sem=recv_sems.at[a, k],
                device_id=to, device_id_type=MESH)

        started = []
        for a in range(n):
            mine = pltpu.make_async_copy(ins[a], slot(a, *me), local_sems.at[a])
            mine.start()
            started.append(mine)
        first = []
        for a in range(n):
            first.append(copy(a, 0, me, sibling, src=ins[a]))
            first += [copy(a, 1 + j, me, (*chip, c), src=ins[a]) for j, chip in enumerate(chips)]
        for cp in first:
            cp.start()
        passed = []
        for j, chip in enumerate(chips):
            for a in range(n):
                copy(a, 1 + j, (*chip, c), me).wait_recv()
                fwd = copy(a, 4 + j, (*chip, c), sibling)
                fwd.start()
                passed.append(fwd)
        for a in range(n):
            copy(a, 0, sibling, me).wait_recv()
            for j, chip in enumerate(chips):
                copy(a, 4 + j, (*chip, 1 - c), me).wait_recv()
        for cp in first + passed:
            cp.wait_send()
        for mine in started:
            mine.wait()

    hbm = pl.BlockSpec(memory_space=pltpu.HBM)
    return _call(
        body, deps=deps, name=name,
        out_shape=[jax.ShapeDtypeStruct((N_DEV,) + a.shape, a.dtype) for a in arrays],
        in_specs=[hbm] * n, out_specs=[hbm] * n,
        scratch_shapes=[pltpu.SemaphoreType.DMA((n, 7)), pltpu.SemaphoreType.DMA((n, 7)),
                        pltpu.SemaphoreType.DMA((n,))],
    )(*arrays)


def _plane_tiles(rr, cc, tr=512, tc=512):
    if rr % 8 == 0:
        tr = _tile(rr, tr)
        return (rr // tr, pl.BlockSpec((tr, cc), lambda i: (i, 0)),
                pl.BlockSpec((N_DEV, tr, cc), lambda i: (0, i, 0)))
    tc = _tile(cc, tc)
    return (cc // tc, pl.BlockSpec((rr, tc), lambda i: (0, i)),
            pl.BlockSpec((N_DEV, rr, tc), lambda i: (0, 0, i)))


def _own_slot(name, src, gather, rank):
    shape = ((N_DEV,) + src.shape) if gather else src.shape
    rr, cc = shape[1], shape[2]
    by_rows = rr % 8 == 0
    tr, tc = (_tile(rr, 512), cc) if by_rows else (rr, _tile(cc, 512))
    steps = rr // tr if by_rows else cc // tc

    def body(rank_ref, s_ref, o_ref):
        o_ref[...] = s_ref[...].astype(o_ref.dtype)

    def at(i):
        return (i, 0) if by_rows else (0, i)

    if gather:
        in_spec = pl.BlockSpec((tr, tc), lambda i, rk: at(i))
    else:
        in_spec = pl.BlockSpec((None, tr, tc), lambda i, rk: (rk[0], *at(i)))
    grid_spec = pltpu.PrefetchScalarGridSpec(
        num_scalar_prefetch=1, grid=(steps,), in_specs=[in_spec],
        out_specs=pl.BlockSpec((None, tr, tc), lambda i, rk: (rk[0], *at(i))))
    return _call(body, name=name, grid_spec=grid_spec, out_shape=jax.ShapeDtypeStruct(shape, BF16),
                 compiler_params=_params(("arbitrary",)))(jnp.reshape(rank, (1,)).astype(jnp.int32), src)


_HBM = pl.BlockSpec(memory_space=pltpu.HBM)
_SEM = pl.BlockSpec(memory_space=pltpu.SEMAPHORE)
_FLIPS = [(kx, ky, kc) for kx in (0, 1) for ky in (0, 1) for kc in (0, 1)][1:]


def _peers():
    x, y, c = _my_pos()
    out = []
    for kx, ky, kc in _FLIPS:
        px, py, pc = (1 - x if kx else x), (1 - y if ky else y), (1 - c if kc else c)
        out.append(((px, py, pc), 4 * px + 2 * py + pc))
    return out


def _exchange_copy(srcs, lands, send_sems, recv_sems, a, k, peer, peer_rank, slot):
    return pltpu.make_async_remote_copy(
        src_ref=lands[a].at[slot] if srcs is None else srcs[a].at[peer_rank],
        dst_ref=lands[a].at[slot],
        send_sem=send_sems[a].at[k], recv_sem=recv_sems[a].at[k],
        device_id=peer, device_id_type=MESH)


def _exchange_start(name, lands, srcs=None, after=()):
    n = len(lands)
    n_src = 0 if srcs is None else n
    n_in = n + n_src + len(after)

    def body(*refs):
        lnd = refs[:n]
        src = None if srcs is None else refs[n:2 * n]
        send_sems, recv_sems = refs[n_in:n_in + n], refs[n_in + n:n_in + 2 * n]
        token = refs[-1]
        me = _my_rank()
        for a in range(n):
            for k, (peer, peer_rank) in enumerate(_peers()):
                _exchange_copy(src, lnd, send_sems, recv_sems, a, k, peer, peer_rank, me).start()
        token[...] = jnp.zeros_like(token)

    sems = [pltpu.SemaphoreType.DMA((7,))] * (2 * n)
    thru = list(lands) + ([] if srcs is None else list(srcs))
    outs = pl.pallas_call(
        body, name=name,
        out_shape=(*sems, *[pltpu.HBM(t.shape, t.dtype) for t in thru], jax.ShapeDtypeStruct((8, 128), F32)),
        in_specs=[*[_HBM] * len(thru), *[pl.BlockSpec(memory_space=pl.ANY)] * len(after)],
        out_specs=(*[_SEM] * (2 * n), *[_HBM] * len(thru), pl.BlockSpec(memory_space=pltpu.VMEM)),
        input_output_aliases={i: 2 * n + i for i in range(len(thru))},
        compiler_params=pltpu.CompilerParams(has_side_effects=pltpu.SideEffectType.DATAFLOW_SIDE_EFFECTING),
    )(*[pltpu.with_memory_space_constraint(t, pltpu.HBM) for t in thru], *after)
    lands_thru = outs[2 * n:3 * n]
    srcs_thru = None if srcs is None else outs[3 * n:4 * n]
    return outs[:n], outs[n:2 * n], srcs_thru, lands_thru, outs[-1]


def _exchange_wait(name, send_sems, recv_sems, srcs, lands, after):
    n = len(lands)
    thru = list(lands) + ([] if srcs is None else list(srcs))

    def body(*refs):
        lnd = refs[:n]
        src = None if srcs is None else refs[n:2 * n]
        ssem, rsem = refs[len(thru):len(thru) + n], refs[len(thru) + n:len(thru) + 2 * n]
        for a in range(n):
            for k, (peer, peer_rank) in enumerate(_peers()):
                cp = _exchange_copy(src, lnd, ssem, rsem, a, k, peer, peer_rank, peer_rank)
                cp.wait_send()
                cp.wait_recv()

    outs = pl.pallas_call(
        body, name=name,
        out_shape=tuple(pltpu.HBM(t.shape, t.dtype) for t in thru),
        in_specs=[*[_HBM] * len(thru), *[_SEM] * (2 * n), pl.BlockSpec(memory_space=pl.ANY)],
        out_specs=tuple([_HBM] * len(thru)),
        input_output_aliases={i: i for i in range(len(thru))},
        compiler_params=pltpu.CompilerParams(has_side_effects=pltpu.SideEffectType.DATAFLOW_SIDE_EFFECTING),
    )(*thru, *send_sems, *recv_sems, after)
    return outs[:n]


_SIDE = pltpu.CompilerParams(has_side_effects=pltpu.SideEffectType.DATAFLOW_SIDE_EFFECTING)
_ANY = pl.BlockSpec(memory_space=pl.ANY)


def _chips():
    x, y, _ = _my_pos()
    return [(1 - x, y), (x, 1 - y), (1 - x, 1 - y)]


def _slot_copy(lnd, slot, send_sem, recv_sem, to):
    return pltpu.make_async_remote_copy(src_ref=lnd.at[slot], dst_ref=lnd.at[slot], send_sem=send_sem,
                                        recv_sem=recv_sem, device_id=to, device_id_type=MESH)


def _gather2_start(name, lands, after=()):
    n = len(lands)
    n_in = n + len(after)

    def body(*refs):
        lnd = refs[:n]
        send, recv_sib, recv_ici = refs[n_in:n_in + n], refs[n_in + n:n_in + 2 * n], refs[n_in + 2 * n:n_in + 3 * n]
        x, y, c = _my_pos()
        me = 4 * x + 2 * y + c
        for a in range(n):
            _slot_copy(lnd[a], me, send[a].at[0], recv_sib[a].at[0], (x, y, 1 - c)).start()
            for j, chip in enumerate(_chips()):
                _slot_copy(lnd[a], me, send[a].at[1 + j], recv_ici[a].at[j], (*chip, c)).start()
        refs[-1][...] = jnp.zeros_like(refs[-1])

    sems = [pltpu.SemaphoreType.DMA((4,))] * n + [pltpu.SemaphoreType.DMA((1,))] * n + [pltpu.SemaphoreType.DMA((3,))] * n
    outs = pl.pallas_call(
        body, name=name,
        out_shape=(*sems, *[pltpu.HBM(t.shape, t.dtype) for t in lands], jax.ShapeDtypeStruct((8, 128), F32)),
        in_specs=[*[_HBM] * n, *[_ANY] * len(after)],
        out_specs=(*[_SEM] * (3 * n), *[_HBM] * n, pl.BlockSpec(memory_space=pltpu.VMEM)),
        input_output_aliases={i: 3 * n + i for i in range(n)}, compiler_params=_SIDE,
    )(*[pltpu.with_memory_space_constraint(t, pltpu.HBM) for t in lands], *after)
    return outs[:n], outs[n:2 * n], outs[2 * n:3 * n], outs[3 * n:4 * n], outs[-1]


def _gather2_forward(name, lands, recv_ici, after):
    n = len(lands)

    def body(*refs):
        lnd, arrived = refs[:n], refs[n:2 * n]
        send, recv = refs[2 * n + 1:3 * n + 1], refs[3 * n + 1:4 * n + 1]
        x, y, c = _my_pos()
        for j, (cx, cy) in enumerate(_chips()):
            slot = 4 * cx + 2 * cy + c
            for a in range(n):
                _slot_copy(lnd[a], slot, send[a].at[j], arrived[a].at[j], (cx, cy, c)).wait_recv()
                _slot_copy(lnd[a], slot, send[a].at[j], recv[a].at[j], (x, y, 1 - c)).start()
        refs[-1][...] = jnp.zeros_like(refs[-1])

    sems = [pltpu.SemaphoreType.DMA((3,))] * (2 * n)
    outs = pl.pallas_call(
        body, name=name,
        out_shape=(*sems, *[pltpu.HBM(t.shape, t.dtype) for t in lands], jax.ShapeDtypeStruct((8, 128), F32)),
        in_specs=[*[_HBM] * n, *[_SEM] * n, _ANY],
        out_specs=(*[_SEM] * (2 * n), *[_HBM] * n, pl.BlockSpec(memory_space=pltpu.VMEM)),
        input_output_aliases={i: 2 * n + i for i in range(n)}, compiler_params=_SIDE,
    )(*lands, *recv_ici, after)
    return outs[:n], outs[n:2 * n], outs[2 * n:3 * n], outs[-1]


def _gather2_wait(name, lands, send_a, recv_sib, send_b, recv_b, after):
    n = len(lands)

    def body(*refs):
        lnd = refs[:n]
        sa, rs, sb, rb = (refs[(1 + i) * n:(2 + i) * n] for i in range(4))
        x, y, c = _my_pos()
        me = 4 * x + 2 * y + c
        for a in range(n):
            for k in range(4):
                _slot_copy(lnd[a], me, sa[a].at[k], rs[a].at[0], (x, y, 1 - c)).wait_send()
            _slot_copy(lnd[a], me - c + (1 - c), sa[a].at[0], rs[a].at[0], (x, y, 1 - c)).wait_recv()
            for j, (cx, cy) in enumerate(_chips()):
                _slot_copy(lnd[a], 4 * cx + 2 * cy + c, sb[a].at[j], rb[a].at[j], (x, y, 1 - c)).wait_send()
                _slot_copy(lnd[a], 4 * cx + 2 * cy + (1 - c), sb[a].at[j], rb[a].at[j], (x, y, 1 - c)).wait_recv()

    outs = pl.pallas_call(
        body, name=name,
        out_shape=tuple(pltpu.HBM(t.shape, t.dtype) for t in lands),
        in_specs=[*[_HBM] * n, *[_SEM] * (4 * n), _ANY],
        out_specs=tuple([_HBM] * n),
        input_output_aliases={i: i for i in range(n)}, compiler_params=_SIDE,
    )(*lands, *send_a, *recv_sib, *send_b, *recv_b, after)
    return outs


NN = ((1,), (0,))
NT = ((1,), (1,))
TN = ((0,), (0,))


def _matmul(name, a, b, *, contract, grid, a_spec, b_spec, out_specs, out_shapes, acc_shape,
            extra=(), extra_specs=(), epilogue=None, deps=()):
    nk = grid[2]
    n_extra = len(extra)
    n_out = len(out_shapes)

    def body(*refs):
        a_ref, b_ref = refs[0], refs[1]
        extra_refs = refs[2:2 + n_extra]
        out_refs = refs[2 + n_extra:2 + n_extra + n_out]
        acc_ref = refs[-1]
        k = pl.program_id(2)

        def prod():
            if len(b_ref.shape) == 2:
                return lax.dot_general(a_ref[...].astype(BF16), b_ref[...].astype(BF16), (contract, ((), ())),
                                       preferred_element_type=F32)
            kk = a_ref.shape[1] // b_ref.shape[0]
            acc = None
            for u in range(b_ref.shape[0]):
                part = lax.dot_general(a_ref[:, u * kk:(u + 1) * kk].astype(BF16), b_ref[u].astype(BF16),
                                       (contract, ((), ())), preferred_element_type=F32)
                acc = part if acc is None else acc + part
            return acc

        def finish(acc):
            res = (acc,) if epilogue is None else epilogue(acc, *[r[...] for r in extra_refs])
            for o_ref, val in zip(out_refs, res):
                o_ref[...] = val.astype(o_ref.dtype)

        if nk == 1:
            finish(prod())
            return

        @pl.when(k == 0)
        def _():
            acc_ref[...] = prod()

        @pl.when((k > 0) & (k < nk - 1))
        def _():
            acc_ref[...] += prod()

        @pl.when(k == nk - 1)
        def _():
            finish(acc_ref[...] + prod())

    outs = _call(
        body, deps=deps, name=name, grid=grid,
        in_specs=[a_spec, b_spec, *extra_specs], out_specs=list(out_specs), out_shape=list(out_shapes),
        scratch_shapes=[pltpu.VMEM(acc_shape if nk > 1 else (8, 128), F32)],
        compiler_params=_params(("parallel", "parallel", "arbitrary")),
    )(a, b, *extra)
    return outs


def _tile(n, t):
    t = min(n, t)
    assert n % t == 0, (n, t)
    return t


def _mm_plain(name, a, b, contract, out_dtype, tm=1024, tn=1024, tk=2048, extra=(), epilogue=None,
              n_out=1, out_dtypes=None, deps=()):
    if contract == NN:
        (m, kd), (_, n) = a.shape, b.shape
    elif contract == NT:
        (m, kd), (n, _) = a.shape, b.shape
    else:
        (kd, m), (_, n) = a.shape, b.shape
    tm, tn, tk = _tile(m, tm), _tile(n, tn), _tile(kd, tk)
    if contract == NN:
        a_spec = pl.BlockSpec((tm, tk), lambda i, j, k: (i, k))
        b_spec = pl.BlockSpec((tk, tn), lambda i, j, k: (k, j))
    elif contract == NT:
        a_spec = pl.BlockSpec((tm, tk), lambda i, j, k: (i, k))
        b_spec = pl.BlockSpec((tn, tk), lambda i, j, k: (j, k))
    else:
        a_spec = pl.BlockSpec((tk, tm), lambda i, j, k: (k, i))
        b_spec = pl.BlockSpec((tk, tn), lambda i, j, k: (k, j))
    o_spec = pl.BlockSpec((tm, tn), lambda i, j, k: (i, j))
    out_dtypes = out_dtypes or [out_dtype] * n_out
    outs = _matmul(
        name, a, b, contract=contract, grid=(m // tm, n // tn, kd // tk), a_spec=a_spec, b_spec=b_spec,
        out_specs=[o_spec] * len(out_dtypes), out_shapes=[jax.ShapeDtypeStruct((m, n), dt) for dt in out_dtypes],
        acc_shape=(tm, tn), extra=extra, extra_specs=[o_spec] * len(extra), epilogue=epilogue, deps=deps)
    return outs[0] if len(out_dtypes) == 1 else outs


def _rows_call(name, body, row_in, vec_in, row_out, vec_out, s, deps=()):
    t = _tile(s, ROW_T)
    in_specs = []
    args = []
    for arr, width, cb in row_in:
        in_specs.append(pl.BlockSpec((t, width), functools.partial(lambda i, cb: (i, cb), cb=cb)))
        args.append(arr)
    for v in vec_in:
        in_specs.append(pl.BlockSpec(v.shape, lambda i: (0, 0)))
        args.append(v)
    out_specs = []
    out_shapes = []
    for width, dt in row_out:
        out_specs.append(pl.BlockSpec((t, width), lambda i: (i, 0)))
        out_shapes.append(jax.ShapeDtypeStruct((s, width), dt))
    for width in vec_out:
        out_specs.append(pl.BlockSpec((1, width), lambda i: (0, 0)))
        out_shapes.append(jax.ShapeDtypeStruct((1, width), F32))
    return _call(body, deps=deps, name=name, grid=(s // t,), in_specs=in_specs, out_specs=out_specs,
                 out_shape=out_shapes, compiler_params=_params(("arbitrary",)))(*args)


def _acc_vec(ref, val):
    _acc_row(ref, jnp.sum(val, axis=0, keepdims=True))


def _acc_row(ref, part):
    @pl.when(pl.program_id(0) == 0)
    def _():
        ref[...] = part

    @pl.when(pl.program_id(0) > 0)
    def _():
        ref[...] += part


def _rms(v):
    return lax.rsqrt(jnp.mean(v * v, axis=-1, keepdims=True) + EPS)


def _norm_bwd(dxn, xn, r):
    return r * (dxn - xn * jnp.mean(dxn * xn, axis=-1, keepdims=True))


def _premix(x, g, scale, shift, deps=()):
    s = x.shape[0]

    def body(x_ref, g_ref, sc_ref, sh_ref, h_ref):
        xv = x_ref[...]
        h_ref[...] = ((xv * _rms(xv) * g_ref[...]) * (1.0 + sc_ref[...]) + sh_ref[...]).astype(BF16)

    return _rows_call("premix", body, [(x, D_MODEL, 0)], [g, scale, shift], [(D_MODEL, BF16)], [], s, deps)[0]


def _sigmoid(z):
    return 1.0 / (1.0 + jnp.exp(-z))


def _mix_fwd(o_fox, o_gla, pm, g_fox, g_gla):
    s = o_fox.shape[0]

    def body(of_ref, og_ref, gr_ref, gf_ref, gg_ref, mix_ref):
        for h in range(FOX_HEADS):
            sl = slice(h * FOX_HEAD_DIM, (h + 1) * FOX_HEAD_DIM)
            seg = of_ref[:, sl]
            mix_ref[:, sl] = (seg * _rms(seg) * gf_ref[:, sl]).astype(BF16)
        for h in range(GLA_HEADS):
            sl = slice(h * GLA_DV, (h + 1) * GLA_DV)
            seg = og_ref[:, sl]
            gr = gr_ref[:, sl].astype(F32)
            val = (seg * _rms(seg) * gg_ref[:, sl]) * (gr * _sigmoid(gr))
            mix_ref[:, pl.ds(FOX_HEADS * FOX_HEAD_DIM + h * GLA_DV, GLA_DV)] = val.astype(BF16)

    return _rows_call("mix_fwd", body, [(o_fox, 1024, 0), (o_gla, 1024, 0), (pm, 1024, 5)], [g_fox, g_gla],
                      [(D_MODEL, BF16)], [], s)[0]


def _mix_bwd(dmix, o_fox, o_gla, pm, g_fox, g_gla, deps=()):
    s = o_fox.shape[0]

    def body(dm_ref, of_ref, og_ref, gr_ref, gf_ref, gg_ref, dof_ref, dog_ref, dgr_ref, dgf_ref, dgg_ref):
        dgf = []
        for h in range(FOX_HEADS):
            sl = slice(h * FOX_HEAD_DIM, (h + 1) * FOX_HEAD_DIM)
            seg = of_ref[:, sl]
            r = _rms(seg)
            segn = seg * r
            dout = dm_ref[:, sl].astype(F32)
            dgf.append(jnp.sum(dout * segn, axis=0, keepdims=True))
            dof_ref[:, sl] = _norm_bwd(dout * gf_ref[:, sl], segn, r).astype(BF16)
        dgg = []
        for h in range(GLA_HEADS):
            sl = slice(h * GLA_DV, (h + 1) * GLA_DV)
            seg = og_ref[:, sl]
            r = _rms(seg)
            segn = seg * r
            gl = segn * gg_ref[:, sl]
            gr = gr_ref[:, sl].astype(F32)
            sig = _sigmoid(gr)
            dout = dm_ref[:, pl.ds(FOX_HEADS * FOX_HEAD_DIM + h * GLA_DV, GLA_DV)].astype(F32)
            dgr_ref[:, sl] = (dout * gl * (sig * (1.0 + gr * (1.0 - sig)))).astype(BF16)
            dgl = dout * (gr * sig)
            dgg.append(jnp.sum(dgl * segn, axis=0, keepdims=True))
            dog_ref[:, sl] = _norm_bwd(dgl * gg_ref[:, sl], segn, r).astype(BF16)
        _acc_row(dgf_ref, jnp.concatenate(dgf, axis=1))
        _acc_row(dgg_ref, jnp.concatenate(dgg, axis=1))

    return _rows_call("mix_bwd", body, [(dmix, D_MODEL, 0), (o_fox, 1024, 0), (o_gla, 1024, 0), (pm, 1024, 5)],
                      [g_fox, g_gla], [(1024, BF16), (1024, BF16), (1024, BF16)], [1024, 1024], s, deps)


def _postmix_premlp(x, y, gate_m, g_post_mix, g_pre_mlp, scale_f, shift_f):
    s = x.shape[0]

    def body(x_ref, y_ref, gm_ref, gpm_ref, gpl_ref, sc_ref, sh_ref, x1_ref, h2_ref):
        yv = y_ref[...].astype(F32)
        x1 = x_ref[...] + gm_ref[...] * (yv * _rms(yv) * gpm_ref[...])
        x1_ref[...] = x1
        h2_ref[...] = ((x1 * _rms(x1) * gpl_ref[...]) * (1.0 + sc_ref[...]) + sh_ref[...]).astype(BF16)

    return _rows_call("postmix_premlp", body, [(x, D_MODEL, 0), (y, D_MODEL, 0)],
                      [gate_m, g_post_mix, g_pre_mlp, scale_f, shift_f], [(D_MODEL, F32), (D_MODEL, BF16)], [], s)


def _loss_postmlp_bwd(x1, y2, target, gate_f, g_post_mlp):
    s = x1.shape[0]

    def body(x1_ref, y2_ref, t_ref, gf_ref, g_ref, dx2_ref, dy2_ref, loss_ref, dgate_ref, dg_ref):
        yv = y2_ref[...].astype(F32)
        r = _rms(yv)
        yn = yv * r
        o = yn * g_ref[...]
        e = (x1_ref[...] + gf_ref[...] * o) - t_ref[...]
        part = 0.5 * jnp.sum(jnp.mean(e * e, axis=-1, keepdims=True), axis=0, keepdims=True)
        _acc_vec(loss_ref, jnp.broadcast_to(part, (1, 128)))
        dx2 = e * (1.0 / D_MODEL)
        dx2_ref[...] = dx2
        _acc_vec(dgate_ref, dx2 * o)
        do = dx2 * gf_ref[...]
        _acc_vec(dg_ref, do * yn)
        dy2_ref[...] = _norm_bwd(do * g_ref[...], yn, r).astype(BF16)

    return _rows_call("loss_postmlp_bwd", body, [(x1, D_MODEL, 0), (y2, D_MODEL, 0), (target, D_MODEL, 0)],
                      [gate_f, g_post_mlp], [(D_MODEL, F32), (D_MODEL, BF16)], [128, D_MODEL, D_MODEL], s)


def _premlp_postmix_bwd(dh2, dx2, x1, y, scale_f, g_pre_mlp, gate_m, g_post_mix, deps=()):
    s = x1.shape[0]

    def body(dh2_ref, dx2_ref, x1_ref, y_ref, sc_ref, gpl_ref, gm_ref, gpm_ref,
             dx1_ref, dy_ref, dsc_ref, dsh_ref, dgpl_ref, dgm_ref, dgpm_ref):
        x1 = x1_ref[...]
        r1 = _rms(x1)
        x1n = x1 * r1
        dh2 = dh2_ref[...].astype(F32)
        _acc_vec(dsc_ref, dh2 * (x1n * gpl_ref[...]))
        _acc_vec(dsh_ref, dh2)
        dn2 = dh2 * (1.0 + sc_ref[...])
        _acc_vec(dgpl_ref, dn2 * x1n)
        dx1 = dx2_ref[...] + _norm_bwd(dn2 * gpl_ref[...], x1n, r1)
        dx1_ref[...] = dx1
        yv = y_ref[...].astype(F32)
        ry = _rms(yv)
        yn = yv * ry
        _acc_vec(dgm_ref, dx1 * (yn * gpm_ref[...]))
        do = dx1 * gm_ref[...]
        _acc_vec(dgpm_ref, do * yn)
        dy_ref[...] = _norm_bwd(do * gpm_ref[...], yn, ry).astype(BF16)

    return _rows_call("premlp_postmix_bwd", body,
                      [(dh2, D_MODEL, 0), (dx2, D_MODEL, 0), (x1, D_MODEL, 0), (y, D_MODEL, 0)],
                      [scale_f, g_pre_mlp, gate_m, g_post_mix], [(D_MODEL, F32), (D_MODEL, BF16)],
                      [D_MODEL] * 5, s, deps)


def _premix_bwd(dh, dx1, x, g_pre_mix, scale_m):
    s = x.shape[0]

    def body(dh_ref, dx1_ref, x_ref, g_ref, sc_ref, gx_ref, dsc_ref, dsh_ref, dg_ref):
        xv = x_ref[...]
        r = _rms(xv)
        xn = xv * r
        dh = dh_ref[...].astype(F32)
        _acc_vec(dsc_ref, dh * (xn * g_ref[...]))
        _acc_vec(dsh_ref, dh)
        dn1 = dh * (1.0 + sc_ref[...])
        _acc_vec(dg_ref, dn1 * xn)
        gx_ref[...] = dx1_ref[...] + _norm_bwd(dn1 * g_ref[...], xn, r)

    return _rows_call("premix_bwd", body, [(dh, D_MODEL, 0), (dx1, D_MODEL, 0), (x, D_MODEL, 0)],
                      [g_pre_mix, scale_m], [(D_MODEL, F32)], [D_MODEL] * 3, s)


def _split3(v):
    hi = v.astype(BF16)
    r1 = v - hi.astype(F32)
    mid = r1.astype(BF16)
    lo = (r1 - mid.astype(F32)).astype(BF16)
    return hi, mid, lo


def _dot_exact01(v, tri, contract=NN, tri_first=False):
    acc = None
    for part in _split3(v):
        lhs, rhs = (tri, part) if tri_first else (part, tri)
        p = lax.dot_general(lhs, rhs, (contract, ((), ())), preferred_element_type=F32)
        acc = p if acc is None else acc + p
    return acc


def _log_sigmoid(z):
    return jnp.minimum(z, 0.0) - jnp.log(1.0 + jnp.exp(-jnp.abs(z)))


def _fox_cum(small, bvec):
    s = small.shape[0]
    t = _tile(s, CUM_T)

    def body(sm_ref, b_ref, out_ref, carry):
        @pl.when(pl.program_id(0) == 0)
        def _():
            carry[...] = jnp.zeros_like(carry)

        lf = _log_sigmoid(sm_ref[...] + b_ref[...])
        lft = lf.T[0:FOX_HEADS, :]
        row = lax.broadcasted_iota(jnp.int32, (t, t), 0)
        col = lax.broadcasted_iota(jnp.int32, (t, t), 1)
        upper = (row <= col).astype(BF16)
        cum = _dot_exact01(lft, upper) + carry[:, 0:1]
        out_ref[...] = cum
        carry[...] = carry[...] + jnp.sum(lft, axis=1, keepdims=True)

    return _call(body, name="fox_cum", grid=(s // t,),
                 in_specs=[pl.BlockSpec((t, W_SMALL), lambda i: (i, 0)), pl.BlockSpec((1, W_SMALL), lambda i: (0, 0))],
                 out_specs=pl.BlockSpec((FOX_HEADS, t), lambda i: (0, i)),
                 out_shape=jax.ShapeDtypeStruct((FOX_HEADS, s), F32),
                 scratch_shapes=[pltpu.VMEM((FOX_HEADS, 128), F32)],
                 compiler_params=_params(("arbitrary",)))(small, bvec)


def _fox_cum_bwd(dc, dcq, small, bvec):
    s = small.shape[0]
    t = _tile(s, CUM_T)
    nb = s // t

    def body(dc_ref, dcq_ref, sm_ref, b_ref, out_ref, db_ref, carry):
        @pl.when(pl.program_id(0) == 0)
        def _():
            carry[...] = jnp.zeros_like(carry)
            db_ref[...] = jnp.zeros_like(db_ref)

        lane = lax.broadcasted_iota(jnp.int32, (t, W_SMALL), 1)
        dcq = jnp.zeros((t, W_SMALL), F32)
        for hh in range(FOX_HEADS):
            dcq = jnp.where(lane == hh, dcq_ref[hh], dcq)
        dcv = dc_ref[...] + dcq.T[0:FOX_HEADS, :]
        row = lax.broadcasted_iota(jnp.int32, (t, t), 0)
        col = lax.broadcasted_iota(jnp.int32, (t, t), 1)
        lower = (row >= col).astype(BF16)
        dlf = _dot_exact01(dcv, lower) + carry[:, 0:1]
        carry[...] = carry[...] + jnp.sum(dcv, axis=1, keepdims=True)
        z = sm_ref[...] + b_ref[...]
        zt = z.T[0:FOX_HEADS, :]
        dff = dlf * _sigmoid(-zt)
        db_ref[...] = db_ref[...] + jnp.sum(dff, axis=1, keepdims=True)
        full = jnp.concatenate([dff, jnp.zeros((W_SMALL - FOX_HEADS, t), F32)], axis=0)
        out_ref[...] = full.T

    return _call(body, name="fox_cum_bwd", grid=(nb,),
                 in_specs=[pl.BlockSpec((FOX_HEADS, t), lambda i: (0, nb - 1 - i)),
                           pl.BlockSpec((FOX_HEADS, t, 1), lambda i: (0, nb - 1 - i, 0)),
                           pl.BlockSpec((t, W_SMALL), lambda i: (nb - 1 - i, 0)),
                           pl.BlockSpec((1, W_SMALL), lambda i: (0, 0))],
                 out_specs=[pl.BlockSpec((t, W_SMALL), lambda i: (nb - 1 - i, 0)),
                            pl.BlockSpec((FOX_HEADS, 128), lambda i: (0, 0))],
                 out_shape=[jax.ShapeDtypeStruct((s, W_SMALL), F32), jax.ShapeDtypeStruct((FOX_HEADS, 128), F32)],
                 scratch_shapes=[pltpu.VMEM((FOX_HEADS, 128), F32)],
                 compiler_params=_params(("arbitrary",)))(dc, dcq, small, bvec)


FOX_SCALE = FOX_HEAD_DIM ** -0.5


def _fox_fwd(pm, crow):
    s = pm.shape[0]
    t = _tile(s, FOX_T)
    nb = s // t
    parts = 2
    hq = t // parts

    def body(q_ref, k_ref, v_ref, c_ref, o_ref, lse_ref):
        i = pl.program_id(1)
        qs = [q_ref[g * hq:(g + 1) * hq, :] for g in range(parts)]

        def block(j, carry, diagonal):
            rows = pl.ds(pl.multiple_of(j * t, t), t)
            k_all, v_all, c_all = k_ref[rows, :], v_ref[rows, :], c_ref[j]
            out = []
            for g, (m_prev, l_prev, acc) in enumerate(carry):
                nk = (g + 1) * hq if diagonal else t
                kb, vb, cb = k_all[:nk], v_all[:nk], c_all[:, :nk]
                sc = lax.dot_general(qs[g], kb, (NT, ((), ())), preferred_element_type=F32)
                sc = sc * FOX_SCALE - cb
                if diagonal:
                    row = lax.broadcasted_iota(jnp.int32, (hq, nk), 0) + g * hq
                    col = lax.broadcasted_iota(jnp.int32, (hq, nk), 1)
                    sc = jnp.where(row >= col, sc, NEG)
                m_new = jnp.maximum(m_prev, jnp.max(sc, axis=1, keepdims=True))
                alpha = jnp.exp(m_prev - m_new)
                p = jnp.exp(sc - m_new)
                l_new = alpha * l_prev + jnp.sum(p, axis=1, keepdims=True)
                pv = jnp.dot(p.astype(BF16), vb, preferred_element_type=F32)
                out.append((m_new, l_new, alpha * acc + pv))
            return tuple(out)

        init = tuple((jnp.full((hq, 1), NEG, F32), jnp.zeros((hq, 1), F32), jnp.zeros((hq, 128), F32))
                     for _ in range(parts))
        carry = lax.fori_loop(0, i, lambda j, cr: block(j, cr, False), init)
        carry = block(i, carry, True)
        for g, (m_fin, l_fin, acc) in enumerate(carry):
            o_ref[g * hq:(g + 1) * hq, :] = acc / l_fin
            lse_ref[g * hq:(g + 1) * hq, :] = m_fin + jnp.log(l_fin)

    return _call(
        body, name="fox_fwd", grid=(FOX_HEADS, nb),
        in_specs=[pl.BlockSpec((t, 128), lambda h, i: (i, h)),
                  pl.BlockSpec((s, 128), lambda h, i: (0, FOX_HEADS + h)),
                  pl.BlockSpec((s, 128), lambda h, i: (0, 2 * FOX_HEADS + h)),
                  pl.BlockSpec((None, nb, 1, t), lambda h, i: (h, 0, 0, 0))],
        out_specs=[pl.BlockSpec((t, 128), lambda h, i: (i, h)),
                   pl.BlockSpec((None, t, 1), lambda h, i: (h, i, 0))],
        out_shape=[jax.ShapeDtypeStruct((s, FOX_HEADS * 128), F32), jax.ShapeDtypeStruct((FOX_HEADS, s, 1), F32)],
        compiler_params=_params(("parallel", "arbitrary")),
    )(pm, pm, pm, crow.reshape(FOX_HEADS, nb, 1, t))


def _fox_bwd(pm, crow, o, lse, do):
    s = pm.shape[0]
    t = _tile(s, FOX_T)
    nb = s // t

    parts = 2
    hq = t // parts

    def body(q_ref, do_ref, o_ref, lse_ref, k_ref, v_ref, c_ref, dq_ref, dk_ref, dv_ref, dc_ref, dcq_ref, delta_s):
        j = pl.program_id(1)

        @pl.when(j == 0)
        def _():
            dq_ref[...] = jnp.zeros_like(dq_ref)
            dcq_ref[...] = jnp.zeros_like(dcq_ref)
            delta_s[...] = jnp.sum(do_ref[...].astype(F32) * o_ref[...], axis=1, keepdims=True)

        k_all, v_all, c_all = k_ref[...], v_ref[...], c_ref[...]

        def grow(acc, part, axis):
            n = part.shape[axis]
            if n == acc.shape[axis]:
                return acc + part
            if axis == 0:
                return jnp.concatenate([acc[:n] + part, acc[n:]], axis=0)
            return jnp.concatenate([acc[:, :n] + part, acc[:, n:]], axis=1)

        def block(i, carry, diagonal):
            dk_acc, dv_acc, dc_acc = carry
            for g in range(parts):
                nk = (g + 1) * hq if diagonal else t
                kb, vb, cb = k_all[:nk], v_all[:nk], c_all[:, :nk]
                rows = pl.ds(pl.multiple_of(i * t + g * hq, hq), hq)
                q, dov = q_ref[rows, :], do_ref[rows, :]
                sc = lax.dot_general(q, kb, (NT, ((), ())), preferred_element_type=F32)
                p = jnp.exp(sc * FOX_SCALE - cb - lse_ref[rows, :])
                if diagonal:
                    row = lax.broadcasted_iota(jnp.int32, (hq, nk), 0) + g * hq
                    col = lax.broadcasted_iota(jnp.int32, (hq, nk), 1)
                    p = jnp.where(row >= col, p, 0.0)
                dp = lax.dot_general(dov, vb, (NT, ((), ())), preferred_element_type=F32)
                ds = p * (dp - delta_s[rows, :])
                dsb = ds.astype(BF16)
                dv_acc = grow(dv_acc, lax.dot_general(p.astype(BF16), dov, (TN, ((), ())),
                                                      preferred_element_type=F32), 0)
                dk_acc = grow(dk_acc, lax.dot_general(dsb, q, (TN, ((), ())), preferred_element_type=F32), 0)
                dq_ref[rows, :] += jnp.dot(dsb, kb, preferred_element_type=F32) * FOX_SCALE
                dc_acc = grow(dc_acc, -jnp.sum(ds, axis=0, keepdims=True), 1)
                dcq_ref[rows, :] += jnp.sum(ds, axis=1, keepdims=True)
            return dk_acc, dv_acc, dc_acc

        carry = (jnp.zeros((t, 128), F32), jnp.zeros((t, 128), F32), jnp.zeros((1, t), F32))
        carry = block(j, carry, True)
        dk_acc, dv_acc, dc_acc = lax.fori_loop(j + 1, nb, lambda i, cr: block(i, cr, False), carry)
        dk_ref[...] = (dk_acc * FOX_SCALE).astype(dk_ref.dtype)
        dv_ref[...] = dv_acc.astype(dv_ref.dtype)
        dc_ref[...] = dc_acc

    whole = lambda h, j: (0, h)
    return _call(
        body, name="fox_bwd", grid=(FOX_HEADS, nb),
        in_specs=[pl.BlockSpec((s, 128), whole), pl.BlockSpec((s, 128), whole), pl.BlockSpec((s, 128), whole),
                  pl.BlockSpec((None, s, 1), lambda h, j: (h, 0, 0)),
                  pl.BlockSpec((t, 128), lambda h, j: (j, FOX_HEADS + h)),
                  pl.BlockSpec((t, 128), lambda h, j: (j, 2 * FOX_HEADS + h)),
                  pl.BlockSpec((None, 1, t), lambda h, j: (h, 0, j))],
        out_specs=[pl.BlockSpec((s, 128), whole),
                   pl.BlockSpec((t, 128), lambda h, j: (j, h)),
                   pl.BlockSpec((t, 128), lambda h, j: (j, h)),
                   pl.BlockSpec((None, 1, t), lambda h, j: (h, 0, j)),
                   pl.BlockSpec((None, s, 1), lambda h, j: (h, 0, 0))],
        out_shape=[jax.ShapeDtypeStruct((s, 1024), F32), jax.ShapeDtypeStruct((s, 1024), BF16),
                   jax.ShapeDtypeStruct((s, 1024), BF16), jax.ShapeDtypeStruct((FOX_HEADS, 1, s), F32),
                   jax.ShapeDtypeStruct((FOX_HEADS, s, 1), F32)],
        scratch_shapes=[pltpu.VMEM((s, 1), F32)],
        compiler_params=_params(("parallel", "arbitrary")),
    )(pm, do, o, lse, pm, pm, crow)


GLA_SCALE = GLA_DK ** -0.5
GLA_Q_BLK = 3072 // 128
GLA_K_BLK = 3584 // 128
GLA_V_BLK = 4096 // 256


def _gla_gate(sm, wa_ref, b_ref):
    return jnp.dot(sm.astype(BF16), wa_ref[...], preferred_element_type=F32) + b_ref[...]


def _chunk_tri(n, kind):
    row = lax.broadcasted_iota(jnp.int32, (n, n), 0)
    col = lax.broadcasted_iota(jnp.int32, (n, n), 1)
    shift = CHUNK.bit_length() - 1
    same = (row >> shift) == (col >> shift)
    if kind == "upto":
        same = same & (row >= col)
    elif kind == "before":
        same = same & (row > col)
    return same.astype(BF16)


def _gla_fwd(pm, small, wa_pad, b_a2, deps=()):
    s = pm.shape[0]
    r = _tile(s, GLA_R)
    nc = r // CHUNK

    def body(q_ref, k_ref, v_ref, sm_ref, wa_ref, b_ref, o_ref, st_ref, state):
        @pl.when(pl.program_id(1) == 0)
        def _():
            state[...] = jnp.zeros_like(state)

        la_all = _log_sigmoid(_gla_gate(sm_ref[...], wa_ref, b_ref)) * (1.0 / GLA_TEMP)
        tri = _chunk_tri(CHUNK, "upto")
        uts, decays = [], []
        for c in range(nc):
            rows = slice(c * CHUNK, (c + 1) * CHUNK)
            la = la_all[rows]
            cum = _dot_exact01(la, tri, tri_first=True)
            total = jnp.sum(la, axis=0, keepdims=True)
            kdec = k_ref[rows, :].astype(F32) * jnp.exp(total - cum)
            uts.append(lax.dot_general(v_ref[rows, :], kdec.astype(BF16), (TN, ((), ())),
                                       preferred_element_type=F32))
            decays.append(jnp.exp(total))
        cur = state[...]
        ends = []
        for c in range(nc):
            cur = cur * decays[c] + uts[c]
            ends.append(cur.astype(BF16))
        state[...] = cur
        for c in range(nc):
            rows = slice(c * CHUNK, (c + 1) * CHUNK)
            st_ref[c] = ends[c]
            qs = (q_ref[rows, :].astype(F32) * GLA_SCALE).astype(BF16)
            o_ref[rows, :] = lax.dot_general(qs, ends[c], (NT, ((), ())), preferred_element_type=F32)

    return _call(
        body, deps=deps, name="gla_fwd", grid=(GLA_HEADS, s // r),
        in_specs=[pl.BlockSpec((r, 128), lambda h, i: (i, GLA_Q_BLK + h)),
                  pl.BlockSpec((r, 128), lambda h, i: (i, GLA_K_BLK + h)),
                  pl.BlockSpec((r, 256), lambda h, i: (i, GLA_V_BLK + h)),
                  pl.BlockSpec((r, W_SMALL), lambda h, i: (i, 0)),
                  pl.BlockSpec((W_SMALL, 128), lambda h, i: (0, h)),
                  pl.BlockSpec((1, 128), lambda h, i: (0, h))],
        out_specs=[pl.BlockSpec((r, 256), lambda h, i: (i, h)),
                   pl.BlockSpec((nc, None, GLA_DV, GLA_DK), lambda h, i: (i, h, 0, 0))],
        out_shape=[jax.ShapeDtypeStruct((s, 1024), F32),
                   jax.ShapeDtypeStruct((s // CHUNK, GLA_HEADS, GLA_DV, GLA_DK), BF16)],
        scratch_shapes=[pltpu.VMEM((GLA_DV, GLA_DK), F32)],
        compiler_params=_params(("parallel", "arbitrary")),
    )(pm, pm, pm, small, wa_pad, b_a2)


def _gla_bwd(pm, small, wa_pad, b_a2, states, do):
    s = pm.shape[0]
    r = _tile(s, GLA_R)
    nc = r // CHUNK
    nb = s // r

    def body(q_ref, k_ref, v_ref, sm_ref, wa_ref, b_ref, do_ref, st_ref, prev_ref,
             dq_ref, dk_ref, dv_ref, dza_ref, db_ref, carry):
        step = pl.program_id(1)

        @pl.when(step == 0)
        def _():
            carry[...] = jnp.zeros_like(carry)
            db_ref[...] = jnp.zeros_like(db_ref)

        z_all = _gla_gate(sm_ref[...], wa_ref, b_ref)
        la_all = _log_sigmoid(z_all) * (1.0 / GLA_TEMP)
        tri = _chunk_tri(CHUNK, "upto")
        tri_strict = _chunk_tri(CHUNK, "before")
        ws, decays, kdecs, gouts = [], [], [], []
        for c in range(nc):
            rows = slice(c * CHUNK, (c + 1) * CHUNK)
            la = la_all[rows]
            cum = _dot_exact01(la, tri, tri_first=True)
            total = jnp.sum(la, axis=0, keepdims=True)
            w = jnp.exp(total - cum)
            ws.append(w)
            decays.append(jnp.exp(total))
            kdecs.append(k_ref[rows, :].astype(F32) * w)
            dov = do_ref[rows, :]
            qs = (q_ref[rows, :].astype(F32) * GLA_SCALE).astype(BF16)
            dq_ref[rows, :] = (jnp.dot(dov, st_ref[c], preferred_element_type=F32) * GLA_SCALE).astype(BF16)
            gouts.append(lax.dot_general(dov, qs, (TN, ((), ())), preferred_element_type=F32))
        cur = carry[...]
        gts = [None] * nc
        for c in reversed(range(nc)):
            gts[c] = gouts[c] + cur
            cur = gts[c] * decays[c]
        carry[...] = cur
        db = jnp.zeros((1, 128), F32)
        for c in range(nc):
            rows = slice(c * CHUNK, (c + 1) * CHUNK)
            gtb = gts[c].astype(BF16)
            dv_ref[rows, :] = lax.dot_general(kdecs[c].astype(BF16), gtb, (NT, ((), ())),
                                              preferred_element_type=F32).astype(BF16)
            dkdec = jnp.dot(v_ref[rows, :], gtb, preferred_element_type=F32)
            dk_ref[rows, :] = (dkdec * ws[c]).astype(BF16)
            e = dkdec * kdecs[c]
            if c > 0:
                prev = st_ref[c - 1].astype(F32)
            else:
                prev = jnp.where(step == nb - 1, 0.0, prev_ref[0].astype(F32))
            dtot = jnp.sum(gts[c] * prev, axis=0, keepdims=True) * decays[c]
            dla = dtot + _dot_exact01(e, tri_strict, tri_first=True)
            dza = dla * (1.0 / GLA_TEMP) * _sigmoid(-z_all[rows])
            dza_ref[rows, :] = dza.astype(BF16)
            db = db + jnp.sum(dza, axis=0, keepdims=True)
        db_ref[...] += db

    blk = lambda h, i: nb - 1 - i
    return _call(
        body, name="gla_bwd", grid=(GLA_HEADS, nb),
        in_specs=[pl.BlockSpec((r, 128), lambda h, i: (blk(h, i), GLA_Q_BLK + h)),
                  pl.BlockSpec((r, 128), lambda h, i: (blk(h, i), GLA_K_BLK + h)),
                  pl.BlockSpec((r, 256), lambda h, i: (blk(h, i), GLA_V_BLK + h)),
                  pl.BlockSpec((r, W_SMALL), lambda h, i: (blk(h, i), 0)),
                  pl.BlockSpec((W_SMALL, 128), lambda h, i: (0, h)),
                  pl.BlockSpec((1, 128), lambda h, i: (0, h)),
                  pl.BlockSpec((r, 256), lambda h, i: (blk(h, i), h)),
                  pl.BlockSpec((nc, None, GLA_DV, GLA_DK), lambda h, i: (blk(h, i), h, 0, 0)),
                  pl.BlockSpec((1, None, GLA_DV, GLA_DK),
                               lambda h, i: (jnp.maximum(blk(h, i) * nc - 1, 0), h, 0, 0))],
        out_specs=[pl.BlockSpec((r, 128), lambda h, i: (blk(h, i), h)),
                   pl.BlockSpec((r, 128), lambda h, i: (blk(h, i), h)),
                   pl.BlockSpec((r, 256), lambda h, i: (blk(h, i), h)),
                   pl.BlockSpec((r, 128), lambda h, i: (blk(h, i), h)),
                   pl.BlockSpec((1, 128), lambda h, i: (0, h))],
        out_shape=[jax.ShapeDtypeStruct((s, 512), BF16), jax.ShapeDtypeStruct((s, 512), BF16),
                   jax.ShapeDtypeStruct((s, 1024), BF16), jax.ShapeDtypeStruct((s, 512), BF16),
                   jax.ShapeDtypeStruct((1, 512), F32)],
        scratch_shapes=[pltpu.VMEM((GLA_DV, GLA_DK), F32)],
        compiler_params=_params(("parallel", "arbitrary")),
    )(pm, pm, pm, small, wa_pad, b_a2, do, states, states)


def _modulation(c_all, w_ada):
    n = w_ada.shape[1]
    tn = _tile(n, 512)

    def body(c_ref, w_ref, out_ref, ca_ref):
        cv = c_ref[...]
        ca = cv * _sigmoid(cv)
        ca_ref[...] = ca
        out_ref[...] = jnp.dot(ca.astype(BF16), w_ref[...].astype(BF16), preferred_element_type=F32)

    return _call(body, name="modulation", grid=(n // tn,),
                 in_specs=[pl.BlockSpec((N_DEV, D_MODEL), lambda j: (0, 0)),
                           pl.BlockSpec((D_MODEL, tn), lambda j: (0, j))],
                 out_specs=[pl.BlockSpec((N_DEV, tn), lambda j: (0, j)),
                            pl.BlockSpec((N_DEV, D_MODEL), lambda j: (0, 0))],
                 out_shape=[jax.ShapeDtypeStruct((N_DEV, n), F32), jax.ShapeDtypeStruct((N_DEV, D_MODEL), F32)],
                 compiler_params=_params(("arbitrary",)))(c_all, w_ada)


def _adamw_math(w, g, m, v):
    m = ADAM_B1 * m + (1.0 - ADAM_B1) * g
    v = ADAM_B2 * v + (1.0 - ADAM_B2) * (g * g)
    m_hat = m / (1.0 - ADAM_B1 ** ADAM_STEP)
    v_hat = v / (1.0 - ADAM_B2 ** ADAM_STEP)
    delta = -ADAM_LR * (m_hat / (jnp.sqrt(v_hat) + ADAM_EPS) + ADAM_WD * w)
    return delta, m, v


def _adamw_slabs(name, w, slabs, m, v, tr=256):
    rr, cc = w.shape

    def body(w_ref, s_ref, m_ref, v_ref, g_ref, d_ref, nm_ref, nv_ref):
        g = s_ref[0].astype(F32)
        for r in range(1, N_DEV):
            g = g + s_ref[r].astype(F32)
        g_ref[...] = g
        d, nm, nv = _adamw_math(w_ref[...], g, m_ref[...], v_ref[...])
        d_ref[...] = d
        nm_ref[...] = nm
        nv_ref[...] = nv

    steps, spec, slab_spec = _plane_tiles(rr, cc, tr)
    return _call(body, name=name, grid=(steps,),
                 in_specs=[spec, slab_spec, spec, spec],
                 out_specs=[spec] * 4, out_shape=[jax.ShapeDtypeStruct((rr, cc), F32)] * 4,
                 compiler_params=_params(("parallel",)))(w, slabs, m, v)


def _adamw_ada(w, cat, dm, m, v, tr=256, deps=()):
    rr, cc = w.shape
    tr = _tile(rr, tr)

    def body(w_ref, ca_ref, dm_ref, m_ref, v_ref, g_ref, d_ref, nm_ref, nv_ref):
        g = ca_ref[:, 0:1] * dm_ref[0:1, :]
        for b in range(1, N_DEV):
            g = g + ca_ref[:, b:b + 1] * dm_ref[b:b + 1, :]
        g_ref[...] = g
        d, nm, nv = _adamw_math(w_ref[...], g, m_ref[...], v_ref[...])
        d_ref[...] = d
        nm_ref[...] = nm
        nv_ref[...] = nv

    spec = pl.BlockSpec((tr, cc), lambda i: (i, 0))
    return _call(body, deps=deps, name="adamw_ada", grid=(rr // tr,),
                 in_specs=[spec, pl.BlockSpec((tr, N_DEV), lambda i: (i, 0)),
                           pl.BlockSpec((N_DEV, cc), lambda i: (0, 0)), spec, spec],
                 out_specs=[spec] * 4, out_shape=[jax.ShapeDtypeStruct((rr, cc), F32)] * 4,
                 compiler_params=_params(("parallel",)))(w, cat, dm, m, v)


def _sum_devices(gathered):
    ln = gathered.shape[-1]

    def body(g_ref, out_ref):
        acc = g_ref[0]
        for r in range(1, N_DEV):
            acc = acc + g_ref[r]
        out_ref[...] = acc

    return _call(body, name="sum_devices",
                 in_specs=[pl.BlockSpec(memory_space=pltpu.VMEM)], out_specs=pl.BlockSpec(memory_space=pltpu.VMEM),
                 out_shape=jax.ShapeDtypeStruct((1, ln), F32))(gathered)


def _adamw_flat(w, g, m, v):
    def body(w_ref, g_ref, m_ref, v_ref, d_ref, nm_ref, nv_ref):
        d, nm, nv = _adamw_math(w_ref[...], g_ref[...], m_ref[...], v_ref[...])
        d_ref[...] = d
        nm_ref[...] = nm
        nv_ref[...] = nv

    vm = pl.BlockSpec(memory_space=pltpu.VMEM)
    return _call(body, name="adamw_small", in_specs=[vm] * 4, out_specs=[vm] * 3,
                 out_shape=[jax.ShapeDtypeStruct(w.shape, F32)] * 3)(w, g, m, v)


def _from_col_shards(g):
    return jnp.transpose(g, (1, 0, 2)).reshape(g.shape[1], N_DEV * g.shape[2])


def _pad_lanes(v, n):
    return jnp.concatenate([v, jnp.zeros(v.shape[:-1] + (n - v.shape[-1],), v.dtype)], axis=-1)


def kernel(x, c, w_ada, b_ada, g_pre_mix, g_post_mix, w_in, b_fgate, w_gla_a2, b_gla_a2, g_fox_out, g_gla_out, w_out, g_pre_mlp, g_post_mlp, w_mlp_in, w_mlp_out, loss_target, m_w_ada, m_b_ada, m_g_pre_mix, m_g_post_mix, m_w_in, m_b_fgate, m_w_gla_a2, m_b_gla_a2, m_g_fox_out, m_g_gla_out, m_w_out, m_g_pre_mlp, m_g_post_mlp, m_w_mlp_in, m_w_mlp_out, v_w_ada, v_b_ada, v_g_pre_mix, v_g_post_mix, v_w_in, v_b_fgate, v_w_gla_a2, v_b_gla_a2, v_g_fox_out, v_g_gla_out, v_w_out, v_g_pre_mlp, v_g_post_mlp, v_w_mlp_in, v_w_mlp_out):
    rank = _my_rank()
    xs = x[0]
    s = xs.shape[0]
    target = loss_target[0]

    w_in_t, m_in_t, v_in_t = w_in[0].T, m_w_in[0].T, v_w_in[0].T
    c_all, wa2_g, ggla_g, win_g = _all_gather("gather_first", [c, w_gla_a2[0], g_gla_out[0], w_in_t.astype(BF16)])
    rest = [_own_slot("own_w_out", w_out[0], True, rank), _own_slot("own_w_mlp_in", w_mlp_in[0], True, rank)]
    gs_send, gs_sib, gs_ici, gs_land, gs_token = _gather2_start("gather_rest_start", rest, after=(c_all,))
    last = [_own_slot("own_w_mlp_out", w_mlp_out[0], True, rank)]
    gl_send, gl_sib, gl_ici, gl_land, gl_token = _gather2_start("gather_last_start", last, after=(gs_token,))
    w_a2 = _from_col_shards(wa2_g)
    g_gla = _from_col_shards(ggla_g).reshape(1, 1024)
    g_fox = g_fox_out.reshape(1, 1024)
    win_full = win_g.reshape(N_DEV * 771, D_MODEL)
    w_main = jnp.concatenate([win_full[:3072], win_full[3080:5128], win_full[5144:6168]], axis=0)
    w_small = jnp.concatenate([win_full[3072:3080], win_full[5128:5144],
                               jnp.zeros((W_SMALL - 24, D_MODEL), BF16)], axis=0)
    wa_pad =jnp.concatenate([jnp.zeros((8, 512), BF16), w_a2.astype(BF16), jnp.zeros((104, 512), BF16)], axis=0)
    bf_vec = _pad_lanes(b_fgate, W_SMALL)

    mod_part, c_act = _modulation(c_all.reshape(N_DEV, D_MODEL), w_ada[0])
    (mod_g,) = _all_gather("gather_mod", [mod_part])
    mod = lax.dynamic_slice_in_dim(mod_g, rank, 1, axis=1).reshape(1, 6 * D_MODEL) + b_ada
    shift_m, scale_m, gate_m, shift_f, scale_f, gate_f = [mod[:, i * D_MODEL:(i + 1) * D_MODEL] for i in range(6)]

    h = _premix(xs, g_pre_mix, scale_m, shift_m, deps=(gl_token,))
    pm = _mm_plain("proj_main", h, w_main, NT, BF16)
    small = _mm_plain("proj_small", h, w_small, NT, F32)
    crow = _fox_cum(small, bf_vec).reshape(FOX_HEADS, 1, s)
    o_fox, lse = _fox_fwd(pm, crow)
    gs_fsend, gs_frecv, gs_land, gs_ftoken = _gather2_forward("gather_rest_forward", gs_land, gs_ici, o_fox)
    o_gla, states = _gla_fwd(pm, small, wa_pad, b_gla_a2, deps=(gs_ftoken,))
    mix = _mix_fwd(o_fox, o_gla, pm, g_fox, g_gla)
    wout_g, wmi_g = _gather2_wait("gather_rest_wait", gs_land, gs_send, gs_sib, gs_fsend, gs_frecv, mix)
    w_out_full = wout_g.reshape(D_MODEL, D_MODEL)
    y = _mm_plain("out_proj", mix, w_out_full, NN, BF16)
    x1, h2 = _postmix_premlp(xs, y, gate_m, g_post_mix, g_pre_mlp, scale_f, shift_f)
    gl_fsend, gl_frecv, gl_land, gl_ftoken = _gather2_forward("gather_last_forward", gl_land, gl_ici, h2)

    tm, tn, tk = _tile(s, 1024), 1024, 2048
    nsh = 1024 // tn

    def relu2(acc):
        rl = jnp.maximum(acc, 0.0)
        return rl * rl, rl

    z, a_relu = _matmul(
        "mlp_in", h2, wmi_g, contract=NN, grid=(s // tm, D_FF // tn, D_MODEL // tk),
        a_spec=pl.BlockSpec((tm, tk), lambda i, j, k: (i, k)),
        b_spec=pl.BlockSpec((None, tk, tn), lambda i, j, k: (j // nsh, k, j % nsh)),
        out_specs=[pl.BlockSpec((tm, tn), lambda i, j, k: (i, j))] * 2,
        out_shapes=[jax.ShapeDtypeStruct((s, D_FF), BF16)] * 2, acc_shape=(tm, tn), epilogue=relu2,
        deps=(gl_ftoken,))
    (wmo_g,) = _gather2_wait("gather_last_wait", gl_land, gl_send, gl_sib, gl_fsend, gl_frecv, z)
    w_mo_full = wmo_g.reshape(D_FF, D_MODEL)
    y2 = _mm_plain("mlp_out", z, w_mo_full, NN, BF16)

    dx2, dy2, loss_vec, dgate_f, dg_post_mlp = _loss_postmlp_bwd(x1, y2, target, gate_f, g_post_mlp)
    loss = lax.psum(loss_vec[0, 0], ("x", "y", "c"))

    da = _mm_plain("mlp_out_dx", dy2, w_mo_full, NT, BF16, extra=(a_relu,),
                   epilogue=lambda acc, rl: (acc * (2.0 * rl.astype(F32)),))
    dw_mo = _mm_plain("mlp_out_dw", z, dy2, TN, BF16)
    dw_mo = dw_mo.reshape(N_DEV, 1024, D_MODEL)
    x_mo = _exchange_start("grad_mlp_out_start", [_own_slot("own_dw_mlp_out", dw_mo, False, rank)], [dw_mo])
    tkx = 2048
    (dh2,) = _matmul(
        "mlp_in_dx", da, wmi_g, contract=NT, grid=(s // tm, D_MODEL // tn, D_FF // tkx),
        a_spec=pl.BlockSpec((tm, tkx), lambda i, j, k: (i, k)),
        b_spec=pl.BlockSpec((tkx // 1024, tn, 1024), lambda i, j, k: (k, j, 0)),
        out_specs=[pl.BlockSpec((tm, tn), lambda i, j, k: (i, j))],
        out_shapes=[jax.ShapeDtypeStruct((s, D_MODEL), BF16)], acc_shape=(tm, tn), deps=(x_mo[4],))
    ts = _tile(s, 2048)
    (dw_mi,) = _matmul(
        "mlp_in_dw", h2, da, contract=TN, grid=(D_MODEL // 1024, D_FF // tn, s // ts),
        a_spec=pl.BlockSpec((ts, 1024), lambda i, j, k: (k, i)),
        b_spec=pl.BlockSpec((ts, tn), lambda i, j, k: (k, j)),
        out_specs=[pl.BlockSpec((None, 1024, tn), lambda i, j, k: (j // nsh, i, j % nsh))],
        out_shapes=[jax.ShapeDtypeStruct((N_DEV, D_MODEL, 1024), BF16)], acc_shape=(1024, tn))
    x_mi = _exchange_start("grad_mlp_in_start", [_own_slot("own_dw_mlp_in", dw_mi, False, rank)], [dw_mi])

    dx1, dy, dscale_f, dshift_f, dg_pre_mlp, dgate_m, dg_post_mix = _premlp_postmix_bwd(
        dh2, dx2, x1, y, scale_f, g_pre_mlp, gate_m, g_post_mix, deps=(x_mi[4],))

    dmix = _mm_plain("out_proj_dx", dy, w_out_full, NT, BF16)
    dw_out = _mm_plain("out_proj_dw", mix, dy, TN, BF16)
    dw_out = dw_out.reshape(N_DEV, 256, D_MODEL)
    x_out = _exchange_start("grad_out_start", [_own_slot("own_dw_out", dw_out, False, rank)], [dw_out])
    do_fox, do_gla, dgr, dg_fox, dg_gla = _mix_bwd(dmix, o_fox, o_gla, pm, g_fox, g_gla, deps=(x_out[4],))

    dq, dk, dv, dc, dcq = _fox_bwd(pm, crow, o_fox, lse, do_fox)
    dsmall_f, db_f = _fox_cum_bwd(dc.reshape(FOX_HEADS, s), dcq, small, bf_vec)
    dgq, dgk, dgv, dza, db_a2 = _gla_bwd(pm, small, wa_pad, b_gla_a2, states, do_gla)
    dsmall = _mm_plain("gate_dx", dza, wa_pad, NT, F32, tn=128, extra=(dsmall_f,),
                       epilogue=lambda acc, other: (acc + other,))
    dwa_pad = _mm_plain("gate_dw", small, dza, TN, F32, tm=128, tn=512)

    dpm = jnp.concatenate([dq.astype(BF16), dk.astype(BF16), dv.astype(BF16), dgq.astype(BF16), dgk.astype(BF16),
                           dgv.astype(BF16), dgr], axis=1)
    dw_main = _mm_plain("proj_main_dw", dpm, h, TN, BF16)
    dw_small = _mm_plain("proj_small_dw", dsmall, h, TN, BF16, tm=128)
    dwin_full = jnp.concatenate([dw_main[:3072], dw_small[0:8], dw_main[3072:5120], dw_small[8:24],
                                 dw_main[5120:6144]], axis=0)
    dwin_slabs = dwin_full.reshape(N_DEV, 771, D_MODEL)
    x_in = _exchange_start("grad_in_start", [_own_slot("own_dw_in", dwin_slabs, False, rank)], [dwin_slabs])
    tmx = _tile(s, 1024)
    (dh,) = _matmul(
        "proj_main_dx", dpm, w_main, contract=NN, grid=(s // tmx, D_MODEL // 1024, W_MAIN // 2048),
        a_spec=pl.BlockSpec((tmx, 2048), lambda i, j, k: (i, k)),
        b_spec=pl.BlockSpec((2048, 1024), lambda i, j, k: (k, j)),
        out_specs=[pl.BlockSpec((tmx, 1024), lambda i, j, k: (i, j))],
        out_shapes=[jax.ShapeDtypeStruct((s, D_MODEL), BF16)], acc_shape=(tmx, 1024),
        extra=(dsmall, w_small),
        extra_specs=[pl.BlockSpec((tmx, W_SMALL), lambda i, j, k: (i, 0)),
                     pl.BlockSpec((W_SMALL, 1024), lambda i, j, k: (0, j))],
        epilogue=lambda acc, dsm, wsm: (acc + jnp.dot(dsm.astype(BF16), wsm, preferred_element_type=F32),),
        deps=(x_in[4],))
    grad_x, dscale_m, dshift_m, dg_pre_mix = _premix_bwd(dh, dx1, xs, g_pre_mix, scale_m)

    dmod = jnp.concatenate([dshift_m, dscale_m, dgate_m, dshift_f, dscale_f, dgate_f], axis=1)
    flat = jnp.concatenate(
        [dmod, dg_pre_mix, dg_post_mix, dg_fox, dg_pre_mlp, dg_post_mlp, db_a2,
         dwa_pad[8:24, :].reshape(1, GLA_RANK * 512), dg_gla, _pad_lanes(db_f[:, 0].reshape(1, FOX_HEADS), 128)],
        axis=1)

    (r_mo,) = _exchange_wait("grad_mlp_out_wait", *x_mo[:4], grad_x)
    g_mo, d_mo, nm_mo, nv_mo = _adamw_slabs("adamw_w_mlp_out", w_mlp_out[0], r_mo, m_w_mlp_out[0], v_w_mlp_out[0])
    (r_mi,) = _exchange_wait("grad_mlp_in_wait", *x_mi[:4], g_mo)
    g_mi, d_mi, nm_mi, nv_mi = _adamw_slabs("adamw_w_mlp_in", w_mlp_in[0], r_mi, m_w_mlp_in[0], v_w_mlp_in[0])
    (r_out,) = _exchange_wait("grad_out_wait", *x_out[:4], g_mi)
    g_out, d_out, nm_out, nv_out = _adamw_slabs("adamw_w_out", w_out[0], r_out, m_w_out[0], v_w_out[0])

    (flat_g,) = _all_gather("gather_small_grads", [flat], deps=(g_out,))
    tot = _sum_devices(flat_g)
    dm_cols = lax.dynamic_slice_in_dim(flat_g[:, 0, :6 * D_MODEL], rank * 1536, 1536, axis=1)
    (r_in,) = _exchange_wait("grad_in_wait", *x_in[:4], tot)
    in_t = _adamw_slabs("adamw_w_in", w_in_t, r_in, m_in_t, v_in_t)
    g_in, d_in, nm_in, nv_in = [a.T for a in in_t]
    g_ada, d_ada, nm_ada, nv_ada = _adamw_ada(w_ada[0], c_act.T, dm_cols, m_w_ada[0], v_w_ada[0], deps=(in_t[0],))

    o = 0
    seg = {}
    for name, n in (("b_ada", 12288), ("g_pre_mix", 2048), ("g_post_mix", 2048), ("g_fox_out", 1024),
                    ("g_pre_mlp", 2048), ("g_post_mlp", 2048), ("b_gla_a2", 512), ("w_gla_a2", 8192),
                    ("g_gla_out", 1024), ("b_fgate", 128)):
        seg[name] = tot[:, o:o + n]
        o += n
    g_wa2 = lax.dynamic_slice_in_dim(seg["w_gla_a2"].reshape(GLA_RANK, 512), rank * 64, 64, axis=1)
    g_ggla = lax.dynamic_slice_in_dim(seg["g_gla_out"].reshape(GLA_HEADS, GLA_DV), rank * 32, 32, axis=1)
    small_names = ["b_ada", "g_pre_mix", "g_post_mix", "g_fox_out", "g_pre_mlp", "g_post_mlp", "b_gla_a2",
                   "w_gla_a2", "g_gla_out", "b_fgate"]
    small_grads = {**seg, "w_gla_a2": g_wa2.reshape(1, 1024), "g_gla_out": g_ggla.reshape(1, 128)}
    weights = dict(b_ada=b_ada, g_pre_mix=g_pre_mix, g_post_mix=g_post_mix, g_fox_out=g_fox_out,
                   g_pre_mlp=g_pre_mlp, g_post_mlp=g_post_mlp, b_gla_a2=b_gla_a2, w_gla_a2=w_gla_a2,
                   g_gla_out=g_gla_out, b_fgate=b_fgate)
    moms = dict(b_ada=m_b_ada, g_pre_mix=m_g_pre_mix, g_post_mix=m_g_post_mix, g_fox_out=m_g_fox_out,
                g_pre_mlp=m_g_pre_mlp, g_post_mlp=m_g_post_mlp, b_gla_a2=m_b_gla_a2, w_gla_a2=m_w_gla_a2,
                g_gla_out=m_g_gla_out, b_fgate=m_b_fgate)
    vels = dict(b_ada=v_b_ada, g_pre_mix=v_g_pre_mix, g_post_mix=v_g_post_mix, g_fox_out=v_g_fox_out,
                g_pre_mlp=v_g_pre_mlp, g_post_mlp=v_g_post_mlp, b_gla_a2=v_b_gla_a2, w_gla_a2=v_w_gla_a2,
                g_gla_out=v_g_gla_out, b_fgate=v_b_fgate)

    def flatten(d, fill):
        parts = []
        for nm in small_names:
            p = d[nm].reshape(1, -1)
            if nm == "b_fgate":
                p = jnp.concatenate([p[:, :FOX_HEADS], jnp.full((1, 128 - FOX_HEADS), fill, F32)], axis=1)
            parts.append(p)
        return jnp.concatenate(parts, axis=1).reshape(-1, 128)

    fw, fg, fm, fv = flatten(weights, 0.0), flatten(small_grads, 0.0), flatten(moms, 0.0), flatten(vels, 1.0)
    fd, fnm, fnv = _adamw_flat(fw, fg, fm, fv)

    def unflatten(fl):
        fl = fl.reshape(1, -1)
        out = {}
        o = 0
        for nm in small_names:
            n = 128 if nm == "b_fgate" else weights[nm].size
            piece = fl[:, o:o + n]
            if nm == "b_fgate":
                piece = piece[:, :FOX_HEADS]
            out[nm] = piece.reshape(weights[nm].shape)
            o += n
        return out

    sg, sd, snm, snv = unflatten(fg), unflatten(fd), unflatten(fnm), unflatten(fnv)

    big = dict(w_ada=(g_ada, d_ada, nm_ada, nv_ada), w_in=(g_in, d_in, nm_in, nv_in),
               w_out=(g_out, d_out, nm_out, nv_out), w_mlp_in=(g_mi, d_mi, nm_mi, nv_mi),
               w_mlp_out=(g_mo, d_mo, nm_mo, nv_mo))
    order = ["w_ada", "b_ada", "g_pre_mix", "g_post_mix", "w_in", "b_fgate", "w_gla_a2", "b_gla_a2", "g_fox_out",
             "g_gla_out", "w_out", "g_pre_mlp", "g_post_mlp", "w_mlp_in", "w_mlp_out"]

    def pick(nm, idx):
        if nm in big:
            return big[nm][idx][None]
        return (sg, sd, snm, snv)[idx][nm]

    grads = [pick(nm, 0) for nm in order]
    deltas = [pick(nm, 1) for nm in order]
    new_m = [pick(nm, 2) for nm in order]
    new_v = [pick(nm, 3) for nm in order]
    return (loss, grad_x[None], *grads, *deltas, *new_m, *new_v)
```
